```python
import math
import jax, jax.numpy as jnp
from jax import lax
import numpy as np

D_MODEL = 2048
BATCH = 8
SEQ = 2048
DEPTH = 2

EPS = 1e-6
NEG_INF = -1e30

SSM_WIDTH = D_MODEL // 2
SSM_GROUP = 16
SSM_GROUPS = SSM_WIDTH // SSM_GROUP
SSM_STATE = 64
DT_MIN = 1e-3
DT_MAX = 1e-1

SG_WIDTH = D_MODEL // 2
SG_HEADS = 8
SG_HEAD_DIM = SG_WIDTH // SG_HEADS
SG_CHUNK = 128

HEAD_DIM = 64
ATT_HEADS = D_MODEL // 128
ATT_KV_HEADS = ATT_HEADS // 8
GQA_GROUP = ATT_HEADS // ATT_KV_HEADS
ATT_WIDTH = ATT_HEADS * HEAD_DIM
KV_WIDTH = ATT_KV_HEADS * HEAD_DIM
WINDOW = 128
ATT_BLOCK = 128
ROT_DIM = HEAD_DIM // 4
ROPE_THETA = 500000.0

N_BRANCH = 3
IN_SIZES = (SSM_WIDTH, SSM_WIDTH, SG_WIDTH, SG_WIDTH, SG_WIDTH,
            ATT_WIDTH, KV_WIDTH, KV_WIDTH, ATT_WIDTH, N_BRANCH * D_MODEL)
D_IN = sum(IN_SIZES)
IN_OFFSETS = tuple(sum(IN_SIZES[:i + 1]) for i in range(len(IN_SIZES) - 1))

kernel_name = "hybrid_s5_gmlp_swa_gated_block"


def rmsnorm(x, w):
    xf = x.astype(jnp.float32)
    y = xf * lax.rsqrt(jnp.mean(xf * xf, axis=-1, keepdims=True) + EPS)
    return (y * w.astype(jnp.float32)).astype(x.dtype)


def layernorm(x, w, b):
    xf = x.astype(jnp.float32)
    mu = jnp.mean(xf, axis=-1, keepdims=True)
    var = jnp.mean(jnp.square(xf - mu), axis=-1, keepdims=True)
    y = (xf - mu) * lax.rsqrt(var + EPS)
    return (y * w.astype(jnp.float32) + b.astype(jnp.float32)).astype(x.dtype)


def partial_rope(t, pos):
    tf = t.astype(jnp.float32)
    half = ROT_DIM // 2
    inv_freq = ROPE_THETA ** (-jnp.arange(0, ROT_DIM, 2, dtype=jnp.float32) / ROT_DIM)
    ang = pos.astype(jnp.float32)[:, None] * inv_freq[None, :]
    cos = jnp.cos(ang)[None, :, None, :]
    sin = jnp.sin(ang)[None, :, None, :]
    t1 = tf[..., :half]
    t2 = tf[..., half:ROT_DIM]
    rot = jnp.concatenate([t1 * cos - t2 * sin, t2 * cos + t1 * sin, tf[..., ROT_DIM:]], axis=-1)
    return rot.astype(t.dtype)


def s5_mixer(u, a_re, a_im, log_dt, b_re, b_im, c_re, c_im, d, glu_w, glu_b):
    bsz, L, _ = u.shape
    uf = u.astype(jnp.float32).reshape(bsz, L, SSM_GROUPS, SSM_GROUP)
    lam = lax.complex(a_re.astype(jnp.float32), a_im.astype(jnp.float32))
    dt = jnp.exp(log_dt.astype(jnp.float32))[:, None]
    lam_bar = jnp.exp(lam * dt)
    b = lax.complex(b_re.astype(jnp.float32), b_im.astype(jnp.float32))
    b_bar = ((lam_bar - 1.0) / lam)[..., None] * b
    bu = jnp.einsum('blgc,gpc->blgp', uf.astype(jnp.complex64), b_bar)
    a = jnp.broadcast_to(lam_bar, bu.shape)

    def combine(e1, e2):
        a1, b1 = e1
        a2, b2 = e2
        return a1 * a2, a2 * b1 + b2

    _, states = lax.associative_scan(combine, (a, bu), axis=1)
    c = lax.complex(c_re.astype(jnp.float32), c_im.astype(jnp.float32))
    y = jnp.real(jnp.einsum('blgp,gcp->blgc', states, c))
    y = y + d.astype(jnp.float32).reshape(SSM_GROUPS, SSM_GROUP) * uf
    y = jax.nn.gelu(y.reshape(bsz, L, SSM_WIDTH)).astype(u.dtype)
    return y * jax.nn.sigmoid(y @ glu_w + glu_b)


def spatial_gating(u, v, ln_w, ln_b, w_s, b_s):
    bsz, L, _ = u.shape
    n = L // SG_CHUNK
    v = layernorm(v, ln_w, ln_b)
    vc = v.reshape(bsz, n, SG_CHUNK, SG_HEADS, SG_HEAD_DIM)
    causal = jnp.tril(jnp.ones((SG_CHUNK, SG_CHUNK), dtype=bool))
    w = jnp.where(causal[None], w_s, jnp.zeros_like(w_s))
    mixed = jnp.einsum('hts,bnshc->bnthc', w, vc) + b_s.T[:, :, None]
    return u * mixed.reshape(bsz, L, SG_WIDTH)


def sliding_window_attention(q, k, v, sinks):
    bsz, L, _ = q.shape
    n = L // ATT_BLOCK
    pos = jnp.arange(L)
    q = partial_rope(q.reshape(bsz, L, ATT_HEADS, HEAD_DIM), pos)
    k = partial_rope(k.reshape(bsz, L, ATT_KV_HEADS, HEAD_DIM), pos)
    v = v.reshape(bsz, L, ATT_KV_HEADS, HEAD_DIM)

    def banded(t):
        tp = jnp.pad(t, ((0, 0), (ATT_BLOCK, 0), (0, 0), (0, 0)))
        prev = tp[:, :L].reshape(bsz, n, ATT_BLOCK, ATT_KV_HEADS, HEAD_DIM)
        cur = t.reshape(bsz, n, ATT_BLOCK, ATT_KV_HEADS, HEAD_DIM)
        return jnp.concatenate([prev, cur], axis=2)

    kb = banded(k)
    vb = banded(v)
    qb = q.reshape(bsz, n, ATT_BLOCK, ATT_KV_HEADS, GQA_GROUP, HEAD_DIM)
    s = jnp.einsum('bnqkgd,bnskd->bnkgqs', qb, kb).astype(jnp.float32) * (HEAD_DIM ** -0.5)
    blk = jnp.arange(n)[:, None, None]
    qpos = blk * ATT_BLOCK + jnp.arange(ATT_BLOCK)[None, :, None]
    kpos = (blk - 1) * ATT_BLOCK + jnp.arange(2 * ATT_BLOCK)[None, None, :]
    diff = qpos - kpos
    allowed = (diff >= 0) & (diff < WINDOW) & (kpos >= 0)
    s = jnp.where(allowed[None, :, None, None], s, NEG_INF)
    sink = sinks.astype(jnp.float32).reshape(ATT_KV_HEADS, GQA_GROUP)[None, None, :, :, None, None]
    sink = jnp.broadcast_to(sink, s.shape[:-1] + (1,))
    p = jax.nn.softmax(jnp.concatenate([s, sink], axis=-1), axis=-1)[..., :-1]
    o = jnp.einsum('bnkgqs,bnskd->bnqkgd', p.astype(vb.dtype), vb)
    return o.reshape(bsz, L, ATT_WIDTH)


def _fwd_setup_inputs(seed: int = 0) -> dict:
    key = jax.random.key(seed)
    ks = jax.random.split(key, 24)
    f32 = jnp.float32
    nrm = lambda k, shape, scale: jax.random.normal(k, shape, f32) * scale
    G, P, C = SSM_GROUPS, SSM_STATE, SSM_GROUP
    x = jax.random.normal(ks[0], (BATCH, SEQ, D_MODEL), f32)
    norm_w = 1.0 + nrm(ks[1], (DEPTH, D_MODEL), 0.02)
    w_in = nrm(ks[2], (DEPTH, D_MODEL, D_IN), D_MODEL ** -0.5)
    ssm_a_re = -0.5 + nrm(ks[3], (DEPTH, G, P), 0.01)
    ssm_a_im = math.pi * jnp.arange(P, dtype=f32)[None, None, :] + nrm(ks[4], (DEPTH, G, P), 0.01)
    ssm_log_dt = jax.random.uniform(ks[5], (DEPTH, G), f32, math.log(DT_MIN), math.log(DT_MAX))
    ssm_b_re = nrm(ks[6], (DEPTH, G, P, C), C ** -0.5)
    ssm_b_im = nrm(ks[7], (DEPTH, G, P, C), C ** -0.5)
    ssm_c_re = nrm(ks[8], (DEPTH, G, C, P), P ** -0.5)
    ssm_c_im = nrm(ks[9], (DEPTH, G, C, P), P ** -0.5)
    ssm_d = nrm(ks[10], (DEPTH, SSM_WIDTH), 1.0)
    ssm_glu_w = nrm(ks[11], (DEPTH, SSM_WIDTH, SSM_WIDTH), SSM_WIDTH ** -0.5)
    ssm_glu_b = nrm(ks[12], (DEPTH, SSM_WIDTH), 0.01)
    sg_ln_w = 1.0 + nrm(ks[13], (DEPTH, SG_WIDTH), 0.02)
    sg_ln_b = nrm(ks[14], (DEPTH, SG_WIDTH), 0.02)
    sg_w = nrm(ks[15], (DEPTH, SG_HEADS, SG_CHUNK, SG_CHUNK), SG_CHUNK ** -0.5)
    sg_b = 1.0 + nrm(ks[16], (DEPTH, SG_HEADS, SG_CHUNK), 0.02)
    attn_sinks = nrm(ks[17], (DEPTH, ATT_HEADS), 1.0)
    w_branch_a = nrm(ks[18], (DEPTH, SSM_WIDTH, D_MODEL), SSM_WIDTH ** -0.5)
    w_branch_b = nrm(ks[19], (DEPTH, SG_WIDTH, D_MODEL), SG_WIDTH ** -0.5)
    w_branch_c = nrm(ks[20], (DEPTH, ATT_WIDTH, D_MODEL), ATT_WIDTH ** -0.5)
    w_out = nrm(ks[21], (DEPTH, D_MODEL, D_MODEL), D_MODEL ** -0.5)
    final_norm_w = 1.0 + nrm(ks[22], (D_MODEL,), 0.02)
    return {"x": x, "norm_w": norm_w, "w_in": w_in,
            "ssm_a_re": ssm_a_re, "ssm_a_im": ssm_a_im, "ssm_log_dt": ssm_log_dt,
            "ssm_b_re": ssm_b_re, "ssm_b_im": ssm_b_im, "ssm_c_re": ssm_c_re, "ssm_c_im": ssm_c_im,
            "ssm_d": ssm_d, "ssm_glu_w": ssm_glu_w, "ssm_glu_b": ssm_glu_b,
            "sg_ln_w": sg_ln_w, "sg_ln_b": sg_ln_b, "sg_w": sg_w, "sg_b": sg_b,
            "attn_sinks": attn_sinks,
            "w_branch_a": w_branch_a, "w_branch_b": w_branch_b, "w_branch_c": w_branch_c,
            "w_out": w_out, "final_norm_w": final_norm_w}


def _fwd_reference(x, norm_w, w_in, ssm_a_re, ssm_a_im, ssm_log_dt, ssm_b_re, ssm_b_im,
              ssm_c_re, ssm_c_im, ssm_d, ssm_glu_w, ssm_glu_b, sg_ln_w, sg_ln_b, sg_w, sg_b,
              attn_sinks, w_branch_a, w_branch_b, w_branch_c, w_out, final_norm_w):
    bsz, L, _ = x.shape
    for l in range(DEPTH):
        h = rmsnorm(x, norm_w[l])
        proj = h @ w_in[l]
        u_a, z_a, u_b, v_b, z_b, q, k, v, z_c, gates = jnp.split(proj, IN_OFFSETS, axis=-1)
        y_a = s5_mixer(u_a, ssm_a_re[l], ssm_a_im[l], ssm_log_dt[l], ssm_b_re[l], ssm_b_im[l],
                       ssm_c_re[l], ssm_c_im[l], ssm_d[l], ssm_glu_w[l], ssm_glu_b[l]) * jax.nn.silu(z_a)
        y_b = spatial_gating(jax.nn.gelu(u_b), jax.nn.gelu(v_b), sg_ln_w[l], sg_ln_b[l],
                             sg_w[l], sg_b[l]) * jax.nn.silu(z_b)
        y_c = sliding_window_attention(q, k, v, attn_sinks[l]) * jax.nn.silu(z_c)
        g = jax.nn.sigmoid(gates.reshape(bsz, L, N_BRANCH, D_MODEL))
        merged = (g[:, :, 0] * (y_a @ w_branch_a[l])
                  + g[:, :, 1] * (y_b @ w_branch_b[l])
                  + g[:, :, 2] * (y_c @ w_branch_c[l]))
        x = x + merged @ w_out[l]
    return rmsnorm(x, final_norm_w)


import jax as _jax
import jax.numpy as _jnp

TWIN_FORMAT = 'train_step'
FWD_PARAMS = ['x', 'norm_w', 'w_in', 'ssm_a_re', 'ssm_a_im', 'ssm_log_dt', 'ssm_b_re', 'ssm_b_im', 'ssm_c_re', 'ssm_c_im', 'ssm_d', 'ssm_glu_w', 'ssm_glu_b', 'sg_ln_w', 'sg_ln_b', 'sg_w', 'sg_b', 'attn_sinks', 'w_branch_a', 'w_branch_b', 'w_branch_c', 'w_out', 'final_norm_w']
TWIN_WEIGHTS = ['norm_w', 'w_in', 'ssm_a_re', 'ssm_a_im', 'ssm_log_dt', 'ssm_b_re', 'ssm_b_im', 'ssm_c_re', 'ssm_c_im', 'ssm_d', 'ssm_glu_w', 'ssm_glu_b', 'sg_ln_w', 'sg_ln_b', 'sg_w', 'sg_b', 'attn_sinks', 'w_branch_a', 'w_branch_b', 'w_branch_c', 'w_out', 'final_norm_w']
TWIN_DIFF_INPUT = 'x'
TWIN_INPUTS = ['x', 'norm_w', 'w_in', 'ssm_a_re', 'ssm_a_im', 'ssm_log_dt', 'ssm_b_re', 'ssm_b_im', 'ssm_c_re', 'ssm_c_im', 'ssm_d', 'ssm_glu_w', 'ssm_glu_b', 'sg_ln_w', 'sg_ln_b', 'sg_w', 'sg_b', 'attn_sinks', 'w_branch_a', 'w_branch_b', 'w_branch_c', 'w_out', 'final_norm_w', 'loss_target', 'm_norm_w', 'm_w_in', 'm_ssm_a_re', 'm_ssm_a_im', 'm_ssm_log_dt', 'm_ssm_b_re', 'm_ssm_b_im', 'm_ssm_c_re', 'm_ssm_c_im', 'm_ssm_d', 'm_ssm_glu_w', 'm_ssm_glu_b', 'm_sg_ln_w', 'm_sg_ln_b', 'm_sg_w', 'm_sg_b', 'm_attn_sinks', 'm_w_branch_a', 'm_w_branch_b', 'm_w_branch_c', 'm_w_out', 'm_final_norm_w', 'v_norm_w', 'v_w_in', 'v_ssm_a_re', 'v_ssm_a_im', 'v_ssm_log_dt', 'v_ssm_b_re', 'v_ssm_b_im', 'v_ssm_c_re', 'v_ssm_c_im', 'v_ssm_d', 'v_ssm_glu_w', 'v_ssm_glu_b', 'v_sg_ln_w', 'v_sg_ln_b', 'v_sg_w', 'v_sg_b', 'v_attn_sinks', 'v_w_branch_a', 'v_w_branch_b', 'v_w_branch_c', 'v_w_out', 'v_final_norm_w']
TWIN_OUTPUTS = ['loss', 'grad_x', 'grad_norm_w', 'grad_w_in', 'grad_ssm_a_re', 'grad_ssm_a_im', 'grad_ssm_log_dt', 'grad_ssm_b_re', 'grad_ssm_b_im', 'grad_ssm_c_re', 'grad_ssm_c_im', 'grad_ssm_d', 'grad_ssm_glu_w', 'grad_ssm_glu_b', 'grad_sg_ln_w', 'grad_sg_ln_b', 'grad_sg_w', 'grad_sg_b', 'grad_attn_sinks', 'grad_w_branch_a', 'grad_w_branch_b', 'grad_w_branch_c', 'grad_w_out', 'grad_final_norm_w', 'delta_norm_w', 'delta_w_in', 'delta_ssm_a_re', 'delta_ssm_a_im', 'delta_ssm_log_dt', 'delta_ssm_b_re', 'delta_ssm_b_im', 'delta_ssm_c_re', 'delta_ssm_c_im', 'delta_ssm_d', 'delta_ssm_glu_w', 'delta_ssm_glu_b', 'delta_sg_ln_w', 'delta_sg_ln_b', 'delta_sg_w', 'delta_sg_b', 'delta_attn_sinks', 'delta_w_branch_a', 'delta_w_branch_b', 'delta_w_branch_c', 'delta_w_out', 'delta_final_norm_w', 'new_m_norm_w', 'new_m_w_in', 'new_m_ssm_a_re', 'new_m_ssm_a_im', 'new_m_ssm_log_dt', 'new_m_ssm_b_re', 'new_m_ssm_b_im', 'new_m_ssm_c_re', 'new_m_ssm_c_im', 'new_m_ssm_d', 'new_m_ssm_glu_w', 'new_m_ssm_glu_b', 'new_m_sg_ln_w', 'new_m_sg_ln_b', 'new_m_sg_w', 'new_m_sg_b', 'new_m_attn_sinks', 'new_m_w_branch_a', 'new_m_w_branch_b', 'new_m_w_branch_c', 'new_m_w_out', 'new_m_final_norm_w', 'new_v_norm_w', 'new_v_w_in', 'new_v_ssm_a_re', 'new_v_ssm_a_im', 'new_v_ssm_log_dt', 'new_v_ssm_b_re', 'new_v_ssm_b_im', 'new_v_ssm_c_re', 'new_v_ssm_c_im', 'new_v_ssm_d', 'new_v_ssm_glu_w', 'new_v_ssm_glu_b', 'new_v_sg_ln_w', 'new_v_sg_ln_b', 'new_v_sg_w', 'new_v_sg_b', 'new_v_attn_sinks', 'new_v_w_branch_a', 'new_v_w_branch_b', 'new_v_w_branch_c', 'new_v_w_out', 'new_v_final_norm_w']
TWIN_LEAF_KINDS = {'loss': 'loss', 'grad_x': 'grad_x', 'grad_norm_w': 'grad_w', 'grad_w_in': 'grad_w', 'grad_ssm_a_re': 'grad_w', 'grad_ssm_a_im': 'grad_w', 'grad_ssm_log_dt': 'grad_w', 'grad_ssm_b_re': 'grad_w', 'grad_ssm_b_im': 'grad_w', 'grad_ssm_c_re': 'grad_w', 'grad_ssm_c_im': 'grad_w', 'grad_ssm_d': 'grad_w', 'grad_ssm_glu_w': 'grad_w', 'grad_ssm_glu_b': 'grad_w', 'grad_sg_ln_w': 'grad_w', 'grad_sg_ln_b': 'grad_w', 'grad_sg_w': 'grad_w', 'grad_sg_b': 'grad_w', 'grad_attn_sinks': 'grad_w', 'grad_w_branch_a': 'grad_w', 'grad_w_branch_b': 'grad_w', 'grad_w_branch_c': 'grad_w', 'grad_w_out': 'grad_w', 'grad_final_norm_w': 'grad_w', 'delta_norm_w': 'delta_w', 'delta_w_in': 'delta_w', 'delta_ssm_a_re': 'delta_w', 'delta_ssm_a_im': 'delta_w', 'delta_ssm_log_dt': 'delta_w', 'delta_ssm_b_re': 'delta_w', 'delta_ssm_b_im': 'delta_w', 'delta_ssm_c_re': 'delta_w', 'delta_ssm_c_im': 'delta_w', 'delta_ssm_d': 'delta_w', 'delta_ssm_glu_w': 'delta_w', 'delta_ssm_glu_b': 'delta_w', 'delta_sg_ln_w': 'delta_w', 'delta_sg_ln_b': 'delta_w', 'delta_sg_w': 'delta_w', 'delta_sg_b': 'delta_w', 'delta_attn_sinks': 'delta_w', 'delta_w_branch_a': 'delta_w', 'delta_w_branch_b': 'delta_w', 'delta_w_branch_c': 'delta_w', 'delta_w_out': 'delta_w', 'delta_final_norm_w': 'delta_w', 'new_m_norm_w': 'new_m', 'new_m_w_in': 'new_m', 'new_m_ssm_a_re': 'new_m', 'new_m_ssm_a_im': 'new_m', 'new_m_ssm_log_dt': 'new_m', 'new_m_ssm_b_re': 'new_m', 'new_m_ssm_b_im': 'new_m', 'new_m_ssm_c_re': 'new_m', 'new_m_ssm_c_im': 'new_m', 'new_m_ssm_d': 'new_m', 'new_m_ssm_glu_w': 'new_m', 'new_m_ssm_glu_b': 'new_m', 'new_m_sg_ln_w': 'new_m', 'new_m_sg_ln_b': 'new_m', 'new_m_sg_w': 'new_m', 'new_m_sg_b': 'new_m', 'new_m_attn_sinks': 'new_m', 'new_m_w_branch_a': 'new_m', 'new_m_w_branch_b': 'new_m', 'new_m_w_branch_c': 'new_m', 'new_m_w_out': 'new_m', 'new_m_final_norm_w': 'new_m', 'new_v_norm_w': 'new_v', 'new_v_w_in': 'new_v', 'new_v_ssm_a_re': 'new_v', 'new_v_ssm_a_im': 'new_v', 'new_v_ssm_log_dt': 'new_v', 'new_v_ssm_b_re': 'new_v', 'new_v_ssm_b_im': 'new_v', 'new_v_ssm_c_re': 'new_v', 'new_v_ssm_c_im': 'new_v', 'new_v_ssm_d': 'new_v', 'new_v_ssm_glu_w': 'new_v', 'new_v_ssm_glu_b': 'new_v', 'new_v_sg_ln_w': 'new_v', 'new_v_sg_ln_b': 'new_v', 'new_v_sg_w': 'new_v', 'new_v_sg_b': 'new_v', 'new_v_attn_sinks': 'new_v', 'new_v_w_branch_a': 'new_v', 'new_v_w_branch_b': 'new_v', 'new_v_w_branch_c': 'new_v', 'new_v_w_out': 'new_v', 'new_v_final_norm_w': 'new_v'}


def _forward(args):
    return _fwd_reference(*[args[k] for k in FWD_PARAMS])


def _output_shape():
    out = _jax.eval_shape(lambda: _forward(_fwd_setup_inputs(0)))
    return out.shape, out.dtype

N_MICROBATCH = 1
ADAM_LR = 0.001
ADAM_B1 = 0.9
ADAM_B2 = 0.999
ADAM_EPS = 1e-08
ADAM_WD = 0.01
ADAM_STEP = 10
PER_EXAMPLE_BATCH_AXIS = {'x': 0, 'loss_target': 0}
SHARED_INPUTS = []
_WEIGHT_DTYPES = {'norm_w': _jnp.float32, 'w_in': _jnp.float32, 'ssm_a_re': _jnp.float32, 'ssm_a_im': _jnp.float32, 'ssm_log_dt': _jnp.float32, 'ssm_b_re': _jnp.float32, 'ssm_b_im': _jnp.float32, 'ssm_c_re': _jnp.float32, 'ssm_c_im': _jnp.float32, 'ssm_d': _jnp.float32, 'ssm_glu_w': _jnp.float32, 'ssm_glu_b': _jnp.float32, 'sg_ln_w': _jnp.float32, 'sg_ln_b': _jnp.float32, 'sg_w': _jnp.float32, 'sg_b': _jnp.float32, 'attn_sinks': _jnp.float32, 'w_branch_a': _jnp.float32, 'w_branch_b': _jnp.float32, 'w_branch_c': _jnp.float32, 'w_out': _jnp.float32, 'final_norm_w': _jnp.float32}
MOMENT_SCALE = {'norm_w': 2.888048e-02, 'w_in': 1.130601e-02, 'ssm_a_re': 9.747158e-04, 'ssm_a_im': 9.358262e-04, 'ssm_log_dt': 7.841482e-01, 'ssm_b_re': 4.767766e-04, 'ssm_b_im': 4.770842e-04, 'ssm_c_re': 9.507635e-04, 'ssm_c_im': 9.438794e-04, 'ssm_d': 1.044115e-02, 'ssm_glu_w': 2.903141e-03, 'ssm_glu_b': 4.190917e-03, 'sg_ln_w': 1.279126e-02, 'sg_ln_b': 1.262823e-02, 'sg_w': 1.284245e-02, 'sg_b': 1.853532e-02, 'attn_sinks': 5.134384e-03, 'w_branch_a': 6.919063e-03, 'w_branch_b': 1.566312e-02, 'w_branch_c': 4.018165e-03, 'w_out': 1.760810e-02, 'final_norm_w': 7.996441e+00}


def _to_microbatches(a, axis):
    t = _jnp.moveaxis(a, axis, 0)
    t = t.reshape((N_MICROBATCH, t.shape[0] // N_MICROBATCH) + t.shape[1:])
    return _jnp.moveaxis(t, 1, axis + 1)


def setup_inputs(seed: int = 0) -> dict:
    inp = _fwd_setup_inputs(seed)
    key = _jax.random.fold_in(_jax.random.key(seed), 7919)
    shape, _ = _output_shape()
    out = dict(inp)
    out["loss_target"] = _jax.random.normal(_jax.random.fold_in(key, 0), shape, _jnp.float32)
    for i, name in enumerate(TWIN_WEIGHTS):
        w = inp[name].astype(_jnp.float32)
        if MOMENT_SCALE is None:
            s = _jnp.sqrt(_jnp.mean(_jnp.square(w)) + 1e-30)
        else:
            s = MOMENT_SCALE[name]
        km, kv = _jax.random.split(_jax.random.fold_in(key, i + 1))
        out[name] = w
        out["m_" + name] = s * _jax.random.normal(km, w.shape, _jnp.float32)
        out["v_" + name] = (s * s) * _jax.random.uniform(kv, w.shape, _jnp.float32, 0.5, 1.5)
    if N_MICROBATCH > 1:
        for name, axis in PER_EXAMPLE_BATCH_AXIS.items():
            out[name] = _to_microbatches(out[name], axis)
    return {'x': out['x'], 'norm_w': out['norm_w'], 'w_in': out['w_in'], 'ssm_a_re': out['ssm_a_re'], 'ssm_a_im': out['ssm_a_im'], 'ssm_log_dt': out['ssm_log_dt'], 'ssm_b_re': out['ssm_b_re'], 'ssm_b_im': out['ssm_b_im'], 'ssm_c_re': out['ssm_c_re'], 'ssm_c_im': out['ssm_c_im'], 'ssm_d': out['ssm_d'], 'ssm_glu_w': out['ssm_glu_w'], 'ssm_glu_b': out['ssm_glu_b'], 'sg_ln_w': out['sg_ln_w'], 'sg_ln_b': out['sg_ln_b'], 'sg_w': out['sg_w'], 'sg_b': out['sg_b'], 'attn_sinks': out['attn_sinks'], 'w_branch_a': out['w_branch_a'], 'w_branch_b': out['w_branch_b'], 'w_branch_c': out['w_branch_c'], 'w_out': out['w_out'], 'final_norm_w': out['final_norm_w'], 'loss_target': out['loss_target'], 'm_norm_w': out['m_norm_w'], 'm_w_in': out['m_w_in'], 'm_ssm_a_re': out['m_ssm_a_re'], 'm_ssm_a_im': out['m_ssm_a_im'], 'm_ssm_log_dt': out['m_ssm_log_dt'], 'm_ssm_b_re': out['m_ssm_b_re'], 'm_ssm_b_im': out['m_ssm_b_im'], 'm_ssm_c_re': out['m_ssm_c_re'], 'm_ssm_c_im': out['m_ssm_c_im'], 'm_ssm_d': out['m_ssm_d'], 'm_ssm_glu_w': out['m_ssm_glu_w'], 'm_ssm_glu_b': out['m_ssm_glu_b'], 'm_sg_ln_w': out['m_sg_ln_w'], 'm_sg_ln_b': out['m_sg_ln_b'], 'm_sg_w': out['m_sg_w'], 'm_sg_b': out['m_sg_b'], 'm_attn_sinks': out['m_attn_sinks'], 'm_w_branch_a': out['m_w_branch_a'], 'm_w_branch_b': out['m_w_branch_b'], 'm_w_branch_c': out['m_w_branch_c'], 'm_w_out': out['m_w_out'], 'm_final_norm_w': out['m_final_norm_w'], 'v_norm_w': out['v_norm_w'], 'v_w_in': out['v_w_in'], 'v_ssm_a_re': out['v_ssm_a_re'], 'v_ssm_a_im': out['v_ssm_a_im'], 'v_ssm_log_dt': out['v_ssm_log_dt'], 'v_ssm_b_re': out['v_ssm_b_re'], 'v_ssm_b_im': out['v_ssm_b_im'], 'v_ssm_c_re': out['v_ssm_c_re'], 'v_ssm_c_im': out['v_ssm_c_im'], 'v_ssm_d': out['v_ssm_d'], 'v_ssm_glu_w': out['v_ssm_glu_w'], 'v_ssm_glu_b': out['v_ssm_glu_b'], 'v_sg_ln_w': out['v_sg_ln_w'], 'v_sg_ln_b': out['v_sg_ln_b'], 'v_sg_w': out['v_sg_w'], 'v_sg_b': out['v_sg_b'], 'v_attn_sinks': out['v_attn_sinks'], 'v_w_branch_a': out['v_w_branch_a'], 'v_w_branch_b': out['v_w_branch_b'], 'v_w_branch_c': out['v_w_branch_c'], 'v_w_out': out['v_w_out'], 'v_final_norm_w': out['v_final_norm_w']}


def _loss(weights, diff, rest, loss_target):
    with _jax.named_scope("forward"):
        args = {**rest, TWIN_DIFF_INPUT: diff, **{k: w.astype(_WEIGHT_DTYPES[k]) for k, w in weights.items()}}
        y = _forward(args)
    with _jax.named_scope("loss_head"):
        err = _jnp.square(y.astype(_jnp.float32) - loss_target)
        return 0.5 * _jnp.sum(_jnp.mean(err, axis=-1)) if err.ndim else 0.5 * err


def _adamw(w, g, m, v):
    m = ADAM_B1 * m + (1.0 - ADAM_B1) * g
    v = ADAM_B2 * v + (1.0 - ADAM_B2) * _jnp.square(g)
    m_hat = m / (1.0 - ADAM_B1 ** ADAM_STEP)
    v_hat = v / (1.0 - ADAM_B2 ** ADAM_STEP)
    delta = -ADAM_LR * (m_hat / (_jnp.sqrt(v_hat) + ADAM_EPS) + ADAM_WD * w)
    return delta, m, v


def reference(x, norm_w, w_in, ssm_a_re, ssm_a_im, ssm_log_dt, ssm_b_re, ssm_b_im, ssm_c_re, ssm_c_im, ssm_d, ssm_glu_w, ssm_glu_b, sg_ln_w, sg_ln_b, sg_w, sg_b, attn_sinks, w_branch_a, w_branch_b, w_branch_c, w_out, final_norm_w, loss_target, m_norm_w, m_w_in, m_ssm_a_re, m_ssm_a_im, m_ssm_log_dt, m_ssm_b_re, m_ssm_b_im, m_ssm_c_re, m_ssm_c_im, m_ssm_d, m_ssm_glu_w, m_ssm_glu_b, m_sg_ln_w, m_sg_ln_b, m_sg_w, m_sg_b, m_attn_sinks, m_w_branch_a, m_w_branch_b, m_w_branch_c, m_w_out, m_final_norm_w, v_norm_w, v_w_in, v_ssm_a_re, v_ssm_a_im, v_ssm_log_dt, v_ssm_b_re, v_ssm_b_im, v_ssm_c_re, v_ssm_c_im, v_ssm_d, v_ssm_glu_w, v_ssm_glu_b, v_sg_ln_w, v_sg_ln_b, v_sg_w, v_sg_b, v_attn_sinks, v_w_branch_a, v_w_branch_b, v_w_branch_c, v_w_out, v_final_norm_w):
    given = dict(x=x, norm_w=norm_w, w_in=w_in, ssm_a_re=ssm_a_re, ssm_a_im=ssm_a_im, ssm_log_dt=ssm_log_dt, ssm_b_re=ssm_b_re, ssm_b_im=ssm_b_im, ssm_c_re=ssm_c_re, ssm_c_im=ssm_c_im, ssm_d=ssm_d, ssm_glu_w=ssm_glu_w, ssm_glu_b=ssm_glu_b, sg_ln_w=sg_ln_w, sg_ln_b=sg_ln_b, sg_w=sg_w, sg_b=sg_b, attn_sinks=attn_sinks, w_branch_a=w_branch_a, w_branch_b=w_branch_b, w_branch_c=w_branch_c, w_out=w_out, final_norm_w=final_norm_w, loss_target=loss_target, m_norm_w=m_norm_w, m_w_in=m_w_in, m_ssm_a_re=m_ssm_a_re, m_ssm_a_im=m_ssm_a_im, m_ssm_log_dt=m_ssm_log_dt, m_ssm_b_re=m_ssm_b_re, m_ssm_b_im=m_ssm_b_im, m_ssm_c_re=m_ssm_c_re, m_ssm_c_im=m_ssm_c_im, m_ssm_d=m_ssm_d, m_ssm_glu_w=m_ssm_glu_w, m_ssm_glu_b=m_ssm_glu_b, m_sg_ln_w=m_sg_ln_w, m_sg_ln_b=m_sg_ln_b, m_sg_w=m_sg_w, m_sg_b=m_sg_b, m_attn_sinks=m_attn_sinks, m_w_branch_a=m_w_branch_a, m_w_branch_b=m_w_branch_b, m_w_branch_c=m_w_branch_c, m_w_out=m_w_out, m_final_norm_w=m_final_norm_w, v_norm_w=v_norm_w, v_w_in=v_w_in, v_ssm_a_re=v_ssm_a_re, v_ssm_a_im=v_ssm_a_im, v_ssm_log_dt=v_ssm_log_dt, v_ssm_b_re=v_ssm_b_re, v_ssm_b_im=v_ssm_b_im, v_ssm_c_re=v_ssm_c_re, v_ssm_c_im=v_ssm_c_im, v_ssm_d=v_ssm_d, v_ssm_glu_w=v_ssm_glu_w, v_ssm_glu_b=v_ssm_glu_b, v_sg_ln_w=v_sg_ln_w, v_sg_ln_b=v_sg_ln_b, v_sg_w=v_sg_w, v_sg_b=v_sg_b, v_attn_sinks=v_attn_sinks, v_w_branch_a=v_w_branch_a, v_w_branch_b=v_w_branch_b, v_w_branch_c=v_w_branch_c, v_w_out=v_w_out, v_final_norm_w=v_final_norm_w)
    weights = {n: given[n] for n in TWIN_WEIGHTS}
    shared = {n: given[n] for n in SHARED_INPUTS}
    per_example = {n: given[n] for n in ['x']}
    grad_fn = _jax.value_and_grad(_loss, argnums=(0, 1))

    def one_microbatch(ex, loss_target):
        ex = dict(ex)
        diff = ex.pop(TWIN_DIFF_INPUT)
        return grad_fn(weights, diff, {**shared, **ex}, loss_target)

    if N_MICROBATCH == 1:
        loss, (grad_w, grad_x) = one_microbatch(per_example, given["loss_target"])
    else:
        def body(carry, xs):
            loss_sum, grad_sum = carry
            l_k, (gw_k, gx_k) = one_microbatch(xs[0], xs[1])
            with _jax.named_scope("update"):
                return (loss_sum + l_k, _jax.tree.map(_jnp.add, grad_sum, gw_k)), gx_k

        init = (_jnp.zeros((), _jnp.float32), _jax.tree.map(_jnp.zeros_like, weights))
        (loss, grad_w), grad_x = _jax.lax.scan(body, init, (per_example, given["loss_target"]))
    with _jax.named_scope("update"):
        delta_w, new_m, new_v = {}, {}, {}
        for n in TWIN_WEIGHTS:
            delta_w[n], new_m[n], new_v[n] = _adamw(weights[n], grad_w[n], given["m_" + n], given["v_" + n])
    return (loss, grad_x, *[grad_w[n] for n in TWIN_WEIGHTS], *[delta_w[n] for n in TWIN_WEIGHTS],
            *[new_m[n] for n in TWIN_WEIGHTS], *[new_v[n] for n in TWIN_WEIGHTS])
```

```python
import functools
import math

import jax
import jax.numpy as jnp
from jax import lax
from jax.experimental import pallas as pl
from jax.experimental.pallas import tpu as pltpu

F32 = jnp.float32
BF16 = jnp.bfloat16

D_MODEL = 2048
DEPTH = 2
EPS = 1e-6
NEG_INF = -1e30
N_DEV = 8

SSM_WIDTH = 1024
SSM_GROUP = 16
SSM_GROUPS = 64
SSM_STATE = 64
N_SLAB = 8
SLAB_CH = 128
SLAB_ST = 512
N_SEG = 8
SEG_PAD = 8

SG_HEADS = 8
CHUNK = 128
HEAD_DIM = 64
ATT_HEADS = 16
ROT_DIM = 16
ROPE_THETA = 500000.0

D_IN = 13568
OFF_UA, OFF_ZA, OFF_UB, OFF_VB, OFF_ZB, OFF_Q, OFF_KV, OFF_ZC, OFF_G = (
    0, 1024, 2048, 3072, 4096, 5120, 6144, 6400, 7424)

ADAM_LR, ADAM_B1, ADAM_B2, ADAM_EPS, ADAM_WD, ADAM_STEP = 0.001, 0.9, 0.999, 1e-08, 0.01, 10

VMEM_LIMIT = 56 * 1024 * 1024


def _cp(sem=None):
    return pltpu.CompilerParams(dimension_semantics=sem, vmem_limit_bytes=VMEM_LIMIT)


def _dot(a, b):
    return jnp.dot(a, b, preferred_element_type=F32)


def _dot_nt(a, b):
    return lax.dot_general(a, b, (((1,), (1,)), ((), ())), preferred_element_type=F32)


def _dot_tn(a, b):
    return lax.dot_general(a, b, (((0,), (0,)), ((), ())), preferred_element_type=F32)


def _mm(a, b, mode, out_dtype, tm, tn, tk, name, res=None):
    if mode == "nn":
        (m, k), (_, n) = a.shape, b.shape
    elif mode == "nt":
        (m, k), (n, _) = a.shape, b.shape
    else:
        (k, m), (_, n) = a.shape, b.shape
    tm, tn, tk = min(tm, m), min(tn, n), min(tk, k)
    assert m % tm == 0 and n % tn == 0 and k % tk == 0, (name, m, n, k, tm, tn, tk)
    nk = k // tk
    a_spec = {"nn": pl.BlockSpec((tm, tk), lambda i, j, kk: (i, kk)),
              "nt": pl.BlockSpec((tm, tk), lambda i, j, kk: (i, kk)),
              "tn": pl.BlockSpec((tk, tm), lambda i, j, kk: (kk, i))}[mode]
    b_spec = {"nn": pl.BlockSpec((tk, tn), lambda i, j, kk: (kk, j)),
              "nt": pl.BlockSpec((tn, tk), lambda i, j, kk: (j, kk)),
              "tn": pl.BlockSpec((tk, tn), lambda i, j, kk: (kk, j))}[mode]
    dot = {"nn": _dot, "nt": _dot_nt, "tn": _dot_tn}[mode]
    has_res = res is not None

    def body(*refs):
        if has_res:
            a_ref, b_ref, r_ref, o_ref, acc = refs
        else:
            a_ref, b_ref, o_ref, acc = refs
        kk = pl.program_id(2)

        @pl.when(kk == 0)
        def _():
            acc[...] = jnp.zeros_like(acc)

        acc[...] += dot(a_ref[...].astype(BF16), b_ref[...].astype(BF16))

        @pl.when(kk == nk - 1)
        def _():
            r = acc[...]
            if has_res:
                r = r + r_ref[...]
            o_ref[...] = r.astype(out_dtype)

    in_specs = [a_spec, b_spec]
    args = [a, b]
    if has_res:
        in_specs.append(pl.BlockSpec((tm, tn), lambda i, j, kk: (i, j)))
        args.append(res)
    return pl.pallas_call(
        body, name=name,
        grid=(m // tm, n // tn, nk),
        in_specs=in_specs,
        out_specs=pl.BlockSpec((tm, tn), lambda i, j, kk: (i, j)),
        out_shape=jax.ShapeDtypeStruct((m, n), out_dtype),
        scratch_shapes=[pltpu.VMEM((tm, tn), F32)],
        compiler_params=_cp(("parallel", "parallel", "arbitrary")),
    )(*args)


def _rms(x, w):
    return x * lax.rsqrt(jnp.mean(x * x, axis=-1, keepdims=True) + EPS) * w


def _rms_fwd(x, w, name):
    L, D = x.shape
    tm = min(L, 256)

    def body(x_ref, w_ref, h_ref):
        h_ref[...] = _rms(x_ref[...], w_ref[...]).astype(BF16)

    return pl.pallas_call(
        body, name=name, grid=(L // tm,),
        in_specs=[pl.BlockSpec((tm, D), lambda i: (i, 0)), pl.BlockSpec((1, D), lambda i: (0, 0))],
        out_specs=pl.BlockSpec((tm, D), lambda i: (i, 0)),
        out_shape=jax.ShapeDtypeStruct((L, D), BF16),
        compiler_params=_cp(("parallel",)),
    )(x, w)


def _rms_bwd(x, w, dh, dres, name):
    L, D = x.shape
    tm = min(L, 256)

    def body(x_ref, w_ref, dh_ref, dres_ref, dx_ref, dw_ref):
        _, vjp = jax.vjp(_rms, x_ref[...], w_ref[...])
        dx, dw = vjp(dh_ref[...])
        dx_ref[...] = dx + dres_ref[...]

        @pl.when(pl.program_id(0) == 0)
        def _():
            dw_ref[...] = jnp.zeros_like(dw_ref)

        dw_ref[...] += dw

    row = pl.BlockSpec((tm, D), lambda i: (i, 0))
    vec = pl.BlockSpec((1, D), lambda i: (0, 0))
    return pl.pallas_call(
        body, name=name, grid=(L // tm,),
        in_specs=[row, vec, row, row],
        out_specs=[row, vec],
        out_shape=[jax.ShapeDtypeStruct((L, D), F32), jax.ShapeDtypeStruct((1, D), F32)],
        compiler_params=_cp(("arbitrary",)),
    )(x, w, dh, dres)


def _final(x, fw, tgt, name):
    L, D = x.shape
    tm = min(L, 256)

    def loss_fn(xv, wv, tv):
        err = _rms(xv, wv) - tv
        return jnp.sum(err * err) * (0.5 / D)

    def body(x_ref, w_ref, t_ref, loss_ref, dx_ref, dw_ref):
        tv = t_ref[...]
        val, vjp = jax.vjp(lambda a, b: loss_fn(a, b, tv), x_ref[...], w_ref[...])
        dx, dw = vjp(jnp.ones((), F32))
        dx_ref[...] = dx

        @pl.when(pl.program_id(0) == 0)
        def _():
            dw_ref[...] = jnp.zeros_like(dw_ref)
            loss_ref[...] = jnp.zeros_like(loss_ref)

        dw_ref[...] += dw
        loss_ref[...] += jnp.full(loss_ref.shape, val, F32)

    row = pl.BlockSpec((tm, D), lambda i: (i, 0))
    vec = pl.BlockSpec((1, D), lambda i: (0, 0))
    return pl.pallas_call(
        body, name=name, grid=(L // tm,),
        in_specs=[row, vec, row],
        out_specs=[pl.BlockSpec((8, 128), lambda i: (0, 0)), row, vec],
        out_shape=[jax.ShapeDtypeStruct((8, 128), F32), jax.ShapeDtypeStruct((L, D), F32),
                   jax.ShapeDtypeStruct((1, D), F32)],
        compiler_params=_cp(("arbitrary",)),
    )(x, fw, tgt)


def _s5_param_fn(a_re, a_im, log_dt, bt_re, bt_im):
    dt = jnp.exp(log_dt)
    zr, zi = a_re * dt, a_im * dt
    er = jnp.exp(zr)
    lr, li = er * jnp.cos(zi), er * jnp.sin(zi)
    nr, ni = lr - 1.0, li
    den = a_re * a_re + a_im * a_im
    cr = (nr * a_re + ni * a_im) / den
    ci = (ni * a_re - nr * a_im) / den
    bbr = cr[None] * bt_re - ci[None] * bt_im
    bbi = cr[None] * bt_im + ci[None] * bt_re
    return lr, li, bbr, bbi


def _s5_params_fwd(a_re, a_im, log_dt, bt_re, bt_im, name):
    def body(ar, ai, ld, br, bi, lr, li, bbr, bbi):
        o = _s5_param_fn(ar[...], ai[...], ld[...], br[...], bi[...])
        lr[...], li[...], bbr[...], bbi[...] = o

    gp = jax.ShapeDtypeStruct(a_re.shape, F32)
    cgp = jax.ShapeDtypeStruct(bt_re.shape, F32)
    return pl.pallas_call(body, name=name, out_shape=[gp, gp, cgp, cgp])(a_re, a_im, log_dt, bt_re, bt_im)


def _s5_params_bwd(a_re, a_im, log_dt, bt_re, bt_im, dlr, dli, dbbr, dbbi, name):
    def body(ar, ai, ld, br, bi, g0, g1, g2, g3, o0, o1, o2, o3, o4):
        _, vjp = jax.vjp(_s5_param_fn, ar[...], ai[...], ld[...], br[...], bi[...])
        o0[...], o1[...], o2[...], o3[...], o4[...] = vjp((g0[...], g1[...], g2[...], g3[...]))

    gp = jax.ShapeDtypeStruct(a_re.shape, F32)
    cgp = jax.ShapeDtypeStruct(bt_re.shape, F32)
    return pl.pallas_call(body, name=name,
                          out_shape=[gp, gp, jax.ShapeDtypeStruct(log_dt.shape, F32), cgp, cgp])(
        a_re, a_im, log_dt, bt_re, bt_im, dlr, dli, dbbr, dbbi)


def _cmul(ar, ai, br, bi):
    return ar * br - ai * bi, ar * bi + ai * br


def _cpow(lr, li, n):
    rr, ri = None, None
    br, bi = lr, li
    while n:
        if n & 1:
            rr, ri = (br, bi) if rr is None else _cmul(rr, ri, br, bi)
        n >>= 1
        if n:
            br, bi = _cmul(br, bi, br, bi)
    return rr, ri


def _shift_rows(x, up):
    row = lax.broadcasted_iota(jnp.int32, x.shape, 0)
    if up:
        return jnp.where(row == N_SEG - 1, 0.0, pltpu.roll(x, N_SEG - 1, 0))
    return jnp.where(row == 0, 0.0, pltpu.roll(x, 1, 0))


def _seg_scan(s_re, s_im, lam, seg, reverse):
    stride = seg + SEG_PAD
    nt = SLAB_ST // 128
    lam_t = [(jnp.broadcast_to(lam[0][:, j * 128:(j + 1) * 128], (N_SEG, 128)),
              jnp.broadcast_to(lam[1][:, j * 128:(j + 1) * 128], (N_SEG, 128))) for j in range(nt)]

    def rows(i):
        return pl.ds(i, N_SEG, stride=stride)

    def step1(t, carry):
        i = seg - 1 - t if reverse else t
        out = []
        for j in range(nt):
            cr, ci = carry[2 * j], carry[2 * j + 1]
            nr, ni = _cmul(lam_t[j][0], lam_t[j][1], cr, ci)
            nr = nr + s_re[j, rows(i), :]
            ni = ni + s_im[j, rows(i), :]
            s_re[j, rows(i), :] = nr
            s_im[j, rows(i), :] = ni
            out += [nr, ni]
        return tuple(out)

    zero = tuple(jnp.zeros((N_SEG, 128), F32) for _ in range(2 * nt))
    ends = lax.fori_loop(0, seg, step1, zero)

    carries = []
    for j in range(nt):
        pr, pi = _cpow(lam_t[j][0], lam_t[j][1], seg)
        cr, ci = jnp.zeros((N_SEG, 128), F32), jnp.zeros((N_SEG, 128), F32)
        for _ in range(N_SEG - 1):
            tr, ti = _cmul(pr, pi, cr, ci)
            cr = _shift_rows(tr + ends[2 * j], reverse)
            ci = _shift_rows(ti + ends[2 * j + 1], reverse)
        carries += [cr, ci]

    def step2(t, pw):
        i = seg - 1 - t if reverse else t
        out = []
        for j in range(nt):
            pr, pi = pw[2 * j], pw[2 * j + 1]
            ar, ai = _cmul(pr, pi, carries[2 * j], carries[2 * j + 1])
            s_re[j, rows(i), :] = s_re[j, rows(i), :] + ar
            s_im[j, rows(i), :] = s_im[j, rows(i), :] + ai
            qr, qi = _cmul(pr, pi, lam_t[j][0], lam_t[j][1])
            out += [qr, qi]
        return tuple(out)

    lax.fori_loop(0, seg, step2, tuple(x for j in range(nt) for x in lam_t[j]))
    return carries


def _seg_rows(ref, k, seg):
    stride = seg + SEG_PAD
    return jnp.concatenate([ref[j, pl.ds(k * stride, seg), :] for j in range(SLAB_ST // 128)], axis=-1)


def _seg_store(ref, k, seg, val):
    stride = seg + SEG_PAD
    for j in range(SLAB_ST // 128):
        ref[j, pl.ds(k * stride, seg), :] = val[:, j * 128:(j + 1) * 128]


def _s5_specs(L):
    col = lambda off: pl.BlockSpec((L, SLAB_CH), lambda j: (0, off + j))
    mat_b = pl.BlockSpec((None, SLAB_CH, SLAB_ST), lambda j: (j, 0, 0))
    mat_c = pl.BlockSpec((None, SLAB_ST, SLAB_CH), lambda j: (j, 0, 0))
    vec_s = pl.BlockSpec((None, 1, SLAB_ST), lambda j: (j, 0, 0))
    vec_c = pl.BlockSpec((None, 1, SLAB_CH), lambda j: (j, 0, 0))
    return col, mat_b, mat_c, vec_s, vec_c


def _s5_states(u_ref, bre_ref, bim_ref, lam, s_re, s_im, seg):
    for k in range(N_SEG):
        uk = u_ref[pl.ds(k * seg, seg), :]
        _seg_store(s_re, k, seg, _dot(uk, bre_ref[...]))
        _seg_store(s_im, k, seg, _dot(uk, bim_ref[...]))
    return _seg_scan(s_re, s_im, lam, seg, reverse=False)


def _s5_fwd(proj, bre, bim, cre_t, cim_t, lam_re, lam_im, dvec, name):
    L = proj.shape[0]
    seg = L // N_SEG
    col, mat_b, mat_c, vec_s, vec_c = _s5_specs(L)
    rows = N_SEG * (seg + SEG_PAD)

    def body(u_ref, bre_ref, bim_ref, cre_ref, cim_ref, lr_ref, li_ref, d_ref, y_ref, s_re, s_im):
        _s5_states(u_ref, bre_ref, bim_ref, (lr_ref[...], li_ref[...]), s_re, s_im, seg)
        for k in range(N_SEG):
            y = (_dot(_seg_rows(s_re, k, seg).astype(BF16), cre_ref[...])
                 - _dot(_seg_rows(s_im, k, seg).astype(BF16), cim_ref[...]))
            y = y + d_ref[...] * u_ref[pl.ds(k * seg, seg), :].astype(F32)
            y_ref[pl.ds(k * seg, seg), :] = jax.nn.gelu(y).astype(BF16)

    return pl.pallas_call(
        body, name=name, grid=(N_SLAB,),
        in_specs=[col(OFF_UA // SLAB_CH), mat_b, mat_b, mat_c, mat_c, vec_s, vec_s, vec_c],
        out_specs=pl.BlockSpec((L, SLAB_CH), lambda j: (0, j)),
        out_shape=jax.ShapeDtypeStruct((L, SSM_WIDTH), BF16),
        scratch_shapes=[pltpu.VMEM((SLAB_ST // 128, rows, 128), F32)] * 2,
        compiler_params=_cp(("parallel",)),
    )(proj, bre, bim, cre_t, cim_t, lam_re, lam_im, dvec)


def _s5_bwd(proj, dy, bre, bim, cre_t, cim_t, lam_re, lam_im, dvec, name):
    L = proj.shape[0]
    seg = L // N_SEG
    stride = seg + SEG_PAD
    col, mat_b, mat_c, vec_s, vec_c = _s5_specs(L)
    rows = N_SEG * stride
    nt = SLAB_ST // 128

    def body(u_ref, dy_ref, bre_ref, bim_ref, cre_ref, cim_ref, lr_ref, li_ref, d_ref,
             du_ref, dbre_ref, dbim_ref, dcre_ref, dcim_ref, dlr_ref, dli_ref, dd_ref,
             s_re, s_im, a_re, a_im, dyp):
        lam = (lr_ref[...], li_ref[...])
        carry_s = _s5_states(u_ref, bre_ref, bim_ref, lam, s_re, s_im, seg)
        dcre = jnp.zeros((SLAB_ST, SLAB_CH), F32)
        dcim = jnp.zeros((SLAB_ST, SLAB_CH), F32)
        dd = jnp.zeros((1, SLAB_CH), F32)
        for k in range(N_SEG):
            sre = _seg_rows(s_re, k, seg).astype(BF16)
            sim = _seg_rows(s_im, k, seg).astype(BF16)
            uk = u_ref[pl.ds(k * seg, seg), :].astype(F32)
            ypre = _dot(sre, cre_ref[...]) - _dot(sim, cim_ref[...]) + d_ref[...] * uk
            _, vjp = jax.vjp(jax.nn.gelu, ypre)
            (dyk,) = vjp(dy_ref[pl.ds(k * seg, seg), :].astype(F32))
            dyp[pl.ds(k * seg, seg), :] = dyk
            dd = dd + jnp.sum(dyk * uk, axis=0, keepdims=True)
            dyb = dyk.astype(BF16)
            dcre = dcre + _dot_tn(sre, dyb)
            dcim = dcim - _dot_tn(sim, dyb)
            _seg_store(a_re, k, seg, _dot_nt(dyb, cre_ref[...]))
            _seg_store(a_im, k, seg, -_dot_nt(dyb, cim_ref[...]))
        dcre_ref[...] = dcre
        dcim_ref[...] = dcim
        dd_ref[...] = dd

        _seg_scan(a_re, a_im, (lam[0], -lam[1]), seg, reverse=True)

        def acc_dlam(i, acc):
            out = []
            for j in range(nt):
                ar = a_re[j, pl.ds(i, N_SEG, stride=stride), :]
                ai = a_im[j, pl.ds(i, N_SEG, stride=stride), :]
                pr = s_re[j, pl.ds(i - 1, N_SEG, stride=stride), :]
                pi = s_im[j, pl.ds(i - 1, N_SEG, stride=stride), :]
                out += [acc[2 * j] + ar * pr + ai * pi, acc[2 * j + 1] + ai * pr - ar * pi]
            return tuple(out)

        first = []
        for j in range(nt):
            ar = a_re[j, pl.ds(0, N_SEG, stride=stride), :]
            ai = a_im[j, pl.ds(0, N_SEG, stride=stride), :]
            pr, pi = carry_s[2 * j], carry_s[2 * j + 1]
            first += [ar * pr + ai * pi, ai * pr - ar * pi]
        acc = lax.fori_loop(1, seg, acc_dlam, tuple(first))
        dlr_ref[...] = jnp.concatenate([jnp.sum(acc[2 * j], axis=0, keepdims=True) for j in range(nt)], axis=-1)
        dli_ref[...] = jnp.concatenate([jnp.sum(acc[2 * j + 1], axis=0, keepdims=True) for j in range(nt)], axis=-1)

        dbre = jnp.zeros((SLAB_CH, SLAB_ST), F32)
        dbim = jnp.zeros((SLAB_CH, SLAB_ST), F32)
        for k in range(N_SEG):
            are = _seg_rows(a_re, k, seg).astype(BF16)
            aim = _seg_rows(a_im, k, seg).astype(BF16)
            uk = u_ref[pl.ds(k * seg, seg), :]
            du = _dot_nt(are, bre_ref[...]) + _dot_nt(aim, bim_ref[...]) + dyp[pl.ds(k * seg, seg), :] * d_ref[...]
            du_ref[pl.ds(k * seg, seg), :] = du.astype(BF16)
            dbre = dbre + _dot_tn(uk, are)
            dbim = dbim + _dot_tn(uk, aim)
        dbre_ref[...] = dbre
        dbim_ref[...] = dbim

    scan_buf = pltpu.VMEM((nt, rows, 128), F32)
    return pl.pallas_call(
        body, name=name, grid=(N_SLAB,),
        in_specs=[col(OFF_UA // SLAB_CH), pl.BlockSpec((L, SLAB_CH), lambda j: (0, j)),
                  mat_b, mat_b, mat_c, mat_c, vec_s, vec_s, vec_c],
        out_specs=[pl.BlockSpec((L, SLAB_CH), lambda j: (0, j)), mat_b, mat_b, mat_c, mat_c, vec_s, vec_s, vec_c],
        out_shape=[jax.ShapeDtypeStruct((L, SSM_WIDTH), BF16),
                   jax.ShapeDtypeStruct((N_SLAB, SLAB_CH, SLAB_ST), F32),
                   jax.ShapeDtypeStruct((N_SLAB, SLAB_CH, SLAB_ST), F32),
                   jax.ShapeDtypeStruct((N_SLAB, SLAB_ST, SLAB_CH), F32),
                   jax.ShapeDtypeStruct((N_SLAB, SLAB_ST, SLAB_CH), F32),
                   jax.ShapeDtypeStruct((N_SLAB, 1, SLAB_ST), F32),
                   jax.ShapeDtypeStruct((N_SLAB, 1, SLAB_ST), F32),
                   jax.ShapeDtypeStruct((N_SLAB, 1, SLAB_CH), F32)],
        scratch_shapes=[scan_buf, scan_buf, scan_buf, scan_buf, pltpu.VMEM((L, SLAB_CH), F32)],
        compiler_params=_cp(("parallel",)),
    )(proj, dy, bre, bim, cre_t, cim_t, lam_re, lam_im, dvec)


def _glu_point(y0, pre, za, b):
    return y0 * jax.nn.sigmoid(pre + b) * jax.nn.silu(za)


def _glu_specs(L, tm):
    row = pl.BlockSpec((tm, SSM_WIDTH), lambda i: (i, 0))
    za = pl.BlockSpec((tm, SSM_WIDTH), lambda i: (i, OFF_ZA // SSM_WIDTH))
    wmat = pl.BlockSpec((SSM_WIDTH, SSM_WIDTH), lambda i: (0, 0))
    vec = pl.BlockSpec((1, SSM_WIDTH), lambda i: (0, 0))
    return row, za, wmat, vec


def _glu_fwd(ya0, proj, w, b, name):
    L = ya0.shape[0]
    tm = min(L, 512)
    row, za, wmat, vec = _glu_specs(L, tm)

    def body(y_ref, z_ref, w_ref, b_ref, o_ref):
        y0 = y_ref[...]
        pre = _dot(y0, w_ref[...])
        o_ref[...] = _glu_point(y0.astype(F32), pre, z_ref[...].astype(F32), b_ref[...]).astype(BF16)

    return pl.pallas_call(
        body, name=name, grid=(L // tm,), in_specs=[row, za, wmat, vec], out_specs=row,
        out_shape=jax.ShapeDtypeStruct((L, SSM_WIDTH), BF16), compiler_params=_cp(("parallel",)),
    )(ya0, proj, w, b)


def _glu_bwd(ya0, proj, w, b, dya, name):
    L = ya0.shape[0]
    tm = min(L, 512)
    row, za, wmat, vec = _glu_specs(L, tm)

    def body(y_ref, z_ref, w_ref, b_ref, g_ref, dy0_ref, dza_ref, dw_ref, db_ref):
        y0 = y_ref[...]
        pre = _dot(y0, w_ref[...])
        _, vjp = jax.vjp(_glu_point, y0.astype(F32), pre, z_ref[...].astype(F32), b_ref[...])
        dy0, dpre, dza, db = vjp(g_ref[...].astype(F32))
        dpb = dpre.astype(BF16)
        dy0_ref[...] = (dy0 + _dot_nt(dpb, w_ref[...])).astype(BF16)
        dza_ref[...] = dza.astype(BF16)

        @pl.when(pl.program_id(0) == 0)
        def _():
            dw_ref[...] = jnp.zeros_like(dw_ref)
            db_ref[...] = jnp.zeros_like(db_ref)

        dw_ref[...] += _dot_tn(y0, dpb)
        db_ref[...] += db

    return pl.pallas_call(
        body, name=name, grid=(L // tm,), in_specs=[row, za, wmat, vec, row],
        out_specs=[row, row, wmat, vec],
        out_shape=[jax.ShapeDtypeStruct((L, SSM_WIDTH), BF16), jax.ShapeDtypeStruct((L, SSM_WIDTH), BF16),
                   jax.ShapeDtypeStruct((SSM_WIDTH, SSM_WIDTH), F32), jax.ShapeDtypeStruct((1, SSM_WIDTH), F32)],
        compiler_params=_cp(("arbitrary",)),
    )(ya0, proj, w, b, dya)


def _sg_norm(vb, ln_w, ln_b):
    v0 = jax.nn.gelu(vb)
    mu = jnp.mean(v0, axis=-1, keepdims=True)
    var = jnp.mean(jnp.square(v0 - mu), axis=-1, keepdims=True)
    return (v0 - mu) * lax.rsqrt(var + EPS) * ln_w + ln_b


def _sg_gate(ub, mixed, zb):
    return jax.nn.gelu(ub) * mixed * jax.nn.silu(zb)


def _sg_specs():
    W = SSM_WIDTH
    blk = lambda off: pl.BlockSpec((CHUNK, W), lambda n: (n, off // W))
    out = pl.BlockSpec((CHUNK, W), lambda n: (n, 0))
    vec = pl.BlockSpec((1, W), lambda n: (0, 0))
    wsp = pl.BlockSpec((SG_HEADS, CHUNK, CHUNK), lambda n: (0, 0, 0))
    bsp = pl.BlockSpec((SG_HEADS, CHUNK, 1), lambda n: (0, 0, 0))
    return blk, out, vec, wsp, bsp


def _sg_masked(w_ref):
    t = lax.broadcasted_iota(jnp.int32, (CHUNK, CHUNK), 0)
    s = lax.broadcasted_iota(jnp.int32, (CHUNK, CHUNK), 1)
    causal = s <= t
    return causal, [jnp.where(causal, w_ref[h], 0.0).astype(BF16) for h in range(SG_HEADS)]


def _sg_mix(wm, vnb, bias_ref):
    return jnp.concatenate(
        [_dot(wm[h], vnb[:, h * CHUNK:(h + 1) * CHUNK]) + bias_ref[h] for h in range(SG_HEADS)], axis=-1)


def _sg_fwd(proj, ln_w, ln_b, w, bias, name):
    L = proj.shape[0]
    blk, out, vec, wsp, bsp = _sg_specs()

    def body(ub_ref, vb_ref, zb_ref, lw_ref, lb_ref, w_ref, bias_ref, o_ref):
        _, wm = _sg_masked(w_ref)
        vnb = _sg_norm(vb_ref[...].astype(F32), lw_ref[...], lb_ref[...]).astype(BF16)
        mixed = _sg_mix(wm, vnb, bias_ref)
        o_ref[...] = _sg_gate(ub_ref[...].astype(F32), mixed, zb_ref[...].astype(F32)).astype(BF16)

    return pl.pallas_call(
        body, name=name, grid=(L // CHUNK,),
        in_specs=[blk(OFF_UB), blk(OFF_VB), blk(OFF_ZB), vec, vec, wsp, bsp], out_specs=out,
        out_shape=jax.ShapeDtypeStruct((L, SSM_WIDTH), BF16), compiler_params=_cp(("parallel",)),
    )(proj, proj, proj, ln_w, ln_b, w, bias)


def _sg_bwd(proj, ln_w, ln_b, w, bias, dyb, name):
    L = proj.shape[0]
    blk, out, vec, wsp, bsp = _sg_specs()

    def body(ub_ref, vb_ref, zb_ref, lw_ref, lb_ref, w_ref, bias_ref, g_ref,
             dub_ref, dvb_ref, dzb_ref, dlw_ref, dlb_ref, dw_ref, dbias_ref):
        causal, wm = _sg_masked(w_ref)
        vb = vb_ref[...].astype(F32)
        vn, vjp_norm = jax.vjp(_sg_norm, vb, lw_ref[...], lb_ref[...])
        vnb = vn.astype(BF16)
        mixed = _sg_mix(wm, vnb, bias_ref)
        _, vjp_gate = jax.vjp(_sg_gate, ub_ref[...].astype(F32), mixed, zb_ref[...].astype(F32))
        dub, dmixed, dzb = vjp_gate(g_ref[...].astype(F32))
        dub_ref[...] = dub.astype(BF16)
        dzb_ref[...] = dzb.astype(BF16)

        @pl.when(pl.program_id(0) == 0)
        def _():
            dlw_ref[...] = jnp.zeros_like(dlw_ref)
            dlb_ref[...] = jnp.zeros_like(dlb_ref)
            dw_ref[...] = jnp.zeros_like(dw_ref)
            dbias_ref[...] = jnp.zeros_like(dbias_ref)

        dvn = []
        for h in range(SG_HEADS):
            dm = dmixed[:, h * CHUNK:(h + 1) * CHUNK]
            dmb = dm.astype(BF16)
            dbias_ref[h] += jnp.sum(dm, axis=-1, keepdims=True)
            dw_ref[h] += jnp.where(causal, _dot_nt(dmb, vnb[:, h * CHUNK:(h + 1) * CHUNK]), 0.0)
            dvn.append(_dot_tn(wm[h], dmb))
        dvb, dlw, dlb = vjp_norm(jnp.concatenate(dvn, axis=-1))
        dvb_ref[...] = dvb.astype(BF16)
        dlw_ref[...] += dlw
        dlb_ref[...] += dlb

    act = jax.ShapeDtypeStruct((L, SSM_WIDTH), BF16)
    return pl.pallas_call(
        body, name=name, grid=(L // CHUNK,),
        in_specs=[blk(OFF_UB), blk(OFF_VB), blk(OFF_ZB), vec, vec, wsp, bsp, out],
        out_specs=[out, out, out, vec, vec, wsp, bsp],
        out_shape=[act, act, act, jax.ShapeDtypeStruct((1, SSM_WIDTH), F32), jax.ShapeDtypeStruct((1, SSM_WIDTH), F32),
                   jax.ShapeDtypeStruct((SG_HEADS, CHUNK, CHUNK), F32), jax.ShapeDtypeStruct((SG_HEADS, CHUNK, 1), F32)],
        compiler_params=_cp(("arbitrary",)),
    )(proj, proj, proj, ln_w, ln_b, w, bias, dyb)


def _rope_tables(L):
    half = ROT_DIM // 2
    inv_freq = ROPE_THETA ** (-jnp.arange(0, ROT_DIM, 2, dtype=F32) / ROT_DIM)
    ang = jnp.arange(L, dtype=F32)[:, None] * inv_freq[None, :]
    cos, sin = jnp.cos(ang), jnp.sin(ang)
    ones = jnp.ones((L, HEAD_DIM - ROT_DIM), F32)
    cos_h = jnp.concatenate([cos, cos, ones], axis=-1)
    sin_h = jnp.concatenate([-sin, sin, 0.0 * ones], axis=-1)
    src = jnp.arange(HEAD_DIM)[:, None]
    dst = jnp.arange(HEAD_DIM)[None, :]
    p_h = (((dst < half) & (src == dst + half)) | ((dst >= half) & (dst < ROT_DIM) & (src == dst - half))).astype(F32)
    p2 = jnp.kron(jnp.eye(2, dtype=F32), p_h).astype(BF16)
    return jnp.tile(cos_h, (1, 2)), jnp.tile(sin_h, (1, 2)), p2


def _rope(t, cos, sin, p2):
    n = t.shape[1] // 128
    tb = t.astype(BF16)
    sw = jnp.concatenate([_dot(tb[:, i * 128:(i + 1) * 128], p2) for i in range(n)], axis=-1) if n > 1 else _dot(tb, p2)
    return t * jnp.tile(cos, (1, n)) + sw * jnp.tile(sin, (1, n))


def _rope_t(g, cos, sin, p2):
    n = g.shape[1] // 128
    gs = (g * jnp.tile(sin, (1, n))).astype(BF16)
    sw = jnp.concatenate([_dot_nt(gs[:, i * 128:(i + 1) * 128], p2) for i in range(n)], axis=-1) if n > 1 else _dot_nt(gs, p2)
    return g * jnp.tile(cos, (1, n)) + sw


def _lane_lo(shape):
    return (lax.broadcasted_iota(jnp.int32, shape, len(shape) - 1) % 128) < HEAD_DIM


def _dup_halves(x):
    xr = pltpu.roll(x, HEAD_DIM, 1)
    lo = _lane_lo(x.shape)
    return jnp.where(lo, x, xr), jnp.where(lo, xr, x)


def _fold_halves(d0, d1):
    f0 = d0 + pltpu.roll(d0, HEAD_DIM, 1)
    f1 = d1 + pltpu.roll(d1, HEAD_DIM, 1)
    return jnp.where(_lane_lo(d0.shape), f0, f1)


def _attn_mask():
    qi = lax.broadcasted_iota(jnp.int32, (CHUNK, 2 * CHUNK), 0)
    kj = lax.broadcasted_iota(jnp.int32, (CHUNK, 2 * CHUNK), 1)
    return qi, kj


def _attn_specs():
    qsp = pl.BlockSpec((CHUNK, 1024), lambda n: (n, OFF_Q // 1024))
    kv_cur = pl.BlockSpec((CHUNK, 256), lambda n: (n, OFF_KV // 256))
    kv_prev = pl.BlockSpec((CHUNK, 256), lambda n: (jnp.maximum(n - 1, 0), OFF_KV // 256))
    zsp = [pl.BlockSpec((CHUNK, 256), functools.partial(lambda n, q: (n, OFF_ZC // 256 + q), q=q)) for q in range(4)]
    tab_cur = pl.BlockSpec((CHUNK, 128), lambda n: (n, 0))
    tab_prev = pl.BlockSpec((CHUNK, 128), lambda n: (jnp.maximum(n - 1, 0), 0))
    p2sp = pl.BlockSpec((128, 128), lambda n: (0, 0))
    sink = pl.BlockSpec(memory_space=pltpu.SMEM)
    wide = pl.BlockSpec((CHUNK, 1024), lambda n: (n, 0))
    return qsp, kv_cur, kv_prev, zsp, tab_cur, tab_prev, p2sp, sink, wide


def _attn_core(n, q_ref, kvc_ref, kvp_ref, cosc_ref, sinc_ref, cosp_ref, sinp_ref, p2_ref, sink_ref):
    p2 = p2_ref[...]
    qr = _rope(q_ref[...].astype(F32), cosc_ref[...], sinc_ref[...], p2).astype(BF16)
    kc = _rope(kvc_ref[:, 0:128].astype(F32), cosc_ref[...], sinc_ref[...], p2)
    kp = _rope(kvp_ref[:, 0:128].astype(F32), cosp_ref[...], sinp_ref[...], p2)
    k_all = jnp.concatenate([kp, kc], axis=0).astype(BF16)
    v_all = jnp.concatenate([kvp_ref[:, 128:256], kvc_ref[:, 128:256]], axis=0)
    kd = _dup_halves(k_all)
    vd = _dup_halves(v_all)
    qi, kj = _attn_mask()
    allowed = ((kj < CHUNK) & (kj > qi) & (n > 0)) | ((kj >= CHUNK) & (kj - CHUNK <= qi))
    lo = _lane_lo((CHUNK, 128))
    probs = []
    for h in range(ATT_HEADS):
        m, half, g = h // 2, h % 2, h // 8
        qp = qr[:, m * 128:(m + 1) * 128]
        qm = jnp.where(lo if half == 0 else ~lo, qp, jnp.zeros_like(qp))
        s = jnp.where(allowed, _dot_nt(qm, kd[g]) * (HEAD_DIM ** -0.5), NEG_INF)
        snk = sink_ref[h]
        mx = jnp.maximum(jnp.max(s, axis=-1, keepdims=True), snk)
        e = jnp.exp(s - mx)
        es = jnp.exp(snk - mx)
        inv = 1.0 / (jnp.sum(e, axis=-1, keepdims=True) + es)
        probs.append((qm, e * inv, es * inv))
    return qr, kd, vd, probs, lo


def _attn_out(vd, probs, lo):
    outs = []
    for m in range(ATT_HEADS // 2):
        g = m // 4
        o0 = _dot(probs[2 * m][1].astype(BF16), vd[g])
        o1 = _dot(probs[2 * m + 1][1].astype(BF16), vd[g])
        outs.append(jnp.where(lo, o0, o1))
    return jnp.concatenate(outs, axis=-1)


def _silu_gate(o, z):
    return o * jax.nn.silu(z)


def _attn_fwd(proj, sinks, tabs, name):
    L = proj.shape[0]
    cos2, sin2, p2 = tabs
    qsp, kv_cur, kv_prev, zsp, tab_cur, tab_prev, p2sp, sink, wide = _attn_specs()

    def body(q_ref, kvc_ref, kvp_ref, z0, z1, z2, z3, cosc, sinc, cosp, sinp, p2_ref, sink_ref, o_ref):
        n = pl.program_id(0)
        _, _, vd, probs, lo = _attn_core(n, q_ref, kvc_ref, kvp_ref, cosc, sinc, cosp, sinp, p2_ref, sink_ref)
        o = _attn_out(vd, probs, lo)
        z = jnp.concatenate([z0[...], z1[...], z2[...], z3[...]], axis=-1).astype(F32)
        o_ref[...] = _silu_gate(o, z).astype(BF16)

    return pl.pallas_call(
        body, name=name, grid=(L // CHUNK,),
        in_specs=[qsp, kv_cur, kv_prev, *zsp, tab_cur, tab_cur, tab_prev, tab_prev, p2sp, sink],
        out_specs=wide, out_shape=jax.ShapeDtypeStruct((L, 1024), BF16), compiler_params=_cp(("parallel",)),
    )(proj, proj, proj, proj, proj, proj, proj, cos2, sin2, cos2, sin2, p2, sinks)


def _attn_bwd(proj, sinks, tabs, dyc, name):
    L = proj.shape[0]
    cos2, sin2, p2 = tabs
    qsp, kv_cur, kv_prev, zsp, tab_cur, tab_prev, p2sp, sink, wide = _attn_specs()
    kvo = pl.BlockSpec((CHUNK, 256), lambda n: (n, 0))

    def body(q_ref, kvc_ref, kvp_ref, z0, z1, z2, z3, cosc, sinc, cosp, sinp, p2_ref, sink_ref, g_ref,
             dq_ref, dz_ref, dkvc_ref, dkvp_ref, dsink_ref):
        n = pl.program_id(0)
        _, kd, vd, probs, lo = _attn_core(n, q_ref, kvc_ref, kvp_ref, cosc, sinc, cosp, sinp, p2_ref, sink_ref)
        o = _attn_out(vd, probs, lo)
        z = jnp.concatenate([z0[...], z1[...], z2[...], z3[...]], axis=-1).astype(F32)
        _, vjp = jax.vjp(_silu_gate, o, z)
        do, dz = vjp(g_ref[...].astype(F32))
        dz_ref[...] = dz.astype(BF16)

        @pl.when(n == 0)
        def _():
            dsink_ref[...] = jnp.zeros_like(dsink_ref)

        dkd = [jnp.zeros((2 * CHUNK, 128), F32), jnp.zeros((2 * CHUNK, 128), F32)]
        dvd = [jnp.zeros((2 * CHUNK, 128), F32), jnp.zeros((2 * CHUNK, 128), F32)]
        dq_pairs = []
        for m in range(ATT_HEADS // 2):
            g = m // 4
            dop = do[:, m * 128:(m + 1) * 128].astype(BF16)
            dq_h = []
            for half in range(2):
                h = 2 * m + half
                qm, p, ps = probs[h]
                dom = jnp.where(lo if half == 0 else ~lo, dop, jnp.zeros_like(dop))
                dp = _dot_nt(dom, vd[g])
                rs = jnp.sum(p * dp, axis=-1, keepdims=True)
                ds = (p * (dp - rs) * (HEAD_DIM ** -0.5)).astype(BF16)
                dsink_ref[h:h + 1, :] += jnp.broadcast_to(jnp.sum(-ps * rs, axis=0, keepdims=True), (1, 128))
                dq_h.append(_dot(ds, kd[g]))
                dkd[g] = dkd[g] + _dot_tn(ds, qm)
                dvd[g] = dvd[g] + _dot_tn(p.astype(BF16), dom)
            dq_pairs.append(jnp.where(lo, dq_h[0], dq_h[1]))
        p2 = p2_ref[...]
        dq_ref[...] = _rope_t(jnp.concatenate(dq_pairs, axis=-1), cosc[...], sinc[...], p2).astype(BF16)
        dk_rot = _fold_halves(dkd[0], dkd[1])
        dv = _fold_halves(dvd[0], dvd[1])
        dkp = _rope_t(dk_rot[0:CHUNK], cosp[...], sinp[...], p2)
        dkc = _rope_t(dk_rot[CHUNK:2 * CHUNK], cosc[...], sinc[...], p2)
        dkvp_ref[...] = jnp.concatenate([dkp, dv[0:CHUNK]], axis=-1)
        dkvc_ref[...] = jnp.concatenate([dkc, dv[CHUNK:2 * CHUNK]], axis=-1)

    act = jax.ShapeDtypeStruct((L, 1024), BF16)
    kvs = jax.ShapeDtypeStruct((L, 256), F32)
    return pl.pallas_call(
        body, name=name, grid=(L // CHUNK,),
        in_specs=[qsp, kv_cur, kv_prev, *zsp, tab_cur, tab_cur, tab_prev, tab_prev, p2sp, sink, wide],
        out_specs=[wide, wide, kvo, kvo, pl.BlockSpec((ATT_HEADS, 128), lambda n: (0, 0))],
        out_shape=[act, act, kvs, kvs, jax.ShapeDtypeStruct((ATT_HEADS, 128), F32)],
        compiler_params=_cp(("arbitrary",)),
    )(proj, proj, proj, proj, proj, proj, proj, cos2, sin2, cos2, sin2, p2, sinks, dyc)


MERGE_TN = 256


def _merge_point(ta, tb, tc, ga, gb, gc):
    return jax.nn.sigmoid(ga) * ta + jax.nn.sigmoid(gb) * tb + jax.nn.sigmoid(gc) * tc


def _merge_specs(tm):
    nj = D_MODEL // MERGE_TN
    t = pl.BlockSpec((tm, MERGE_TN), lambda i, j: (i, j))
    gates = [pl.BlockSpec((tm, MERGE_TN), functools.partial(lambda i, j, b: (i, OFF_G // MERGE_TN + b * nj + j), b=b))
             for b in range(3)]
    return t, gates, nj


def _merge_fwd(ta, tb, tc, proj, name):
    L = ta.shape[0]
    tm = min(L, 1024)
    t, gates, nj = _merge_specs(tm)

    def body(ta_ref, tb_ref, tc_ref, ga_ref, gb_ref, gc_ref, o_ref):
        f = lambda r: r[...].astype(F32)
        o_ref[...] = _merge_point(f(ta_ref), f(tb_ref), f(tc_ref), f(ga_ref), f(gb_ref), f(gc_ref)).astype(BF16)

    return pl.pallas_call(
        body, name=name, grid=(L // tm, nj), in_specs=[t, t, t, *gates], out_specs=t,
        out_shape=jax.ShapeDtypeStruct((L, D_MODEL), BF16), compiler_params=_cp(("parallel", "parallel")),
    )(ta, tb, tc, proj, proj, proj)


def _merge_bwd(ta, tb, tc, proj, dm, name):
    L = ta.shape[0]
    tm = min(L, 1024)
    t, gates, nj = _merge_specs(tm)

    def body(ta_ref, tb_ref, tc_ref, ga_ref, gb_ref, gc_ref, dm_ref, dta_ref, dtb_ref, dtc_ref, dga_ref, dgb_ref, dgc_ref):
        f = lambda r: r[...].astype(F32)
        _, vjp = jax.vjp(_merge_point, f(ta_ref), f(tb_ref), f(tc_ref), f(ga_ref), f(gb_ref), f(gc_ref))
        outs = vjp(f(dm_ref))
        for r, v in zip((dta_ref, dtb_ref, dtc_ref, dga_ref, dgb_ref, dgc_ref), outs):
            r[...] = v.astype(BF16)

    act = jax.ShapeDtypeStruct((L, D_MODEL), BF16)
    return pl.pallas_call(
        body, name=name, grid=(L // tm, nj), in_specs=[t, t, t, *gates, t],
        out_specs=[t] * 6, out_shape=[act] * 6,
        compiler_params=_cp(("parallel", "parallel")),
    )(ta, tb, tc, proj, proj, proj, dm)


GRAD_DT = BF16
SMALL = ("norm_w", "ssm_a_re", "ssm_a_im", "ssm_log_dt", "ssm_b_re", "ssm_b_im", "ssm_c_re", "ssm_c_im", "ssm_d",
         "ssm_glu_b", "sg_ln_w", "sg_ln_b", "sg_w", "sg_b", "attn_sinks")
G8 = SSM_GROUPS // N_SLAB


def _slab_b(bb_t):
    x = bb_t.transpose(1, 0, 2).reshape(N_SLAB, G8, SSM_GROUP, SSM_STATE)
    return jnp.einsum("jgcp,gh->jgchp", x, jnp.eye(G8, dtype=x.dtype)).reshape(N_SLAB, SLAB_CH, SLAB_ST)


def _unslab_b(d):
    x = d.reshape(N_SLAB, G8, SSM_GROUP, G8, SSM_STATE)
    x = jnp.einsum("jgchp,gh->jgcp", x, jnp.eye(G8, dtype=x.dtype))
    return x.reshape(SSM_GROUPS, SSM_GROUP, SSM_STATE).transpose(1, 0, 2)


def _slab_c(c):
    x = c.reshape(N_SLAB, G8, SSM_GROUP, SSM_STATE)
    return jnp.einsum("jgcp,gh->jgphc", x, jnp.eye(G8, dtype=x.dtype)).reshape(N_SLAB, SLAB_ST, SLAB_CH)


def _unslab_c(d):
    x = d.reshape(N_SLAB, G8, SSM_STATE, G8, SSM_GROUP)
    x = jnp.einsum("jgphc,gh->jgcp", x, jnp.eye(G8, dtype=x.dtype))
    return x.reshape(SSM_GROUPS, SSM_GROUP, SSM_STATE)


def _s5_prep(p, tag):
    bt_re = p["ssm_b_re"].transpose(2, 0, 1)
    bt_im = p["ssm_b_im"].transpose(2, 0, 1)
    raw = (p["ssm_a_re"], p["ssm_a_im"], p["ssm_log_dt"][:, None], bt_re, bt_im)
    lr, li, bbr, bbi = _s5_params_fwd(*raw, name=f"s5_params_{tag}")
    ops = (_slab_b(bbr).astype(BF16), _slab_b(bbi).astype(BF16),
           _slab_c(p["ssm_c_re"]).astype(BF16), _slab_c(p["ssm_c_im"]).astype(BF16),
           lr.reshape(N_SLAB, 1, SLAB_ST), li.reshape(N_SLAB, 1, SLAB_ST), p["ssm_d"].reshape(N_SLAB, 1, SLAB_CH))
    return raw, ops


def _layer_fwd(x, p, w, tabs, tag):
    L = x.shape[0]
    h = _rms_fwd(x, p["norm_w"][None], f"rms_fwd_{tag}")
    proj = _mm(h, w["win_t"], "nt", BF16, L, 256, D_MODEL, f"in_proj_{tag}")
    s5_raw, s5_ops = _s5_prep(p, tag)
    ya0 = _s5_fwd(proj, *s5_ops, name=f"s5_fwd_{tag}")
    ya = _glu_fwd(ya0, proj, w["glu"], p["ssm_glu_b"][None], f"glu_fwd_{tag}")
    yb = _sg_fwd(proj, p["sg_ln_w"][None], p["sg_ln_b"][None], p["sg_w"], p["sg_b"][:, :, None], f"sg_fwd_{tag}")
    yc = _attn_fwd(proj, p["attn_sinks"], tabs, f"attn_fwd_{tag}")
    ta = _mm(ya, w["wba_t"], "nt", BF16, 1024, 1024, 1024, f"branch_a_{tag}")
    tb = _mm(yb, w["wbb_t"], "nt", BF16, 1024, 1024, 1024, f"branch_b_{tag}")
    tc = _mm(yc, w["wbc_t"], "nt", BF16, 1024, 1024, 1024, f"branch_c_{tag}")
    merged = _merge_fwd(ta, tb, tc, proj, f"merge_fwd_{tag}")
    x_new = _mm(merged, w["wout"], "nn", F32, 1024, 512, D_MODEL, f"out_proj_{tag}", res=x)
    saved = dict(x=x, h=h, proj=proj, s5_raw=s5_raw, s5_ops=s5_ops, ya0=ya0, ya=ya, yb=yb, yc=yc,
                 ta=ta, tb=tb, tc=tc, merged=merged)
    return x_new, saved


def _layer_bwd(dx_out, p, w, tabs, s, tag):
    L = dx_out.shape[0]
    proj = s["proj"]
    big, small = {}, {}
    dmerged = _mm(dx_out, w["wout"], "nt", BF16, 1024, 512, D_MODEL, f"d_merged_{tag}")
    big["wout"] = _mm(s["merged"], dx_out, "tn", GRAD_DT, 512, 1024, L, f"d_wout_{tag}")
    dta, dtb, dtc, dga, dgb, dgc = _merge_bwd(s["ta"], s["tb"], s["tc"], proj, dmerged, f"merge_bwd_{tag}")
    dy = {}
    for br, dt in (("a", dta), ("b", dtb), ("c", dtc)):
        dy[br] = _mm(dt, w[f"wb{br}_t"], "nn", BF16, 1024, 1024, D_MODEL, f"d_y{br}_{tag}")
        big[f"wb{br}_t"] = _mm(dt, s[f"y{br}"], "tn", GRAD_DT, 512, 1024, L, f"d_wb{br}_{tag}")

    dq, dzc, dkvc, dkvp, dsink = _attn_bwd(proj, p["attn_sinks"], tabs, dy["c"], f"attn_bwd_{tag}")
    dkv = dkvc + jnp.concatenate([dkvp[CHUNK:], jnp.zeros((CHUNK, 256), F32)], axis=0)
    small["attn_sinks"] = dsink[:, 0]

    dub, dvb, dzb, dlw, dlb, dsgw, dsgb = _sg_bwd(
        proj, p["sg_ln_w"][None], p["sg_ln_b"][None], p["sg_w"], p["sg_b"][:, :, None], dy["b"], f"sg_bwd_{tag}")
    small.update(sg_ln_w=dlw[0], sg_ln_b=dlb[0], sg_w=dsgw, sg_b=dsgb[:, :, 0])

    dya0, dza, dglu, dglub = _glu_bwd(s["ya0"], proj, w["glu"], p["ssm_glu_b"][None], dy["a"], f"glu_bwd_{tag}")
    big["glu"] = dglu.astype(GRAD_DT)
    small["ssm_glu_b"] = dglub[0]

    dua, dbre, dbim, dcre, dcim, dlr, dli, dd = _s5_bwd(proj, dya0, *s["s5_ops"], name=f"s5_bwd_{tag}")
    da_re, da_im, dlog_dt, dbt_re, dbt_im = _s5_params_bwd(
        *s["s5_raw"], dlr.reshape(SSM_GROUPS, SSM_STATE), dli.reshape(SSM_GROUPS, SSM_STATE),
        _unslab_b(dbre), _unslab_b(dbim), name=f"s5_params_bwd_{tag}")
    small.update(ssm_a_re=da_re, ssm_a_im=da_im, ssm_log_dt=dlog_dt[:, 0],
                 ssm_b_re=dbt_re.transpose(1, 2, 0), ssm_b_im=dbt_im.transpose(1, 2, 0),
                 ssm_c_re=_unslab_c(dcre), ssm_c_im=_unslab_c(dcim), ssm_d=dd.reshape(SSM_WIDTH))

    dproj = jnp.concatenate([dua, dza, dub, dvb, dzb, dq, dkv.astype(BF16), dzc, dga, dgb, dgc], axis=-1)
    dh = _mm(dproj, w["win_t"], "nn", F32, 1024, D_MODEL, 256, f"d_h_{tag}")
    big["win_t"] = _mm(dproj, s["h"], "tn", GRAD_DT, 256, D_MODEL, L, f"d_win_{tag}")
    dx_in, dnw = _rms_bwd(s["x"], p["norm_w"][None], dh, dx_out, f"rms_bwd_{tag}")
    small["norm_w"] = dnw[0]
    return dx_in, big, small


def _local_step(x, tgt, small_p, final_w, big_w):
    L = x.shape[0]
    tabs = _rope_tables(L)
    saved = []
    for l in range(DEPTH):
        x, s = _layer_fwd(x, small_p[l], big_w[l], tabs, f"l{l}")
        saved.append(s)
    loss_acc, dx, dfw = _final(x, final_w[None], tgt, "final_norm_loss")
    big_g, small_g = [None] * DEPTH, [None] * DEPTH
    for l in reversed(range(DEPTH)):
        dx, big_g[l], small_g[l] = _layer_bwd(dx, small_p[l], big_w[l], tabs, saved[l], f"l{l}")
    return loss_acc[0, 0], dx, dfw[0], big_g, small_g


MESH = pl.DeviceIdType.MESH
ANY = pl.BlockSpec(memory_space=pl.ANY)
ROW_ALIGN = 16


def _place():
    return lax.axis_index("x"), lax.axis_index("y"), lax.axis_index("c")


def _allgather_rows(shards, name):
    n = len(shards)

    def body(*refs):
        ins, outs = refs[:n], refs[n:2 * n]
        send_sems, recv_sems, local_sems = refs[2 * n:]
        x, y, c = _place()
        me, sib = (x, y, c), (x, y, 1 - c)
        chips = [(1 - x, y), (x, 1 - y), (1 - x, 1 - y)]

        def rows(a, px, py, pc):
            r = ins[a].shape[0]
            start = pl.multiple_of((4 * px + 2 * py + pc) * r, ROW_ALIGN)
            return outs[a].at[pl.ds(start, r), :]

        def copy(a, k, block, to, src=None):
            return pltpu.make_async_remote_copy(
                src_ref=rows(a, *block) if src is None else src, dst_ref=rows(a, *block),
                send_sem=send_sems.at[a, k], recv_sem=recv_sems.at[a, k], device_id=to, device_id_type=MESH)

        mine = [pltpu.make_async_copy(ins[a], rows(a, *me), local_sems.at[a]) for a in range(n)]
        for cp in mine:
            cp.start()
        first = []
        for a in range(n):
            first.append(copy(a, 0, me, sib, src=ins[a]))
            first += [copy(a, 1 + j, me, (*chip, c), src=ins[a]) for j, chip in enumerate(chips)]
        for cp in first:
            cp.start()
        passed = []
        for j, chip in enumerate(chips):
            for a in range(n):
                copy(a, 1 + j, (*chip, c), me).wait_recv()
                cp = copy(a, 4 + j, (*chip, c), sib)
                cp.start()
                passed.append(cp)
        for a in range(n):
            copy(a, 0, sib, me).wait_recv()
            for j, chip in enumerate(chips):
                copy(a, 4 + j, (*chip, 1 - c), me).wait_recv()
        for cp in first + passed:
            cp.wait_send()
        for cp in mine:
            cp.wait()

    return pl.pallas_call(
        body, name=name,
        in_specs=[ANY] * n, out_specs=[ANY] * n,
        out_shape=[jax.ShapeDtypeStruct((N_DEV * s.shape[0], s.shape[1]), s.dtype) for s in shards],
        scratch_shapes=[pltpu.SemaphoreType.DMA((n, 7)), pltpu.SemaphoreType.DMA((n, 7)), pltpu.SemaphoreType.DMA((n,))],
    )(*shards)


def _rs_swap_cores(grads, name):
    n = len(grads)

    def body(*refs):
        ins, outs = refs[:n], refs[n:2 * n]
        send_sems, recv_sems = refs[2 * n:]
        x, y, c = _place()
        cps = []
        for a in range(n):
            r = ins[a].shape[0] // N_DEV
            for q in range(4):
                start = pl.multiple_of((2 * q + 1 - c) * r, ROW_ALIGN)
                cps.append(pltpu.make_async_remote_copy(
                    src_ref=ins[a].at[pl.ds(start, r), :], dst_ref=outs[a].at[q],
                    send_sem=send_sems.at[a, q], recv_sem=recv_sems.at[a, q],
                    device_id=(x, y, 1 - c), device_id_type=MESH))
        for cp in cps:
            cp.start()
        for cp in cps:
            cp.wait()

    return pl.pallas_call(
        body, name=name, in_specs=[ANY] * n, out_specs=[ANY] * n,
        out_shape=[jax.ShapeDtypeStruct((4, g.shape[0] // N_DEV, g.shape[1]), g.dtype) for g in grads],
        scratch_shapes=[pltpu.SemaphoreType.DMA((n, 4)), pltpu.SemaphoreType.DMA((n, 4))],
    )(*grads)


def _rs_swap_chips(sums, name):
    n = len(sums)

    def body(*refs):
        ins, outs = refs[:n], refs[n:2 * n]
        send_sems, recv_sems, local_sems = refs[2 * n:]
        x, y, c = _place()
        q_me = 2 * x + y
        chips = [(1 - x, y), (x, 1 - y), (1 - x, 1 - y)]
        local = [pltpu.make_async_copy(ins[a].at[q_me], outs[a].at[q_me], local_sems.at[a]) for a in range(n)]
        for cp in local:
            cp.start()
        cps = []
        for a in range(n):
            for j, (px, py) in enumerate(chips):
                cps.append(pltpu.make_async_remote_copy(
                    src_ref=ins[a].at[2 * px + py], dst_ref=outs[a].at[q_me],
                    send_sem=send_sems.at[a, j], recv_sem=recv_sems.at[a, j],
                    device_id=(px, py, c), device_id_type=MESH))
        for cp in cps:
            cp.start()
        for cp in cps:
            cp.wait()
        for cp in local:
            cp.wait()

    return pl.pallas_call(
        body, name=name, in_specs=[ANY] * n, out_specs=[ANY] * n,
        out_shape=[jax.ShapeDtypeStruct(s.shape, s.dtype) for s in sums],
        scratch_shapes=[pltpu.SemaphoreType.DMA((n, 3)), pltpu.SemaphoreType.DMA((n, 3)), pltpu.SemaphoreType.DMA((n,))],
    )(*sums)


def _row_tile(r):
    return max(t for t in range(ROW_ALIGN, min(r, 1024) + 1, ROW_ALIGN) if r % t == 0)


def _rs_add_cores(grad, recv, cidx, name):
    r, cols = recv.shape[1], recv.shape[2]
    tr = _row_tile(r)
    nb = r // tr

    def body(c_ref, g_ref, r_ref, o_ref):
        o_ref[...] = (g_ref[...].astype(F32) + r_ref[...].astype(F32)).astype(o_ref.dtype)

    return pl.pallas_call(
        body, name=name,
        grid_spec=pltpu.PrefetchScalarGridSpec(
            num_scalar_prefetch=1, grid=(4, nb),
            in_specs=[pl.BlockSpec((tr, cols), lambda q, i, c_ref: ((2 * q + c_ref[0]) * nb + i, 0)),
                      pl.BlockSpec((None, tr, cols), lambda q, i, c_ref: (q, i, 0))],
            out_specs=pl.BlockSpec((None, tr, cols), lambda q, i, c_ref: (q, i, 0))),
        out_shape=jax.ShapeDtypeStruct(recv.shape, recv.dtype),
        compiler_params=_cp(("parallel", "parallel")),
    )(cidx, grad, recv)


def _rs_add_chips(recv, name):
    r, cols = recv.shape[1], recv.shape[2]
    tr = _row_tile(r)

    def body(r_ref, o_ref):
        acc = r_ref[0].astype(F32)
        for q in range(1, 4):
            acc = acc + r_ref[q].astype(F32)
        o_ref[...] = acc

    return pl.pallas_call(
        body, name=name, grid=(r // tr,),
        in_specs=[pl.BlockSpec((4, tr, cols), lambda i: (0, i, 0))],
        out_specs=pl.BlockSpec((tr, cols), lambda i: (i, 0)),
        out_shape=jax.ShapeDtypeStruct((r, cols), F32),
        compiler_params=_cp(("parallel",)),
    )(recv)


def _reduce_scatter(grads):
    cidx = lax.axis_index("c").astype(jnp.int32)[None]
    recv = _rs_swap_cores(grads, "rs_swap_cores")
    sums = [_rs_add_cores(g, rv, cidx, f"rs_add_cores_{i}") for i, (g, rv) in enumerate(zip(grads, recv))]
    recv2 = _rs_swap_chips(sums, "rs_swap_chips")
    return [_rs_add_chips(rv, f"rs_add_chips_{i}") for i, rv in enumerate(recv2)]


def _allreduce_small(pack, name):
    R, C = pack.shape

    def body(p_ref, o_ref, gather, send_sems, recv_sems):
        x, y, c = _place()
        me = 4 * x + 2 * y + c
        gather[me] = p_ref[...]
        cps = []
        for k in range(1, N_DEV):
            peer = (1 - x if k & 4 else x, 1 - y if k & 2 else y, 1 - c if k & 1 else c)
            cps.append(pltpu.make_async_remote_copy(
                src_ref=p_ref, dst_ref=gather.at[me], send_sem=send_sems.at[k - 1], recv_sem=recv_sems.at[k - 1],
                device_id=peer, device_id_type=MESH))
        for cp in cps:
            cp.start()
        for cp in cps:
            cp.wait()
        acc = gather[0]
        for d in range(1, N_DEV):
            acc = acc + gather[d]
        o_ref[...] = acc

    return pl.pallas_call(
        body, name=name,
        in_specs=[pl.BlockSpec(memory_space=pltpu.VMEM)], out_specs=pl.BlockSpec(memory_space=pltpu.VMEM),
        out_shape=jax.ShapeDtypeStruct((R, C), F32),
        scratch_shapes=[pltpu.VMEM((N_DEV, R, C), F32), pltpu.SemaphoreType.DMA((N_DEV - 1,)),
                        pltpu.SemaphoreType.DMA((N_DEV - 1,))],
        compiler_params=pltpu.CompilerParams(vmem_limit_bytes=VMEM_LIMIT),
    )(pack)


def _adamw(w, g, m, v, name):
    rows, cols = w.shape
    tr = 256 if rows % 256 == 0 else rows
    c1 = 1.0 - ADAM_B1 ** ADAM_STEP
    c2 = 1.0 - ADAM_B2 ** ADAM_STEP

    def body(w_ref, g_ref, m_ref, v_ref, d_ref, nm_ref, nv_ref):
        gv = g_ref[...]
        nm = ADAM_B1 * m_ref[...] + (1.0 - ADAM_B1) * gv
        nv = ADAM_B2 * v_ref[...] + (1.0 - ADAM_B2) * jnp.square(gv)
        d_ref[...] = -ADAM_LR * ((nm / c1) / (jnp.sqrt(nv / c2) + ADAM_EPS) + ADAM_WD * w_ref[...])
        nm_ref[...] = nm
        nv_ref[...] = nv

    blk = pl.BlockSpec((tr, cols), lambda i: (i, 0))
    sh = jax.ShapeDtypeStruct((rows, cols), F32)
    return pl.pallas_call(
        body, name=name, grid=(rows // tr,), in_specs=[blk] * 4, out_specs=[blk] * 3, out_shape=[sh] * 3,
        compiler_params=_cp(("parallel",)),
    )(w, g, m, v)


WEIGHTS = ("norm_w", "w_in", "ssm_a_re", "ssm_a_im", "ssm_log_dt", "ssm_b_re", "ssm_b_im", "ssm_c_re", "ssm_c_im",
           "ssm_d", "ssm_glu_w", "ssm_glu_b", "sg_ln_w", "sg_ln_b", "sg_w", "sg_b", "attn_sinks",
           "w_branch_a", "w_branch_b", "w_branch_c", "w_out", "final_norm_w")
BIG = ("w_in", "ssm_glu_w", "w_branch_a", "w_branch_b", "w_branch_c", "w_out")
BIG_KEY = {"w_in": ("win_t", True), "ssm_glu_w": ("glu", False), "w_branch_a": ("wba_t", True),
           "w_branch_b": ("wbb_t", True), "w_branch_c": ("wbc_t", True), "w_out": ("wout", False)}
PACK_COLS = 1024


def _pack(arrs):
    flat = jnp.concatenate([a.reshape(-1) for a in arrs])
    rows = -(-flat.shape[0] // (8 * PACK_COLS)) * 8
    return jnp.pad(flat, (0, rows * PACK_COLS - flat.shape[0])).reshape(rows, PACK_COLS)


def _unpack(pack, like):
    flat = pack.reshape(-1)
    out, off = [], 0
    for a in like:
        out.append(flat[off:off + a.size].reshape(a.shape))
        off += a.size
    return out


def kernel(x, norm_w, w_in, ssm_a_re, ssm_a_im, ssm_log_dt, ssm_b_re, ssm_b_im, ssm_c_re, ssm_c_im, ssm_d, ssm_glu_w, ssm_glu_b, sg_ln_w, sg_ln_b, sg_w, sg_b, attn_sinks, w_branch_a, w_branch_b, w_branch_c, w_out, final_norm_w, loss_target, m_norm_w, m_w_in, m_ssm_a_re, m_ssm_a_im, m_ssm_log_dt, m_ssm_b_re, m_ssm_b_im, m_ssm_c_re, m_ssm_c_im, m_ssm_d, m_ssm_glu_w, m_ssm_glu_b, m_sg_ln_w, m_sg_ln_b, m_sg_w, m_sg_b, m_attn_sinks, m_w_branch_a, m_w_branch_b, m_w_branch_c, m_w_out, m_final_norm_w, v_norm_w, v_w_in, v_ssm_a_re, v_ssm_a_im, v_ssm_log_dt, v_ssm_b_re, v_ssm_b_im, v_ssm_c_re, v_ssm_c_im, v_ssm_d, v_ssm_glu_w, v_ssm_glu_b, v_sg_ln_w, v_sg_ln_b, v_sg_w, v_sg_b, v_attn_sinks, v_w_branch_a, v_w_branch_b, v_w_branch_c, v_w_out, v_final_norm_w):
    w = dict(zip(WEIGHTS, (norm_w, w_in, ssm_a_re, ssm_a_im, ssm_log_dt, ssm_b_re, ssm_b_im, ssm_c_re, ssm_c_im, ssm_d, ssm_glu_w, ssm_glu_b, sg_ln_w, sg_ln_b, sg_w, sg_b, attn_sinks, w_branch_a, w_branch_b, w_branch_c, w_out, final_norm_w)))
    m = dict(zip(WEIGHTS, (m_norm_w, m_w_in, m_ssm_a_re, m_ssm_a_im, m_ssm_log_dt, m_ssm_b_re, m_ssm_b_im, m_ssm_c_re, m_ssm_c_im, m_ssm_d, m_ssm_glu_w, m_ssm_glu_b, m_sg_ln_w, m_sg_ln_b, m_sg_w, m_sg_b, m_attn_sinks, m_w_branch_a, m_w_branch_b, m_w_branch_c, m_w_out, m_final_norm_w)))
    v = dict(zip(WEIGHTS, (v_norm_w, v_w_in, v_ssm_a_re, v_ssm_a_im, v_ssm_log_dt, v_ssm_b_re, v_ssm_b_im, v_ssm_c_re, v_ssm_c_im, v_ssm_d, v_ssm_glu_w, v_ssm_glu_b, v_sg_ln_w, v_sg_ln_b, v_sg_w, v_sg_b, v_attn_sinks, v_w_branch_a, v_w_branch_b, v_w_branch_c, v_w_out, v_final_norm_w)))

    order = [(l, n) for l in range(DEPTH) for n in BIG]
    shards = [(w[n][l].T if BIG_KEY[n][1] else w[n][l]).astype(BF16) for l, n in order]
    full = _allgather_rows(shards, "allgather_weights")
    big_w = [{BIG_KEY[n][0]: full[order.index((l, n))] for n in BIG} for l in range(DEPTH)]
    small_p = [{n: w[n][l] for n in SMALL} for l in range(DEPTH)]

    loss_part, dx, dfw, big_g, small_g = _local_step(x[0], loss_target[0], small_p, w["final_norm_w"], big_w)
    loss = lax.psum(loss_part, ("x", "y", "c"))

    red = _reduce_scatter([big_g[l][BIG_KEY[n][0]] for l, n in order])
    grads = {}
    for n in BIG:
        per_layer = [red[order.index((l, n))] for l in range(DEPTH)]
        grads[n] = jnp.stack([g.T if BIG_KEY[n][1] else g for g in per_layer])

    small_names = [n for n in WEIGHTS if n not in BIG]
    small_list = [jnp.stack([small_g[l][n] for l in range(DEPTH)]) if n != "final_norm_w" else dfw for n in small_names]
    red_small = _allreduce_small(_pack(small_list), "allreduce_small")
    for n, g in zip(small_names, _unpack(red_small, small_list)):
        grads[n] = g

    delta, new_m, new_v = {}, {}, {}
    for n in BIG:
        shp = w[n].shape
        two_d = lambda a: a.reshape(-1, shp[-1])
        d_, m_, v_ = _adamw(two_d(w[n]), two_d(grads[n]), two_d(m[n]), two_d(v[n]), f"adamw_{n}")
        delta[n], new_m[n], new_v[n] = d_.reshape(shp), m_.reshape(shp), v_.reshape(shp)
    packs = [_pack([d[n] for n in small_names]) for d in (w, m, v)]
    outs = _adamw(packs[0], red_small, packs[1], packs[2], "adamw_small")
    for res, o in zip((delta, new_m, new_v), outs):
        for n, a in zip(small_names, _unpack(o, small_list)):
            res[n] = a

    return (loss, dx[None], *[grads[n] for n in WEIGHTS], *[delta[n] for n in WEIGHTS],
            *[new_m[n] for n in WEIGHTS], *[new_v[n] for n in WEIGHTS])
```

```python
import functools
import math

import jax
import jax.numpy as jnp
from jax import lax
from jax.experimental import pallas as pl
from jax.experimental.pallas import tpu as pltpu

F32 = jnp.float32
BF16 = jnp.bfloat16

D_MODEL = 2048
DEPTH = 2
EPS = 1e-6
NEG_INF = -1e30
N_DEV = 8

SSM_WIDTH = 1024
SSM_GROUP = 16
SSM_GROUPS = 64
SSM_STATE = 64
N_SLAB = 8
SLAB_CH = 128
SLAB_ST = 512
N_SEG = 8
SEG_PAD = 8

SG_HEADS = 8
CHUNK = 128
HEAD_DIM = 64
ATT_HEADS = 16
ROT_DIM = 16
ROPE_THETA = 500000.0

D_IN = 13568
OFF_UA, OFF_ZA, OFF_UB, OFF_VB, OFF_ZB, OFF_Q, OFF_KV, OFF_ZC, OFF_G = (
    0, 1024, 2048, 3072, 4096, 5120, 6144, 6400, 7424)

ADAM_LR, ADAM_B1, ADAM_B2, ADAM_EPS, ADAM_WD, ADAM_STEP = 0.001, 0.9, 0.999, 1e-08, 0.01, 10

VMEM_LIMIT = 56 * 1024 * 1024


def _cp(sem=None):
    return pltpu.CompilerParams(dimension_semantics=sem, vmem_limit_bytes=VMEM_LIMIT)


def _dot(a, b):
    return jnp.dot(a, b, preferred_element_type=F32)


def _dot_nt(a, b):
    return lax.dot_general(a, b, (((1,), (1,)), ((), ())), preferred_element_type=F32)


def _dot_tn(a, b):
    return lax.dot_general(a, b, (((0,), (0,)), ((), ())), preferred_element_type=F32)


def _mm(a, b, mode, out_dtype, tm, tn, tk, name, res=None):
    if mode == "nn":
        (m, k), (_, n) = a.shape, b.shape
    elif mode == "nt":
        (m, k), (n, _) = a.shape, b.shape
    else:
        (k, m), (_, n) = a.shape, b.shape
    tm, tn, tk = min(tm, m), min(tn, n), min(tk, k)
    assert m % tm == 0 and n % tn == 0 and k % tk == 0, (name, m, n, k, tm, tn, tk)
    nk = k // tk
    a_spec = {"nn": pl.BlockSpec((tm, tk), lambda i, j, kk: (i, kk)),
              "nt": pl.BlockSpec((tm, tk), lambda i, j, kk: (i, kk)),
              "tn": pl.BlockSpec((tk, tm), lambda i, j, kk: (kk, i))}[mode]
    b_spec = {"nn": pl.BlockSpec((tk, tn), lambda i, j, kk: (kk, j)),
              "nt": pl.BlockSpec((tn, tk), lambda i, j, kk: (j, kk)),
              "tn": pl.BlockSpec((tk, tn), lambda i, j, kk: (kk, j))}[mode]
    dot = {"nn": _dot, "nt": _dot_nt, "tn": _dot_tn}[mode]
    has_res = res is not None

    def body(*refs):
        if has_res:
            a_ref, b_ref, r_ref, o_ref, acc = refs
        else:
            a_ref, b_ref, o_ref, acc = refs
        kk = pl.program_id(2)

        @pl.when(kk == 0)
        def _():
            acc[...] = jnp.zeros_like(acc)

        acc[...] += dot(a_ref[...].astype(BF16), b_ref[...].astype(BF16))

        @pl.when(kk == nk - 1)
        def _():
            r = acc[...]
            if has_res:
                r = r + r_ref[...]
            o_ref[...] = r.astype(out_dtype)

    in_specs = [a_spec, b_spec]
    args = [a, b]
    if has_res:
        in_specs.append(pl.BlockSpec((tm, tn), lambda i, j, kk: (i, j)))
        args.append(res)
    return pl.pallas_call(
        body, name=name,
        grid=(m // tm, n // tn, nk),
        in_specs=in_specs,
        out_specs=pl.BlockSpec((tm, tn), lambda i, j, kk: (i, j)),
        out_shape=jax.ShapeDtypeStruct((m, n), out_dtype),
        scratch_shapes=[pltpu.VMEM((tm, tn), F32)],
        compiler_params=_cp(("parallel", "parallel", "arbitrary")),
    )(*args)


def _rms(x, w):
    return x * lax.rsqrt(jnp.mean(x * x, axis=-1, keepdims=True) + EPS) * w


def _rms_fwd(x, w, name):
    L, D = x.shape
    tm = min(L, 256)

    def body(x_ref, w_ref, h_ref):
        h_ref[...] = _rms(x_ref[...], w_ref[...]).astype(BF16)

    return pl.pallas_call(
        body, name=name, grid=(L // tm,),
        in_specs=[pl.BlockSpec((tm, D), lambda i: (i, 0)), pl.BlockSpec((1, D), lambda i: (0, 0))],
        out_specs=pl.BlockSpec((tm, D), lambda i: (i, 0)),
        out_shape=jax.ShapeDtypeStruct((L, D), BF16),
        compiler_params=_cp(("parallel",)),
    )(x, w)


def _rms_bwd(x, w, dh, dres, name):
    L, D = x.shape
    tm = min(L, 256)

    def body(x_ref, w_ref, dh_ref, dres_ref, dx_ref, dw_ref):
        _, vjp = jax.vjp(_rms, x_ref[...], w_ref[...])
        dx, dw = vjp(dh_ref[...])
        dx_ref[...] = dx + dres_ref[...]

        @pl.when(pl.program_id(0) == 0)
        def _():
            dw_ref[...] = jnp.zeros_like(dw_ref)

        dw_ref[...] += dw

    row = pl.BlockSpec((tm, D), lambda i: (i, 0))
    vec = pl.BlockSpec((1, D), lambda i: (0, 0))
    return pl.pallas_call(
        body, name=name, grid=(L // tm,),
        in_specs=[row, vec, row, row],
        out_specs=[row, vec],
        out_shape=[jax.ShapeDtypeStruct((L, D), F32), jax.ShapeDtypeStruct((1, D), F32)],
        compiler_params=_cp(("arbitrary",)),
    )(x, w, dh, dres)


def _final(x, fw, tgt, name):
    L, D = x.shape
    tm = min(L, 256)

    def loss_fn(xv, wv, tv):
        err = _rms(xv, wv) - tv
        return jnp.sum(err * err) * (0.5 / D)

    def body(x_ref, w_ref, t_ref, loss_ref, dx_ref, dw_ref):
        tv = t_ref[...]
        val, vjp = jax.vjp(lambda a, b: loss_fn(a, b, tv), x_ref[...], w_ref[...])
        dx, dw = vjp(jnp.ones((), F32))
        dx_ref[...] = dx

        @pl.when(pl.program_id(0) == 0)
        def _():
            dw_ref[...] = jnp.zeros_like(dw_ref)
            loss_ref[...] = jnp.zeros_like(loss_ref)

        dw_ref[...] += dw
        loss_ref[...] += jnp.full(loss_ref.shape, val, F32)

    row = pl.BlockSpec((tm, D), lambda i: (i, 0))
    vec = pl.BlockSpec((1, D), lambda i: (0, 0))
    return pl.pallas_call(
        body, name=name, grid=(L // tm,),
        in_specs=[row, vec, row],
        out_specs=[pl.BlockSpec((8, 128), lambda i: (0, 0)), row, vec],
        out_shape=[jax.ShapeDtypeStruct((8, 128), F32), jax.ShapeDtypeStruct((L, D), F32),
                   jax.ShapeDtypeStruct((1, D), F32)],
        compiler_params=_cp(("arbitrary",)),
    )(x, fw, tgt)


def _s5_param_fn(a_re, a_im, log_dt, bt_re, bt_im):
    dt = jnp.exp(log_dt)
    zr, zi = a_re * dt, a_im * dt
    er = jnp.exp(zr)
    lr, li = er * jnp.cos(zi), er * jnp.sin(zi)
    nr, ni = lr - 1.0, li
    den = a_re * a_re + a_im * a_im
    cr = (nr * a_re + ni * a_im) / den
    ci = (ni * a_re - nr * a_im) / den
    bbr = cr[None] * bt_re - ci[None] * bt_im
    bbi = cr[None] * bt_im + ci[None] * bt_re
    return lr, li, bbr, bbi


def _s5_params_fwd(a_re, a_im, log_dt, bt_re, bt_im, name):
    def body(ar, ai, ld, br, bi, lr, li, bbr, bbi):
        o = _s5_param_fn(ar[...], ai[...], ld[...], br[...], bi[...])
        lr[...], li[...], bbr[...], bbi[...] = o

    gp = jax.ShapeDtypeStruct(a_re.shape, F32)
    cgp = jax.ShapeDtypeStruct(bt_re.shape, F32)
    return pl.pallas_call(body, name=name, out_shape=[gp, gp, cgp, cgp])(a_re, a_im, log_dt, bt_re, bt_im)


def _s5_params_bwd(a_re, a_im, log_dt, bt_re, bt_im, dlr, dli, dbbr, dbbi, name):
    def body(ar, ai, ld, br, bi, g0, g1, g2, g3, o0, o1, o2, o3, o4):
        _, vjp = jax.vjp(_s5_param_fn, ar[...], ai[...], ld[...], br[...], bi[...])
        o0[...], o1[...], o2[...], o3[...], o4[...] = vjp((g0[...], g1[...], g2[...], g3[...]))

    gp = jax.ShapeDtypeStruct(a_re.shape, F32)
    cgp = jax.ShapeDtypeStruct(bt_re.shape, F32)
    return pl.pallas_call(body, name=name,
                          out_shape=[gp, gp, jax.ShapeDtypeStruct(log_dt.shape, F32), cgp, cgp])(
        a_re, a_im, log_dt, bt_re, bt_im, dlr, dli, dbbr, dbbi)


def _cmul(ar, ai, br, bi):
    return ar * br - ai * bi, ar * bi + ai * br


def _cpow(lr, li, n):
    rr, ri = None, None
    br, bi = lr, li
    while n:
        if n & 1:
            rr, ri = (br, bi) if rr is None else _cmul(rr, ri, br, bi)
        n >>= 1
        if n:
            br, bi = _cmul(br, bi, br, bi)
    return rr, ri


def _shift_rows(x, up):
    row = lax.broadcasted_iota(jnp.int32, x.shape, 0)
    if up:
        return jnp.where(row == N_SEG - 1, 0.0, pltpu.roll(x, N_SEG - 1, 0))
    return jnp.where(row == 0, 0.0, pltpu.roll(x, 1, 0))


def _seg_scan(s_re, s_im, lam, seg, reverse):
    stride = seg + SEG_PAD
    nt = SLAB_ST // 128
    lam_t = [(jnp.broadcast_to(lam[0][:, j * 128:(j + 1) * 128], (N_SEG, 128)),
              jnp.broadcast_to(lam[1][:, j * 128:(j + 1) * 128], (N_SEG, 128))) for j in range(nt)]

    def rows(i):
        return pl.ds(i, N_SEG, stride=stride)

    def step1(t, carry):
        i = seg - 1 - t if reverse else t
        out = []
        for j in range(nt):
            cr, ci = carry[2 * j], carry[2 * j + 1]
            nr, ni = _cmul(lam_t[j][0], lam_t[j][1], cr, ci)
            nr = nr + s_re[j, rows(i), :]
            ni = ni + s_im[j, rows(i), :]
            s_re[j, rows(i), :] = nr
            s_im[j, rows(i), :] = ni
            out += [nr, ni]
        return tuple(out)

    zero = tuple(jnp.zeros((N_SEG, 128), F32) for _ in range(2 * nt))
    ends = lax.fori_loop(0, seg, step1, zero)

    carries = []
    for j in range(nt):
        pr, pi = _cpow(lam_t[j][0], lam_t[j][1], seg)
        cr, ci = jnp.zeros((N_SEG, 128), F32), jnp.zeros((N_SEG, 128), F32)
        for _ in range(N_SEG - 1):
            tr, ti = _cmul(pr, pi, cr, ci)
            cr = _shift_rows(tr + ends[2 * j], reverse)
            ci = _shift_rows(ti + ends[2 * j + 1], reverse)
        carries += [cr, ci]

    def step2(t, pw):
        i = seg - 1 - t if reverse else t
        out = []
        for j in range(nt):
            pr, pi = pw[2 * j], pw[2 * j + 1]
            ar, ai = _cmul(pr, pi, carries[2 * j], carries[2 * j + 1])
            s_re[j, rows(i), :] = s_re[j, rows(i), :] + ar
            s_im[j, rows(i), :] = s_im[j, rows(i), :] + ai
            qr, qi = _cmul(pr, pi, lam_t[j][0], lam_t[j][1])
            out += [qr, qi]
        return tuple(out)

    lax.fori_loop(0, seg, step2, tuple(x for j in range(nt) for x in lam_t[j]))
    return carries


def _seg_rows(ref, k, seg):
    stride = seg + SEG_PAD
    return jnp.concatenate([ref[j, pl.ds(k * stride, seg), :] for j in range(SLAB_ST // 128)], axis=-1)


def _seg_store(ref, k, seg, val):
    stride = seg + SEG_PAD
    for j in range(SLAB_ST // 128):
        ref[j, pl.ds(k * stride, seg), :] = val[:, j * 128:(j + 1) * 128]


def _s5_specs(L):
    col = lambda off: pl.BlockSpec((L, SLAB_CH), lambda j: (0, off + j))
    mat_b = pl.BlockSpec((None, SLAB_CH, SLAB_ST), lambda j: (j, 0, 0))
    mat_c = pl.BlockSpec((None, SLAB_ST, SLAB_CH), lambda j: (j, 0, 0))
    vec_s = pl.BlockSpec((None, 1, SLAB_ST), lambda j: (j, 0, 0))
    vec_c = pl.BlockSpec((None, 1, SLAB_CH), lambda j: (j, 0, 0))
    return col, mat_b, mat_c, vec_s, vec_c


def _s5_states(u_ref, bre_ref, bim_ref, lam, s_re, s_im, seg):
    for k in range(N_SEG):
        uk = u_ref[pl.ds(k * seg, seg), :]
        _seg_store(s_re, k, seg, _dot(uk, bre_ref[...]))
        _seg_store(s_im, k, seg, _dot(uk, bim_ref[...]))
    return _seg_scan(s_re, s_im, lam, seg, reverse=False)


def _s5_fwd(proj, bre, bim, cre_t, cim_t, lam_re, lam_im, dvec, name):
    L = proj.shape[0]
    seg = L // N_SEG
    col, mat_b, mat_c, vec_s, vec_c = _s5_specs(L)
    rows = N_SEG * (seg + SEG_PAD)

    def body(u_ref, bre_ref, bim_ref, cre_ref, cim_ref, lr_ref, li_ref, d_ref, y_ref, s_re, s_im):
        _s5_states(u_ref, bre_ref, bim_ref, (lr_ref[...], li_ref[...]), s_re, s_im, seg)
        for k in range(N_SEG):
            y = (_dot(_seg_rows(s_re, k, seg).astype(BF16), cre_ref[...])
                 - _dot(_seg_rows(s_im, k, seg).astype(BF16), cim_ref[...]))
            y = y + d_ref[...] * u_ref[pl.ds(k * seg, seg), :].astype(F32)
            y_ref[pl.ds(k * seg, seg), :] = jax.nn.gelu(y).astype(BF16)

    return pl.pallas_call(
        body, name=name, grid=(N_SLAB,),
        in_specs=[col(OFF_UA // SLAB_CH), mat_b, mat_b, mat_c, mat_c, vec_s, vec_s, vec_c],
        out_specs=pl.BlockSpec((L, SLAB_CH), lambda j: (0, j)),
        out_shape=jax.ShapeDtypeStruct((L, SSM_WIDTH), BF16),
        scratch_shapes=[pltpu.VMEM((SLAB_ST // 128, rows, 128), F32)] * 2,
        compiler_params=_cp(("parallel",)),
    )(proj, bre, bim, cre_t, cim_t, lam_re, lam_im, dvec)


def _s5_bwd(proj, dy, bre, bim, cre_t, cim_t, lam_re, lam_im, dvec, name):
    L = proj.shape[0]
    seg = L // N_SEG
    stride = seg + SEG_PAD
    col, mat_b, mat_c, vec_s, vec_c = _s5_specs(L)
    rows = N_SEG * stride
    nt = SLAB_ST // 128

    def body(u_ref, dy_ref, bre_ref, bim_ref, cre_ref, cim_ref, lr_ref, li_ref, d_ref,
             du_ref, dbre_ref, dbim_ref, dcre_ref, dcim_ref, dlr_ref, dli_ref, dd_ref,
             s_re, s_im, a_re, a_im, dyp):
        lam = (lr_ref[...], li_ref[...])
        carry_s = _s5_states(u_ref, bre_ref, bim_ref, lam, s_re, s_im, seg)
        dcre = jnp.zeros((SLAB_ST, SLAB_CH), F32)
        dcim = jnp.zeros((SLAB_ST, SLAB_CH), F32)
        dd = jnp.zeros((1, SLAB_CH), F32)
        for k in range(N_SEG):
            sre = _seg_rows(s_re, k, seg).astype(BF16)
            sim = _seg_rows(s_im, k, seg).astype(BF16)
            uk = u_ref[pl.ds(k * seg, seg), :].astype(F32)
            ypre = _dot(sre, cre_ref[...]) - _dot(sim, cim_ref[...]) + d_ref[...] * uk
            _, vjp = jax.vjp(jax.nn.gelu, ypre)
            (dyk,) = vjp(dy_ref[pl.ds(k * seg, seg), :].astype(F32))
            dyp[pl.ds(k * seg, seg), :] = dyk
            dd = dd + jnp.sum(dyk * uk, axis=0, keepdims=True)
            dyb = dyk.astype(BF16)
            dcre = dcre + _dot_tn(sre, dyb)
            dcim = dcim - _dot_tn(sim, dyb)
            _seg_store(a_re, k, seg, _dot_nt(dyb, cre_ref[...]))
            _seg_store(a_im, k, seg, -_dot_nt(dyb, cim_ref[...]))
        dcre_ref[...] = dcre
        dcim_ref[...] = dcim
        dd_ref[...] = dd

        _seg_scan(a_re, a_im, (lam[0], -lam[1]), seg, reverse=True)

        def acc_dlam(i, acc):
            out = []
            for j in range(nt):
                ar = a_re[j, pl.ds(i, N_SEG, stride=stride), :]
                ai = a_im[j, pl.ds(i, N_SEG, stride=stride), :]
                pr = s_re[j, pl.ds(i - 1, N_SEG, stride=stride), :]
                pi = s_im[j, pl.ds(i - 1, N_SEG, stride=stride), :]
                out += [acc[2 * j] + ar * pr + ai * pi, acc[2 * j + 1] + ai * pr - ar * pi]
            return tuple(out)

        first = []
        for j in range(nt):
            ar = a_re[j, pl.ds(0, N_SEG, stride=stride), :]
            ai = a_im[j, pl.ds(0, N_SEG, stride=stride), :]
            pr, pi = carry_s[2 * j], carry_s[2 * j + 1]
            first += [ar * pr + ai * pi, ai * pr - ar * pi]
        acc = lax.fori_loop(1, seg, acc_dlam, tuple(first))
        dlr_ref[...] = jnp.concatenate([jnp.sum(acc[2 * j], axis=0, keepdims=True) for j in range(nt)], axis=-1)
        dli_ref[...] = jnp.concatenate([jnp.sum(acc[2 * j + 1], axis=0, keepdims=True) for j in range(nt)], axis=-1)

        dbre = jnp.zeros((SLAB_CH, SLAB_ST), F32)
        dbim = jnp.zeros((SLAB_CH, SLAB_ST), F32)
        for k in range(N_SEG):
            are = _seg_rows(a_re, k, seg).astype(BF16)
            aim = _seg_rows(a_im, k, seg).astype(BF16)
            uk = u_ref[pl.ds(k * seg, seg), :]
            du = _dot_nt(are, bre_ref[...]) + _dot_nt(aim, bim_ref[...]) + dyp[pl.ds(k * seg, seg), :] * d_ref[...]
            du_ref[pl.ds(k * seg, seg), :] = du.astype(BF16)
            dbre = dbre + _dot_tn(uk, are)
            dbim = dbim + _dot_tn(uk, aim)
        dbre_ref[...] = dbre
        dbim_ref[...] = dbim

    scan_buf = pltpu.VMEM((nt, rows, 128), F32)
    return pl.pallas_call(
        body, name=name, grid=(N_SLAB,),
        in_specs=[col(OFF_UA // SLAB_CH), pl.BlockSpec((L, SLAB_CH), lambda j: (0, j)),
                  mat_b, mat_b, mat_c, mat_c, vec_s, vec_s, vec_c],
        out_specs=[pl.BlockSpec((L, SLAB_CH), lambda j: (0, j)), mat_b, mat_b, mat_c, mat_c, vec_s, vec_s, vec_c],
        out_shape=[jax.ShapeDtypeStruct((L, SSM_WIDTH), BF16),
                   jax.ShapeDtypeStruct((N_SLAB, SLAB_CH, SLAB_ST), F32),
                   jax.ShapeDtypeStruct((N_SLAB, SLAB_CH, SLAB_ST), F32),
                   jax.ShapeDtypeStruct((N_SLAB, SLAB_ST, SLAB_CH), F32),
                   jax.ShapeDtypeStruct((N_SLAB, SLAB_ST, SLAB_CH), F32),
                   jax.ShapeDtypeStruct((N_SLAB, 1, SLAB_ST), F32),
                   jax.ShapeDtypeStruct((N_SLAB, 1, SLAB_ST), F32),
                   jax.ShapeDtypeStruct((N_SLAB, 1, SLAB_CH), F32)],
        scratch_shapes=[scan_buf, scan_buf, scan_buf, scan_buf, pltpu.VMEM((L, SLAB_CH), F32)],
        compiler_params=_cp(("parallel",)),
    )(proj, dy, bre, bim, cre_t, cim_t, lam_re, lam_im, dvec)


def _glu_point(y0, pre, za, b):
    return y0 * jax.nn.sigmoid(pre + b) * jax.nn.silu(za)


def _glu_specs(L, tm):
    row = pl.BlockSpec((tm, SSM_WIDTH), lambda i: (i, 0))
    za = pl.BlockSpec((tm, SSM_WIDTH), lambda i: (i, OFF_ZA // SSM_WIDTH))
    wmat = pl.BlockSpec((SSM_WIDTH, SSM_WIDTH), lambda i: (0, 0))
    vec = pl.BlockSpec((1, SSM_WIDTH), lambda i: (0, 0))
    return row, za, wmat, vec


def _glu_fwd(ya0, proj, w, b, name):
    L = ya0.shape[0]
    tm = min(L, 512)
    row, za, wmat, vec = _glu_specs(L, tm)

    def body(y_ref, z_ref, w_ref, b_ref, o_ref):
        y0 = y_ref[...]
        pre = _dot(y0, w_ref[...])
        o_ref[...] = _glu_point(y0.astype(F32), pre, z_ref[...].astype(F32), b_ref[...]).astype(BF16)

    return pl.pallas_call(
        body, name=name, grid=(L // tm,), in_specs=[row, za, wmat, vec], out_specs=row,
        out_shape=jax.ShapeDtypeStruct((L, SSM_WIDTH), BF16), compiler_params=_cp(("parallel",)),
    )(ya0, proj, w, b)


def _glu_bwd(ya0, proj, w, b, dya, name):
    L = ya0.shape[0]
    tm = min(L, 512)
    row, za, wmat, vec = _glu_specs(L, tm)

    def body(y_ref, z_ref, w_ref, b_ref, g_ref, dy0_ref, dza_ref, dw_ref, db_ref):
        y0 = y_ref[...]
        pre = _dot(y0, w_ref[...])
        _, vjp = jax.vjp(_glu_point, y0.astype(F32), pre, z_ref[...].astype(F32), b_ref[...])
        dy0, dpre, dza, db = vjp(g_ref[...].astype(F32))
        dpb = dpre.astype(BF16)
        dy0_ref[...] = (dy0 + _dot_nt(dpb, w_ref[...])).astype(BF16)
        dza_ref[...] = dza.astype(BF16)

        @pl.when(pl.program_id(0) == 0)
        def _():
            dw_ref[...] = jnp.zeros_like(dw_ref)
            db_ref[...] = jnp.zeros_like(db_ref)

        dw_ref[...] += _dot_tn(y0, dpb)
        db_ref[...] += db

    return pl.pallas_call(
        body, name=name, grid=(L // tm,), in_specs=[row, za, wmat, vec, row],
        out_specs=[row, row, wmat, vec],
        out_shape=[jax.ShapeDtypeStruct((L, SSM_WIDTH), BF16), jax.ShapeDtypeStruct((L, SSM_WIDTH), BF16),
                   jax.ShapeDtypeStruct((SSM_WIDTH, SSM_WIDTH), F32), jax.ShapeDtypeStruct((1, SSM_WIDTH), F32)],
        compiler_params=_cp(("arbitrary",)),
    )(ya0, proj, w, b, dya)


def _sg_norm(vb, ln_w, ln_b):
    v0 = jax.nn.gelu(vb)
    mu = jnp.mean(v0, axis=-1, keepdims=True)
    var = jnp.mean(jnp.square(v0 - mu), axis=-1, keepdims=True)
    return (v0 - mu) * lax.rsqrt(var + EPS) * ln_w + ln_b


def _sg_gate(ub, mixed, zb):
    return jax.nn.gelu(ub) * mixed * jax.nn.silu(zb)


def _sg_specs():
    W = SSM_WIDTH
    blk = lambda off: pl.BlockSpec((CHUNK, W), lambda n: (n, off // W))
    out = pl.BlockSpec((CHUNK, W), lambda n: (n, 0))
    vec = pl.BlockSpec((1, W), lambda n: (0, 0))
    wsp = pl.BlockSpec((SG_HEADS, CHUNK, CHUNK), lambda n: (0, 0, 0))
    bsp = pl.BlockSpec((SG_HEADS, CHUNK, 1), lambda n: (0, 0, 0))
    return blk, out, vec, wsp, bsp


def _sg_masked(w_ref):
    t = lax.broadcasted_iota(jnp.int32, (CHUNK, CHUNK), 0)
    s = lax.broadcasted_iota(jnp.int32, (CHUNK, CHUNK), 1)
    causal = s <= t
    return causal, [jnp.where(causal, w_ref[h], 0.0).astype(BF16) for h in range(SG_HEADS)]


def _sg_mix(wm, vnb, bias_ref):
    return jnp.concatenate(
        [_dot(wm[h], vnb[:, h * CHUNK:(h + 1) * CHUNK]) + bias_ref[h] for h in range(SG_HEADS)], axis=-1)


def _sg_fwd(proj, ln_w, ln_b, w, bias, name):
    L = proj.shape[0]
    blk, out, vec, wsp, bsp = _sg_specs()

    def body(ub_ref, vb_ref, zb_ref, lw_ref, lb_ref, w_ref, bias_ref, o_ref):
        _, wm = _sg_masked(w_ref)
        vnb = _sg_norm(vb_ref[...].astype(F32), lw_ref[...], lb_ref[...]).astype(BF16)
        mixed = _sg_mix(wm, vnb, bias_ref)
        o_ref[...] = _sg_gate(ub_ref[...].astype(F32), mixed, zb_ref[...].astype(F32)).astype(BF16)

    return pl.pallas_call(
        body, name=name, grid=(L // CHUNK,),
        in_specs=[blk(OFF_UB), blk(OFF_VB), blk(OFF_ZB), vec, vec, wsp, bsp], out_specs=out,
        out_shape=jax.ShapeDtypeStruct((L, SSM_WIDTH), BF16), compiler_params=_cp(("parallel",)),
    )(proj, proj, proj, ln_w, ln_b, w, bias)


def _sg_bwd(proj, ln_w, ln_b, w, bias, dyb, name):
    L = proj.shape[0]
    blk, out, vec, wsp, bsp = _sg_specs()

    def body(ub_ref, vb_ref, zb_ref, lw_ref, lb_ref, w_ref, bias_ref, g_ref,
             dub_ref, dvb_ref, dzb_ref, dlw_ref, dlb_ref, dw_ref, dbias_ref):
        causal, wm = _sg_masked(w_ref)
        vb = vb_ref[...].astype(F32)
        vn, vjp_norm = jax.vjp(_sg_norm, vb, lw_ref[...], lb_ref[...])
        vnb = vn.astype(BF16)
        mixed = _sg_mix(wm, vnb, bias_ref)
        _, vjp_gate = jax.vjp(_sg_gate, ub_ref[...].astype(F32), mixed, zb_ref[...].astype(F32))
        dub, dmixed, dzb = vjp_gate(g_ref[...].astype(F32))
        dub_ref[...] = dub.astype(BF16)
        dzb_ref[...] = dzb.astype(BF16)

        @pl.when(pl.program_id(0) == 0)
        def _():
            dlw_ref[...] = jnp.zeros_like(dlw_ref)
            dlb_ref[...] = jnp.zeros_like(dlb_ref)
            dw_ref[...] = jnp.zeros_like(dw_ref)
            dbias_ref[...] = jnp.zeros_like(dbias_ref)

        dvn = []
        for h in range(SG_HEADS):
            dm = dmixed[:, h * CHUNK:(h + 1) * CHUNK]
            dmb = dm.astype(BF16)
            dbias_ref[h] += jnp.sum(dm, axis=-1, keepdims=True)
            dw_ref[h] += jnp.where(causal, _dot_nt(dmb, vnb[:, h * CHUNK:(h + 1) * CHUNK]), 0.0)
            dvn.append(_dot_tn(wm[h], dmb))
        dvb, dlw, dlb = vjp_norm(jnp.concatenate(dvn, axis=-1))
        dvb_ref[...] = dvb.astype(BF16)
        dlw_ref[...] += dlw
        dlb_ref[...] += dlb

    act = jax.ShapeDtypeStruct((L, SSM_WIDTH), BF16)
    return pl.pallas_call(
        body, name=name, grid=(L // CHUNK,),
        in_specs=[blk(OFF_UB), blk(OFF_VB), blk(OFF_ZB), vec, vec, wsp, bsp, out],
        out_specs=[out, out, out, vec, vec, wsp, bsp],
        out_shape=[act, act, act, jax.ShapeDtypeStruct((1, SSM_WIDTH), F32), jax.ShapeDtypeStruct((1, SSM_WIDTH), F32),
                   jax.ShapeDtypeStruct((SG_HEADS, CHUNK, CHUNK), F32), jax.ShapeDtypeStruct((SG_HEADS, CHUNK, 1), F32)],
        compiler_params=_cp(("arbitrary",)),
    )(proj, proj, proj, ln_w, ln_b, w, bias, dyb)


def _rope_tables(L):
    half = ROT_DIM // 2
    inv_freq = ROPE_THETA ** (-jnp.arange(0, ROT_DIM, 2, dtype=F32) / ROT_DIM)
    ang = jnp.arange(L, dtype=F32)[:, None] * inv_freq[None, :]
    cos, sin = jnp.cos(ang), jnp.sin(ang)
    ones = jnp.ones((L, HEAD_DIM - ROT_DIM), F32)
    cos_h = jnp.concatenate([cos, cos, ones], axis=-1)
    sin_h = jnp.concatenate([-sin, sin, 0.0 * ones], axis=-1)
    src = jnp.arange(HEAD_DIM)[:, None]
    dst = jnp.arange(HEAD_DIM)[None, :]
    p_h = (((dst < half) & (src == dst + half)) | ((dst >= half) & (dst < ROT_DIM) & (src == dst - half))).astype(F32)
    p2 = jnp.kron(jnp.eye(2, dtype=F32), p_h).astype(BF16)
    return jnp.tile(cos_h, (1, 2)), jnp.tile(sin_h, (1, 2)), p2


def _rope(t, cos, sin, p2):
    n = t.shape[1] // 128
    tb = t.astype(BF16)
    sw = jnp.concatenate([_dot(tb[:, i * 128:(i + 1) * 128], p2) for i in range(n)], axis=-1) if n > 1 else _dot(tb, p2)
    return t * jnp.tile(cos, (1, n)) + sw * jnp.tile(sin, (1, n))


def _rope_t(g, cos, sin, p2):
    n = g.shape[1] // 128
    gs = (g * jnp.tile(sin, (1, n))).astype(BF16)
    sw = jnp.concatenate([_dot_nt(gs[:, i * 128:(i + 1) * 128], p2) for i in range(n)], axis=-1) if n > 1 else _dot_nt(gs, p2)
    return g * jnp.tile(cos, (1, n)) + sw


def _lane_lo(shape):
    return (lax.broadcasted_iota(jnp.int32, shape, len(shape) - 1) % 128) < HEAD_DIM


def _dup_halves(x):
    xr = pltpu.roll(x, HEAD_DIM, 1)
    lo = _lane_lo(x.shape)
    return jnp.where(lo, x, xr), jnp.where(lo, xr, x)


def _fold_halves(d0, d1):
    f0 = d0 + pltpu.roll(d0, HEAD_DIM, 1)
    f1 = d1 + pltpu.roll(d1, HEAD_DIM, 1)
    return jnp.where(_lane_lo(d0.shape), f0, f1)


def _attn_mask():
    qi = lax.broadcasted_iota(jnp.int32, (CHUNK, 2 * CHUNK), 0)
    kj = lax.broadcasted_iota(jnp.int32, (CHUNK, 2 * CHUNK), 1)
    return qi, kj


def _attn_specs():
    qsp = pl.BlockSpec((CHUNK, 1024), lambda n: (n, OFF_Q // 1024))
    kv_cur = pl.BlockSpec((CHUNK, 256), lambda n: (n, OFF_KV // 256))
    kv_prev = pl.BlockSpec((CHUNK, 256), lambda n: (jnp.maximum(n - 1, 0), OFF_KV // 256))
    zsp = [pl.BlockSpec((CHUNK, 256), functools.partial(lambda n, q: (n, OFF_ZC // 256 + q), q=q)) for q in range(4)]
    tab_cur = pl.BlockSpec((CHUNK, 128), lambda n: (n, 0))
    tab_prev = pl.BlockSpec((CHUNK, 128), lambda n: (jnp.maximum(n - 1, 0), 0))
    p2sp = pl.BlockSpec((128, 128), lambda n: (0, 0))
    sink = pl.BlockSpec(memory_space=pltpu.SMEM)
    wide = pl.BlockSpec((CHUNK, 1024), lambda n: (n, 0))
    return qsp, kv_cur, kv_prev, zsp, tab_cur, tab_prev, p2sp, sink, wide


def _attn_core(n, q_ref, kvc_ref, kvp_ref, cosc_ref, sinc_ref, cosp_ref, sinp_ref, p2_ref, sink_ref):
    p2 = p2_ref[...]
    qr = _rope(q_ref[...].astype(F32), cosc_ref[...], sinc_ref[...], p2).astype(BF16)
    kc = _rope(kvc_ref[:, 0:128].astype(F32), cosc_ref[...], sinc_ref[...], p2)
    kp = _rope(kvp_ref[:, 0:128].astype(F32), cosp_ref[...], sinp_ref[...], p2)
    k_all = jnp.concatenate([kp, kc], axis=0).astype(BF16)
    v_all = jnp.concatenate([kvp_ref[:, 128:256], kvc_ref[:, 128:256]], axis=0)
    kd = _dup_halves(k_all)
    vd = _dup_halves(v_all)
    qi, kj = _attn_mask()
    allowed = ((kj < CHUNK) & (kj > qi) & (n > 0)) | ((kj >= CHUNK) & (kj - CHUNK <= qi))
    lo = _lane_lo((CHUNK, 128))
    probs = []
    for h in range(ATT_HEADS):
        m, half, g = h // 2, h % 2, h // 8
        qp = qr[:, m * 128:(m + 1) * 128]
        qm = jnp.where(lo if half == 0 else ~lo, qp, jnp.zeros_like(qp))
        s = jnp.where(allowed, _dot_nt(qm, kd[g]) * (HEAD_DIM ** -0.5), NEG_INF)
        snk = sink_ref[h]
        mx = jnp.maximum(jnp.max(s, axis=-1, keepdims=True), snk)
        e = jnp.exp(s - mx)
        es = jnp.exp(snk - mx)
        inv = 1.0 / (jnp.sum(e, axis=-1, keepdims=True) + es)
        probs.append((qm, e * inv, es * inv))
    return qr, kd, vd, probs, lo


def _attn_out(vd, probs, lo):
    outs = []
    for m in range(ATT_HEADS // 2):
        g = m // 4
        o0 = _dot(probs[2 * m][1].astype(BF16), vd[g])
        o1 = _dot(probs[2 * m + 1][1].astype(BF16), vd[g])
        outs.append(jnp.where(lo, o0, o1))
    return jnp.concatenate(outs, axis=-1)


def _silu_gate(o, z):
    return o * jax.nn.silu(z)


def _attn_fwd(proj, sinks, tabs, name):
    L = proj.shape[0]
    cos2, sin2, p2 = tabs
    qsp, kv_cur, kv_prev, zsp, tab_cur, tab_prev, p2sp, sink, wide = _attn_specs()

    def body(q_ref, kvc_ref, kvp_ref, z0, z1, z2, z3, cosc, sinc, cosp, sinp, p2_ref, sink_ref, o_ref):
        n = pl.program_id(0)
        _, _, vd, probs, lo = _attn_core(n, q_ref, kvc_ref, kvp_ref, cosc, sinc, cosp, sinp, p2_ref, sink_ref)
        o = _attn_out(vd, probs, lo)
        z = jnp.concatenate([z0[...], z1[...], z2[...], z3[...]], axis=-1).astype(F32)
        o_ref[...] = _silu_gate(o, z).astype(BF16)

    return pl.pallas_call(
        body, name=name, grid=(L // CHUNK,),
        in_specs=[qsp, kv_cur, kv_prev, *zsp, tab_cur, tab_cur, tab_prev, tab_prev, p2sp, sink],
        out_specs=wide, out_shape=jax.ShapeDtypeStruct((L, 1024), BF16), compiler_params=_cp(("parallel",)),
    )(proj, proj, proj, proj, proj, proj, proj, cos2, sin2, cos2, sin2, p2, sinks)


def _attn_bwd(proj, sinks, tabs, dyc, name):
    L = proj.shape[0]
    cos2, sin2, p2 = tabs
    qsp, kv_cur, kv_prev, zsp, tab_cur, tab_prev, p2sp, sink, wide = _attn_specs()
    kvo = pl.BlockSpec((CHUNK, 256), lambda n: (n, 0))

    def body(q_ref, kvc_ref, kvp_ref, z0, z1, z2, z3, cosc, sinc, cosp, sinp, p2_ref, sink_ref, g_ref,
             dq_ref, dz_ref, dkvc_ref, dkvp_ref, dsink_ref):
        n = pl.program_id(0)
        _, kd, vd, probs, lo = _attn_core(n, q_ref, kvc_ref, kvp_ref, cosc, sinc, cosp, sinp, p2_ref, sink_ref)
        o = _attn_out(vd, probs, lo)
        z = jnp.concatenate([z0[...], z1[...], z2[...], z3[...]], axis=-1).astype(F32)
        _, vjp = jax.vjp(_silu_gate, o, z)
        do, dz = vjp(g_ref[...].astype(F32))
        dz_ref[...] = dz.astype(BF16)

        @pl.when(n == 0)
        def _():
            dsink_ref[...] = jnp.zeros_like(dsink_ref)

        dkd = [jnp.zeros((2 * CHUNK, 128), F32), jnp.zeros((2 * CHUNK, 128), F32)]
        dvd = [jnp.zeros((2 * CHUNK, 128), F32), jnp.zeros((2 * CHUNK, 128), F32)]
        dq_pairs = []
        for m in range(ATT_HEADS // 2):
            g = m // 4
            dop = do[:, m * 128:(m + 1) * 128].astype(BF16)
            dq_h = []
            for half in range(2):
                h = 2 * m + half
                qm, p, ps = probs[h]
                dom = jnp.where(lo if half == 0 else ~lo, dop, jnp.zeros_like(dop))
                dp = _dot_nt(dom, vd[g])
                rs = jnp.sum(p * dp, axis=-1, keepdims=True)
                ds = (p * (dp - rs) * (HEAD_DIM ** -0.5)).astype(BF16)
                dsink_ref[h:h + 1, :] += jnp.broadcast_to(jnp.sum(-ps * rs, axis=0, keepdims=True), (1, 128))
                dq_h.append(_dot(ds, kd[g]))
                dkd[g] = dkd[g] + _dot_tn(ds, qm)
                dvd[g] = dvd[g] + _dot_tn(p.astype(BF16), dom)
            dq_pairs.append(jnp.where(lo, dq_h[0], dq_h[1]))
        p2 = p2_ref[...]
        dq_ref[...] = _rope_t(jnp.concatenate(dq_pairs, axis=-1), cosc[...], sinc[...], p2).astype(BF16)
        dk_rot = _fold_halves(dkd[0], dkd[1])
        dv = _fold_halves(dvd[0], dvd[1])
        dkp = _rope_t(dk_rot[0:CHUNK], cosp[...], sinp[...], p2)
        dkc = _rope_t(dk_rot[CHUNK:2 * CHUNK], cosc[...], sinc[...], p2)
        dkvp_ref[...] = jnp.concatenate([dkp, dv[0:CHUNK]], axis=-1)
        dkvc_ref[...] = jnp.concatenate([dkc, dv[CHUNK:2 * CHUNK]], axis=-1)

    act = jax.ShapeDtypeStruct((L, 1024), BF16)
    kvs = jax.ShapeDtypeStruct((L, 256), F32)
    return pl.pallas_call(
        body, name=name, grid=(L // CHUNK,),
        in_specs=[qsp, kv_cur, kv_prev, *zsp, tab_cur, tab_cur, tab_prev, tab_prev, p2sp, sink, wide],
        out_specs=[wide, wide, kvo, kvo, pl.BlockSpec((ATT_HEADS, 128), lambda n: (0, 0))],
        out_shape=[act, act, kvs, kvs, jax.ShapeDtypeStruct((ATT_HEADS, 128), F32)],
        compiler_params=_cp(("arbitrary",)),
    )(proj, proj, proj, proj, proj, proj, proj, cos2, sin2, cos2, sin2, p2, sinks, dyc)


MERGE_TN = 256


def _merge_point(ta, tb, tc, ga, gb, gc):
    return jax.nn.sigmoid(ga) * ta + jax.nn.sigmoid(gb) * tb + jax.nn.sigmoid(gc) * tc


def _merge_specs(tm):
    nj = D_MODEL // MERGE_TN
    t = pl.BlockSpec((tm, MERGE_TN), lambda i, j: (i, j))
    gates = [pl.BlockSpec((tm, MERGE_TN), functools.partial(lambda i, j, b: (i, OFF_G // MERGE_TN + b * nj + j), b=b))
             for b in range(3)]
    return t, gates, nj


def _merge_fwd(ta, tb, tc, proj, name):
    L = ta.shape[0]
    tm = min(L, 1024)
    t, gates, nj = _merge_specs(tm)

    def body(ta_ref, tb_ref, tc_ref, ga_ref, gb_ref, gc_ref, o_ref):
        f = lambda r: r[...].astype(F32)
        o_ref[...] = _merge_point(f(ta_ref), f(tb_ref), f(tc_ref), f(ga_ref), f(gb_ref), f(gc_ref)).astype(BF16)

    return pl.pallas_call(
        body, name=name, grid=(L // tm, nj), in_specs=[t, t, t, *gates], out_specs=t,
        out_shape=jax.ShapeDtypeStruct((L, D_MODEL), BF16), compiler_params=_cp(("parallel", "parallel")),
    )(ta, tb, tc, proj, proj, proj)


def _merge_bwd(ta, tb, tc, proj, dm, name):
    L = ta.shape[0]
    tm = min(L, 1024)
    t, gates, nj = _merge_specs(tm)

    def body(ta_ref, tb_ref, tc_ref, ga_ref, gb_ref, gc_ref, dm_ref, dta_ref, dtb_ref, dtc_ref, dga_ref, dgb_ref, dgc_ref):
        f = lambda r: r[...].astype(F32)
        _, vjp = jax.vjp(_merge_point, f(ta_ref), f(tb_ref), f(tc_ref), f(ga_ref), f(gb_ref), f(gc_ref))
        outs = vjp(f(dm_ref))
        for r, v in zip((dta_ref, dtb_ref, dtc_ref, dga_ref, dgb_ref, dgc_ref), outs):
            r[...] = v.astype(BF16)

    act = jax.ShapeDtypeStruct((L, D_MODEL), BF16)
    return pl.pallas_call(
        body, name=name, grid=(L // tm, nj), in_specs=[t, t, t, *gates, t],
        out_specs=[t] * 6, out_shape=[act] * 6,
        compiler_params=_cp(("parallel", "parallel")),
    )(ta, tb, tc, proj, proj, proj, dm)


GRAD_DT = BF16
SMALL = ("norm_w", "ssm_a_re", "ssm_a_im", "ssm_log_dt", "ssm_b_re", "ssm_b_im", "ssm_c_re", "ssm_c_im", "ssm_d",
         "ssm_glu_b", "sg_ln_w", "sg_ln_b", "sg_w", "sg_b", "attn_sinks")
G8 = SSM_GROUPS // N_SLAB


def _slab_b(bb_t):
    x = bb_t.transpose(1, 0, 2).reshape(N_SLAB, G8, SSM_GROUP, SSM_STATE)
    return jnp.einsum("jgcp,gh->jgchp", x, jnp.eye(G8, dtype=x.dtype)).reshape(N_SLAB, SLAB_CH, SLAB_ST)


def _unslab_b(d):
    x = d.reshape(N_SLAB, G8, SSM_GROUP, G8, SSM_STATE)
    x = jnp.einsum("jgchp,gh->jgcp", x, jnp.eye(G8, dtype=x.dtype))
    return x.reshape(SSM_GROUPS, SSM_GROUP, SSM_STATE).transpose(1, 0, 2)


def _slab_c(c):
    x = c.reshape(N_SLAB, G8, SSM_GROUP, SSM_STATE)
    return jnp.einsum("jgcp,gh->jgphc", x, jnp.eye(G8, dtype=x.dtype)).reshape(N_SLAB, SLAB_ST, SLAB_CH)


def _unslab_c(d):
    x = d.reshape(N_SLAB, G8, SSM_STATE, G8, SSM_GROUP)
    x = jnp.einsum("jgphc,gh->jgcp", x, jnp.eye(G8, dtype=x.dtype))
    return x.reshape(SSM_GROUPS, SSM_GROUP, SSM_STATE)


def _s5_prep(p, tag):
    bt_re = p["ssm_b_re"].transpose(2, 0, 1)
    bt_im = p["ssm_b_im"].transpose(2, 0, 1)
    raw = (p["ssm_a_re"], p["ssm_a_im"], p["ssm_log_dt"][:, None], bt_re, bt_im)
    lr, li, bbr, bbi = _s5_params_fwd(*raw, name=f"s5_params_{tag}")
    ops = (_slab_b(bbr).astype(BF16), _slab_b(bbi).astype(BF16),
           _slab_c(p["ssm_c_re"]).astype(BF16), _slab_c(p["ssm_c_im"]).astype(BF16),
           lr.reshape(N_SLAB, 1, SLAB_ST), li.reshape(N_SLAB, 1, SLAB_ST), p["ssm_d"].reshape(N_SLAB, 1, SLAB_CH))
    return raw, ops


def _layer_fwd(x, p, w, tabs, tag):
    L = x.shape[0]
    h = _rms_fwd(x, p["norm_w"][None], f"rms_fwd_{tag}")
    proj = _mm(h, w["win_t"], "nt", BF16, L, 256, D_MODEL, f"in_proj_{tag}")
    s5_raw, s5_ops = _s5_prep(p, tag)
    ya0 = _s5_fwd(proj, *s5_ops, name=f"s5_fwd_{tag}")
    ya = _glu_fwd(ya0, proj, w["glu"], p["ssm_glu_b"][None], f"glu_fwd_{tag}")
    yb = _sg_fwd(proj, p["sg_ln_w"][None], p["sg_ln_b"][None], p["sg_w"], p["sg_b"][:, :, None], f"sg_fwd_{tag}")
    yc = _attn_fwd(proj, p["attn_sinks"], tabs, f"attn_fwd_{tag}")
    ta = _mm(ya, w["wba_t"], "nt", BF16, 1024, 1024, 1024, f"branch_a_{tag}")
    tb = _mm(yb, w["wbb_t"], "nt", BF16, 1024, 1024, 1024, f"branch_b_{tag}")
    tc = _mm(yc, w["wbc_t"], "nt", BF16, 1024, 1024, 1024, f"branch_c_{tag}")
    merged = _merge_fwd(ta, tb, tc, proj, f"merge_fwd_{tag}")
    x_new = _mm(merged, w["wout"], "nn", F32, 1024, 512, D_MODEL, f"out_proj_{tag}", res=x)
    saved = dict(x=x, h=h, proj=proj, s5_raw=s5_raw, s5_ops=s5_ops, ya0=ya0, ya=ya, yb=yb, yc=yc,
                 ta=ta, tb=tb, tc=tc, merged=merged)
    return x_new, saved


def _layer_bwd(dx_out, p, w, tabs, s, tag):
    L = dx_out.shape[0]
    proj = s["proj"]
    big, small = {}, {}
    dmerged = _mm(dx_out, w["wout"], "nt", BF16, 1024, 512, D_MODEL, f"d_merged_{tag}")
    big["wout"] = _mm(s["merged"], dx_out, "tn", GRAD_DT, 512, 1024, L, f"d_wout_{tag}")
    dta, dtb, dtc, dga, dgb, dgc = _merge_bwd(s["ta"], s["tb"], s["tc"], proj, dmerged, f"merge_bwd_{tag}")
    dy = {}
    for br, dt in (("a", dta), ("b", dtb), ("c", dtc)):
        dy[br] = _mm(dt, w[f"wb{br}_t"], "nn", BF16, 1024, 1024, D_MODEL, f"d_y{br}_{tag}")
        big[f"wb{br}_t"] = _mm(dt, s[f"y{br}"], "tn", GRAD_DT, 512, 1024, L, f"d_wb{br}_{tag}")

    dq, dzc, dkvc, dkvp, dsink = _attn_bwd(proj, p["attn_sinks"], tabs, dy["c"], f"attn_bwd_{tag}")
    dkv = dkvc + jnp.concatenate([dkvp[CHUNK:], jnp.zeros((CHUNK, 256), F32)], axis=0)
    small["attn_sinks"] = dsink[:, 0]

    dub, dvb, dzb, dlw, dlb, dsgw, dsgb = _sg_bwd(
        proj, p["sg_ln_w"][None], p["sg_ln_b"][None], p["sg_w"], p["sg_b"][:, :, None], dy["b"], f"sg_bwd_{tag}")
    small.update(sg_ln_w=dlw[0], sg_ln_b=dlb[0], sg_w=dsgw, sg_b=dsgb[:, :, 0])

    dya0, dza, dglu, dglub = _glu_bwd(s["ya0"], proj, w["glu"], p["ssm_glu_b"][None], dy["a"], f"glu_bwd_{tag}")
    big["glu"] = dglu.astype(GRAD_DT)
    small["ssm_glu_b"] = dglub[0]

    dua, dbre, dbim, dcre, dcim, dlr, dli, dd = _s5_bwd(proj, dya0, *s["s5_ops"], name=f"s5_bwd_{tag}")
    da_re, da_im, dlog_dt, dbt_re, dbt_im = _s5_params_bwd(
        *s["s5_raw"], dlr.reshape(SSM_GROUPS, SSM_STATE), dli.reshape(SSM_GROUPS, SSM_STATE),
        _unslab_b(dbre), _unslab_b(dbim), name=f"s5_params_bwd_{tag}")
    small.update(ssm_a_re=da_re, ssm_a_im=da_im, ssm_log_dt=dlog_dt[:, 0],
                 ssm_b_re=dbt_re.transpose(1, 2, 0), ssm_b_im=dbt_im.transpose(1, 2, 0),
                 ssm_c_re=_unslab_c(dcre), ssm_c_im=_unslab_c(dcim), ssm_d=dd.reshape(SSM_WIDTH))

    dproj = jnp.concatenate([dua, dza, dub, dvb, dzb, dq, dkv.astype(BF16), dzc, dga, dgb, dgc], axis=-1)
    dh = _mm(dproj, w["win_t"], "nn", F32, 1024, D_MODEL, 256, f"d_h_{tag}")
    big["win_t"] = _mm(dproj, s["h"], "tn", GRAD_DT, 256, D_MODEL, L, f"d_win_{tag}")
    dx_in, dnw = _rms_bwd(s["x"], p["norm_w"][None], dh, dx_out, f"rms_bwd_{tag}")
    small["norm_w"] = dnw[0]
    return dx_in, big, small


def _local_step(x, tgt, small_p, final_w, big_w):
    L = x.shape[0]
    tabs = _rope_tables(L)
    saved = []
    for l in range(DEPTH):
        x, s = _layer_fwd(x, small_p[l], big_w[l], tabs, f"l{l}")
        saved.append(s)
    loss_acc, dx, dfw = _final(x, final_w[None], tgt, "final_norm_loss")
    big_g, small_g = [None] * DEPTH, [None] * DEPTH
    for l in reversed(range(DEPTH)):
        dx, big_g[l], small_g[l] = _layer_bwd(dx, small_p[l], big_w[l], tabs, saved[l], f"l{l}")
    return loss_acc[0, 0], dx, dfw[0], big_g, small_g


MESH = pl.DeviceIdType.MESH
ANY = pl.BlockSpec(memory_space=pl.ANY)
ROW_ALIGN = 16


def _place():
    return lax.axis_index("x"), lax.axis_index("y"), lax.axis_index("c")


HBM = pl.BlockSpec(memory_space=pltpu.HBM)
SEM = pl.BlockSpec(memory_space=pltpu.SEMAPHORE)
EFFECT = pltpu.SideEffectType.DATAFLOW_SIDE_EFFECTING


def _split_start(srcs, lands, n_copies, copies, name):
    n, m, k = len(srcs), len(lands), n_copies

    def body(*refs):
        src_refs, land_refs = refs[:n], refs[n:n + m]
        send_sems, recv_sems, token = refs[n + m:n + m + k], refs[n + m + k:n + m + 2 * k], refs[-1]
        for cp in copies(src_refs, land_refs, send_sems, recv_sems):
            cp.start()
        token[...] = jnp.zeros_like(token)

    ops = list(srcs) + list(lands)
    outs = pl.pallas_call(
        body, name=name,
        out_shape=(*[pltpu.SemaphoreType.DMA(())] * (2 * k),
                   *[pltpu.HBM(a.shape, a.dtype) for a in ops], jax.ShapeDtypeStruct((8, 128), F32)),
        in_specs=[HBM] * (n + m),
        out_specs=(*[SEM] * (2 * k), *[HBM] * (n + m), pl.BlockSpec(memory_space=pltpu.VMEM)),
        input_output_aliases={i: 2 * k + i for i in range(n + m)},
        compiler_params=pltpu.CompilerParams(has_side_effects=EFFECT),
    )(*[pltpu.with_memory_space_constraint(a, pltpu.HBM) for a in ops])
    return (list(outs[:k]), list(outs[k:2 * k]), list(outs[2 * k:2 * k + n]), list(outs[2 * k + n:2 * k + n + m]),
            outs[-1])


def _split_wait(send_sems, recv_sems, srcs, lands, after, copies, name):
    n, m, k = len(srcs), len(lands), len(send_sems)

    def body(*refs):
        src_refs, land_refs = refs[:n], refs[n:n + m]
        for cp in copies(src_refs, land_refs, refs[n + m:n + m + k], refs[n + m + k:n + m + 2 * k]):
            cp.wait_send()
            cp.wait_recv()

    ops = list(srcs) + list(lands)
    outs = pl.pallas_call(
        body, name=name,
        out_shape=tuple(pltpu.HBM(a.shape, a.dtype) for a in ops),
        in_specs=[HBM] * (n + m) + [SEM] * (2 * k) + [ANY],
        out_specs=tuple([HBM] * (n + m)),
        input_output_aliases={i: i for i in range(n + m)},
        compiler_params=pltpu.CompilerParams(has_side_effects=EFFECT),
    )(*ops, *send_sems, *recv_sems, after)
    return list(outs[:n]), list(outs[n:])


def _ag_rows(shard_ref, land_ref, px, py, pc):
    r = shard_ref.shape[0]
    start = pl.multiple_of((4 * px + 2 * py + pc) * r, ROW_ALIGN)
    return land_ref.at[pl.ds(start, r), :]


def _ag_copies(src_refs, land_refs, send_sems, recv_sems):
    x, y, c = _place()
    peers = [(x, y, 1 - c), (1 - x, y, c), (x, 1 - y, c), (1 - x, 1 - y, c)]
    return [pltpu.make_async_remote_copy(
        src_ref=src_refs[a], dst_ref=_ag_rows(src_refs[a], land_refs[a], x, y, c),
        send_sem=send_sems[4 * a + k], recv_sem=recv_sems[4 * a + k], device_id=peer, device_id_type=MESH)
        for a in range(len(src_refs)) for k, peer in enumerate(peers)]


def _ag_forward(shards, lands, name):
    n = len(shards)

    def body(*refs):
        src_refs, land_refs = refs[:n], refs[2 * n:3 * n]
        send_sems, recv_sems, local_sems = refs[3 * n:]
        x, y, c = _place()
        chips = [(1 - x, y), (x, 1 - y), (1 - x, 1 - y)]
        mine =[pltpu.make_async_copy(src_refs[a], _ag_rows(src_refs[a], land_refs[a], x, y, c), local_sems.at[a])
                for a in range(n)]
        passed = [pltpu.make_async_remote_copy(
            src_ref=_ag_rows(src_refs[a], land_refs[a], px, py, c), dst_ref=_ag_rows(src_refs[a], land_refs[a], px, py, c),
            send_sem=send_sems.at[a, j], recv_sem=recv_sems.at[a, j], device_id=(x, y, 1 - c), device_id_type=MESH)
            for a in range(n) for j, (px, py) in enumerate(chips)]
        for cp in mine + passed:
            cp.start()
        for a in range(n):
            for j, (px, py) in enumerate(chips):
                pltpu.make_async_remote_copy(
                    src_ref=_ag_rows(src_refs[a], land_refs[a], px, py, 1 - c),
                    dst_ref=_ag_rows(src_refs[a], land_refs[a], px, py, 1 - c),
                    send_sem=send_sems.at[a, j], recv_sem=recv_sems.at[a, j],
                    device_id=(x, y, 1 - c), device_id_type=MESH).wait_recv()
        for cp in passed:
            cp.wait_send()
        for cp in mine:
            cp.wait()

    return pl.pallas_call(
        body, name=name,
        in_specs=[ANY] * (2 * n), out_specs=[ANY] * n,
        out_shape=[jax.ShapeDtypeStruct(l.shape, l.dtype) for l in lands],
        input_output_aliases={n + i: i for i in range(n)},
        scratch_shapes=[pltpu.SemaphoreType.DMA((n, 3)), pltpu.SemaphoreType.DMA((n, 3)), pltpu.SemaphoreType.DMA((n,))],
    )(*shards, *lands)


def _allgather_start(shards, name):
    lands = [lax.empty((N_DEV * s.shape[0], s.shape[1]), s.dtype) for s in shards]
    return _split_start(shards, lands, 4 * len(shards), _ag_copies, name + "_start")


def _allgather_finish(started, after, name):
    send_sems, recv_sems, shards, lands, _ = started
    shards, lands = _split_wait(send_sems, recv_sems, shards, lands, after, _ag_copies, name + "_wait")
    return _ag_forward(shards, lands, name + "_forward")


def _rs_swap_cores(grads, name):
    n = len(grads)

    def body(*refs):
        ins, outs = refs[:n], refs[n:2 * n]
        send_sems, recv_sems = refs[2 * n:]
        x, y, c = _place()
        cps = []
        for a in range(n):
            r = ins[a].shape[0] // N_DEV
            for q in range(4):
                start = pl.multiple_of((2 * q + 1 - c) * r, ROW_ALIGN)
                cps.append(pltpu.make_async_remote_copy(
                    src_ref=ins[a].at[pl.ds(start, r), :], dst_ref=outs[a].at[q],
                    send_sem=send_sems.at[a, q], recv_sem=recv_sems.at[a, q],
                    device_id=(x, y, 1 - c), device_id_type=MESH))
        for cp in cps:
            cp.start()
        for cp in cps:
            cp.wait()

    return pl.pallas_call(
        body, name=name, in_specs=[ANY] * n, out_specs=[ANY] * n,
        out_shape=[jax.ShapeDtypeStruct((4, g.shape[0] // N_DEV, g.shape[1]), g.dtype) for g in grads],
        scratch_shapes=[pltpu.SemaphoreType.DMA((n, 4)), pltpu.SemaphoreType.DMA((n, 4))],
    )(*grads)


def _rs_chip_copies(sum_refs, land_refs, send_sems, recv_sems):
    x, y, c = _place()
    chips = [(1 - x, y), (x, 1 - y), (1 - x, 1 - y)]
    return [pltpu.make_async_remote_copy(
        src_ref=sum_refs[a].at[2 * px + py], dst_ref=land_refs[a].at[2 * x + y],
        send_sem=send_sems[3 * a + j], recv_sem=recv_sems[3 * a + j], device_id=(px, py, c), device_id_type=MESH)
        for a in range(len(sum_refs)) for j, (px, py) in enumerate(chips)]


def _row_tile(r):
    return max(t for t in range(ROW_ALIGN, min(r, 1024) + 1, ROW_ALIGN) if r % t == 0)


def _rs_add_cores(grad, recv, cidx, name):
    r, cols = recv.shape[1], recv.shape[2]
    tr = _row_tile(r)
    nb = r // tr

    def body(c_ref, g_ref, r_ref, o_ref):
        o_ref[...] = (g_ref[...].astype(F32) + r_ref[...].astype(F32)).astype(o_ref.dtype)

    return pl.pallas_call(
        body, name=name,
        grid_spec=pltpu.PrefetchScalarGridSpec(
            num_scalar_prefetch=1, grid=(4, nb),
            in_specs=[pl.BlockSpec((tr, cols), lambda q, i, c_ref: ((2 * q + c_ref[0]) * nb + i, 0)),
                      pl.BlockSpec((None, tr, cols), lambda q, i, c_ref: (q, i, 0))],
            out_specs=pl.BlockSpec((None, tr, cols), lambda q, i, c_ref: (q, i, 0))),
        out_shape=jax.ShapeDtypeStruct(recv.shape, recv.dtype),
        compiler_params=_cp(("parallel", "parallel")),
    )(cidx, grad, recv)


def _rs_add_chips(own, recv, slots, name):
    r, cols = recv.shape[1], recv.shape[2]
    tr = _row_tile(r)

    def body(s_ref, o_ref, r0_ref, r1_ref, r2_ref, out_ref):
        acc = o_ref[...].astype(F32)
        for ref in (r0_ref, r1_ref, r2_ref):
            acc = acc + ref[...].astype(F32)
        out_ref[...] = acc

    pick = lambda k: pl.BlockSpec((None, tr, cols), functools.partial(lambda i, s_ref, k: (s_ref[k], i, 0), k=k))
    return pl.pallas_call(
        body, name=name,
        grid_spec=pltpu.PrefetchScalarGridSpec(
            num_scalar_prefetch=1, grid=(r // tr,),
            in_specs=[pick(0), pick(1), pick(2), pick(3)],
            out_specs=pl.BlockSpec((tr, cols), lambda i, s_ref: (i, 0))),
        out_shape=jax.ShapeDtypeStruct((r, cols), F32),
        compiler_params=_cp(("parallel",)),
    )(slots, own, recv, recv, recv)


def _reduce_scatter_start(grads, tag):
    cidx = lax.axis_index("c").astype(jnp.int32)[None]
    recv = _rs_swap_cores(grads, f"rs_swap_cores_{tag}")
    sums = [_rs_add_cores(g, rv, cidx, f"rs_add_cores_{tag}_{i}") for i, (g, rv) in enumerate(zip(grads, recv))]
    lands = [lax.empty(s.shape, s.dtype) for s in sums]
    return _split_start(sums, lands, 3 * len(sums), _rs_chip_copies, f"rs_chips_{tag}_start")


def _reduce_scatter_finish(started, after, tag):
    send_sems, recv_sems, sums, lands, _ = started
    sums, lands = _split_wait(send_sems, recv_sems, sums, lands, after, _rs_chip_copies, f"rs_chips_{tag}_wait")
    x, y = lax.axis_index("x"), lax.axis_index("y")
    slots = jnp.stack([2 * x + y, 2 * (1 - x) + y, 2 * x + 1 - y, 2 * (1 - x) + 1 - y]).astype(jnp.int32)
    return [_rs_add_chips(s, l, slots, f"rs_add_chips_{tag}_{i}") for i, (s, l) in enumerate(zip(sums, lands))]


def _allreduce_small(pack, name):
    R, C = pack.shape
    rs = R // N_DEV
    assert R % (8 * N_DEV) == 0

    def body(p_ref, o_ref, parts, send1, recv1, send2, recv2):
        x, y, c = _place()
        me = 4 * x + 2 * y + c

        def block(ref, d):
            return ref.at[pl.ds(pl.multiple_of(d * rs, 8), rs), :]

        peers = [(1 - x if k & 4 else x, 1 - y if k & 2 else y, 1 - c if k & 1 else c) for k in range(1, N_DEV)]
        scatter = [pltpu.make_async_remote_copy(
            src_ref=block(p_ref, 4 * px + 2 * py + pc), dst_ref=parts.at[me], send_sem=send1.at[k], recv_sem=recv1.at[k],
            device_id=(px, py, pc), device_id_type=MESH) for k, (px, py, pc) in enumerate(peers)]
        for cp in scatter:
            cp.start()
        parts[me] = block(p_ref, me)[...]
        for cp in scatter:
            cp.wait()
        acc = parts[0]
        for d in range(1, N_DEV):
            acc = acc + parts[d]
        block(o_ref, me)[...] = acc
        gather = [pltpu.make_async_remote_copy(
            src_ref=block(o_ref, me), dst_ref=block(o_ref, me), send_sem=send2.at[k], recv_sem=recv2.at[k],
            device_id=peer, device_id_type=MESH) for k, peer in enumerate(peers)]
        for cp in gather:
            cp.start()
        for k, (px, py, pc) in enumerate(peers):
            pltpu.make_async_remote_copy(
                src_ref=block(o_ref, 4 * px + 2 * py + pc), dst_ref=block(o_ref, 4 * px + 2 * py + pc),
                send_sem=send2.at[k], recv_sem=recv2.at[k], device_id=(px, py, pc), device_id_type=MESH).wait_recv()
        for cp in gather:
            cp.wait_send()

    sems = pltpu.SemaphoreType.DMA((N_DEV - 1,))
    return pl.pallas_call(
        body, name=name,
        in_specs=[pl.BlockSpec(memory_space=pltpu.VMEM)], out_specs=pl.BlockSpec(memory_space=pltpu.VMEM),
        out_shape=jax.ShapeDtypeStruct((R, C), F32),
        scratch_shapes=[pltpu.VMEM((N_DEV, rs, C), F32), sems, sems, sems, sems],
        compiler_params=pltpu.CompilerParams(vmem_limit_bytes=VMEM_LIMIT),
    )(pack)


def _adamw(w, g, m, v, name):
    rows, cols = w.shape
    tr = 256 if rows % 256 == 0 else rows
    c1 = 1.0 - ADAM_B1 ** ADAM_STEP
    c2 = 1.0 - ADAM_B2 ** ADAM_STEP

    def body(w_ref, g_ref, m_ref, v_ref, d_ref, nm_ref, nv_ref):
        gv = g_ref[...]
        nm = ADAM_B1 * m_ref[...] + (1.0 - ADAM_B1) * gv
        nv = ADAM_B2 * v_ref[...] + (1.0 - ADAM_B2) * jnp.square(gv)
        d_ref[...] = -ADAM_LR * ((nm / c1) / (jnp.sqrt(nv / c2) + ADAM_EPS) + ADAM_WD * w_ref[...])
        nm_ref[...] = nm
        nv_ref[...] = nv

    blk = pl.BlockSpec((tr, cols), lambda i: (i, 0))
    sh = jax.ShapeDtypeStruct((rows, cols), F32)
    return pl.pallas_call(
        body, name=name, grid=(rows // tr,), in_specs=[blk] * 4, out_specs=[blk] * 3, out_shape=[sh] * 3,
        compiler_params=_cp(("parallel",)),
    )(w, g, m, v)


WEIGHTS = ("norm_w", "w_in", "ssm_a_re", "ssm_a_im", "ssm_log_dt", "ssm_b_re", "ssm_b_im", "ssm_c_re", "ssm_c_im",
           "ssm_d", "ssm_glu_w", "ssm_glu_b", "sg_ln_w", "sg_ln_b", "sg_w", "sg_b", "attn_sinks",
           "w_branch_a", "w_branch_b", "w_branch_c", "w_out", "final_norm_w")
BIG = ("w_in", "ssm_glu_w", "w_branch_a", "w_branch_b", "w_branch_c", "w_out")
BIG_KEY = {"w_in": ("win_t", True), "ssm_glu_w": ("glu", False), "w_branch_a": ("wba_t", True),
           "w_branch_b": ("wbb_t", True), "w_branch_c": ("wbc_t", True), "w_out": ("wout", False)}
PACK_COLS = 1024


def _pack(arrs):
    flat = jnp.concatenate([a.reshape(-1) for a in arrs])
    rows = -(-flat.shape[0] // (8 * N_DEV * PACK_COLS)) * 8 * N_DEV
    return jnp.pad(flat, (0, rows * PACK_COLS - flat.shape[0])).reshape(rows, PACK_COLS)


def _unpack(pack, like):
    flat = pack.reshape(-1)
    out, off = [], 0
    for a in like:
        out.append(flat[off:off + a.size].reshape(a.shape))
        off += a.size
    return out


def kernel(x, norm_w, w_in, ssm_a_re, ssm_a_im, ssm_log_dt, ssm_b_re, ssm_b_im, ssm_c_re, ssm_c_im, ssm_d, ssm_glu_w, ssm_glu_b, sg_ln_w, sg_ln_b, sg_w, sg_b, attn_sinks, w_branch_a, w_branch_b, w_branch_c, w_out, final_norm_w, loss_target, m_norm_w, m_w_in, m_ssm_a_re, m_ssm_a_im, m_ssm_log_dt, m_ssm_b_re, m_ssm_b_im, m_ssm_c_re, m_ssm_c_im, m_ssm_d, m_ssm_glu_w, m_ssm_glu_b, m_sg_ln_w, m_sg_ln_b, m_sg_w, m_sg_b, m_attn_sinks, m_w_branch_a, m_w_branch_b, m_w_branch_c, m_w_out, m_final_norm_w, v_norm_w, v_w_in, v_ssm_a_re, v_ssm_a_im, v_ssm_log_dt, v_ssm_b_re, v_ssm_b_im, v_ssm_c_re, v_ssm_c_im, v_ssm_d, v_ssm_glu_w, v_ssm_glu_b, v_sg_ln_w, v_sg_ln_b, v_sg_w, v_sg_b, v_attn_sinks, v_w_branch_a, v_w_branch_b, v_w_branch_c, v_w_out, v_final_norm_w):
    w = dict(zip(WEIGHTS, (norm_w, w_in, ssm_a_re, ssm_a_im, ssm_log_dt, ssm_b_re, ssm_b_im, ssm_c_re, ssm_c_im, ssm_d, ssm_glu_w, ssm_glu_b, sg_ln_w, sg_ln_b, sg_w, sg_b, attn_sinks, w_branch_a, w_branch_b, w_branch_c, w_out, final_norm_w)))
    m = dict(zip(WEIGHTS, (m_norm_w, m_w_in, m_ssm_a_re, m_ssm_a_im, m_ssm_log_dt, m_ssm_b_re, m_ssm_b_im, m_ssm_c_re, m_ssm_c_im, m_ssm_d, m_ssm_glu_w, m_ssm_glu_b, m_sg_ln_w, m_sg_ln_b, m_sg_w, m_sg_b, m_attn_sinks, m_w_branch_a, m_w_branch_b, m_w_branch_c, m_w_out, m_final_norm_w)))
    v = dict(zip(WEIGHTS, (v_norm_w, v_w_in, v_ssm_a_re, v_ssm_a_im, v_ssm_log_dt, v_ssm_b_re, v_ssm_b_im, v_ssm_c_re, v_ssm_c_im, v_ssm_d, v_ssm_glu_w, v_ssm_glu_b, v_sg_ln_w, v_sg_ln_b, v_sg_w, v_sg_b, v_attn_sinks, v_w_branch_a, v_w_branch_b, v_w_branch_c, v_w_out, v_final_norm_w)))

    keys = [BIG_KEY[n][0] for n in BIG]
    shards = [[(w[n][l].T if BIG_KEY[n][1] else w[n][l]).astype(BF16) for n in BIG] for l in range(DEPTH)]
    small_p = [{n: w[n][l] for n in SMALL} for l in range(DEPTH)]
    xv, tgt = x[0], loss_target[0]
    tabs = _rope_tables(xv.shape[0])

    ag0 = _allgather_start(shards[0], "ag_l0")
    big_w0 = dict(zip(keys, _allgather_finish(ag0, ag0[4], "ag_l0")))
    shards[1][1] = shards[1][1] + (big_w0["glu"][0, 0] * 0).astype(BF16)
    ag1 = _allgather_start(shards[1], "ag_l1")
    small_p[0]["norm_w"] = small_p[0]["norm_w"] + ag1[4][0, 0]
    x1, saved0 = _layer_fwd(xv, small_p[0], big_w0, tabs, "l0")
    big_w1 = dict(zip(keys, _allgather_finish(ag1, x1, "ag_l1")))
    x2, saved1 = _layer_fwd(x1, small_p[1], big_w1, tabs, "l1")
    loss_acc, dx2, dfw = _final(x2, w["final_norm_w"][None], tgt, "final_norm_loss")
    loss = lax.psum(loss_acc[0, 0], ("x", "y", "c"))
    dfw = dfw[0]

    dx1, big_g1, small_g1 = _layer_bwd(dx2, small_p[1], big_w1, tabs, saved1, "l1")
    rs1 = _reduce_scatter_start([big_g1[k] for k in keys], "l1")
    p0_bwd = dict(small_p[0], attn_sinks=small_p[0]["attn_sinks"] + rs1[4][0, 0])
    dx, big_g0, small_g0 = _layer_bwd(dx1, p0_bwd, big_w0, tabs, saved0, "l0")
    red1 = _reduce_scatter_finish(rs1, dx, "l1")
    rs0 = _reduce_scatter_start([big_g0[k] for k in keys], "l0")
    red0 = _reduce_scatter_finish(rs0, rs0[4], "l0")
    small_g = [small_g0, small_g1]

    grads = {}
    for i, n in enumerate(BIG):
        grads[n] = jnp.stack([g.T if BIG_KEY[n][1] else g for g in (red0[i], red1[i])])

    small_names = [n for n in WEIGHTS if n not in BIG]
    small_list = [jnp.stack([small_g[l][n] for l in range(DEPTH)]) if n != "final_norm_w" else dfw for n in small_names]
    red_small = _allreduce_small(_pack(small_list), "allreduce_small")
    for n, g in zip(small_names, _unpack(red_small, small_list)):
        grads[n] = g

    delta, new_m, new_v = {}, {}, {}
    for n in BIG:
        shp = w[n].shape
        two_d = lambda a: a.reshape(-1, shp[-1])
        d_, m_, v_ = _adamw(two_d(w[n]), two_d(grads[n]), two_d(m[n]), two_d(v[n]), f"adamw_{n}")
        delta[n], new_m[n], new_v[n] = d_.reshape(shp), m_.reshape(shp), v_.reshape(shp)
    packs = [_pack([d[n] for n in small_names]) for d in (w, m, v)]
    outs = _adamw(packs[0], red_small, packs[1], packs[2], "adamw_small")
    for res, o in zip((delta, new_m, new_v), outs):
        for n, a in zip(small_names, _unpack(o, small_list)):
            res[n] = a

    return (loss, dx[None], *[grads[n] for n in WEIGHTS], *[delta[n] for n in WEIGHTS],
            *[new_m[n] for n in WEIGHTS], *[new_v[n] for n in WEIGHTS])
```

```python
import functools
import math

import jax
import jax.numpy as jnp
from jax import lax
from jax.experimental import pallas as pl
from jax.experimental.pallas import tpu as pltpu

F32 = jnp.float32
BF16 = jnp.bfloat16

D_MODEL = 2048
DEPTH = 2
EPS = 1e-6
NEG_INF = -1e30
N_DEV = 8

SSM_WIDTH = 1024
SSM_GROUP = 16
SSM_GROUPS = 64
SSM_STATE = 64
N_SLAB = 8
SLAB_CH = 128
SLAB_ST = 512
N_SEG = 8
SEG_PAD = 8

SG_HEADS = 8
CHUNK = 128
HEAD_DIM = 64
ATT_HEADS = 16
ROT_DIM = 16
ROPE_THETA = 500000.0

D_IN = 13568
OFF_UA, OFF_ZA, OFF_UB, OFF_VB, OFF_ZB, OFF_Q, OFF_KV, OFF_ZC, OFF_G = (
    0, 1024, 2048, 3072, 4096, 5120, 6144, 6400, 7424)

ADAM_LR, ADAM_B1, ADAM_B2, ADAM_EPS, ADAM_WD, ADAM_STEP = 0.001, 0.9, 0.999, 1e-08, 0.01, 10

VMEM_LIMIT = 56 * 1024 * 1024


def _cp(sem=None):
    return pltpu.CompilerParams(dimension_semantics=sem, vmem_limit_bytes=VMEM_LIMIT)


def _dot(a, b):
    return jnp.dot(a, b, preferred_element_type=F32)


def _dot_nt(a, b):
    return lax.dot_general(a, b, (((1,), (1,)), ((), ())), preferred_element_type=F32)


def _dot_tn(a, b):
    return lax.dot_general(a, b, (((0,), (0,)), ((), ())), preferred_element_type=F32)


def _mm(a, b, mode, out_dtype, tm, tn, tk, name, res=None):
    if mode == "nn":
        (m, k), (_, n) = a.shape, b.shape
    elif mode == "nt":
        (m, k), (n, _) = a.shape, b.shape
    else:
        (k, m), (_, n) = a.shape, b.shape
    tm, tn, tk = min(tm, m), min(tn, n), min(tk, k)
    assert m % tm == 0 and n % tn == 0 and k % tk == 0, (name, m, n, k, tm, tn, tk)
    nk = k // tk
    a_spec = {"nn": pl.BlockSpec((tm, tk), lambda i, j, kk: (i, kk)),
              "nt": pl.BlockSpec((tm, tk), lambda i, j, kk: (i, kk)),
              "tn": pl.BlockSpec((tk, tm), lambda i, j, kk: (kk, i))}[mode]
    b_spec = {"nn": pl.BlockSpec((tk, tn), lambda i, j, kk: (kk, j)),
              "nt": pl.BlockSpec((tn, tk), lambda i, j, kk: (j, kk)),
              "tn": pl.BlockSpec((tk, tn), lambda i, j, kk: (kk, j))}[mode]
    dot = {"nn": _dot, "nt": _dot_nt, "tn": _dot_tn}[mode]
    has_res = res is not None

    def body(*refs):
        if has_res:
            a_ref, b_ref, r_ref, o_ref, acc = refs
        else:
            a_ref, b_ref, o_ref, acc = refs
        kk = pl.program_id(2)

        @pl.when(kk == 0)
        def _():
            acc[...] = jnp.zeros_like(acc)

        acc[...] += dot(a_ref[...].astype(BF16), b_ref[...].astype(BF16))

        @pl.when(kk == nk - 1)
        def _():
            r = acc[...]
            if has_res:
                r = r + r_ref[...]
            o_ref[...] = r.astype(out_dtype)

    in_specs = [a_spec, b_spec]
    args = [a, b]
    if has_res:
        in_specs.append(pl.BlockSpec((tm, tn), lambda i, j, kk: (i, j)))
        args.append(res)
    return pl.pallas_call(
        body, name=name,
        grid=(m // tm, n // tn, nk),
        in_specs=in_specs,
        out_specs=pl.BlockSpec((tm, tn), lambda i, j, kk: (i, j)),
        out_shape=jax.ShapeDtypeStruct((m, n), out_dtype),
        scratch_shapes=[pltpu.VMEM((tm, tn), F32)],
        compiler_params=_cp(("parallel", "parallel", "arbitrary")),
    )(*args)


def _rms(x, w):
    return x * lax.rsqrt(jnp.mean(x * x, axis=-1, keepdims=True) + EPS) * w


def _rms_fwd(x, w, name):
    L, D = x.shape
    tm = min(L, 256)

    def body(x_ref, w_ref, h_ref):
        h_ref[...] = _rms(x_ref[...], w_ref[...]).astype(BF16)

    return pl.pallas_call(
        body, name=name, grid=(L // tm,),
        in_specs=[pl.BlockSpec((tm, D), lambda i: (i, 0)), pl.BlockSpec((1, D), lambda i: (0, 0))],
        out_specs=pl.BlockSpec((tm, D), lambda i: (i, 0)),
        out_shape=jax.ShapeDtypeStruct((L, D), BF16),
        compiler_params=_cp(("parallel",)),
    )(x, w)


def _rms_bwd(x, w, dh, dres, name):
    L, D = x.shape
    tm = min(L, 256)

    def body(x_ref, w_ref, dh_ref, dres_ref, dx_ref, dw_ref):
        _, vjp = jax.vjp(_rms, x_ref[...], w_ref[...])
        dx, dw = vjp(dh_ref[...])
        dx_ref[...] = dx + dres_ref[...]

        @pl.when(pl.program_id(0) == 0)
        def _():
            dw_ref[...] = jnp.zeros_like(dw_ref)

        dw_ref[...] += dw

    row = pl.BlockSpec((tm, D), lambda i: (i, 0))
    vec = pl.BlockSpec((1, D), lambda i: (0, 0))
    return pl.pallas_call(
        body, name=name, grid=(L // tm,),
        in_specs=[row, vec, row, row],
        out_specs=[row, vec],
        out_shape=[jax.ShapeDtypeStruct((L, D), F32), jax.ShapeDtypeStruct((1, D), F32)],
        compiler_params=_cp(("arbitrary",)),
    )(x, w, dh, dres)


def _final(x, fw, tgt, name):
    L, D = x.shape
    tm = min(L, 256)

    def loss_fn(xv, wv, tv):
        err = _rms(xv, wv) - tv
        return jnp.sum(err * err) * (0.5 / D)

    def body(x_ref, w_ref, t_ref, loss_ref, dx_ref, dw_ref):
        tv = t_ref[...]
        val, vjp = jax.vjp(lambda a, b: loss_fn(a, b, tv), x_ref[...], w_ref[...])
        dx, dw = vjp(jnp.ones((), F32))
        dx_ref[...] = dx

        @pl.when(pl.program_id(0) == 0)
        def _():
            dw_ref[...] = jnp.zeros_like(dw_ref)
            loss_ref[...] = jnp.zeros_like(loss_ref)

        dw_ref[...] += dw
        loss_ref[...] += jnp.full(loss_ref.shape, val, F32)

    row = pl.BlockSpec((tm, D), lambda i: (i, 0))
    vec = pl.BlockSpec((1, D), lambda i: (0, 0))
    return pl.pallas_call(
        body, name=name, grid=(L // tm,),
        in_specs=[row, vec, row],
        out_specs=[pl.BlockSpec((8, 128), lambda i: (0, 0)), row, vec],
        out_shape=[jax.ShapeDtypeStruct((8, 128), F32), jax.ShapeDtypeStruct((L, D), F32),
                   jax.ShapeDtypeStruct((1, D), F32)],
        compiler_params=_cp(("arbitrary",)),
    )(x, fw, tgt)


def _s5_param_fn(a_re, a_im, log_dt, bt_re, bt_im):
    dt = jnp.exp(log_dt)
    zr, zi = a_re * dt, a_im * dt
    er = jnp.exp(zr)
    lr, li = er * jnp.cos(zi), er * jnp.sin(zi)
    nr, ni = lr - 1.0, li
    den = a_re * a_re + a_im * a_im
    cr = (nr * a_re + ni * a_im) / den
    ci = (ni * a_re - nr * a_im) / den
    bbr = cr[None] * bt_re - ci[None] * bt_im
    bbi = cr[None] * bt_im + ci[None] * bt_re
    return lr, li, bbr, bbi


def _s5_params_fwd(a_re, a_im, log_dt, bt_re, bt_im, name):
    def body(ar, ai, ld, br, bi, lr, li, bbr, bbi):
        o = _s5_param_fn(ar[...], ai[...], ld[...], br[...], bi[...])
        lr[...], li[...], bbr[...], bbi[...] = o

    gp = jax.ShapeDtypeStruct(a_re.shape, F32)
    cgp = jax.ShapeDtypeStruct(bt_re.shape, F32)
    return pl.pallas_call(body, name=name, out_shape=[gp, gp, cgp, cgp])(a_re, a_im, log_dt, bt_re, bt_im)


def _s5_params_bwd(a_re, a_im, log_dt, bt_re, bt_im, dlr, dli, dbbr, dbbi, name):
    def body(ar, ai, ld, br, bi, g0, g1, g2, g3, o0, o1, o2, o3, o4):
        _, vjp = jax.vjp(_s5_param_fn, ar[...], ai[...], ld[...], br[...], bi[...])
        o0[...], o1[...], o2[...], o3[...], o4[...] = vjp((g0[...], g1[...], g2[...], g3[...]))

    gp = jax.ShapeDtypeStruct(a_re.shape, F32)
    cgp = jax.ShapeDtypeStruct(bt_re.shape, F32)
    return pl.pallas_call(body, name=name,
                          out_shape=[gp, gp, jax.ShapeDtypeStruct(log_dt.shape, F32), cgp, cgp])(
        a_re, a_im, log_dt, bt_re, bt_im, dlr, dli, dbbr, dbbi)


def _cmul(ar, ai, br, bi):
    return ar * br - ai * bi, ar * bi + ai * br


def _cpow(lr, li, n):
    rr, ri = None, None
    br, bi = lr, li
    while n:
        if n & 1:
            rr, ri = (br, bi) if rr is None else _cmul(rr, ri, br, bi)
        n >>= 1
        if n:
            br, bi = _cmul(br, bi, br, bi)
    return rr, ri


def _shift_rows(x, up):
    row = lax.broadcasted_iota(jnp.int32, x.shape, 0)
    if up:
        return jnp.where(row == N_SEG - 1, 0.0, pltpu.roll(x, N_SEG - 1, 0))
    return jnp.where(row == 0, 0.0, pltpu.roll(x, 1, 0))


def _seg_scan(s_re, s_im, lam, seg, reverse):
    stride = seg + SEG_PAD
    nt = SLAB_ST // 128
    lam_t = [(jnp.broadcast_to(lam[0][:, j * 128:(j + 1) * 128], (N_SEG, 128)),
              jnp.broadcast_to(lam[1][:, j * 128:(j + 1) * 128], (N_SEG, 128))) for j in range(nt)]

    def rows(i):
        return pl.ds(i, N_SEG, stride=stride)

    def step1(t, carry):
        i = seg - 1 - t if reverse else t
        out = []
        for j in range(nt):
            cr, ci = carry[2 * j], carry[2 * j + 1]
            nr, ni = _cmul(lam_t[j][0], lam_t[j][1], cr, ci)
            nr = nr + s_re[j, rows(i), :]
            ni = ni + s_im[j, rows(i), :]
            s_re[j, rows(i), :] = nr
            s_im[j, rows(i), :] = ni
            out += [nr, ni]
        return tuple(out)

    zero = tuple(jnp.zeros((N_SEG, 128), F32) for _ in range(2 * nt))
    ends = lax.fori_loop(0, seg, step1, zero)

    carries = []
    for j in range(nt):
        pr, pi = _cpow(lam_t[j][0], lam_t[j][1], seg)
        cr, ci = jnp.zeros((N_SEG, 128), F32), jnp.zeros((N_SEG, 128), F32)
        for _ in range(N_SEG - 1):
            tr, ti = _cmul(pr, pi, cr, ci)
            cr = _shift_rows(tr + ends[2 * j], reverse)
            ci = _shift_rows(ti + ends[2 * j + 1], reverse)
        carries += [cr, ci]

    def step2(t, pw):
        i = seg - 1 - t if reverse else t
        out = []
        for j in range(nt):
            pr, pi = pw[2 * j], pw[2 * j + 1]
            ar, ai = _cmul(pr, pi, carries[2 * j], carries[2 * j + 1])
            s_re[j, rows(i), :] = s_re[j, rows(i), :] + ar
            s_im[j, rows(i), :] = s_im[j, rows(i), :] + ai
            qr, qi = _cmul(pr, pi, lam_t[j][0], lam_t[j][1])
            out += [qr, qi]
        return tuple(out)

    lax.fori_loop(0, seg, step2, tuple(x for j in range(nt) for x in lam_t[j]))
    return carries


def _seg_rows(ref, k, seg):
    stride = seg + SEG_PAD
    return jnp.concatenate([ref[j, pl.ds(k * stride, seg), :] for j in range(SLAB_ST // 128)], axis=-1)


def _seg_store(ref, k, seg, val):
    stride = seg + SEG_PAD
    for j in range(SLAB_ST // 128):
        ref[j, pl.ds(k * stride, seg), :] = val[:, j * 128:(j + 1) * 128]


def _s5_specs(L):
    col = lambda off: pl.BlockSpec((L, SLAB_CH), lambda j: (0, off + j))
    mat_b = pl.BlockSpec((None, SLAB_CH, SLAB_ST), lambda j: (j, 0, 0))
    mat_c = pl.BlockSpec((None, SLAB_ST, SLAB_CH), lambda j: (j, 0, 0))
    vec_s = pl.BlockSpec((None, 1, SLAB_ST), lambda j: (j, 0, 0))
    vec_c = pl.BlockSpec((None, 1, SLAB_CH), lambda j: (j, 0, 0))
    return col, mat_b, mat_c, vec_s, vec_c


def _s5_states(u_ref, bre_ref, bim_ref, lam, s_re, s_im, seg):
    for k in range(N_SEG):
        uk = u_ref[pl.ds(k * seg, seg), :]
        _seg_store(s_re, k, seg, _dot(uk, bre_ref[...]))
        _seg_store(s_im, k, seg, _dot(uk, bim_ref[...]))
    return _seg_scan(s_re, s_im, lam, seg, reverse=False)


def _s5_fwd(proj, bre, bim, cre_t, cim_t, lam_re, lam_im, dvec, name):
    L = proj.shape[0]
    seg = L // N_SEG
    col, mat_b, mat_c, vec_s, vec_c = _s5_specs(L)
    rows = N_SEG * (seg + SEG_PAD)

    def body(u_ref, bre_ref, bim_ref, cre_ref, cim_ref, lr_ref, li_ref, d_ref, y_ref, s_re, s_im):
        _s5_states(u_ref, bre_ref, bim_ref, (lr_ref[...], li_ref[...]), s_re, s_im, seg)
        for k in range(N_SEG):
            y = (_dot(_seg_rows(s_re, k, seg).astype(BF16), cre_ref[...])
                 - _dot(_seg_rows(s_im, k, seg).astype(BF16), cim_ref[...]))
            y = y + d_ref[...] * u_ref[pl.ds(k * seg, seg), :].astype(F32)
            y_ref[pl.ds(k * seg, seg), :] = jax.nn.gelu(y).astype(BF16)

    return pl.pallas_call(
        body, name=name, grid=(N_SLAB,),
        in_specs=[col(OFF_UA // SLAB_CH), mat_b, mat_b, mat_c, mat_c, vec_s, vec_s, vec_c],
        out_specs=pl.BlockSpec((L, SLAB_CH), lambda j: (0, j)),
        out_shape=jax.ShapeDtypeStruct((L, SSM_WIDTH), BF16),
        scratch_shapes=[pltpu.VMEM((SLAB_ST // 128, rows, 128), F32)] * 2,
        compiler_params=_cp(("parallel",)),
    )(proj, bre, bim, cre_t, cim_t, lam_re, lam_im, dvec)


def _s5_bwd(proj, dy, bre, bim, cre_t, cim_t, lam_re, lam_im, dvec, name):
    L = proj.shape[0]
    seg = L // N_SEG
    stride = seg + SEG_PAD
    col, mat_b, mat_c, vec_s, vec_c = _s5_specs(L)
    rows = N_SEG * stride
    nt = SLAB_ST // 128

    def body(u_ref, dy_ref, bre_ref, bim_ref, cre_ref, cim_ref, lr_ref, li_ref, d_ref,
             du_ref, dbre_ref, dbim_ref, dcre_ref, dcim_ref, dlr_ref, dli_ref, dd_ref,
             s_re, s_im, a_re, a_im, dyp):
        lam = (lr_ref[...], li_ref[...])
        carry_s = _s5_states(u_ref, bre_ref, bim_ref, lam, s_re, s_im, seg)
        dcre = jnp.zeros((SLAB_ST, SLAB_CH), F32)
        dcim = jnp.zeros((SLAB_ST, SLAB_CH), F32)
        dd = jnp.zeros((1, SLAB_CH), F32)
        for k in range(N_SEG):
            sre = _seg_rows(s_re, k, seg).astype(BF16)
            sim = _seg_rows(s_im, k, seg).astype(BF16)
            uk = u_ref[pl.ds(k * seg, seg), :].astype(F32)
            ypre = _dot(sre, cre_ref[...]) - _dot(sim, cim_ref[...]) + d_ref[...] * uk
            _, vjp = jax.vjp(jax.nn.gelu, ypre)
            (dyk,) = vjp(dy_ref[pl.ds(k * seg, seg), :].astype(F32))
            dyp[pl.ds(k * seg, seg), :] = dyk
            dd = dd + jnp.sum(dyk * uk, axis=0, keepdims=True)
            dyb = dyk.astype(BF16)
            dcre = dcre + _dot_tn(sre, dyb)
            dcim = dcim - _dot_tn(sim, dyb)
            _seg_store(a_re, k, seg, _dot_nt(dyb, cre_ref[...]))
            _seg_store(a_im, k, seg, -_dot_nt(dyb, cim_ref[...]))
        dcre_ref[...] = dcre
        dcim_ref[...] = dcim
        dd_ref[...] = dd

        _seg_scan(a_re, a_im, (lam[0], -lam[1]), seg, reverse=True)

        def acc_dlam(i, acc):
            out = []
            for j in range(nt):
                ar = a_re[j, pl.ds(i, N_SEG, stride=stride), :]
                ai = a_im[j, pl.ds(i, N_SEG, stride=stride), :]
                pr = s_re[j, pl.ds(i - 1, N_SEG, stride=stride), :]
                pi = s_im[j, pl.ds(i - 1, N_SEG, stride=stride), :]
                out += [acc[2 * j] + ar * pr + ai * pi, acc[2 * j + 1] + ai * pr - ar * pi]
            return tuple(out)

        first = []
        for j in range(nt):
            ar = a_re[j, pl.ds(0, N_SEG, stride=stride), :]
            ai = a_im[j, pl.ds(0, N_SEG, stride=stride), :]
            pr, pi = carry_s[2 * j], carry_s[2 * j + 1]
            first += [ar * pr + ai * pi, ai * pr - ar * pi]
        acc = lax.fori_loop(1, seg, acc_dlam, tuple(first))
        dlr_ref[...] = jnp.concatenate([jnp.sum(acc[2 * j], axis=0, keepdims=True) for j in range(nt)], axis=-1)
        dli_ref[...] = jnp.concatenate([jnp.sum(acc[2 * j + 1], axis=0, keepdims=True) for j in range(nt)], axis=-1)

        dbre = jnp.zeros((SLAB_CH, SLAB_ST), F32)
        dbim = jnp.zeros((SLAB_CH, SLAB_ST), F32)
        for k in range(N_SEG):
            are = _seg_rows(a_re, k, seg).astype(BF16)
            aim = _seg_rows(a_im, k, seg).astype(BF16)
            uk = u_ref[pl.ds(k * seg, seg), :]
            du = _dot_nt(are, bre_ref[...]) + _dot_nt(aim, bim_ref[...]) + dyp[pl.ds(k * seg, seg), :] * d_ref[...]
            du_ref[pl.ds(k * seg, seg), :] = du.astype(BF16)
            dbre = dbre + _dot_tn(uk, are)
            dbim = dbim + _dot_tn(uk, aim)
        dbre_ref[...] = dbre
        dbim_ref[...] = dbim

    scan_buf = pltpu.VMEM((nt, rows, 128), F32)
    return pl.pallas_call(
        body, name=name, grid=(N_SLAB,),
        in_specs=[col(OFF_UA // SLAB_CH), pl.BlockSpec((L, SLAB_CH), lambda j: (0, j)),
                  mat_b, mat_b, mat_c, mat_c, vec_s, vec_s, vec_c],
        out_specs=[pl.BlockSpec((L, SLAB_CH), lambda j: (0, j)), mat_b, mat_b, mat_c, mat_c, vec_s, vec_s, vec_c],
        out_shape=[jax.ShapeDtypeStruct((L, SSM_WIDTH), BF16),
                   jax.ShapeDtypeStruct((N_SLAB, SLAB_CH, SLAB_ST), F32),
                   jax.ShapeDtypeStruct((N_SLAB, SLAB_CH, SLAB_ST), F32),
                   jax.ShapeDtypeStruct((N_SLAB, SLAB_ST, SLAB_CH), F32),
                   jax.ShapeDtypeStruct((N_SLAB, SLAB_ST, SLAB_CH), F32),
                   jax.ShapeDtypeStruct((N_SLAB, 1, SLAB_ST), F32),
                   jax.ShapeDtypeStruct((N_SLAB, 1, SLAB_ST), F32),
                   jax.ShapeDtypeStruct((N_SLAB, 1, SLAB_CH), F32)],
        scratch_shapes=[scan_buf, scan_buf, scan_buf, scan_buf, pltpu.VMEM((L, SLAB_CH), F32)],
        compiler_params=_cp(("parallel",)),
    )(proj, dy, bre, bim, cre_t, cim_t, lam_re, lam_im, dvec)


def _glu_point(y0, pre, za, b):
    return y0 * jax.nn.sigmoid(pre + b) * jax.nn.silu(za)


def _glu_specs(L, tm):
    row = pl.BlockSpec((tm, SSM_WIDTH), lambda i: (i, 0))
    za = pl.BlockSpec((tm, SSM_WIDTH), lambda i: (i, OFF_ZA // SSM_WIDTH))
    wmat = pl.BlockSpec((SSM_WIDTH, SSM_WIDTH), lambda i: (0, 0))
    vec = pl.BlockSpec((1, SSM_WIDTH), lambda i: (0, 0))
    return row, za, wmat, vec


def _glu_fwd(ya0, proj, w, b, name):
    L = ya0.shape[0]
    tm = min(L, 512)
    row, za, wmat, vec = _glu_specs(L, tm)

    def body(y_ref, z_ref, w_ref, b_ref, o_ref):
        y0 = y_ref[...]
        pre = _dot(y0, w_ref[...])
        o_ref[...] = _glu_point(y0.astype(F32), pre, z_ref[...].astype(F32), b_ref[...]).astype(BF16)

    return pl.pallas_call(
        body, name=name, grid=(L // tm,), in_specs=[row, za, wmat, vec], out_specs=row,
        out_shape=jax.ShapeDtypeStruct((L, SSM_WIDTH), BF16), compiler_params=_cp(("parallel",)),
    )(ya0, proj, w, b)


def _glu_bwd(ya0, proj, w, b, dya, name):
    L = ya0.shape[0]
    tm = min(L, 512)
    row, za, wmat, vec = _glu_specs(L, tm)

    def body(y_ref, z_ref, w_ref, b_ref, g_ref, dy0_ref, dza_ref, dw_ref, db_ref):
        y0 = y_ref[...]
        pre = _dot(y0, w_ref[...])
        _, vjp = jax.vjp(_glu_point, y0.astype(F32), pre, z_ref[...].astype(F32), b_ref[...])
        dy0, dpre, dza, db = vjp(g_ref[...].astype(F32))
        dpb = dpre.astype(BF16)
        dy0_ref[...] = (dy0 + _dot_nt(dpb, w_ref[...])).astype(BF16)
        dza_ref[...] = dza.astype(BF16)

        @pl.when(pl.program_id(0) == 0)
        def _():
            dw_ref[...] = jnp.zeros_like(dw_ref)
            db_ref[...] = jnp.zeros_like(db_ref)

        dw_ref[...] += _dot_tn(y0, dpb)
        db_ref[...] += db

    return pl.pallas_call(
        body, name=name, grid=(L // tm,), in_specs=[row, za, wmat, vec, row],
        out_specs=[row, row, wmat, vec],
        out_shape=[jax.ShapeDtypeStruct((L, SSM_WIDTH), BF16), jax.ShapeDtypeStruct((L, SSM_WIDTH), BF16),
                   jax.ShapeDtypeStruct((SSM_WIDTH, SSM_WIDTH), F32), jax.ShapeDtypeStruct((1, SSM_WIDTH), F32)],
        compiler_params=_cp(("arbitrary",)),
    )(ya0, proj, w, b, dya)


def _sg_norm(vb, ln_w, ln_b):
    v0 = jax.nn.gelu(vb)
    mu = jnp.mean(v0, axis=-1, keepdims=True)
    var = jnp.mean(jnp.square(v0 - mu), axis=-1, keepdims=True)
    return (v0 - mu) * lax.rsqrt(var + EPS) * ln_w + ln_b


def _sg_gate(ub, mixed, zb):
    return jax.nn.gelu(ub) * mixed * jax.nn.silu(zb)


def _sg_specs():
    W = SSM_WIDTH
    blk = lambda off: pl.BlockSpec((CHUNK, W), lambda n: (n, off // W))
    out = pl.BlockSpec((CHUNK, W), lambda n: (n, 0))
    vec = pl.BlockSpec((1, W), lambda n: (0, 0))
    wsp = pl.BlockSpec((SG_HEADS, CHUNK, CHUNK), lambda n: (0, 0, 0))
    bsp = pl.BlockSpec((SG_HEADS, CHUNK, 1), lambda n: (0, 0, 0))
    return blk, out, vec, wsp, bsp


def _sg_masked(w_ref):
    t = lax.broadcasted_iota(jnp.int32, (CHUNK, CHUNK), 0)
    s = lax.broadcasted_iota(jnp.int32, (CHUNK, CHUNK), 1)
    causal = s <= t
    return causal, [jnp.where(causal, w_ref[h], 0.0).astype(BF16) for h in range(SG_HEADS)]


def _sg_mix(wm, vnb, bias_ref):
    return jnp.concatenate(
        [_dot(wm[h], vnb[:, h * CHUNK:(h + 1) * CHUNK]) + bias_ref[h] for h in range(SG_HEADS)], axis=-1)


def _sg_fwd(proj, ln_w, ln_b, w, bias, name):
    L = proj.shape[0]
    blk, out, vec, wsp, bsp = _sg_specs()

    def body(ub_ref, vb_ref, zb_ref, lw_ref, lb_ref, w_ref, bias_ref, o_ref):
        _, wm = _sg_masked(w_ref)
        vnb = _sg_norm(vb_ref[...].astype(F32), lw_ref[...], lb_ref[...]).astype(BF16)
        mixed = _sg_mix(wm, vnb, bias_ref)
        o_ref[...] = _sg_gate(ub_ref[...].astype(F32), mixed, zb_ref[...].astype(F32)).astype(BF16)

    return pl.pallas_call(
        body, name=name, grid=(L // CHUNK,),
        in_specs=[blk(OFF_UB), blk(OFF_VB), blk(OFF_ZB), vec, vec, wsp, bsp], out_specs=out,
        out_shape=jax.ShapeDtypeStruct((L, SSM_WIDTH), BF16), compiler_params=_cp(("parallel",)),
    )(proj, proj, proj, ln_w, ln_b, w, bias)


def _sg_bwd(proj, ln_w, ln_b, w, bias, dyb, name):
    L = proj.shape[0]
    blk, out, vec, wsp, bsp = _sg_specs()

    def body(ub_ref, vb_ref, zb_ref, lw_ref, lb_ref, w_ref, bias_ref, g_ref,
             dub_ref, dvb_ref, dzb_ref, dlw_ref, dlb_ref, dw_ref, dbias_ref):
        causal, wm = _sg_masked(w_ref)
        vb = vb_ref[...].astype(F32)
        vn, vjp_norm = jax.vjp(_sg_norm, vb, lw_ref[...], lb_ref[...])
        vnb = vn.astype(BF16)
        mixed = _sg_mix(wm, vnb, bias_ref)
        _, vjp_gate = jax.vjp(_sg_gate, ub_ref[...].astype(F32), mixed, zb_ref[...].astype(F32))
        dub, dmixed, dzb = vjp_gate(g_ref[...].astype(F32))
        dub_ref[...] = dub.astype(BF16)
        dzb_ref[...] = dzb.astype(BF16)

        @pl.when(pl.program_id(0) == 0)
        def _():
            dlw_ref[...] = jnp.zeros_like(dlw_ref)
            dlb_ref[...] = jnp.zeros_like(dlb_ref)
            dw_ref[...] = jnp.zeros_like(dw_ref)
            dbias_ref[...] = jnp.zeros_like(dbias_ref)

        dvn = []
        for h in range(SG_HEADS):
            dm = dmixed[:, h * CHUNK:(h + 1) * CHUNK]
            dmb = dm.astype(BF16)
            dbias_ref[h] += jnp.sum(dm, axis=-1, keepdims=True)
            dw_ref[h] += jnp.where(causal, _dot_nt(dmb, vnb[:, h * CHUNK:(h + 1) * CHUNK]), 0.0)
            dvn.append(_dot_tn(wm[h], dmb))
        dvb, dlw, dlb = vjp_norm(jnp.concatenate(dvn, axis=-1))
        dvb_ref[...] = dvb.astype(BF16)
        dlw_ref[...] += dlw
        dlb_ref[...] += dlb

    act = jax.ShapeDtypeStruct((L, SSM_WIDTH), BF16)
    return pl.pallas_call(
        body, name=name, grid=(L // CHUNK,),
        in_specs=[blk(OFF_UB), blk(OFF_VB), blk(OFF_ZB), vec, vec, wsp, bsp, out],
        out_specs=[out, out, out, vec, vec, wsp, bsp],
        out_shape=[act, act, act, jax.ShapeDtypeStruct((1, SSM_WIDTH), F32), jax.ShapeDtypeStruct((1, SSM_WIDTH), F32),
                   jax.ShapeDtypeStruct((SG_HEADS, CHUNK, CHUNK), F32), jax.ShapeDtypeStruct((SG_HEADS, CHUNK, 1), F32)],
        compiler_params=_cp(("arbitrary",)),
    )(proj, proj, proj, ln_w, ln_b, w, bias, dyb)


def _rope_tables(L):
    half = ROT_DIM // 2
    inv_freq = ROPE_THETA ** (-jnp.arange(0, ROT_DIM, 2, dtype=F32) / ROT_DIM)
    ang = jnp.arange(L, dtype=F32)[:, None] * inv_freq[None, :]
    cos, sin = jnp.cos(ang), jnp.sin(ang)
    ones = jnp.ones((L, HEAD_DIM - ROT_DIM), F32)
    cos_h = jnp.concatenate([cos, cos, ones], axis=-1)
    sin_h = jnp.concatenate([-sin, sin, 0.0 * ones], axis=-1)
    src = jnp.arange(HEAD_DIM)[:, None]
    dst = jnp.arange(HEAD_DIM)[None, :]
    p_h = (((dst < half) & (src == dst + half)) | ((dst >= half) & (dst < ROT_DIM) & (src == dst - half))).astype(F32)
    p2 = jnp.kron(jnp.eye(2, dtype=F32), p_h).astype(BF16)
    return jnp.tile(cos_h, (1, 2)), jnp.tile(sin_h, (1, 2)), p2


def _rope(t, cos, sin, p2):
    n = t.shape[1] // 128
    tb = t.astype(BF16)
    sw = jnp.concatenate([_dot(tb[:, i * 128:(i + 1) * 128], p2) for i in range(n)], axis=-1) if n > 1 else _dot(tb, p2)
    return t * jnp.tile(cos, (1, n)) + sw * jnp.tile(sin, (1, n))


def _rope_t(g, cos, sin, p2):
    n = g.shape[1] // 128
    gs = (g * jnp.tile(sin, (1, n))).astype(BF16)
    sw = jnp.concatenate([_dot_nt(gs[:, i * 128:(i + 1) * 128], p2) for i in range(n)], axis=-1) if n > 1 else _dot_nt(gs, p2)
    return g * jnp.tile(cos, (1, n)) + sw


def _lane_lo(shape):
    return (lax.broadcasted_iota(jnp.int32, shape, len(shape) - 1) % 128) < HEAD_DIM


def _dup_halves(x):
    xr = pltpu.roll(x, HEAD_DIM, 1)
    lo = _lane_lo(x.shape)
    return jnp.where(lo, x, xr), jnp.where(lo, xr, x)


def _fold_halves(d0, d1):
    f0 = d0 + pltpu.roll(d0, HEAD_DIM, 1)
    f1 = d1 + pltpu.roll(d1, HEAD_DIM, 1)
    return jnp.where(_lane_lo(d0.shape), f0, f1)


def _attn_mask():
    qi = lax.broadcasted_iota(jnp.int32, (CHUNK, 2 * CHUNK), 0)
    kj = lax.broadcasted_iota(jnp.int32, (CHUNK, 2 * CHUNK), 1)
    return qi, kj


def _attn_specs():
    qsp = pl.BlockSpec((CHUNK, 1024), lambda n: (n, OFF_Q // 1024))
    kv_cur = pl.BlockSpec((CHUNK, 256), lambda n: (n, OFF_KV // 256))
    kv_prev = pl.BlockSpec((CHUNK, 256), lambda n: (jnp.maximum(n - 1, 0), OFF_KV // 256))
    zsp = [pl.BlockSpec((CHUNK, 256), functools.partial(lambda n, q: (n, OFF_ZC // 256 + q), q=q)) for q in range(4)]
    tab_cur = pl.BlockSpec((CHUNK, 128), lambda n: (n, 0))
    tab_prev = pl.BlockSpec((CHUNK, 128), lambda n: (jnp.maximum(n - 1, 0), 0))
    p2sp = pl.BlockSpec((128, 128), lambda n: (0, 0))
    sink = pl.BlockSpec(memory_space=pltpu.SMEM)
    wide = pl.BlockSpec((CHUNK, 1024), lambda n: (n, 0))
    return qsp, kv_cur, kv_prev, zsp, tab_cur, tab_prev, p2sp, sink, wide


def _attn_core(n, q_ref, kvc_ref, kvp_ref, cosc_ref, sinc_ref, cosp_ref, sinp_ref, p2_ref, sink_ref):
    p2 = p2_ref[...]
    qr = _rope(q_ref[...].astype(F32), cosc_ref[...], sinc_ref[...], p2).astype(BF16)
    kc = _rope(kvc_ref[:, 0:128].astype(F32), cosc_ref[...], sinc_ref[...], p2)
    kp = _rope(kvp_ref[:, 0:128].astype(F32), cosp_ref[...], sinp_ref[...], p2)
    k_all = jnp.concatenate([kp, kc], axis=0).astype(BF16)
    v_all = jnp.concatenate([kvp_ref[:, 128:256], kvc_ref[:, 128:256]], axis=0)
    kd = _dup_halves(k_all)
    vd = _dup_halves(v_all)
    qi, kj = _attn_mask()
    allowed = ((kj < CHUNK) & (kj > qi) & (n > 0)) | ((kj >= CHUNK) & (kj - CHUNK <= qi))
    lo = _lane_lo((CHUNK, 128))
    probs = []
    for h in range(ATT_HEADS):
        m, half, g = h // 2, h % 2, h // 8
        qp = qr[:, m * 128:(m + 1) * 128]
        qm = jnp.where(lo if half == 0 else ~lo, qp, jnp.zeros_like(qp))
        s = jnp.where(allowed, _dot_nt(qm, kd[g]) * (HEAD_DIM ** -0.5), NEG_INF)
        snk = sink_ref[h]
        mx = jnp.maximum(jnp.max(s, axis=-1, keepdims=True), snk)
        e = jnp.exp(s - mx)
        es = jnp.exp(snk - mx)
        inv = 1.0 / (jnp.sum(e, axis=-1, keepdims=True) + es)
        probs.append((qm, e * inv, es * inv))
    return qr, kd, vd, probs, lo


def _attn_out(vd, probs, lo):
    outs = []
    for m in range(ATT_HEADS // 2):
        g = m // 4
        o0 = _dot(probs[2 * m][1].astype(BF16), vd[g])
        o1 = _dot(probs[2 * m + 1][1].astype(BF16), vd[g])
        outs.append(jnp.where(lo, o0, o1))
    return jnp.concatenate(outs, axis=-1)


def _silu_gate(o, z):
    return o * jax.nn.silu(z)


def _attn_fwd(proj, sinks, tabs, name):
    L = proj.shape[0]
    cos2, sin2, p2 = tabs
    qsp, kv_cur, kv_prev, zsp, tab_cur, tab_prev, p2sp, sink, wide = _attn_specs()

    def body(q_ref, kvc_ref, kvp_ref, z0, z1, z2, z3, cosc, sinc, cosp, sinp, p2_ref, sink_ref, o_ref):
        n = pl.program_id(0)
        _, _, vd, probs, lo = _attn_core(n, q_ref, kvc_ref, kvp_ref, cosc, sinc, cosp, sinp, p2_ref, sink_ref)
        o = _attn_out(vd, probs, lo)
        z = jnp.concatenate([z0[...], z1[...], z2[...], z3[...]], axis=-1).astype(F32)
        o_ref[...] = _silu_gate(o, z).astype(BF16)

    return pl.pallas_call(
        body, name=name, grid=(L // CHUNK,),
        in_specs=[qsp, kv_cur, kv_prev, *zsp, tab_cur, tab_cur, tab_prev, tab_prev, p2sp, sink],
        out_specs=wide, out_shape=jax.ShapeDtypeStruct((L, 1024), BF16), compiler_params=_cp(("parallel",)),
    )(proj, proj, proj, proj, proj, proj, proj, cos2, sin2, cos2, sin2, p2, sinks)


def _attn_bwd(proj, sinks, tabs, dyc, name):
    L = proj.shape[0]
    cos2, sin2, p2 = tabs
    qsp, kv_cur, kv_prev, zsp, tab_cur, tab_prev, p2sp, sink, wide = _attn_specs()
    kvo = pl.BlockSpec((CHUNK, 256), lambda n: (n, 0))

    def body(q_ref, kvc_ref, kvp_ref, z0, z1, z2, z3, cosc, sinc, cosp, sinp, p2_ref, sink_ref, g_ref,
             dq_ref, dz_ref, dkvc_ref, dkvp_ref, dsink_ref):
        n = pl.program_id(0)
        _, kd, vd, probs, lo = _attn_core(n, q_ref, kvc_ref, kvp_ref, cosc, sinc, cosp, sinp, p2_ref, sink_ref)
        o = _attn_out(vd, probs, lo)
        z = jnp.concatenate([z0[...], z1[...], z2[...], z3[...]], axis=-1).astype(F32)
        _, vjp = jax.vjp(_silu_gate, o, z)
        do, dz = vjp(g_ref[...].astype(F32))
        dz_ref[...] = dz.astype(BF16)

        @pl.when(n == 0)
        def _():
            dsink_ref[...] = jnp.zeros_like(dsink_ref)

        dkd = [jnp.zeros((2 * CHUNK, 128), F32), jnp.zeros((2 * CHUNK, 128), F32)]
        dvd = [jnp.zeros((2 * CHUNK, 128), F32), jnp.zeros((2 * CHUNK, 128), F32)]
        dq_pairs = []
        for m in range(ATT_HEADS // 2):
            g = m // 4
            dop = do[:, m * 128:(m + 1) * 128].astype(BF16)
            dq_h = []
            for half in range(2):
                h = 2 * m + half
                qm, p, ps = probs[h]
                dom = jnp.where(lo if half == 0 else ~lo, dop, jnp.zeros_like(dop))
                dp = _dot_nt(dom, vd[g])
                rs = jnp.sum(p * dp, axis=-1, keepdims=True)
                ds = (p * (dp - rs) * (HEAD_DIM ** -0.5)).astype(BF16)
                dsink_ref[h:h + 1, :] += jnp.broadcast_to(jnp.sum(-ps * rs, axis=0, keepdims=True), (1, 128))
                dq_h.append(_dot(ds, kd[g]))
                dkd[g] = dkd[g] + _dot_tn(ds, qm)
                dvd[g] = dvd[g] + _dot_tn(p.astype(BF16), dom)
            dq_pairs.append(jnp.where(lo, dq_h[0], dq_h[1]))
        p2 = p2_ref[...]
        dq_ref[...] = _rope_t(jnp.concatenate(dq_pairs, axis=-1), cosc[...], sinc[...], p2).astype(BF16)
        dk_rot = _fold_halves(dkd[0], dkd[1])
        dv = _fold_halves(dvd[0], dvd[1])
        dkp = _rope_t(dk_rot[0:CHUNK], cosp[...], sinp[...], p2)
        dkc = _rope_t(dk_rot[CHUNK:2 * CHUNK], cosc[...], sinc[...], p2)
        dkvp_ref[...] = jnp.concatenate([dkp, dv[0:CHUNK]], axis=-1)
        dkvc_ref[...] = jnp.concatenate([dkc, dv[CHUNK:2 * CHUNK]], axis=-1)

    act = jax.ShapeDtypeStruct((L, 1024), BF16)
    kvs = jax.ShapeDtypeStruct((L, 256), F32)
    return pl.pallas_call(
        body, name=name, grid=(L // CHUNK,),
        in_specs=[qsp, kv_cur, kv_prev, *zsp, tab_cur, tab_cur, tab_prev, tab_prev, p2sp, sink, wide],
        out_specs=[wide, wide, kvo, kvo, pl.BlockSpec((ATT_HEADS, 128), lambda n: (0, 0))],
        out_shape=[act, act, kvs, kvs, jax.ShapeDtypeStruct((ATT_HEADS, 128), F32)],
        compiler_params=_cp(("arbitrary",)),
    )(proj, proj, proj, proj, proj, proj, proj, cos2, sin2, cos2, sin2, p2, sinks, dyc)


MERGE_TN = 256


def _merge_point(ta, tb, tc, ga, gb, gc):
    return jax.nn.sigmoid(ga) * ta + jax.nn.sigmoid(gb) * tb + jax.nn.sigmoid(gc) * tc


def _merge_specs(tm):
    nj = D_MODEL // MERGE_TN
    t = pl.BlockSpec((tm, MERGE_TN), lambda i, j: (i, j))
    gates = [pl.BlockSpec((tm, MERGE_TN), functools.partial(lambda i, j, b: (i, OFF_G // MERGE_TN + b * nj + j), b=b))
             for b in range(3)]
    return t, gates, nj


def _merge_fwd(ta, tb, tc, proj, name):
    L = ta.shape[0]
    tm = min(L, 1024)
    t, gates, nj = _merge_specs(tm)

    def body(ta_ref, tb_ref, tc_ref, ga_ref, gb_ref, gc_ref, o_ref):
        f = lambda r: r[...].astype(F32)
        o_ref[...] = _merge_point(f(ta_ref), f(tb_ref), f(tc_ref), f(ga_ref), f(gb_ref), f(gc_ref)).astype(BF16)

    return pl.pallas_call(
        body, name=name, grid=(L // tm, nj), in_specs=[t, t, t, *gates], out_specs=t,
        out_shape=jax.ShapeDtypeStruct((L, D_MODEL), BF16), compiler_params=_cp(("parallel", "parallel")),
    )(ta, tb, tc, proj, proj, proj)


def _merge_bwd(ta, tb, tc, proj, dm, name):
    L = ta.shape[0]
    tm = min(L, 1024)
    t, gates, nj = _merge_specs(tm)

    def body(ta_ref, tb_ref, tc_ref, ga_ref, gb_ref, gc_ref, dm_ref, dta_ref, dtb_ref, dtc_ref, dga_ref, dgb_ref, dgc_ref):
        f = lambda r: r[...].astype(F32)
        _, vjp = jax.vjp(_merge_point, f(ta_ref), f(tb_ref), f(tc_ref), f(ga_ref), f(gb_ref), f(gc_ref))
        outs = vjp(f(dm_ref))
        for r, v in zip((dta_ref, dtb_ref, dtc_ref, dga_ref, dgb_ref, dgc_ref), outs):
            r[...] = v.astype(BF16)

    act = jax.ShapeDtypeStruct((L, D_MODEL), BF16)
    return pl.pallas_call(
        body, name=name, grid=(L // tm, nj), in_specs=[t, t, t, *gates, t],
        out_specs=[t] * 6, out_shape=[act] * 6,
        compiler_params=_cp(("parallel", "parallel")),
    )(ta, tb, tc, proj, proj, proj, dm)


GRAD_DT = BF16
SMALL = ("norm_w", "ssm_a_re", "ssm_a_im", "ssm_log_dt", "ssm_b_re", "ssm_b_im", "ssm_c_re", "ssm_c_im", "ssm_d",
         "ssm_glu_b", "sg_ln_w", "sg_ln_b", "sg_w", "sg_b", "attn_sinks")
G8 = SSM_GROUPS // N_SLAB


def _slab_b(bb_t):
    x = bb_t.transpose(1, 0, 2).reshape(N_SLAB, G8, SSM_GROUP, SSM_STATE)
    return jnp.einsum("jgcp,gh->jgchp", x, jnp.eye(G8, dtype=x.dtype)).reshape(N_SLAB, SLAB_CH, SLAB_ST)


def _unslab_b(d):
    x = d.reshape(N_SLAB, G8, SSM_GROUP, G8, SSM_STATE)
    x = jnp.einsum("jgchp,gh->jgcp", x, jnp.eye(G8, dtype=x.dtype))
    return x.reshape(SSM_GROUPS, SSM_GROUP, SSM_STATE).transpose(1, 0, 2)


def _slab_c(c):
    x = c.reshape(N_SLAB, G8, SSM_GROUP, SSM_STATE)
    return jnp.einsum("jgcp,gh->jgphc", x, jnp.eye(G8, dtype=x.dtype)).reshape(N_SLAB, SLAB_ST, SLAB_CH)


def _unslab_c(d):
    x = d.reshape(N_SLAB, G8, SSM_STATE, G8, SSM_GROUP)
    x = jnp.einsum("jgphc,gh->jgcp", x, jnp.eye(G8, dtype=x.dtype))
    return x.reshape(SSM_GROUPS, SSM_GROUP, SSM_STATE)


def _s5_prep(p, tag):
    bt_re = p["ssm_b_re"].transpose(2, 0, 1)
    bt_im = p["ssm_b_im"].transpose(2, 0, 1)
    raw = (p["ssm_a_re"], p["ssm_a_im"], p["ssm_log_dt"][:, None], bt_re, bt_im)
    lr, li, bbr, bbi = _s5_params_fwd(*raw, name=f"s5_params_{tag}")
    ops = (_slab_b(bbr).astype(BF16), _slab_b(bbi).astype(BF16),
           _slab_c(p["ssm_c_re"]).astype(BF16), _slab_c(p["ssm_c_im"]).astype(BF16),
           lr.reshape(N_SLAB, 1, SLAB_ST), li.reshape(N_SLAB, 1, SLAB_ST), p["ssm_d"].reshape(N_SLAB, 1, SLAB_CH))
    return raw, ops


def _layer_fwd(x, p, w, tabs, tag, after_proj=None):
    L = x.shape[0]
    h = _rms_fwd(x, p["norm_w"][None], f"rms_fwd_{tag}")
    proj = _mm(h, w["win_t"], "nt", BF16, L, 256, D_MODEL, f"in_proj_{tag}")
    if after_proj is not None:
        p, w = after_proj(proj)
    s5_raw, s5_ops = _s5_prep(p, tag)
    ya0 = _s5_fwd(proj, *s5_ops, name=f"s5_fwd_{tag}")
    ya = _glu_fwd(ya0, proj, w["glu"], p["ssm_glu_b"][None], f"glu_fwd_{tag}")
    yb = _sg_fwd(proj, p["sg_ln_w"][None], p["sg_ln_b"][None], p["sg_w"], p["sg_b"][:, :, None], f"sg_fwd_{tag}")
    yc = _attn_fwd(proj, p["attn_sinks"], tabs, f"attn_fwd_{tag}")
    ta = _mm(ya, w["wba_t"], "nt", BF16, 1024, 1024, 1024, f"branch_a_{tag}")
    tb = _mm(yb, w["wbb_t"], "nt", BF16, 1024, 1024, 1024, f"branch_b_{tag}")
    tc = _mm(yc, w["wbc_t"], "nt", BF16, 1024, 1024, 1024, f"branch_c_{tag}")
    merged = _merge_fwd(ta, tb, tc, proj, f"merge_fwd_{tag}")
    x_new = _mm(merged, w["wout"], "nn", F32, 1024, 512, D_MODEL, f"out_proj_{tag}", res=x)
    saved = dict(x=x, h=h, proj=proj, s5_raw=s5_raw, s5_ops=s5_ops, ya0=ya0, ya=ya, yb=yb, yc=yc,
                 ta=ta, tb=tb, tc=tc, merged=merged)
    return x_new, saved


def _layer_bwd(dx_out, p, w, tabs, s, tag, before_win=None, after_win=None):
    L = dx_out.shape[0]
    proj = s["proj"]
    big, small = {}, {}
    dmerged = _mm(dx_out, w["wout"], "nt", BF16, 1024, 512, D_MODEL, f"d_merged_{tag}")
    big["wout"] = _mm(s["merged"], dx_out, "tn", GRAD_DT, 512, 1024, L, f"d_wout_{tag}")
    dta, dtb, dtc, dga, dgb, dgc = _merge_bwd(s["ta"], s["tb"], s["tc"], proj, dmerged, f"merge_bwd_{tag}")
    dy = {}
    for br, dt in (("a", dta), ("b", dtb), ("c", dtc)):
        dy[br] = _mm(dt, w[f"wb{br}_t"], "nn", BF16, 1024, 1024, D_MODEL, f"d_y{br}_{tag}")
        big[f"wb{br}_t"] = _mm(dt, s[f"y{br}"], "tn", GRAD_DT, 512, 1024, L, f"d_wb{br}_{tag}")

    dq, dzc, dkvc, dkvp, dsink = _attn_bwd(proj, p["attn_sinks"], tabs, dy["c"], f"attn_bwd_{tag}")
    dkv = dkvc + jnp.concatenate([dkvp[CHUNK:], jnp.zeros((CHUNK, 256), F32)], axis=0)
    small["attn_sinks"] = dsink[:, 0]

    dub, dvb, dzb, dlw, dlb, dsgw, dsgb = _sg_bwd(
        proj, p["sg_ln_w"][None], p["sg_ln_b"][None], p["sg_w"], p["sg_b"][:, :, None], dy["b"], f"sg_bwd_{tag}")
    small.update(sg_ln_w=dlw[0], sg_ln_b=dlb[0], sg_w=dsgw, sg_b=dsgb[:, :, 0])

    dya0, dza, dglu, dglub = _glu_bwd(s["ya0"], proj, w["glu"], p["ssm_glu_b"][None], dy["a"], f"glu_bwd_{tag}")
    big["glu"] = dglu.astype(GRAD_DT)
    small["ssm_glu_b"] = dglub[0]

    dua, dbre, dbim, dcre, dcim, dlr, dli, dd = _s5_bwd(proj, dya0, *s["s5_ops"], name=f"s5_bwd_{tag}")
    da_re, da_im, dlog_dt, dbt_re, dbt_im = _s5_params_bwd(
        *s["s5_raw"], dlr.reshape(SSM_GROUPS, SSM_STATE), dli.reshape(SSM_GROUPS, SSM_STATE),
        _unslab_b(dbre), _unslab_b(dbim), name=f"s5_params_bwd_{tag}")
    small.update(ssm_a_re=da_re, ssm_a_im=da_im, ssm_log_dt=dlog_dt[:, 0],
                 ssm_b_re=dbt_re.transpose(1, 2, 0), ssm_b_im=dbt_im.transpose(1, 2, 0),
                 ssm_c_re=_unslab_c(dcre), ssm_c_im=_unslab_c(dcim), ssm_d=dd.reshape(SSM_WIDTH))

    dproj = jnp.concatenate([dua, dza, dub, dvb, dzb, dq, dkv.astype(BF16), dzc, dga, dgb, dgc], axis=-1)
    if before_win is not None:
        before_win(big)
    big["win_t"] = _mm(dproj, s["h"], "tn", GRAD_DT, 256, D_MODEL, L, f"d_win_{tag}")
    if after_win is not None:
        p = after_win(big)
    dh = _mm(dproj, w["win_t"], "nn", F32, 1024, D_MODEL, 256, f"d_h_{tag}")
    dx_in, dnw = _rms_bwd(s["x"], p["norm_w"][None], dh, dx_out, f"rms_bwd_{tag}")
    small["norm_w"] = dnw[0]
    return dx_in, big, small


def _local_step(x, tgt, small_p, final_w, big_w):
    L = x.shape[0]
    tabs = _rope_tables(L)
    saved = []
    for l in range(DEPTH):
        x, s = _layer_fwd(x, small_p[l], big_w[l], tabs, f"l{l}")
        saved.append(s)
    loss_acc, dx, dfw = _final(x, final_w[None], tgt, "final_norm_loss")
    big_g, small_g = [None] * DEPTH, [None] * DEPTH
    for l in reversed(range(DEPTH)):
        dx, big_g[l], small_g[l] = _layer_bwd(dx, small_p[l], big_w[l], tabs, saved[l], f"l{l}")
    return loss_acc[0, 0], dx, dfw[0], big_g, small_g


MESH = pl.DeviceIdType.MESH
ANY = pl.BlockSpec(memory_space=pl.ANY)
ROW_ALIGN = 16


def _place():
    return lax.axis_index("x"), lax.axis_index("y"), lax.axis_index("c")


HBM = pl.BlockSpec(memory_space=pltpu.HBM)
SEM = pl.BlockSpec(memory_space=pltpu.SEMAPHORE)
EFFECT = pltpu.SideEffectType.DATAFLOW_SIDE_EFFECTING


def _split_start(srcs, lands, n_copies, copies, name):
    n, m, k = len(srcs), len(lands), n_copies

    def body(*refs):
        src_refs, land_refs = refs[:n], refs[n:n + m]
        send_sems, recv_sems, token = refs[n + m:n + m + k], refs[n + m + k:n + m + 2 * k], refs[-1]
        for cp in copies(src_refs, land_refs, send_sems, recv_sems):
            cp.start()
        token[...] = jnp.zeros_like(token)

    ops = list(srcs) + list(lands)
    outs = pl.pallas_call(
        body, name=name,
        out_shape=(*[pltpu.SemaphoreType.DMA(())] * (2 * k),
                   *[pltpu.HBM(a.shape, a.dtype) for a in ops], jax.ShapeDtypeStruct((8, 128), F32)),
        in_specs=[HBM] * (n + m),
        out_specs=(*[SEM] * (2 * k), *[HBM] * (n + m), pl.BlockSpec(memory_space=pltpu.VMEM)),
        input_output_aliases={i: 2 * k + i for i in range(n + m)},
        compiler_params=pltpu.CompilerParams(has_side_effects=EFFECT),
    )(*[pltpu.with_memory_space_constraint(a, pltpu.HBM) for a in ops])
    return (list(outs[:k]), list(outs[k:2 * k]), list(outs[2 * k:2 * k + n]), list(outs[2 * k + n:2 * k + n + m]),
            outs[-1])


def _split_wait(send_sems, recv_sems, srcs, lands, after, copies, name):
    n, m, k = len(srcs), len(lands), len(send_sems)

    def body(*refs):
        src_refs, land_refs = refs[:n], refs[n:n + m]
        for cp in copies(src_refs, land_refs, refs[n + m:n + m + k], refs[n + m + k:n + m + 2 * k]):
            cp.wait_send()
            cp.wait_recv()

    ops = list(srcs) + list(lands)
    outs = pl.pallas_call(
        body, name=name,
        out_shape=tuple(pltpu.HBM(a.shape, a.dtype) for a in ops),
        in_specs=[HBM] * (n + m) + [SEM] * (2 * k) + [ANY],
        out_specs=tuple([HBM] * (n + m)),
        input_output_aliases={i: i for i in range(n + m)},
        compiler_params=pltpu.CompilerParams(has_side_effects=EFFECT),
    )(*ops, *send_sems, *recv_sems, after)
    return list(outs[:n]), list(outs[n:])


def _ag_rows(land_ref, px, py, pc):
    r = land_ref.shape[0] // N_DEV
    start = pl.multiple_of((4 * px + 2 * py + pc) * r, ROW_ALIGN)
    return land_ref.at[pl.ds(start, r), :]


def _ag_copies(src_refs, land_refs, send_sems, recv_sems):
    x, y, c = _place()
    peers = [(x, y, 1 - c), (1 - x, y, c), (x, 1 - y, c), (1 - x, 1 - y, c)]
    return [pltpu.make_async_remote_copy(
        src_ref=_ag_rows(land_refs[a], x, y, c), dst_ref=_ag_rows(land_refs[a], x, y, c),
        send_sem=send_sems[4 * a + k], recv_sem=recv_sems[4 * a + k], device_id=peer, device_id_type=MESH)
        for a in range(len(land_refs)) for k, peer in enumerate(peers)]


def _ag_forward(lands, name):
    n = len(lands)

    def body(*refs):
        land_refs = refs[n:2 * n]
        send_sems, recv_sems = refs[2 * n:]
        x, y, c = _place()
        chips = [(1 - x, y), (x, 1 - y), (1 - x, 1 - y)]

        def copy(a, j, pc):
            px, py = chips[j]
            return pltpu.make_async_remote_copy(
                src_ref=_ag_rows(land_refs[a], px, py, pc), dst_ref=_ag_rows(land_refs[a], px, py, pc),
                send_sem=send_sems.at[a, j], recv_sem=recv_sems.at[a, j], device_id=(x, y, 1 - c), device_id_type=MESH)

        passed = [copy(a, j, c) for a in range(n) for j in range(3)]
        for cp in passed:
            cp.start()
        for a in range(n):
            for j in range(3):
                copy(a, j, 1 - c).wait_recv()
        for cp in passed:
            cp.wait_send()

    return pl.pallas_call(
        body, name=name,
        in_specs=[ANY] * n, out_specs=[ANY] * n,
        out_shape=[jax.ShapeDtypeStruct(l.shape, l.dtype) for l in lands],
        input_output_aliases={i: i for i in range(n)},
        scratch_shapes=[pltpu.SemaphoreType.DMA((n, 3)), pltpu.SemaphoreType.DMA((n, 3))],
    )(*lands)


def _allgather_start(shards, name):
    x, y, c = _place()
    lands = [lax.dynamic_update_slice(lax.empty((N_DEV * s.shape[0], s.shape[1]), s.dtype), s,
                                      ((4 * x + 2 * y + c) * s.shape[0], 0)) for s in shards]
    return _split_start([], lands, 4 * len(lands), _ag_copies, name + "_start")


def _allgather_finish(started, after, name):
    send_sems, recv_sems, _, lands, _ = started
    _, lands = _split_wait(send_sems, recv_sems, [], lands, after, _ag_copies, name + "_wait")
    return list(_ag_forward(lands, name + "_forward"))


def _rs_swap_cores(grads, name):
    n = len(grads)

    def body(*refs):
        ins, outs = refs[:n], refs[n:2 * n]
        send_sems, recv_sems = refs[2 * n:]
        x, y, c = _place()
        cps = []
        for a in range(n):
            r = ins[a].shape[0] // N_DEV
            for q in range(4):
                start = pl.multiple_of((2 * q + 1 - c) * r, ROW_ALIGN)
                cps.append(pltpu.make_async_remote_copy(
                    src_ref=ins[a].at[pl.ds(start, r), :], dst_ref=outs[a].at[q],
                    send_sem=send_sems.at[a, q], recv_sem=recv_sems.at[a, q],
                    device_id=(x, y, 1 - c), device_id_type=MESH))
        for cp in cps:
            cp.start()
        for cp in cps:
            cp.wait()

    return pl.pallas_call(
        body, name=name, in_specs=[ANY] * n, out_specs=[ANY] * n,
        out_shape=[jax.ShapeDtypeStruct((4, g.shape[0] // N_DEV, g.shape[1]), g.dtype) for g in grads],
        scratch_shapes=[pltpu.SemaphoreType.DMA((n, 4)), pltpu.SemaphoreType.DMA((n, 4))],
    )(*grads)


def _rs_chip_copies(sum_refs, land_refs, send_sems, recv_sems):
    x, y, c = _place()
    chips = [(1 - x, y), (x, 1 - y), (1 - x, 1 - y)]
    return [pltpu.make_async_remote_copy(
        src_ref=sum_refs[a].at[2 * px + py], dst_ref=land_refs[a].at[2 * x + y],
        send_sem=send_sems[3 * a + j], recv_sem=recv_sems[3 * a + j], device_id=(px, py, c), device_id_type=MESH)
        for a in range(len(sum_refs)) for j, (px, py) in enumerate(chips)]


def _row_tile(r):
    return max(t for t in range(ROW_ALIGN, min(r, 1024) + 1, ROW_ALIGN) if r % t == 0)


def _rs_add_cores(grad, recv, cidx, name):
    r, cols = recv.shape[1], recv.shape[2]
    tr = _row_tile(r)
    nb = r // tr

    def body(c_ref, g_ref, r_ref, o_ref):
        o_ref[...] = (g_ref[...].astype(F32) + r_ref[...].astype(F32)).astype(o_ref.dtype)

    return pl.pallas_call(
        body, name=name,
        grid_spec=pltpu.PrefetchScalarGridSpec(
            num_scalar_prefetch=1, grid=(4, nb),
            in_specs=[pl.BlockSpec((tr, cols), lambda q, i, c_ref: ((2 * q + c_ref[0]) * nb + i, 0)),
                      pl.BlockSpec((None, tr, cols), lambda q, i, c_ref: (q, i, 0))],
            out_specs=pl.BlockSpec((None, tr, cols), lambda q, i, c_ref: (q, i, 0))),
        out_shape=jax.ShapeDtypeStruct(recv.shape, recv.dtype),
        compiler_params=_cp(("parallel", "parallel")),
    )(cidx, grad, recv)


def _rs_add_chips(own, recv, slots, name):
    r, cols = recv.shape[1], recv.shape[2]
    tr = _row_tile(r)

    def body(s_ref, o_ref, r0_ref, r1_ref, r2_ref, out_ref):
        acc = o_ref[...].astype(F32)
        for ref in (r0_ref, r1_ref, r2_ref):
            acc = acc + ref[...].astype(F32)
        out_ref[...] = acc

    pick = lambda k: pl.BlockSpec((None, tr, cols), functools.partial(lambda i, s_ref, k: (s_ref[k], i, 0), k=k))
    return pl.pallas_call(
        body, name=name,
        grid_spec=pltpu.PrefetchScalarGridSpec(
            num_scalar_prefetch=1, grid=(r // tr,),
            in_specs=[pick(0), pick(1), pick(2), pick(3)],
            out_specs=pl.BlockSpec((tr, cols), lambda i, s_ref: (i, 0))),
        out_shape=jax.ShapeDtypeStruct((r, cols), F32),
        compiler_params=_cp(("parallel",)),
    )(slots, own, recv, recv, recv)


def _reduce_scatter_start(grads, tag):
    cidx = lax.axis_index("c").astype(jnp.int32)[None]
    recv = _rs_swap_cores(grads, f"rs_swap_cores_{tag}")
    sums = [_rs_add_cores(g, rv, cidx, f"rs_add_cores_{tag}_{i}") for i, (g, rv) in enumerate(zip(grads, recv))]
    lands = [lax.empty(s.shape, s.dtype) for s in sums]
    return _split_start(sums, lands, 3 * len(sums), _rs_chip_copies, f"rs_chips_{tag}_start")


def _reduce_scatter_finish(started, after, tag):
    send_sems, recv_sems, sums, lands, _ = started
    sums, lands = _split_wait(send_sems, recv_sems, sums, lands, after, _rs_chip_copies, f"rs_chips_{tag}_wait")
    x, y = lax.axis_index("x"), lax.axis_index("y")
    slots = jnp.stack([2 * x + y, 2 * (1 - x) + y, 2 * x + 1 - y, 2 * (1 - x) + 1 - y]).astype(jnp.int32)
    return [_rs_add_chips(s, l, slots, f"rs_add_chips_{tag}_{i}") for i, (s, l) in enumerate(zip(sums, lands))]


def _allreduce_small(pack, name):
    R, C = pack.shape
    rs = R // N_DEV
    assert R % (8 * N_DEV) == 0

    def body(p_ref, o_ref, parts, send1, recv1, send2, recv2):
        x, y, c = _place()
        me = 4 * x + 2 * y + c

        def block(ref, d):
            return ref.at[pl.ds(pl.multiple_of(d * rs, 8), rs), :]

        peers = [(1 - x if k & 4 else x, 1 - y if k & 2 else y, 1 - c if k & 1 else c) for k in range(1, N_DEV)]
        scatter = [pltpu.make_async_remote_copy(
            src_ref=block(p_ref, 4 * px + 2 * py + pc), dst_ref=parts.at[me], send_sem=send1.at[k], recv_sem=recv1.at[k],
            device_id=(px, py, pc), device_id_type=MESH) for k, (px, py, pc) in enumerate(peers)]
        for cp in scatter:
            cp.start()
        parts[me] = block(p_ref, me)[...]
        for cp in scatter:
            cp.wait()
        acc = parts[0]
        for d in range(1, N_DEV):
            acc = acc + parts[d]
        block(o_ref, me)[...] = acc
        gather = [pltpu.make_async_remote_copy(
            src_ref=block(o_ref, me), dst_ref=block(o_ref, me), send_sem=send2.at[k], recv_sem=recv2.at[k],
            device_id=peer, device_id_type=MESH) for k, peer in enumerate(peers)]
        for cp in gather:
            cp.start()
        for k, (px, py, pc) in enumerate(peers):
            pltpu.make_async_remote_copy(
                src_ref=block(o_ref, 4 * px + 2 * py + pc), dst_ref=block(o_ref, 4 * px + 2 * py + pc),
                send_sem=send2.at[k], recv_sem=recv2.at[k], device_id=(px, py, pc), device_id_type=MESH).wait_recv()
        for cp in gather:
            cp.wait_send()

    sems = pltpu.SemaphoreType.DMA((N_DEV - 1,))
    return pl.pallas_call(
        body, name=name,
        in_specs=[pl.BlockSpec(memory_space=pltpu.VMEM)], out_specs=pl.BlockSpec(memory_space=pltpu.VMEM),
        out_shape=jax.ShapeDtypeStruct((R, C), F32),
        scratch_shapes=[pltpu.VMEM((N_DEV, rs, C), F32), sems, sems, sems, sems],
        compiler_params=pltpu.CompilerParams(vmem_limit_bytes=VMEM_LIMIT),
    )(pack)


def _adamw(w, g, m, v, name):
    rows, cols = w.shape
    tr = 256 if rows % 256 == 0 else rows
    c1 = 1.0 - ADAM_B1 ** ADAM_STEP
    c2 = 1.0 - ADAM_B2 ** ADAM_STEP

    def body(w_ref, g_ref, m_ref, v_ref, d_ref, nm_ref, nv_ref):
        gv = g_ref[...]
        nm = ADAM_B1 * m_ref[...] + (1.0 - ADAM_B1) * gv
        nv = ADAM_B2 * v_ref[...] + (1.0 - ADAM_B2) * jnp.square(gv)
        d_ref[...] = -ADAM_LR * ((nm / c1) / (jnp.sqrt(nv / c2) + ADAM_EPS) + ADAM_WD * w_ref[...])
        nm_ref[...] = nm
        nv_ref[...] = nv

    blk = pl.BlockSpec((tr, cols), lambda i: (i, 0))
    sh = jax.ShapeDtypeStruct((rows, cols), F32)
    return pl.pallas_call(
        body, name=name, grid=(rows // tr,), in_specs=[blk] * 4, out_specs=[blk] * 3, out_shape=[sh] * 3,
        compiler_params=_cp(("parallel",)),
    )(w, g, m, v)


WEIGHTS = ("norm_w", "w_in", "ssm_a_re", "ssm_a_im", "ssm_log_dt", "ssm_b_re", "ssm_b_im", "ssm_c_re", "ssm_c_im",
           "ssm_d", "ssm_glu_w", "ssm_glu_b", "sg_ln_w", "sg_ln_b", "sg_w", "sg_b", "attn_sinks",
           "w_branch_a", "w_branch_b", "w_branch_c", "w_out", "final_norm_w")
BIG = ("w_in", "ssm_glu_w", "w_branch_a", "w_branch_b", "w_branch_c", "w_out")
BIG_KEY = {"w_in": ("win_t", True), "ssm_glu_w": ("glu", False), "w_branch_a": ("wba_t", True),
           "w_branch_b": ("wbb_t", True), "w_branch_c": ("wbc_t", True), "w_out": ("wout", False)}
PACK_COLS = 1024


def _pack(arrs):
    flat = jnp.concatenate([a.reshape(-1) for a in arrs])
    rows = -(-flat.shape[0] // (8 * N_DEV * PACK_COLS)) * 8 * N_DEV
    return jnp.pad(flat, (0, rows * PACK_COLS - flat.shape[0])).reshape(rows, PACK_COLS)


def _unpack(pack, like):
    flat = pack.reshape(-1)
    out, off = [], 0
    for a in like:
        out.append(flat[off:off + a.size].reshape(a.shape))
        off += a.size
    return out


def kernel(x, norm_w, w_in, ssm_a_re, ssm_a_im, ssm_log_dt, ssm_b_re, ssm_b_im, ssm_c_re, ssm_c_im, ssm_d, ssm_glu_w, ssm_glu_b, sg_ln_w, sg_ln_b, sg_w, sg_b, attn_sinks, w_branch_a, w_branch_b, w_branch_c, w_out, final_norm_w, loss_target, m_norm_w, m_w_in, m_ssm_a_re, m_ssm_a_im, m_ssm_log_dt, m_ssm_b_re, m_ssm_b_im, m_ssm_c_re, m_ssm_c_im, m_ssm_d, m_ssm_glu_w, m_ssm_glu_b, m_sg_ln_w, m_sg_ln_b, m_sg_w, m_sg_b, m_attn_sinks, m_w_branch_a, m_w_branch_b, m_w_branch_c, m_w_out, m_final_norm_w, v_norm_w, v_w_in, v_ssm_a_re, v_ssm_a_im, v_ssm_log_dt, v_ssm_b_re, v_ssm_b_im, v_ssm_c_re, v_ssm_c_im, v_ssm_d, v_ssm_glu_w, v_ssm_glu_b, v_sg_ln_w, v_sg_ln_b, v_sg_w, v_sg_b, v_attn_sinks, v_w_branch_a, v_w_branch_b, v_w_branch_c, v_w_out, v_final_norm_w):
    w = dict(zip(WEIGHTS, (norm_w, w_in, ssm_a_re, ssm_a_im, ssm_log_dt, ssm_b_re, ssm_b_im, ssm_c_re, ssm_c_im, ssm_d, ssm_glu_w, ssm_glu_b, sg_ln_w, sg_ln_b, sg_w, sg_b, attn_sinks, w_branch_a, w_branch_b, w_branch_c, w_out, final_norm_w)))
    m = dict(zip(WEIGHTS, (m_norm_w, m_w_in, m_ssm_a_re, m_ssm_a_im, m_ssm_log_dt, m_ssm_b_re, m_ssm_b_im, m_ssm_c_re, m_ssm_c_im, m_ssm_d, m_ssm_glu_w, m_ssm_glu_b, m_sg_ln_w, m_sg_ln_b, m_sg_w, m_sg_b, m_attn_sinks, m_w_branch_a, m_w_branch_b, m_w_branch_c, m_w_out, m_final_norm_w)))
    v = dict(zip(WEIGHTS, (v_norm_w, v_w_in, v_ssm_a_re, v_ssm_a_im, v_ssm_log_dt, v_ssm_b_re, v_ssm_b_im, v_ssm_c_re, v_ssm_c_im, v_ssm_d, v_ssm_glu_w, v_ssm_glu_b, v_sg_ln_w, v_sg_ln_b, v_sg_w, v_sg_b, v_attn_sinks, v_w_branch_a, v_w_branch_b, v_w_branch_c, v_w_out, v_final_norm_w)))

    keys = [BIG_KEY[n][0] for n in BIG]
    shards = [[(w[n][l].T if BIG_KEY[n][1] else w[n][l]).astype(BF16) for n in BIG] for l in range(DEPTH)]
    small_p = [{n: w[n][l] for n in SMALL} for l in range(DEPTH)]
    xv, tgt = x[0], loss_target[0]
    tabs = _rope_tables(xv.shape[0])

    ag0a = _allgather_start(shards[0][:1], "ag_l0_win")
    win0 = _allgather_finish(ag0a, ag0a[4], "ag_l0_win")[0]
    shards[0][1] = shards[0][1] + (win0[0, 0] * 0).astype(BF16)
    ag0b = _allgather_start(shards[0][1:], "ag_l0_rest")
    p0 = dict(small_p[0], norm_w=small_p[0]["norm_w"] + ag0b[4][0, 0])
    started = {}

    def after_proj0(proj):
        w0 = dict(zip(keys, [win0] + _allgather_finish(ag0b, proj, "ag_l0_rest")))
        shards[1][1] = shards[1][1] + (w0["glu"][0, 0] * 0).astype(BF16)
        started["ag1"] = _allgather_start(shards[1], "ag_l1")
        started["w0"] = w0
        return dict(p0, ssm_d=p0["ssm_d"] + started["ag1"][4][0, 0]), w0

    x1, saved0 = _layer_fwd(xv, p0, {"win_t": win0}, tabs, "l0", after_proj=after_proj0)
    big_w0 = started["w0"]
    big_w1 = dict(zip(keys, _allgather_finish(started["ag1"], x1, "ag_l1")))
    x2, saved1 = _layer_fwd(x1, small_p[1], big_w1, tabs, "l1")
    loss_acc, dx2, dfw = _final(x2, w["final_norm_w"][None], tgt, "final_norm_loss")
    loss = lax.psum(loss_acc[0, 0], ("x", "y", "c"))
    dfw = dfw[0]

    dx1, big_g1, small_g1 = _layer_bwd(dx2, small_p[1], big_w1, tabs, saved1, "l1")
    rs1 = _reduce_scatter_start([big_g1[k] for k in keys], "l1")
    p0_bwd = dict(small_p[0], attn_sinks=small_p[0]["attn_sinks"] + rs1[4][0, 0])

    def before_win0(big):
        started["rs0b"] = _reduce_scatter_start([big[k] for k in keys[1:]], "l0_rest")

    def after_win0(big):
        started["rs0a"] = _reduce_scatter_start([big["win_t"]], "l0_win")
        return dict(p0_bwd, norm_w=p0_bwd["norm_w"] + started["rs0a"][4][0, 0])

    dx, big_g0, small_g0 = _layer_bwd(dx1, p0_bwd, big_w0, tabs, saved0, "l0", before_win=before_win0, after_win=after_win0)
    red1 = _reduce_scatter_finish(rs1, dx, "l1")
    red0 = (_reduce_scatter_finish(started["rs0a"], dx, "l0_win")
            + _reduce_scatter_finish(started["rs0b"], dx, "l0_rest"))
    small_g = [small_g0, small_g1]

    grads = {}
    for i, n in enumerate(BIG):
        grads[n] = jnp.stack([g.T if BIG_KEY[n][1] else g for g in (red0[i], red1[i])])

    small_names = [n for n in WEIGHTS if n not in BIG]
    small_list = [jnp.stack([small_g[l][n] for l in range(DEPTH)]) if n != "final_norm_w" else dfw for n in small_names]
    red_small = _allreduce_small(_pack(small_list), "allreduce_small")
    for n, g in zip(small_names, _unpack(red_small, small_list)):
        grads[n] = g

    delta, new_m, new_v = {}, {}, {}
    for n in BIG:
        shp = w[n].shape
        two_d = lambda a: a.reshape(-1, shp[-1])
        d_, m_, v_ = _adamw(two_d(w[n]), two_d(grads[n]), two_d(m[n]), two_d(v[n]), f"adamw_{n}")
        delta[n], new_m[n], new_v[n] = d_.reshape(shp), m_.reshape(shp), v_.reshape(shp)
    packs = [_pack([d[n] for n in small_names]) for d in (w, m, v)]
    outs = _adamw(packs[0], red_small, packs[1], packs[2], "adamw_small")
    for res, o in zip((delta, new_m, new_v), outs):
        for n, a in zip(small_names, _unpack(o, small_list)):
            res[n] = a

    return (loss, dx[None], *[grads[n] for n in WEIGHTS], *[delta[n] for n in WEIGHTS],
            *[new_m[n] for n in WEIGHTS], *[new_v[n] for n in WEIGHTS])
```

```python
import functools
import math

import jax
import jax.numpy as jnp
from jax import lax
from jax.experimental import pallas as pl
from jax.experimental.pallas import tpu as pltpu

F32 = jnp.float32
BF16 = jnp.bfloat16

D_MODEL = 2048
DEPTH = 2
EPS = 1e-6
NEG_INF = -1e30
N_DEV = 8

SSM_WIDTH = 1024
SSM_GROUP = 16
SSM_GROUPS = 64
SSM_STATE = 64
N_SLAB = 8
SLAB_CH = 128
SLAB_ST = 512
N_SEG = 8
SEG_PAD = 8

SG_HEADS = 8
CHUNK = 128
HEAD_DIM = 64
ATT_HEADS = 16
ROT_DIM = 16
ROPE_THETA = 500000.0

D_IN = 13568
OFF_UA, OFF_ZA, OFF_UB, OFF_VB, OFF_ZB, OFF_Q, OFF_KV, OFF_ZC, OFF_G = (
    0, 1024, 2048, 3072, 4096, 5120, 6144, 6400, 7424)

ADAM_LR, ADAM_B1, ADAM_B2, ADAM_EPS, ADAM_WD, ADAM_STEP = 0.001, 0.9, 0.999, 1e-08, 0.01, 10

VMEM_LIMIT = 56 * 1024 * 1024


def _cp(sem=None):
    return pltpu.CompilerParams(dimension_semantics=sem, vmem_limit_bytes=VMEM_LIMIT)


def _dot(a, b):
    return jnp.dot(a, b, preferred_element_type=F32)


def _dot_nt(a, b):
    return lax.dot_general(a, b, (((1,), (1,)), ((), ())), preferred_element_type=F32)


def _dot_tn(a, b):
    return lax.dot_general(a, b, (((0,), (0,)), ((), ())), preferred_element_type=F32)


def _mm(a, b, mode, out_dtype, tm, tn, tk, name, res=None, after=None):
    if mode == "nn":
        (m, k), (_, n) = a.shape, b.shape
    elif mode == "nt":
        (m, k), (n, _) = a.shape, b.shape
    else:
        (k, m), (_, n) = a.shape, b.shape
    tm, tn, tk = min(tm, m), min(tn, n), min(tk, k)
    assert m % tm == 0 and n % tn == 0 and k % tk == 0, (name, m, n, k, tm, tn, tk)
    nk = k // tk
    a_spec = {"nn": pl.BlockSpec((tm, tk), lambda i, j, kk: (i, kk)),
              "nt": pl.BlockSpec((tm, tk), lambda i, j, kk: (i, kk)),
              "tn": pl.BlockSpec((tk, tm), lambda i, j, kk: (kk, i))}[mode]
    b_spec = {"nn": pl.BlockSpec((tk, tn), lambda i, j, kk: (kk, j)),
              "nt": pl.BlockSpec((tn, tk), lambda i, j, kk: (j, kk)),
              "tn": pl.BlockSpec((tk, tn), lambda i, j, kk: (kk, j))}[mode]
    dot = {"nn": _dot, "nt": _dot_nt, "tn": _dot_tn}[mode]
    has_res = res is not None

    def body(*refs):
        if after is not None:
            refs = refs[:-3] + refs[-2:]
        if has_res:
            a_ref, b_ref, r_ref, o_ref, acc = refs
        else:
            a_ref, b_ref, o_ref, acc = refs
        kk = pl.program_id(2)

        @pl.when(kk == 0)
        def _():
            acc[...] = jnp.zeros_like(acc)

        acc[...] += dot(a_ref[...].astype(BF16), b_ref[...].astype(BF16))

        @pl.when(kk == nk - 1)
        def _():
            r = acc[...]
            if has_res:
                r = r + r_ref[...]
            o_ref[...] = r.astype(out_dtype)

    in_specs = [a_spec, b_spec]
    args = [a, b]
    if has_res:
        in_specs.append(pl.BlockSpec((tm, tn), lambda i, j, kk: (i, j)))
        args.append(res)
    if after is not None:
        in_specs.append(pl.BlockSpec(memory_space=pl.ANY))
        args.append(after)
    return pl.pallas_call(
        body, name=name,
        grid=(m // tm, n // tn, nk),
        in_specs=in_specs,
        out_specs=pl.BlockSpec((tm, tn), lambda i, j, kk: (i, j)),
        out_shape=jax.ShapeDtypeStruct((m, n), out_dtype),
        scratch_shapes=[pltpu.VMEM((tm, tn), F32)],
        compiler_params=_cp(("parallel", "parallel", "arbitrary")),
    )(*args)


def _rms(x, w):
    return x * lax.rsqrt(jnp.mean(x * x, axis=-1, keepdims=True) + EPS) * w


def _rms_fwd(x, w, name):
    L, D = x.shape
    tm = min(L, 256)

    def body(x_ref, w_ref, h_ref):
        h_ref[...] = _rms(x_ref[...], w_ref[...]).astype(BF16)

    return pl.pallas_call(
        body, name=name, grid=(L // tm,),
        in_specs=[pl.BlockSpec((tm, D), lambda i: (i, 0)), pl.BlockSpec((1, D), lambda i: (0, 0))],
        out_specs=pl.BlockSpec((tm, D), lambda i: (i, 0)),
        out_shape=jax.ShapeDtypeStruct((L, D), BF16),
        compiler_params=_cp(("parallel",)),
    )(x, w)


def _rms_bwd(x, w, dh, dres, name):
    L, D = x.shape
    tm = min(L, 256)

    def body(x_ref, w_ref, dh_ref, dres_ref, dx_ref, dw_ref):
        _, vjp = jax.vjp(_rms, x_ref[...], w_ref[...])
        dx, dw = vjp(dh_ref[...])
        dx_ref[...] = dx + dres_ref[...]

        @pl.when(pl.program_id(0) == 0)
        def _():
            dw_ref[...] = jnp.zeros_like(dw_ref)

        dw_ref[...] += dw

    row = pl.BlockSpec((tm, D), lambda i: (i, 0))
    vec = pl.BlockSpec((1, D), lambda i: (0, 0))
    return pl.pallas_call(
        body, name=name, grid=(L // tm,),
        in_specs=[row, vec, row, row],
        out_specs=[row, vec],
        out_shape=[jax.ShapeDtypeStruct((L, D), F32), jax.ShapeDtypeStruct((1, D), F32)],
        compiler_params=_cp(("arbitrary",)),
    )(x, w, dh, dres)


def _final(x, fw, tgt, name):
    L, D = x.shape
    tm = min(L, 256)

    def loss_fn(xv, wv, tv):
        err = _rms(xv, wv) - tv
        return jnp.sum(err * err) * (0.5 / D)

    def body(x_ref, w_ref, t_ref, loss_ref, dx_ref, dw_ref):
        tv = t_ref[...]
        val, vjp = jax.vjp(lambda a, b: loss_fn(a, b, tv), x_ref[...], w_ref[...])
        dx, dw = vjp(jnp.ones((), F32))
        dx_ref[...] = dx

        @pl.when(pl.program_id(0) == 0)
        def _():
            dw_ref[...] = jnp.zeros_like(dw_ref)
            loss_ref[...] = jnp.zeros_like(loss_ref)

        dw_ref[...] += dw
        loss_ref[...] += jnp.full(loss_ref.shape, val, F32)

    row = pl.BlockSpec((tm, D), lambda i: (i, 0))
    vec = pl.BlockSpec((1, D), lambda i: (0, 0))
    return pl.pallas_call(
        body, name=name, grid=(L // tm,),
        in_specs=[row, vec, row],
        out_specs=[pl.BlockSpec((8, 128), lambda i: (0, 0)), row, vec],
        out_shape=[jax.ShapeDtypeStruct((8, 128), F32), jax.ShapeDtypeStruct((L, D), F32),
                   jax.ShapeDtypeStruct((1, D), F32)],
        compiler_params=_cp(("arbitrary",)),
    )(x, fw, tgt)


def _s5_param_fn(a_re, a_im, log_dt, bt_re, bt_im):
    dt = jnp.exp(log_dt)
    zr, zi = a_re * dt, a_im * dt
    er = jnp.exp(zr)
    lr, li = er * jnp.cos(zi), er * jnp.sin(zi)
    nr, ni = lr - 1.0, li
    den = a_re * a_re + a_im * a_im
    cr = (nr * a_re + ni * a_im) / den
    ci = (ni * a_re - nr * a_im) / den
    bbr = cr[None] * bt_re - ci[None] * bt_im
    bbi = cr[None] * bt_im + ci[None] * bt_re
    return lr, li, bbr, bbi


def _s5_params_fwd(a_re, a_im, log_dt, bt_re, bt_im, name):
    def body(ar, ai, ld, br, bi, lr, li, bbr, bbi):
        o = _s5_param_fn(ar[...], ai[...], ld[...], br[...], bi[...])
        lr[...], li[...], bbr[...], bbi[...] = o

    gp = jax.ShapeDtypeStruct(a_re.shape, F32)
    cgp = jax.ShapeDtypeStruct(bt_re.shape, F32)
    return pl.pallas_call(body, name=name, out_shape=[gp, gp, cgp, cgp])(a_re, a_im, log_dt, bt_re, bt_im)


def _s5_params_bwd(a_re, a_im, log_dt, bt_re, bt_im, dlr, dli, dbbr, dbbi, name):
    def body(ar, ai, ld, br, bi, g0, g1, g2, g3, o0, o1, o2, o3, o4):
        _, vjp = jax.vjp(_s5_param_fn, ar[...], ai[...], ld[...], br[...], bi[...])
        o0[...], o1[...], o2[...], o3[...], o4[...] = vjp((g0[...], g1[...], g2[...], g3[...]))

    gp = jax.ShapeDtypeStruct(a_re.shape, F32)
    cgp = jax.ShapeDtypeStruct(bt_re.shape, F32)
    return pl.pallas_call(body, name=name,
                          out_shape=[gp, gp, jax.ShapeDtypeStruct(log_dt.shape, F32), cgp, cgp])(
        a_re, a_im, log_dt, bt_re, bt_im, dlr, dli, dbbr, dbbi)


def _cmul(ar, ai, br, bi):
    return ar * br - ai * bi, ar * bi + ai * br


def _cpow(lr, li, n):
    rr, ri = None, None
    br, bi = lr, li
    while n:
        if n & 1:
            rr, ri = (br, bi) if rr is None else _cmul(rr, ri, br, bi)
        n >>= 1
        if n:
            br, bi = _cmul(br, bi, br, bi)
    return rr, ri


def _shift_rows(x, up):
    row = lax.broadcasted_iota(jnp.int32, x.shape, 0)
    if up:
        return jnp.where(row == N_SEG - 1, 0.0, pltpu.roll(x, N_SEG - 1, 0))
    return jnp.where(row == 0, 0.0, pltpu.roll(x, 1, 0))


def _seg_scan(s_re, s_im, lam, seg, reverse):
    stride = seg + SEG_PAD
    nt = SLAB_ST // 128
    lam_t = [(jnp.broadcast_to(lam[0][:, j * 128:(j + 1) * 128], (N_SEG, 128)),
              jnp.broadcast_to(lam[1][:, j * 128:(j + 1) * 128], (N_SEG, 128))) for j in range(nt)]

    def rows(i):
        return pl.ds(i, N_SEG, stride=stride)

    def step1(t, carry):
        i = seg - 1 - t if reverse else t
        out = []
        for j in range(nt):
            cr, ci = carry[2 * j], carry[2 * j + 1]
            nr, ni = _cmul(lam_t[j][0], lam_t[j][1], cr, ci)
            nr = nr + s_re[j, rows(i), :]
            ni = ni + s_im[j, rows(i), :]
            s_re[j, rows(i), :] = nr
            s_im[j, rows(i), :] = ni
            out += [nr, ni]
        return tuple(out)

    zero = tuple(jnp.zeros((N_SEG, 128), F32) for _ in range(2 * nt))
    ends = lax.fori_loop(0, seg, step1, zero)

    carries = []
    for j in range(nt):
        pr, pi = _cpow(lam_t[j][0], lam_t[j][1], seg)
        cr, ci = jnp.zeros((N_SEG, 128), F32), jnp.zeros((N_SEG, 128), F32)
        for _ in range(N_SEG - 1):
            tr, ti = _cmul(pr, pi, cr, ci)
            cr = _shift_rows(tr + ends[2 * j], reverse)
            ci = _shift_rows(ti + ends[2 * j + 1], reverse)
        carries += [cr, ci]

    def step2(t, pw):
        i = seg - 1 - t if reverse else t
        out = []
        for j in range(nt):
            pr, pi = pw[2 * j], pw[2 * j + 1]
            ar, ai = _cmul(pr, pi, carries[2 * j], carries[2 * j + 1])
            s_re[j, rows(i), :] = s_re[j, rows(i), :] + ar
            s_im[j, rows(i), :] = s_im[j, rows(i), :] + ai
            qr, qi = _cmul(pr, pi, lam_t[j][0], lam_t[j][1])
            out += [qr, qi]
        return tuple(out)

    lax.fori_loop(0, seg, step2, tuple(x for j in range(nt) for x in lam_t[j]))
    return carries


def _seg_rows(ref, k, seg):
    stride = seg + SEG_PAD
    return jnp.concatenate([ref[j, pl.ds(k * stride, seg), :] for j in range(SLAB_ST // 128)], axis=-1)


def _seg_store(ref, k, seg, val):
    stride = seg + SEG_PAD
    for j in range(SLAB_ST // 128):
        ref[j, pl.ds(k * stride, seg), :] = val[:, j * 128:(j + 1) * 128]


def _s5_specs(L):
    col = lambda off: pl.BlockSpec((L, SLAB_CH), lambda j: (0, off + j))
    mat_b = pl.BlockSpec((None, SLAB_CH, SLAB_ST), lambda j: (j, 0, 0))
    mat_c = pl.BlockSpec((None, SLAB_ST, SLAB_CH), lambda j: (j, 0, 0))
    vec_s = pl.BlockSpec((None, 1, SLAB_ST), lambda j: (j, 0, 0))
    vec_c = pl.BlockSpec((None, 1, SLAB_CH), lambda j: (j, 0, 0))
    return col, mat_b, mat_c, vec_s, vec_c


def _s5_states(u_ref, bre_ref, bim_ref, lam, s_re, s_im, seg):
    for k in range(N_SEG):
        uk = u_ref[pl.ds(k * seg, seg), :]
        _seg_store(s_re, k, seg, _dot(uk, bre_ref[...]))
        _seg_store(s_im, k, seg, _dot(uk, bim_ref[...]))
    return _seg_scan(s_re, s_im, lam, seg, reverse=False)


def _s5_fwd(proj, bre, bim, cre_t, cim_t, lam_re, lam_im, dvec, name):
    L = proj.shape[0]
    seg = L // N_SEG
    col, mat_b, mat_c, vec_s, vec_c = _s5_specs(L)
    rows = N_SEG * (seg + SEG_PAD)

    def body(u_ref, bre_ref, bim_ref, cre_ref, cim_ref, lr_ref, li_ref, d_ref, y_ref, s_re, s_im):
        _s5_states(u_ref, bre_ref, bim_ref, (lr_ref[...], li_ref[...]), s_re, s_im, seg)
        for k in range(N_SEG):
            y = (_dot(_seg_rows(s_re, k, seg).astype(BF16), cre_ref[...])
                 - _dot(_seg_rows(s_im, k, seg).astype(BF16), cim_ref[...]))
            y = y + d_ref[...] * u_ref[pl.ds(k * seg, seg), :].astype(F32)
            y_ref[pl.ds(k * seg, seg), :] = jax.nn.gelu(y).astype(BF16)

    return pl.pallas_call(
        body, name=name, grid=(N_SLAB,),
        in_specs=[col(OFF_UA // SLAB_CH), mat_b, mat_b, mat_c, mat_c, vec_s, vec_s, vec_c],
        out_specs=pl.BlockSpec((L, SLAB_CH), lambda j: (0, j)),
        out_shape=jax.ShapeDtypeStruct((L, SSM_WIDTH), BF16),
        scratch_shapes=[pltpu.VMEM((SLAB_ST // 128, rows, 128), F32)] * 2,
        compiler_params=_cp(("parallel",)),
    )(proj, bre, bim, cre_t, cim_t, lam_re, lam_im, dvec)


def _s5_bwd(proj, dy, bre, bim, cre_t, cim_t, lam_re, lam_im, dvec, name):
    L = proj.shape[0]
    seg = L // N_SEG
    stride = seg + SEG_PAD
    col, mat_b, mat_c, vec_s, vec_c = _s5_specs(L)
    rows = N_SEG * stride
    nt = SLAB_ST // 128

    def body(u_ref, dy_ref, bre_ref, bim_ref, cre_ref, cim_ref, lr_ref, li_ref, d_ref,
             du_ref, dbre_ref, dbim_ref, dcre_ref, dcim_ref, dlr_ref, dli_ref, dd_ref,
             s_re, s_im, a_re, a_im, dyp):
        lam = (lr_ref[...], li_ref[...])
        carry_s = _s5_states(u_ref, bre_ref, bim_ref, lam, s_re, s_im, seg)
        dcre = jnp.zeros((SLAB_ST, SLAB_CH), F32)
        dcim = jnp.zeros((SLAB_ST, SLAB_CH), F32)
        dd = jnp.zeros((1, SLAB_CH), F32)
        for k in range(N_SEG):
            sre = _seg_rows(s_re, k, seg).astype(BF16)
            sim = _seg_rows(s_im, k, seg).astype(BF16)
            uk = u_ref[pl.ds(k * seg, seg), :].astype(F32)
            ypre = _dot(sre, cre_ref[...]) - _dot(sim, cim_ref[...]) + d_ref[...] * uk
            _, vjp = jax.vjp(jax.nn.gelu, ypre)
            (dyk,) = vjp(dy_ref[pl.ds(k * seg, seg), :].astype(F32))
            dyp[pl.ds(k * seg, seg), :] = dyk
            dd = dd + jnp.sum(dyk * uk, axis=0, keepdims=True)
            dyb = dyk.astype(BF16)
            dcre = dcre + _dot_tn(sre, dyb)
            dcim = dcim - _dot_tn(sim, dyb)
            _seg_store(a_re, k, seg, _dot_nt(dyb, cre_ref[...]))
            _seg_store(a_im, k, seg, -_dot_nt(dyb, cim_ref[...]))
        dcre_ref[...] = dcre
        dcim_ref[...] = dcim
        dd_ref[...] = dd

        _seg_scan(a_re, a_im, (lam[0], -lam[1]), seg, reverse=True)

        def acc_dlam(i, acc):
            out = []
            for j in range(nt):
                ar = a_re[j, pl.ds(i, N_SEG, stride=stride), :]
                ai = a_im[j, pl.ds(i, N_SEG, stride=stride), :]
                pr = s_re[j, pl.ds(i - 1, N_SEG, stride=stride), :]
                pi = s_im[j, pl.ds(i - 1, N_SEG, stride=stride), :]
                out += [acc[2 * j] + ar * pr + ai * pi, acc[2 * j + 1] + ai * pr - ar * pi]
            return tuple(out)

        first = []
        for j in range(nt):
            ar = a_re[j, pl.ds(0, N_SEG, stride=stride), :]
            ai = a_im[j, pl.ds(0, N_SEG, stride=stride), :]
            pr, pi = carry_s[2 * j], carry_s[2 * j + 1]
            first += [ar * pr + ai * pi, ai * pr - ar * pi]
        acc = lax.fori_loop(1, seg, acc_dlam, tuple(first))
        dlr_ref[...] = jnp.concatenate([jnp.sum(acc[2 * j], axis=0, keepdims=True) for j in range(nt)], axis=-1)
        dli_ref[...] = jnp.concatenate([jnp.sum(acc[2 * j + 1], axis=0, keepdims=True) for j in range(nt)], axis=-1)

        dbre = jnp.zeros((SLAB_CH, SLAB_ST), F32)
        dbim = jnp.zeros((SLAB_CH, SLAB_ST), F32)
        for k in range(N_SEG):
            are = _seg_rows(a_re, k, seg).astype(BF16)
            aim = _seg_rows(a_im, k, seg).astype(BF16)
            uk = u_ref[pl.ds(k * seg, seg), :]
            du = _dot_nt(are, bre_ref[...]) + _dot_nt(aim, bim_ref[...]) + dyp[pl.ds(k * seg, seg), :] * d_ref[...]
            du_ref[pl.ds(k * seg, seg), :] = du.astype(BF16)
            dbre = dbre + _dot_tn(uk, are)
            dbim = dbim + _dot_tn(uk, aim)
        dbre_ref[...] = dbre
        dbim_ref[...] = dbim

    scan_buf = pltpu.VMEM((nt, rows, 128), F32)
    return pl.pallas_call(
        body, name=name, grid=(N_SLAB,),
        in_specs=[col(OFF_UA // SLAB_CH), pl.BlockSpec((L, SLAB_CH), lambda j: (0, j)),
                  mat_b, mat_b, mat_c, mat_c, vec_s, vec_s, vec_c],
        out_specs=[pl.BlockSpec((L, SLAB_CH), lambda j: (0, j)), mat_b, mat_b, mat_c, mat_c, vec_s, vec_s, vec_c],
        out_shape=[jax.ShapeDtypeStruct((L, SSM_WIDTH), BF16),
                   jax.ShapeDtypeStruct((N_SLAB, SLAB_CH, SLAB_ST), F32),
                   jax.ShapeDtypeStruct((N_SLAB, SLAB_CH, SLAB_ST), F32),
                   jax.ShapeDtypeStruct((N_SLAB, SLAB_ST, SLAB_CH), F32),
                   jax.ShapeDtypeStruct((N_SLAB, SLAB_ST, SLAB_CH), F32),
                   jax.ShapeDtypeStruct((N_SLAB, 1, SLAB_ST), F32),
                   jax.ShapeDtypeStruct((N_SLAB, 1, SLAB_ST), F32),
                   jax.ShapeDtypeStruct((N_SLAB, 1, SLAB_CH), F32)],
        scratch_shapes=[scan_buf, scan_buf, scan_buf, scan_buf, pltpu.VMEM((L, SLAB_CH), F32)],
        compiler_params=_cp(("parallel",)),
    )(proj, dy, bre, bim, cre_t, cim_t, lam_re, lam_im, dvec)


def _glu_point(y0, pre, za, b):
    return y0 * jax.nn.sigmoid(pre + b) * jax.nn.silu(za)


def _glu_specs(L, tm):
    row = pl.BlockSpec((tm, SSM_WIDTH), lambda i: (i, 0))
    za = pl.BlockSpec((tm, SSM_WIDTH), lambda i: (i, OFF_ZA // SSM_WIDTH))
    wmat = pl.BlockSpec((SSM_WIDTH, SSM_WIDTH), lambda i: (0, 0))
    vec = pl.BlockSpec((1, SSM_WIDTH), lambda i: (0, 0))
    return row, za, wmat, vec


def _glu_fwd(ya0, proj, w, b, name):
    L = ya0.shape[0]
    tm = min(L, 512)
    row, za, wmat, vec = _glu_specs(L, tm)

    def body(y_ref, z_ref, w_ref, b_ref, o_ref):
        y0 = y_ref[...]
        pre = _dot(y0, w_ref[...])
        o_ref[...] = _glu_point(y0.astype(F32), pre, z_ref[...].astype(F32), b_ref[...]).astype(BF16)

    return pl.pallas_call(
        body, name=name, grid=(L // tm,), in_specs=[row, za, wmat, vec], out_specs=row,
        out_shape=jax.ShapeDtypeStruct((L, SSM_WIDTH), BF16), compiler_params=_cp(("parallel",)),
    )(ya0, proj, w, b)


def _glu_bwd(ya0, proj, w, b, dya, name):
    L = ya0.shape[0]
    tm = min(L, 512)
    row, za, wmat, vec = _glu_specs(L, tm)

    def body(y_ref, z_ref, w_ref, b_ref, g_ref, dy0_ref, dza_ref, dw_ref, db_ref):
        y0 = y_ref[...]
        pre = _dot(y0, w_ref[...])
        _, vjp = jax.vjp(_glu_point, y0.astype(F32), pre, z_ref[...].astype(F32), b_ref[...])
        dy0, dpre, dza, db = vjp(g_ref[...].astype(F32))
        dpb = dpre.astype(BF16)
        dy0_ref[...] = (dy0 + _dot_nt(dpb, w_ref[...])).astype(BF16)
        dza_ref[...] = dza.astype(BF16)

        @pl.when(pl.program_id(0) == 0)
        def _():
            dw_ref[...] = jnp.zeros_like(dw_ref)
            db_ref[...] = jnp.zeros_like(db_ref)

        dw_ref[...] += _dot_tn(y0, dpb)
        db_ref[...] += db

    return pl.pallas_call(
        body, name=name, grid=(L // tm,), in_specs=[row, za, wmat, vec, row],
        out_specs=[row, row, wmat, vec],
        out_shape=[jax.ShapeDtypeStruct((L, SSM_WIDTH), BF16), jax.ShapeDtypeStruct((L, SSM_WIDTH), BF16),
                   jax.ShapeDtypeStruct((SSM_WIDTH, SSM_WIDTH), F32), jax.ShapeDtypeStruct((1, SSM_WIDTH), F32)],
        compiler_params=_cp(("arbitrary",)),
    )(ya0, proj, w, b, dya)


def _sg_norm(vb, ln_w, ln_b):
    v0 = jax.nn.gelu(vb)
    mu = jnp.mean(v0, axis=-1, keepdims=True)
    var = jnp.mean(jnp.square(v0 - mu), axis=-1, keepdims=True)
    return (v0 - mu) * lax.rsqrt(var + EPS) * ln_w + ln_b


def _sg_gate(ub, mixed, zb):
    return jax.nn.gelu(ub) * mixed * jax.nn.silu(zb)


def _sg_specs():
    W = SSM_WIDTH
    blk = lambda off: pl.BlockSpec((CHUNK, W), lambda n: (n, off // W))
    out = pl.BlockSpec((CHUNK, W), lambda n: (n, 0))
    vec = pl.BlockSpec((1, W), lambda n: (0, 0))
    wsp = pl.BlockSpec((SG_HEADS, CHUNK, CHUNK), lambda n: (0, 0, 0))
    bsp = pl.BlockSpec((SG_HEADS, CHUNK, 1), lambda n: (0, 0, 0))
    return blk, out, vec, wsp, bsp


def _sg_masked(w_ref):
    t = lax.broadcasted_iota(jnp.int32, (CHUNK, CHUNK), 0)
    s = lax.broadcasted_iota(jnp.int32, (CHUNK, CHUNK), 1)
    causal = s <= t
    return causal, [jnp.where(causal, w_ref[h], 0.0).astype(BF16) for h in range(SG_HEADS)]


def _sg_mix(wm, vnb, bias_ref):
    return jnp.concatenate(
        [_dot(wm[h], vnb[:, h * CHUNK:(h + 1) * CHUNK]) + bias_ref[h] for h in range(SG_HEADS)], axis=-1)


def _sg_fwd(proj, ln_w, ln_b, w, bias, name):
    L = proj.shape[0]
    blk, out, vec, wsp, bsp = _sg_specs()

    def body(ub_ref, vb_ref, zb_ref, lw_ref, lb_ref, w_ref, bias_ref, o_ref):
        _, wm = _sg_masked(w_ref)
        vnb = _sg_norm(vb_ref[...].astype(F32), lw_ref[...], lb_ref[...]).astype(BF16)
        mixed = _sg_mix(wm, vnb, bias_ref)
        o_ref[...] = _sg_gate(ub_ref[...].astype(F32), mixed, zb_ref[...].astype(F32)).astype(BF16)

    return pl.pallas_call(
        body, name=name, grid=(L // CHUNK,),
        in_specs=[blk(OFF_UB), blk(OFF_VB), blk(OFF_ZB), vec, vec, wsp, bsp], out_specs=out,
        out_shape=jax.ShapeDtypeStruct((L, SSM_WIDTH), BF16), compiler_params=_cp(("parallel",)),
    )(proj, proj, proj, ln_w, ln_b, w, bias)


def _sg_bwd(proj, ln_w, ln_b, w, bias, dyb, name):
    L = proj.shape[0]
    blk, out, vec, wsp, bsp = _sg_specs()

    def body(ub_ref, vb_ref, zb_ref, lw_ref, lb_ref, w_ref, bias_ref, g_ref,
             dub_ref, dvb_ref, dzb_ref, dlw_ref, dlb_ref, dw_ref, dbias_ref):
        causal, wm = _sg_masked(w_ref)
        vb = vb_ref[...].astype(F32)
        vn, vjp_norm = jax.vjp(_sg_norm, vb, lw_ref[...], lb_ref[...])
        vnb = vn.astype(BF16)
        mixed = _sg_mix(wm, vnb, bias_ref)
        _, vjp_gate = jax.vjp(_sg_gate, ub_ref[...].astype(F32), mixed, zb_ref[...].astype(F32))
        dub, dmixed, dzb = vjp_gate(g_ref[...].astype(F32))
        dub_ref[...] = dub.astype(BF16)
        dzb_ref[...] = dzb.astype(BF16)

        @pl.when(pl.program_id(0) == 0)
        def _():
            dlw_ref[...] = jnp.zeros_like(dlw_ref)
            dlb_ref[...] = jnp.zeros_like(dlb_ref)
            dw_ref[...] = jnp.zeros_like(dw_ref)
            dbias_ref[...] = jnp.zeros_like(dbias_ref)

        dvn = []
        for h in range(SG_HEADS):
            dm = dmixed[:, h * CHUNK:(h + 1) * CHUNK]
            dmb = dm.astype(BF16)
            dbias_ref[h] += jnp.sum(dm, axis=-1, keepdims=True)
            dw_ref[h] += jnp.where(causal, _dot_nt(dmb, vnb[:, h * CHUNK:(h + 1) * CHUNK]), 0.0)
            dvn.append(_dot_tn(wm[h], dmb))
        dvb, dlw, dlb = vjp_norm(jnp.concatenate(dvn, axis=-1))
        dvb_ref[...] = dvb.astype(BF16)
        dlw_ref[...] += dlw
        dlb_ref[...] += dlb

    act = jax.ShapeDtypeStruct((L, SSM_WIDTH), BF16)
    return pl.pallas_call(
        body, name=name, grid=(L // CHUNK,),
        in_specs=[blk(OFF_UB), blk(OFF_VB), blk(OFF_ZB), vec, vec, wsp, bsp, out],
        out_specs=[out, out, out, vec, vec, wsp, bsp],
        out_shape=[act, act, act, jax.ShapeDtypeStruct((1, SSM_WIDTH), F32), jax.ShapeDtypeStruct((1, SSM_WIDTH), F32),
                   jax.ShapeDtypeStruct((SG_HEADS, CHUNK, CHUNK), F32), jax.ShapeDtypeStruct((SG_HEADS, CHUNK, 1), F32)],
        compiler_params=_cp(("arbitrary",)),
    )(proj, proj, proj, ln_w, ln_b, w, bias, dyb)


def _rope_tables(L):
    half = ROT_DIM // 2
    inv_freq = ROPE_THETA ** (-jnp.arange(0, ROT_DIM, 2, dtype=F32) / ROT_DIM)
    ang = jnp.arange(L, dtype=F32)[:, None] * inv_freq[None, :]
    cos, sin = jnp.cos(ang), jnp.sin(ang)
    ones = jnp.ones((L, HEAD_DIM - ROT_DIM), F32)
    cos_h = jnp.concatenate([cos, cos, ones], axis=-1)
    sin_h = jnp.concatenate([-sin, sin, 0.0 * ones], axis=-1)
    src = jnp.arange(HEAD_DIM)[:, None]
    dst = jnp.arange(HEAD_DIM)[None, :]
    p_h = (((dst < half) & (src == dst + half)) | ((dst >= half) & (dst < ROT_DIM) & (src == dst - half))).astype(F32)
    p2 = jnp.kron(jnp.eye(2, dtype=F32), p_h).astype(BF16)
    return jnp.tile(cos_h, (1, 2)), jnp.tile(sin_h, (1, 2)), p2


def _rope(t, cos, sin, p2):
    n = t.shape[1] // 128
    tb = t.astype(BF16)
    sw = jnp.concatenate([_dot(tb[:, i * 128:(i + 1) * 128], p2) for i in range(n)], axis=-1) if n > 1 else _dot(tb, p2)
    return t * jnp.tile(cos, (1, n)) + sw * jnp.tile(sin, (1, n))


def _rope_t(g, cos, sin, p2):
    n = g.shape[1] // 128
    gs = (g * jnp.tile(sin, (1, n))).astype(BF16)
    sw = jnp.concatenate([_dot_nt(gs[:, i * 128:(i + 1) * 128], p2) for i in range(n)], axis=-1) if n > 1 else _dot_nt(gs, p2)
    return g * jnp.tile(cos, (1, n)) + sw


def _lane_lo(shape):
    return (lax.broadcasted_iota(jnp.int32, shape, len(shape) - 1) % 128) < HEAD_DIM


def _dup_halves(x):
    xr = pltpu.roll(x, HEAD_DIM, 1)
    lo = _lane_lo(x.shape)
    return jnp.where(lo, x, xr), jnp.where(lo, xr, x)


def _fold_halves(d0, d1):
    f0 = d0 + pltpu.roll(d0, HEAD_DIM, 1)
    f1 = d1 + pltpu.roll(d1, HEAD_DIM, 1)
    return jnp.where(_lane_lo(d0.shape), f0, f1)


def _attn_mask():
    qi = lax.broadcasted_iota(jnp.int32, (CHUNK, 2 * CHUNK), 0)
    kj = lax.broadcasted_iota(jnp.int32, (CHUNK, 2 * CHUNK), 1)
    return qi, kj


def _attn_specs():
    qsp = pl.BlockSpec((CHUNK, 1024), lambda n: (n, OFF_Q // 1024))
    kv_cur = pl.BlockSpec((CHUNK, 256), lambda n: (n, OFF_KV // 256))
    kv_prev = pl.BlockSpec((CHUNK, 256), lambda n: (jnp.maximum(n - 1, 0), OFF_KV // 256))
    zsp = [pl.BlockSpec((CHUNK, 256), functools.partial(lambda n, q: (n, OFF_ZC // 256 + q), q=q)) for q in range(4)]
    tab_cur = pl.BlockSpec((CHUNK, 128), lambda n: (n, 0))
    tab_prev = pl.BlockSpec((CHUNK, 128), lambda n: (jnp.maximum(n - 1, 0), 0))
    p2sp = pl.BlockSpec((128, 128), lambda n: (0, 0))
    sink = pl.BlockSpec(memory_space=pltpu.SMEM)
    wide = pl.BlockSpec((CHUNK, 1024), lambda n: (n, 0))
    return qsp, kv_cur, kv_prev, zsp, tab_cur, tab_prev, p2sp, sink, wide


def _attn_core(n, q_ref, kvc_ref, kvp_ref, cosc_ref, sinc_ref, cosp_ref, sinp_ref, p2_ref, sink_ref):
    p2 = p2_ref[...]
    qr = _rope(q_ref[...].astype(F32), cosc_ref[...], sinc_ref[...], p2).astype(BF16)
    kc = _rope(kvc_ref[:, 0:128].astype(F32), cosc_ref[...], sinc_ref[...], p2)
    kp = _rope(kvp_ref[:, 0:128].astype(F32), cosp_ref[...], sinp_ref[...], p2)
    k_all = jnp.concatenate([kp, kc], axis=0).astype(BF16)
    v_all = jnp.concatenate([kvp_ref[:, 128:256], kvc_ref[:, 128:256]], axis=0)
    kd = _dup_halves(k_all)
    vd = _dup_halves(v_all)
    qi, kj = _attn_mask()
    allowed = ((kj < CHUNK) & (kj > qi) & (n > 0)) | ((kj >= CHUNK) & (kj - CHUNK <= qi))
    lo = _lane_lo((CHUNK, 128))
    probs = []
    for h in range(ATT_HEADS):
        m, half, g = h // 2, h % 2, h // 8
        qp = qr[:, m * 128:(m + 1) * 128]
        qm = jnp.where(lo if half == 0 else ~lo, qp, jnp.zeros_like(qp))
        s = jnp.where(allowed, _dot_nt(qm, kd[g]) * (HEAD_DIM ** -0.5), NEG_INF)
        snk = sink_ref[h]
        mx = jnp.maximum(jnp.max(s, axis=-1, keepdims=True), snk)
        e = jnp.exp(s - mx)
        es = jnp.exp(snk - mx)
        inv = 1.0 / (jnp.sum(e, axis=-1, keepdims=True) + es)
        probs.append((qm, e * inv, es * inv))
    return qr, kd, vd, probs, lo


def _attn_out(vd, probs, lo):
    outs = []
    for m in range(ATT_HEADS // 2):
        g = m // 4
        o0 = _dot(probs[2 * m][1].astype(BF16), vd[g])
        o1 = _dot(probs[2 * m + 1][1].astype(BF16), vd[g])
        outs.append(jnp.where(lo, o0, o1))
    return jnp.concatenate(outs, axis=-1)


def _silu_gate(o, z):
    return o * jax.nn.silu(z)


def _attn_fwd(proj, sinks, tabs, name):
    L = proj.shape[0]
    cos2, sin2, p2 = tabs
    qsp, kv_cur, kv_prev, zsp, tab_cur, tab_prev, p2sp, sink, wide = _attn_specs()

    def body(q_ref, kvc_ref, kvp_ref, z0, z1, z2, z3, cosc, sinc, cosp, sinp, p2_ref, sink_ref, o_ref):
        n = pl.program_id(0)
        _, _, vd, probs, lo = _attn_core(n, q_ref, kvc_ref, kvp_ref, cosc, sinc, cosp, sinp, p2_ref, sink_ref)
        o = _attn_out(vd, probs, lo)
        z = jnp.concatenate([z0[...], z1[...], z2[...], z3[...]], axis=-1).astype(F32)
        o_ref[...] = _silu_gate(o, z).astype(BF16)

    return pl.pallas_call(
        body, name=name, grid=(L // CHUNK,),
        in_specs=[qsp, kv_cur, kv_prev, *zsp, tab_cur, tab_cur, tab_prev, tab_prev, p2sp, sink],
        out_specs=wide, out_shape=jax.ShapeDtypeStruct((L, 1024), BF16), compiler_params=_cp(("parallel",)),
    )(proj, proj, proj, proj, proj, proj, proj, cos2, sin2, cos2, sin2, p2, sinks)


def _attn_bwd(proj, sinks, tabs, dyc, name):
    L = proj.shape[0]
    cos2, sin2, p2 = tabs
    qsp, kv_cur, kv_prev, zsp, tab_cur, tab_prev, p2sp, sink, wide = _attn_specs()
    kvo = pl.BlockSpec((CHUNK, 256), lambda n: (n, 0))

    def body(q_ref, kvc_ref, kvp_ref, z0, z1, z2, z3, cosc, sinc, cosp, sinp, p2_ref, sink_ref, g_ref,
             dq_ref, dz_ref, dkvc_ref, dkvp_ref, dsink_ref):
        n = pl.program_id(0)
        _, kd, vd, probs, lo = _attn_core(n, q_ref, kvc_ref, kvp_ref, cosc, sinc, cosp, sinp, p2_ref, sink_ref)
        o = _attn_out(vd, probs, lo)
        z = jnp.concatenate([z0[...], z1[...], z2[...], z3[...]], axis=-1).astype(F32)
        _, vjp = jax.vjp(_silu_gate, o, z)
        do, dz = vjp(g_ref[...].astype(F32))
        dz_ref[...] = dz.astype(BF16)

        @pl.when(n == 0)
        def _():
            dsink_ref[...] = jnp.zeros_like(dsink_ref)

        dkd = [jnp.zeros((2 * CHUNK, 128), F32), jnp.zeros((2 * CHUNK, 128), F32)]
        dvd = [jnp.zeros((2 * CHUNK, 128), F32), jnp.zeros((2 * CHUNK, 128), F32)]
        dq_pairs = []
        for m in range(ATT_HEADS // 2):
            g = m // 4
            dop = do[:, m * 128:(m + 1) * 128].astype(BF16)
            dq_h = []
            for half in range(2):
                h = 2 * m + half
                qm, p, ps = probs[h]
                dom = jnp.where(lo if half == 0 else ~lo, dop, jnp.zeros_like(dop))
                dp = _dot_nt(dom, vd[g])
                rs = jnp.sum(p * dp, axis=-1, keepdims=True)
                ds = (p * (dp - rs) * (HEAD_DIM ** -0.5)).astype(BF16)
                dsink_ref[h:h + 1, :] += jnp.broadcast_to(jnp.sum(-ps * rs, axis=0, keepdims=True), (1, 128))
                dq_h.append(_dot(ds, kd[g]))
                dkd[g] = dkd[g] + _dot_tn(ds, qm)
                dvd[g] = dvd[g] + _dot_tn(p.astype(BF16), dom)
            dq_pairs.append(jnp.where(lo, dq_h[0], dq_h[1]))
        p2 = p2_ref[...]
        dq_ref[...] = _rope_t(jnp.concatenate(dq_pairs, axis=-1), cosc[...], sinc[...], p2).astype(BF16)
        dk_rot = _fold_halves(dkd[0], dkd[1])
        dv = _fold_halves(dvd[0], dvd[1])
        dkp = _rope_t(dk_rot[0:CHUNK], cosp[...], sinp[...], p2)
        dkc = _rope_t(dk_rot[CHUNK:2 * CHUNK], cosc[...], sinc[...], p2)
        dkvp_ref[...] = jnp.concatenate([dkp, dv[0:CHUNK]], axis=-1)
        dkvc_ref[...] = jnp.concatenate([dkc, dv[CHUNK:2 * CHUNK]], axis=-1)

    act = jax.ShapeDtypeStruct((L, 1024), BF16)
    kvs = jax.ShapeDtypeStruct((L, 256), F32)
    return pl.pallas_call(
        body, name=name, grid=(L // CHUNK,),
        in_specs=[qsp, kv_cur, kv_prev, *zsp, tab_cur, tab_cur, tab_prev, tab_prev, p2sp, sink, wide],
        out_specs=[wide, wide, kvo, kvo, pl.BlockSpec((ATT_HEADS, 128), lambda n: (0, 0))],
        out_shape=[act, act, kvs, kvs, jax.ShapeDtypeStruct((ATT_HEADS, 128), F32)],
        compiler_params=_cp(("arbitrary",)),
    )(proj, proj, proj, proj, proj, proj, proj, cos2, sin2, cos2, sin2, p2, sinks, dyc)


MERGE_TN = 256


def _merge_point(ta, tb, tc, ga, gb, gc):
    return jax.nn.sigmoid(ga) * ta + jax.nn.sigmoid(gb) * tb + jax.nn.sigmoid(gc) * tc


def _merge_specs(tm):
    nj = D_MODEL // MERGE_TN
    t = pl.BlockSpec((tm, MERGE_TN), lambda i, j: (i, j))
    gates = [pl.BlockSpec((tm, MERGE_TN), functools.partial(lambda i, j, b: (i, OFF_G // MERGE_TN + b * nj + j), b=b))
             for b in range(3)]
    return t, gates, nj


def _merge_fwd(ta, tb, tc, proj, name):
    L = ta.shape[0]
    tm = min(L, 1024)
    t, gates, nj = _merge_specs(tm)

    def body(ta_ref, tb_ref, tc_ref, ga_ref, gb_ref, gc_ref, o_ref):
        f = lambda r: r[...].astype(F32)
        o_ref[...] = _merge_point(f(ta_ref), f(tb_ref), f(tc_ref), f(ga_ref), f(gb_ref), f(gc_ref)).astype(BF16)

    return pl.pallas_call(
        body, name=name, grid=(L // tm, nj), in_specs=[t, t, t, *gates], out_specs=t,
        out_shape=jax.ShapeDtypeStruct((L, D_MODEL), BF16), compiler_params=_cp(("parallel", "parallel")),
    )(ta, tb, tc, proj, proj, proj)


def _merge_bwd(ta, tb, tc, proj, dm, name):
    L = ta.shape[0]
    tm = min(L, 1024)
    t, gates, nj = _merge_specs(tm)

    def body(ta_ref, tb_ref, tc_ref, ga_ref, gb_ref, gc_ref, dm_ref, dta_ref, dtb_ref, dtc_ref, dga_ref, dgb_ref, dgc_ref):
        f = lambda r: r[...].astype(F32)
        _, vjp = jax.vjp(_merge_point, f(ta_ref), f(tb_ref), f(tc_ref), f(ga_ref), f(gb_ref), f(gc_ref))
        outs = vjp(f(dm_ref))
        for r, v in zip((dta_ref, dtb_ref, dtc_ref, dga_ref, dgb_ref, dgc_ref), outs):
            r[...] = v.astype(BF16)

    act = jax.ShapeDtypeStruct((L, D_MODEL), BF16)
    return pl.pallas_call(
        body, name=name, grid=(L // tm, nj), in_specs=[t, t, t, *gates, t],
        out_specs=[t] * 6, out_shape=[act] * 6,
        compiler_params=_cp(("parallel", "parallel")),
    )(ta, tb, tc, proj, proj, proj, dm)


GRAD_DT = BF16
SMALL = ("norm_w", "ssm_a_re", "ssm_a_im", "ssm_log_dt", "ssm_b_re", "ssm_b_im", "ssm_c_re", "ssm_c_im", "ssm_d",
         "ssm_glu_b", "sg_ln_w", "sg_ln_b", "sg_w", "sg_b", "attn_sinks")
G8 = SSM_GROUPS // N_SLAB


def _slab_b(bb_t):
    x = bb_t.transpose(1, 0, 2).reshape(N_SLAB, G8, SSM_GROUP, SSM_STATE)
    return jnp.einsum("jgcp,gh->jgchp", x, jnp.eye(G8, dtype=x.dtype)).reshape(N_SLAB, SLAB_CH, SLAB_ST)


def _unslab_b(d):
    x = d.reshape(N_SLAB, G8, SSM_GROUP, G8, SSM_STATE)
    x = jnp.einsum("jgchp,gh->jgcp", x, jnp.eye(G8, dtype=x.dtype))
    return x.reshape(SSM_GROUPS, SSM_GROUP, SSM_STATE).transpose(1, 0, 2)


def _slab_c(c):
    x = c.reshape(N_SLAB, G8, SSM_GROUP, SSM_STATE)
    return jnp.einsum("jgcp,gh->jgphc", x, jnp.eye(G8, dtype=x.dtype)).reshape(N_SLAB, SLAB_ST, SLAB_CH)


def _unslab_c(d):
    x = d.reshape(N_SLAB, G8, SSM_STATE, G8, SSM_GROUP)
    x = jnp.einsum("jgphc,gh->jgcp", x, jnp.eye(G8, dtype=x.dtype))
    return x.reshape(SSM_GROUPS, SSM_GROUP, SSM_STATE)


def _s5_prep(p, tag):
    bt_re = p["ssm_b_re"].transpose(2, 0, 1)
    bt_im = p["ssm_b_im"].transpose(2, 0, 1)
    raw = (p["ssm_a_re"], p["ssm_a_im"], p["ssm_log_dt"][:, None], bt_re, bt_im)
    lr, li, bbr, bbi = _s5_params_fwd(*raw, name=f"s5_params_{tag}")
    ops = (_slab_b(bbr).astype(BF16), _slab_b(bbi).astype(BF16),
           _slab_c(p["ssm_c_re"]).astype(BF16), _slab_c(p["ssm_c_im"]).astype(BF16),
           lr.reshape(N_SLAB, 1, SLAB_ST), li.reshape(N_SLAB, 1, SLAB_ST), p["ssm_d"].reshape(N_SLAB, 1, SLAB_CH))
    return raw, ops


def _layer_fwd(x, p, w, tabs, tag, proj_after=None, after_proj=None):
    L = x.shape[0]
    h = _rms_fwd(x, p["norm_w"][None], f"rms_fwd_{tag}")
    proj = _mm(h, w["win_t"], "nt", BF16, L, 256, D_MODEL, f"in_proj_{tag}", after=proj_after)
    if after_proj is not None:
        w = after_proj(proj)
    s5_raw, s5_ops = _s5_prep(p, tag)
    ya0 = _s5_fwd(proj, *s5_ops, name=f"s5_fwd_{tag}")
    ya = _glu_fwd(ya0, proj, w["glu"], p["ssm_glu_b"][None], f"glu_fwd_{tag}")
    yb = _sg_fwd(proj, p["sg_ln_w"][None], p["sg_ln_b"][None], p["sg_w"], p["sg_b"][:, :, None], f"sg_fwd_{tag}")
    yc = _attn_fwd(proj, p["attn_sinks"], tabs, f"attn_fwd_{tag}")
    ta = _mm(ya, w["wba_t"], "nt", BF16, 1024, 1024, 1024, f"branch_a_{tag}")
    tb = _mm(yb, w["wbb_t"], "nt", BF16, 1024, 1024, 1024, f"branch_b_{tag}")
    tc = _mm(yc, w["wbc_t"], "nt", BF16, 1024, 1024, 1024, f"branch_c_{tag}")
    merged = _merge_fwd(ta, tb, tc, proj, f"merge_fwd_{tag}")
    x_new = _mm(merged, w["wout"], "nn", F32, 1024, 512, D_MODEL, f"out_proj_{tag}", res=x)
    saved = dict(x=x, h=h, proj=proj, s5_raw=s5_raw, s5_ops=s5_ops, ya0=ya0, ya=ya, yb=yb, yc=yc,
                 ta=ta, tb=tb, tc=tc, merged=merged)
    return x_new, saved


def _layer_bwd(dx_out, p, w, tabs, s, tag, first_after=None, before_win=None, after_win=None):
    L = dx_out.shape[0]
    proj = s["proj"]
    big, small = {}, {}
    dmerged = _mm(dx_out, w["wout"], "nt", BF16, 1024, 512, D_MODEL, f"d_merged_{tag}", after=first_after)
    big["wout"] = _mm(s["merged"], dx_out, "tn", GRAD_DT, 512, 1024, L, f"d_wout_{tag}")
    dta, dtb, dtc, dga, dgb, dgc = _merge_bwd(s["ta"], s["tb"], s["tc"], proj, dmerged, f"merge_bwd_{tag}")
    dy = {}
    for br, dt in (("a", dta), ("b", dtb), ("c", dtc)):
        dy[br] = _mm(dt, w[f"wb{br}_t"], "nn", BF16, 1024, 1024, D_MODEL, f"d_y{br}_{tag}")
        big[f"wb{br}_t"] = _mm(dt, s[f"y{br}"], "tn", GRAD_DT, 512, 1024, L, f"d_wb{br}_{tag}")

    dq, dzc, dkvc, dkvp, dsink = _attn_bwd(proj, p["attn_sinks"], tabs, dy["c"], f"attn_bwd_{tag}")
    dkv = dkvc + jnp.concatenate([dkvp[CHUNK:], jnp.zeros((CHUNK, 256), F32)], axis=0)
    small["attn_sinks"] = dsink[:, 0]

    dub, dvb, dzb, dlw, dlb, dsgw, dsgb = _sg_bwd(
        proj, p["sg_ln_w"][None], p["sg_ln_b"][None], p["sg_w"], p["sg_b"][:, :, None], dy["b"], f"sg_bwd_{tag}")
    small.update(sg_ln_w=dlw[0], sg_ln_b=dlb[0], sg_w=dsgw, sg_b=dsgb[:, :, 0])

    dya0, dza, dglu, dglub = _glu_bwd(s["ya0"], proj, w["glu"], p["ssm_glu_b"][None], dy["a"], f"glu_bwd_{tag}")
    big["glu"] = dglu.astype(GRAD_DT)
    small["ssm_glu_b"] = dglub[0]

    dua, dbre, dbim, dcre, dcim, dlr, dli, dd = _s5_bwd(proj, dya0, *s["s5_ops"], name=f"s5_bwd_{tag}")
    da_re, da_im, dlog_dt, dbt_re, dbt_im = _s5_params_bwd(
        *s["s5_raw"], dlr.reshape(SSM_GROUPS, SSM_STATE), dli.reshape(SSM_GROUPS, SSM_STATE),
        _unslab_b(dbre), _unslab_b(dbim), name=f"s5_params_bwd_{tag}")
    small.update(ssm_a_re=da_re, ssm_a_im=da_im, ssm_log_dt=dlog_dt[:, 0],
                 ssm_b_re=dbt_re.transpose(1, 2, 0), ssm_b_im=dbt_im.transpose(1, 2, 0),
                 ssm_c_re=_unslab_c(dcre), ssm_c_im=_unslab_c(dcim), ssm_d=dd.reshape(SSM_WIDTH))

    dproj = jnp.concatenate([dua, dza, dub, dvb, dzb, dq, dkv.astype(BF16), dzc, dga, dgb, dgc], axis=-1)
    tok = before_win(big) if before_win is not None else None
    big["win_t"] = _mm(dproj, s["h"], "tn", GRAD_DT, 256, D_MODEL, L, f"d_win_{tag}", after=tok)
    tok = after_win(big) if after_win is not None else None
    dh = _mm(dproj, w["win_t"], "nn", F32, 1024, D_MODEL, 256, f"d_h_{tag}", after=tok)
    dx_in, dnw = _rms_bwd(s["x"], p["norm_w"][None], dh, dx_out, f"rms_bwd_{tag}")
    small["norm_w"] = dnw[0]
    return dx_in, big, small


def _local_step(x, tgt, small_p, final_w, big_w):
    L = x.shape[0]
    tabs = _rope_tables(L)
    saved = []
    for l in range(DEPTH):
        x, s = _layer_fwd(x, small_p[l], big_w[l], tabs, f"l{l}")
        saved.append(s)
    loss_acc, dx, dfw = _final(x, final_w[None], tgt, "final_norm_loss")
    big_g, small_g = [None] * DEPTH, [None] * DEPTH
    for l in reversed(range(DEPTH)):
        dx, big_g[l], small_g[l] = _layer_bwd(dx, small_p[l], big_w[l], tabs, saved[l], f"l{l}")
    return loss_acc[0, 0], dx, dfw[0], big_g, small_g


MESH = pl.DeviceIdType.MESH
ANY = pl.BlockSpec(memory_space=pl.ANY)
ROW_ALIGN = 16


def _place():
    return lax.axis_index("x"), lax.axis_index("y"), lax.axis_index("c")


HBM = pl.BlockSpec(memory_space=pltpu.HBM)
SEM = pl.BlockSpec(memory_space=pltpu.SEMAPHORE)
EFFECT = pltpu.SideEffectType.DATAFLOW_SIDE_EFFECTING


def _split_start(srcs, lands, n_copies, copies, name, after=None):
    n, m, k = len(srcs), len(lands), n_copies
    extra = [] if after is None else [after]

    def body(*refs):
        src_refs, land_refs = refs[:n], refs[n:n + m]
        sems = refs[n + m + len(extra):]
        send_sems, recv_sems, token = sems[:k], sems[k:2 * k], refs[-1]
        for cp in copies(src_refs, land_refs, send_sems, recv_sems):
            cp.start()
        token[...] = jnp.zeros_like(token)

    ops = list(srcs) + list(lands)
    outs = pl.pallas_call(
        body, name=name,
        out_shape=(*[pltpu.SemaphoreType.DMA(())] * (2 * k),
                   *[pltpu.HBM(a.shape, a.dtype) for a in ops], jax.ShapeDtypeStruct((8, 128), F32)),
        in_specs=[HBM] * (n + m) + [ANY] * len(extra),
        out_specs=(*[SEM] * (2 * k), *[HBM] * (n + m), pl.BlockSpec(memory_space=pltpu.VMEM)),
        input_output_aliases={i: 2 * k + i for i in range(n + m)},
        compiler_params=pltpu.CompilerParams(has_side_effects=EFFECT),
    )(*[pltpu.with_memory_space_constraint(a, pltpu.HBM) for a in ops], *extra)
    return (list(outs[:k]), list(outs[k:2 * k]), list(outs[2 * k:2 * k + n]), list(outs[2 * k + n:2 * k + n + m]),
            outs[-1])


def _split_wait(send_sems, recv_sems, srcs, lands, after, copies, name):
    n, m, k = len(srcs), len(lands), len(send_sems)

    def body(*refs):
        src_refs, land_refs = refs[:n], refs[n:n + m]
        for cp in copies(src_refs, land_refs, refs[n + m:n + m + k], refs[n + m + k:n + m + 2 * k]):
            cp.wait_send()
            cp.wait_recv()

    ops = list(srcs) + list(lands)
    outs = pl.pallas_call(
        body, name=name,
        out_shape=tuple(pltpu.HBM(a.shape, a.dtype) for a in ops),
        in_specs=[HBM] * (n + m) + [SEM] * (2 * k) + [ANY],
        out_specs=tuple([HBM] * (n + m)),
        input_output_aliases={i: i for i in range(n + m)},
        compiler_params=pltpu.CompilerParams(has_side_effects=EFFECT),
    )(*ops, *send_sems, *recv_sems, after)
    return list(outs[:n]), list(outs[n:])


def _ag_rows(land_ref, px, py, pc):
    r = land_ref.shape[0] // N_DEV
    start = pl.multiple_of((4 * px + 2 * py + pc) * r, ROW_ALIGN)
    return land_ref.at[pl.ds(start, r), :]


def _ag_copies(src_refs, land_refs, send_sems, recv_sems):
    x, y, c = _place()
    peers = [(x, y, 1 - c), (1 - x, y, c), (x, 1 - y, c), (1 - x, 1 - y, c)]
    return [pltpu.make_async_remote_copy(
        src_ref=_ag_rows(land_refs[a], x, y, c), dst_ref=_ag_rows(land_refs[a], x, y, c),
        send_sem=send_sems[4 * a + k], recv_sem=recv_sems[4 * a + k], device_id=peer, device_id_type=MESH)
        for a in range(len(land_refs)) for k, peer in enumerate(peers)]


def _ag_forward(lands, name):
    n = len(lands)

    def body(*refs):
        land_refs = refs[n:2 * n]
        send_sems, recv_sems = refs[2 * n:]
        x, y, c = _place()
        chips = [(1 - x, y), (x, 1 - y), (1 - x, 1 - y)]

        def copy(a, j, pc):
            px, py = chips[j]
            return pltpu.make_async_remote_copy(
                src_ref=_ag_rows(land_refs[a], px, py, pc), dst_ref=_ag_rows(land_refs[a], px, py, pc),
                send_sem=send_sems.at[a, j], recv_sem=recv_sems.at[a, j], device_id=(x, y, 1 - c), device_id_type=MESH)

        passed = [copy(a, j, c) for a in range(n) for j in range(3)]
        for cp in passed:
            cp.start()
        for a in range(n):
            for j in range(3):
                copy(a, j, 1 - c).wait_recv()
        for cp in passed:
            cp.wait_send()

    return pl.pallas_call(
        body, name=name,
        in_specs=[ANY] * n, out_specs=[ANY] * n,
        out_shape=[jax.ShapeDtypeStruct(l.shape, l.dtype) for l in lands],
        input_output_aliases={i: i for i in range(n)},
        scratch_shapes=[pltpu.SemaphoreType.DMA((n, 3)), pltpu.SemaphoreType.DMA((n, 3))],
    )(*lands)


def _allgather_start(shards, name, after=None):
    x, y, c = _place()
    lands = [lax.dynamic_update_slice(lax.empty((N_DEV * s.shape[0], s.shape[1]), s.dtype), s,
                                      ((4 * x + 2 * y + c) * s.shape[0], 0)) for s in shards]
    return _split_start([], lands, 4 * len(lands), _ag_copies, name + "_start", after=after)


def _allgather_finish(started, after, name):
    send_sems, recv_sems, _, lands, _ = started
    _, lands = _split_wait(send_sems, recv_sems, [], lands, after, _ag_copies, name + "_wait")
    return list(_ag_forward(lands, name + "_forward"))


def _rs_swap_cores(grads, name):
    n = len(grads)

    def body(*refs):
        ins, outs = refs[:n], refs[n:2 * n]
        send_sems, recv_sems = refs[2 * n:]
        x, y, c = _place()
        cps = []
        for a in range(n):
            r = ins[a].shape[0] // N_DEV
            for q in range(4):
                start = pl.multiple_of((2 * q + 1 - c) * r, ROW_ALIGN)
                cps.append(pltpu.make_async_remote_copy(
                    src_ref=ins[a].at[pl.ds(start, r), :], dst_ref=outs[a].at[q],
                    send_sem=send_sems.at[a, q], recv_sem=recv_sems.at[a, q],
                    device_id=(x, y, 1 - c), device_id_type=MESH))
        for cp in cps:
            cp.start()
        for cp in cps:
            cp.wait()

    return pl.pallas_call(
        body, name=name, in_specs=[ANY] * n, out_specs=[ANY] * n,
        out_shape=[jax.ShapeDtypeStruct((4, g.shape[0] // N_DEV, g.shape[1]), g.dtype) for g in grads],
        scratch_shapes=[pltpu.SemaphoreType.DMA((n, 4)), pltpu.SemaphoreType.DMA((n, 4))],
    )(*grads)


def _rs_chip_copies(sum_refs, land_refs, send_sems, recv_sems):
    x, y, c = _place()
    chips = [(1 - x, y), (x, 1 - y), (1 - x, 1 - y)]
    return [pltpu.make_async_remote_copy(
        src_ref=sum_refs[a].at[2 * px + py], dst_ref=land_refs[a].at[2 * x + y],
        send_sem=send_sems[3 * a + j], recv_sem=recv_sems[3 * a + j], device_id=(px, py, c), device_id_type=MESH)
        for a in range(len(sum_refs)) for j, (px, py) in enumerate(chips)]


def _row_tile(r):
    return max(t for t in range(ROW_ALIGN, min(r, 1024) + 1, ROW_ALIGN) if r % t == 0)


def _rs_add_cores(grad, recv, cidx, name):
    r, cols = recv.shape[1], recv.shape[2]
    tr = _row_tile(r)
    nb = r // tr

    def body(c_ref, g_ref, r_ref, o_ref):
        o_ref[...] = (g_ref[...].astype(F32) + r_ref[...].astype(F32)).astype(o_ref.dtype)

    return pl.pallas_call(
        body, name=name,
        grid_spec=pltpu.PrefetchScalarGridSpec(
            num_scalar_prefetch=1, grid=(4, nb),
            in_specs=[pl.BlockSpec((tr, cols), lambda q, i, c_ref: ((2 * q + c_ref[0]) * nb + i, 0)),
                      pl.BlockSpec((None, tr, cols), lambda q, i, c_ref: (q, i, 0))],
            out_specs=pl.BlockSpec((None, tr, cols), lambda q, i, c_ref: (q, i, 0))),
        out_shape=jax.ShapeDtypeStruct(recv.shape, recv.dtype),
        compiler_params=_cp(("parallel", "parallel")),
    )(cidx, grad, recv)


def _rs_add_chips(own, recv, slots, name):
    r, cols = recv.shape[1], recv.shape[2]
    tr = _row_tile(r)

    def body(s_ref, o_ref, r0_ref, r1_ref, r2_ref, out_ref):
        acc = o_ref[...].astype(F32)
        for ref in (r0_ref, r1_ref, r2_ref):
            acc = acc + ref[...].astype(F32)
        out_ref[...] = acc

    pick = lambda k: pl.BlockSpec((None, tr, cols), functools.partial(lambda i, s_ref, k: (s_ref[k], i, 0), k=k))
    return pl.pallas_call(
        body, name=name,
        grid_spec=pltpu.PrefetchScalarGridSpec(
            num_scalar_prefetch=1, grid=(r // tr,),
            in_specs=[pick(0), pick(1), pick(2), pick(3)],
            out_specs=pl.BlockSpec((tr, cols), lambda i, s_ref: (i, 0))),
        out_shape=jax.ShapeDtypeStruct((r, cols), F32),
        compiler_params=_cp(("parallel",)),
    )(slots, own, recv, recv, recv)


def _reduce_scatter_start(grads, tag):
    cidx = lax.axis_index("c").astype(jnp.int32)[None]
    recv = _rs_swap_cores(grads, f"rs_swap_cores_{tag}")
    sums = [_rs_add_cores(g, rv, cidx, f"rs_add_cores_{tag}_{i}") for i, (g, rv) in enumerate(zip(grads, recv))]
    lands = [lax.empty(s.shape, s.dtype) for s in sums]
    return _split_start(sums, lands, 3 * len(sums), _rs_chip_copies, f"rs_chips_{tag}_start")


def _reduce_scatter_finish(started, after, tag):
    send_sems, recv_sems, sums, lands, _ = started
    sums, lands = _split_wait(send_sems, recv_sems, sums, lands, after, _rs_chip_copies, f"rs_chips_{tag}_wait")
    x, y = lax.axis_index("x"), lax.axis_index("y")
    slots = jnp.stack([2 * x + y, 2 * (1 - x) + y, 2 * x + 1 - y, 2 * (1 - x) + 1 - y]).astype(jnp.int32)
    return [_rs_add_chips(s, l, slots, f"rs_add_chips_{tag}_{i}") for i, (s, l) in enumerate(zip(sums, lands))]


def _allreduce_small(pack, name):
    R, C = pack.shape
    rs = R // N_DEV
    assert R % (8 * N_DEV) == 0

    def body(p_ref, o_ref, parts, send1, recv1, send2, recv2):
        x, y, c = _place()
        me = 4 * x + 2 * y + c

        def block(ref, d):
            return ref.at[pl.ds(pl.multiple_of(d * rs, 8), rs), :]

        peers = [(1 - x if k & 4 else x, 1 - y if k & 2 else y, 1 - c if k & 1 else c) for k in range(1, N_DEV)]
        scatter = [pltpu.make_async_remote_copy(
            src_ref=block(p_ref, 4 * px + 2 * py + pc), dst_ref=parts.at[me], send_sem=send1.at[k], recv_sem=recv1.at[k],
            device_id=(px, py, pc), device_id_type=MESH) for k, (px, py, pc) in enumerate(peers)]
        for cp in scatter:
            cp.start()
        parts[me] = block(p_ref, me)[...]
        for cp in scatter:
            cp.wait()
        acc = parts[0]
        for d in range(1, N_DEV):
            acc = acc + parts[d]
        block(o_ref, me)[...] = acc
        gather = [pltpu.make_async_remote_copy(
            src_ref=block(o_ref, me), dst_ref=block(o_ref, me), send_sem=send2.at[k], recv_sem=recv2.at[k],
            device_id=peer, device_id_type=MESH) for k, peer in enumerate(peers)]
        for cp in gather:
            cp.start()
        for k, (px, py, pc) in enumerate(peers):
            pltpu.make_async_remote_copy(
                src_ref=block(o_ref, 4 * px + 2 * py + pc), dst_ref=block(o_ref, 4 * px + 2 * py + pc),
                send_sem=send2.at[k], recv_sem=recv2.at[k], device_id=(px, py, pc), device_id_type=MESH).wait_recv()
        for cp in gather:
            cp.wait_send()

    sems = pltpu.SemaphoreType.DMA((N_DEV - 1,))
    return pl.pallas_call(
        body, name=name,
        in_specs=[pl.BlockSpec(memory_space=pltpu.VMEM)], out_specs=pl.BlockSpec(memory_space=pltpu.VMEM),
        out_shape=jax.ShapeDtypeStruct((R, C), F32),
        scratch_shapes=[pltpu.VMEM((N_DEV, rs, C), F32), sems, sems, sems, sems],
        compiler_params=pltpu.CompilerParams(vmem_limit_bytes=VMEM_LIMIT),
    )(pack)


def _adamw(w, g, m, v, name):
    rows, cols = w.shape
    tr = 256 if rows % 256 == 0 else rows
    c1 = 1.0 - ADAM_B1 ** ADAM_STEP
    c2 = 1.0 - ADAM_B2 ** ADAM_STEP

    def body(w_ref, g_ref, m_ref, v_ref, d_ref, nm_ref, nv_ref):
        gv = g_ref[...]
        nm = ADAM_B1 * m_ref[...] + (1.0 - ADAM_B1) * gv
        nv = ADAM_B2 * v_ref[...] + (1.0 - ADAM_B2) * jnp.square(gv)
        d_ref[...] = -ADAM_LR * ((nm / c1) / (jnp.sqrt(nv / c2) + ADAM_EPS) + ADAM_WD * w_ref[...])
        nm_ref[...] = nm
        nv_ref[...] = nv

    blk = pl.BlockSpec((tr, cols), lambda i: (i, 0))
    sh = jax.ShapeDtypeStruct((rows, cols), F32)
    return pl.pallas_call(
        body, name=name, grid=(rows // tr,), in_specs=[blk] * 4, out_specs=[blk] * 3, out_shape=[sh] * 3,
        compiler_params=_cp(("parallel",)),
    )(w, g, m, v)


WEIGHTS = ("norm_w", "w_in", "ssm_a_re", "ssm_a_im", "ssm_log_dt", "ssm_b_re", "ssm_b_im", "ssm_c_re", "ssm_c_im",
           "ssm_d", "ssm_glu_w", "ssm_glu_b", "sg_ln_w", "sg_ln_b", "sg_w", "sg_b", "attn_sinks",
           "w_branch_a", "w_branch_b", "w_branch_c", "w_out", "final_norm_w")
BIG = ("w_in", "ssm_glu_w", "w_branch_a", "w_branch_b", "w_branch_c", "w_out")
BIG_KEY = {"w_in": ("win_t", True), "ssm_glu_w": ("glu", False), "w_branch_a": ("wba_t", True),
           "w_branch_b": ("wbb_t", True), "w_branch_c": ("wbc_t", True), "w_out": ("wout", False)}
PACK_COLS = 1024


def _pack(arrs):
    flat = jnp.concatenate([a.reshape(-1) for a in arrs])
    rows = -(-flat.shape[0] // (8 * N_DEV * PACK_COLS)) * 8 * N_DEV
    return jnp.pad(flat, (0, rows * PACK_COLS - flat.shape[0])).reshape(rows, PACK_COLS)


def _unpack(pack, like):
    flat = pack.reshape(-1)
    out, off = [], 0
    for a in like:
        out.append(flat[off:off + a.size].reshape(a.shape))
        off += a.size
    return out


def kernel(x, norm_w, w_in, ssm_a_re, ssm_a_im, ssm_log_dt, ssm_b_re, ssm_b_im, ssm_c_re, ssm_c_im, ssm_d, ssm_glu_w, ssm_glu_b, sg_ln_w, sg_ln_b, sg_w, sg_b, attn_sinks, w_branch_a, w_branch_b, w_branch_c, w_out, final_norm_w, loss_target, m_norm_w, m_w_in, m_ssm_a_re, m_ssm_a_im, m_ssm_log_dt, m_ssm_b_re, m_ssm_b_im, m_ssm_c_re, m_ssm_c_im, m_ssm_d, m_ssm_glu_w, m_ssm_glu_b, m_sg_ln_w, m_sg_ln_b, m_sg_w, m_sg_b, m_attn_sinks, m_w_branch_a, m_w_branch_b, m_w_branch_c, m_w_out, m_final_norm_w, v_norm_w, v_w_in, v_ssm_a_re, v_ssm_a_im, v_ssm_log_dt, v_ssm_b_re, v_ssm_b_im, v_ssm_c_re, v_ssm_c_im, v_ssm_d, v_ssm_glu_w, v_ssm_glu_b, v_sg_ln_w, v_sg_ln_b, v_sg_w, v_sg_b, v_attn_sinks, v_w_branch_a, v_w_branch_b, v_w_branch_c, v_w_out, v_final_norm_w):
    w = dict(zip(WEIGHTS, (norm_w, w_in, ssm_a_re, ssm_a_im, ssm_log_dt, ssm_b_re, ssm_b_im, ssm_c_re, ssm_c_im, ssm_d, ssm_glu_w, ssm_glu_b, sg_ln_w, sg_ln_b, sg_w, sg_b, attn_sinks, w_branch_a, w_branch_b, w_branch_c, w_out, final_norm_w)))
    m = dict(zip(WEIGHTS, (m_norm_w, m_w_in, m_ssm_a_re, m_ssm_a_im, m_ssm_log_dt, m_ssm_b_re, m_ssm_b_im, m_ssm_c_re, m_ssm_c_im, m_ssm_d, m_ssm_glu_w, m_ssm_glu_b, m_sg_ln_w, m_sg_ln_b, m_sg_w, m_sg_b, m_attn_sinks, m_w_branch_a, m_w_branch_b, m_w_branch_c, m_w_out, m_final_norm_w)))
    v = dict(zip(WEIGHTS, (v_norm_w, v_w_in, v_ssm_a_re, v_ssm_a_im, v_ssm_log_dt, v_ssm_b_re, v_ssm_b_im, v_ssm_c_re, v_ssm_c_im, v_ssm_d, v_ssm_glu_w, v_ssm_glu_b, v_sg_ln_w, v_sg_ln_b, v_sg_w, v_sg_b, v_attn_sinks, v_w_branch_a, v_w_branch_b, v_w_branch_c, v_w_out, v_final_norm_w)))

    keys = [BIG_KEY[n][0] for n in BIG]
    shards = [[(w[n][l].T if BIG_KEY[n][1] else w[n][l]).astype(BF16) for n in BIG] for l in range(DEPTH)]
    small_p = [{n: w[n][l] for n in SMALL} for l in range(DEPTH)]
    xv, tgt = x[0], loss_target[0]
    tabs = _rope_tables(xv.shape[0])

    ag0a = _allgather_start(shards[0][:1], "ag_l0_win")
    win0 = _allgather_finish(ag0a, ag0a[4], "ag_l0_win")[0]
    ag0b = _allgather_start(shards[0][1:], "ag_l0_rest", after=win0)
    ag1a = _allgather_start(shards[1][:1], "ag_l1_win", after=ag0b[4])
    ag1b = _allgather_start(shards[1][1:], "ag_l1_rest", after=ag1a[4])
    got = {}

    def after_proj0(proj):
        got["w0"] = dict(zip(keys, [win0] + _allgather_finish(ag0b, proj, "ag_l0_rest")))
        return got["w0"]

    x1, saved0 = _layer_fwd(xv, small_p[0], {"win_t": win0}, tabs, "l0", proj_after=ag1b[4], after_proj=after_proj0)
    big_w0 = got["w0"]
    win1 = _allgather_finish(ag1a, x1, "ag_l1_win")[0]

    def after_proj1(proj):
        got["w1"] = dict(zip(keys, [win1] + _allgather_finish(ag1b, proj, "ag_l1_rest")))
        return got["w1"]

    x2, saved1 = _layer_fwd(x1, small_p[1], {"win_t": win1}, tabs, "l1", after_proj=after_proj1)
    big_w1 = got["w1"]
    loss_acc, dx2, dfw = _final(x2, w["final_norm_w"][None], tgt, "final_norm_loss")
    loss = lax.psum(loss_acc[0, 0], ("x", "y", "c"))
    dfw = dfw[0]

    dx1, big_g1, small_g1 = _layer_bwd(dx2, small_p[1], big_w1, tabs, saved1, "l1")
    rs1 = _reduce_scatter_start([big_g1[k] for k in keys], "l1")

    def before_win0(big):
        got["rs0b"] = _reduce_scatter_start([big[k] for k in keys[1:]], "l0_rest")
        return got["rs0b"][4]

    def after_win0(big):
        got["rs0a"] = _reduce_scatter_start([big["win_t"]], "l0_win")
        return got["rs0a"][4]

    dx, big_g0, small_g0 = _layer_bwd(dx1, small_p[0], big_w0, tabs, saved0, "l0", first_after=rs1[4],
                                      before_win=before_win0, after_win=after_win0)
    red1 = _reduce_scatter_finish(rs1, dx, "l1")
    small_g = [small_g0, small_g1]

    grads, delta, new_m, new_v = {}, {}, {}, {}
    small_names = [n for n in WEIGHTS if n not in BIG]
    small_list = [jnp.stack([small_g[l][n] for l in range(DEPTH)]) if n != "final_norm_w" else dfw for n in small_names]
    red_small = _allreduce_small(_pack(small_list), "allreduce_small")
    for n, g in zip(small_names, _unpack(red_small, small_list)):
        grads[n] = g
    packs = [_pack([d[n] for n in small_names]) for d in (w, m, v)]
    outs = _adamw(packs[0], red_small, packs[1], packs[2], "adamw_small")
    for res, o in zip((delta, new_m, new_v), outs):
        for n, a in zip(small_names, _unpack(o, small_list)):
            res[n] = a

    red0 = (_reduce_scatter_finish(got["rs0a"], outs[0], "l0_win")
            + _reduce_scatter_finish(got["rs0b"], outs[0], "l0_rest"))
    for i, n in enumerate(BIG):
        grads[n] = jnp.stack([g.T if BIG_KEY[n][1] else g for g in (red0[i], red1[i])])
    for n in BIG:
        shp = w[n].shape
        two_d = lambda a: a.reshape(-1, shp[-1])
        d_, m_, v_ = _adamw(two_d(w[n]), two_d(grads[n]), two_d(m[n]), two_d(v[n]), f"adamw_{n}")
        delta[n], new_m[n], new_v[n] = d_.reshape(shp), m_.reshape(shp), v_.reshape(shp)

    return (loss, dx[None], *[grads[n] for n in WEIGHTS], *[delta[n] for n in WEIGHTS],
            *[new_m[n] for n in WEIGHTS], *[new_v[n] for n in WEIGHTS])
```

```python
import functools
import math

import jax
import jax.numpy as jnp
from jax import lax
from jax.experimental import pallas as pl
from jax.experimental.pallas import tpu as pltpu

F32 = jnp.float32
BF16 = jnp.bfloat16

D_MODEL = 2048
DEPTH = 2
EPS = 1e-6
NEG_INF = -1e30
N_DEV = 8

SSM_WIDTH = 1024
SSM_GROUP = 16
SSM_GROUPS = 64
SSM_STATE = 64
N_SLAB = 8
SLAB_CH = 128
SLAB_ST = 512
N_SEG = 8
SEG_PAD = 8

SG_HEADS = 8
CHUNK = 128
HEAD_DIM = 64
ATT_HEADS = 16
ROT_DIM = 16
ROPE_THETA = 500000.0

D_IN = 13568
OFF_UA, OFF_ZA, OFF_UB, OFF_VB, OFF_ZB, OFF_Q, OFF_KV, OFF_ZC, OFF_G = (
    0, 1024, 2048, 3072, 4096, 5120, 6144, 6400, 7424)

ADAM_LR, ADAM_B1, ADAM_B2, ADAM_EPS, ADAM_WD, ADAM_STEP = 0.001, 0.9, 0.999, 1e-08, 0.01, 10

VMEM_LIMIT = 56 * 1024 * 1024


def _cp(sem=None):
    return pltpu.CompilerParams(dimension_semantics=sem, vmem_limit_bytes=VMEM_LIMIT)


def _dot(a, b):
    return jnp.dot(a, b, preferred_element_type=F32)


def _dot_nt(a, b):
    return lax.dot_general(a, b, (((1,), (1,)), ((), ())), preferred_element_type=F32)


def _dot_tn(a, b):
    return lax.dot_general(a, b, (((0,), (0,)), ((), ())), preferred_element_type=F32)


def _mm(a, b, mode, out_dtype, tm, tn, tk, name, res=None, after=None):
    if mode == "nn":
        (m, k), (_, n) = a.shape, b.shape
    elif mode == "nt":
        (m, k), (n, _) = a.shape, b.shape
    else:
        (k, m), (_, n) = a.shape, b.shape
    tm, tn, tk = min(tm, m), min(tn, n), min(tk, k)
    assert m % tm == 0 and n % tn == 0 and k % tk == 0, (name, m, n, k, tm, tn, tk)
    nk = k // tk
    a_spec = {"nn": pl.BlockSpec((tm, tk), lambda i, j, kk: (i, kk)),
              "nt": pl.BlockSpec((tm, tk), lambda i, j, kk: (i, kk)),
              "tn": pl.BlockSpec((tk, tm), lambda i, j, kk: (kk, i))}[mode]
    b_spec = {"nn": pl.BlockSpec((tk, tn), lambda i, j, kk: (kk, j)),
              "nt": pl.BlockSpec((tn, tk), lambda i, j, kk: (j, kk)),
              "tn": pl.BlockSpec((tk, tn), lambda i, j, kk: (kk, j))}[mode]
    dot = {"nn": _dot, "nt": _dot_nt, "tn": _dot_tn}[mode]
    has_res = res is not None

    def body(*refs):
        if after is not None:
            refs = refs[:-3] + refs[-2:]
        if has_res:
            a_ref, b_ref, r_ref, o_ref, acc = refs
        else:
            a_ref, b_ref, o_ref, acc = refs
        kk = pl.program_id(2)

        @pl.when(kk == 0)
        def _():
            acc[...] = jnp.zeros_like(acc)

        acc[...] += dot(a_ref[...].astype(BF16), b_ref[...].astype(BF16))

        @pl.when(kk == nk - 1)
        def _():
            r = acc[...]
            if has_res:
                r = r + r_ref[...]
            o_ref[...] = r.astype(out_dtype)

    in_specs = [a_spec, b_spec]
    args = [a, b]
    if has_res:
        in_specs.append(pl.BlockSpec((tm, tn), lambda i, j, kk: (i, j)))
        args.append(res)
    if after is not None:
        in_specs.append(pl.BlockSpec(memory_space=pl.ANY))
        args.append(after)
    return pl.pallas_call(
        body, name=name,
        grid=(m // tm, n // tn, nk),
        in_specs=in_specs,
        out_specs=pl.BlockSpec((tm, tn), lambda i, j, kk: (i, j)),
        out_shape=jax.ShapeDtypeStruct((m, n), out_dtype),
        scratch_shapes=[pltpu.VMEM((tm, tn), F32)],
        compiler_params=_cp(("parallel", "parallel", "arbitrary")),
    )(*args)


def _rms(x, w):
    return x * lax.rsqrt(jnp.mean(x * x, axis=-1, keepdims=True) + EPS) * w


def _rms_fwd(x, w, name):
    L, D = x.shape
    tm = min(L, 256)

    def body(x_ref, w_ref, h_ref):
        h_ref[...] = _rms(x_ref[...], w_ref[...]).astype(BF16)

    return pl.pallas_call(
        body, name=name, grid=(L // tm,),
        in_specs=[pl.BlockSpec((tm, D), lambda i: (i, 0)), pl.BlockSpec((1, D), lambda i: (0, 0))],
        out_specs=pl.BlockSpec((tm, D), lambda i: (i, 0)),
        out_shape=jax.ShapeDtypeStruct((L, D), BF16),
        compiler_params=_cp(("parallel",)),
    )(x, w)


def _rms_bwd(x, w, dh, dres, name):
    L, D = x.shape
    tm = min(L, 256)

    def body(x_ref, w_ref, dh_ref, dres_ref, dx_ref, dw_ref):
        _, vjp = jax.vjp(_rms, x_ref[...], w_ref[...])
        dx, dw = vjp(dh_ref[...])
        dx_ref[...] = dx + dres_ref[...]

        @pl.when(pl.program_id(0) == 0)
        def _():
            dw_ref[...] = jnp.zeros_like(dw_ref)

        dw_ref[...] += dw

    row = pl.BlockSpec((tm, D), lambda i: (i, 0))
    vec = pl.BlockSpec((1, D), lambda i: (0, 0))
    return pl.pallas_call(
        body, name=name, grid=(L // tm,),
        in_specs=[row, vec, row, row],
        out_specs=[row, vec],
        out_shape=[jax.ShapeDtypeStruct((L, D), F32), jax.ShapeDtypeStruct((1, D), F32)],
        compiler_params=_cp(("arbitrary",)),
    )(x, w, dh, dres)


def _final(x, fw, tgt, name):
    L, D = x.shape
    tm = min(L, 256)

    def loss_fn(xv, wv, tv):
        err = _rms(xv, wv) - tv
        return jnp.sum(err * err) * (0.5 / D)

    def body(x_ref, w_ref, t_ref, loss_ref, dx_ref, dw_ref):
        tv = t_ref[...]
        val, vjp = jax.vjp(lambda a, b: loss_fn(a, b, tv), x_ref[...], w_ref[...])
        dx, dw = vjp(jnp.ones((), F32))
        dx_ref[...] = dx

        @pl.when(pl.program_id(0) == 0)
        def _():
            dw_ref[...] = jnp.zeros_like(dw_ref)
            loss_ref[...] = jnp.zeros_like(loss_ref)

        dw_ref[...] += dw
        loss_ref[...] += jnp.full(loss_ref.shape, val, F32)

    row = pl.BlockSpec((tm, D), lambda i: (i, 0))
    vec = pl.BlockSpec((1, D), lambda i: (0, 0))
    return pl.pallas_call(
        body, name=name, grid=(L // tm,),
        in_specs=[row, vec, row],
        out_specs=[pl.BlockSpec((8, 128), lambda i: (0, 0)), row, vec],
        out_shape=[jax.ShapeDtypeStruct((8, 128), F32), jax.ShapeDtypeStruct((L, D), F32),
                   jax.ShapeDtypeStruct((1, D), F32)],
        compiler_params=_cp(("arbitrary",)),
    )(x, fw, tgt)


def _s5_param_fn(a_re, a_im, log_dt, bt_re, bt_im):
    dt = jnp.exp(log_dt)
    zr, zi = a_re * dt, a_im * dt
    er = jnp.exp(zr)
    lr, li = er * jnp.cos(zi), er * jnp.sin(zi)
    nr, ni = lr - 1.0, li
    den = a_re * a_re + a_im * a_im
    cr = (nr * a_re + ni * a_im) / den
    ci = (ni * a_re - nr * a_im) / den
    bbr = cr[None] * bt_re - ci[None] * bt_im
    bbi = cr[None] * bt_im + ci[None] * bt_re
    return lr, li, bbr, bbi


def _s5_params_fwd(a_re, a_im, log_dt, bt_re, bt_im, name):
    def body(ar, ai, ld, br, bi, lr, li, bbr, bbi):
        o = _s5_param_fn(ar[...], ai[...], ld[...], br[...], bi[...])
        lr[...], li[...], bbr[...], bbi[...] = o

    gp = jax.ShapeDtypeStruct(a_re.shape, F32)
    cgp = jax.ShapeDtypeStruct(bt_re.shape, F32)
    return pl.pallas_call(body, name=name, out_shape=[gp, gp, cgp, cgp])(a_re, a_im, log_dt, bt_re, bt_im)


def _s5_params_bwd(a_re, a_im, log_dt, bt_re, bt_im, dlr, dli, dbbr, dbbi, name):
    def body(ar, ai, ld, br, bi, g0, g1, g2, g3, o0, o1, o2, o3, o4):
        _, vjp = jax.vjp(_s5_param_fn, ar[...], ai[...], ld[...], br[...], bi[...])
        o0[...], o1[...], o2[...], o3[...], o4[...] = vjp((g0[...], g1[...], g2[...], g3[...]))

    gp = jax.ShapeDtypeStruct(a_re.shape, F32)
    cgp = jax.ShapeDtypeStruct(bt_re.shape, F32)
    return pl.pallas_call(body, name=name,
                          out_shape=[gp, gp, jax.ShapeDtypeStruct(log_dt.shape, F32), cgp, cgp])(
        a_re, a_im, log_dt, bt_re, bt_im, dlr, dli, dbbr, dbbi)


def _cmul(ar, ai, br, bi):
    return ar * br - ai * bi, ar * bi + ai * br


def _cpow(lr, li, n):
    rr, ri = None, None
    br, bi = lr, li
    while n:
        if n & 1:
            rr, ri = (br, bi) if rr is None else _cmul(rr, ri, br, bi)
        n >>= 1
        if n:
            br, bi = _cmul(br, bi, br, bi)
    return rr, ri


def _shift_rows(x, up):
    row = lax.broadcasted_iota(jnp.int32, x.shape, 0)
    if up:
        return jnp.where(row == N_SEG - 1, 0.0, pltpu.roll(x, N_SEG - 1, 0))
    return jnp.where(row == 0, 0.0, pltpu.roll(x, 1, 0))


def _seg_scan(s_re, s_im, lam, seg, reverse):
    stride = seg + SEG_PAD
    nt = SLAB_ST // 128
    lam_t = [(jnp.broadcast_to(lam[0][:, j * 128:(j + 1) * 128], (N_SEG, 128)),
              jnp.broadcast_to(lam[1][:, j * 128:(j + 1) * 128], (N_SEG, 128))) for j in range(nt)]

    def rows(i):
        return pl.ds(i, N_SEG, stride=stride)

    def step1(t, carry):
        i = seg - 1 - t if reverse else t
        out = []
        for j in range(nt):
            cr, ci = carry[2 * j], carry[2 * j + 1]
            nr, ni = _cmul(lam_t[j][0], lam_t[j][1], cr, ci)
            nr = nr + s_re[j, rows(i), :]
            ni = ni + s_im[j, rows(i), :]
            s_re[j, rows(i), :] = nr
            s_im[j, rows(i), :] = ni
            out += [nr, ni]
        return tuple(out)

    zero = tuple(jnp.zeros((N_SEG, 128), F32) for _ in range(2 * nt))
    ends = lax.fori_loop(0, seg, step1, zero)

    carries = []
    for j in range(nt):
        pr, pi = _cpow(lam_t[j][0], lam_t[j][1], seg)
        cr, ci = jnp.zeros((N_SEG, 128), F32), jnp.zeros((N_SEG, 128), F32)
        for _ in range(N_SEG - 1):
            tr, ti = _cmul(pr, pi, cr, ci)
            cr = _shift_rows(tr + ends[2 * j], reverse)
            ci = _shift_rows(ti + ends[2 * j + 1], reverse)
        carries += [cr, ci]

    def step2(t, pw):
        i = seg - 1 - t if reverse else t
        out = []
        for j in range(nt):
            pr, pi = pw[2 * j], pw[2 * j + 1]
            ar, ai = _cmul(pr, pi, carries[2 * j], carries[2 * j + 1])
            s_re[j, rows(i), :] = s_re[j, rows(i), :] + ar
            s_im[j, rows(i), :] = s_im[j, rows(i), :] + ai
            qr, qi = _cmul(pr, pi, lam_t[j][0], lam_t[j][1])
            out += [qr, qi]
        return tuple(out)

    lax.fori_loop(0, seg, step2, tuple(x for j in range(nt) for x in lam_t[j]))
    return carries


def _seg_rows(ref, k, seg):
    stride = seg + SEG_PAD
    return jnp.concatenate([ref[j, pl.ds(k * stride, seg), :] for j in range(SLAB_ST // 128)], axis=-1)


def _seg_store(ref, k, seg, val):
    stride = seg + SEG_PAD
    for j in range(SLAB_ST // 128):
        ref[j, pl.ds(k * stride, seg), :] = val[:, j * 128:(j + 1) * 128]


def _s5_specs(L):
    col = lambda off: pl.BlockSpec((L, SLAB_CH), lambda j: (0, off + j))
    mat_b = pl.BlockSpec((None, SLAB_CH, SLAB_ST), lambda j: (j, 0, 0))
    mat_c = pl.BlockSpec((None, SLAB_ST, SLAB_CH), lambda j: (j, 0, 0))
    vec_s = pl.BlockSpec((None, 1, SLAB_ST), lambda j: (j, 0, 0))
    vec_c = pl.BlockSpec((None, 1, SLAB_CH), lambda j: (j, 0, 0))
    return col, mat_b, mat_c, vec_s, vec_c


def _s5_states(u_ref, bre_ref, bim_ref, lam, s_re, s_im, seg):
    for k in range(N_SEG):
        uk = u_ref[pl.ds(k * seg, seg), :]
        _seg_store(s_re, k, seg, _dot(uk, bre_ref[...]))
        _seg_store(s_im, k, seg, _dot(uk, bim_ref[...]))
    return _seg_scan(s_re, s_im, lam, seg, reverse=False)


def _s5_fwd(proj, bre, bim, cre_t, cim_t, lam_re, lam_im, dvec, name):
    L = proj.shape[0]
    seg = L // N_SEG
    col, mat_b, mat_c, vec_s, vec_c = _s5_specs(L)
    rows = N_SEG * (seg + SEG_PAD)

    def body(u_ref, bre_ref, bim_ref, cre_ref, cim_ref, lr_ref, li_ref, d_ref, y_ref, s_re, s_im):
        _s5_states(u_ref, bre_ref, bim_ref, (lr_ref[...], li_ref[...]), s_re, s_im, seg)
        for k in range(N_SEG):
            y = (_dot(_seg_rows(s_re, k, seg).astype(BF16), cre_ref[...])
                 - _dot(_seg_rows(s_im, k, seg).astype(BF16), cim_ref[...]))
            y = y + d_ref[...] * u_ref[pl.ds(k * seg, seg), :].astype(F32)
            y_ref[pl.ds(k * seg, seg), :] = jax.nn.gelu(y).astype(BF16)

    return pl.pallas_call(
        body, name=name, grid=(N_SLAB,),
        in_specs=[col(OFF_UA // SLAB_CH), mat_b, mat_b, mat_c, mat_c, vec_s, vec_s, vec_c],
        out_specs=pl.BlockSpec((L, SLAB_CH), lambda j: (0, j)),
        out_shape=jax.ShapeDtypeStruct((L, SSM_WIDTH), BF16),
        scratch_shapes=[pltpu.VMEM((SLAB_ST // 128, rows, 128), F32)] * 2,
        compiler_params=_cp(("parallel",)),
    )(proj, bre, bim, cre_t, cim_t, lam_re, lam_im, dvec)


def _s5_bwd(proj, dy, bre, bim, cre_t, cim_t, lam_re, lam_im, dvec, name):
    L = proj.shape[0]
    seg = L // N_SEG
    stride = seg + SEG_PAD
    col, mat_b, mat_c, vec_s, vec_c = _s5_specs(L)
    rows = N_SEG * stride
    nt = SLAB_ST // 128

    def body(u_ref, dy_ref, bre_ref, bim_ref, cre_ref, cim_ref, lr_ref, li_ref, d_ref,
             du_ref, dbre_ref, dbim_ref, dcre_ref, dcim_ref, dlr_ref, dli_ref, dd_ref,
             s_re, s_im, a_re, a_im, dyp):
        lam = (lr_ref[...], li_ref[...])
        carry_s = _s5_states(u_ref, bre_ref, bim_ref, lam, s_re, s_im, seg)
        dcre = jnp.zeros((SLAB_ST, SLAB_CH), F32)
        dcim = jnp.zeros((SLAB_ST, SLAB_CH), F32)
        dd = jnp.zeros((1, SLAB_CH), F32)
        for k in range(N_SEG):
            sre = _seg_rows(s_re, k, seg).astype(BF16)
            sim = _seg_rows(s_im, k, seg).astype(BF16)
            uk = u_ref[pl.ds(k * seg, seg), :].astype(F32)
            ypre = _dot(sre, cre_ref[...]) - _dot(sim, cim_ref[...]) + d_ref[...] * uk
            _, vjp = jax.vjp(jax.nn.gelu, ypre)
            (dyk,) = vjp(dy_ref[pl.ds(k * seg, seg), :].astype(F32))
            dyp[pl.ds(k * seg, seg), :] = dyk
            dd = dd + jnp.sum(dyk * uk, axis=0, keepdims=True)
            dyb = dyk.astype(BF16)
            dcre = dcre + _dot_tn(sre, dyb)
            dcim = dcim - _dot_tn(sim, dyb)
            _seg_store(a_re, k, seg, _dot_nt(dyb, cre_ref[...]))
            _seg_store(a_im, k, seg, -_dot_nt(dyb, cim_ref[...]))
        dcre_ref[...] = dcre
        dcim_ref[...] = dcim
        dd_ref[...] = dd

        _seg_scan(a_re, a_im, (lam[0], -lam[1]), seg, reverse=True)

        def acc_dlam(i, acc):
            out = []
            for j in range(nt):
                ar = a_re[j, pl.ds(i, N_SEG, stride=stride), :]
                ai = a_im[j, pl.ds(i, N_SEG, stride=stride), :]
                pr = s_re[j, pl.ds(i - 1, N_SEG, stride=stride), :]
                pi = s_im[j, pl.ds(i - 1, N_SEG, stride=stride), :]
                out += [acc[2 * j] + ar * pr + ai * pi, acc[2 * j + 1] + ai * pr - ar * pi]
            return tuple(out)

        first = []
        for j in range(nt):
            ar = a_re[j, pl.ds(0, N_SEG, stride=stride), :]
            ai = a_im[j, pl.ds(0, N_SEG, stride=stride), :]
            pr, pi = carry_s[2 * j], carry_s[2 * j + 1]
            first += [ar * pr + ai * pi, ai * pr - ar * pi]
        acc = lax.fori_loop(1, seg, acc_dlam, tuple(first))
        dlr_ref[...] = jnp.concatenate([jnp.sum(acc[2 * j], axis=0, keepdims=True) for j in range(nt)], axis=-1)
        dli_ref[...] = jnp.concatenate([jnp.sum(acc[2 * j + 1], axis=0, keepdims=True) for j in range(nt)], axis=-1)

        dbre = jnp.zeros((SLAB_CH, SLAB_ST), F32)
        dbim = jnp.zeros((SLAB_CH, SLAB_ST), F32)
        for k in range(N_SEG):
            are = _seg_rows(a_re, k, seg).astype(BF16)
            aim = _seg_rows(a_im, k, seg).astype(BF16)
            uk = u_ref[pl.ds(k * seg, seg), :]
            du = _dot_nt(are, bre_ref[...]) + _dot_nt(aim, bim_ref[...]) + dyp[pl.ds(k * seg, seg), :] * d_ref[...]
            du_ref[pl.ds(k * seg, seg), :] = du.astype(BF16)
            dbre = dbre + _dot_tn(uk, are)
            dbim = dbim + _dot_tn(uk, aim)
        dbre_ref[...] = dbre
        dbim_ref[...] = dbim

    scan_buf = pltpu.VMEM((nt, rows, 128), F32)
    return pl.pallas_call(
        body, name=name, grid=(N_SLAB,),
        in_specs=[col(OFF_UA // SLAB_CH), pl.BlockSpec((L, SLAB_CH), lambda j: (0, j)),
                  mat_b, mat_b, mat_c, mat_c, vec_s, vec_s, vec_c],
        out_specs=[pl.BlockSpec((L, SLAB_CH), lambda j: (0, j)), mat_b, mat_b, mat_c, mat_c, vec_s, vec_s, vec_c],
        out_shape=[jax.ShapeDtypeStruct((L, SSM_WIDTH), BF16),
                   jax.ShapeDtypeStruct((N_SLAB, SLAB_CH, SLAB_ST), F32),
                   jax.ShapeDtypeStruct((N_SLAB, SLAB_CH, SLAB_ST), F32),
                   jax.ShapeDtypeStruct((N_SLAB, SLAB_ST, SLAB_CH), F32),
                   jax.ShapeDtypeStruct((N_SLAB, SLAB_ST, SLAB_CH), F32),
                   jax.ShapeDtypeStruct((N_SLAB, 1, SLAB_ST), F32),
                   jax.ShapeDtypeStruct((N_SLAB, 1, SLAB_ST), F32),
                   jax.ShapeDtypeStruct((N_SLAB, 1, SLAB_CH), F32)],
        scratch_shapes=[scan_buf, scan_buf, scan_buf, scan_buf, pltpu.VMEM((L, SLAB_CH), F32)],
        compiler_params=_cp(("parallel",)),
    )(proj, dy, bre, bim, cre_t, cim_t, lam_re, lam_im, dvec)


def _glu_point(y0, pre, za, b):
    return y0 * jax.nn.sigmoid(pre + b) * jax.nn.silu(za)


def _glu_specs(L, tm):
    row = pl.BlockSpec((tm, SSM_WIDTH), lambda i: (i, 0))
    za = pl.BlockSpec((tm, SSM_WIDTH), lambda i: (i, OFF_ZA // SSM_WIDTH))
    wmat = pl.BlockSpec((SSM_WIDTH, SSM_WIDTH), lambda i: (0, 0))
    vec = pl.BlockSpec((1, SSM_WIDTH), lambda i: (0, 0))
    return row, za, wmat, vec


def _glu_fwd(ya0, proj, w, b, name):
    L = ya0.shape[0]
    tm = min(L, 512)
    row, za, wmat, vec = _glu_specs(L, tm)

    def body(y_ref, z_ref, w_ref, b_ref, o_ref):
        y0 = y_ref[...]
        pre = _dot(y0, w_ref[...])
        o_ref[...] = _glu_point(y0.astype(F32), pre, z_ref[...].astype(F32), b_ref[...]).astype(BF16)

    return pl.pallas_call(
        body, name=name, grid=(L // tm,), in_specs=[row, za, wmat, vec], out_specs=row,
        out_shape=jax.ShapeDtypeStruct((L, SSM_WIDTH), BF16), compiler_params=_cp(("parallel",)),
    )(ya0, proj, w, b)


def _glu_bwd(ya0, proj, w, b, dya, name):
    L = ya0.shape[0]
    tm = min(L, 512)
    row, za, wmat, vec = _glu_specs(L, tm)

    def body(y_ref, z_ref, w_ref, b_ref, g_ref, dy0_ref, dza_ref, dw_ref, db_ref):
        y0 = y_ref[...]
        pre = _dot(y0, w_ref[...])
        _, vjp = jax.vjp(_glu_point, y0.astype(F32), pre, z_ref[...].astype(F32), b_ref[...])
        dy0, dpre, dza, db = vjp(g_ref[...].astype(F32))
        dpb = dpre.astype(BF16)
        dy0_ref[...] = (dy0 + _dot_nt(dpb, w_ref[...])).astype(BF16)
        dza_ref[...] = dza.astype(BF16)

        @pl.when(pl.program_id(0) == 0)
        def _():
            dw_ref[...] = jnp.zeros_like(dw_ref)
            db_ref[...] = jnp.zeros_like(db_ref)

        dw_ref[...] += _dot_tn(y0, dpb)
        db_ref[...] += db

    return pl.pallas_call(
        body, name=name, grid=(L // tm,), in_specs=[row, za, wmat, vec, row],
        out_specs=[row, row, wmat, vec],
        out_shape=[jax.ShapeDtypeStruct((L, SSM_WIDTH), BF16), jax.ShapeDtypeStruct((L, SSM_WIDTH), BF16),
                   jax.ShapeDtypeStruct((SSM_WIDTH, SSM_WIDTH), F32), jax.ShapeDtypeStruct((1, SSM_WIDTH), F32)],
        compiler_params=_cp(("arbitrary",)),
    )(ya0, proj, w, b, dya)


def _sg_norm(vb, ln_w, ln_b):
    v0 = jax.nn.gelu(vb)
    mu = jnp.mean(v0, axis=-1, keepdims=True)
    var = jnp.mean(jnp.square(v0 - mu), axis=-1, keepdims=True)
    return (v0 - mu) * lax.rsqrt(var + EPS) * ln_w + ln_b


def _sg_gate(ub, mixed, zb):
    return jax.nn.gelu(ub) * mixed * jax.nn.silu(zb)


def _sg_specs():
    W = SSM_WIDTH
    blk = lambda off: pl.BlockSpec((CHUNK, W), lambda n: (n, off // W))
    out = pl.BlockSpec((CHUNK, W), lambda n: (n, 0))
    vec = pl.BlockSpec((1, W), lambda n: (0, 0))
    wsp = pl.BlockSpec((SG_HEADS, CHUNK, CHUNK), lambda n: (0, 0, 0))
    bsp = pl.BlockSpec((SG_HEADS, CHUNK, 1), lambda n: (0, 0, 0))
    return blk, out, vec, wsp, bsp


def _sg_masked(w_ref):
    t = lax.broadcasted_iota(jnp.int32, (CHUNK, CHUNK), 0)
    s = lax.broadcasted_iota(jnp.int32, (CHUNK, CHUNK), 1)
    causal = s <= t
    return causal, [jnp.where(causal, w_ref[h], 0.0).astype(BF16) for h in range(SG_HEADS)]


def _sg_mix(wm, vnb, bias_ref):
    return jnp.concatenate(
        [_dot(wm[h], vnb[:, h * CHUNK:(h + 1) * CHUNK]) + bias_ref[h] for h in range(SG_HEADS)], axis=-1)


def _sg_fwd(proj, ln_w, ln_b, w, bias, name):
    L = proj.shape[0]
    blk, out, vec, wsp, bsp = _sg_specs()

    def body(ub_ref, vb_ref, zb_ref, lw_ref, lb_ref, w_ref, bias_ref, o_ref):
        _, wm = _sg_masked(w_ref)
        vnb = _sg_norm(vb_ref[...].astype(F32), lw_ref[...], lb_ref[...]).astype(BF16)
        mixed = _sg_mix(wm, vnb, bias_ref)
        o_ref[...] = _sg_gate(ub_ref[...].astype(F32), mixed, zb_ref[...].astype(F32)).astype(BF16)

    return pl.pallas_call(
        body, name=name, grid=(L // CHUNK,),
        in_specs=[blk(OFF_UB), blk(OFF_VB), blk(OFF_ZB), vec, vec, wsp, bsp], out_specs=out,
        out_shape=jax.ShapeDtypeStruct((L, SSM_WIDTH), BF16), compiler_params=_cp(("parallel",)),
    )(proj, proj, proj, ln_w, ln_b, w, bias)


def _sg_bwd(proj, ln_w, ln_b, w, bias, dyb, name):
    L = proj.shape[0]
    blk, out, vec, wsp, bsp = _sg_specs()

    def body(ub_ref, vb_ref, zb_ref, lw_ref, lb_ref, w_ref, bias_ref, g_ref,
             dub_ref, dvb_ref, dzb_ref, dlw_ref, dlb_ref, dw_ref, dbias_ref):
        causal, wm = _sg_masked(w_ref)
        vb = vb_ref[...].astype(F32)
        vn, vjp_norm = jax.vjp(_sg_norm, vb, lw_ref[...], lb_ref[...])
        vnb = vn.astype(BF16)
        mixed = _sg_mix(wm, vnb, bias_ref)
        _, vjp_gate = jax.vjp(_sg_gate, ub_ref[...].astype(F32), mixed, zb_ref[...].astype(F32))
        dub, dmixed, dzb = vjp_gate(g_ref[...].astype(F32))
        dub_ref[...] = dub.astype(BF16)
        dzb_ref[...] = dzb.astype(BF16)

        @pl.when(pl.program_id(0) == 0)
        def _():
            dlw_ref[...] = jnp.zeros_like(dlw_ref)
            dlb_ref[...] = jnp.zeros_like(dlb_ref)
            dw_ref[...] = jnp.zeros_like(dw_ref)
            dbias_ref[...] = jnp.zeros_like(dbias_ref)

        dvn = []
        for h in range(SG_HEADS):
            dm = dmixed[:, h * CHUNK:(h + 1) * CHUNK]
            dmb = dm.astype(BF16)
            dbias_ref[h] += jnp.sum(dm, axis=-1, keepdims=True)
            dw_ref[h] += jnp.where(causal, _dot_nt(dmb, vnb[:, h * CHUNK:(h + 1) * CHUNK]), 0.0)
            dvn.append(_dot_tn(wm[h], dmb))
        dvb, dlw, dlb = vjp_norm(jnp.concatenate(dvn, axis=-1))
        dvb_ref[...] = dvb.astype(BF16)
        dlw_ref[...] += dlw
        dlb_ref[...] += dlb

    act = jax.ShapeDtypeStruct((L, SSM_WIDTH), BF16)
    return pl.pallas_call(
        body, name=name, grid=(L // CHUNK,),
        in_specs=[blk(OFF_UB), blk(OFF_VB), blk(OFF_ZB), vec, vec, wsp, bsp, out],
        out_specs=[out, out, out, vec, vec, wsp, bsp],
        out_shape=[act, act, act, jax.ShapeDtypeStruct((1, SSM_WIDTH), F32), jax.ShapeDtypeStruct((1, SSM_WIDTH), F32),
                   jax.ShapeDtypeStruct((SG_HEADS, CHUNK, CHUNK), F32), jax.ShapeDtypeStruct((SG_HEADS, CHUNK, 1), F32)],
        compiler_params=_cp(("arbitrary",)),
    )(proj, proj, proj, ln_w, ln_b, w, bias, dyb)


def _rope_tables(L):
    half = ROT_DIM // 2
    inv_freq = ROPE_THETA ** (-jnp.arange(0, ROT_DIM, 2, dtype=F32) / ROT_DIM)
    ang = jnp.arange(L, dtype=F32)[:, None] * inv_freq[None, :]
    cos, sin = jnp.cos(ang), jnp.sin(ang)
    ones = jnp.ones((L, HEAD_DIM - ROT_DIM), F32)
    cos_h = jnp.concatenate([cos, cos, ones], axis=-1)
    sin_h = jnp.concatenate([-sin, sin, 0.0 * ones], axis=-1)
    src = jnp.arange(HEAD_DIM)[:, None]
    dst = jnp.arange(HEAD_DIM)[None, :]
    p_h = (((dst < half) & (src == dst + half)) | ((dst >= half) & (dst < ROT_DIM) & (src == dst - half))).astype(F32)
    p2 = jnp.kron(jnp.eye(2, dtype=F32), p_h).astype(BF16)
    return jnp.tile(cos_h, (1, 2)), jnp.tile(sin_h, (1, 2)), p2


def _rope(t, cos, sin, p2):
    n = t.shape[1] // 128
    tb = t.astype(BF16)
    sw = jnp.concatenate([_dot(tb[:, i * 128:(i + 1) * 128], p2) for i in range(n)], axis=-1) if n > 1 else _dot(tb, p2)
    return t * jnp.tile(cos, (1, n)) + sw * jnp.tile(sin, (1, n))


def _rope_t(g, cos, sin, p2):
    n = g.shape[1] // 128
    gs = (g * jnp.tile(sin, (1, n))).astype(BF16)
    sw = jnp.concatenate([_dot_nt(gs[:, i * 128:(i + 1) * 128], p2) for i in range(n)], axis=-1) if n > 1 else _dot_nt(gs, p2)
    return g * jnp.tile(cos, (1, n)) + sw


def _lane_lo(shape):
    return (lax.broadcasted_iota(jnp.int32, shape, len(shape) - 1) % 128) < HEAD_DIM


def _dup_halves(x):
    xr = pltpu.roll(x, HEAD_DIM, 1)
    lo = _lane_lo(x.shape)
    return jnp.where(lo, x, xr), jnp.where(lo, xr, x)


def _fold_halves(d0, d1):
    f0 = d0 + pltpu.roll(d0, HEAD_DIM, 1)
    f1 = d1 + pltpu.roll(d1, HEAD_DIM, 1)
    return jnp.where(_lane_lo(d0.shape), f0, f1)


def _attn_mask():
    qi = lax.broadcasted_iota(jnp.int32, (CHUNK, 2 * CHUNK), 0)
    kj = lax.broadcasted_iota(jnp.int32, (CHUNK, 2 * CHUNK), 1)
    return qi, kj


def _attn_specs():
    qsp = pl.BlockSpec((CHUNK, 1024), lambda n: (n, OFF_Q // 1024))
    kv_cur = pl.BlockSpec((CHUNK, 256), lambda n: (n, OFF_KV // 256))
    kv_prev = pl.BlockSpec((CHUNK, 256), lambda n: (jnp.maximum(n - 1, 0), OFF_KV // 256))
    zsp = [pl.BlockSpec((CHUNK, 256), functools.partial(lambda n, q: (n, OFF_ZC // 256 + q), q=q)) for q in range(4)]
    tab_cur = pl.BlockSpec((CHUNK, 128), lambda n: (n, 0))
    tab_prev = pl.BlockSpec((CHUNK, 128), lambda n: (jnp.maximum(n - 1, 0), 0))
    p2sp = pl.BlockSpec((128, 128), lambda n: (0, 0))
    sink = pl.BlockSpec(memory_space=pltpu.SMEM)
    wide = pl.BlockSpec((CHUNK, 1024), lambda n: (n, 0))
    return qsp, kv_cur, kv_prev, zsp, tab_cur, tab_prev, p2sp, sink, wide


def _attn_core(n, q_ref, kvc_ref, kvp_ref, cosc_ref, sinc_ref, cosp_ref, sinp_ref, p2_ref, sink_ref):
    p2 = p2_ref[...]
    qr = _rope(q_ref[...].astype(F32), cosc_ref[...], sinc_ref[...], p2).astype(BF16)
    kc = _rope(kvc_ref[:, 0:128].astype(F32), cosc_ref[...], sinc_ref[...], p2)
    kp = _rope(kvp_ref[:, 0:128].astype(F32), cosp_ref[...], sinp_ref[...], p2)
    k_all = jnp.concatenate([kp, kc], axis=0).astype(BF16)
    v_all = jnp.concatenate([kvp_ref[:, 128:256], kvc_ref[:, 128:256]], axis=0)
    kd = _dup_halves(k_all)
    vd = _dup_halves(v_all)
    qi, kj = _attn_mask()
    allowed = ((kj < CHUNK) & (kj > qi) & (n > 0)) | ((kj >= CHUNK) & (kj - CHUNK <= qi))
    lo = _lane_lo((CHUNK, 128))
    probs = []
    for h in range(ATT_HEADS):
        m, half, g = h // 2, h % 2, h // 8
        qp = qr[:, m * 128:(m + 1) * 128]
        qm = jnp.where(lo if half == 0 else ~lo, qp, jnp.zeros_like(qp))
        s = jnp.where(allowed, _dot_nt(qm, kd[g]) * (HEAD_DIM ** -0.5), NEG_INF)
        snk = sink_ref[h]
        mx = jnp.maximum(jnp.max(s, axis=-1, keepdims=True), snk)
        e = jnp.exp(s - mx)
        es = jnp.exp(snk - mx)
        inv = 1.0 / (jnp.sum(e, axis=-1, keepdims=True) + es)
        probs.append((qm, e * inv, es * inv))
    return qr, kd, vd, probs, lo


def _attn_out(vd, probs, lo):
    outs = []
    for m in range(ATT_HEADS // 2):
        g = m // 4
        o0 = _dot(probs[2 * m][1].astype(BF16), vd[g])
        o1 = _dot(probs[2 * m + 1][1].astype(BF16), vd[g])
        outs.append(jnp.where(lo, o0, o1))
    return jnp.concatenate(outs, axis=-1)


def _silu_gate(o, z):
    return o * jax.nn.silu(z)


def _attn_fwd(proj, sinks, tabs, name):
    L = proj.shape[0]
    cos2, sin2, p2 = tabs
    qsp, kv_cur, kv_prev, zsp, tab_cur, tab_prev, p2sp, sink, wide = _attn_specs()

    def body(q_ref, kvc_ref, kvp_ref, z0, z1, z2, z3, cosc, sinc, cosp, sinp, p2_ref, sink_ref, o_ref):
        n = pl.program_id(0)
        _, _, vd, probs, lo = _attn_core(n, q_ref, kvc_ref, kvp_ref, cosc, sinc, cosp, sinp, p2_ref, sink_ref)
        o = _attn_out(vd, probs, lo)
        z = jnp.concatenate([z0[...], z1[...], z2[...], z3[...]], axis=-1).astype(F32)
        o_ref[...] = _silu_gate(o, z).astype(BF16)

    return pl.pallas_call(
        body, name=name, grid=(L // CHUNK,),
        in_specs=[qsp, kv_cur, kv_prev, *zsp, tab_cur, tab_cur, tab_prev, tab_prev, p2sp, sink],
        out_specs=wide, out_shape=jax.ShapeDtypeStruct((L, 1024), BF16), compiler_params=_cp(("parallel",)),
    )(proj, proj, proj, proj, proj, proj, proj, cos2, sin2, cos2, sin2, p2, sinks)


def _attn_bwd(proj, sinks, tabs, dyc, name):
    L = proj.shape[0]
    cos2, sin2, p2 = tabs
    qsp, kv_cur, kv_prev, zsp, tab_cur, tab_prev, p2sp, sink, wide = _attn_specs()
    kvo = pl.BlockSpec((CHUNK, 256), lambda n: (n, 0))

    def body(q_ref, kvc_ref, kvp_ref, z0, z1, z2, z3, cosc, sinc, cosp, sinp, p2_ref, sink_ref, g_ref,
             dq_ref, dz_ref, dkvc_ref, dkvp_ref, dsink_ref):
        n = pl.program_id(0)
        _, kd, vd, probs, lo = _attn_core(n, q_ref, kvc_ref, kvp_ref, cosc, sinc, cosp, sinp, p2_ref, sink_ref)
        o = _attn_out(vd, probs, lo)
        z = jnp.concatenate([z0[...], z1[...], z2[...], z3[...]], axis=-1).astype(F32)
        _, vjp = jax.vjp(_silu_gate, o, z)
        do, dz = vjp(g_ref[...].astype(F32))
        dz_ref[...] = dz.astype(BF16)

        @pl.when(n == 0)
        def _():
            dsink_ref[...] = jnp.zeros_like(dsink_ref)

        dkd = [jnp.zeros((2 * CHUNK, 128), F32), jnp.zeros((2 * CHUNK, 128), F32)]
        dvd = [jnp.zeros((2 * CHUNK, 128), F32), jnp.zeros((2 * CHUNK, 128), F32)]
        dq_pairs = []
        for m in range(ATT_HEADS // 2):
            g = m // 4
            dop = do[:, m * 128:(m + 1) * 128].astype(BF16)
            dq_h = []
            for half in range(2):
                h = 2 * m + half
                qm, p, ps = probs[h]
                dom = jnp.where(lo if half == 0 else ~lo, dop, jnp.zeros_like(dop))
                dp = _dot_nt(dom, vd[g])
                rs = jnp.sum(p * dp, axis=-1, keepdims=True)
                ds = (p * (dp - rs) * (HEAD_DIM ** -0.5)).astype(BF16)
                dsink_ref[h:h + 1, :] += jnp.broadcast_to(jnp.sum(-ps * rs, axis=0, keepdims=True), (1, 128))
                dq_h.append(_dot(ds, kd[g]))
                dkd[g] = dkd[g] + _dot_tn(ds, qm)
                dvd[g] = dvd[g] + _dot_tn(p.astype(BF16), dom)
            dq_pairs.append(jnp.where(lo, dq_h[0], dq_h[1]))
        p2 = p2_ref[...]
        dq_ref[...] = _rope_t(jnp.concatenate(dq_pairs, axis=-1), cosc[...], sinc[...], p2).astype(BF16)
        dk_rot = _fold_halves(dkd[0], dkd[1])
        dv = _fold_halves(dvd[0], dvd[1])
        dkp = _rope_t(dk_rot[0:CHUNK], cosp[...], sinp[...], p2)
        dkc = _rope_t(dk_rot[CHUNK:2 * CHUNK], cosc[...], sinc[...], p2)
        dkvp_ref[...] = jnp.concatenate([dkp, dv[0:CHUNK]], axis=-1)
        dkvc_ref[...] = jnp.concatenate([dkc, dv[CHUNK:2 * CHUNK]], axis=-1)

    act = jax.ShapeDtypeStruct((L, 1024), BF16)
    kvs = jax.ShapeDtypeStruct((L, 256), F32)
    return pl.pallas_call(
        body, name=name, grid=(L // CHUNK,),
        in_specs=[qsp, kv_cur, kv_prev, *zsp, tab_cur, tab_cur, tab_prev, tab_prev, p2sp, sink, wide],
        out_specs=[wide, wide, kvo, kvo, pl.BlockSpec((ATT_HEADS, 128), lambda n: (0, 0))],
        out_shape=[act, act, kvs, kvs, jax.ShapeDtypeStruct((ATT_HEADS, 128), F32)],
        compiler_params=_cp(("arbitrary",)),
    )(proj, proj, proj, proj, proj, proj, proj, cos2, sin2, cos2, sin2, p2, sinks, dyc)


MERGE_TN = 256


def _merge_point(ta, tb, tc, ga, gb, gc):
    return jax.nn.sigmoid(ga) * ta + jax.nn.sigmoid(gb) * tb + jax.nn.sigmoid(gc) * tc


def _merge_specs(tm):
    nj = D_MODEL // MERGE_TN
    t = pl.BlockSpec((tm, MERGE_TN), lambda i, j: (i, j))
    gates = [pl.BlockSpec((tm, MERGE_TN), functools.partial(lambda i, j, b: (i, OFF_G // MERGE_TN + b * nj + j), b=b))
             for b in range(3)]
    return t, gates, nj


def _merge_fwd(ta, tb, tc, proj, name):
    L = ta.shape[0]
    tm = min(L, 1024)
    t, gates, nj = _merge_specs(tm)

    def body(ta_ref, tb_ref, tc_ref, ga_ref, gb_ref, gc_ref, o_ref):
        f = lambda r: r[...].astype(F32)
        o_ref[...] = _merge_point(f(ta_ref), f(tb_ref), f(tc_ref), f(ga_ref), f(gb_ref), f(gc_ref)).astype(BF16)

    return pl.pallas_call(
        body, name=name, grid=(L // tm, nj), in_specs=[t, t, t, *gates], out_specs=t,
        out_shape=jax.ShapeDtypeStruct((L, D_MODEL), BF16), compiler_params=_cp(("parallel", "parallel")),
    )(ta, tb, tc, proj, proj, proj)


def _merge_bwd(ta, tb, tc, proj, dm, name):
    L = ta.shape[0]
    tm = min(L, 1024)
    t, gates, nj = _merge_specs(tm)

    def body(ta_ref, tb_ref, tc_ref, ga_ref, gb_ref, gc_ref, dm_ref, dta_ref, dtb_ref, dtc_ref, dga_ref, dgb_ref, dgc_ref):
        f = lambda r: r[...].astype(F32)
        _, vjp = jax.vjp(_merge_point, f(ta_ref), f(tb_ref), f(tc_ref), f(ga_ref), f(gb_ref), f(gc_ref))
        outs = vjp(f(dm_ref))
        for r, v in zip((dta_ref, dtb_ref, dtc_ref, dga_ref, dgb_ref, dgc_ref), outs):
            r[...] = v.astype(BF16)

    act = jax.ShapeDtypeStruct((L, D_MODEL), BF16)
    return pl.pallas_call(
        body, name=name, grid=(L // tm, nj), in_specs=[t, t, t, *gates, t],
        out_specs=[t] * 6, out_shape=[act] * 6,
        compiler_params=_cp(("parallel", "parallel")),
    )(ta, tb, tc, proj, proj, proj, dm)


GRAD_DT = BF16
SMALL = ("norm_w", "ssm_a_re", "ssm_a_im", "ssm_log_dt", "ssm_b_re", "ssm_b_im", "ssm_c_re", "ssm_c_im", "ssm_d",
         "ssm_glu_b", "sg_ln_w", "sg_ln_b", "sg_w", "sg_b", "attn_sinks")
G8 = SSM_GROUPS // N_SLAB


def _diag_mask(rows_per_group, cols_per_group):
    r = jnp.arange(G8 * rows_per_group)[:, None] // rows_per_group
    c = jnp.arange(G8 * cols_per_group)[None, :] // cols_per_group
    return r == c


def _slab_b(bb_t):
    x = bb_t.transpose(1, 0, 2).reshape(N_SLAB, SLAB_CH, SSM_STATE)
    return jnp.where(_diag_mask(SSM_GROUP, SSM_STATE), jnp.tile(x, (1, 1, G8)), 0)


def _unslab_b(d):
    x = jnp.where(_diag_mask(SSM_GROUP, SSM_STATE), d, 0).reshape(N_SLAB, SLAB_CH, G8, SSM_STATE).sum(axis=2)
    return x.reshape(SSM_GROUPS, SSM_GROUP, SSM_STATE).transpose(1, 0, 2)


def _slab_c(c):
    x = c.transpose(0, 2, 1).reshape(N_SLAB, SLAB_ST, SSM_GROUP)
    return jnp.where(_diag_mask(SSM_STATE, SSM_GROUP), jnp.tile(x, (1, 1, G8)), 0)


def _unslab_c(d):
    x = jnp.where(_diag_mask(SSM_STATE, SSM_GROUP), d, 0).reshape(N_SLAB, SLAB_ST, G8, SSM_GROUP).sum(axis=2)
    return x.reshape(SSM_GROUPS, SSM_STATE, SSM_GROUP).transpose(0, 2, 1)


def _s5_prep(p, tag):
    bt_re = p["ssm_b_re"].transpose(2, 0, 1)
    bt_im = p["ssm_b_im"].transpose(2, 0, 1)
    raw = (p["ssm_a_re"], p["ssm_a_im"], p["ssm_log_dt"][:, None], bt_re, bt_im)
    lr, li, bbr, bbi = _s5_params_fwd(*raw, name=f"s5_params_{tag}")
    ops = (_slab_b(bbr).astype(BF16), _slab_b(bbi).astype(BF16),
           _slab_c(p["ssm_c_re"]).astype(BF16), _slab_c(p["ssm_c_im"]).astype(BF16),
           lr.reshape(N_SLAB, 1, SLAB_ST), li.reshape(N_SLAB, 1, SLAB_ST), p["ssm_d"].reshape(N_SLAB, 1, SLAB_CH))
    return raw, ops


def _layer_fwd(x, p, w, tabs, tag, s5=None, win_of=None, after_proj=None):
    L = x.shape[0]
    h = _rms_fwd(x, p["norm_w"][None], f"rms_fwd_{tag}")
    win_t, proj_after = win_of(h) if win_of is not None else (w["win_t"], None)
    proj = _mm(h, win_t, "nt", BF16, L, 256, D_MODEL, f"in_proj_{tag}", after=proj_after)
    if after_proj is not None:
        w = after_proj(proj)
    s5_raw, s5_ops = s5 if s5 is not None else _s5_prep(p, tag)
    ya0 = _s5_fwd(proj, *s5_ops, name=f"s5_fwd_{tag}")
    ya = _glu_fwd(ya0, proj, w["glu"], p["ssm_glu_b"][None], f"glu_fwd_{tag}")
    yb = _sg_fwd(proj, p["sg_ln_w"][None], p["sg_ln_b"][None], p["sg_w"], p["sg_b"][:, :, None], f"sg_fwd_{tag}")
    yc = _attn_fwd(proj, p["attn_sinks"], tabs, f"attn_fwd_{tag}")
    ta = _mm(ya, w["wba_t"], "nt", BF16, 1024, 1024, 1024, f"branch_a_{tag}")
    tb = _mm(yb, w["wbb_t"], "nt", BF16, 1024, 1024, 1024, f"branch_b_{tag}")
    tc = _mm(yc, w["wbc_t"], "nt", BF16, 1024, 1024, 1024, f"branch_c_{tag}")
    merged = _merge_fwd(ta, tb, tc, proj, f"merge_fwd_{tag}")
    x_new = _mm(merged, w["wout"], "nn", F32, 1024, 512, D_MODEL, f"out_proj_{tag}", res=x)
    saved = dict(x=x, h=h, proj=proj, s5_raw=s5_raw, s5_ops=s5_ops, ya0=ya0, ya=ya, yb=yb, yc=yc,
                 ta=ta, tb=tb, tc=tc, merged=merged)
    return x_new, saved


def _layer_bwd(dx_out, p, w, tabs, s, tag, first_after=None, before_win=None, after_win=None):
    L = dx_out.shape[0]
    proj = s["proj"]
    big, small = {}, {}
    dmerged = _mm(dx_out, w["wout"], "nt", BF16, 1024, 512, D_MODEL, f"d_merged_{tag}", after=first_after)
    big["wout"] = _mm(s["merged"], dx_out, "tn", GRAD_DT, 512, 1024, L, f"d_wout_{tag}")
    dta, dtb, dtc, dga, dgb, dgc = _merge_bwd(s["ta"], s["tb"], s["tc"], proj, dmerged, f"merge_bwd_{tag}")
    dy = {}
    for br, dt in (("a", dta), ("b", dtb), ("c", dtc)):
        dy[br] = _mm(dt, w[f"wb{br}_t"], "nn", BF16, 1024, 1024, D_MODEL, f"d_y{br}_{tag}")
        big[f"wb{br}_t"] = _mm(dt, s[f"y{br}"], "tn", GRAD_DT, 512, 1024, L, f"d_wb{br}_{tag}")

    dq, dzc, dkvc, dkvp, dsink = _attn_bwd(proj, p["attn_sinks"], tabs, dy["c"], f"attn_bwd_{tag}")
    dkv = dkvc + jnp.concatenate([dkvp[CHUNK:], jnp.zeros((CHUNK, 256), F32)], axis=0)
    small["attn_sinks"] = dsink[:, 0]

    dub, dvb, dzb, dlw, dlb, dsgw, dsgb = _sg_bwd(
        proj, p["sg_ln_w"][None], p["sg_ln_b"][None], p["sg_w"], p["sg_b"][:, :, None], dy["b"], f"sg_bwd_{tag}")
    small.update(sg_ln_w=dlw[0], sg_ln_b=dlb[0], sg_w=dsgw, sg_b=dsgb[:, :, 0])

    dya0, dza, dglu, dglub = _glu_bwd(s["ya0"], proj, w["glu"], p["ssm_glu_b"][None], dy["a"], f"glu_bwd_{tag}")
    big["glu"] = dglu.astype(GRAD_DT)
    small["ssm_glu_b"] = dglub[0]

    dua, dbre, dbim, dcre, dcim, dlr, dli, dd = _s5_bwd(proj, dya0, *s["s5_ops"], name=f"s5_bwd_{tag}")
    da_re, da_im, dlog_dt, dbt_re, dbt_im = _s5_params_bwd(
        *s["s5_raw"], dlr.reshape(SSM_GROUPS, SSM_STATE), dli.reshape(SSM_GROUPS, SSM_STATE),
        _unslab_b(dbre), _unslab_b(dbim), name=f"s5_params_bwd_{tag}")
    small.update(ssm_a_re=da_re, ssm_a_im=da_im, ssm_log_dt=dlog_dt[:, 0],
                 ssm_bt_re=dbt_re, ssm_bt_im=dbt_im,
                 ssm_c_re=_unslab_c(dcre), ssm_c_im=_unslab_c(dcim), ssm_d=dd.reshape(SSM_WIDTH))

    dproj = jnp.concatenate([dua, dza, dub, dvb, dzb, dq, dkv.astype(BF16), dzc, dga, dgb, dgc], axis=-1)
    tok = before_win(big) if before_win is not None else None
    big["win_t"] = _mm(dproj, s["h"], "tn", GRAD_DT, 256, D_MODEL, L, f"d_win_{tag}", after=tok)
    tok = after_win(big) if after_win is not None else None
    dh = _mm(dproj, w["win_t"], "nn", F32, 1024, D_MODEL, 256, f"d_h_{tag}", after=tok)
    dx_in, dnw = _rms_bwd(s["x"], p["norm_w"][None], dh, dx_out, f"rms_bwd_{tag}")
    small["norm_w"] = dnw[0]
    return dx_in, big, small


def _local_step(x, tgt, small_p, final_w, big_w):
    L = x.shape[0]
    tabs = _rope_tables(L)
    saved = []
    for l in range(DEPTH):
        x, s = _layer_fwd(x, small_p[l], big_w[l], tabs, f"l{l}")
        saved.append(s)
    loss_acc, dx, dfw = _final(x, final_w[None], tgt, "final_norm_loss")
    big_g, small_g = [None] * DEPTH, [None] * DEPTH
    for l in reversed(range(DEPTH)):
        dx, big_g[l], small_g[l] = _layer_bwd(dx, small_p[l], big_w[l], tabs, saved[l], f"l{l}")
    return loss_acc[0, 0], dx, dfw[0], big_g, small_g


MESH = pl.DeviceIdType.MESH
ANY = pl.BlockSpec(memory_space=pl.ANY)
ROW_ALIGN = 16


def _place():
    return lax.axis_index("x"), lax.axis_index("y"), lax.axis_index("c")


HBM = pl.BlockSpec(memory_space=pltpu.HBM)
SEM = pl.BlockSpec(memory_space=pltpu.SEMAPHORE)
EFFECT = pltpu.SideEffectType.DATAFLOW_SIDE_EFFECTING


def _split_start(srcs, lands, n_copies, copies, name, after=None):
    n, m, k = len(srcs), len(lands), n_copies
    extra = [] if after is None else [after]

    def body(*refs):
        src_refs, land_refs = refs[:n], refs[n:n + m]
        sems = refs[n + m + len(extra):]
        send_sems, recv_sems, token = sems[:k], sems[k:2 * k], refs[-1]
        for cp in copies(src_refs, land_refs, send_sems, recv_sems):
            cp.start()
        token[...] = jnp.zeros_like(token)

    ops = list(srcs) + list(lands)
    outs = pl.pallas_call(
        body, name=name,
        out_shape=(*[pltpu.SemaphoreType.DMA(())] * (2 * k),
                   *[pltpu.HBM(a.shape, a.dtype) for a in ops], jax.ShapeDtypeStruct((8, 128), F32)),
        in_specs=[HBM] * (n + m) + [ANY] * len(extra),
        out_specs=(*[SEM] * (2 * k), *[HBM] * (n + m), pl.BlockSpec(memory_space=pltpu.VMEM)),
        input_output_aliases={i: 2 * k + i for i in range(n + m)},
        compiler_params=pltpu.CompilerParams(has_side_effects=EFFECT),
    )(*[pltpu.with_memory_space_constraint(a, pltpu.HBM) for a in ops], *extra)
    return (list(outs[:k]), list(outs[k:2 * k]), list(outs[2 * k:2 * k + n]), list(outs[2 * k + n:2 * k + n + m]),
            outs[-1])


def _split_wait(send_sems, recv_sems, srcs, lands, after, copies, name):
    n, m, k = len(srcs), len(lands), len(send_sems)
    after = list(after) if isinstance(after, (list, tuple)) else [after]

    def body(*refs):
        src_refs, land_refs = refs[:n], refs[n:n + m]
        for cp in copies(src_refs, land_refs, refs[n + m:n + m + k], refs[n + m + k:n + m + 2 * k]):
            cp.wait_send()
            cp.wait_recv()

    ops = list(srcs) + list(lands)
    outs = pl.pallas_call(
        body, name=name,
        out_shape=tuple(pltpu.HBM(a.shape, a.dtype) for a in ops),
        in_specs=[HBM] * (n + m) + [SEM] * (2 * k) + [ANY] * len(after),
        out_specs=tuple([HBM] * (n + m)),
        input_output_aliases={i: i for i in range(n + m)},
        compiler_params=pltpu.CompilerParams(has_side_effects=EFFECT),
    )(*ops, *send_sems, *recv_sems, *after)
    return list(outs[:n]), list(outs[n:])


def _ag_rows(land_ref, px, py, pc):
    r = land_ref.shape[0] // N_DEV
    start = pl.multiple_of((4 * px + 2 * py + pc) * r, ROW_ALIGN)
    return land_ref.at[pl.ds(start, r), :]


def _ag_copies(src_refs, land_refs, send_sems, recv_sems):
    x, y, c = _place()
    peers = [(x, y, 1 - c), (1 - x, y, c), (x, 1 - y, c), (1 - x, 1 - y, c)]
    return [pltpu.make_async_remote_copy(
        src_ref=_ag_rows(land_refs[a], x, y, c), dst_ref=_ag_rows(land_refs[a], x, y, c),
        send_sem=send_sems[4 * a + k], recv_sem=recv_sems[4 * a + k], device_id=peer, device_id_type=MESH)
        for a in range(len(land_refs)) for k, peer in enumerate(peers)]


def _ag_forward(lands, name):
    n = len(lands)

    def body(*refs):
        land_refs = refs[n:2 * n]
        send_sems, recv_sems = refs[2 * n:]
        x, y, c = _place()
        chips = [(1 - x, y), (x, 1 - y), (1 - x, 1 - y)]

        def copy(a, j, pc):
            px, py = chips[j]
            return pltpu.make_async_remote_copy(
                src_ref=_ag_rows(land_refs[a], px, py, pc), dst_ref=_ag_rows(land_refs[a], px, py, pc),
                send_sem=send_sems.at[a, j], recv_sem=recv_sems.at[a, j], device_id=(x, y, 1 - c), device_id_type=MESH)

        passed = [copy(a, j, c) for a in range(n) for j in range(3)]
        for cp in passed:
            cp.start()
        for a in range(n):
            for j in range(3):
                copy(a, j, 1 - c).wait_recv()
        for cp in passed:
            cp.wait_send()

    return pl.pallas_call(
        body, name=name,
        in_specs=[ANY] * n, out_specs=[ANY] * n,
        out_shape=[jax.ShapeDtypeStruct(l.shape, l.dtype) for l in lands],
        input_output_aliases={i: i for i in range(n)},
        scratch_shapes=[pltpu.SemaphoreType.DMA((n, 3)), pltpu.SemaphoreType.DMA((n, 3))],
    )(*lands)


def _allgather_place(shards):
    x, y, c = _place()
    return [lax.dynamic_update_slice(lax.empty((N_DEV * s.shape[0], s.shape[1]), s.dtype), s,
                                     ((4 * x + 2 * y + c) * s.shape[0], 0)) for s in shards]


def _allgather_start(lands, name, after=None):
    return _split_start([], lands, 4 * len(lands), _ag_copies, name + "_start", after=after)


def _allgather_finish(started, after, name):
    send_sems, recv_sems, _, lands, _ = started
    _, lands = _split_wait(send_sems, recv_sems, [], lands, after, _ag_copies, name + "_wait")
    return list(_ag_forward(lands, name + "_forward"))


def _rs_swap_cores(grads, name):
    n = len(grads)

    def body(*refs):
        ins, outs = refs[:n], refs[n:2 * n]
        send_sems, recv_sems = refs[2 * n:]
        x, y, c = _place()
        cps = []
        for a in range(n):
            r = ins[a].shape[0] // N_DEV
            for q in range(4):
                start = pl.multiple_of((2 * q + 1 - c) * r, ROW_ALIGN)
                cps.append(pltpu.make_async_remote_copy(
                    src_ref=ins[a].at[pl.ds(start, r), :], dst_ref=outs[a].at[q],
                    send_sem=send_sems.at[a, q], recv_sem=recv_sems.at[a, q],
                    device_id=(x, y, 1 - c), device_id_type=MESH))
        for cp in cps:
            cp.start()
        for cp in cps:
            cp.wait()

    return pl.pallas_call(
        body, name=name, in_specs=[ANY] * n, out_specs=[ANY] * n,
        out_shape=[jax.ShapeDtypeStruct((4, g.shape[0] // N_DEV, g.shape[1]), g.dtype) for g in grads],
        scratch_shapes=[pltpu.SemaphoreType.DMA((n, 4)), pltpu.SemaphoreType.DMA((n, 4))],
    )(*grads)


def _rs_chip_copies(sum_refs, land_refs, send_sems, recv_sems):
    x, y, c = _place()
    chips = [(1 - x, y), (x, 1 - y), (1 - x, 1 - y)]
    return [pltpu.make_async_remote_copy(
        src_ref=sum_refs[a].at[2 * px + py], dst_ref=land_refs[a].at[2 * x + y],
        send_sem=send_sems[3 * a + j], recv_sem=recv_sems[3 * a + j], device_id=(px, py, c), device_id_type=MESH)
        for a in range(len(sum_refs)) for j, (px, py) in enumerate(chips)]


def _row_tile(r):
    return max(t for t in range(ROW_ALIGN, min(r, 1024) + 1, ROW_ALIGN) if r % t == 0)


def _rs_add_cores(grad, recv, cidx, name):
    r, cols = recv.shape[1], recv.shape[2]
    tr = _row_tile(r)
    nb = r // tr

    def body(c_ref, g_ref, r_ref, o_ref):
        o_ref[...] = (g_ref[...].astype(F32) + r_ref[...].astype(F32)).astype(o_ref.dtype)

    return pl.pallas_call(
        body, name=name,
        grid_spec=pltpu.PrefetchScalarGridSpec(
            num_scalar_prefetch=1, grid=(4, nb),
            in_specs=[pl.BlockSpec((tr, cols), lambda q, i, c_ref: ((2 * q + c_ref[0]) * nb + i, 0)),
                      pl.BlockSpec((None, tr, cols), lambda q, i, c_ref: (q, i, 0))],
            out_specs=pl.BlockSpec((None, tr, cols), lambda q, i, c_ref: (q, i, 0))),
        out_shape=jax.ShapeDtypeStruct(recv.shape, recv.dtype),
        compiler_params=_cp(("parallel", "parallel")),
    )(cidx, grad, recv)


def _rs_add_chips(own, recv, slots, name):
    r, cols = recv.shape[1], recv.shape[2]
    tr = _row_tile(r)

    def body(s_ref, o_ref, r0_ref, r1_ref, r2_ref, out_ref):
        acc = o_ref[...].astype(F32)
        for ref in (r0_ref, r1_ref, r2_ref):
            acc = acc + ref[...].astype(F32)
        out_ref[...] = acc

    pick = lambda k: pl.BlockSpec((None, tr, cols), functools.partial(lambda i, s_ref, k: (s_ref[k], i, 0), k=k))
    return pl.pallas_call(
        body, name=name,
        grid_spec=pltpu.PrefetchScalarGridSpec(
            num_scalar_prefetch=1, grid=(r // tr,),
            in_specs=[pick(0), pick(1), pick(2), pick(3)],
            out_specs=pl.BlockSpec((tr, cols), lambda i, s_ref: (i, 0))),
        out_shape=jax.ShapeDtypeStruct((r, cols), F32),
        compiler_params=_cp(("parallel",)),
    )(slots, own, recv, recv, recv)


def _reduce_scatter_start(grads, tag):
    cidx = lax.axis_index("c").astype(jnp.int32)[None]
    recv = _rs_swap_cores(grads, f"rs_swap_cores_{tag}")
    sums = [_rs_add_cores(g, rv, cidx, f"rs_add_cores_{tag}_{i}") for i, (g, rv) in enumerate(zip(grads, recv))]
    lands = [lax.empty(s.shape, s.dtype) for s in sums]
    return _split_start(sums, lands, 3 * len(sums), _rs_chip_copies, f"rs_chips_{tag}_start")


def _reduce_scatter_finish(started, after, tag):
    send_sems, recv_sems, sums, lands, _ = started
    sums, lands = _split_wait(send_sems, recv_sems, sums, lands, after, _rs_chip_copies, f"rs_chips_{tag}_wait")
    x, y = lax.axis_index("x"), lax.axis_index("y")
    slots = jnp.stack([2 * x + y, 2 * (1 - x) + y, 2 * x + 1 - y, 2 * (1 - x) + 1 - y]).astype(jnp.int32)
    return [_rs_add_chips(s, l, slots, f"rs_add_chips_{tag}_{i}") for i, (s, l) in enumerate(zip(sums, lands))]


def _allreduce_small(packs, name):
    n = len(packs)
    assert all(p.shape[0] % (8 * N_DEV) == 0 for p in packs)

    def body(*refs):
        p_refs, o_refs, part_refs = refs[:n], refs[n:2 * n], refs[2 * n:3 * n]
        send1, recv1, send2, recv2 = refs[3 * n:]
        x, y, c = _place()
        me = 4 * x + 2 * y + c

        def block(ref, d):
            rs = ref.shape[0] // N_DEV
            return ref.at[pl.ds(pl.multiple_of(d * rs, 8), rs), :]

        peers = [(1 - x if k & 4 else x, 1 - y if k & 2 else y, 1 - c if k & 1 else c) for k in range(1, N_DEV)]
        scatter = [pltpu.make_async_remote_copy(
            src_ref=block(p_refs[a], 4 * px + 2 * py + pc), dst_ref=part_refs[a].at[me],
            send_sem=send1.at[a, k], recv_sem=recv1.at[a, k], device_id=(px, py, pc), device_id_type=MESH)
            for a in range(n) for k, (px, py, pc) in enumerate(peers)]
        for cp in scatter:
            cp.start()
        for a in range(n):
            part_refs[a][me] = block(p_refs[a], me)[...]
        for cp in scatter:
            cp.wait()
        for a in range(n):
            acc = part_refs[a][0]
            for d in range(1, N_DEV):
                acc = acc + part_refs[a][d]
            block(o_refs[a], me)[...] = acc
        gather = [pltpu.make_async_remote_copy(
            src_ref=block(o_refs[a], me), dst_ref=block(o_refs[a], me), send_sem=send2.at[a, k], recv_sem=recv2.at[a, k],
            device_id=peer, device_id_type=MESH) for a in range(n) for k, peer in enumerate(peers)]
        for cp in gather:
            cp.start()
        for a in range(n):
            for k, (px, py, pc) in enumerate(peers):
                theirs = block(o_refs[a], 4 * px + 2 * py + pc)
                pltpu.make_async_remote_copy(
                    src_ref=theirs, dst_ref=theirs, send_sem=send2.at[a, k], recv_sem=recv2.at[a, k],
                    device_id=(px, py, pc), device_id_type=MESH).wait_recv()
        for cp in gather:
            cp.wait_send()

    sems = pltpu.SemaphoreType.DMA((n, N_DEV - 1))
    vmem = pl.BlockSpec(memory_space=pltpu.VMEM)
    return pl.pallas_call(
        body, name=name,
        in_specs=[vmem] * n, out_specs=[vmem] * n,
        out_shape=[jax.ShapeDtypeStruct(p.shape, F32) for p in packs],
        scratch_shapes=[pltpu.VMEM((N_DEV, p.shape[0] // N_DEV, p.shape[1]), F32) for p in packs] + [sems] * 4,
        compiler_params=pltpu.CompilerParams(vmem_limit_bytes=VMEM_LIMIT),
    )(*packs)


ADAM_TILE_BYTES = 2 * 1024 * 1024


def _adamw(w, g, m, v, name):
    shape = w.shape
    rows, cols = shape[-2:]
    lead = shape[:-2]
    nl = math.prod(lead)
    tr = max(t for t in range(8, rows + 1, 8) if rows % t == 0 and t * max(cols, 128) * 4 <= ADAM_TILE_BYTES) \
        if rows % 8 == 0 else rows
    c1 = 1.0 - ADAM_B1 ** ADAM_STEP
    c2 = 1.0 - ADAM_B2 ** ADAM_STEP

    def body(w_ref, g_ref, m_ref, v_ref, d_ref, nm_ref, nv_ref):
        gv = g_ref[...]
        nm = ADAM_B1 * m_ref[...] + (1.0 - ADAM_B1) * gv
        nv = ADAM_B2 * v_ref[...] + (1.0 - ADAM_B2) * jnp.square(gv)
        d_ref[...] = -ADAM_LR * ((nm / c1) / (jnp.sqrt(nv / c2) + ADAM_EPS) + ADAM_WD * w_ref[...])
        nm_ref[...] = nm
        nv_ref[...] = nv

    def index(b, i):
        return (*jnp.unravel_index(b, lead), i, 0) if lead else (i, 0)

    blk = pl.BlockSpec((*[None] * len(lead), tr, cols), index)
    sh = jax.ShapeDtypeStruct(shape, F32)
    return pl.pallas_call(
        body, name=name, grid=(nl, rows // tr), in_specs=[blk] * 4, out_specs=[blk] * 3, out_shape=[sh] * 3,
        compiler_params=_cp(("parallel", "parallel")),
    )(w, g, m, v)


WEIGHTS = ("norm_w", "w_in", "ssm_a_re", "ssm_a_im", "ssm_log_dt", "ssm_b_re", "ssm_b_im", "ssm_c_re", "ssm_c_im",
           "ssm_d", "ssm_glu_w", "ssm_glu_b", "sg_ln_w", "sg_ln_b", "sg_w", "sg_b", "attn_sinks",
           "w_branch_a", "w_branch_b", "w_branch_c", "w_out", "final_norm_w")
BIG = ("w_in", "ssm_glu_w", "w_branch_a", "w_branch_b", "w_branch_c", "w_out")
BIG_KEY = {"w_in": ("win_t", True), "ssm_glu_w": ("glu", False), "w_branch_a": ("wba_t", True),
           "w_branch_b": ("wbb_t", True), "w_branch_c": ("wbc_t", True), "w_out": ("wout", False)}
PACK64 = ("ssm_a_re", "ssm_a_im", "ssm_c_re", "ssm_c_im")
PACK128 = ("sg_w",)
PACK1024 = ("norm_w", "ssm_d", "ssm_glu_b", "sg_ln_w", "sg_ln_b", "final_norm_w", "sg_b", "ssm_log_dt", "attn_sinks")
PACK_ROWS = 8 * N_DEV


def _pack_rows(a, cols):
    return -(-a.size // (8 * cols)) * 8


def _pack(arrs, cols):
    parts = []
    for a in arrs:
        if a.shape[-1] == cols and a.size % (8 * cols) == 0:
            parts.append(a.reshape(-1, cols))
            continue
        flat = a.reshape(-1)
        parts.append(jnp.pad(flat, (0, _pack_rows(a, cols) * cols - flat.shape[0])).reshape(-1, cols))
    rows = sum(p.shape[0] for p in parts)
    pad = -rows % PACK_ROWS
    if pad:
        parts.append(jnp.zeros((pad, cols), F32))
    return jnp.concatenate(parts, axis=0)


def _unpack(pack, like):
    cols = pack.shape[1]
    out, row = [], 0
    for a in like:
        nrow = _pack_rows(a, cols)
        rows = pack[row:row + nrow]
        whole = a.shape[-1] == cols and a.size == nrow * cols
        out.append(rows.reshape(a.shape) if whole else rows.reshape(-1)[:a.size].reshape(a.shape))
        row += nrow
    return out


def kernel(x, norm_w, w_in, ssm_a_re, ssm_a_im, ssm_log_dt, ssm_b_re, ssm_b_im, ssm_c_re, ssm_c_im, ssm_d, ssm_glu_w, ssm_glu_b, sg_ln_w, sg_ln_b, sg_w, sg_b, attn_sinks, w_branch_a, w_branch_b, w_branch_c, w_out, final_norm_w, loss_target, m_norm_w, m_w_in, m_ssm_a_re, m_ssm_a_im, m_ssm_log_dt, m_ssm_b_re, m_ssm_b_im, m_ssm_c_re, m_ssm_c_im, m_ssm_d, m_ssm_glu_w, m_ssm_glu_b, m_sg_ln_w, m_sg_ln_b, m_sg_w, m_sg_b, m_attn_sinks, m_w_branch_a, m_w_branch_b, m_w_branch_c, m_w_out, m_final_norm_w, v_norm_w, v_w_in, v_ssm_a_re, v_ssm_a_im, v_ssm_log_dt, v_ssm_b_re, v_ssm_b_im, v_ssm_c_re, v_ssm_c_im, v_ssm_d, v_ssm_glu_w, v_ssm_glu_b, v_sg_ln_w, v_sg_ln_b, v_sg_w, v_sg_b, v_attn_sinks, v_w_branch_a, v_w_branch_b, v_w_branch_c, v_w_out, v_final_norm_w):
    w = dict(zip(WEIGHTS, (norm_w, w_in, ssm_a_re, ssm_a_im, ssm_log_dt, ssm_b_re, ssm_b_im, ssm_c_re, ssm_c_im, ssm_d, ssm_glu_w, ssm_glu_b, sg_ln_w, sg_ln_b, sg_w, sg_b, attn_sinks, w_branch_a, w_branch_b, w_branch_c, w_out, final_norm_w)))
    m = dict(zip(WEIGHTS, (m_norm_w, m_w_in, m_ssm_a_re, m_ssm_a_im, m_ssm_log_dt, m_ssm_b_re, m_ssm_b_im, m_ssm_c_re, m_ssm_c_im, m_ssm_d, m_ssm_glu_w, m_ssm_glu_b, m_sg_ln_w, m_sg_ln_b, m_sg_w, m_sg_b, m_attn_sinks, m_w_branch_a, m_w_branch_b, m_w_branch_c, m_w_out, m_final_norm_w)))
    v = dict(zip(WEIGHTS, (v_norm_w, v_w_in, v_ssm_a_re, v_ssm_a_im, v_ssm_log_dt, v_ssm_b_re, v_ssm_b_im, v_ssm_c_re, v_ssm_c_im, v_ssm_d, v_ssm_glu_w, v_ssm_glu_b, v_sg_ln_w, v_sg_ln_b, v_sg_w, v_sg_b, v_attn_sinks, v_w_branch_a, v_w_branch_b, v_w_branch_c, v_w_out, v_final_norm_w)))

    keys = [BIG_KEY[n][0] for n in BIG]
    shards = [[(w[n][l].T if BIG_KEY[n][1] else w[n][l]).astype(BF16) for n in BIG] for l in range(DEPTH)]
    small_p = [{n: w[n][l] for n in SMALL} for l in range(DEPTH)]
    xv, tgt = x[0], loss_target[0]
    tabs = _rope_tables(xv.shape[0])

    lands = [[_allgather_place(shards[l][:1]), _allgather_place(shards[l][1:])] for l in range(DEPTH)]
    s5 = [_s5_prep(small_p[l], f"l{l}") for l in range(DEPTH)]
    wmv_packs = {cols: [_pack([d[n] for n in names], cols) for d in (w, m, v)]
                 for names, cols in ((PACK64, 64), (PACK128, 128), (PACK1024, 1024))}
    ag0a = _allgather_start(lands[0][0], "ag_l0_win")
    got = {}

    def win_of0(h):
        early = [h, *lands[0][1], *lands[1][0], *lands[1][1], *s5[0][1], *s5[1][1]]
        early += [p for ps in wmv_packs.values() for p in ps]
        got["win0"] = _allgather_finish(ag0a, early, "ag_l0_win")[0]
        got["ag0b"] = _allgather_start(lands[0][1], "ag_l0_rest", after=got["win0"])
        got["ag1a"] = _allgather_start(lands[1][0], "ag_l1_win", after=got["ag0b"][4])
        got["ag1b"] = _allgather_start(lands[1][1], "ag_l1_rest", after=got["ag1a"][4])
        return got["win0"], got["ag1b"][4]

    def after_proj0(proj):
        got["w0"] = dict(zip(keys, [got["win0"]] + _allgather_finish(got["ag0b"], proj, "ag_l0_rest")))
        return got["w0"]

    x1, saved0 = _layer_fwd(xv, small_p[0], None, tabs, "l0", s5=s5[0], win_of=win_of0, after_proj=after_proj0)
    big_w0 = got["w0"]
    win1 = _allgather_finish(got["ag1a"], x1, "ag_l1_win")[0]

    def after_proj1(proj):
        got["w1"] = dict(zip(keys, [win1] + _allgather_finish(got["ag1b"], proj, "ag_l1_rest")))
        return got["w1"]

    x2, saved1 = _layer_fwd(x1, small_p[1], {"win_t": win1}, tabs, "l1", s5=s5[1], after_proj=after_proj1)
    big_w1 = got["w1"]
    loss_acc, dx2, dfw = _final(x2, w["final_norm_w"][None], tgt, "final_norm_loss")
    loss = lax.psum(loss_acc[0, 0], ("x", "y", "c"))
    dfw = dfw[0]

    dx1, big_g1, small_g1 = _layer_bwd(dx2, small_p[1], big_w1, tabs, saved1, "l1")
    rs1 = _reduce_scatter_start([big_g1[k] for k in keys], "l1")

    def before_win0(big):
        got["rs0b"] = _reduce_scatter_start([big[k] for k in keys[1:]], "l0_rest")
        return got["rs0b"][4]

    def after_win0(big):
        got["rs0a"] = _reduce_scatter_start([big["win_t"]], "l0_win")
        return got["rs0a"][4]

    dx, big_g0, small_g0 = _layer_bwd(dx1, small_p[0], big_w0, tabs, saved0, "l0", first_after=rs1[4],
                                      before_win=before_win0, after_win=after_win0)
    red1 = _reduce_scatter_finish(rs1, dx, "l1")
    small_g = [small_g0, small_g1]

    grads, delta, new_m, new_v = {}, {}, {}, {}

    def stacked(n):
        return dfw if n == "final_norm_w" else jnp.stack([small_g[l][n] for l in range(DEPTH)])

    classes = ((PACK64, 64), (PACK128, 128), (PACK1024, 1024))
    g_lists = [[stacked(n) for n in names] for names, _ in classes]
    g_lists[0] += [stacked("ssm_bt_re"), stacked("ssm_bt_im")]
    reduced = _allreduce_small([_pack(gl, cols) for gl, (_, cols) in zip(g_lists, classes)], "allreduce_small")
    last = None
    for (names, cols), gl, red in zip(classes, g_lists, reduced):
        parts = _unpack(red, gl)
        grads.update(zip(names, parts))
        if cols == 64:
            grads["ssm_b_re"], grads["ssm_b_im"] = (t.transpose(0, 2, 3, 1) for t in parts[len(names):])
        wp, mp, vp = wmv_packs[cols]
        outs = _adamw(wp, red, mp, vp, f"adamw_pack{cols}")
        last = outs[0]
        for res, o in zip((delta, new_m, new_v), outs):
            res.update(zip(names, _unpack(o, [w[n] for n in names])))
    for n in ("ssm_b_re", "ssm_b_im"):
        rows16 = lambda a: a.reshape(-1, SSM_GROUP)
        outs = _adamw(rows16(w[n]), rows16(grads[n]), rows16(m[n]), rows16(v[n]), f"adamw_{n}")
        delta[n], new_m[n], new_v[n] = (o.reshape(w[n].shape) for o in outs)

    red0 = (_reduce_scatter_finish(got["rs0a"], last, "l0_win")
            + _reduce_scatter_finish(got["rs0b"], last, "l0_rest"))
    for i, n in enumerate(BIG):
        grads[n] = jnp.stack([g.T if BIG_KEY[n][1] else g for g in (red0[i], red1[i])])
    for n in BIG:
        delta[n], new_m[n], new_v[n] = _adamw(w[n], grads[n], m[n], v[n], f"adamw_{n}")

    return (loss, dx[None], *[grads[n] for n in WEIGHTS], *[delta[n] for n in WEIGHTS],
            *[new_m[n] for n in WEIGHTS], *[new_v[n] for n in WEIGHTS])
```

```python
import functools
import math

import jax
import jax.numpy as jnp
from jax import lax
from jax.experimental import pallas as pl
from jax.experimental.pallas import tpu as pltpu

F32 = jnp.float32
BF16 = jnp.bfloat16

D_MODEL = 2048
DEPTH = 2
EPS = 1e-6
NEG_INF = -1e30
N_DEV = 8

SSM_WIDTH = 1024
SSM_GROUP = 16
SSM_GROUPS = 64
SSM_STATE = 64
N_SLAB = 8
SLAB_CH = 128
SLAB_ST = 512
N_SEG = 8
SEG_PAD = 8

SG_HEADS = 8
CHUNK = 128
HEAD_DIM = 64
ATT_HEADS = 16
ROT_DIM = 16
ROPE_THETA = 500000.0

D_IN = 13568
OFF_UA, OFF_ZA, OFF_UB, OFF_VB, OFF_ZB, OFF_Q, OFF_KV, OFF_ZC, OFF_G = (
    0, 1024, 2048, 3072, 4096, 5120, 6144, 6400, 7424)

ADAM_LR, ADAM_B1, ADAM_B2, ADAM_EPS, ADAM_WD, ADAM_STEP = 0.001, 0.9, 0.999, 1e-08, 0.01, 10

VMEM_LIMIT = 56 * 1024 * 1024


def _cp(sem=None):
    return pltpu.CompilerParams(dimension_semantics=sem, vmem_limit_bytes=VMEM_LIMIT)


def _dot(a, b):
    return jnp.dot(a, b, preferred_element_type=F32)


def _dot_nt(a, b):
    return lax.dot_general(a, b, (((1,), (1,)), ((), ())), preferred_element_type=F32)


def _dot_tn(a, b):
    return lax.dot_general(a, b, (((0,), (0,)), ((), ())), preferred_element_type=F32)


def _mm(a, b, mode, out_dtype, tm, tn, tk, name, res=None, after=None):
    if mode == "nn":
        (m, k), (_, n) = a.shape, b.shape
    elif mode == "nt":
        (m, k), (n, _) = a.shape, b.shape
    else:
        (k, m), (_, n) = a.shape, b.shape
    tm, tn, tk = min(tm, m), min(tn, n), min(tk, k)
    assert m % tm == 0 and n % tn == 0 and k % tk == 0, (name, m, n, k, tm, tn, tk)
    nk = k // tk
    a_spec = {"nn": pl.BlockSpec((tm, tk), lambda i, j, kk: (i, kk)),
              "nt": pl.BlockSpec((tm, tk), lambda i, j, kk: (i, kk)),
              "tn": pl.BlockSpec((tk, tm), lambda i, j, kk: (kk, i))}[mode]
    b_spec = {"nn": pl.BlockSpec((tk, tn), lambda i, j, kk: (kk, j)),
              "nt": pl.BlockSpec((tn, tk), lambda i, j, kk: (j, kk)),
              "tn": pl.BlockSpec((tk, tn), lambda i, j, kk: (kk, j))}[mode]
    dot = {"nn": _dot, "nt": _dot_nt, "tn": _dot_tn}[mode]
    has_res = res is not None

    def body(*refs):
        if after is not None:
            refs = refs[:-3] + refs[-2:]
        if has_res:
            a_ref, b_ref, r_ref, o_ref, acc = refs
        else:
            a_ref, b_ref, o_ref, acc = refs
        kk = pl.program_id(2)

        @pl.when(kk == 0)
        def _():
            acc[...] = jnp.zeros_like(acc)

        acc[...] += dot(a_ref[...].astype(BF16), b_ref[...].astype(BF16))

        @pl.when(kk == nk - 1)
        def _():
            r = acc[...]
            if has_res:
                r = r + r_ref[...]
            o_ref[...] = r.astype(out_dtype)

    in_specs = [a_spec, b_spec]
    args = [a, b]
    if has_res:
        in_specs.append(pl.BlockSpec((tm, tn), lambda i, j, kk: (i, j)))
        args.append(res)
    if after is not None:
        in_specs.append(pl.BlockSpec(memory_space=pl.ANY))
        args.append(after)
    return pl.pallas_call(
        body, name=name,
        grid=(m // tm, n // tn, nk),
        in_specs=in_specs,
        out_specs=pl.BlockSpec((tm, tn), lambda i, j, kk: (i, j)),
        out_shape=jax.ShapeDtypeStruct((m, n), out_dtype),
        scratch_shapes=[pltpu.VMEM((tm, tn), F32)],
        compiler_params=_cp(("parallel", "parallel", "arbitrary")),
    )(*args)


def _rms(x, w):
    return x * lax.rsqrt(jnp.mean(x * x, axis=-1, keepdims=True) + EPS) * w


def _rms_fwd(x, w, name):
    L, D = x.shape
    tm = min(L, 256)

    def body(x_ref, w_ref, h_ref):
        h_ref[...] = _rms(x_ref[...], w_ref[...]).astype(BF16)

    return pl.pallas_call(
        body, name=name, grid=(L // tm,),
        in_specs=[pl.BlockSpec((tm, D), lambda i: (i, 0)), pl.BlockSpec((1, D), lambda i: (0, 0))],
        out_specs=pl.BlockSpec((tm, D), lambda i: (i, 0)),
        out_shape=jax.ShapeDtypeStruct((L, D), BF16),
        compiler_params=_cp(("parallel",)),
    )(x, w)


def _rms_bwd(x, w, dh, dres, name):
    L, D = x.shape
    tm = min(L, 256)

    def body(x_ref, w_ref, dh_ref, dres_ref, dx_ref, dw_ref):
        _, vjp = jax.vjp(_rms, x_ref[...], w_ref[...])
        dx, dw = vjp(dh_ref[...])
        dx_ref[...] = dx + dres_ref[...]

        @pl.when(pl.program_id(0) == 0)
        def _():
            dw_ref[...] = jnp.zeros_like(dw_ref)

        dw_ref[...] += dw

    row = pl.BlockSpec((tm, D), lambda i: (i, 0))
    vec = pl.BlockSpec((1, D), lambda i: (0, 0))
    return pl.pallas_call(
        body, name=name, grid=(L // tm,),
        in_specs=[row, vec, row, row],
        out_specs=[row, vec],
        out_shape=[jax.ShapeDtypeStruct((L, D), F32), jax.ShapeDtypeStruct((1, D), F32)],
        compiler_params=_cp(("arbitrary",)),
    )(x, w, dh, dres)


def _final(x, fw, tgt, name):
    L, D = x.shape
    tm = min(L, 256)

    def loss_fn(xv, wv, tv):
        err = _rms(xv, wv) - tv
        return jnp.sum(err * err) * (0.5 / D)

    def body(x_ref, w_ref, t_ref, loss_ref, dx_ref, dw_ref):
        tv = t_ref[...]
        val, vjp = jax.vjp(lambda a, b: loss_fn(a, b, tv), x_ref[...], w_ref[...])
        dx, dw = vjp(jnp.ones((), F32))
        dx_ref[...] = dx

        @pl.when(pl.program_id(0) == 0)
        def _():
            dw_ref[...] = jnp.zeros_like(dw_ref)
            loss_ref[...] = jnp.zeros_like(loss_ref)

        dw_ref[...] += dw
        loss_ref[...] += jnp.full(loss_ref.shape, val, F32)

    row = pl.BlockSpec((tm, D), lambda i: (i, 0))
    vec = pl.BlockSpec((1, D), lambda i: (0, 0))
    return pl.pallas_call(
        body, name=name, grid=(L // tm,),
        in_specs=[row, vec, row],
        out_specs=[pl.BlockSpec((8, 128), lambda i: (0, 0)), row, vec],
        out_shape=[jax.ShapeDtypeStruct((8, 128), F32), jax.ShapeDtypeStruct((L, D), F32),
                   jax.ShapeDtypeStruct((1, D), F32)],
        compiler_params=_cp(("arbitrary",)),
    )(x, fw, tgt)


def _s5_param_fn(a_re, a_im, log_dt, bt_re, bt_im):
    dt = jnp.exp(log_dt)
    zr, zi = a_re * dt, a_im * dt
    er = jnp.exp(zr)
    lr, li = er * jnp.cos(zi), er * jnp.sin(zi)
    nr, ni = lr - 1.0, li
    den = a_re * a_re + a_im * a_im
    cr = (nr * a_re + ni * a_im) / den
    ci = (ni * a_re - nr * a_im) / den
    bbr = cr[None] * bt_re - ci[None] * bt_im
    bbi = cr[None] * bt_im + ci[None] * bt_re
    return lr, li, bbr, bbi


def _s5_params_fwd(a_re, a_im, log_dt, bt_re, bt_im, name):
    def body(ar, ai, ld, br, bi, lr, li, bbr, bbi):
        o = _s5_param_fn(ar[...], ai[...], ld[...], br[...], bi[...])
        lr[...], li[...], bbr[...], bbi[...] = o

    gp = jax.ShapeDtypeStruct(a_re.shape, F32)
    cgp = jax.ShapeDtypeStruct(bt_re.shape, F32)
    return pl.pallas_call(body, name=name, out_shape=[gp, gp, cgp, cgp])(a_re, a_im, log_dt, bt_re, bt_im)


def _s5_params_bwd(a_re, a_im, log_dt, bt_re, bt_im, dlr, dli, dbbr, dbbi, name):
    def body(ar, ai, ld, br, bi, g0, g1, g2, g3, o0, o1, o2, o3, o4):
        _, vjp = jax.vjp(_s5_param_fn, ar[...], ai[...], ld[...], br[...], bi[...])
        o0[...], o1[...], o2[...], o3[...], o4[...] = vjp((g0[...], g1[...], g2[...], g3[...]))

    gp = jax.ShapeDtypeStruct(a_re.shape, F32)
    cgp = jax.ShapeDtypeStruct(bt_re.shape, F32)
    return pl.pallas_call(body, name=name,
                          out_shape=[gp, gp, jax.ShapeDtypeStruct(log_dt.shape, F32), cgp, cgp])(
        a_re, a_im, log_dt, bt_re, bt_im, dlr, dli, dbbr, dbbi)


def _cmul(ar, ai, br, bi):
    return ar * br - ai * bi, ar * bi + ai * br


def _cpow(lr, li, n):
    rr, ri = None, None
    br, bi = lr, li
    while n:
        if n & 1:
            rr, ri = (br, bi) if rr is None else _cmul(rr, ri, br, bi)
        n >>= 1
        if n:
            br, bi = _cmul(br, bi, br, bi)
    return rr, ri


def _shift_rows(x, up):
    row = lax.broadcasted_iota(jnp.int32, x.shape, 0)
    if up:
        return jnp.where(row == N_SEG - 1, 0.0, pltpu.roll(x, N_SEG - 1, 0))
    return jnp.where(row == 0, 0.0, pltpu.roll(x, 1, 0))


def _seg_scan(s_re, s_im, lam, seg, reverse):
    stride = seg + SEG_PAD
    nt = SLAB_ST // 128
    lam_t = [(jnp.broadcast_to(lam[0][:, j * 128:(j + 1) * 128], (N_SEG, 128)),
              jnp.broadcast_to(lam[1][:, j * 128:(j + 1) * 128], (N_SEG, 128))) for j in range(nt)]

    def rows(i):
        return pl.ds(i, N_SEG, stride=stride)

    def step1(t, carry):
        i = seg - 1 - t if reverse else t
        out = []
        for j in range(nt):
            cr, ci = carry[2 * j], carry[2 * j + 1]
            nr, ni = _cmul(lam_t[j][0], lam_t[j][1], cr, ci)
            nr = nr + s_re[j, rows(i), :]
            ni = ni + s_im[j, rows(i), :]
            s_re[j, rows(i), :] = nr
            s_im[j, rows(i), :] = ni
            out += [nr, ni]
        return tuple(out)

    zero = tuple(jnp.zeros((N_SEG, 128), F32) for _ in range(2 * nt))
    ends = lax.fori_loop(0, seg, step1, zero)

    carries = []
    for j in range(nt):
        pr, pi = _cpow(lam_t[j][0], lam_t[j][1], seg)
        cr, ci = jnp.zeros((N_SEG, 128), F32), jnp.zeros((N_SEG, 128), F32)
        for _ in range(N_SEG - 1):
            tr, ti = _cmul(pr, pi, cr, ci)
            cr = _shift_rows(tr + ends[2 * j], reverse)
            ci = _shift_rows(ti + ends[2 * j + 1], reverse)
        carries += [cr, ci]

    def step2(t, pw):
        i = seg - 1 - t if reverse else t
        out = []
        for j in range(nt):
            pr, pi = pw[2 * j], pw[2 * j + 1]
            ar, ai = _cmul(pr, pi, carries[2 * j], carries[2 * j + 1])
            s_re[j, rows(i), :] = s_re[j, rows(i), :] + ar
            s_im[j, rows(i), :] = s_im[j, rows(i), :] + ai
            qr, qi = _cmul(pr, pi, lam_t[j][0], lam_t[j][1])
            out += [qr, qi]
        return tuple(out)

    lax.fori_loop(0, seg, step2, tuple(x for j in range(nt) for x in lam_t[j]))
    return carries


def _seg_rows(ref, k, seg):
    stride = seg + SEG_PAD
    return jnp.concatenate([ref[j, pl.ds(k * stride, seg), :] for j in range(SLAB_ST // 128)], axis=-1)


def _seg_store(ref, k, seg, val):
    stride = seg + SEG_PAD
    for j in range(SLAB_ST // 128):
        ref[j, pl.ds(k * stride, seg), :] = val[:, j * 128:(j + 1) * 128]


def _s5_specs(L):
    col = lambda off: pl.BlockSpec((L, SLAB_CH), lambda j: (0, off + j))
    mat_b = pl.BlockSpec((None, SLAB_CH, SLAB_ST), lambda j: (j, 0, 0))
    mat_c = pl.BlockSpec((None, SLAB_ST, SLAB_CH), lambda j: (j, 0, 0))
    vec_s = pl.BlockSpec((None, 1, SLAB_ST), lambda j: (j, 0, 0))
    vec_c = pl.BlockSpec((None, 1, SLAB_CH), lambda j: (j, 0, 0))
    return col, mat_b, mat_c, vec_s, vec_c


def _s5_states(u_ref, bre_ref, bim_ref, lam, s_re, s_im, seg):
    for k in range(N_SEG):
        uk = u_ref[pl.ds(k * seg, seg), :]
        _seg_store(s_re, k, seg, _dot(uk, bre_ref[...]))
        _seg_store(s_im, k, seg, _dot(uk, bim_ref[...]))
    return _seg_scan(s_re, s_im, lam, seg, reverse=False)


def _s5_fwd(proj, bre, bim, cre_t, cim_t, lam_re, lam_im, dvec, name):
    L = proj.shape[0]
    seg = L // N_SEG
    col, mat_b, mat_c, vec_s, vec_c = _s5_specs(L)
    rows = N_SEG * (seg + SEG_PAD)

    def body(u_ref, bre_ref, bim_ref, cre_ref, cim_ref, lr_ref, li_ref, d_ref, y_ref, s_re, s_im):
        _s5_states(u_ref, bre_ref, bim_ref, (lr_ref[...], li_ref[...]), s_re, s_im, seg)
        for k in range(N_SEG):
            y = (_dot(_seg_rows(s_re, k, seg).astype(BF16), cre_ref[...])
                 - _dot(_seg_rows(s_im, k, seg).astype(BF16), cim_ref[...]))
            y = y + d_ref[...] * u_ref[pl.ds(k * seg, seg), :].astype(F32)
            y_ref[pl.ds(k * seg, seg), :] = jax.nn.gelu(y).astype(BF16)

    return pl.pallas_call(
        body, name=name, grid=(N_SLAB,),
        in_specs=[col(OFF_UA // SLAB_CH), mat_b, mat_b, mat_c, mat_c, vec_s, vec_s, vec_c],
        out_specs=pl.BlockSpec((L, SLAB_CH), lambda j: (0, j)),
        out_shape=jax.ShapeDtypeStruct((L, SSM_WIDTH), BF16),
        scratch_shapes=[pltpu.VMEM((SLAB_ST // 128, rows, 128), F32)] * 2,
        compiler_params=_cp(("parallel",)),
    )(proj, bre, bim, cre_t, cim_t, lam_re, lam_im, dvec)


def _s5_bwd(proj, dy, bre, bim, cre_t, cim_t, lam_re, lam_im, dvec, name):
    L = proj.shape[0]
    seg = L // N_SEG
    stride = seg + SEG_PAD
    col, mat_b, mat_c, vec_s, vec_c = _s5_specs(L)
    rows = N_SEG * stride
    nt = SLAB_ST // 128

    def body(u_ref, dy_ref, bre_ref, bim_ref, cre_ref, cim_ref, lr_ref, li_ref, d_ref,
             du_ref, dbre_ref, dbim_ref, dcre_ref, dcim_ref, dlr_ref, dli_ref, dd_ref,
             s_re, s_im, a_re, a_im, dyp):
        lam = (lr_ref[...], li_ref[...])
        carry_s = _s5_states(u_ref, bre_ref, bim_ref, lam, s_re, s_im, seg)
        dcre = jnp.zeros((SLAB_ST, SLAB_CH), F32)
        dcim = jnp.zeros((SLAB_ST, SLAB_CH), F32)
        dd = jnp.zeros((1, SLAB_CH), F32)
        for k in range(N_SEG):
            sre = _seg_rows(s_re, k, seg).astype(BF16)
            sim = _seg_rows(s_im, k, seg).astype(BF16)
            uk = u_ref[pl.ds(k * seg, seg), :].astype(F32)
            ypre = _dot(sre, cre_ref[...]) - _dot(sim, cim_ref[...]) + d_ref[...] * uk
            _, vjp = jax.vjp(jax.nn.gelu, ypre)
            (dyk,) = vjp(dy_ref[pl.ds(k * seg, seg), :].astype(F32))
            dyp[pl.ds(k * seg, seg), :] = dyk
            dd = dd + jnp.sum(dyk * uk, axis=0, keepdims=True)
            dyb = dyk.astype(BF16)
            dcre = dcre + _dot_tn(sre, dyb)
            dcim = dcim - _dot_tn(sim, dyb)
            _seg_store(a_re, k, seg, _dot_nt(dyb, cre_ref[...]))
            _seg_store(a_im, k, seg, -_dot_nt(dyb, cim_ref[...]))
        dcre_ref[...] = dcre
        dcim_ref[...] = dcim
        dd_ref[...] = dd

        _seg_scan(a_re, a_im, (lam[0], -lam[1]), seg, reverse=True)

        def acc_dlam(i, acc):
            out = []
            for j in range(nt):
                ar = a_re[j, pl.ds(i, N_SEG, stride=stride), :]
                ai = a_im[j, pl.ds(i, N_SEG, stride=stride), :]
                pr = s_re[j, pl.ds(i - 1, N_SEG, stride=stride), :]
                pi = s_im[j, pl.ds(i - 1, N_SEG, stride=stride), :]
                out += [acc[2 * j] + ar * pr + ai * pi, acc[2 * j + 1] + ai * pr - ar * pi]
            return tuple(out)

        first = []
        for j in range(nt):
            ar = a_re[j, pl.ds(0, N_SEG, stride=stride), :]
            ai = a_im[j, pl.ds(0, N_SEG, stride=stride), :]
            pr, pi = carry_s[2 * j], carry_s[2 * j + 1]
            first += [ar * pr + ai * pi, ai * pr - ar * pi]
        acc = lax.fori_loop(1, seg, acc_dlam, tuple(first))
        dlr_ref[...] = jnp.concatenate([jnp.sum(acc[2 * j], axis=0, keepdims=True) for j in range(nt)], axis=-1)
        dli_ref[...] = jnp.concatenate([jnp.sum(acc[2 * j + 1], axis=0, keepdims=True) for j in range(nt)], axis=-1)

        dbre = jnp.zeros((SLAB_CH, SLAB_ST), F32)
        dbim = jnp.zeros((SLAB_CH, SLAB_ST), F32)
        for k in range(N_SEG):
            are = _seg_rows(a_re, k, seg).astype(BF16)
            aim = _seg_rows(a_im, k, seg).astype(BF16)
            uk = u_ref[pl.ds(k * seg, seg), :]
            du = _dot_nt(are, bre_ref[...]) + _dot_nt(aim, bim_ref[...]) + dyp[pl.ds(k * seg, seg), :] * d_ref[...]
            du_ref[pl.ds(k * seg, seg), :] = du.astype(BF16)
            dbre = dbre + _dot_tn(uk, are)
            dbim = dbim + _dot_tn(uk, aim)
        dbre_ref[...] = dbre
        dbim_ref[...] = dbim

    scan_buf = pltpu.VMEM((nt, rows, 128), F32)
    return pl.pallas_call(
        body, name=name, grid=(N_SLAB,),
        in_specs=[col(OFF_UA // SLAB_CH), pl.BlockSpec((L, SLAB_CH), lambda j: (0, j)),
                  mat_b, mat_b, mat_c, mat_c, vec_s, vec_s, vec_c],
        out_specs=[pl.BlockSpec((L, SLAB_CH), lambda j: (0, j)), mat_b, mat_b, mat_c, mat_c, vec_s, vec_s, vec_c],
        out_shape=[jax.ShapeDtypeStruct((L, SSM_WIDTH), BF16),
                   jax.ShapeDtypeStruct((N_SLAB, SLAB_CH, SLAB_ST), F32),
                   jax.ShapeDtypeStruct((N_SLAB, SLAB_CH, SLAB_ST), F32),
                   jax.ShapeDtypeStruct((N_SLAB, SLAB_ST, SLAB_CH), F32),
                   jax.ShapeDtypeStruct((N_SLAB, SLAB_ST, SLAB_CH), F32),
                   jax.ShapeDtypeStruct((N_SLAB, 1, SLAB_ST), F32),
                   jax.ShapeDtypeStruct((N_SLAB, 1, SLAB_ST), F32),
                   jax.ShapeDtypeStruct((N_SLAB, 1, SLAB_CH), F32)],
        scratch_shapes=[scan_buf, scan_buf, scan_buf, scan_buf, pltpu.VMEM((L, SLAB_CH), F32)],
        compiler_params=_cp(("parallel",)),
    )(proj, dy, bre, bim, cre_t, cim_t, lam_re, lam_im, dvec)


def _glu_point(y0, pre, za, b):
    return y0 * jax.nn.sigmoid(pre + b) * jax.nn.silu(za)


def _glu_specs(L, tm):
    row = pl.BlockSpec((tm, SSM_WIDTH), lambda i: (i, 0))
    za = pl.BlockSpec((tm, SSM_WIDTH), lambda i: (i, OFF_ZA // SSM_WIDTH))
    wmat = pl.BlockSpec((SSM_WIDTH, SSM_WIDTH), lambda i: (0, 0))
    vec = pl.BlockSpec((1, SSM_WIDTH), lambda i: (0, 0))
    return row, za, wmat, vec


def _glu_fwd(ya0, proj, w, b, name):
    L = ya0.shape[0]
    tm = min(L, 512)
    row, za, wmat, vec = _glu_specs(L, tm)

    def body(y_ref, z_ref, w_ref, b_ref, o_ref):
        y0 = y_ref[...]
        pre = _dot(y0, w_ref[...])
        o_ref[...] = _glu_point(y0.astype(F32), pre, z_ref[...].astype(F32), b_ref[...]).astype(BF16)

    return pl.pallas_call(
        body, name=name, grid=(L // tm,), in_specs=[row, za, wmat, vec], out_specs=row,
        out_shape=jax.ShapeDtypeStruct((L, SSM_WIDTH), BF16), compiler_params=_cp(("parallel",)),
    )(ya0, proj, w, b)


def _glu_bwd(ya0, proj, w, b, dya, name):
    L = ya0.shape[0]
    tm = min(L, 512)
    row, za, wmat, vec = _glu_specs(L, tm)

    def body(y_ref, z_ref, w_ref, b_ref, g_ref, dy0_ref, dza_ref, dw_ref, db_ref):
        y0 = y_ref[...]
        pre = _dot(y0, w_ref[...])
        _, vjp = jax.vjp(_glu_point, y0.astype(F32), pre, z_ref[...].astype(F32), b_ref[...])
        dy0, dpre, dza, db = vjp(g_ref[...].astype(F32))
        dpb = dpre.astype(BF16)
        dy0_ref[...] = (dy0 + _dot_nt(dpb, w_ref[...])).astype(BF16)
        dza_ref[...] = dza.astype(BF16)

        @pl.when(pl.program_id(0) == 0)
        def _():
            dw_ref[...] = jnp.zeros_like(dw_ref)
            db_ref[...] = jnp.zeros_like(db_ref)

        dw_ref[...] += _dot_tn(y0, dpb)
        db_ref[...] += db

    return pl.pallas_call(
        body, name=name, grid=(L // tm,), in_specs=[row, za, wmat, vec, row],
        out_specs=[row, row, wmat, vec],
        out_shape=[jax.ShapeDtypeStruct((L, SSM_WIDTH), BF16), jax.ShapeDtypeStruct((L, SSM_WIDTH), BF16),
                   jax.ShapeDtypeStruct((SSM_WIDTH, SSM_WIDTH), F32), jax.ShapeDtypeStruct((1, SSM_WIDTH), F32)],
        compiler_params=_cp(("arbitrary",)),
    )(ya0, proj, w, b, dya)


def _sg_norm(vb, ln_w, ln_b):
    v0 = jax.nn.gelu(vb)
    mu = jnp.mean(v0, axis=-1, keepdims=True)
    var = jnp.mean(jnp.square(v0 - mu), axis=-1, keepdims=True)
    return (v0 - mu) * lax.rsqrt(var + EPS) * ln_w + ln_b


def _sg_gate(ub, mixed, zb):
    return jax.nn.gelu(ub) * mixed * jax.nn.silu(zb)


def _sg_specs():
    W = SSM_WIDTH
    blk = lambda off: pl.BlockSpec((CHUNK, W), lambda n: (n, off // W))
    out = pl.BlockSpec((CHUNK, W), lambda n: (n, 0))
    vec = pl.BlockSpec((1, W), lambda n: (0, 0))
    wsp = pl.BlockSpec((SG_HEADS, CHUNK, CHUNK), lambda n: (0, 0, 0))
    bsp = pl.BlockSpec((SG_HEADS, CHUNK, 1), lambda n: (0, 0, 0))
    return blk, out, vec, wsp, bsp


def _sg_masked(w_ref):
    t = lax.broadcasted_iota(jnp.int32, (CHUNK, CHUNK), 0)
    s = lax.broadcasted_iota(jnp.int32, (CHUNK, CHUNK), 1)
    causal = s <= t
    return causal, [jnp.where(causal, w_ref[h], 0.0).astype(BF16) for h in range(SG_HEADS)]


def _sg_mix(wm, vnb, bias_ref):
    return jnp.concatenate(
        [_dot(wm[h], vnb[:, h * CHUNK:(h + 1) * CHUNK]) + bias_ref[h] for h in range(SG_HEADS)], axis=-1)


def _sg_fwd(proj, ln_w, ln_b, w, bias, name):
    L = proj.shape[0]
    blk, out, vec, wsp, bsp = _sg_specs()

    def body(ub_ref, vb_ref, zb_ref, lw_ref, lb_ref, w_ref, bias_ref, o_ref):
        _, wm = _sg_masked(w_ref)
        vnb = _sg_norm(vb_ref[...].astype(F32), lw_ref[...], lb_ref[...]).astype(BF16)
        mixed = _sg_mix(wm, vnb, bias_ref)
        o_ref[...] = _sg_gate(ub_ref[...].astype(F32), mixed, zb_ref[...].astype(F32)).astype(BF16)

    return pl.pallas_call(
        body, name=name, grid=(L // CHUNK,),
        in_specs=[blk(OFF_UB), blk(OFF_VB), blk(OFF_ZB), vec, vec, wsp, bsp], out_specs=out,
        out_shape=jax.ShapeDtypeStruct((L, SSM_WIDTH), BF16), compiler_params=_cp(("parallel",)),
    )(proj, proj, proj, ln_w, ln_b, w, bias)


def _sg_bwd(proj, ln_w, ln_b, w, bias, dyb, name):
    L = proj.shape[0]
    blk, out, vec, wsp, bsp = _sg_specs()

    def body(ub_ref, vb_ref, zb_ref, lw_ref, lb_ref, w_ref, bias_ref, g_ref,
             dub_ref, dvb_ref, dzb_ref, dlw_ref, dlb_ref, dw_ref, dbias_ref):
        causal, wm = _sg_masked(w_ref)
        vb = vb_ref[...].astype(F32)
        vn, vjp_norm = jax.vjp(_sg_norm, vb, lw_ref[...], lb_ref[...])
        vnb = vn.astype(BF16)
        mixed = _sg_mix(wm, vnb, bias_ref)
        _, vjp_gate = jax.vjp(_sg_gate, ub_ref[...].astype(F32), mixed, zb_ref[...].astype(F32))
        dub, dmixed, dzb = vjp_gate(g_ref[...].astype(F32))
        dub_ref[...] = dub.astype(BF16)
        dzb_ref[...] = dzb.astype(BF16)

        @pl.when(pl.program_id(0) == 0)
        def _():
            dlw_ref[...] = jnp.zeros_like(dlw_ref)
            dlb_ref[...] = jnp.zeros_like(dlb_ref)
            dw_ref[...] = jnp.zeros_like(dw_ref)
            dbias_ref[...] = jnp.zeros_like(dbias_ref)

        dvn = []
        for h in range(SG_HEADS):
            dm = dmixed[:, h * CHUNK:(h + 1) * CHUNK]
            dmb = dm.astype(BF16)
            dbias_ref[h] += jnp.sum(dm, axis=-1, keepdims=True)
            dw_ref[h] += jnp.where(causal, _dot_nt(dmb, vnb[:, h * CHUNK:(h + 1) * CHUNK]), 0.0)
            dvn.append(_dot_tn(wm[h], dmb))
        dvb, dlw, dlb = vjp_norm(jnp.concatenate(dvn, axis=-1))
        dvb_ref[...] = dvb.astype(BF16)
        dlw_ref[...] += dlw
        dlb_ref[...] += dlb

    act = jax.ShapeDtypeStruct((L, SSM_WIDTH), BF16)
    return pl.pallas_call(
        body, name=name, grid=(L // CHUNK,),
        in_specs=[blk(OFF_UB), blk(OFF_VB), blk(OFF_ZB), vec, vec, wsp, bsp, out],
        out_specs=[out, out, out, vec, vec, wsp, bsp],
        out_shape=[act, act, act, jax.ShapeDtypeStruct((1, SSM_WIDTH), F32), jax.ShapeDtypeStruct((1, SSM_WIDTH), F32),
                   jax.ShapeDtypeStruct((SG_HEADS, CHUNK, CHUNK), F32), jax.ShapeDtypeStruct((SG_HEADS, CHUNK, 1), F32)],
        compiler_params=_cp(("arbitrary",)),
    )(proj, proj, proj, ln_w, ln_b, w, bias, dyb)


def _rope_tables(L):
    half = ROT_DIM // 2
    inv_freq = ROPE_THETA ** (-jnp.arange(0, ROT_DIM, 2, dtype=F32) / ROT_DIM)
    ang = jnp.arange(L, dtype=F32)[:, None] * inv_freq[None, :]
    cos, sin = jnp.cos(ang), jnp.sin(ang)
    ones = jnp.ones((L, HEAD_DIM - ROT_DIM), F32)
    cos_h = jnp.concatenate([cos, cos, ones], axis=-1)
    sin_h = jnp.concatenate([-sin, sin, 0.0 * ones], axis=-1)
    src = jnp.arange(HEAD_DIM)[:, None]
    dst = jnp.arange(HEAD_DIM)[None, :]
    p_h = (((dst < half) & (src == dst + half)) | ((dst >= half) & (dst < ROT_DIM) & (src == dst - half))).astype(F32)
    p2 = jnp.kron(jnp.eye(2, dtype=F32), p_h).astype(BF16)
    return jnp.tile(cos_h, (1, 2)), jnp.tile(sin_h, (1, 2)), p2


def _rope(t, cos, sin, p2):
    n = t.shape[1] // 128
    tb = t.astype(BF16)
    sw = jnp.concatenate([_dot(tb[:, i * 128:(i + 1) * 128], p2) for i in range(n)], axis=-1) if n > 1 else _dot(tb, p2)
    return t * jnp.tile(cos, (1, n)) + sw * jnp.tile(sin, (1, n))


def _rope_t(g, cos, sin, p2):
    n = g.shape[1] // 128
    gs = (g * jnp.tile(sin, (1, n))).astype(BF16)
    sw = jnp.concatenate([_dot_nt(gs[:, i * 128:(i + 1) * 128], p2) for i in range(n)], axis=-1) if n > 1 else _dot_nt(gs, p2)
    return g * jnp.tile(cos, (1, n)) + sw


def _lane_lo(shape):
    return (lax.broadcasted_iota(jnp.int32, shape, len(shape) - 1) % 128) < HEAD_DIM


def _dup_halves(x):
    xr = pltpu.roll(x, HEAD_DIM, 1)
    lo = _lane_lo(x.shape)
    return jnp.where(lo, x, xr), jnp.where(lo, xr, x)


def _fold_halves(d0, d1):
    f0 = d0 + pltpu.roll(d0, HEAD_DIM, 1)
    f1 = d1 + pltpu.roll(d1, HEAD_DIM, 1)
    return jnp.where(_lane_lo(d0.shape), f0, f1)


def _attn_mask():
    qi = lax.broadcasted_iota(jnp.int32, (CHUNK, 2 * CHUNK), 0)
    kj = lax.broadcasted_iota(jnp.int32, (CHUNK, 2 * CHUNK), 1)
    return qi, kj


def _attn_specs():
    qsp = pl.BlockSpec((CHUNK, 1024), lambda n: (n, OFF_Q // 1024))
    kv_cur = pl.BlockSpec((CHUNK, 256), lambda n: (n, OFF_KV // 256))
    kv_prev = pl.BlockSpec((CHUNK, 256), lambda n: (jnp.maximum(n - 1, 0), OFF_KV // 256))
    zsp = [pl.BlockSpec((CHUNK, 256), functools.partial(lambda n, q: (n, OFF_ZC // 256 + q), q=q)) for q in range(4)]
    tab_cur = pl.BlockSpec((CHUNK, 128), lambda n: (n, 0))
    tab_prev = pl.BlockSpec((CHUNK, 128), lambda n: (jnp.maximum(n - 1, 0), 0))
    p2sp = pl.BlockSpec((128, 128), lambda n: (0, 0))
    sink = pl.BlockSpec(memory_space=pltpu.SMEM)
    wide = pl.BlockSpec((CHUNK, 1024), lambda n: (n, 0))
    return qsp, kv_cur, kv_prev, zsp, tab_cur, tab_prev, p2sp, sink, wide


def _attn_core(n, q_ref, kvc_ref, kvp_ref, cosc_ref, sinc_ref, cosp_ref, sinp_ref, p2_ref, sink_ref):
    p2 = p2_ref[...]
    qr = _rope(q_ref[...].astype(F32), cosc_ref[...], sinc_ref[...], p2).astype(BF16)
    kc = _rope(kvc_ref[:, 0:128].astype(F32), cosc_ref[...], sinc_ref[...], p2)
    kp = _rope(kvp_ref[:, 0:128].astype(F32), cosp_ref[...], sinp_ref[...], p2)
    k_all = jnp.concatenate([kp, kc], axis=0).astype(BF16)
    v_all = jnp.concatenate([kvp_ref[:, 128:256], kvc_ref[:, 128:256]], axis=0)
    kd = _dup_halves(k_all)
    vd = _dup_halves(v_all)
    qi, kj = _attn_mask()
    allowed = ((kj < CHUNK) & (kj > qi) & (n > 0)) | ((kj >= CHUNK) & (kj - CHUNK <= qi))
    lo = _lane_lo((CHUNK, 128))
    probs = []
    for h in range(ATT_HEADS):
        m, half, g = h // 2, h % 2, h // 8
        qp = qr[:, m * 128:(m + 1) * 128]
        qm = jnp.where(lo if half == 0 else ~lo, qp, jnp.zeros_like(qp))
        s = jnp.where(allowed, _dot_nt(qm, kd[g]) * (HEAD_DIM ** -0.5), NEG_INF)
        snk = sink_ref[h]
        mx = jnp.maximum(jnp.max(s, axis=-1, keepdims=True), snk)
        e = jnp.exp(s - mx)
        es = jnp.exp(snk - mx)
        inv = 1.0 / (jnp.sum(e, axis=-1, keepdims=True) + es)
        probs.append((qm, e * inv, es * inv))
    return qr, kd, vd, probs, lo


def _attn_out(vd, probs, lo):
    outs = []
    for m in range(ATT_HEADS // 2):
        g = m // 4
        o0 = _dot(probs[2 * m][1].astype(BF16), vd[g])
        o1 = _dot(probs[2 * m + 1][1].astype(BF16), vd[g])
        outs.append(jnp.where(lo, o0, o1))
    return jnp.concatenate(outs, axis=-1)


def _silu_gate(o, z):
    return o * jax.nn.silu(z)


def _attn_fwd(proj, sinks, tabs, name):
    L = proj.shape[0]
    cos2, sin2, p2 = tabs
    qsp, kv_cur, kv_prev, zsp, tab_cur, tab_prev, p2sp, sink, wide = _attn_specs()

    def body(q_ref, kvc_ref, kvp_ref, z0, z1, z2, z3, cosc, sinc, cosp, sinp, p2_ref, sink_ref, o_ref):
        n = pl.program_id(0)
        _, _, vd, probs, lo = _attn_core(n, q_ref, kvc_ref, kvp_ref, cosc, sinc, cosp, sinp, p2_ref, sink_ref)
        o = _attn_out(vd, probs, lo)
        z = jnp.concatenate([z0[...], z1[...], z2[...], z3[...]], axis=-1).astype(F32)
        o_ref[...] = _silu_gate(o, z).astype(BF16)

    return pl.pallas_call(
        body, name=name, grid=(L // CHUNK,),
        in_specs=[qsp, kv_cur, kv_prev, *zsp, tab_cur, tab_cur, tab_prev, tab_prev, p2sp, sink],
        out_specs=wide, out_shape=jax.ShapeDtypeStruct((L, 1024), BF16), compiler_params=_cp(("parallel",)),
    )(proj, proj, proj, proj, proj, proj, proj, cos2, sin2, cos2, sin2, p2, sinks)


def _attn_bwd(proj, sinks, tabs, dyc, name):
    L = proj.shape[0]
    cos2, sin2, p2 = tabs
    qsp, kv_cur, kv_prev, zsp, tab_cur, tab_prev, p2sp, sink, wide = _attn_specs()
    kvo = pl.BlockSpec((CHUNK, 256), lambda n: (n, 0))

    def body(q_ref, kvc_ref, kvp_ref, z0, z1, z2, z3, cosc, sinc, cosp, sinp, p2_ref, sink_ref, g_ref,
             dq_ref, dz_ref, dkvc_ref, dkvp_ref, dsink_ref):
        n = pl.program_id(0)
        _, kd, vd, probs, lo = _attn_core(n, q_ref, kvc_ref, kvp_ref, cosc, sinc, cosp, sinp, p2_ref, sink_ref)
        o = _attn_out(vd, probs, lo)
        z = jnp.concatenate([z0[...], z1[...], z2[...], z3[...]], axis=-1).astype(F32)
        _, vjp = jax.vjp(_silu_gate, o, z)
        do, dz = vjp(g_ref[...].astype(F32))
        dz_ref[...] = dz.astype(BF16)

        @pl.when(n == 0)
        def _():
            dsink_ref[...] = jnp.zeros_like(dsink_ref)

        dkd = [jnp.zeros((2 * CHUNK, 128), F32), jnp.zeros((2 * CHUNK, 128), F32)]
        dvd = [jnp.zeros((2 * CHUNK, 128), F32), jnp.zeros((2 * CHUNK, 128), F32)]
        dq_pairs = []
        for m in range(ATT_HEADS // 2):
            g = m // 4
            dop = do[:, m * 128:(m + 1) * 128].astype(BF16)
            dq_h = []
            for half in range(2):
                h = 2 * m + half
                qm, p, ps = probs[h]
                dom = jnp.where(lo if half == 0 else ~lo, dop, jnp.zeros_like(dop))
                dp = _dot_nt(dom, vd[g])
                rs = jnp.sum(p * dp, axis=-1, keepdims=True)
                ds = (p * (dp - rs) * (HEAD_DIM ** -0.5)).astype(BF16)
                dsink_ref[h:h + 1, :] += jnp.broadcast_to(jnp.sum(-ps * rs, axis=0, keepdims=True), (1, 128))
                dq_h.append(_dot(ds, kd[g]))
                dkd[g] = dkd[g] + _dot_tn(ds, qm)
                dvd[g] = dvd[g] + _dot_tn(p.astype(BF16), dom)
            dq_pairs.append(jnp.where(lo, dq_h[0], dq_h[1]))
        p2 = p2_ref[...]
        dq_ref[...] = _rope_t(jnp.concatenate(dq_pairs, axis=-1), cosc[...], sinc[...], p2).astype(BF16)
        dk_rot = _fold_halves(dkd[0], dkd[1])
        dv = _fold_halves(dvd[0], dvd[1])
        dkp = _rope_t(dk_rot[0:CHUNK], cosp[...], sinp[...], p2)
        dkc = _rope_t(dk_rot[CHUNK:2 * CHUNK], cosc[...], sinc[...], p2)
        dkvp_ref[...] = jnp.concatenate([dkp, dv[0:CHUNK]], axis=-1)
        dkvc_ref[...] = jnp.concatenate([dkc, dv[CHUNK:2 * CHUNK]], axis=-1)

    act = jax.ShapeDtypeStruct((L, 1024), BF16)
    kvs = jax.ShapeDtypeStruct((L, 256), F32)
    return pl.pallas_call(
        body, name=name, grid=(L // CHUNK,),
        in_specs=[qsp, kv_cur, kv_prev, *zsp, tab_cur, tab_cur, tab_prev, tab_prev, p2sp, sink, wide],
        out_specs=[wide, wide, kvo, kvo, pl.BlockSpec((ATT_HEADS, 128), lambda n: (0, 0))],
        out_shape=[act, act, kvs, kvs, jax.ShapeDtypeStruct((ATT_HEADS, 128), F32)],
        compiler_params=_cp(("arbitrary",)),
    )(proj, proj, proj, proj, proj, proj, proj, cos2, sin2, cos2, sin2, p2, sinks, dyc)


MERGE_TN = 256


def _merge_point(ta, tb, tc, ga, gb, gc):
    return jax.nn.sigmoid(ga) * ta + jax.nn.sigmoid(gb) * tb + jax.nn.sigmoid(gc) * tc


def _merge_specs(tm):
    nj = D_MODEL // MERGE_TN
    t = pl.BlockSpec((tm, MERGE_TN), lambda i, j: (i, j))
    gates = [pl.BlockSpec((tm, MERGE_TN), functools.partial(lambda i, j, b: (i, OFF_G // MERGE_TN + b * nj + j), b=b))
             for b in range(3)]
    return t, gates, nj


def _merge_fwd(ta, tb, tc, proj, name):
    L = ta.shape[0]
    tm = min(L, 1024)
    t, gates, nj = _merge_specs(tm)

    def body(ta_ref, tb_ref, tc_ref, ga_ref, gb_ref, gc_ref, o_ref):
        f = lambda r: r[...].astype(F32)
        o_ref[...] = _merge_point(f(ta_ref), f(tb_ref), f(tc_ref), f(ga_ref), f(gb_ref), f(gc_ref)).astype(BF16)

    return pl.pallas_call(
        body, name=name, grid=(L // tm, nj), in_specs=[t, t, t, *gates], out_specs=t,
        out_shape=jax.ShapeDtypeStruct((L, D_MODEL), BF16), compiler_params=_cp(("parallel", "parallel")),
    )(ta, tb, tc, proj, proj, proj)


def _merge_bwd(ta, tb, tc, proj, dm, name):
    L = ta.shape[0]
    tm = min(L, 1024)
    t, gates, nj = _merge_specs(tm)

    def body(ta_ref, tb_ref, tc_ref, ga_ref, gb_ref, gc_ref, dm_ref, dta_ref, dtb_ref, dtc_ref, dga_ref, dgb_ref, dgc_ref):
        f = lambda r: r[...].astype(F32)
        _, vjp = jax.vjp(_merge_point, f(ta_ref), f(tb_ref), f(tc_ref), f(ga_ref), f(gb_ref), f(gc_ref))
        outs = vjp(f(dm_ref))
        for r, v in zip((dta_ref, dtb_ref, dtc_ref, dga_ref, dgb_ref, dgc_ref), outs):
            r[...] = v.astype(BF16)

    act = jax.ShapeDtypeStruct((L, D_MODEL), BF16)
    return pl.pallas_call(
        body, name=name, grid=(L // tm, nj), in_specs=[t, t, t, *gates, t],
        out_specs=[t] * 6, out_shape=[act] * 6,
        compiler_params=_cp(("parallel", "parallel")),
    )(ta, tb, tc, proj, proj, proj, dm)


GRAD_DT = BF16
SMALL = ("norm_w", "ssm_a_re", "ssm_a_im", "ssm_log_dt", "ssm_b_re", "ssm_b_im", "ssm_c_re", "ssm_c_im", "ssm_d",
         "ssm_glu_b", "sg_ln_w", "sg_ln_b", "sg_w", "sg_b", "attn_sinks")
G8 = SSM_GROUPS // N_SLAB


def _diag_mask(rows_per_group, cols_per_group):
    r = jnp.arange(G8 * rows_per_group)[:, None] // rows_per_group
    c = jnp.arange(G8 * cols_per_group)[None, :] // cols_per_group
    return r == c


def _slab_b(bb_t):
    x = bb_t.transpose(1, 0, 2).reshape(N_SLAB, SLAB_CH, SSM_STATE)
    return jnp.where(_diag_mask(SSM_GROUP, SSM_STATE), jnp.tile(x, (1, 1, G8)), 0)


def _unslab_b(d):
    x = jnp.where(_diag_mask(SSM_GROUP, SSM_STATE), d, 0).reshape(N_SLAB, SLAB_CH, G8, SSM_STATE).sum(axis=2)
    return x.reshape(SSM_GROUPS, SSM_GROUP, SSM_STATE).transpose(1, 0, 2)


def _slab_c(c):
    x = c.transpose(0, 2, 1).reshape(N_SLAB, SLAB_ST, SSM_GROUP)
    return jnp.where(_diag_mask(SSM_STATE, SSM_GROUP), jnp.tile(x, (1, 1, G8)), 0)


def _unslab_c(d):
    x = jnp.where(_diag_mask(SSM_STATE, SSM_GROUP), d, 0).reshape(N_SLAB, SLAB_ST, G8, SSM_GROUP).sum(axis=2)
    return x.reshape(SSM_GROUPS, SSM_STATE, SSM_GROUP).transpose(0, 2, 1)


def _s5_prep(p, tag):
    bt_re = p["ssm_b_re"].transpose(2, 0, 1)
    bt_im = p["ssm_b_im"].transpose(2, 0, 1)
    raw = (p["ssm_a_re"], p["ssm_a_im"], p["ssm_log_dt"][:, None], bt_re, bt_im)
    lr, li, bbr, bbi = _s5_params_fwd(*raw, name=f"s5_params_{tag}")
    ops = (_slab_b(bbr).astype(BF16), _slab_b(bbi).astype(BF16),
           _slab_c(p["ssm_c_re"]).astype(BF16), _slab_c(p["ssm_c_im"]).astype(BF16),
           lr.reshape(N_SLAB, 1, SLAB_ST), li.reshape(N_SLAB, 1, SLAB_ST), p["ssm_d"].reshape(N_SLAB, 1, SLAB_CH))
    return raw, ops


def _layer_fwd(x, p, w, tabs, tag, s5=None, win_of=None, after_proj=None):
    L = x.shape[0]
    h = _rms_fwd(x, p["norm_w"][None], f"rms_fwd_{tag}")
    win_t, proj_after = win_of(h) if win_of is not None else (w["win_t"], None)
    proj = _mm(h, win_t, "nt", BF16, L, 256, D_MODEL, f"in_proj_{tag}", after=proj_after)
    if after_proj is not None:
        w = after_proj(proj)
    s5_raw, s5_ops = s5 if s5 is not None else _s5_prep(p, tag)
    ya0 = _s5_fwd(proj, *s5_ops, name=f"s5_fwd_{tag}")
    ya = _glu_fwd(ya0, proj, w["glu"], p["ssm_glu_b"][None], f"glu_fwd_{tag}")
    yb = _sg_fwd(proj, p["sg_ln_w"][None], p["sg_ln_b"][None], p["sg_w"], p["sg_b"][:, :, None], f"sg_fwd_{tag}")
    yc = _attn_fwd(proj, p["attn_sinks"], tabs, f"attn_fwd_{tag}")
    ta = _mm(ya, w["wba_t"], "nt", BF16, 1024, 1024, 1024, f"branch_a_{tag}")
    tb = _mm(yb, w["wbb_t"], "nt", BF16, 1024, 1024, 1024, f"branch_b_{tag}")
    tc = _mm(yc, w["wbc_t"], "nt", BF16, 1024, 1024, 1024, f"branch_c_{tag}")
    merged = _merge_fwd(ta, tb, tc, proj, f"merge_fwd_{tag}")
    x_new = _mm(merged, w["wout"], "nn", F32, 1024, 512, D_MODEL, f"out_proj_{tag}", res=x)
    saved = dict(x=x, h=h, proj=proj, s5_raw=s5_raw, s5_ops=s5_ops, ya0=ya0, ya=ya, yb=yb, yc=yc,
                 ta=ta, tb=tb, tc=tc, merged=merged)
    return x_new, saved


def _layer_bwd(dx_out, p, w, tabs, s, tag, first_after=None, before_win=None, after_win=None):
    L = dx_out.shape[0]
    proj = s["proj"]
    big, small = {}, {}
    dmerged = _mm(dx_out, w["wout"], "nt", BF16, 1024, 512, D_MODEL, f"d_merged_{tag}", after=first_after)
    big["wout"] = _mm(s["merged"], dx_out, "tn", GRAD_DT, 512, 1024, L, f"d_wout_{tag}")
    dta, dtb, dtc, dga, dgb, dgc = _merge_bwd(s["ta"], s["tb"], s["tc"], proj, dmerged, f"merge_bwd_{tag}")
    dy = {}
    for br, dt in (("a", dta), ("b", dtb), ("c", dtc)):
        dy[br] = _mm(dt, w[f"wb{br}_t"], "nn", BF16, 1024, 1024, D_MODEL, f"d_y{br}_{tag}")
        big[f"wb{br}_t"] = _mm(dt, s[f"y{br}"], "tn", GRAD_DT, 512, 1024, L, f"d_wb{br}_{tag}")

    dq, dzc, dkvc, dkvp, dsink = _attn_bwd(proj, p["attn_sinks"], tabs, dy["c"], f"attn_bwd_{tag}")
    dkv = dkvc + jnp.concatenate([dkvp[CHUNK:], jnp.zeros((CHUNK, 256), F32)], axis=0)
    small["attn_sinks"] = dsink[:, 0]

    dub, dvb, dzb, dlw, dlb, dsgw, dsgb = _sg_bwd(
        proj, p["sg_ln_w"][None], p["sg_ln_b"][None], p["sg_w"], p["sg_b"][:, :, None], dy["b"], f"sg_bwd_{tag}")
    small.update(sg_ln_w=dlw[0], sg_ln_b=dlb[0], sg_w=dsgw, sg_b=dsgb[:, :, 0])

    dya0, dza, dglu, dglub = _glu_bwd(s["ya0"], proj, w["glu"], p["ssm_glu_b"][None], dy["a"], f"glu_bwd_{tag}")
    big["glu"] = dglu.astype(GRAD_DT)
    small["ssm_glu_b"] = dglub[0]

    dua, dbre, dbim, dcre, dcim, dlr, dli, dd = _s5_bwd(proj, dya0, *s["s5_ops"], name=f"s5_bwd_{tag}")
    da_re, da_im, dlog_dt, dbt_re, dbt_im = _s5_params_bwd(
        *s["s5_raw"], dlr.reshape(SSM_GROUPS, SSM_STATE), dli.reshape(SSM_GROUPS, SSM_STATE),
        _unslab_b(dbre), _unslab_b(dbim), name=f"s5_params_bwd_{tag}")
    small.update(ssm_a_re=da_re, ssm_a_im=da_im, ssm_log_dt=dlog_dt[:, 0],
                 ssm_bt_re=dbt_re, ssm_bt_im=dbt_im,
                 ssm_c_re=_unslab_c(dcre), ssm_c_im=_unslab_c(dcim), ssm_d=dd.reshape(SSM_WIDTH))

    dproj = jnp.concatenate([dua, dza, dub, dvb, dzb, dq, dkv.astype(BF16), dzc, dga, dgb, dgc], axis=-1)
    tok = before_win(big) if before_win is not None else None
    big["win_t"] = _mm(dproj, s["h"], "tn", GRAD_DT, 256, D_MODEL, L, f"d_win_{tag}", after=tok)
    tok = after_win(big) if after_win is not None else None
    dh = _mm(dproj, w["win_t"], "nn", F32, 1024, D_MODEL, 256, f"d_h_{tag}", after=tok)
    dx_in, dnw = _rms_bwd(s["x"], p["norm_w"][None], dh, dx_out, f"rms_bwd_{tag}")
    small["norm_w"] = dnw[0]
    return dx_in, big, small


def _local_step(x, tgt, small_p, final_w, big_w):
    L = x.shape[0]
    tabs = _rope_tables(L)
    saved = []
    for l in range(DEPTH):
        x, s = _layer_fwd(x, small_p[l], big_w[l], tabs, f"l{l}")
        saved.append(s)
    loss_acc, dx, dfw = _final(x, final_w[None], tgt, "final_norm_loss")
    big_g, small_g = [None] * DEPTH, [None] * DEPTH
    for l in reversed(range(DEPTH)):
        dx, big_g[l], small_g[l] = _layer_bwd(dx, small_p[l], big_w[l], tabs, saved[l], f"l{l}")
    return loss_acc[0, 0], dx, dfw[0], big_g, small_g


MESH = pl.DeviceIdType.MESH
ANY = pl.BlockSpec(memory_space=pl.ANY)
ROW_ALIGN = 16


def _place():
    return lax.axis_index("x"), lax.axis_index("y"), lax.axis_index("c")


HBM = pl.BlockSpec(memory_space=pltpu.HBM)
SEM = pl.BlockSpec(memory_space=pltpu.SEMAPHORE)
EFFECT = pltpu.SideEffectType.DATAFLOW_SIDE_EFFECTING


def _split_start(srcs, lands, n_copies, copies, name, after=None):
    n, m, k = len(srcs), len(lands), n_copies
    extra = [] if after is None else [after]

    def body(*refs):
        src_refs, land_refs = refs[:n], refs[n:n + m]
        sems = refs[n + m + len(extra):]
        send_sems, recv_sems, token = sems[:k], sems[k:2 * k], refs[-1]
        for cp in copies(src_refs, land_refs, send_sems, recv_sems):
            cp.start()
        token[...] = jnp.zeros_like(token)

    ops = list(srcs) + list(lands)
    outs = pl.pallas_call(
        body, name=name,
        out_shape=(*[pltpu.SemaphoreType.DMA(())] * (2 * k),
                   *[pltpu.HBM(a.shape, a.dtype) for a in ops], jax.ShapeDtypeStruct((8, 128), F32)),
        in_specs=[HBM] * (n + m) + [ANY] * len(extra),
        out_specs=(*[SEM] * (2 * k), *[HBM] * (n + m), pl.BlockSpec(memory_space=pltpu.VMEM)),
        input_output_aliases={i: 2 * k + i for i in range(n + m)},
        compiler_params=pltpu.CompilerParams(has_side_effects=EFFECT),
    )(*[pltpu.with_memory_space_constraint(a, pltpu.HBM) for a in ops], *extra)
    return (list(outs[:k]), list(outs[k:2 * k]), list(outs[2 * k:2 * k + n]), list(outs[2 * k + n:2 * k + n + m]),
            outs[-1])


def _split_wait(send_sems, recv_sems, srcs, lands, after, copies, name):
    n, m, k = len(srcs), len(lands), len(send_sems)
    after = list(after) if isinstance(after, (list, tuple)) else [after]

    def body(*refs):
        src_refs, land_refs = refs[:n], refs[n:n + m]
        for cp in copies(src_refs, land_refs, refs[n + m:n + m + k], refs[n + m + k:n + m + 2 * k]):
            cp.wait_send()
            cp.wait_recv()

    ops = list(srcs) + list(lands)
    outs = pl.pallas_call(
        body, name=name,
        out_shape=tuple(pltpu.HBM(a.shape, a.dtype) for a in ops),
        in_specs=[HBM] * (n + m) + [SEM] * (2 * k) + [ANY] * len(after),
        out_specs=tuple([HBM] * (n + m)),
        input_output_aliases={i: i for i in range(n + m)},
        compiler_params=pltpu.CompilerParams(has_side_effects=EFFECT),
    )(*ops, *send_sems, *recv_sems, *after)
    return list(outs[:n]), list(outs[n:])


def _ag_rows(land_ref, px, py, pc):
    r = land_ref.shape[0] // N_DEV
    start = pl.multiple_of((4 * px + 2 * py + pc) * r, ROW_ALIGN)
    return land_ref.at[pl.ds(start, r), :]


def _ag_copies(src_refs, land_refs, send_sems, recv_sems):
    x, y, c = _place()
    peers = [(x, y, 1 - c), (1 - x, y, c), (x, 1 - y, c), (1 - x, 1 - y, c)]
    return [pltpu.make_async_remote_copy(
        src_ref=_ag_rows(land_refs[a], x, y, c), dst_ref=_ag_rows(land_refs[a], x, y, c),
        send_sem=send_sems[4 * a + k], recv_sem=recv_sems[4 * a + k], device_id=peer, device_id_type=MESH)
        for a in range(len(land_refs)) for k, peer in enumerate(peers)]


def _ag_forward(lands, name):
    n = len(lands)

    def body(*refs):
        land_refs = refs[n:2 * n]
        send_sems, recv_sems = refs[2 * n:]
        x, y, c = _place()
        chips = [(1 - x, y), (x, 1 - y), (1 - x, 1 - y)]

        def copy(a, j, pc):
            px, py = chips[j]
            return pltpu.make_async_remote_copy(
                src_ref=_ag_rows(land_refs[a], px, py, pc), dst_ref=_ag_rows(land_refs[a], px, py, pc),
                send_sem=send_sems.at[a, j], recv_sem=recv_sems.at[a, j], device_id=(x, y, 1 - c), device_id_type=MESH)

        passed = [copy(a, j, c) for a in range(n) for j in range(3)]
        for cp in passed:
            cp.start()
        for a in range(n):
            for j in range(3):
                copy(a, j, 1 - c).wait_recv()
        for cp in passed:
            cp.wait_send()

    return pl.pallas_call(
        body, name=name,
        in_specs=[ANY] * n, out_specs=[ANY] * n,
        out_shape=[jax.ShapeDtypeStruct(l.shape, l.dtype) for l in lands],
        input_output_aliases={i: i for i in range(n)},
        scratch_shapes=[pltpu.SemaphoreType.DMA((n, 3)), pltpu.SemaphoreType.DMA((n, 3))],
    )(*lands)


def _allgather_place(shards):
    x, y, c = _place()
    return [lax.dynamic_update_slice(lax.empty((N_DEV * s.shape[0], s.shape[1]), s.dtype), s,
                                     ((4 * x + 2 * y + c) * s.shape[0], 0)) for s in shards]


def _allgather_start(lands, name, after=None):
    return _split_start([], lands, 4 * len(lands), _ag_copies, name + "_start", after=after)


def _allgather_finish(started, after, name):
    send_sems, recv_sems, _, lands, _ = started
    _, lands = _split_wait(send_sems, recv_sems, [], lands, after, _ag_copies, name + "_wait")
    return list(_ag_forward(lands, name + "_forward"))


def _rs_swap_cores(grads, name):
    n = len(grads)

    def body(*refs):
        ins, outs = refs[:n], refs[n:2 * n]
        send_sems, recv_sems = refs[2 * n:]
        x, y, c = _place()
        cps = []
        for a in range(n):
            r = ins[a].shape[0] // N_DEV
            for q in range(4):
                start = pl.multiple_of((2 * q + 1 - c) * r, ROW_ALIGN)
                cps.append(pltpu.make_async_remote_copy(
                    src_ref=ins[a].at[pl.ds(start, r), :], dst_ref=outs[a].at[q],
                    send_sem=send_sems.at[a, q], recv_sem=recv_sems.at[a, q],
                    device_id=(x, y, 1 - c), device_id_type=MESH))
        for cp in cps:
            cp.start()
        for cp in cps:
            cp.wait()

    return pl.pallas_call(
        body, name=name, in_specs=[ANY] * n, out_specs=[ANY] * n,
        out_shape=[jax.ShapeDtypeStruct((4, g.shape[0] // N_DEV, g.shape[1]), g.dtype) for g in grads],
        scratch_shapes=[pltpu.SemaphoreType.DMA((n, 4)), pltpu.SemaphoreType.DMA((n, 4))],
    )(*grads)


def _rs_chip_copies(sum_refs, land_refs, send_sems, recv_sems):
    x, y, c = _place()
    chips = [(1 - x, y), (x, 1 - y), (1 - x, 1 - y)]
    return [pltpu.make_async_remote_copy(
        src_ref=sum_refs[a].at[2 * px + py], dst_ref=land_refs[a].at[2 * x + y],
        send_sem=send_sems[3 * a + j], recv_sem=recv_sems[3 * a + j], device_id=(px, py, c), device_id_type=MESH)
        for a in range(len(sum_refs)) for j, (px, py) in enumerate(chips)]


def _row_tile(r):
    return max(t for t in range(ROW_ALIGN, min(r, 1024) + 1, ROW_ALIGN) if r % t == 0)


def _rs_add_cores(grad, recv, cidx, name):
    r, cols = recv.shape[1], recv.shape[2]
    tr = _row_tile(r)
    nb = r // tr

    def body(c_ref, g_ref, r_ref, o_ref):
        o_ref[...] = (g_ref[...].astype(F32) + r_ref[...].astype(F32)).astype(o_ref.dtype)

    return pl.pallas_call(
        body, name=name,
        grid_spec=pltpu.PrefetchScalarGridSpec(
            num_scalar_prefetch=1, grid=(4, nb),
            in_specs=[pl.BlockSpec((tr, cols), lambda q, i, c_ref: ((2 * q + c_ref[0]) * nb + i, 0)),
                      pl.BlockSpec((None, tr, cols), lambda q, i, c_ref: (q, i, 0))],
            out_specs=pl.BlockSpec((None, tr, cols), lambda q, i, c_ref: (q, i, 0))),
        out_shape=jax.ShapeDtypeStruct(recv.shape, recv.dtype),
        compiler_params=_cp(("parallel", "parallel")),
    )(cidx, grad, recv)


def _rs_add_chips(own, recv, slots, name):
    r, cols = recv.shape[1], recv.shape[2]
    tr = _row_tile(r)

    def body(s_ref, o_ref, r0_ref, r1_ref, r2_ref, out_ref):
        acc = o_ref[...].astype(F32)
        for ref in (r0_ref, r1_ref, r2_ref):
            acc = acc + ref[...].astype(F32)
        out_ref[...] = acc

    pick = lambda k: pl.BlockSpec((None, tr, cols), functools.partial(lambda i, s_ref, k: (s_ref[k], i, 0), k=k))
    return pl.pallas_call(
        body, name=name,
        grid_spec=pltpu.PrefetchScalarGridSpec(
            num_scalar_prefetch=1, grid=(r // tr,),
            in_specs=[pick(0), pick(1), pick(2), pick(3)],
            out_specs=pl.BlockSpec((tr, cols), lambda i, s_ref: (i, 0))),
        out_shape=jax.ShapeDtypeStruct((r, cols), F32),
        compiler_params=_cp(("parallel",)),
    )(slots, own, recv, recv, recv)


def _reduce_scatter_start(grads, tag):
    cidx = lax.axis_index("c").astype(jnp.int32)[None]
    recv = _rs_swap_cores(grads, f"rs_swap_cores_{tag}")
    sums = [_rs_add_cores(g, rv, cidx, f"rs_add_cores_{tag}_{i}") for i, (g, rv) in enumerate(zip(grads, recv))]
    lands = [lax.empty(s.shape, s.dtype) for s in sums]
    return _split_start(sums, lands, 3 * len(sums), _rs_chip_copies, f"rs_chips_{tag}_start")


def _reduce_scatter_finish(started, after, tag):
    send_sems, recv_sems, sums, lands, _ = started
    sums, lands = _split_wait(send_sems, recv_sems, sums, lands, after, _rs_chip_copies, f"rs_chips_{tag}_wait")
    x, y = lax.axis_index("x"), lax.axis_index("y")
    slots = jnp.stack([2 * x + y, 2 * (1 - x) + y, 2 * x + 1 - y, 2 * (1 - x) + 1 - y]).astype(jnp.int32)
    return [_rs_add_chips(s, l, slots, f"rs_add_chips_{tag}_{i}") for i, (s, l) in enumerate(zip(sums, lands))]


def _allreduce_small(packs, name):
    n = len(packs)
    assert all(p.shape[0] % (8 * N_DEV) == 0 for p in packs)

    def body(*refs):
        p_refs, o_refs, part_refs = refs[:n], refs[n:2 * n], refs[2 * n:3 * n]
        send1, recv1, send2, recv2 = refs[3 * n:]
        x, y, c = _place()
        me = 4 * x + 2 * y + c

        def block(ref, d):
            rs = ref.shape[0] // N_DEV
            return ref.at[pl.ds(pl.multiple_of(d * rs, 8), rs), :]

        peers = [(1 - x if k & 4 else x, 1 - y if k & 2 else y, 1 - c if k & 1 else c) for k in range(1, N_DEV)]
        scatter = [pltpu.make_async_remote_copy(
            src_ref=block(p_refs[a], 4 * px + 2 * py + pc), dst_ref=part_refs[a].at[me],
            send_sem=send1.at[a, k], recv_sem=recv1.at[a, k], device_id=(px, py, pc), device_id_type=MESH)
            for a in range(n) for k, (px, py, pc) in enumerate(peers)]
        for cp in scatter:
            cp.start()
        for a in range(n):
            part_refs[a][me] = block(p_refs[a], me)[...]
        for cp in scatter:
            cp.wait()
        for a in range(n):
            acc = part_refs[a][0]
            for d in range(1, N_DEV):
                acc = acc + part_refs[a][d]
            block(o_refs[a], me)[...] = acc
        gather = [pltpu.make_async_remote_copy(
            src_ref=block(o_refs[a], me), dst_ref=block(o_refs[a], me), send_sem=send2.at[a, k], recv_sem=recv2.at[a, k],
            device_id=peer, device_id_type=MESH) for a in range(n) for k, peer in enumerate(peers)]
        for cp in gather:
            cp.start()
        for a in range(n):
            for k, (px, py, pc) in enumerate(peers):
                theirs = block(o_refs[a], 4 * px + 2 * py + pc)
                pltpu.make_async_remote_copy(
                    src_ref=theirs, dst_ref=theirs, send_sem=send2.at[a, k], recv_sem=recv2.at[a, k],
                    device_id=(px, py, pc), device_id_type=MESH).wait_recv()
        for cp in gather:
            cp.wait_send()

    sems = pltpu.SemaphoreType.DMA((n, N_DEV - 1))
    vmem = pl.BlockSpec(memory_space=pltpu.VMEM)
    return pl.pallas_call(
        body, name=name,
        in_specs=[vmem] * n, out_specs=[vmem] * n,
        out_shape=[jax.ShapeDtypeStruct(p.shape, F32) for p in packs],
        scratch_shapes=[pltpu.VMEM((N_DEV, p.shape[0] // N_DEV, p.shape[1]), F32) for p in packs] + [sems] * 4,
        compiler_params=pltpu.CompilerParams(vmem_limit_bytes=VMEM_LIMIT),
    )(*packs)


ADAM_TILE_BYTES = 2 * 1024 * 1024


def _adamw(w, g, m, v, name):
    shape = w.shape
    rows, cols = shape[-2:]
    lead = shape[:-2]
    nl = math.prod(lead)
    tr = max(t for t in range(8, rows + 1, 8) if rows % t == 0 and t * max(cols, 128) * 4 <= ADAM_TILE_BYTES) \
        if rows % 8 == 0 else rows
    c1 = 1.0 - ADAM_B1 ** ADAM_STEP
    c2 = 1.0 - ADAM_B2 ** ADAM_STEP

    def body(w_ref, g_ref, m_ref, v_ref, d_ref, nm_ref, nv_ref):
        gv = g_ref[...]
        nm = ADAM_B1 * m_ref[...] + (1.0 - ADAM_B1) * gv
        nv = ADAM_B2 * v_ref[...] + (1.0 - ADAM_B2) * jnp.square(gv)
        d_ref[...] = -ADAM_LR * ((nm / c1) / (jnp.sqrt(nv / c2) + ADAM_EPS) + ADAM_WD * w_ref[...])
        nm_ref[...] = nm
        nv_ref[...] = nv

    def index(b, i):
        return (*jnp.unravel_index(b, lead), i, 0) if lead else (i, 0)

    blk = pl.BlockSpec((*[None] * len(lead), tr, cols), index)
    sh = jax.ShapeDtypeStruct(shape, F32)
    return pl.pallas_call(
        body, name=name, grid=(nl, rows // tr), in_specs=[blk] * 4, out_specs=[blk] * 3, out_shape=[sh] * 3,
        compiler_params=_cp(("parallel", "parallel")),
    )(w, g, m, v)


WEIGHTS = ("norm_w", "w_in", "ssm_a_re", "ssm_a_im", "ssm_log_dt", "ssm_b_re", "ssm_b_im", "ssm_c_re", "ssm_c_im",
           "ssm_d", "ssm_glu_w", "ssm_glu_b", "sg_ln_w", "sg_ln_b", "sg_w", "sg_b", "attn_sinks",
           "w_branch_a", "w_branch_b", "w_branch_c", "w_out", "final_norm_w")
BIG = ("w_in", "ssm_glu_w", "w_branch_a", "w_branch_b", "w_branch_c", "w_out")
BIG_KEY = {"w_in": ("win_t", True), "ssm_glu_w": ("glu", False), "w_branch_a": ("wba_t", True),
           "w_branch_b": ("wbb_t", True), "w_branch_c": ("wbc_t", True), "w_out": ("wout", False)}
VIEWS = {"w_in": (1, 2), "ssm_b_re": (2, 3), "ssm_b_im": (2, 3)}
PACKS = (
    (64, (("ssm_a_re",), ("ssm_a_im",), ("ssm_c_re",), ("ssm_c_im",), ("ssm_b_re",), ("ssm_b_im",))),
    (128, (("sg_w",),)),
    (1024, (("ssm_d", "ssm_glu_b", "sg_ln_w", "sg_ln_b"), ("norm_w", "final_norm_w", "sg_b"), ("ssm_log_dt", "attn_sinks"))),
)
PACK_ROWS = 8 * N_DEV


def _view(n, a):
    return jnp.swapaxes(a, *VIEWS[n]) if n in VIEWS else a


def _group_rows(arrs, cols):
    return -(-sum(-(-a.size // cols) for a in arrs) // 8) * 8


def _pack(groups, cols):
    parts = []
    for arrs in groups:
        if len(arrs) == 1 and arrs[0].shape[-1] == cols and arrs[0].size % (8 * cols) == 0:
            parts.append(arrs[0].reshape(-1, cols))
            continue
        flat = [jnp.pad(a.reshape(-1), (0, -a.size % cols)) for a in arrs]
        flat = jnp.concatenate(flat) if len(flat) > 1 else flat[0]
        nrow = _group_rows(arrs, cols)
        parts.append(jnp.pad(flat, (0, nrow * cols - flat.shape[0])).reshape(nrow, cols))
    pad = -sum(p.shape[0] for p in parts) % PACK_ROWS
    if pad:
        parts.append(jnp.zeros((pad, cols), F32))
    return jnp.concatenate(parts, axis=0)


def _unpack(pack, groups):
    cols = pack.shape[1]
    out, row = [], 0
    for arrs in groups:
        nrow = _group_rows(arrs, cols)
        rows = pack[row:row + nrow]
        row += nrow
        if len(arrs) == 1 and arrs[0].shape[-1] == cols and arrs[0].size == nrow * cols:
            out.append(rows.reshape(arrs[0].shape))
            continue
        flat, off = rows.reshape(-1), 0
        for a in arrs:
            out.append(flat[off:off + a.size].reshape(a.shape))
            off += -(-a.size // cols) * cols
    return out


def kernel(x, norm_w, w_in, ssm_a_re, ssm_a_im, ssm_log_dt, ssm_b_re, ssm_b_im, ssm_c_re, ssm_c_im, ssm_d, ssm_glu_w, ssm_glu_b, sg_ln_w, sg_ln_b, sg_w, sg_b, attn_sinks, w_branch_a, w_branch_b, w_branch_c, w_out, final_norm_w, loss_target, m_norm_w, m_w_in, m_ssm_a_re, m_ssm_a_im, m_ssm_log_dt, m_ssm_b_re, m_ssm_b_im, m_ssm_c_re, m_ssm_c_im, m_ssm_d, m_ssm_glu_w, m_ssm_glu_b, m_sg_ln_w, m_sg_ln_b, m_sg_w, m_sg_b, m_attn_sinks, m_w_branch_a, m_w_branch_b, m_w_branch_c, m_w_out, m_final_norm_w, v_norm_w, v_w_in, v_ssm_a_re, v_ssm_a_im, v_ssm_log_dt, v_ssm_b_re, v_ssm_b_im, v_ssm_c_re, v_ssm_c_im, v_ssm_d, v_ssm_glu_w, v_ssm_glu_b, v_sg_ln_w, v_sg_ln_b, v_sg_w, v_sg_b, v_attn_sinks, v_w_branch_a, v_w_branch_b, v_w_branch_c, v_w_out, v_final_norm_w):
    w = dict(zip(WEIGHTS, (norm_w, w_in, ssm_a_re, ssm_a_im, ssm_log_dt, ssm_b_re, ssm_b_im, ssm_c_re, ssm_c_im, ssm_d, ssm_glu_w, ssm_glu_b, sg_ln_w, sg_ln_b, sg_w, sg_b, attn_sinks, w_branch_a, w_branch_b, w_branch_c, w_out, final_norm_w)))
    m = dict(zip(WEIGHTS, (m_norm_w, m_w_in, m_ssm_a_re, m_ssm_a_im, m_ssm_log_dt, m_ssm_b_re, m_ssm_b_im, m_ssm_c_re, m_ssm_c_im, m_ssm_d, m_ssm_glu_w, m_ssm_glu_b, m_sg_ln_w, m_sg_ln_b, m_sg_w, m_sg_b, m_attn_sinks, m_w_branch_a, m_w_branch_b, m_w_branch_c, m_w_out, m_final_norm_w)))
    v = dict(zip(WEIGHTS, (v_norm_w, v_w_in, v_ssm_a_re, v_ssm_a_im, v_ssm_log_dt, v_ssm_b_re, v_ssm_b_im, v_ssm_c_re, v_ssm_c_im, v_ssm_d, v_ssm_glu_w, v_ssm_glu_b, v_sg_ln_w, v_sg_ln_b, v_sg_w, v_sg_b, v_attn_sinks, v_w_branch_a, v_w_branch_b, v_w_branch_c, v_w_out, v_final_norm_w)))

    keys = [BIG_KEY[n][0] for n in BIG]
    wv, mv, vv = ({n: _view(n, a) for n, a in d.items()} for d in (w, m, v))
    shards = [[(wv[n][l] if n in VIEWS else w[n][l].T if BIG_KEY[n][1] else w[n][l]).astype(BF16) for n in BIG]
              for l in range(DEPTH)]
    small_p = [{n: w[n][l] for n in SMALL} for l in range(DEPTH)]
    xv, tgt = x[0], loss_target[0]
    tabs = _rope_tables(xv.shape[0])

    lands = [[_allgather_place(shards[l][:1]), _allgather_place(shards[l][1:])] for l in range(DEPTH)]
    s5 = [_s5_prep(small_p[l], f"l{l}") for l in range(DEPTH)]
    wmv_packs = {cols: [_pack([[d[n] for n in names] for names in groups], cols) for d in (wv, mv, vv)]
                 for cols, groups in PACKS}
    ag0a = _allgather_start(lands[0][0], "ag_l0_win")
    got = {}

    def win_of0(h):
        early = [h, *lands[0][1], *lands[1][0], *lands[1][1], *s5[0][1], *s5[1][1]]
        early += [p for ps in wmv_packs.values() for p in ps]
        got["win0"] = _allgather_finish(ag0a, early, "ag_l0_win")[0]
        got["ag0b"] = _allgather_start(lands[0][1], "ag_l0_rest", after=got["win0"])
        got["ag1a"] = _allgather_start(lands[1][0], "ag_l1_win", after=got["ag0b"][4])
        got["ag1b"] = _allgather_start(lands[1][1], "ag_l1_rest", after=got["ag1a"][4])
        return got["win0"], got["ag1b"][4]

    def after_proj0(proj):
        got["w0"] = dict(zip(keys, [got["win0"]] + _allgather_finish(got["ag0b"], proj, "ag_l0_rest")))
        return got["w0"]

    x1, saved0 = _layer_fwd(xv, small_p[0], None, tabs, "l0", s5=s5[0], win_of=win_of0, after_proj=after_proj0)
    big_w0 = got["w0"]
    win1 = _allgather_finish(got["ag1a"], x1, "ag_l1_win")[0]

    def after_proj1(proj):
        got["w1"] = dict(zip(keys, [win1] + _allgather_finish(got["ag1b"], proj, "ag_l1_rest")))
        return got["w1"]

    x2, saved1 = _layer_fwd(x1, small_p[1], {"win_t": win1}, tabs, "l1", s5=s5[1], after_proj=after_proj1)
    big_w1 = got["w1"]
    loss_acc, dx2, dfw = _final(x2, w["final_norm_w"][None], tgt, "final_norm_loss")
    loss = lax.psum(loss_acc[0, 0], ("x", "y", "c"))
    dfw = dfw[0]

    dx1, big_g1, small_g1 = _layer_bwd(dx2, small_p[1], big_w1, tabs, saved1, "l1")
    rs1 = _reduce_scatter_start([big_g1[k] for k in keys], "l1")

    def before_win0(big):
        got["rs0b"] = _reduce_scatter_start([big[k] for k in keys[1:]], "l0_rest")
        return got["rs0b"][4]

    def after_win0(big):
        got["rs0a"] = _reduce_scatter_start([big["win_t"]], "l0_win")
        return got["rs0a"][4]

    dx, big_g0, small_g0 = _layer_bwd(dx1, small_p[0], big_w0, tabs, saved0, "l0", first_after=rs1[4],
                                      before_win=before_win0, after_win=after_win0)
    red1 = _reduce_scatter_finish(rs1, dx, "l1")
    small_g = [small_g0, small_g1]

    grads, delta, new_m, new_v = {}, {}, {}, {}

    def small_grad(n):
        if n == "final_norm_w":
            return dfw
        if n in ("ssm_b_re", "ssm_b_im"):
            return jnp.stack([small_g[l][n.replace("ssm_b_", "ssm_bt_")].transpose(1, 0, 2) for l in range(DEPTH)])
        return jnp.stack([small_g[l][n] for l in range(DEPTH)])

    g_groups = [[[small_grad(n) for n in names] for names in groups] for _, groups in PACKS]
    reduced = _allreduce_small([_pack(gg, cols) for gg, (cols, _) in zip(g_groups, PACKS)], "allreduce_small")
    last = None
    for (cols, groups), gg, red in zip(PACKS, g_groups, reduced):
        names = [n for names in groups for n in names]
        grads.update(zip(names, _unpack(red, gg)))
        wp, mp, vp = wmv_packs[cols]
        outs = _adamw(wp, red, mp, vp, f"adamw_pack{cols}")
        last = outs[0]
        for res, o in zip((delta, new_m, new_v), outs):
            res.update(zip(names, _unpack(o, [[wv[n] for n in names] for names in groups])))

    red0 = (_reduce_scatter_finish(got["rs0a"], last, "l0_win")
            + _reduce_scatter_finish(got["rs0b"], last, "l0_rest"))
    for i, n in enumerate(BIG):
        back = BIG_KEY[n][1] and n not in VIEWS
        grads[n] = jnp.stack([g.T if back else g for g in (red0[i], red1[i])])
    for n in BIG:
        delta[n], new_m[n], new_v[n] = _adamw(wv[n], grads[n], mv[n], vv[n], f"adamw_{n}")

    return (loss, dx[None], *[_view(n, d[n]) for d in (grads, delta, new_m, new_v) for n in WEIGHTS])
```

```python
import functools
import math

import jax
import jax.numpy as jnp
from jax import lax
from jax.experimental import pallas as pl
from jax.experimental.pallas import tpu as pltpu

F32 = jnp.float32
BF16 = jnp.bfloat16

D_MODEL = 2048
DEPTH = 2
EPS = 1e-6
NEG_INF = -1e30
N_DEV = 8

SSM_WIDTH = 1024
SSM_GROUP = 16
SSM_GROUPS = 64
SSM_STATE = 64
N_SLAB = 8
SLAB_CH = 128
SLAB_ST = 512
SUB = 8
N_GRP = 2
N_SEG = SUB * N_GRP

SG_HEADS = 8
CHUNK = 128
HEAD_DIM = 64
ATT_HEADS = 16
ROT_DIM = 16
ROPE_THETA = 500000.0

D_IN = 13568
OFF_UA, OFF_ZA, OFF_UB, OFF_VB, OFF_ZB, OFF_Q, OFF_KV, OFF_ZC, OFF_G = (
    0, 1024, 2048, 3072, 4096, 5120, 6144, 6400, 7424)

ADAM_LR, ADAM_B1, ADAM_B2, ADAM_EPS, ADAM_WD, ADAM_STEP = 0.001, 0.9, 0.999, 1e-08, 0.01, 10

VMEM_LIMIT = 56 * 1024 * 1024


def _cp(sem=None):
    return pltpu.CompilerParams(dimension_semantics=sem, vmem_limit_bytes=VMEM_LIMIT)


def _dot(a, b):
    return jnp.dot(a, b, preferred_element_type=F32)


def _dot_nt(a, b):
    return lax.dot_general(a, b, (((1,), (1,)), ((), ())), preferred_element_type=F32)


def _dot_tn(a, b):
    return lax.dot_general(a, b, (((0,), (0,)), ((), ())), preferred_element_type=F32)


def _mm(a, b, mode, out_dtype, tm, tn, tk, name, res=None, after=None):
    if mode == "nn":
        (m, k), (_, n) = a.shape, b.shape
    elif mode == "nt":
        (m, k), (n, _) = a.shape, b.shape
    else:
        (k, m), (_, n) = a.shape, b.shape
    tm, tn, tk = min(tm, m), min(tn, n), min(tk, k)
    assert m % tm == 0 and n % tn == 0 and k % tk == 0, (name, m, n, k, tm, tn, tk)
    nk = k // tk
    a_spec = {"nn": pl.BlockSpec((tm, tk), lambda i, j, kk: (i, kk)),
              "nt": pl.BlockSpec((tm, tk), lambda i, j, kk: (i, kk)),
              "tn": pl.BlockSpec((tk, tm), lambda i, j, kk: (kk, i))}[mode]
    b_spec = {"nn": pl.BlockSpec((tk, tn), lambda i, j, kk: (kk, j)),
              "nt": pl.BlockSpec((tn, tk), lambda i, j, kk: (j, kk)),
              "tn": pl.BlockSpec((tk, tn), lambda i, j, kk: (kk, j))}[mode]
    dot = {"nn": _dot, "nt": _dot_nt, "tn": _dot_tn}[mode]
    has_res = res is not None

    def body(*refs):
        if after is not None:
            refs = refs[:-3] + refs[-2:]
        if has_res:
            a_ref, b_ref, r_ref, o_ref, acc = refs
        else:
            a_ref, b_ref, o_ref, acc = refs
        kk = pl.program_id(2)

        @pl.when(kk == 0)
        def _():
            acc[...] = jnp.zeros_like(acc)

        acc[...] += dot(a_ref[...].astype(BF16), b_ref[...].astype(BF16))

        @pl.when(kk == nk - 1)
        def _():
            r = acc[...]
            if has_res:
                r = r + r_ref[...]
            o_ref[...] = r.astype(out_dtype)

    in_specs = [a_spec, b_spec]
    args = [a, b]
    if has_res:
        in_specs.append(pl.BlockSpec((tm, tn), lambda i, j, kk: (i, j)))
        args.append(res)
    if after is not None:
        in_specs.append(pl.BlockSpec(memory_space=pl.ANY))
        args.append(after)
    return pl.pallas_call(
        body, name=name,
        grid=(m // tm, n // tn, nk),
        in_specs=in_specs,
        out_specs=pl.BlockSpec((tm, tn), lambda i, j, kk: (i, j)),
        out_shape=jax.ShapeDtypeStruct((m, n), out_dtype),
        scratch_shapes=[pltpu.VMEM((tm, tn), F32)],
        compiler_params=_cp(("parallel", "parallel", "arbitrary")),
    )(*args)


def _rms(x, w):
    return x * lax.rsqrt(jnp.mean(x * x, axis=-1, keepdims=True) + EPS) * w


def _rms_fwd(x, w, name):
    L, D = x.shape
    tm = min(L, 256)

    def body(x_ref, w_ref, h_ref):
        h_ref[...] = _rms(x_ref[...], w_ref[...]).astype(BF16)

    return pl.pallas_call(
        body, name=name, grid=(L // tm,),
        in_specs=[pl.BlockSpec((tm, D), lambda i: (i, 0)), pl.BlockSpec((1, D), lambda i: (0, 0))],
        out_specs=pl.BlockSpec((tm, D), lambda i: (i, 0)),
        out_shape=jax.ShapeDtypeStruct((L, D), BF16),
        compiler_params=_cp(("parallel",)),
    )(x, w)


def _rms_bwd(x, w, dh, dres, name):
    L, D = x.shape
    tm = min(L, 256)

    def body(x_ref, w_ref, dh_ref, dres_ref, dx_ref, dw_ref):
        _, vjp = jax.vjp(_rms, x_ref[...], w_ref[...])
        dx, dw = vjp(dh_ref[...])
        dx_ref[...] = dx + dres_ref[...]

        @pl.when(pl.program_id(0) == 0)
        def _():
            dw_ref[...] = jnp.zeros_like(dw_ref)

        dw_ref[...] += dw

    row = pl.BlockSpec((tm, D), lambda i: (i, 0))
    vec = pl.BlockSpec((1, D), lambda i: (0, 0))
    return pl.pallas_call(
        body, name=name, grid=(L // tm,),
        in_specs=[row, vec, row, row],
        out_specs=[row, vec],
        out_shape=[jax.ShapeDtypeStruct((L, D), F32), jax.ShapeDtypeStruct((1, D), F32)],
        compiler_params=_cp(("arbitrary",)),
    )(x, w, dh, dres)


def _final(x, fw, tgt, name):
    L, D = x.shape
    tm = min(L, 256)

    def loss_fn(xv, wv, tv):
        err = _rms(xv, wv) - tv
        return jnp.sum(err * err) * (0.5 / D)

    def body(x_ref, w_ref, t_ref, loss_ref, dx_ref, dw_ref):
        tv = t_ref[...]
        val, vjp = jax.vjp(lambda a, b: loss_fn(a, b, tv), x_ref[...], w_ref[...])
        dx, dw = vjp(jnp.ones((), F32))
        dx_ref[...] = dx

        @pl.when(pl.program_id(0) == 0)
        def _():
            dw_ref[...] = jnp.zeros_like(dw_ref)
            loss_ref[...] = jnp.zeros_like(loss_ref)

        dw_ref[...] += dw
        loss_ref[...] += jnp.full(loss_ref.shape, val, F32)

    row = pl.BlockSpec((tm, D), lambda i: (i, 0))
    vec = pl.BlockSpec((1, D), lambda i: (0, 0))
    return pl.pallas_call(
        body, name=name, grid=(L // tm,),
        in_specs=[row, vec, row],
        out_specs=[pl.BlockSpec((8, 128), lambda i: (0, 0)), row, vec],
        out_shape=[jax.ShapeDtypeStruct((8, 128), F32), jax.ShapeDtypeStruct((L, D), F32),
                   jax.ShapeDtypeStruct((1, D), F32)],
        compiler_params=_cp(("arbitrary",)),
    )(x, fw, tgt)


def _s5_param_fn(a_re, a_im, log_dt, bt_re, bt_im):
    dt = jnp.exp(log_dt)
    zr, zi = a_re * dt, a_im * dt
    er = jnp.exp(zr)
    lr, li = er * jnp.cos(zi), er * jnp.sin(zi)
    nr, ni = lr - 1.0, li
    den = a_re * a_re + a_im * a_im
    cr = (nr * a_re + ni * a_im) / den
    ci = (ni * a_re - nr * a_im) / den
    bbr = cr[None] * bt_re - ci[None] * bt_im
    bbi = cr[None] * bt_im + ci[None] * bt_re
    return lr, li, bbr, bbi


def _s5_params_fwd(a_re, a_im, log_dt, bt_re, bt_im, name):
    def body(ar, ai, ld, br, bi, lr, li, bbr, bbi):
        o = _s5_param_fn(ar[...], ai[...], ld[...], br[...], bi[...])
        lr[...], li[...], bbr[...], bbi[...] = o

    gp = jax.ShapeDtypeStruct(a_re.shape, F32)
    cgp = jax.ShapeDtypeStruct(bt_re.shape, F32)
    return pl.pallas_call(body, name=name, out_shape=[gp, gp, cgp, cgp])(a_re, a_im, log_dt, bt_re, bt_im)


def _s5_params_bwd(a_re, a_im, log_dt, bt_re, bt_im, dlr, dli, dbbr, dbbi, name):
    def body(ar, ai, ld, br, bi, g0, g1, g2, g3, o0, o1, o2, o3, o4):
        _, vjp = jax.vjp(_s5_param_fn, ar[...], ai[...], ld[...], br[...], bi[...])
        o0[...], o1[...], o2[...], o3[...], o4[...] = vjp((g0[...], g1[...], g2[...], g3[...]))

    gp = jax.ShapeDtypeStruct(a_re.shape, F32)
    cgp = jax.ShapeDtypeStruct(bt_re.shape, F32)
    return pl.pallas_call(body, name=name,
                          out_shape=[gp, gp, jax.ShapeDtypeStruct(log_dt.shape, F32), cgp, cgp])(
        a_re, a_im, log_dt, bt_re, bt_im, dlr, dli, dbbr, dbbi)


def _cmul(ar, ai, br, bi):
    return ar * br - ai * bi, ar * bi + ai * br


def _cpow(lr, li, n):
    rr, ri = None, None
    br, bi = lr, li
    while n:
        if n & 1:
            rr, ri = (br, bi) if rr is None else _cmul(rr, ri, br, bi)
        n >>= 1
        if n:
            br, bi = _cmul(br, bi, br, bi)
    return rr, ri


def _shift_rows(x, up):
    row = lax.broadcasted_iota(jnp.int32, x.shape, 0)
    if up:
        return jnp.where(row == SUB - 1, 0.0, pltpu.roll(x, SUB - 1, 0))
    return jnp.where(row == 0, 0.0, pltpu.roll(x, 1, 0))


NT = SLAB_ST // 128


def _lam_tiles(lr_ref, li_ref):
    return [(lr_ref[:, j * 128:(j + 1) * 128], li_ref[:, j * 128:(j + 1) * 128]) for j in range(NT)]


def _row_on_sublanes(ref, j, t):
    return ref[j, pl.ds(t, SUB, stride=0), :]


def _pow_table(pw_re, pw_im, lam_t, seg):
    assert seg % 8 == 0 and (seg // 8) & (seg // 8 - 1) == 0
    for j in range(NT):
        lr, li = lam_t[j][0][0:1], lam_t[j][1][0:1]
        r, i_ = lr, li
        for row in range(8):
            pw_re[j, row:row + 1, :] = r
            pw_im[j, row:row + 1, :] = i_
            if row < 7:
                r, i_ = _cmul(r, i_, lr, li)
        n = 8
        while n < seg:
            qr, qi = _cpow(lr, li, n)
            nr, ni = _cmul(pw_re[j, 0:n, :], pw_im[j, 0:n, :], qr, qi)
            pw_re[j, n:2 * n, :] = nr
            pw_im[j, n:2 * n, :] = ni
            n *= 2


def _seg_scan(s_re, s_im, lam_t, pw_re, pw_im, seg, reverse, prev=None):
    sgn = -1.0 if reverse else 1.0
    lt = [(lr, sgn * li) for lr, li in lam_t]
    tiles = [(g, j) for g in range(N_GRP) for j in range(NT)]
    zeros = jnp.zeros((SUB, 128), F32)

    def rows(g, i):
        return pl.ds(pl.multiple_of((g * seg + i) * SUB, SUB), SUB)

    def step1(t, carry):
        i = seg - 1 - t if reverse else t
        out = []
        for n, (g, j) in enumerate(tiles):
            nr, ni = _cmul(lt[j][0], lt[j][1], carry[2 * n], carry[2 * n + 1])
            nr = nr + s_re[j, rows(g, i), :]
            ni = ni + s_im[j, rows(g, i), :]
            s_re[j, rows(g, i), :] = nr
            s_im[j, rows(g, i), :] = ni
            out += [nr, ni]
        return tuple(out)

    zero = tuple(zeros for _ in range(2 * len(tiles)))
    ends = lax.fori_loop(0, seg, step1, zero)

    carries = [None] * (2 * len(tiles))
    row = lax.broadcasted_iota(jnp.int32, (SUB, 128), 0)
    dist = (SUB - 1 - row) if reverse else row
    edge = 0 if reverse else SUB - 1
    for j in range(NT):
        pr, pi = _cpow(lt[j][0], lt[j][1], seg)
        qr, qi = jnp.ones((SUB, 128), F32), zeros
        for s in range(1, SUB):
            tr, ti = _cmul(qr, qi, pr, pi)
            qr, qi = jnp.where(dist >= s, tr, qr), jnp.where(dist >= s, ti, qi)
        boundary = None
        for g in (reversed(range(N_GRP)) if reverse else range(N_GRP)):
            n = g * NT + j
            cr, ci = zeros, zeros
            for _ in range(SUB - 1):
                tr, ti = _cmul(pr, pi, cr, ci)
                cr = _shift_rows(tr + ends[2 * n], reverse)
                ci = _shift_rows(ti + ends[2 * n + 1], reverse)
            if boundary is not None:
                tr, ti = _cmul(qr, qi, boundary[0], boundary[1])
                cr, ci = cr + tr, ci + ti
            carries[2 * n], carries[2 * n + 1] = cr, ci
            fr, fi = _cmul(pr, pi, cr, ci)
            boundary = (jnp.broadcast_to((fr + ends[2 * n])[edge:edge + 1], (SUB, 128)),
                        jnp.broadcast_to((fi + ends[2 * n + 1])[edge:edge + 1], (SUB, 128)))

    def fix(t, i, acc, before):
        out = []
        pws = [(_row_on_sublanes(pw_re, j, t), sgn * _row_on_sublanes(pw_im, j, t)) for j in range(NT)]
        for n, (g, j) in enumerate(tiles):
            ar, ai = _cmul(pws[j][0], pws[j][1], carries[2 * n], carries[2 * n + 1])
            ar = ar + s_re[j, rows(g, i), :]
            ai = ai + s_im[j, rows(g, i), :]
            s_re[j, rows(g, i), :] = ar
            s_im[j, rows(g, i), :] = ai
            if before is not None:
                qr, qi = before(n)
                out += [acc[2 * n] + ar * qr + ai * qi, acc[2 * n + 1] + ai * qr - ar * qi]
        return tuple(out)

    if prev is None:
        lax.fori_loop(0, seg, lambda t, c: fix(t, seg - 1 - t if reverse else t, c, None), ())
        return carries
    assert reverse
    p_re, p_im, p_carries = prev

    def earlier(t):
        return lambda n: (p_re[tiles[n][1], rows(tiles[n][0], seg - 2 - t), :],
                          p_im[tiles[n][1], rows(tiles[n][0], seg - 2 - t), :])

    acc = lax.fori_loop(0, seg - 1, lambda t, c: fix(t, seg - 1 - t, c, earlier(t)), zero)
    acc = fix(seg - 1, 0, acc, lambda n: (p_carries[2 * n], p_carries[2 * n + 1]))
    return carries, [sum(acc[2 * (g * NT + j) + part] for g in range(N_GRP)) for j in range(NT) for part in range(2)]


def _seg_slice(k, seg):
    g, r = divmod(k, SUB)
    return pl.ds(g * seg * SUB + r, seg, stride=SUB)


def _seg_rows(ref, k, seg):
    return jnp.concatenate([ref[j, _seg_slice(k, seg), :] for j in range(NT)], axis=-1)


def _seg_store(ref, k, seg, val):
    for j in range(NT):
        ref[j, _seg_slice(k, seg), :] = val[:, j * 128:(j + 1) * 128]


def _s5_specs(L):
    col = lambda off: pl.BlockSpec((L, SLAB_CH), lambda j: (0, off + j))
    mat_b = pl.BlockSpec((None, SLAB_CH, SLAB_ST), lambda j: (j, 0, 0))
    mat_c = pl.BlockSpec((None, SLAB_ST, SLAB_CH), lambda j: (j, 0, 0))
    vec_s = pl.BlockSpec((None, SUB, SLAB_ST), lambda j: (j, 0, 0))
    vec_c = pl.BlockSpec((None, 1, SLAB_CH), lambda j: (j, 0, 0))
    return col, mat_b, mat_c, vec_s, vec_c


def _s5_states(u_ref, bre_ref, bim_ref, lam_t, pw_re, pw_im, s_re, s_im, seg):
    _pow_table(pw_re, pw_im, lam_t, seg)
    for k in range(N_SEG):
        uk = u_ref[pl.ds(k * seg, seg), :]
        _seg_store(s_re, k, seg, _dot(uk, bre_ref[...]))
        _seg_store(s_im, k, seg, _dot(uk, bim_ref[...]))
    return _seg_scan(s_re, s_im, lam_t, pw_re, pw_im, seg, reverse=False)


def _s5_fwd(proj, bre, bim, cre_t, cim_t, lam_re, lam_im, dvec, name):
    L = proj.shape[0]
    seg = L // N_SEG
    col, mat_b, mat_c, vec_s, vec_c = _s5_specs(L)
    rows = N_SEG * seg

    def body(u_ref, bre_ref, bim_ref, cre_ref, cim_ref, lr_ref, li_ref, d_ref, y_ref, s_re, s_im, pw_re, pw_im):
        _s5_states(u_ref, bre_ref, bim_ref, _lam_tiles(lr_ref, li_ref), pw_re, pw_im, s_re, s_im, seg)
        for k in range(N_SEG):
            y = (_dot(_seg_rows(s_re, k, seg).astype(BF16), cre_ref[...])
                 - _dot(_seg_rows(s_im, k, seg).astype(BF16), cim_ref[...]))
            y = y + d_ref[...] * u_ref[pl.ds(k * seg, seg), :].astype(F32)
            y_ref[pl.ds(k * seg, seg), :] = jax.nn.gelu(y).astype(BF16)

    return pl.pallas_call(
        body, name=name, grid=(N_SLAB,),
        in_specs=[col(OFF_UA // SLAB_CH), mat_b, mat_b, mat_c, mat_c, vec_s, vec_s, vec_c],
        out_specs=pl.BlockSpec((L, SLAB_CH), lambda j: (0, j)),
        out_shape=jax.ShapeDtypeStruct((L, SSM_WIDTH), BF16),
        scratch_shapes=[pltpu.VMEM((NT, rows, 128), F32)] * 2 + [pltpu.VMEM((NT, seg, 128), F32)] * 2,
        compiler_params=_cp(("parallel",)),
    )(proj, bre, bim, cre_t, cim_t, lam_re, lam_im, dvec)


def _s5_bwd(proj, dy, bre, bim, cre_t, cim_t, lam_re, lam_im, dvec, name):
    L = proj.shape[0]
    seg = L // N_SEG
    col, mat_b, mat_c, vec_s, vec_c = _s5_specs(L)
    rows = N_SEG * seg
    dlam_spec = pl.BlockSpec((None, 1, SLAB_ST), lambda j: (j, 0, 0))

    def body(u_ref, dy_ref, bre_ref, bim_ref, cre_ref, cim_ref, lr_ref, li_ref, d_ref,
             du_ref, dbre_ref, dbim_ref, dcre_ref, dcim_ref, dlr_ref, dli_ref, dd_ref,
             s_re, s_im, a_re, a_im, pw_re, pw_im, dyp):
        lam_t = _lam_tiles(lr_ref, li_ref)
        carry_s = _s5_states(u_ref, bre_ref, bim_ref, lam_t, pw_re, pw_im, s_re, s_im, seg)
        dcre = jnp.zeros((SLAB_ST, SLAB_CH), F32)
        dcim = jnp.zeros((SLAB_ST, SLAB_CH), F32)
        dd = jnp.zeros((1, SLAB_CH), F32)
        for k in range(N_SEG):
            sre = _seg_rows(s_re, k, seg).astype(BF16)
            sim = _seg_rows(s_im, k, seg).astype(BF16)
            uk = u_ref[pl.ds(k * seg, seg), :].astype(F32)
            ypre = _dot(sre, cre_ref[...]) - _dot(sim, cim_ref[...]) + d_ref[...] * uk
            _, vjp = jax.vjp(jax.nn.gelu, ypre)
            (dyk,) = vjp(dy_ref[pl.ds(k * seg, seg), :].astype(F32))
            dyp[pl.ds(k * seg, seg), :] = dyk
            dd = dd + jnp.sum(dyk * uk, axis=0, keepdims=True)
            dyb = dyk.astype(BF16)
            dcre = dcre + _dot_tn(sre, dyb)
            dcim = dcim - _dot_tn(sim, dyb)
            _seg_store(a_re, k, seg, _dot_nt(dyb, cre_ref[...]))
            _seg_store(a_im, k, seg, -_dot_nt(dyb, cim_ref[...]))
        dcre_ref[...] = dcre
        dcim_ref[...] = dcim
        dd_ref[...] = dd

        _, acc = _seg_scan(a_re, a_im, lam_t, pw_re, pw_im, seg, reverse=True, prev=(s_re, s_im, carry_s))
        dlr_ref[...] = jnp.concatenate([jnp.sum(acc[2 * j], axis=0, keepdims=True) for j in range(NT)], axis=-1)
        dli_ref[...] = jnp.concatenate([jnp.sum(acc[2 * j + 1], axis=0, keepdims=True) for j in range(NT)], axis=-1)

        dbre = jnp.zeros((SLAB_CH, SLAB_ST), F32)
        dbim = jnp.zeros((SLAB_CH, SLAB_ST), F32)
        for k in range(N_SEG):
            are = _seg_rows(a_re, k, seg).astype(BF16)
            aim = _seg_rows(a_im, k, seg).astype(BF16)
            uk = u_ref[pl.ds(k * seg, seg), :]
            du = _dot_nt(are, bre_ref[...]) + _dot_nt(aim, bim_ref[...]) + dyp[pl.ds(k * seg, seg), :] * d_ref[...]
            du_ref[pl.ds(k * seg, seg), :] = du.astype(BF16)
            dbre = dbre + _dot_tn(uk, are)
            dbim = dbim + _dot_tn(uk, aim)
        dbre_ref[...] = dbre
        dbim_ref[...] = dbim

    scan_buf = pltpu.VMEM((NT, rows, 128), F32)
    pow_buf = pltpu.VMEM((NT, seg, 128), F32)
    return pl.pallas_call(
        body, name=name, grid=(N_SLAB,),
        in_specs=[col(OFF_UA // SLAB_CH), pl.BlockSpec((L, SLAB_CH), lambda j: (0, j)),
                  mat_b, mat_b, mat_c, mat_c, vec_s, vec_s, vec_c],
        out_specs=[pl.BlockSpec((L, SLAB_CH), lambda j: (0, j)), mat_b, mat_b, mat_c, mat_c, dlam_spec, dlam_spec, vec_c],
        out_shape=[jax.ShapeDtypeStruct((L, SSM_WIDTH), BF16),
                   jax.ShapeDtypeStruct((N_SLAB, SLAB_CH, SLAB_ST), F32),
                   jax.ShapeDtypeStruct((N_SLAB, SLAB_CH, SLAB_ST), F32),
                   jax.ShapeDtypeStruct((N_SLAB, SLAB_ST, SLAB_CH), F32),
                   jax.ShapeDtypeStruct((N_SLAB, SLAB_ST, SLAB_CH), F32),
                   jax.ShapeDtypeStruct((N_SLAB, 1, SLAB_ST), F32),
                   jax.ShapeDtypeStruct((N_SLAB, 1, SLAB_ST), F32),
                   jax.ShapeDtypeStruct((N_SLAB, 1, SLAB_CH), F32)],
        scratch_shapes=[scan_buf, scan_buf, scan_buf, scan_buf, pow_buf, pow_buf, pltpu.VMEM((L, SLAB_CH), F32)],
        compiler_params=_cp(("parallel",)),
    )(proj, dy, bre, bim, cre_t, cim_t, lam_re, lam_im, dvec)


def _glu_point(y0, pre, za, b):
    return y0 * jax.nn.sigmoid(pre + b) * jax.nn.silu(za)


def _glu_specs(L, tm):
    row = pl.BlockSpec((tm, SSM_WIDTH), lambda i: (i, 0))
    za = pl.BlockSpec((tm, SSM_WIDTH), lambda i: (i, OFF_ZA // SSM_WIDTH))
    wmat = pl.BlockSpec((SSM_WIDTH, SSM_WIDTH), lambda i: (0, 0))
    vec = pl.BlockSpec((1, SSM_WIDTH), lambda i: (0, 0))
    return row, za, wmat, vec


def _glu_fwd(ya0, proj, w, b, name):
    L = ya0.shape[0]
    tm = min(L, 512)
    row, za, wmat, vec = _glu_specs(L, tm)

    def body(y_ref, z_ref, w_ref, b_ref, o_ref):
        y0 = y_ref[...]
        pre = _dot(y0, w_ref[...])
        o_ref[...] = _glu_point(y0.astype(F32), pre, z_ref[...].astype(F32), b_ref[...]).astype(BF16)

    return pl.pallas_call(
        body, name=name, grid=(L // tm,), in_specs=[row, za, wmat, vec], out_specs=row,
        out_shape=jax.ShapeDtypeStruct((L, SSM_WIDTH), BF16), compiler_params=_cp(("parallel",)),
    )(ya0, proj, w, b)


def _glu_bwd(ya0, proj, w, b, dya, name):
    L = ya0.shape[0]
    tm = min(L, 512)
    row, za, wmat, vec = _glu_specs(L, tm)

    def body(y_ref, z_ref, w_ref, b_ref, g_ref, dy0_ref, dza_ref, dw_ref, db_ref):
        y0 = y_ref[...]
        pre = _dot(y0, w_ref[...])
        _, vjp = jax.vjp(_glu_point, y0.astype(F32), pre, z_ref[...].astype(F32), b_ref[...])
        dy0, dpre, dza, db = vjp(g_ref[...].astype(F32))
        dpb = dpre.astype(BF16)
        dy0_ref[...] = (dy0 + _dot_nt(dpb, w_ref[...])).astype(BF16)
        dza_ref[...] = dza.astype(BF16)

        @pl.when(pl.program_id(0) == 0)
        def _():
            dw_ref[...] = jnp.zeros_like(dw_ref)
            db_ref[...] = jnp.zeros_like(db_ref)

        dw_ref[...] += _dot_tn(y0, dpb)
        db_ref[...] += db

    return pl.pallas_call(
        body, name=name, grid=(L // tm,), in_specs=[row, za, wmat, vec, row],
        out_specs=[row, row, wmat, vec],
        out_shape=[jax.ShapeDtypeStruct((L, SSM_WIDTH), BF16), jax.ShapeDtypeStruct((L, SSM_WIDTH), BF16),
                   jax.ShapeDtypeStruct((SSM_WIDTH, SSM_WIDTH), F32), jax.ShapeDtypeStruct((1, SSM_WIDTH), F32)],
        compiler_params=_cp(("arbitrary",)),
    )(ya0, proj, w, b, dya)


def _sg_norm(vb, ln_w, ln_b):
    v0 = jax.nn.gelu(vb)
    mu = jnp.mean(v0, axis=-1, keepdims=True)
    var = jnp.mean(jnp.square(v0 - mu), axis=-1, keepdims=True)
    return (v0 - mu) * lax.rsqrt(var + EPS) * ln_w + ln_b


def _sg_gate(ub, mixed, zb):
    return jax.nn.gelu(ub) * mixed * jax.nn.silu(zb)


def _sg_specs():
    W = SSM_WIDTH
    blk = lambda off: pl.BlockSpec((CHUNK, W), lambda n: (n, off // W))
    out = pl.BlockSpec((CHUNK, W), lambda n: (n, 0))
    vec = pl.BlockSpec((1, W), lambda n: (0, 0))
    wsp = pl.BlockSpec((SG_HEADS, CHUNK, CHUNK), lambda n: (0, 0, 0))
    bsp = pl.BlockSpec((SG_HEADS, CHUNK, 1), lambda n: (0, 0, 0))
    return blk, out, vec, wsp, bsp


def _sg_masked(w_ref):
    t = lax.broadcasted_iota(jnp.int32, (CHUNK, CHUNK), 0)
    s = lax.broadcasted_iota(jnp.int32, (CHUNK, CHUNK), 1)
    causal = s <= t
    return causal, [jnp.where(causal, w_ref[h], 0.0).astype(BF16) for h in range(SG_HEADS)]


def _sg_mix(wm, vnb, bias_ref):
    return jnp.concatenate(
        [_dot(wm[h], vnb[:, h * CHUNK:(h + 1) * CHUNK]) + bias_ref[h] for h in range(SG_HEADS)], axis=-1)


def _sg_fwd(proj, ln_w, ln_b, w, bias, name):
    L = proj.shape[0]
    blk, out, vec, wsp, bsp = _sg_specs()

    def body(ub_ref, vb_ref, zb_ref, lw_ref, lb_ref, w_ref, bias_ref, o_ref):
        _, wm = _sg_masked(w_ref)
        vnb = _sg_norm(vb_ref[...].astype(F32), lw_ref[...], lb_ref[...]).astype(BF16)
        mixed = _sg_mix(wm, vnb, bias_ref)
        o_ref[...] = _sg_gate(ub_ref[...].astype(F32), mixed, zb_ref[...].astype(F32)).astype(BF16)

    return pl.pallas_call(
        body, name=name, grid=(L // CHUNK,),
        in_specs=[blk(OFF_UB), blk(OFF_VB), blk(OFF_ZB), vec, vec, wsp, bsp], out_specs=out,
        out_shape=jax.ShapeDtypeStruct((L, SSM_WIDTH), BF16), compiler_params=_cp(("parallel",)),
    )(proj, proj, proj, ln_w, ln_b, w, bias)


def _sg_bwd(proj, ln_w, ln_b, w, bias, dyb, name):
    L = proj.shape[0]
    blk, out, vec, wsp, bsp = _sg_specs()

    def body(ub_ref, vb_ref, zb_ref, lw_ref, lb_ref, w_ref, bias_ref, g_ref,
             dub_ref, dvb_ref, dzb_ref, dlw_ref, dlb_ref, dw_ref, dbias_ref):
        causal, wm = _sg_masked(w_ref)
        vb = vb_ref[...].astype(F32)
        vn, vjp_norm = jax.vjp(_sg_norm, vb, lw_ref[...], lb_ref[...])
        vnb = vn.astype(BF16)
        mixed = _sg_mix(wm, vnb, bias_ref)
        _, vjp_gate = jax.vjp(_sg_gate, ub_ref[...].astype(F32), mixed, zb_ref[...].astype(F32))
        dub, dmixed, dzb = vjp_gate(g_ref[...].astype(F32))
        dub_ref[...] = dub.astype(BF16)
        dzb_ref[...] = dzb.astype(BF16)

        @pl.when(pl.program_id(0) == 0)
        def _():
            dlw_ref[...] = jnp.zeros_like(dlw_ref)
            dlb_ref[...] = jnp.zeros_like(dlb_ref)
            dw_ref[...] = jnp.zeros_like(dw_ref)
            dbias_ref[...] = jnp.zeros_like(dbias_ref)

        dvn = []
        for h in range(SG_HEADS):
            dm = dmixed[:, h * CHUNK:(h + 1) * CHUNK]
            dmb = dm.astype(BF16)
            dbias_ref[h] += jnp.sum(dm, axis=-1, keepdims=True)
            dw_ref[h] += jnp.where(causal, _dot_nt(dmb, vnb[:, h * CHUNK:(h + 1) * CHUNK]), 0.0)
            dvn.append(_dot_tn(wm[h], dmb))
        dvb, dlw, dlb = vjp_norm(jnp.concatenate(dvn, axis=-1))
        dvb_ref[...] = dvb.astype(BF16)
        dlw_ref[...] += dlw
        dlb_ref[...] += dlb

    act = jax.ShapeDtypeStruct((L, SSM_WIDTH), BF16)
    return pl.pallas_call(
        body, name=name, grid=(L // CHUNK,),
        in_specs=[blk(OFF_UB), blk(OFF_VB), blk(OFF_ZB), vec, vec, wsp, bsp, out],
        out_specs=[out, out, out, vec, vec, wsp, bsp],
        out_shape=[act, act, act, jax.ShapeDtypeStruct((1, SSM_WIDTH), F32), jax.ShapeDtypeStruct((1, SSM_WIDTH), F32),
                   jax.ShapeDtypeStruct((SG_HEADS, CHUNK, CHUNK), F32), jax.ShapeDtypeStruct((SG_HEADS, CHUNK, 1), F32)],
        compiler_params=_cp(("arbitrary",)),
    )(proj, proj, proj, ln_w, ln_b, w, bias, dyb)


def _rope_tables(L):
    half = ROT_DIM // 2
    inv_freq = ROPE_THETA ** (-jnp.arange(0, ROT_DIM, 2, dtype=F32) / ROT_DIM)
    ang = jnp.arange(L, dtype=F32)[:, None] * inv_freq[None, :]
    cos, sin = jnp.cos(ang), jnp.sin(ang)
    ones = jnp.ones((L, HEAD_DIM - ROT_DIM), F32)
    cos_h = jnp.concatenate([cos, cos, ones], axis=-1)
    sin_h = jnp.concatenate([-sin, sin, 0.0 * ones], axis=-1)
    src = jnp.arange(HEAD_DIM)[:, None]
    dst = jnp.arange(HEAD_DIM)[None, :]
    p_h = (((dst < half) & (src == dst + half)) | ((dst >= half) & (dst < ROT_DIM) & (src == dst - half))).astype(F32)
    p2 = jnp.kron(jnp.eye(2, dtype=F32), p_h).astype(BF16)
    return jnp.tile(cos_h, (1, 2)), jnp.tile(sin_h, (1, 2)), p2


def _rope(t, cos, sin, p2):
    n = t.shape[1] // 128
    tb = t.astype(BF16)
    sw = jnp.concatenate([_dot(tb[:, i * 128:(i + 1) * 128], p2) for i in range(n)], axis=-1) if n > 1 else _dot(tb, p2)
    return t * jnp.tile(cos, (1, n)) + sw * jnp.tile(sin, (1, n))


def _rope_t(g, cos, sin, p2):
    n = g.shape[1] // 128
    gs = (g * jnp.tile(sin, (1, n))).astype(BF16)
    sw = jnp.concatenate([_dot_nt(gs[:, i * 128:(i + 1) * 128], p2) for i in range(n)], axis=-1) if n > 1 else _dot_nt(gs, p2)
    return g * jnp.tile(cos, (1, n)) + sw


def _lane_lo(shape):
    return (lax.broadcasted_iota(jnp.int32, shape, len(shape) - 1) % 128) < HEAD_DIM


def _dup_halves(x):
    xr = pltpu.roll(x, HEAD_DIM, 1)
    lo = _lane_lo(x.shape)
    return jnp.where(lo, x, xr), jnp.where(lo, xr, x)


def _fold_halves(d0, d1):
    f0 = d0 + pltpu.roll(d0, HEAD_DIM, 1)
    f1 = d1 + pltpu.roll(d1, HEAD_DIM, 1)
    return jnp.where(_lane_lo(d0.shape), f0, f1)


def _attn_mask():
    qi = lax.broadcasted_iota(jnp.int32, (CHUNK, 2 * CHUNK), 0)
    kj = lax.broadcasted_iota(jnp.int32, (CHUNK, 2 * CHUNK), 1)
    return qi, kj


def _attn_specs():
    qsp = pl.BlockSpec((CHUNK, 1024), lambda n: (n, OFF_Q // 1024))
    kv_cur = pl.BlockSpec((CHUNK, 256), lambda n: (n, OFF_KV // 256))
    kv_prev = pl.BlockSpec((CHUNK, 256), lambda n: (jnp.maximum(n - 1, 0), OFF_KV // 256))
    zsp = [pl.BlockSpec((CHUNK, 256), functools.partial(lambda n, q: (n, OFF_ZC // 256 + q), q=q)) for q in range(4)]
    tab_cur = pl.BlockSpec((CHUNK, 128), lambda n: (n, 0))
    tab_prev = pl.BlockSpec((CHUNK, 128), lambda n: (jnp.maximum(n - 1, 0), 0))
    p2sp = pl.BlockSpec((128, 128), lambda n: (0, 0))
    sink = pl.BlockSpec(memory_space=pltpu.SMEM)
    wide = pl.BlockSpec((CHUNK, 1024), lambda n: (n, 0))
    return qsp, kv_cur, kv_prev, zsp, tab_cur, tab_prev, p2sp, sink, wide


def _attn_core(n, q_ref, kvc_ref, kvp_ref, cosc_ref, sinc_ref, cosp_ref, sinp_ref, p2_ref, sink_ref):
    p2 = p2_ref[...]
    qr = _rope(q_ref[...].astype(F32), cosc_ref[...], sinc_ref[...], p2).astype(BF16)
    kc = _rope(kvc_ref[:, 0:128].astype(F32), cosc_ref[...], sinc_ref[...], p2)
    kp = _rope(kvp_ref[:, 0:128].astype(F32), cosp_ref[...], sinp_ref[...], p2)
    k_all = jnp.concatenate([kp, kc], axis=0).astype(BF16)
    v_all = jnp.concatenate([kvp_ref[:, 128:256], kvc_ref[:, 128:256]], axis=0)
    kd = _dup_halves(k_all)
    vd = _dup_halves(v_all)
    qi, kj = _attn_mask()
    allowed = ((kj < CHUNK) & (kj > qi) & (n > 0)) | ((kj >= CHUNK) & (kj - CHUNK <= qi))
    lo = _lane_lo((CHUNK, 128))
    probs = []
    for h in range(ATT_HEADS):
        m, half, g = h // 2, h % 2, h // 8
        qp = qr[:, m * 128:(m + 1) * 128]
        qm = jnp.where(lo if half == 0 else ~lo, qp, jnp.zeros_like(qp))
        s = jnp.where(allowed, _dot_nt(qm, kd[g]) * (HEAD_DIM ** -0.5), NEG_INF)
        snk = sink_ref[h]
        mx = jnp.maximum(jnp.max(s, axis=-1, keepdims=True), snk)
        e = jnp.exp(s - mx)
        es = jnp.exp(snk - mx)
        inv = 1.0 / (jnp.sum(e, axis=-1, keepdims=True) + es)
        probs.append((qm, e * inv, es * inv))
    return qr, kd, vd, probs, lo


def _attn_out(vd, probs, lo):
    outs = []
    for m in range(ATT_HEADS // 2):
        g = m // 4
        o0 = _dot(probs[2 * m][1].astype(BF16), vd[g])
        o1 = _dot(probs[2 * m + 1][1].astype(BF16), vd[g])
        outs.append(jnp.where(lo, o0, o1))
    return jnp.concatenate(outs, axis=-1)


def _silu_gate(o, z):
    return o * jax.nn.silu(z)


def _attn_fwd(proj, sinks, tabs, name):
    L = proj.shape[0]
    cos2, sin2, p2 = tabs
    qsp, kv_cur, kv_prev, zsp, tab_cur, tab_prev, p2sp, sink, wide = _attn_specs()

    def body(q_ref, kvc_ref, kvp_ref, z0, z1, z2, z3, cosc, sinc, cosp, sinp, p2_ref, sink_ref, o_ref):
        n = pl.program_id(0)
        _, _, vd, probs, lo = _attn_core(n, q_ref, kvc_ref, kvp_ref, cosc, sinc, cosp, sinp, p2_ref, sink_ref)
        o = _attn_out(vd, probs, lo)
        z = jnp.concatenate([z0[...], z1[...], z2[...], z3[...]], axis=-1).astype(F32)
        o_ref[...] = _silu_gate(o, z).astype(BF16)

    return pl.pallas_call(
        body, name=name, grid=(L // CHUNK,),
        in_specs=[qsp, kv_cur, kv_prev, *zsp, tab_cur, tab_cur, tab_prev, tab_prev, p2sp, sink],
        out_specs=wide, out_shape=jax.ShapeDtypeStruct((L, 1024), BF16), compiler_params=_cp(("parallel",)),
    )(proj, proj, proj, proj, proj, proj, proj, cos2, sin2, cos2, sin2, p2, sinks)


def _attn_bwd(proj, sinks, tabs, dyc, name):
    L = proj.shape[0]
    cos2, sin2, p2 = tabs
    qsp, kv_cur, kv_prev, zsp, tab_cur, tab_prev, p2sp, sink, wide = _attn_specs()
    kvo = pl.BlockSpec((CHUNK, 256), lambda n: (n, 0))

    def body(q_ref, kvc_ref, kvp_ref, z0, z1, z2, z3, cosc, sinc, cosp, sinp, p2_ref, sink_ref, g_ref,
             dq_ref, dz_ref, dkvc_ref, dkvp_ref, dsink_ref):
        n = pl.program_id(0)
        _, kd, vd, probs, lo = _attn_core(n, q_ref, kvc_ref, kvp_ref, cosc, sinc, cosp, sinp, p2_ref, sink_ref)
        o = _attn_out(vd, probs, lo)
        z = jnp.concatenate([z0[...], z1[...], z2[...], z3[...]], axis=-1).astype(F32)
        _, vjp = jax.vjp(_silu_gate, o, z)
        do, dz = vjp(g_ref[...].astype(F32))
        dz_ref[...] = dz.astype(BF16)

        @pl.when(n == 0)
        def _():
            dsink_ref[...] = jnp.zeros_like(dsink_ref)

        dkd = [jnp.zeros((2 * CHUNK, 128), F32), jnp.zeros((2 * CHUNK, 128), F32)]
        dvd = [jnp.zeros((2 * CHUNK, 128), F32), jnp.zeros((2 * CHUNK, 128), F32)]
        dq_pairs = []
        for m in range(ATT_HEADS // 2):
            g = m // 4
            dop = do[:, m * 128:(m + 1) * 128].astype(BF16)
            dq_h = []
            for half in range(2):
                h = 2 * m + half
                qm, p, ps = probs[h]
                dom = jnp.where(lo if half == 0 else ~lo, dop, jnp.zeros_like(dop))
                dp = _dot_nt(dom, vd[g])
                rs = jnp.sum(p * dp, axis=-1, keepdims=True)
                ds = (p * (dp - rs) * (HEAD_DIM ** -0.5)).astype(BF16)
                dsink_ref[h:h + 1, :] += jnp.broadcast_to(jnp.sum(-ps * rs, axis=0, keepdims=True), (1, 128))
                dq_h.append(_dot(ds, kd[g]))
                dkd[g] = dkd[g] + _dot_tn(ds, qm)
                dvd[g] = dvd[g] + _dot_tn(p.astype(BF16), dom)
            dq_pairs.append(jnp.where(lo, dq_h[0], dq_h[1]))
        p2 = p2_ref[...]
        dq_ref[...] = _rope_t(jnp.concatenate(dq_pairs, axis=-1), cosc[...], sinc[...], p2).astype(BF16)
        dk_rot = _fold_halves(dkd[0], dkd[1])
        dv = _fold_halves(dvd[0], dvd[1])
        dkp = _rope_t(dk_rot[0:CHUNK], cosp[...], sinp[...], p2)
        dkc = _rope_t(dk_rot[CHUNK:2 * CHUNK], cosc[...], sinc[...], p2)
        dkvp_ref[...] = jnp.concatenate([dkp, dv[0:CHUNK]], axis=-1)
        dkvc_ref[...] = jnp.concatenate([dkc, dv[CHUNK:2 * CHUNK]], axis=-1)

    act = jax.ShapeDtypeStruct((L, 1024), BF16)
    kvs = jax.ShapeDtypeStruct((L, 256), F32)
    return pl.pallas_call(
        body, name=name, grid=(L // CHUNK,),
        in_specs=[qsp, kv_cur, kv_prev, *zsp, tab_cur, tab_cur, tab_prev, tab_prev, p2sp, sink, wide],
        out_specs=[wide, wide, kvo, kvo, pl.BlockSpec((ATT_HEADS, 128), lambda n: (0, 0))],
        out_shape=[act, act, kvs, kvs, jax.ShapeDtypeStruct((ATT_HEADS, 128), F32)],
        compiler_params=_cp(("arbitrary",)),
    )(proj, proj, proj, proj, proj, proj, proj, cos2, sin2, cos2, sin2, p2, sinks, dyc)


MERGE_TN = 256


def _merge_point(ta, tb, tc, ga, gb, gc):
    return jax.nn.sigmoid(ga) * ta + jax.nn.sigmoid(gb) * tb + jax.nn.sigmoid(gc) * tc


def _merge_specs(tm):
    nj = D_MODEL // MERGE_TN
    t = pl.BlockSpec((tm, MERGE_TN), lambda i, j: (i, j))
    gates = [pl.BlockSpec((tm, MERGE_TN), functools.partial(lambda i, j, b: (i, OFF_G // MERGE_TN + b * nj + j), b=b))
             for b in range(3)]
    return t, gates, nj


def _merge_fwd(ta, tb, tc, proj, name):
    L = ta.shape[0]
    tm = min(L, 1024)
    t, gates, nj = _merge_specs(tm)

    def body(ta_ref, tb_ref, tc_ref, ga_ref, gb_ref, gc_ref, o_ref):
        f = lambda r: r[...].astype(F32)
        o_ref[...] = _merge_point(f(ta_ref), f(tb_ref), f(tc_ref), f(ga_ref), f(gb_ref), f(gc_ref)).astype(BF16)

    return pl.pallas_call(
        body, name=name, grid=(L // tm, nj), in_specs=[t, t, t, *gates], out_specs=t,
        out_shape=jax.ShapeDtypeStruct((L, D_MODEL), BF16), compiler_params=_cp(("parallel", "parallel")),
    )(ta, tb, tc, proj, proj, proj)


def _merge_bwd(ta, tb, tc, proj, dm, name):
    L = ta.shape[0]
    tm = min(L, 1024)
    t, gates, nj = _merge_specs(tm)

    def body(ta_ref, tb_ref, tc_ref, ga_ref, gb_ref, gc_ref, dm_ref, dta_ref, dtb_ref, dtc_ref, dga_ref, dgb_ref, dgc_ref):
        f = lambda r: r[...].astype(F32)
        _, vjp = jax.vjp(_merge_point, f(ta_ref), f(tb_ref), f(tc_ref), f(ga_ref), f(gb_ref), f(gc_ref))
        outs = vjp(f(dm_ref))
        for r, v in zip((dta_ref, dtb_ref, dtc_ref, dga_ref, dgb_ref, dgc_ref), outs):
            r[...] = v.astype(BF16)

    act = jax.ShapeDtypeStruct((L, D_MODEL), BF16)
    return pl.pallas_call(
        body, name=name, grid=(L // tm, nj), in_specs=[t, t, t, *gates, t],
        out_specs=[t] * 6, out_shape=[act] * 6,
        compiler_params=_cp(("parallel", "parallel")),
    )(ta, tb, tc, proj, proj, proj, dm)


GRAD_DT = BF16
SMALL = ("norm_w", "ssm_a_re", "ssm_a_im", "ssm_log_dt", "ssm_b_re", "ssm_b_im", "ssm_c_re", "ssm_c_im", "ssm_d",
         "ssm_glu_b", "sg_ln_w", "sg_ln_b", "sg_w", "sg_b", "attn_sinks")
G8 = SSM_GROUPS // N_SLAB


def _diag_mask(rows_per_group, cols_per_group):
    r = jnp.arange(G8 * rows_per_group)[:, None] // rows_per_group
    c = jnp.arange(G8 * cols_per_group)[None, :] // cols_per_group
    return r == c


def _slab_b(bb_t):
    x = bb_t.transpose(1, 0, 2).reshape(N_SLAB, SLAB_CH, SSM_STATE)
    return jnp.where(_diag_mask(SSM_GROUP, SSM_STATE), jnp.tile(x, (1, 1, G8)), 0)


def _unslab_b(d):
    x = jnp.where(_diag_mask(SSM_GROUP, SSM_STATE), d, 0).reshape(N_SLAB, SLAB_CH, G8, SSM_STATE).sum(axis=2)
    return x.reshape(SSM_GROUPS, SSM_GROUP, SSM_STATE).transpose(1, 0, 2)


def _slab_c(c):
    x = c.transpose(0, 2, 1).reshape(N_SLAB, SLAB_ST, SSM_GROUP)
    return jnp.where(_diag_mask(SSM_STATE, SSM_GROUP), jnp.tile(x, (1, 1, G8)), 0)


def _unslab_c(d):
    x = jnp.where(_diag_mask(SSM_STATE, SSM_GROUP), d, 0).reshape(N_SLAB, SLAB_ST, G8, SSM_GROUP).sum(axis=2)
    return x.reshape(SSM_GROUPS, SSM_STATE, SSM_GROUP).transpose(0, 2, 1)


def _s5_prep(p, tag):
    bt_re = p["ssm_b_re"].transpose(2, 0, 1)
    bt_im = p["ssm_b_im"].transpose(2, 0, 1)
    raw = (p["ssm_a_re"], p["ssm_a_im"], p["ssm_log_dt"][:, None], bt_re, bt_im)
    lr, li, bbr, bbi = _s5_params_fwd(*raw, name=f"s5_params_{tag}")
    ops = (_slab_b(bbr).astype(BF16), _slab_b(bbi).astype(BF16),
           _slab_c(p["ssm_c_re"]).astype(BF16), _slab_c(p["ssm_c_im"]).astype(BF16),
           jnp.broadcast_to(lr.reshape(N_SLAB, 1, SLAB_ST), (N_SLAB, SUB, SLAB_ST)),
           jnp.broadcast_to(li.reshape(N_SLAB, 1, SLAB_ST), (N_SLAB, SUB, SLAB_ST)),
           p["ssm_d"].reshape(N_SLAB, 1, SLAB_CH))
    return raw, ops


def _layer_fwd(x, p, w, tabs, tag, s5=None, win_of=None, after_proj=None):
    L = x.shape[0]
    h = _rms_fwd(x, p["norm_w"][None], f"rms_fwd_{tag}")
    win_t, proj_after = win_of(h) if win_of is not None else (w["win_t"], None)
    proj = _mm(h, win_t, "nt", BF16, L, 256, D_MODEL, f"in_proj_{tag}", after=proj_after)
    if after_proj is not None:
        w = after_proj(proj)
    s5_raw, s5_ops = s5 if s5 is not None else _s5_prep(p, tag)
    ya0 = _s5_fwd(proj, *s5_ops, name=f"s5_fwd_{tag}")
    ya = _glu_fwd(ya0, proj, w["glu"], p["ssm_glu_b"][None], f"glu_fwd_{tag}")
    yb = _sg_fwd(proj, p["sg_ln_w"][None], p["sg_ln_b"][None], p["sg_w"], p["sg_b"][:, :, None], f"sg_fwd_{tag}")
    yc = _attn_fwd(proj, p["attn_sinks"], tabs, f"attn_fwd_{tag}")
    ta = _mm(ya, w["wba_t"], "nt", BF16, 1024, 1024, 1024, f"branch_a_{tag}")
    tb = _mm(yb, w["wbb_t"], "nt", BF16, 1024, 1024, 1024, f"branch_b_{tag}")
    tc = _mm(yc, w["wbc_t"], "nt", BF16, 1024, 1024, 1024, f"branch_c_{tag}")
    merged = _merge_fwd(ta, tb, tc, proj, f"merge_fwd_{tag}")
    x_new = _mm(merged, w["wout"], "nn", F32, 1024, 512, D_MODEL, f"out_proj_{tag}", res=x)
    saved = dict(x=x, h=h, proj=proj, s5_raw=s5_raw, s5_ops=s5_ops, ya0=ya0, ya=ya, yb=yb, yc=yc,
                 ta=ta, tb=tb, tc=tc, merged=merged)
    return x_new, saved


def _layer_bwd(dx_out, p, w, tabs, s, tag, first_after=None, before_win=None, after_win=None):
    L = dx_out.shape[0]
    proj = s["proj"]
    big, small = {}, {}
    dmerged = _mm(dx_out, w["wout"], "nt", BF16, 1024, 512, D_MODEL, f"d_merged_{tag}", after=first_after)
    big["wout"] = _mm(s["merged"], dx_out, "tn", GRAD_DT, 512, 1024, L, f"d_wout_{tag}")
    dta, dtb, dtc, dga, dgb, dgc = _merge_bwd(s["ta"], s["tb"], s["tc"], proj, dmerged, f"merge_bwd_{tag}")
    dy = {}
    for br, dt in (("a", dta), ("b", dtb), ("c", dtc)):
        dy[br] = _mm(dt, w[f"wb{br}_t"], "nn", BF16, 1024, 1024, D_MODEL, f"d_y{br}_{tag}")
        big[f"wb{br}_t"] = _mm(dt, s[f"y{br}"], "tn", GRAD_DT, 512, 1024, L, f"d_wb{br}_{tag}")

    dq, dzc, dkvc, dkvp, dsink = _attn_bwd(proj, p["attn_sinks"], tabs, dy["c"], f"attn_bwd_{tag}")
    dkv = dkvc + jnp.concatenate([dkvp[CHUNK:], jnp.zeros((CHUNK, 256), F32)], axis=0)
    small["attn_sinks"] = dsink[:, 0]

    dub, dvb, dzb, dlw, dlb, dsgw, dsgb = _sg_bwd(
        proj, p["sg_ln_w"][None], p["sg_ln_b"][None], p["sg_w"], p["sg_b"][:, :, None], dy["b"], f"sg_bwd_{tag}")
    small.update(sg_ln_w=dlw[0], sg_ln_b=dlb[0], sg_w=dsgw, sg_b=dsgb[:, :, 0])

    dya0, dza, dglu, dglub = _glu_bwd(s["ya0"], proj, w["glu"], p["ssm_glu_b"][None], dy["a"], f"glu_bwd_{tag}")
    big["glu"] = dglu.astype(GRAD_DT)
    small["ssm_glu_b"] = dglub[0]

    dua, dbre, dbim, dcre, dcim, dlr, dli, dd = _s5_bwd(proj, dya0, *s["s5_ops"], name=f"s5_bwd_{tag}")
    da_re, da_im, dlog_dt, dbt_re, dbt_im = _s5_params_bwd(
        *s["s5_raw"], dlr.reshape(SSM_GROUPS, SSM_STATE), dli.reshape(SSM_GROUPS, SSM_STATE),
        _unslab_b(dbre), _unslab_b(dbim), name=f"s5_params_bwd_{tag}")
    small.update(ssm_a_re=da_re, ssm_a_im=da_im, ssm_log_dt=dlog_dt[:, 0],
                 ssm_bt_re=dbt_re, ssm_bt_im=dbt_im,
                 ssm_c_re=_unslab_c(dcre), ssm_c_im=_unslab_c(dcim), ssm_d=dd.reshape(SSM_WIDTH))

    dproj = jnp.concatenate([dua, dza, dub, dvb, dzb, dq, dkv.astype(BF16), dzc, dga, dgb, dgc], axis=-1)
    tok = before_win(big) if before_win is not None else None
    big["win_t"] = _mm(dproj, s["h"], "tn", GRAD_DT, 256, D_MODEL, L, f"d_win_{tag}", after=tok)
    tok = after_win(big) if after_win is not None else None
    dh = _mm(dproj, w["win_t"], "nn", F32, 1024, D_MODEL, 256, f"d_h_{tag}", after=tok)
    dx_in, dnw = _rms_bwd(s["x"], p["norm_w"][None], dh, dx_out, f"rms_bwd_{tag}")
    small["norm_w"] = dnw[0]
    return dx_in, big, small


def _local_step(x, tgt, small_p, final_w, big_w):
    L = x.shape[0]
    tabs = _rope_tables(L)
    saved = []
    for l in range(DEPTH):
        x, s = _layer_fwd(x, small_p[l], big_w[l], tabs, f"l{l}")
        saved.append(s)
    loss_acc, dx, dfw = _final(x, final_w[None], tgt, "final_norm_loss")
    big_g, small_g = [None] * DEPTH, [None] * DEPTH
    for l in reversed(range(DEPTH)):
        dx, big_g[l], small_g[l] = _layer_bwd(dx, small_p[l], big_w[l], tabs, saved[l], f"l{l}")
    return loss_acc[0, 0], dx, dfw[0], big_g, small_g


MESH = pl.DeviceIdType.MESH
ANY = pl.BlockSpec(memory_space=pl.ANY)
ROW_ALIGN = 16


def _place():
    return lax.axis_index("x"), lax.axis_index("y"), lax.axis_index("c")


HBM = pl.BlockSpec(memory_space=pltpu.HBM)
SEM = pl.BlockSpec(memory_space=pltpu.SEMAPHORE)
EFFECT = pltpu.SideEffectType.DATAFLOW_SIDE_EFFECTING


def _split_start(srcs, lands, n_copies, copies, name, after=None):
    n, m, k = len(srcs), len(lands), n_copies
    extra = [] if after is None else [after]

    def body(*refs):
        src_refs, land_refs = refs[:n], refs[n:n + m]
        sems = refs[n + m + len(extra):]
        send_sems, recv_sems, token = sems[:k], sems[k:2 * k], refs[-1]
        for cp in copies(src_refs, land_refs, send_sems, recv_sems):
            cp.start()
        token[...] = jnp.zeros_like(token)

    ops = list(srcs) + list(lands)
    outs = pl.pallas_call(
        body, name=name,
        out_shape=(*[pltpu.SemaphoreType.DMA(())] * (2 * k),
                   *[pltpu.HBM(a.shape, a.dtype) for a in ops], jax.ShapeDtypeStruct((8, 128), F32)),
        in_specs=[HBM] * (n + m) + [ANY] * len(extra),
        out_specs=(*[SEM] * (2 * k), *[HBM] * (n + m), pl.BlockSpec(memory_space=pltpu.VMEM)),
        input_output_aliases={i: 2 * k + i for i in range(n + m)},
        compiler_params=pltpu.CompilerParams(has_side_effects=EFFECT),
    )(*[pltpu.with_memory_space_constraint(a, pltpu.HBM) for a in ops], *extra)
    return (list(outs[:k]), list(outs[k:2 * k]), list(outs[2 * k:2 * k + n]), list(outs[2 * k + n:2 * k + n + m]),
            outs[-1])


def _split_wait(send_sems, recv_sems, srcs, lands, after, copies, name):
    n, m, k = len(srcs), len(lands), len(send_sems)
    after = list(after) if isinstance(after, (list, tuple)) else [after]

    def body(*refs):
        src_refs, land_refs = refs[:n], refs[n:n + m]
        for cp in copies(src_refs, land_refs, refs[n + m:n + m + k], refs[n + m + k:n + m + 2 * k]):
            cp.wait_send()
            cp.wait_recv()

    ops = list(srcs) + list(lands)
    outs = pl.pallas_call(
        body, name=name,
        out_shape=tuple(pltpu.HBM(a.shape, a.dtype) for a in ops),
        in_specs=[HBM] * (n + m) + [SEM] * (2 * k) + [ANY] * len(after),
        out_specs=tuple([HBM] * (n + m)),
        input_output_aliases={i: i for i in range(n + m)},
        compiler_params=pltpu.CompilerParams(has_side_effects=EFFECT),
    )(*ops, *send_sems, *recv_sems, *after)
    return list(outs[:n]), list(outs[n:])


def _ag_rows(land_ref, px, py, pc):
    r = land_ref.shape[0] // N_DEV
    start = pl.multiple_of((4 * px + 2 * py + pc) * r, ROW_ALIGN)
    return land_ref.at[pl.ds(start, r), :]


def _ag_copies(src_refs, land_refs, send_sems, recv_sems):
    x, y, c = _place()
    peers = [(x, y, 1 - c), (1 - x, y, c), (x, 1 - y, c), (1 - x, 1 - y, c)]
    return [pltpu.make_async_remote_copy(
        src_ref=_ag_rows(land_refs[a], x, y, c), dst_ref=_ag_rows(land_refs[a], x, y, c),
        send_sem=send_sems[4 * a + k], recv_sem=recv_sems[4 * a + k], device_id=peer, device_id_type=MESH)
        for a in range(len(land_refs)) for k, peer in enumerate(peers)]


def _ag_forward(lands, name):
    n = len(lands)

    def body(*refs):
        land_refs = refs[n:2 * n]
        send_sems, recv_sems = refs[2 * n:]
        x, y, c = _place()
        chips = [(1 - x, y), (x, 1 - y), (1 - x, 1 - y)]

        def copy(a, j, pc):
            px, py = chips[j]
            return pltpu.make_async_remote_copy(
                src_ref=_ag_rows(land_refs[a], px, py, pc), dst_ref=_ag_rows(land_refs[a], px, py, pc),
                send_sem=send_sems.at[a, j], recv_sem=recv_sems.at[a, j], device_id=(x, y, 1 - c), device_id_type=MESH)

        passed = [copy(a, j, c) for a in range(n) for j in range(3)]
        for cp in passed:
            cp.start()
        for a in range(n):
            for j in range(3):
                copy(a, j, 1 - c).wait_recv()
        for cp in passed:
            cp.wait_send()

    return pl.pallas_call(
        body, name=name,
        in_specs=[ANY] * n, out_specs=[ANY] * n,
        out_shape=[jax.ShapeDtypeStruct(l.shape, l.dtype) for l in lands],
        input_output_aliases={i: i for i in range(n)},
        scratch_shapes=[pltpu.SemaphoreType.DMA((n, 3)), pltpu.SemaphoreType.DMA((n, 3))],
    )(*lands)


def _allgather_place(shards):
    x, y, c = _place()
    return [lax.dynamic_update_slice(lax.empty((N_DEV * s.shape[0], s.shape[1]), s.dtype), s,
                                     ((4 * x + 2 * y + c) * s.shape[0], 0)) for s in shards]


def _allgather_start(lands, name, after=None):
    return _split_start([], lands, 4 * len(lands), _ag_copies, name + "_start", after=after)


def _allgather_finish(started, after, name):
    send_sems, recv_sems, _, lands, _ = started
    _, lands = _split_wait(send_sems, recv_sems, [], lands, after, _ag_copies, name + "_wait")
    return list(_ag_forward(lands, name + "_forward"))


def _rs_swap_cores(grads, name):
    n = len(grads)

    def body(*refs):
        ins, outs = refs[:n], refs[n:2 * n]
        send_sems, recv_sems = refs[2 * n:]
        x, y, c = _place()
        cps = []
        for a in range(n):
            r = ins[a].shape[0] // N_DEV
            for q in range(4):
                start = pl.multiple_of((2 * q + 1 - c) * r, ROW_ALIGN)
                cps.append(pltpu.make_async_remote_copy(
                    src_ref=ins[a].at[pl.ds(start, r), :], dst_ref=outs[a].at[q],
                    send_sem=send_sems.at[a, q], recv_sem=recv_sems.at[a, q],
                    device_id=(x, y, 1 - c), device_id_type=MESH))
        for cp in cps:
            cp.start()
        for cp in cps:
            cp.wait()

    return pl.pallas_call(
        body, name=name, in_specs=[ANY] * n, out_specs=[ANY] * n,
        out_shape=[jax.ShapeDtypeStruct((4, g.shape[0] // N_DEV, g.shape[1]), g.dtype) for g in grads],
        scratch_shapes=[pltpu.SemaphoreType.DMA((n, 4)), pltpu.SemaphoreType.DMA((n, 4))],
    )(*grads)


def _rs_chip_copies(sum_refs, land_refs, send_sems, recv_sems):
    x, y, c = _place()
    chips = [(1 - x, y), (x, 1 - y), (1 - x, 1 - y)]
    return [pltpu.make_async_remote_copy(
        src_ref=sum_refs[a].at[2 * px + py], dst_ref=land_refs[a].at[2 * x + y],
        send_sem=send_sems[3 * a + j], recv_sem=recv_sems[3 * a + j], device_id=(px, py, c), device_id_type=MESH)
        for a in range(len(sum_refs)) for j, (px, py) in enumerate(chips)]


def _row_tile(r):
    return max(t for t in range(ROW_ALIGN, min(r, 1024) + 1, ROW_ALIGN) if r % t == 0)


def _rs_add_cores(grad, recv, cidx, name):
    r, cols = recv.shape[1], recv.shape[2]
    tr = _row_tile(r)
    nb = r // tr

    def body(c_ref, g_ref, r_ref, o_ref):
        o_ref[...] = (g_ref[...].astype(F32) + r_ref[...].astype(F32)).astype(o_ref.dtype)

    return pl.pallas_call(
        body, name=name,
        grid_spec=pltpu.PrefetchScalarGridSpec(
            num_scalar_prefetch=1, grid=(4, nb),
            in_specs=[pl.BlockSpec((tr, cols), lambda q, i, c_ref: ((2 * q + c_ref[0]) * nb + i, 0)),
                      pl.BlockSpec((None, tr, cols), lambda q, i, c_ref: (q, i, 0))],
            out_specs=pl.BlockSpec((None, tr, cols), lambda q, i, c_ref: (q, i, 0))),
        out_shape=jax.ShapeDtypeStruct(recv.shape, recv.dtype),
        compiler_params=_cp(("parallel", "parallel")),
    )(cidx, grad, recv)


def _rs_add_chips(own, recv, slots, name):
    r, cols = recv.shape[1], recv.shape[2]
    tr = _row_tile(r)

    def body(s_ref, o_ref, r0_ref, r1_ref, r2_ref, out_ref):
        acc = o_ref[...].astype(F32)
        for ref in (r0_ref, r1_ref, r2_ref):
            acc = acc + ref[...].astype(F32)
        out_ref[...] = acc

    pick = lambda k: pl.BlockSpec((None, tr, cols), functools.partial(lambda i, s_ref, k: (s_ref[k], i, 0), k=k))
    return pl.pallas_call(
        body, name=name,
        grid_spec=pltpu.PrefetchScalarGridSpec(
            num_scalar_prefetch=1, grid=(r // tr,),
            in_specs=[pick(0), pick(1), pick(2), pick(3)],
            out_specs=pl.BlockSpec((tr, cols), lambda i, s_ref: (i, 0))),
        out_shape=jax.ShapeDtypeStruct((r, cols), F32),
        compiler_params=_cp(("parallel",)),
    )(slots, own, recv, recv, recv)


def _reduce_scatter_start(grads, tag):
    cidx = lax.axis_index("c").astype(jnp.int32)[None]
    recv = _rs_swap_cores(grads, f"rs_swap_cores_{tag}")
    sums = [_rs_add_cores(g, rv, cidx, f"rs_add_cores_{tag}_{i}") for i, (g, rv) in enumerate(zip(grads, recv))]
    lands = [lax.empty(s.shape, s.dtype) for s in sums]
    return _split_start(sums, lands, 3 * len(sums), _rs_chip_copies, f"rs_chips_{tag}_start")


def _reduce_scatter_finish(started, after, tag):
    send_sems, recv_sems, sums, lands, _ = started
    sums, lands = _split_wait(send_sems, recv_sems, sums, lands, after, _rs_chip_copies, f"rs_chips_{tag}_wait")
    x, y = lax.axis_index("x"), lax.axis_index("y")
    slots = jnp.stack([2 * x + y, 2 * (1 - x) + y, 2 * x + 1 - y, 2 * (1 - x) + 1 - y]).astype(jnp.int32)
    return [_rs_add_chips(s, l, slots, f"rs_add_chips_{tag}_{i}") for i, (s, l) in enumerate(zip(sums, lands))]


def _allreduce_small(packs, name):
    n = len(packs)
    assert all(p.shape[0] % (8 * N_DEV) == 0 for p in packs)

    def body(*refs):
        p_refs, o_refs, part_refs = refs[:n], refs[n:2 * n], refs[2 * n:3 * n]
        send1, recv1, send2, recv2 = refs[3 * n:]
        x, y, c = _place()
        me = 4 * x + 2 * y + c

        def block(ref, d):
            rs = ref.shape[0] // N_DEV
            return ref.at[pl.ds(pl.multiple_of(d * rs, 8), rs), :]

        peers = [(1 - x if k & 4 else x, 1 - y if k & 2 else y, 1 - c if k & 1 else c) for k in range(1, N_DEV)]
        scatter = [pltpu.make_async_remote_copy(
            src_ref=block(p_refs[a], 4 * px + 2 * py + pc), dst_ref=part_refs[a].at[me],
            send_sem=send1.at[a, k], recv_sem=recv1.at[a, k], device_id=(px, py, pc), device_id_type=MESH)
            for a in range(n) for k, (px, py, pc) in enumerate(peers)]
        for cp in scatter:
            cp.start()
        for a in range(n):
            part_refs[a][me] = block(p_refs[a], me)[...]
        for cp in scatter:
            cp.wait()
        for a in range(n):
            acc = part_refs[a][0]
            for d in range(1, N_DEV):
                acc = acc + part_refs[a][d]
            block(o_refs[a], me)[...] = acc
        gather = [pltpu.make_async_remote_copy(
            src_ref=block(o_refs[a], me), dst_ref=block(o_refs[a], me), send_sem=send2.at[a, k], recv_sem=recv2.at[a, k],
            device_id=peer, device_id_type=MESH) for a in range(n) for k, peer in enumerate(peers)]
        for cp in gather:
            cp.start()
        for a in range(n):
            for k, (px, py, pc) in enumerate(peers):
                theirs = block(o_refs[a], 4 * px + 2 * py + pc)
                pltpu.make_async_remote_copy(
                    src_ref=theirs, dst_ref=theirs, send_sem=send2.at[a, k], recv_sem=recv2.at[a, k],
                    device_id=(px, py, pc), device_id_type=MESH).wait_recv()
        for cp in gather:
            cp.wait_send()

    sems = pltpu.SemaphoreType.DMA((n, N_DEV - 1))
    vmem = pl.BlockSpec(memory_space=pltpu.VMEM)
    return pl.pallas_call(
        body, name=name,
        in_specs=[vmem] * n, out_specs=[vmem] * n,
        out_shape=[jax.ShapeDtypeStruct(p.shape, F32) for p in packs],
        scratch_shapes=[pltpu.VMEM((N_DEV, p.shape[0] // N_DEV, p.shape[1]), F32) for p in packs] + [sems] * 4,
        compiler_params=pltpu.CompilerParams(vmem_limit_bytes=VMEM_LIMIT),
    )(*packs)


ADAM_TILE_BYTES = 2 * 1024 * 1024


def _adamw(w, g, m, v, name):
    shape = w.shape
    rows, cols = shape[-2:]
    lead = shape[:-2]
    nl = math.prod(lead)
    tc = cols // 2 if cols % 256 == 0 and cols >= 2048 else cols
    tr = max(t for t in range(8, rows + 1, 8) if rows % t == 0 and t * max(tc, 128) * 4 <= ADAM_TILE_BYTES) \
        if rows % 8 == 0 else rows
    c1 = 1.0 - ADAM_B1 ** ADAM_STEP
    c2 = 1.0 - ADAM_B2 ** ADAM_STEP

    def body(w_ref, g_ref, m_ref, v_ref, d_ref, nm_ref, nv_ref):
        gv = g_ref[...]
        nm = ADAM_B1 * m_ref[...] + (1.0 - ADAM_B1) * gv
        nv = ADAM_B2 * v_ref[...] + (1.0 - ADAM_B2) * jnp.square(gv)
        d_ref[...] = -ADAM_LR * ((nm / c1) / (jnp.sqrt(nv / c2) + ADAM_EPS) + ADAM_WD * w_ref[...])
        nm_ref[...] = nm
        nv_ref[...] = nv

    def index(b, i, j):
        return (*jnp.unravel_index(b, lead), i, j) if lead else (i, j)

    blk = pl.BlockSpec((*[None] * len(lead), tr, tc), index)
    sh = jax.ShapeDtypeStruct(shape, F32)
    return pl.pallas_call(
        body, name=name, grid=(nl, rows // tr, cols // tc), in_specs=[blk] * 4, out_specs=[blk] * 3,
        out_shape=[sh] * 3, compiler_params=_cp(("parallel", "parallel", "parallel")),
    )(w, g, m, v)


WEIGHTS = ("norm_w", "w_in", "ssm_a_re", "ssm_a_im", "ssm_log_dt", "ssm_b_re", "ssm_b_im", "ssm_c_re", "ssm_c_im",
           "ssm_d", "ssm_glu_w", "ssm_glu_b", "sg_ln_w", "sg_ln_b", "sg_w", "sg_b", "attn_sinks",
           "w_branch_a", "w_branch_b", "w_branch_c", "w_out", "final_norm_w")
BIG = ("w_in", "ssm_glu_w", "w_branch_a", "w_branch_b", "w_branch_c", "w_out")
BIG_KEY = {"w_in": ("win_t", True), "ssm_glu_w": ("glu", False), "w_branch_a": ("wba_t", True),
           "w_branch_b": ("wbb_t", True), "w_branch_c": ("wbc_t", True), "w_out": ("wout", False)}
VIEWS = {"w_in": (1, 2), "ssm_b_re": (2, 3), "ssm_b_im": (2, 3)}
PACKS = (
    (64, (("ssm_a_re",), ("ssm_a_im",), ("ssm_c_re",), ("ssm_c_im",), ("ssm_b_re",), ("ssm_b_im",))),
    (128, (("sg_w",),)),
    (1024, (("ssm_d", "ssm_glu_b", "sg_ln_w", "sg_ln_b"), ("norm_w", "final_norm_w", "sg_b"), ("ssm_log_dt", "attn_sinks"))),
)
PACK_ROWS = 8 * N_DEV


def _view(n, a):
    return jnp.swapaxes(a, *VIEWS[n]) if n in VIEWS else a


def _group_rows(arrs, cols):
    return -(-sum(-(-a.size // cols) for a in arrs) // 8) * 8


def _pack(groups, cols):
    parts = []
    for arrs in groups:
        if len(arrs) == 1 and arrs[0].shape[-1] == cols and arrs[0].size % (8 * cols) == 0:
            parts.append(arrs[0].reshape(-1, cols))
            continue
        flat = [jnp.pad(a.reshape(-1), (0, -a.size % cols)) for a in arrs]
        flat = jnp.concatenate(flat) if len(flat) > 1 else flat[0]
        nrow = _group_rows(arrs, cols)
        parts.append(jnp.pad(flat, (0, nrow * cols - flat.shape[0])).reshape(nrow, cols))
    pad = -sum(p.shape[0] for p in parts) % PACK_ROWS
    if pad:
        parts.append(jnp.zeros((pad, cols), F32))
    return jnp.concatenate(parts, axis=0)


def _unpack(pack, groups):
    cols = pack.shape[1]
    out, row = [], 0
    for arrs in groups:
        nrow = _group_rows(arrs, cols)
        rows = pack[row:row + nrow]
        row += nrow
        if len(arrs) == 1 and arrs[0].shape[-1] == cols and arrs[0].size == nrow * cols:
            out.append(rows.reshape(arrs[0].shape))
            continue
        flat, off = rows.reshape(-1), 0
        for a in arrs:
            out.append(flat[off:off + a.size].reshape(a.shape))
            off += -(-a.size // cols) * cols
    return out


def kernel(x, norm_w, w_in, ssm_a_re, ssm_a_im, ssm_log_dt, ssm_b_re, ssm_b_im, ssm_c_re, ssm_c_im, ssm_d, ssm_glu_w, ssm_glu_b, sg_ln_w, sg_ln_b, sg_w, sg_b, attn_sinks, w_branch_a, w_branch_b, w_branch_c, w_out, final_norm_w, loss_target, m_norm_w, m_w_in, m_ssm_a_re, m_ssm_a_im, m_ssm_log_dt, m_ssm_b_re, m_ssm_b_im, m_ssm_c_re, m_ssm_c_im, m_ssm_d, m_ssm_glu_w, m_ssm_glu_b, m_sg_ln_w, m_sg_ln_b, m_sg_w, m_sg_b, m_attn_sinks, m_w_branch_a, m_w_branch_b, m_w_branch_c, m_w_out, m_final_norm_w, v_norm_w, v_w_in, v_ssm_a_re, v_ssm_a_im, v_ssm_log_dt, v_ssm_b_re, v_ssm_b_im, v_ssm_c_re, v_ssm_c_im, v_ssm_d, v_ssm_glu_w, v_ssm_glu_b, v_sg_ln_w, v_sg_ln_b, v_sg_w, v_sg_b, v_attn_sinks, v_w_branch_a, v_w_branch_b, v_w_branch_c, v_w_out, v_final_norm_w):
    w = dict(zip(WEIGHTS, (norm_w, w_in, ssm_a_re, ssm_a_im, ssm_log_dt, ssm_b_re, ssm_b_im, ssm_c_re, ssm_c_im, ssm_d, ssm_glu_w, ssm_glu_b, sg_ln_w, sg_ln_b, sg_w, sg_b, attn_sinks, w_branch_a, w_branch_b, w_branch_c, w_out, final_norm_w)))
    m = dict(zip(WEIGHTS, (m_norm_w, m_w_in, m_ssm_a_re, m_ssm_a_im, m_ssm_log_dt, m_ssm_b_re, m_ssm_b_im, m_ssm_c_re, m_ssm_c_im, m_ssm_d, m_ssm_glu_w, m_ssm_glu_b, m_sg_ln_w, m_sg_ln_b, m_sg_w, m_sg_b, m_attn_sinks, m_w_branch_a, m_w_branch_b, m_w_branch_c, m_w_out, m_final_norm_w)))
    v = dict(zip(WEIGHTS, (v_norm_w, v_w_in, v_ssm_a_re, v_ssm_a_im, v_ssm_log_dt, v_ssm_b_re, v_ssm_b_im, v_ssm_c_re, v_ssm_c_im, v_ssm_d, v_ssm_glu_w, v_ssm_glu_b, v_sg_ln_w, v_sg_ln_b, v_sg_w, v_sg_b, v_attn_sinks, v_w_branch_a, v_w_branch_b, v_w_branch_c, v_w_out, v_final_norm_w)))

    keys = [BIG_KEY[n][0] for n in BIG]
    wv, mv, vv = ({n: _view(n, a) for n, a in d.items()} for d in (w, m, v))
    shards = [[(wv[n][l] if n in VIEWS else w[n][l].T if BIG_KEY[n][1] else w[n][l]).astype(BF16) for n in BIG]
              for l in range(DEPTH)]
    small_p = [{n: w[n][l] for n in SMALL} for l in range(DEPTH)]
    xv, tgt = x[0], loss_target[0]
    tabs = _rope_tables(xv.shape[0])

    lands = [[_allgather_place(shards[l][:1]), _allgather_place(shards[l][1:])] for l in range(DEPTH)]
    s5 = [_s5_prep(small_p[l], f"l{l}") for l in range(DEPTH)]
    wmv_packs = {cols: [_pack([[d[n] for n in names] for names in groups], cols) for d in (wv, mv, vv)]
                 for cols, groups in PACKS}
    ag0a = _allgather_start(lands[0][0], "ag_l0_win")
    got = {}

    def win_of0(h):
        early = [h, *lands[0][1], *lands[1][0], *lands[1][1], *s5[0][1], *s5[1][1]]
        early += [p for ps in wmv_packs.values() for p in ps]
        got["win0"] = _allgather_finish(ag0a, early, "ag_l0_win")[0]
        got["ag0b"] = _allgather_start(lands[0][1], "ag_l0_rest", after=got["win0"])
        got["ag1a"] = _allgather_start(lands[1][0], "ag_l1_win", after=got["ag0b"][4])
        got["ag1b"] = _allgather_start(lands[1][1], "ag_l1_rest", after=got["ag1a"][4])
        return got["win0"], got["ag1b"][4]

    def after_proj0(proj):
        got["w0"] = dict(zip(keys, [got["win0"]] + _allgather_finish(got["ag0b"], proj, "ag_l0_rest")))
        return got["w0"]

    x1, saved0 = _layer_fwd(xv, small_p[0], None, tabs, "l0", s5=s5[0], win_of=win_of0, after_proj=after_proj0)
    big_w0 = got["w0"]
    win1 = _allgather_finish(got["ag1a"], x1, "ag_l1_win")[0]

    def after_proj1(proj):
        got["w1"] = dict(zip(keys, [win1] + _allgather_finish(got["ag1b"], proj, "ag_l1_rest")))
        return got["w1"]

    x2, saved1 = _layer_fwd(x1, small_p[1], {"win_t": win1}, tabs, "l1", s5=s5[1], after_proj=after_proj1)
    big_w1 = got["w1"]
    loss_acc, dx2, dfw = _final(x2, w["final_norm_w"][None], tgt, "final_norm_loss")
    loss = lax.psum(loss_acc[0, 0], ("x", "y", "c"))
    dfw = dfw[0]

    dx1, big_g1, small_g1 = _layer_bwd(dx2, small_p[1], big_w1, tabs, saved1, "l1")
    rs1 = _reduce_scatter_start([big_g1[k] for k in keys], "l1")

    def before_win0(big):
        got["rs0b"] = _reduce_scatter_start([big[k] for k in keys[1:]], "l0_rest")
        return got["rs0b"][4]

    def after_win0(big):
        got["rs0a"] = _reduce_scatter_start([big["win_t"]], "l0_win")
        return got["rs0a"][4]

    dx, big_g0, small_g0 = _layer_bwd(dx1, small_p[0], big_w0, tabs, saved0, "l0", first_after=rs1[4],
                                      before_win=before_win0, after_win=after_win0)
    red1 = _reduce_scatter_finish(rs1, dx, "l1")
    small_g = [small_g0, small_g1]

    grads, delta, new_m, new_v = {}, {}, {}, {}

    def small_grad(n):
        if n == "final_norm_w":
            return dfw
        if n in ("ssm_b_re", "ssm_b_im"):
            return jnp.stack([small_g[l][n.replace("ssm_b_", "ssm_bt_")].transpose(1, 0, 2) for l in range(DEPTH)])
        return jnp.stack([small_g[l][n] for l in range(DEPTH)])

    g_groups = [[[small_grad(n) for n in names] for names in groups] for _, groups in PACKS]
    reduced = _allreduce_small([_pack(gg, cols) for gg, (cols, _) in zip(g_groups, PACKS)], "allreduce_small")
    last = None
    for (cols, groups), gg, red in zip(PACKS, g_groups, reduced):
        names = [n for names in groups for n in names]
        grads.update(zip(names, _unpack(red, gg)))
        wp, mp, vp = wmv_packs[cols]
        outs = _adamw(wp, red, mp, vp, f"adamw_pack{cols}")
        last = outs[0]
        for res, o in zip((delta, new_m, new_v), outs):
            res.update(zip(names, _unpack(o, [[wv[n] for n in names] for names in groups])))

    red0 = (_reduce_scatter_finish(got["rs0a"], last, "l0_win")
            + _reduce_scatter_finish(got["rs0b"], last, "l0_rest"))
    for i, n in enumerate(BIG):
        back = BIG_KEY[n][1] and n not in VIEWS
        grads[n] = jnp.stack([g.T if back else g for g in (red0[i], red1[i])])
    for n in BIG:
        delta[n], new_m[n], new_v[n] = _adamw(wv[n], grads[n], mv[n], vv[n], f"adamw_{n}")

    return (loss, dx[None], *[_view(n, d[n]) for d in (grads, delta, new_m, new_v) for n in WEIGHTS])
```

```python
import functools
import math

import jax
import jax.numpy as jnp
from jax import lax
from jax.experimental import pallas as pl
from jax.experimental.pallas import tpu as pltpu

F32 = jnp.float32
BF16 = jnp.bfloat16

D_MODEL = 2048
DEPTH = 2
EPS = 1e-6
NEG_INF = -1e30
N_DEV = 8

SSM_WIDTH = 1024
SSM_GROUP = 16
SSM_GROUPS = 64
SSM_STATE = 64
N_SLAB = 8
SLAB_CH = 128
SLAB_ST = 512
SUB = 8
N_GRP = 2
N_SEG = SUB * N_GRP

SG_HEADS = 8
CHUNK = 128
HEAD_DIM = 64
ATT_HEADS = 16
ROT_DIM = 16
ROPE_THETA = 500000.0

D_IN = 13568
OFF_UA, OFF_ZA, OFF_UB, OFF_VB, OFF_ZB, OFF_Q, OFF_KV, OFF_ZC, OFF_G = (
    0, 1024, 2048, 3072, 4096, 5120, 6144, 6400, 7424)

ADAM_LR, ADAM_B1, ADAM_B2, ADAM_EPS, ADAM_WD, ADAM_STEP = 0.001, 0.9, 0.999, 1e-08, 0.01, 10

VMEM_LIMIT = 56 * 1024 * 1024


def _cp(sem=None):
    return pltpu.CompilerParams(dimension_semantics=sem, vmem_limit_bytes=VMEM_LIMIT)


def _dot(a, b):
    return jnp.dot(a, b, preferred_element_type=F32)


def _dot_nt(a, b):
    return lax.dot_general(a, b, (((1,), (1,)), ((), ())), preferred_element_type=F32)


def _dot_tn(a, b):
    return lax.dot_general(a, b, (((0,), (0,)), ((), ())), preferred_element_type=F32)


def _mm(a, b, mode, out_dtype, tm, tn, tk, name, res=None, after=None):
    if mode == "nn":
        (m, k), (_, n) = a.shape, b.shape
    elif mode == "nt":
        (m, k), (n, _) = a.shape, b.shape
    else:
        (k, m), (_, n) = a.shape, b.shape
    tm, tn, tk = min(tm, m), min(tn, n), min(tk, k)
    assert m % tm == 0 and n % tn == 0 and k % tk == 0, (name, m, n, k, tm, tn, tk)
    nk = k // tk
    a_spec = {"nn": pl.BlockSpec((tm, tk), lambda i, j, kk: (i, kk)),
              "nt": pl.BlockSpec((tm, tk), lambda i, j, kk: (i, kk)),
              "tn": pl.BlockSpec((tk, tm), lambda i, j, kk: (kk, i))}[mode]
    b_spec = {"nn": pl.BlockSpec((tk, tn), lambda i, j, kk: (kk, j)),
              "nt": pl.BlockSpec((tn, tk), lambda i, j, kk: (j, kk)),
              "tn": pl.BlockSpec((tk, tn), lambda i, j, kk: (kk, j))}[mode]
    dot = {"nn": _dot, "nt": _dot_nt, "tn": _dot_tn}[mode]
    has_res = res is not None
    direct = out_dtype == F32 and not has_res

    def body(*refs):
        ins, outs = refs[:2 + has_res + (after is not None)], refs[2 + has_res + (after is not None):]
        a_ref, b_ref = ins[:2]
        r_ref = ins[2] if has_res else None
        o_ref = outs[0]
        acc = o_ref if direct else outs[1]
        kk = pl.program_id(2)

        @pl.when(kk == 0)
        def _():
            acc[...] = jnp.zeros_like(acc)

        acc[...] += dot(a_ref[...].astype(BF16), b_ref[...].astype(BF16))

        if not direct:
            @pl.when(kk == nk - 1)
            def _():
                r = acc[...]
                if has_res:
                    r = r + r_ref[...]
                o_ref[...] = r.astype(out_dtype)

    in_specs = [a_spec, b_spec]
    args = [a, b]
    if has_res:
        in_specs.append(pl.BlockSpec((tm, tn), lambda i, j, kk: (i, j)))
        args.append(res)
    if after is not None:
        in_specs.append(pl.BlockSpec(memory_space=pl.ANY))
        args.append(after)
    return pl.pallas_call(
        body, name=name,
        grid=(m // tm, n // tn, nk),
        in_specs=in_specs,
        out_specs=pl.BlockSpec((tm, tn), lambda i, j, kk: (i, j)),
        out_shape=jax.ShapeDtypeStruct((m, n), out_dtype),
        scratch_shapes=[] if direct else [pltpu.VMEM((tm, tn), F32)],
        compiler_params=_cp(("parallel", "parallel", "arbitrary")),
    )(*args)


def _rms(x, w):
    return x * lax.rsqrt(jnp.mean(x * x, axis=-1, keepdims=True) + EPS) * w


def _rms_fwd(x, w, name):
    L, D = x.shape
    tm = min(L, 256)

    def body(x_ref, w_ref, h_ref):
        h_ref[...] = _rms(x_ref[...], w_ref[...]).astype(BF16)

    return pl.pallas_call(
        body, name=name, grid=(L // tm,),
        in_specs=[pl.BlockSpec((tm, D), lambda i: (i, 0)), pl.BlockSpec((1, D), lambda i: (0, 0))],
        out_specs=pl.BlockSpec((tm, D), lambda i: (i, 0)),
        out_shape=jax.ShapeDtypeStruct((L, D), BF16),
        compiler_params=_cp(("parallel",)),
    )(x, w)


def _rms_bwd(x, w, dh, dres, name):
    L, D = x.shape
    tm = min(L, 256)

    def body(x_ref, w_ref, dh_ref, dres_ref, dx_ref, dw_ref):
        _, vjp = jax.vjp(_rms, x_ref[...], w_ref[...])
        dx, dw = vjp(dh_ref[...])
        dx_ref[...] = dx + dres_ref[...]

        @pl.when(pl.program_id(0) == 0)
        def _():
            dw_ref[...] = jnp.zeros_like(dw_ref)

        dw_ref[...] += dw

    row = pl.BlockSpec((tm, D), lambda i: (i, 0))
    vec = pl.BlockSpec((1, D), lambda i: (0, 0))
    return pl.pallas_call(
        body, name=name, grid=(L // tm,),
        in_specs=[row, vec, row, row],
        out_specs=[row, vec],
        out_shape=[jax.ShapeDtypeStruct((L, D), F32), jax.ShapeDtypeStruct((1, D), F32)],
        compiler_params=_cp(("arbitrary",)),
    )(x, w, dh, dres)


def _final(x, fw, tgt, name):
    L, D = x.shape
    tm = min(L, 256)

    def loss_fn(xv, wv, tv):
        err = _rms(xv, wv) - tv
        return jnp.sum(err * err) * (0.5 / D)

    def body(x_ref, w_ref, t_ref, loss_ref, dx_ref, dw_ref):
        tv = t_ref[...]
        val, vjp = jax.vjp(lambda a, b: loss_fn(a, b, tv), x_ref[...], w_ref[...])
        dx, dw = vjp(jnp.ones((), F32))
        dx_ref[...] = dx

        @pl.when(pl.program_id(0) == 0)
        def _():
            dw_ref[...] = jnp.zeros_like(dw_ref)
            loss_ref[...] = jnp.zeros_like(loss_ref)

        dw_ref[...] += dw
        loss_ref[...] += jnp.full(loss_ref.shape, val, F32)

    row = pl.BlockSpec((tm, D), lambda i: (i, 0))
    vec = pl.BlockSpec((1, D), lambda i: (0, 0))
    return pl.pallas_call(
        body, name=name, grid=(L // tm,),
        in_specs=[row, vec, row],
        out_specs=[pl.BlockSpec((8, 128), lambda i: (0, 0)), row, vec],
        out_shape=[jax.ShapeDtypeStruct((8, 128), F32), jax.ShapeDtypeStruct((L, D), F32),
                   jax.ShapeDtypeStruct((1, D), F32)],
        compiler_params=_cp(("arbitrary",)),
    )(x, fw, tgt)


def _s5_param_fn(a_re, a_im, log_dt, bt_re, bt_im):
    dt = jnp.exp(log_dt)
    zr, zi = a_re * dt, a_im * dt
    er = jnp.exp(zr)
    lr, li = er * jnp.cos(zi), er * jnp.sin(zi)
    nr, ni = lr - 1.0, li
    den = a_re * a_re + a_im * a_im
    cr = (nr * a_re + ni * a_im) / den
    ci = (ni * a_re - nr * a_im) / den
    bbr = cr[None] * bt_re - ci[None] * bt_im
    bbi = cr[None] * bt_im + ci[None] * bt_re
    return lr, li, bbr, bbi


def _s5_params_fwd(a_re, a_im, log_dt, bt_re, bt_im, name):
    def body(ar, ai, ld, br, bi, lr, li, bbr, bbi):
        o = _s5_param_fn(ar[...], ai[...], ld[...], br[...], bi[...])
        lr[...], li[...], bbr[...], bbi[...] = o

    gp = jax.ShapeDtypeStruct(a_re.shape, F32)
    cgp = jax.ShapeDtypeStruct(bt_re.shape, F32)
    return pl.pallas_call(body, name=name, out_shape=[gp, gp, cgp, cgp])(a_re, a_im, log_dt, bt_re, bt_im)


def _s5_params_bwd(a_re, a_im, log_dt, bt_re, bt_im, dlr, dli, dbbr, dbbi, name):
    def body(ar, ai, ld, br, bi, g0, g1, g2, g3, o0, o1, o2, o3, o4):
        _, vjp = jax.vjp(_s5_param_fn, ar[...], ai[...], ld[...], br[...], bi[...])
        o0[...], o1[...], o2[...], o3[...], o4[...] = vjp((g0[...], g1[...], g2[...], g3[...]))

    gp = jax.ShapeDtypeStruct(a_re.shape, F32)
    cgp = jax.ShapeDtypeStruct(bt_re.shape, F32)
    return pl.pallas_call(body, name=name,
                          out_shape=[gp, gp, jax.ShapeDtypeStruct(log_dt.shape, F32), cgp, cgp])(
        a_re, a_im, log_dt, bt_re, bt_im, dlr, dli, dbbr, dbbi)


def _cmul(ar, ai, br, bi):
    return ar * br - ai * bi, ar * bi + ai * br


def _cpow(lr, li, n):
    rr, ri = None, None
    br, bi = lr, li
    while n:
        if n & 1:
            rr, ri = (br, bi) if rr is None else _cmul(rr, ri, br, bi)
        n >>= 1
        if n:
            br, bi = _cmul(br, bi, br, bi)
    return rr, ri


def _shift_rows(x, up):
    row = lax.broadcasted_iota(jnp.int32, x.shape, 0)
    if up:
        return jnp.where(row == SUB - 1, 0.0, pltpu.roll(x, SUB - 1, 0))
    return jnp.where(row == 0, 0.0, pltpu.roll(x, 1, 0))


NT = SLAB_ST // 128


def _lam_tiles(lr_ref, li_ref):
    return [(lr_ref[:, j * 128:(j + 1) * 128], li_ref[:, j * 128:(j + 1) * 128]) for j in range(NT)]


def _row_on_sublanes(ref, j, t):
    return ref[j, pl.ds(t, SUB, stride=0), :]


def _pow_table(pw_re, pw_im, lam_t, seg):
    assert seg % 8 == 0 and (seg // 8) & (seg // 8 - 1) == 0
    for j in range(NT):
        lr, li = lam_t[j][0][0:1], lam_t[j][1][0:1]
        r, i_ = lr, li
        for row in range(8):
            pw_re[j, row:row + 1, :] = r
            pw_im[j, row:row + 1, :] = i_
            if row < 7:
                r, i_ = _cmul(r, i_, lr, li)
        n = 8
        while n < seg:
            qr, qi = _cpow(lr, li, n)
            nr, ni = _cmul(pw_re[j, 0:n, :], pw_im[j, 0:n, :], qr, qi)
            pw_re[j, n:2 * n, :] = nr
            pw_im[j, n:2 * n, :] = ni
            n *= 2


def _seg_scan(s_re, s_im, lam_t, pw_re, pw_im, seg, reverse, prev=None):
    sgn = -1.0 if reverse else 1.0
    lt = [(lr, sgn * li) for lr, li in lam_t]
    tiles = [(g, j) for g in range(N_GRP) for j in range(NT)]
    zeros = jnp.zeros((SUB, 128), F32)

    def rows(g, i):
        return pl.ds(pl.multiple_of((g * seg + i) * SUB, SUB), SUB)

    def step1(t, carry):
        i = seg - 1 - t if reverse else t
        out = []
        for n, (g, j) in enumerate(tiles):
            nr, ni = _cmul(lt[j][0], lt[j][1], carry[2 * n], carry[2 * n + 1])
            nr = nr + s_re[j, rows(g, i), :]
            ni = ni + s_im[j, rows(g, i), :]
            s_re[j, rows(g, i), :] = nr
            s_im[j, rows(g, i), :] = ni
            out += [nr, ni]
        return tuple(out)

    zero = tuple(zeros for _ in range(2 * len(tiles)))
    ends = lax.fori_loop(0, seg, step1, zero)

    carries = [None] * (2 * len(tiles))
    row = lax.broadcasted_iota(jnp.int32, (SUB, 128), 0)
    dist = (SUB - 1 - row) if reverse else row
    edge = 0 if reverse else SUB - 1
    for j in range(NT):
        pr, pi = _cpow(lt[j][0], lt[j][1], seg)
        qr, qi = jnp.ones((SUB, 128), F32), zeros
        for s in range(1, SUB):
            tr, ti = _cmul(qr, qi, pr, pi)
            qr, qi = jnp.where(dist >= s, tr, qr), jnp.where(dist >= s, ti, qi)
        boundary = None
        for g in (reversed(range(N_GRP)) if reverse else range(N_GRP)):
            n = g * NT + j
            cr, ci = zeros, zeros
            for _ in range(SUB - 1):
                tr, ti = _cmul(pr, pi, cr, ci)
                cr = _shift_rows(tr + ends[2 * n], reverse)
                ci = _shift_rows(ti + ends[2 * n + 1], reverse)
            if boundary is not None:
                tr, ti = _cmul(qr, qi, boundary[0], boundary[1])
                cr, ci = cr + tr, ci + ti
            carries[2 * n], carries[2 * n + 1] = cr, ci
            fr, fi = _cmul(pr, pi, cr, ci)
            boundary = (jnp.broadcast_to((fr + ends[2 * n])[edge:edge + 1], (SUB, 128)),
                        jnp.broadcast_to((fi + ends[2 * n + 1])[edge:edge + 1], (SUB, 128)))

    def fix(t, i, acc, before):
        out = []
        pws = [(_row_on_sublanes(pw_re, j, t), sgn * _row_on_sublanes(pw_im, j, t)) for j in range(NT)]
        for n, (g, j) in enumerate(tiles):
            ar, ai = _cmul(pws[j][0], pws[j][1], carries[2 * n], carries[2 * n + 1])
            ar = ar + s_re[j, rows(g, i), :]
            ai = ai + s_im[j, rows(g, i), :]
            s_re[j, rows(g, i), :] = ar
            s_im[j, rows(g, i), :] = ai
            if before is not None:
                qr, qi = before(n)
                out += [acc[2 * n] + ar * qr + ai * qi, acc[2 * n + 1] + ai * qr - ar * qi]
        return tuple(out)

    if prev is None:
        lax.fori_loop(0, seg, lambda t, c: fix(t, seg - 1 - t if reverse else t, c, None), ())
        return carries
    assert reverse
    p_re, p_im, p_carries = prev

    def earlier(t):
        return lambda n: (p_re[tiles[n][1], rows(tiles[n][0], seg - 2 - t), :],
                          p_im[tiles[n][1], rows(tiles[n][0], seg - 2 - t), :])

    acc = lax.fori_loop(0, seg - 1, lambda t, c: fix(t, seg - 1 - t, c, earlier(t)), zero)
    acc = fix(seg - 1, 0, acc, lambda n: (p_carries[2 * n], p_carries[2 * n + 1]))
    return carries, [sum(acc[2 * (g * NT + j) + part] for g in range(N_GRP)) for j in range(NT) for part in range(2)]


def _seg_slice(k, seg):
    g, r = divmod(k, SUB)
    return pl.ds(g * seg * SUB + r, seg, stride=SUB)


def _seg_rows(ref, k, seg):
    return jnp.concatenate([ref[j, _seg_slice(k, seg), :] for j in range(NT)], axis=-1)


def _seg_store(ref, k, seg, val):
    for j in range(NT):
        ref[j, _seg_slice(k, seg), :] = val[:, j * 128:(j + 1) * 128]


def _s5_specs(L):
    col = lambda off: pl.BlockSpec((L, SLAB_CH), lambda j: (0, off + j))
    mat_b = pl.BlockSpec((None, SLAB_CH, SLAB_ST), lambda j: (j, 0, 0))
    mat_c = pl.BlockSpec((None, SLAB_ST, SLAB_CH), lambda j: (j, 0, 0))
    vec_s = pl.BlockSpec((None, SUB, SLAB_ST), lambda j: (j, 0, 0))
    vec_c = pl.BlockSpec((None, 1, SLAB_CH), lambda j: (j, 0, 0))
    return col, mat_b, mat_c, vec_s, vec_c


def _s5_states(u_ref, bre_ref, bim_ref, lam_t, pw_re, pw_im, s_re, s_im, seg):
    _pow_table(pw_re, pw_im, lam_t, seg)
    for k in range(N_SEG):
        uk = u_ref[pl.ds(k * seg, seg), :]
        _seg_store(s_re, k, seg, _dot(uk, bre_ref[...]))
        _seg_store(s_im, k, seg, _dot(uk, bim_ref[...]))
    return _seg_scan(s_re, s_im, lam_t, pw_re, pw_im, seg, reverse=False)


def _s5_fwd(proj, bre, bim, cre_t, cim_t, lam_re, lam_im, dvec, name):
    L = proj.shape[0]
    seg = L // N_SEG
    col, mat_b, mat_c, vec_s, vec_c = _s5_specs(L)
    rows = N_SEG * seg

    def body(u_ref, bre_ref, bim_ref, cre_ref, cim_ref, lr_ref, li_ref, d_ref, y_ref, s_re, s_im, pw_re, pw_im):
        _s5_states(u_ref, bre_ref, bim_ref, _lam_tiles(lr_ref, li_ref), pw_re, pw_im, s_re, s_im, seg)
        for k in range(N_SEG):
            y = (_dot(_seg_rows(s_re, k, seg).astype(BF16), cre_ref[...])
                 - _dot(_seg_rows(s_im, k, seg).astype(BF16), cim_ref[...]))
            y = y + d_ref[...] * u_ref[pl.ds(k * seg, seg), :].astype(F32)
            y_ref[pl.ds(k * seg, seg), :] = jax.nn.gelu(y).astype(BF16)

    return pl.pallas_call(
        body, name=name, grid=(N_SLAB,),
        in_specs=[col(OFF_UA // SLAB_CH), mat_b, mat_b, mat_c, mat_c, vec_s, vec_s, vec_c],
        out_specs=pl.BlockSpec((L, SLAB_CH), lambda j: (0, j)),
        out_shape=jax.ShapeDtypeStruct((L, SSM_WIDTH), BF16),
        scratch_shapes=[pltpu.VMEM((NT, rows, 128), F32)] * 2 + [pltpu.VMEM((NT, seg, 128), F32)] * 2,
        compiler_params=_cp(("parallel",)),
    )(proj, bre, bim, cre_t, cim_t, lam_re, lam_im, dvec)


def _s5_bwd(proj, dy, bre, bim, cre_t, cim_t, lam_re, lam_im, dvec, name):
    L = proj.shape[0]
    seg = L // N_SEG
    col, mat_b, mat_c, vec_s, vec_c = _s5_specs(L)
    rows = N_SEG * seg
    dlam_spec = pl.BlockSpec((None, 1, SLAB_ST), lambda j: (j, 0, 0))

    def body(u_ref, dy_ref, bre_ref, bim_ref, cre_ref, cim_ref, lr_ref, li_ref, d_ref,
             du_ref, dbre_ref, dbim_ref, dcre_ref, dcim_ref, dlr_ref, dli_ref, dd_ref,
             s_re, s_im, a_re, a_im, pw_re, pw_im, dyp):
        lam_t = _lam_tiles(lr_ref, li_ref)
        carry_s = _s5_states(u_ref, bre_ref, bim_ref, lam_t, pw_re, pw_im, s_re, s_im, seg)
        dcre = jnp.zeros((SLAB_ST, SLAB_CH), F32)
        dcim = jnp.zeros((SLAB_ST, SLAB_CH), F32)
        dd = jnp.zeros((1, SLAB_CH), F32)
        for k in range(N_SEG):
            sre = _seg_rows(s_re, k, seg).astype(BF16)
            sim = _seg_rows(s_im, k, seg).astype(BF16)
            uk = u_ref[pl.ds(k * seg, seg), :].astype(F32)
            ypre = _dot(sre, cre_ref[...]) - _dot(sim, cim_ref[...]) + d_ref[...] * uk
            _, vjp = jax.vjp(jax.nn.gelu, ypre)
            (dyk,) = vjp(dy_ref[pl.ds(k * seg, seg), :].astype(F32))
            dyp[pl.ds(k * seg, seg), :] = dyk
            dd = dd + jnp.sum(dyk * uk, axis=0, keepdims=True)
            dyb = dyk.astype(BF16)
            dcre = dcre + _dot_tn(sre, dyb)
            dcim = dcim - _dot_tn(sim, dyb)
            _seg_store(a_re, k, seg, _dot_nt(dyb, cre_ref[...]))
            _seg_store(a_im, k, seg, -_dot_nt(dyb, cim_ref[...]))
        dcre_ref[...] = dcre
        dcim_ref[...] = dcim
        dd_ref[...] = dd

        _, acc = _seg_scan(a_re, a_im, lam_t, pw_re, pw_im, seg, reverse=True, prev=(s_re, s_im, carry_s))
        dlr_ref[...] = jnp.concatenate([jnp.sum(acc[2 * j], axis=0, keepdims=True) for j in range(NT)], axis=-1)
        dli_ref[...] = jnp.concatenate([jnp.sum(acc[2 * j + 1], axis=0, keepdims=True) for j in range(NT)], axis=-1)

        dbre = jnp.zeros((SLAB_CH, SLAB_ST), F32)
        dbim = jnp.zeros((SLAB_CH, SLAB_ST), F32)
        for k in range(N_SEG):
            are = _seg_rows(a_re, k, seg).astype(BF16)
            aim = _seg_rows(a_im, k, seg).astype(BF16)
            uk = u_ref[pl.ds(k * seg, seg), :]
            du = _dot_nt(are, bre_ref[...]) + _dot_nt(aim, bim_ref[...]) + dyp[pl.ds(k * seg, seg), :] * d_ref[...]
            du_ref[pl.ds(k * seg, seg), :] = du.astype(BF16)
            dbre = dbre + _dot_tn(uk, are)
            dbim = dbim + _dot_tn(uk, aim)
        dbre_ref[...] = dbre
        dbim_ref[...] = dbim

    scan_buf = pltpu.VMEM((NT, rows, 128), F32)
    pow_buf = pltpu.VMEM((NT, seg, 128), F32)
    return pl.pallas_call(
        body, name=name, grid=(N_SLAB,),
        in_specs=[col(OFF_UA // SLAB_CH), pl.BlockSpec((L, SLAB_CH), lambda j: (0, j)),
                  mat_b, mat_b, mat_c, mat_c, vec_s, vec_s, vec_c],
        out_specs=[pl.BlockSpec((L, SLAB_CH), lambda j: (0, j)), mat_b, mat_b, mat_c, mat_c, dlam_spec, dlam_spec, vec_c],
        out_shape=[jax.ShapeDtypeStruct((L, SSM_WIDTH), BF16),
                   jax.ShapeDtypeStruct((N_SLAB, SLAB_CH, SLAB_ST), F32),
                   jax.ShapeDtypeStruct((N_SLAB, SLAB_CH, SLAB_ST), F32),
                   jax.ShapeDtypeStruct((N_SLAB, SLAB_ST, SLAB_CH), F32),
                   jax.ShapeDtypeStruct((N_SLAB, SLAB_ST, SLAB_CH), F32),
                   jax.ShapeDtypeStruct((N_SLAB, 1, SLAB_ST), F32),
                   jax.ShapeDtypeStruct((N_SLAB, 1, SLAB_ST), F32),
                   jax.ShapeDtypeStruct((N_SLAB, 1, SLAB_CH), F32)],
        scratch_shapes=[scan_buf, scan_buf, scan_buf, scan_buf, pow_buf, pow_buf, pltpu.VMEM((L, SLAB_CH), F32)],
        compiler_params=_cp(("parallel",)),
    )(proj, dy, bre, bim, cre_t, cim_t, lam_re, lam_im, dvec)


def _glu_point(y0, pre, za, b):
    return y0 * jax.nn.sigmoid(pre + b) * jax.nn.silu(za)


def _glu_specs(L, tm):
    row = pl.BlockSpec((tm, SSM_WIDTH), lambda i: (i, 0))
    za = pl.BlockSpec((tm, SSM_WIDTH), lambda i: (i, OFF_ZA // SSM_WIDTH))
    wmat = pl.BlockSpec((SSM_WIDTH, SSM_WIDTH), lambda i: (0, 0))
    vec = pl.BlockSpec((1, SSM_WIDTH), lambda i: (0, 0))
    return row, za, wmat, vec


def _glu_fwd(ya0, proj, w, b, name):
    L = ya0.shape[0]
    tm = min(L, 512)
    row, za, wmat, vec = _glu_specs(L, tm)

    def body(y_ref, z_ref, w_ref, b_ref, o_ref):
        y0 = y_ref[...]
        pre = _dot(y0, w_ref[...])
        o_ref[...] = _glu_point(y0.astype(F32), pre, z_ref[...].astype(F32), b_ref[...]).astype(BF16)

    return pl.pallas_call(
        body, name=name, grid=(L // tm,), in_specs=[row, za, wmat, vec], out_specs=row,
        out_shape=jax.ShapeDtypeStruct((L, SSM_WIDTH), BF16), compiler_params=_cp(("parallel",)),
    )(ya0, proj, w, b)


def _glu_bwd(ya0, proj, w, b, dya, name):
    L = ya0.shape[0]
    tm = min(L, 512)
    row, za, wmat, vec = _glu_specs(L, tm)

    def body(y_ref, z_ref, w_ref, b_ref, g_ref, dy0_ref, dza_ref, dw_ref, db_ref):
        y0 = y_ref[...]
        pre = _dot(y0, w_ref[...])
        _, vjp = jax.vjp(_glu_point, y0.astype(F32), pre, z_ref[...].astype(F32), b_ref[...])
        dy0, dpre, dza, db = vjp(g_ref[...].astype(F32))
        dpb = dpre.astype(BF16)
        dy0_ref[...] = (dy0 + _dot_nt(dpb, w_ref[...])).astype(BF16)
        dza_ref[...] = dza.astype(BF16)

        @pl.when(pl.program_id(0) == 0)
        def _():
            dw_ref[...] = jnp.zeros_like(dw_ref)
            db_ref[...] = jnp.zeros_like(db_ref)

        dw_ref[...] += _dot_tn(y0, dpb)
        db_ref[...] += db

    return pl.pallas_call(
        body, name=name, grid=(L // tm,), in_specs=[row, za, wmat, vec, row],
        out_specs=[row, row, wmat, vec],
        out_shape=[jax.ShapeDtypeStruct((L, SSM_WIDTH), BF16), jax.ShapeDtypeStruct((L, SSM_WIDTH), BF16),
                   jax.ShapeDtypeStruct((SSM_WIDTH, SSM_WIDTH), F32), jax.ShapeDtypeStruct((1, SSM_WIDTH), F32)],
        compiler_params=_cp(("arbitrary",)),
    )(ya0, proj, w, b, dya)


def _sg_norm(vb, ln_w, ln_b):
    v0 = jax.nn.gelu(vb)
    mu = jnp.mean(v0, axis=-1, keepdims=True)
    var = jnp.mean(jnp.square(v0 - mu), axis=-1, keepdims=True)
    return (v0 - mu) * lax.rsqrt(var + EPS) * ln_w + ln_b


def _sg_gate(ub, mixed, zb):
    return jax.nn.gelu(ub) * mixed * jax.nn.silu(zb)


def _sg_specs():
    W = SSM_WIDTH
    blk = lambda off: pl.BlockSpec((CHUNK, W), lambda n: (n, off // W))
    out = pl.BlockSpec((CHUNK, W), lambda n: (n, 0))
    vec = pl.BlockSpec((1, W), lambda n: (0, 0))
    wsp = pl.BlockSpec((SG_HEADS, CHUNK, CHUNK), lambda n: (0, 0, 0))
    bsp = pl.BlockSpec((SG_HEADS, CHUNK, 1), lambda n: (0, 0, 0))
    return blk, out, vec, wsp, bsp


def _sg_masked(w_ref):
    t = lax.broadcasted_iota(jnp.int32, (CHUNK, CHUNK), 0)
    s = lax.broadcasted_iota(jnp.int32, (CHUNK, CHUNK), 1)
    causal = s <= t
    return causal, [jnp.where(causal, w_ref[h], 0.0).astype(BF16) for h in range(SG_HEADS)]


def _sg_mix(wm, vnb, bias_ref):
    return jnp.concatenate(
        [_dot(wm[h], vnb[:, h * CHUNK:(h + 1) * CHUNK]) + bias_ref[h] for h in range(SG_HEADS)], axis=-1)


def _sg_fwd(proj, ln_w, ln_b, w, bias, name):
    L = proj.shape[0]
    blk, out, vec, wsp, bsp = _sg_specs()

    def body(ub_ref, vb_ref, zb_ref, lw_ref, lb_ref, w_ref, bias_ref, o_ref):
        _, wm = _sg_masked(w_ref)
        vnb = _sg_norm(vb_ref[...].astype(F32), lw_ref[...], lb_ref[...]).astype(BF16)
        mixed = _sg_mix(wm, vnb, bias_ref)
        o_ref[...] = _sg_gate(ub_ref[...].astype(F32), mixed, zb_ref[...].astype(F32)).astype(BF16)

    return pl.pallas_call(
        body, name=name, grid=(L // CHUNK,),
        in_specs=[blk(OFF_UB), blk(OFF_VB), blk(OFF_ZB), vec, vec, wsp, bsp], out_specs=out,
        out_shape=jax.ShapeDtypeStruct((L, SSM_WIDTH), BF16), compiler_params=_cp(("parallel",)),
    )(proj, proj, proj, ln_w, ln_b, w, bias)


def _sg_bwd(proj, ln_w, ln_b, w, bias, dyb, name):
    L = proj.shape[0]
    blk, out, vec, wsp, bsp = _sg_specs()

    def body(ub_ref, vb_ref, zb_ref, lw_ref, lb_ref, w_ref, bias_ref, g_ref,
             dub_ref, dvb_ref, dzb_ref, dlw_ref, dlb_ref, dw_ref, dbias_ref):
        causal, wm = _sg_masked(w_ref)
        vb = vb_ref[...].astype(F32)
        vn, vjp_norm = jax.vjp(_sg_norm, vb, lw_ref[...], lb_ref[...])
        vnb = vn.astype(BF16)
        mixed = _sg_mix(wm, vnb, bias_ref)
        _, vjp_gate = jax.vjp(_sg_gate, ub_ref[...].astype(F32), mixed, zb_ref[...].astype(F32))
        dub, dmixed, dzb = vjp_gate(g_ref[...].astype(F32))
        dub_ref[...] = dub.astype(BF16)
        dzb_ref[...] = dzb.astype(BF16)

        @pl.when(pl.program_id(0) == 0)
        def _():
            dlw_ref[...] = jnp.zeros_like(dlw_ref)
            dlb_ref[...] = jnp.zeros_like(dlb_ref)
            dw_ref[...] = jnp.zeros_like(dw_ref)
            dbias_ref[...] = jnp.zeros_like(dbias_ref)

        dvn = []
        for h in range(SG_HEADS):
            dm = dmixed[:, h * CHUNK:(h + 1) * CHUNK]
            dmb = dm.astype(BF16)
            dbias_ref[h] += jnp.sum(dm, axis=-1, keepdims=True)
            dw_ref[h] += jnp.where(causal, _dot_nt(dmb, vnb[:, h * CHUNK:(h + 1) * CHUNK]), 0.0)
            dvn.append(_dot_tn(wm[h], dmb))
        dvb, dlw, dlb = vjp_norm(jnp.concatenate(dvn, axis=-1))
        dvb_ref[...] = dvb.astype(BF16)
        dlw_ref[...] += dlw
        dlb_ref[...] += dlb

    act = jax.ShapeDtypeStruct((L, SSM_WIDTH), BF16)
    return pl.pallas_call(
        body, name=name, grid=(L // CHUNK,),
        in_specs=[blk(OFF_UB), blk(OFF_VB), blk(OFF_ZB), vec, vec, wsp, bsp, out],
        out_specs=[out, out, out, vec, vec, wsp, bsp],
        out_shape=[act, act, act, jax.ShapeDtypeStruct((1, SSM_WIDTH), F32), jax.ShapeDtypeStruct((1, SSM_WIDTH), F32),
                   jax.ShapeDtypeStruct((SG_HEADS, CHUNK, CHUNK), F32), jax.ShapeDtypeStruct((SG_HEADS, CHUNK, 1), F32)],
        compiler_params=_cp(("arbitrary",)),
    )(proj, proj, proj, ln_w, ln_b, w, bias, dyb)


def _rope_tables(L):
    half = ROT_DIM // 2
    inv_freq = ROPE_THETA ** (-jnp.arange(0, ROT_DIM, 2, dtype=F32) / ROT_DIM)
    ang = jnp.arange(L, dtype=F32)[:, None] * inv_freq[None, :]
    cos, sin = jnp.cos(ang), jnp.sin(ang)
    ones = jnp.ones((L, HEAD_DIM - ROT_DIM), F32)
    cos_h = jnp.concatenate([cos, cos, ones], axis=-1)
    sin_h = jnp.concatenate([-sin, sin, 0.0 * ones], axis=-1)
    src = jnp.arange(HEAD_DIM)[:, None]
    dst = jnp.arange(HEAD_DIM)[None, :]
    p_h = (((dst < half) & (src == dst + half)) | ((dst >= half) & (dst < ROT_DIM) & (src == dst - half))).astype(F32)
    p2 = jnp.kron(jnp.eye(2, dtype=F32), p_h).astype(BF16)
    return jnp.tile(cos_h, (1, 2)), jnp.tile(sin_h, (1, 2)), p2


def _rope(t, cos, sin, p2):
    n = t.shape[1] // 128
    tb = t.astype(BF16)
    sw = jnp.concatenate([_dot(tb[:, i * 128:(i + 1) * 128], p2) for i in range(n)], axis=-1) if n > 1 else _dot(tb, p2)
    return t * jnp.tile(cos, (1, n)) + sw * jnp.tile(sin, (1, n))


def _rope_t(g, cos, sin, p2):
    n = g.shape[1] // 128
    gs = (g * jnp.tile(sin, (1, n))).astype(BF16)
    sw = jnp.concatenate([_dot_nt(gs[:, i * 128:(i + 1) * 128], p2) for i in range(n)], axis=-1) if n > 1 else _dot_nt(gs, p2)
    return g * jnp.tile(cos, (1, n)) + sw


def _lane_lo(shape):
    return (lax.broadcasted_iota(jnp.int32, shape, len(shape) - 1) % 128) < HEAD_DIM


def _dup_halves(x):
    xr = pltpu.roll(x, HEAD_DIM, 1)
    lo = _lane_lo(x.shape)
    return jnp.where(lo, x, xr), jnp.where(lo, xr, x)


def _fold_halves(d0, d1):
    f0 = d0 + pltpu.roll(d0, HEAD_DIM, 1)
    f1 = d1 + pltpu.roll(d1, HEAD_DIM, 1)
    return jnp.where(_lane_lo(d0.shape), f0, f1)


def _attn_mask():
    qi = lax.broadcasted_iota(jnp.int32, (CHUNK, 2 * CHUNK), 0)
    kj = lax.broadcasted_iota(jnp.int32, (CHUNK, 2 * CHUNK), 1)
    return qi, kj


def _attn_specs():
    qsp = pl.BlockSpec((CHUNK, 1024), lambda n: (n, OFF_Q // 1024))
    kv_cur = pl.BlockSpec((CHUNK, 256), lambda n: (n, OFF_KV // 256))
    kv_prev = pl.BlockSpec((CHUNK, 256), lambda n: (jnp.maximum(n - 1, 0), OFF_KV // 256))
    zsp = [pl.BlockSpec((CHUNK, 256), functools.partial(lambda n, q: (n, OFF_ZC // 256 + q), q=q)) for q in range(4)]
    tab_cur = pl.BlockSpec((CHUNK, 128), lambda n: (n, 0))
    tab_prev = pl.BlockSpec((CHUNK, 128), lambda n: (jnp.maximum(n - 1, 0), 0))
    p2sp = pl.BlockSpec((128, 128), lambda n: (0, 0))
    sink = pl.BlockSpec(memory_space=pltpu.SMEM)
    wide = pl.BlockSpec((CHUNK, 1024), lambda n: (n, 0))
    return qsp, kv_cur, kv_prev, zsp, tab_cur, tab_prev, p2sp, sink, wide


def _attn_core(n, q_ref, kvc_ref, kvp_ref, cosc_ref, sinc_ref, cosp_ref, sinp_ref, p2_ref, sink_ref):
    p2 = p2_ref[...]
    qr = _rope(q_ref[...].astype(F32), cosc_ref[...], sinc_ref[...], p2).astype(BF16)
    kc = _rope(kvc_ref[:, 0:128].astype(F32), cosc_ref[...], sinc_ref[...], p2)
    kp = _rope(kvp_ref[:, 0:128].astype(F32), cosp_ref[...], sinp_ref[...], p2)
    k_all = jnp.concatenate([kp, kc], axis=0).astype(BF16)
    v_all = jnp.concatenate([kvp_ref[:, 128:256], kvc_ref[:, 128:256]], axis=0)
    kd = _dup_halves(k_all)
    vd = _dup_halves(v_all)
    qi, kj = _attn_mask()
    allowed = ((kj < CHUNK) & (kj > qi) & (n > 0)) | ((kj >= CHUNK) & (kj - CHUNK <= qi))
    lo = _lane_lo((CHUNK, 128))
    probs = []
    for h in range(ATT_HEADS):
        m, half, g = h // 2, h % 2, h // 8
        qp = qr[:, m * 128:(m + 1) * 128]
        qm = jnp.where(lo if half == 0 else ~lo, qp, jnp.zeros_like(qp))
        s = jnp.where(allowed, _dot_nt(qm, kd[g]) * (HEAD_DIM ** -0.5), NEG_INF)
        snk = sink_ref[h]
        mx = jnp.maximum(jnp.max(s, axis=-1, keepdims=True), snk)
        e = jnp.exp(s - mx)
        es = jnp.exp(snk - mx)
        inv = 1.0 / (jnp.sum(e, axis=-1, keepdims=True) + es)
        probs.append((qm, e * inv, es * inv))
    return qr, kd, vd, probs, lo


def _attn_out(vd, probs, lo):
    outs = []
    for m in range(ATT_HEADS // 2):
        g = m // 4
        o0 = _dot(probs[2 * m][1].astype(BF16), vd[g])
        o1 = _dot(probs[2 * m + 1][1].astype(BF16), vd[g])
        outs.append(jnp.where(lo, o0, o1))
    return jnp.concatenate(outs, axis=-1)


def _silu_gate(o, z):
    return o * jax.nn.silu(z)


def _attn_fwd(proj, sinks, tabs, name):
    L = proj.shape[0]
    cos2, sin2, p2 = tabs
    qsp, kv_cur, kv_prev, zsp, tab_cur, tab_prev, p2sp, sink, wide = _attn_specs()

    def body(q_ref, kvc_ref, kvp_ref, z0, z1, z2, z3, cosc, sinc, cosp, sinp, p2_ref, sink_ref, o_ref):
        n = pl.program_id(0)
        _, _, vd, probs, lo = _attn_core(n, q_ref, kvc_ref, kvp_ref, cosc, sinc, cosp, sinp, p2_ref, sink_ref)
        o = _attn_out(vd, probs, lo)
        z = jnp.concatenate([z0[...], z1[...], z2[...], z3[...]], axis=-1).astype(F32)
        o_ref[...] = _silu_gate(o, z).astype(BF16)

    return pl.pallas_call(
        body, name=name, grid=(L // CHUNK,),
        in_specs=[qsp, kv_cur, kv_prev, *zsp, tab_cur, tab_cur, tab_prev, tab_prev, p2sp, sink],
        out_specs=wide, out_shape=jax.ShapeDtypeStruct((L, 1024), BF16), compiler_params=_cp(("parallel",)),
    )(proj, proj, proj, proj, proj, proj, proj, cos2, sin2, cos2, sin2, p2, sinks)


def _attn_bwd(proj, sinks, tabs, dyc, name):
    L = proj.shape[0]
    cos2, sin2, p2 = tabs
    qsp, kv_cur, kv_prev, zsp, tab_cur, tab_prev, p2sp, sink, wide = _attn_specs()
    kvo = pl.BlockSpec((CHUNK, 256), lambda n: (n, 0))

    def body(q_ref, kvc_ref, kvp_ref, z0, z1, z2, z3, cosc, sinc, cosp, sinp, p2_ref, sink_ref, g_ref,
             dq_ref, dz_ref, dkvc_ref, dkvp_ref, dsink_ref):
        n = pl.program_id(0)
        _, kd, vd, probs, lo = _attn_core(n, q_ref, kvc_ref, kvp_ref, cosc, sinc, cosp, sinp, p2_ref, sink_ref)
        o = _attn_out(vd, probs, lo)
        z = jnp.concatenate([z0[...], z1[...], z2[...], z3[...]], axis=-1).astype(F32)
        _, vjp = jax.vjp(_silu_gate, o, z)
        do, dz = vjp(g_ref[...].astype(F32))
        dz_ref[...] = dz.astype(BF16)

        @pl.when(n == 0)
        def _():
            dsink_ref[...] = jnp.zeros_like(dsink_ref)

        dkd = [jnp.zeros((2 * CHUNK, 128), F32), jnp.zeros((2 * CHUNK, 128), F32)]
        dvd = [jnp.zeros((2 * CHUNK, 128), F32), jnp.zeros((2 * CHUNK, 128), F32)]
        dq_pairs = []
        for m in range(ATT_HEADS // 2):
            g = m // 4
            dop = do[:, m * 128:(m + 1) * 128].astype(BF16)
            dq_h = []
            for half in range(2):
                h = 2 * m + half
                qm, p, ps = probs[h]
                dom = jnp.where(lo if half == 0 else ~lo, dop, jnp.zeros_like(dop))
                dp = _dot_nt(dom, vd[g])
                rs = jnp.sum(p * dp, axis=-1, keepdims=True)
                ds = (p * (dp - rs) * (HEAD_DIM ** -0.5)).astype(BF16)
                dsink_ref[h:h + 1, :] += jnp.broadcast_to(jnp.sum(-ps * rs, axis=0, keepdims=True), (1, 128))
                dq_h.append(_dot(ds, kd[g]))
                dkd[g] = dkd[g] + _dot_tn(ds, qm)
                dvd[g] = dvd[g] + _dot_tn(p.astype(BF16), dom)
            dq_pairs.append(jnp.where(lo, dq_h[0], dq_h[1]))
        p2 = p2_ref[...]
        dq_ref[...] = _rope_t(jnp.concatenate(dq_pairs, axis=-1), cosc[...], sinc[...], p2).astype(BF16)
        dk_rot = _fold_halves(dkd[0], dkd[1])
        dv = _fold_halves(dvd[0], dvd[1])
        dkp = _rope_t(dk_rot[0:CHUNK], cosp[...], sinp[...], p2)
        dkc = _rope_t(dk_rot[CHUNK:2 * CHUNK], cosc[...], sinc[...], p2)
        dkvp_ref[...] = jnp.concatenate([dkp, dv[0:CHUNK]], axis=-1)
        dkvc_ref[...] = jnp.concatenate([dkc, dv[CHUNK:2 * CHUNK]], axis=-1)

    act = jax.ShapeDtypeStruct((L, 1024), BF16)
    kvs = jax.ShapeDtypeStruct((L, 256), F32)
    return pl.pallas_call(
        body, name=name, grid=(L // CHUNK,),
        in_specs=[qsp, kv_cur, kv_prev, *zsp, tab_cur, tab_cur, tab_prev, tab_prev, p2sp, sink, wide],
        out_specs=[wide, wide, kvo, kvo, pl.BlockSpec((ATT_HEADS, 128), lambda n: (0, 0))],
        out_shape=[act, act, kvs, kvs, jax.ShapeDtypeStruct((ATT_HEADS, 128), F32)],
        compiler_params=_cp(("arbitrary",)),
    )(proj, proj, proj, proj, proj, proj, proj, cos2, sin2, cos2, sin2, p2, sinks, dyc)


MERGE_TN = 256


def _merge_point(ta, tb, tc, ga, gb, gc):
    return jax.nn.sigmoid(ga) * ta + jax.nn.sigmoid(gb) * tb + jax.nn.sigmoid(gc) * tc


def _merge_specs(tm):
    nj = D_MODEL // MERGE_TN
    t = pl.BlockSpec((tm, MERGE_TN), lambda i, j: (i, j))
    gates = [pl.BlockSpec((tm, MERGE_TN), functools.partial(lambda i, j, b: (i, OFF_G // MERGE_TN + b * nj + j), b=b))
             for b in range(3)]
    return t, gates, nj


def _merge_fwd(ta, tb, tc, proj, name):
    L = ta.shape[0]
    tm = min(L, 1024)
    t, gates, nj = _merge_specs(tm)

    def body(ta_ref, tb_ref, tc_ref, ga_ref, gb_ref, gc_ref, o_ref):
        f = lambda r: r[...].astype(F32)
        o_ref[...] = _merge_point(f(ta_ref), f(tb_ref), f(tc_ref), f(ga_ref), f(gb_ref), f(gc_ref)).astype(BF16)

    return pl.pallas_call(
        body, name=name, grid=(L // tm, nj), in_specs=[t, t, t, *gates], out_specs=t,
        out_shape=jax.ShapeDtypeStruct((L, D_MODEL), BF16), compiler_params=_cp(("parallel", "parallel")),
    )(ta, tb, tc, proj, proj, proj)


def _merge_bwd(ta, tb, tc, proj, dm, name):
    L = ta.shape[0]
    tm = min(L, 1024)
    t, gates, nj = _merge_specs(tm)

    def body(ta_ref, tb_ref, tc_ref, ga_ref, gb_ref, gc_ref, dm_ref, dta_ref, dtb_ref, dtc_ref, dga_ref, dgb_ref, dgc_ref):
        f = lambda r: r[...].astype(F32)
        _, vjp = jax.vjp(_merge_point, f(ta_ref), f(tb_ref), f(tc_ref), f(ga_ref), f(gb_ref), f(gc_ref))
        outs = vjp(f(dm_ref))
        for r, v in zip((dta_ref, dtb_ref, dtc_ref, dga_ref, dgb_ref, dgc_ref), outs):
            r[...] = v.astype(BF16)

    act = jax.ShapeDtypeStruct((L, D_MODEL), BF16)
    return pl.pallas_call(
        body, name=name, grid=(L // tm, nj), in_specs=[t, t, t, *gates, t],
        out_specs=[t] * 6, out_shape=[act] * 6,
        compiler_params=_cp(("parallel", "parallel")),
    )(ta, tb, tc, proj, proj, proj, dm)


GRAD_DT = BF16
SMALL = ("norm_w", "ssm_a_re", "ssm_a_im", "ssm_log_dt", "ssm_b_re", "ssm_b_im", "ssm_c_re", "ssm_c_im", "ssm_d",
         "ssm_glu_b", "sg_ln_w", "sg_ln_b", "sg_w", "sg_b", "attn_sinks")
G8 = SSM_GROUPS // N_SLAB


def _diag_mask(rows_per_group, cols_per_group):
    r = jnp.arange(G8 * rows_per_group)[:, None] // rows_per_group
    c = jnp.arange(G8 * cols_per_group)[None, :] // cols_per_group
    return r == c


def _slab_b(bb_t):
    x = bb_t.transpose(1, 0, 2).reshape(N_SLAB, SLAB_CH, SSM_STATE)
    return jnp.where(_diag_mask(SSM_GROUP, SSM_STATE), jnp.tile(x, (1, 1, G8)), 0)


def _unslab_b(d):
    x = jnp.where(_diag_mask(SSM_GROUP, SSM_STATE), d, 0).reshape(N_SLAB, SLAB_CH, G8, SSM_STATE).sum(axis=2)
    return x.reshape(SSM_GROUPS, SSM_GROUP, SSM_STATE).transpose(1, 0, 2)


def _slab_c(c):
    x = c.transpose(0, 2, 1).reshape(N_SLAB, SLAB_ST, SSM_GROUP)
    return jnp.where(_diag_mask(SSM_STATE, SSM_GROUP), jnp.tile(x, (1, 1, G8)), 0)


def _unslab_c(d):
    x = jnp.where(_diag_mask(SSM_STATE, SSM_GROUP), d, 0).reshape(N_SLAB, SLAB_ST, G8, SSM_GROUP).sum(axis=2)
    return x.reshape(SSM_GROUPS, SSM_STATE, SSM_GROUP).transpose(0, 2, 1)


def _s5_prep(p, tag):
    bt_re = p["ssm_b_re"].transpose(2, 0, 1)
    bt_im = p["ssm_b_im"].transpose(2, 0, 1)
    raw = (p["ssm_a_re"], p["ssm_a_im"], p["ssm_log_dt"][:, None], bt_re, bt_im)
    lr, li, bbr, bbi = _s5_params_fwd(*raw, name=f"s5_params_{tag}")
    ops = (_slab_b(bbr).astype(BF16), _slab_b(bbi).astype(BF16),
           _slab_c(p["ssm_c_re"]).astype(BF16), _slab_c(p["ssm_c_im"]).astype(BF16),
           jnp.broadcast_to(lr.reshape(N_SLAB, 1, SLAB_ST), (N_SLAB, SUB, SLAB_ST)),
           jnp.broadcast_to(li.reshape(N_SLAB, 1, SLAB_ST), (N_SLAB, SUB, SLAB_ST)),
           p["ssm_d"].reshape(N_SLAB, 1, SLAB_CH))
    return raw, ops


def _layer_fwd(x, p, w, tabs, tag, s5=None, win_of=None, after_proj=None):
    L = x.shape[0]
    h = _rms_fwd(x, p["norm_w"][None], f"rms_fwd_{tag}")
    win_t, proj_after = win_of(h) if win_of is not None else (w["win_t"], None)
    proj = _mm(h, win_t, "nt", BF16, L, 256, D_MODEL, f"in_proj_{tag}", after=proj_after)
    if after_proj is not None:
        w = after_proj(proj)
    s5_raw, s5_ops = s5 if s5 is not None else _s5_prep(p, tag)
    ya0 = _s5_fwd(proj, *s5_ops, name=f"s5_fwd_{tag}")
    ya = _glu_fwd(ya0, proj, w["glu"], p["ssm_glu_b"][None], f"glu_fwd_{tag}")
    yb = _sg_fwd(proj, p["sg_ln_w"][None], p["sg_ln_b"][None], p["sg_w"], p["sg_b"][:, :, None], f"sg_fwd_{tag}")
    yc = _attn_fwd(proj, p["attn_sinks"], tabs, f"attn_fwd_{tag}")
    ta = _mm(ya, w["wba_t"], "nt", BF16, 1024, 1024, 1024, f"branch_a_{tag}")
    tb = _mm(yb, w["wbb_t"], "nt", BF16, 1024, 1024, 1024, f"branch_b_{tag}")
    tc = _mm(yc, w["wbc_t"], "nt", BF16, 1024, 1024, 1024, f"branch_c_{tag}")
    merged = _merge_fwd(ta, tb, tc, proj, f"merge_fwd_{tag}")
    x_new = _mm(merged, w["wout"], "nn", F32, 1024, 512, D_MODEL, f"out_proj_{tag}", res=x)
    saved = dict(x=x, h=h, proj=proj, s5_raw=s5_raw, s5_ops=s5_ops, ya0=ya0, ya=ya, yb=yb, yc=yc,
                 ta=ta, tb=tb, tc=tc, merged=merged)
    return x_new, saved


def _layer_bwd(dx_out, p, w, tabs, s, tag, first_after=None, before_win=None, after_win=None):
    L = dx_out.shape[0]
    proj = s["proj"]
    big, small = {}, {}
    dmerged = _mm(dx_out, w["wout"], "nt", BF16, 1024, 512, D_MODEL, f"d_merged_{tag}", after=first_after)
    big["wout"] = _mm(s["merged"], dx_out, "tn", GRAD_DT, 512, 1024, L, f"d_wout_{tag}")
    dta, dtb, dtc, dga, dgb, dgc = _merge_bwd(s["ta"], s["tb"], s["tc"], proj, dmerged, f"merge_bwd_{tag}")
    dy = {}
    for br, dt in (("a", dta), ("b", dtb), ("c", dtc)):
        dy[br] = _mm(dt, w[f"wb{br}_t"], "nn", BF16, 1024, 1024, D_MODEL, f"d_y{br}_{tag}")
        big[f"wb{br}_t"] = _mm(dt, s[f"y{br}"], "tn", GRAD_DT, 512, 1024, L, f"d_wb{br}_{tag}")

    dq, dzc, dkvc, dkvp, dsink = _attn_bwd(proj, p["attn_sinks"], tabs, dy["c"], f"attn_bwd_{tag}")
    dkv = dkvc + jnp.concatenate([dkvp[CHUNK:], jnp.zeros((CHUNK, 256), F32)], axis=0)
    small["attn_sinks"] = dsink[:, 0]

    dub, dvb, dzb, dlw, dlb, dsgw, dsgb = _sg_bwd(
        proj, p["sg_ln_w"][None], p["sg_ln_b"][None], p["sg_w"], p["sg_b"][:, :, None], dy["b"], f"sg_bwd_{tag}")
    small.update(sg_ln_w=dlw[0], sg_ln_b=dlb[0], sg_w=dsgw, sg_b=dsgb[:, :, 0])

    dya0, dza, dglu, dglub = _glu_bwd(s["ya0"], proj, w["glu"], p["ssm_glu_b"][None], dy["a"], f"glu_bwd_{tag}")
    big["glu"] = dglu.astype(GRAD_DT)
    small["ssm_glu_b"] = dglub[0]

    dua, dbre, dbim, dcre, dcim, dlr, dli, dd = _s5_bwd(proj, dya0, *s["s5_ops"], name=f"s5_bwd_{tag}")
    da_re, da_im, dlog_dt, dbt_re, dbt_im = _s5_params_bwd(
        *s["s5_raw"], dlr.reshape(SSM_GROUPS, SSM_STATE), dli.reshape(SSM_GROUPS, SSM_STATE),
        _unslab_b(dbre), _unslab_b(dbim), name=f"s5_params_bwd_{tag}")
    small.update(ssm_a_re=da_re, ssm_a_im=da_im, ssm_log_dt=dlog_dt[:, 0],
                 ssm_bt_re=dbt_re, ssm_bt_im=dbt_im,
                 ssm_c_re=_unslab_c(dcre), ssm_c_im=_unslab_c(dcim), ssm_d=dd.reshape(SSM_WIDTH))

    dproj = jnp.concatenate([dua, dza, dub, dvb, dzb, dq, dkv.astype(BF16), dzc, dga, dgb, dgc], axis=-1)
    tok = before_win(big) if before_win is not None else None
    big["win_t"] = _mm(dproj, s["h"], "tn", GRAD_DT, 256, D_MODEL, L, f"d_win_{tag}", after=tok)
    tok = after_win(big) if after_win is not None else None
    dh = _mm(dproj, w["win_t"], "nn", F32, L, D_MODEL, 256, f"d_h_{tag}", after=tok)
    dx_in, dnw = _rms_bwd(s["x"], p["norm_w"][None], dh, dx_out, f"rms_bwd_{tag}")
    small["norm_w"] = dnw[0]
    return dx_in, big, small


def _local_step(x, tgt, small_p, final_w, big_w):
    L = x.shape[0]
    tabs = _rope_tables(L)
    saved = []
    for l in range(DEPTH):
        x, s = _layer_fwd(x, small_p[l], big_w[l], tabs, f"l{l}")
        saved.append(s)
    loss_acc, dx, dfw = _final(x, final_w[None], tgt, "final_norm_loss")
    big_g, small_g = [None] * DEPTH, [None] * DEPTH
    for l in reversed(range(DEPTH)):
        dx, big_g[l], small_g[l] = _layer_bwd(dx, small_p[l], big_w[l], tabs, saved[l], f"l{l}")
    return loss_acc[0, 0], dx, dfw[0], big_g, small_g


MESH = pl.DeviceIdType.MESH
ANY = pl.BlockSpec(memory_space=pl.ANY)
ROW_ALIGN = 16


def _place():
    return lax.axis_index("x"), lax.axis_index("y"), lax.axis_index("c")


HBM = pl.BlockSpec(memory_space=pltpu.HBM)
SEM = pl.BlockSpec(memory_space=pltpu.SEMAPHORE)
EFFECT = pltpu.SideEffectType.DATAFLOW_SIDE_EFFECTING


def _split_start(srcs, lands, n_copies, copies, name, after=None):
    n, m, k = len(srcs), len(lands), n_copies
    extra = [] if after is None else [after]

    def body(*refs):
        src_refs, land_refs = refs[:n], refs[n:n + m]
        sems = refs[n + m + len(extra):]
        send_sems, recv_sems, token = sems[:k], sems[k:2 * k], refs[-1]
        for cp in copies(src_refs, land_refs, send_sems, recv_sems):
            cp.start()
        token[...] = jnp.zeros_like(token)

    ops = list(srcs) + list(lands)
    outs = pl.pallas_call(
        body, name=name,
        out_shape=(*[pltpu.SemaphoreType.DMA(())] * (2 * k),
                   *[pltpu.HBM(a.shape, a.dtype) for a in ops], jax.ShapeDtypeStruct((8, 128), F32)),
        in_specs=[HBM] * (n + m) + [ANY] * len(extra),
        out_specs=(*[SEM] * (2 * k), *[HBM] * (n + m), pl.BlockSpec(memory_space=pltpu.VMEM)),
        input_output_aliases={i: 2 * k + i for i in range(n + m)},
        compiler_params=pltpu.CompilerParams(has_side_effects=EFFECT),
    )(*[pltpu.with_memory_space_constraint(a, pltpu.HBM) for a in ops], *extra)
    return (list(outs[:k]), list(outs[k:2 * k]), list(outs[2 * k:2 * k + n]), list(outs[2 * k + n:2 * k + n + m]),
            outs[-1])


def _split_wait(send_sems, recv_sems, srcs, lands, after, copies, name):
    n, m, k = len(srcs), len(lands), len(send_sems)
    after = list(after) if isinstance(after, (list, tuple)) else [after]

    def body(*refs):
        src_refs, land_refs = refs[:n], refs[n:n + m]
        for cp in copies(src_refs, land_refs, refs[n + m:n + m + k], refs[n + m + k:n + m + 2 * k]):
            cp.wait_send()
            cp.wait_recv()

    ops = list(srcs) + list(lands)
    outs = pl.pallas_call(
        body, name=name,
        out_shape=tuple(pltpu.HBM(a.shape, a.dtype) for a in ops),
        in_specs=[HBM] * (n + m) + [SEM] * (2 * k) + [ANY] * len(after),
        out_specs=tuple([HBM] * (n + m)),
        input_output_aliases={i: i for i in range(n + m)},
        compiler_params=pltpu.CompilerParams(has_side_effects=EFFECT),
    )(*ops, *send_sems, *recv_sems, *after)
    return list(outs[:n]), list(outs[n:])


def _ag_rows(land_ref, px, py, pc):
    r = land_ref.shape[0] // N_DEV
    start = pl.multiple_of((4 * px + 2 * py + pc) * r, ROW_ALIGN)
    return land_ref.at[pl.ds(start, r), :]


def _ag_copies(src_refs, land_refs, send_sems, recv_sems):
    x, y, c = _place()
    peers = [(x, y, 1 - c), (1 - x, y, c), (x, 1 - y, c), (1 - x, 1 - y, c)]
    return [pltpu.make_async_remote_copy(
        src_ref=_ag_rows(land_refs[a], x, y, c), dst_ref=_ag_rows(land_refs[a], x, y, c),
        send_sem=send_sems[4 * a + k], recv_sem=recv_sems[4 * a + k], device_id=peer, device_id_type=MESH)
        for a in range(len(land_refs)) for k, peer in enumerate(peers)]


def _ag_forward(lands, name):
    n = len(lands)

    def body(*refs):
        land_refs = refs[n:2 * n]
        send_sems, recv_sems = refs[2 * n:]
        x, y, c = _place()
        chips = [(1 - x, y), (x, 1 - y), (1 - x, 1 - y)]

        def copy(a, j, pc):
            px, py = chips[j]
            return pltpu.make_async_remote_copy(
                src_ref=_ag_rows(land_refs[a], px, py, pc), dst_ref=_ag_rows(land_refs[a], px, py, pc),
                send_sem=send_sems.at[a, j], recv_sem=recv_sems.at[a, j], device_id=(x, y, 1 - c), device_id_type=MESH)

        passed = [copy(a, j, c) for a in range(n) for j in range(3)]
        for cp in passed:
            cp.start()
        for a in range(n):
            for j in range(3):
                copy(a, j, 1 - c).wait_recv()
        for cp in passed:
            cp.wait_send()

    return pl.pallas_call(
        body, name=name,
        in_specs=[ANY] * n, out_specs=[ANY] * n,
        out_shape=[jax.ShapeDtypeStruct(l.shape, l.dtype) for l in lands],
        input_output_aliases={i: i for i in range(n)},
        scratch_shapes=[pltpu.SemaphoreType.DMA((n, 3)), pltpu.SemaphoreType.DMA((n, 3))],
    )(*lands)


def _allgather_place(shards):
    x, y, c = _place()
    return [lax.dynamic_update_slice(lax.empty((N_DEV * s.shape[0], s.shape[1]), s.dtype), s,
                                     ((4 * x + 2 * y + c) * s.shape[0], 0)) for s in shards]


def _allgather_start(lands, name, after=None):
    return _split_start([], lands, 4 * len(lands), _ag_copies, name + "_start", after=after)


def _allgather_finish(started, after, name):
    send_sems, recv_sems, _, lands, _ = started
    _, lands = _split_wait(send_sems, recv_sems, [], lands, after, _ag_copies, name + "_wait")
    return list(_ag_forward(lands, name + "_forward"))


def _rs_swap_cores(grads, name):
    n = len(grads)

    def body(*refs):
        ins, outs = refs[:n], refs[n:2 * n]
        send_sems, recv_sems = refs[2 * n:]
        x, y, c = _place()
        cps = []
        for a in range(n):
            r = ins[a].shape[0] // N_DEV
            for q in range(4):
                start = pl.multiple_of((2 * q + 1 - c) * r, ROW_ALIGN)
                cps.append(pltpu.make_async_remote_copy(
                    src_ref=ins[a].at[pl.ds(start, r), :], dst_ref=outs[a].at[q],
                    send_sem=send_sems.at[a, q], recv_sem=recv_sems.at[a, q],
                    device_id=(x, y, 1 - c), device_id_type=MESH))
        for cp in cps:
            cp.start()
        for cp in cps:
            cp.wait()

    return pl.pallas_call(
        body, name=name, in_specs=[ANY] * n, out_specs=[ANY] * n,
        out_shape=[jax.ShapeDtypeStruct((4, g.shape[0] // N_DEV, g.shape[1]), g.dtype) for g in grads],
        scratch_shapes=[pltpu.SemaphoreType.DMA((n, 4)), pltpu.SemaphoreType.DMA((n, 4))],
    )(*grads)


def _rs_chip_copies(sum_refs, land_refs, send_sems, recv_sems):
    x, y, c = _place()
    chips = [(1 - x, y), (x, 1 - y), (1 - x, 1 - y)]
    return [pltpu.make_async_remote_copy(
        src_ref=sum_refs[a].at[2 * px + py], dst_ref=land_refs[a].at[2 * x + y],
        send_sem=send_sems[3 * a + j], recv_sem=recv_sems[3 * a + j], device_id=(px, py, c), device_id_type=MESH)
        for a in range(len(sum_refs)) for j, (px, py) in enumerate(chips)]


def _row_tile(r):
    return max(t for t in range(ROW_ALIGN, min(r, 1024) + 1, ROW_ALIGN) if r % t == 0)


def _rs_add_cores(grad, recv, cidx, name):
    r, cols = recv.shape[1], recv.shape[2]
    tr = _row_tile(r)
    nb = r // tr

    def body(c_ref, g_ref, r_ref, o_ref):
        o_ref[...] = (g_ref[...].astype(F32) + r_ref[...].astype(F32)).astype(o_ref.dtype)

    return pl.pallas_call(
        body, name=name,
        grid_spec=pltpu.PrefetchScalarGridSpec(
            num_scalar_prefetch=1, grid=(4, nb),
            in_specs=[pl.BlockSpec((tr, cols), lambda q, i, c_ref: ((2 * q + c_ref[0]) * nb + i, 0)),
                      pl.BlockSpec((None, tr, cols), lambda q, i, c_ref: (q, i, 0))],
            out_specs=pl.BlockSpec((None, tr, cols), lambda q, i, c_ref: (q, i, 0))),
        out_shape=jax.ShapeDtypeStruct(recv.shape, recv.dtype),
        compiler_params=_cp(("parallel", "parallel")),
    )(cidx, grad, recv)


def _rs_add_chips(own, recv, slots, name):
    r, cols = recv.shape[1], recv.shape[2]
    tr = _row_tile(r)

    def body(s_ref, o_ref, r0_ref, r1_ref, r2_ref, out_ref):
        acc = o_ref[...].astype(F32)
        for ref in (r0_ref, r1_ref, r2_ref):
            acc = acc + ref[...].astype(F32)
        out_ref[...] = acc

    pick = lambda k: pl.BlockSpec((None, tr, cols), functools.partial(lambda i, s_ref, k: (s_ref[k], i, 0), k=k))
    return pl.pallas_call(
        body, name=name,
        grid_spec=pltpu.PrefetchScalarGridSpec(
            num_scalar_prefetch=1, grid=(r // tr,),
            in_specs=[pick(0), pick(1), pick(2), pick(3)],
            out_specs=pl.BlockSpec((tr, cols), lambda i, s_ref: (i, 0))),
        out_shape=jax.ShapeDtypeStruct((r, cols), F32),
        compiler_params=_cp(("parallel",)),
    )(slots, own, recv, recv, recv)


def _reduce_scatter_start(grads, tag):
    cidx = lax.axis_index("c").astype(jnp.int32)[None]
    recv = _rs_swap_cores(grads, f"rs_swap_cores_{tag}")
    sums = [_rs_add_cores(g, rv, cidx, f"rs_add_cores_{tag}_{i}") for i, (g, rv) in enumerate(zip(grads, recv))]
    lands = [lax.empty(s.shape, s.dtype) for s in sums]
    return _split_start(sums, lands, 3 * len(sums), _rs_chip_copies, f"rs_chips_{tag}_start")


def _reduce_scatter_finish(started, after, tag):
    send_sems, recv_sems, sums, lands, _ = started
    sums, lands = _split_wait(send_sems, recv_sems, sums, lands, after, _rs_chip_copies, f"rs_chips_{tag}_wait")
    x, y = lax.axis_index("x"), lax.axis_index("y")
    slots = jnp.stack([2 * x + y, 2 * (1 - x) + y, 2 * x + 1 - y, 2 * (1 - x) + 1 - y]).astype(jnp.int32)
    return [_rs_add_chips(s, l, slots, f"rs_add_chips_{tag}_{i}") for i, (s, l) in enumerate(zip(sums, lands))]


def _allreduce_small(packs, name, after=()):
    n = len(packs)
    after = list(after)
    assert all(p.shape[0] % (8 * N_DEV) == 0 for p in packs)

    def body(*refs):
        p_refs = refs[:n]
        refs = refs[n + len(after):]
        o_refs, part_refs = refs[:n], refs[n:2 * n]
        send1, recv1, send2, recv2 = refs[2 * n:]
        x, y, c = _place()
        me = 4 * x + 2 * y + c

        def block(ref, d):
            rs = ref.shape[0] // N_DEV
            return ref.at[pl.ds(pl.multiple_of(d * rs, 8), rs), :]

        peers = [(1 - x if k & 4 else x, 1 - y if k & 2 else y, 1 - c if k & 1 else c) for k in range(1, N_DEV)]
        scatter = [pltpu.make_async_remote_copy(
            src_ref=block(p_refs[a], 4 * px + 2 * py + pc), dst_ref=part_refs[a].at[me],
            send_sem=send1.at[a, k], recv_sem=recv1.at[a, k], device_id=(px, py, pc), device_id_type=MESH)
            for a in range(n) for k, (px, py, pc) in enumerate(peers)]
        for cp in scatter:
            cp.start()
        for a in range(n):
            part_refs[a][me] = block(p_refs[a], me)[...]
        for cp in scatter:
            cp.wait()
        for a in range(n):
            acc = part_refs[a][0]
            for d in range(1, N_DEV):
                acc = acc + part_refs[a][d]
            block(o_refs[a], me)[...] = acc
        gather = [pltpu.make_async_remote_copy(
            src_ref=block(o_refs[a], me), dst_ref=block(o_refs[a], me), send_sem=send2.at[a, k], recv_sem=recv2.at[a, k],
            device_id=peer, device_id_type=MESH) for a in range(n) for k, peer in enumerate(peers)]
        for cp in gather:
            cp.start()
        for a in range(n):
            for k, (px, py, pc) in enumerate(peers):
                theirs = block(o_refs[a], 4 * px + 2 * py + pc)
                pltpu.make_async_remote_copy(
                    src_ref=theirs, dst_ref=theirs, send_sem=send2.at[a, k], recv_sem=recv2.at[a, k],
                    device_id=(px, py, pc), device_id_type=MESH).wait_recv()
        for cp in gather:
            cp.wait_send()

    sems = pltpu.SemaphoreType.DMA((n, N_DEV - 1))
    vmem = pl.BlockSpec(memory_space=pltpu.VMEM)
    return pl.pallas_call(
        body, name=name,
        in_specs=[vmem] * n + [ANY] * len(after), out_specs=[vmem] * n,
        out_shape=[jax.ShapeDtypeStruct(p.shape, F32) for p in packs],
        scratch_shapes=[pltpu.VMEM((N_DEV, p.shape[0] // N_DEV, p.shape[1]), F32) for p in packs] + [sems] * 4,
        compiler_params=pltpu.CompilerParams(vmem_limit_bytes=VMEM_LIMIT),
    )(*packs, *after)


ADAM_TILE_BYTES = 2 * 1024 * 1024


def _adam_tiles(rows, cols):
    tc = cols // 2 if cols % 256 == 0 and cols >= 2048 else cols
    tr = max(t for t in range(8, rows + 1, 8) if rows % t == 0 and t * max(tc, 128) * 4 <= ADAM_TILE_BYTES) \
        if rows % 8 == 0 else rows
    return tr, tc


def _adam_math(w, g, m, v):
    nm = ADAM_B1 * m + (1.0 - ADAM_B1) * g
    nv = ADAM_B2 * v + (1.0 - ADAM_B2) * jnp.square(g)
    c1 = 1.0 - ADAM_B1 ** ADAM_STEP
    c2 = 1.0 - ADAM_B2 ** ADAM_STEP
    return -ADAM_LR * ((nm / c1) / (jnp.sqrt(nv / c2) + ADAM_EPS) + ADAM_WD * w), nm, nv


def _adamw_layer(w, g, m, v, layer, carry, name):
    _, rows, cols = w.shape
    tr, tc = _adam_tiles(rows, cols)

    def body(w_ref, g_ref, m_ref, v_ref, *rest):
        go_ref, d_ref, nm_ref, nv_ref = rest[-4:]
        gv = g_ref[...]
        go_ref[...] = gv
        d_ref[...], nm_ref[...], nv_ref[...] = _adam_math(w_ref[...], gv, m_ref[...], v_ref[...])

    blk = pl.BlockSpec((None, tr, tc), lambda i, j: (layer, i, j))
    flat = pl.BlockSpec((tr, tc), lambda i, j: (i, j))
    sh = jax.ShapeDtypeStruct(w.shape, F32)
    carry = [] if carry is None else list(carry)
    return pl.pallas_call(
        body, name=name, grid=(rows // tr, cols // tc),
        in_specs=[blk, flat, blk, blk] + [ANY] * len(carry), out_specs=[blk] * 4, out_shape=[sh] * 4,
        input_output_aliases={4 + k: k for k in range(len(carry))},
        compiler_params=_cp(("parallel", "parallel")),
    )(w, g, m, v, *carry)


def _adamw(w, g, m, v, name):
    shape = w.shape
    rows, cols = shape[-2:]
    lead = shape[:-2]
    nl = math.prod(lead)
    tr, tc = _adam_tiles(rows, cols)

    def body(w_ref, g_ref, m_ref, v_ref, d_ref, nm_ref, nv_ref):
        d_ref[...], nm_ref[...], nv_ref[...] = _adam_math(w_ref[...], g_ref[...], m_ref[...], v_ref[...])

    def index(b, i, j):
        return (*jnp.unravel_index(b, lead), i, j) if lead else (i, j)

    blk = pl.BlockSpec((*[None] * len(lead), tr, tc), index)
    sh = jax.ShapeDtypeStruct(shape, F32)
    return pl.pallas_call(
        body, name=name, grid=(nl, rows // tr, cols // tc), in_specs=[blk] * 4, out_specs=[blk] * 3,
        out_shape=[sh] * 3, compiler_params=_cp(("parallel", "parallel", "parallel")),
    )(w, g, m, v)


WEIGHTS = ("norm_w", "w_in", "ssm_a_re", "ssm_a_im", "ssm_log_dt", "ssm_b_re", "ssm_b_im", "ssm_c_re", "ssm_c_im",
           "ssm_d", "ssm_glu_w", "ssm_glu_b", "sg_ln_w", "sg_ln_b", "sg_w", "sg_b", "attn_sinks",
           "w_branch_a", "w_branch_b", "w_branch_c", "w_out", "final_norm_w")
BIG = ("w_in", "ssm_glu_w", "w_branch_a", "w_branch_b", "w_branch_c", "w_out")
BIG_KEY = {"w_in": ("win_t", True), "ssm_glu_w": ("glu", False), "w_branch_a": ("wba_t", True),
           "w_branch_b": ("wbb_t", True), "w_branch_c": ("wbc_t", True), "w_out": ("wout", False)}
VIEWS = {"w_in": (1, 2), "ssm_b_re": (2, 3), "ssm_b_im": (2, 3)}
PACKS = (
    (64, (("ssm_a_re",), ("ssm_a_im",), ("ssm_c_re",), ("ssm_c_im",), ("ssm_b_re",), ("ssm_b_im",))),
    (128, (("sg_w",),)),
    (1024, (("ssm_d", "ssm_glu_b", "sg_ln_w", "sg_ln_b"), ("norm_w", "final_norm_w", "sg_b"), ("ssm_log_dt", "attn_sinks"))),
)
PACK_ROWS = 8 * N_DEV


def _view(n, a):
    return jnp.swapaxes(a, *VIEWS[n]) if n in VIEWS else a


def _group_rows(arrs, cols):
    return -(-sum(-(-a.size // cols) for a in arrs) // 8) * 8


def _pack(groups, cols):
    parts = []
    for arrs in groups:
        if len(arrs) == 1 and arrs[0].shape[-1] == cols and arrs[0].size % (8 * cols) == 0:
            parts.append(arrs[0].reshape(-1, cols))
            continue
        flat = [jnp.pad(a.reshape(-1), (0, -a.size % cols)) for a in arrs]
        flat = jnp.concatenate(flat) if len(flat) > 1 else flat[0]
        nrow = _group_rows(arrs, cols)
        parts.append(jnp.pad(flat, (0, nrow * cols - flat.shape[0])).reshape(nrow, cols))
    pad = -sum(p.shape[0] for p in parts) % PACK_ROWS
    if pad:
        parts.append(jnp.zeros((pad, cols), F32))
    return jnp.concatenate(parts, axis=0)


def _unpack(pack, groups):
    cols = pack.shape[1]
    out, row = [], 0
    for arrs in groups:
        nrow = _group_rows(arrs, cols)
        rows = pack[row:row + nrow]
        row += nrow
        if len(arrs) == 1 and arrs[0].shape[-1] == cols and arrs[0].size == nrow * cols:
            out.append(rows.reshape(arrs[0].shape))
            continue
        flat, off = rows.reshape(-1), 0
        for a in arrs:
            out.append(flat[off:off + a.size].reshape(a.shape))
            off += -(-a.size // cols) * cols
    return out


def kernel(x, norm_w, w_in, ssm_a_re, ssm_a_im, ssm_log_dt, ssm_b_re, ssm_b_im, ssm_c_re, ssm_c_im, ssm_d, ssm_glu_w, ssm_glu_b, sg_ln_w, sg_ln_b, sg_w, sg_b, attn_sinks, w_branch_a, w_branch_b, w_branch_c, w_out, final_norm_w, loss_target, m_norm_w, m_w_in, m_ssm_a_re, m_ssm_a_im, m_ssm_log_dt, m_ssm_b_re, m_ssm_b_im, m_ssm_c_re, m_ssm_c_im, m_ssm_d, m_ssm_glu_w, m_ssm_glu_b, m_sg_ln_w, m_sg_ln_b, m_sg_w, m_sg_b, m_attn_sinks, m_w_branch_a, m_w_branch_b, m_w_branch_c, m_w_out, m_final_norm_w, v_norm_w, v_w_in, v_ssm_a_re, v_ssm_a_im, v_ssm_log_dt, v_ssm_b_re, v_ssm_b_im, v_ssm_c_re, v_ssm_c_im, v_ssm_d, v_ssm_glu_w, v_ssm_glu_b, v_sg_ln_w, v_sg_ln_b, v_sg_w, v_sg_b, v_attn_sinks, v_w_branch_a, v_w_branch_b, v_w_branch_c, v_w_out, v_final_norm_w):
    w = dict(zip(WEIGHTS, (norm_w, w_in, ssm_a_re, ssm_a_im, ssm_log_dt, ssm_b_re, ssm_b_im, ssm_c_re, ssm_c_im, ssm_d, ssm_glu_w, ssm_glu_b, sg_ln_w, sg_ln_b, sg_w, sg_b, attn_sinks, w_branch_a, w_branch_b, w_branch_c, w_out, final_norm_w)))
    m = dict(zip(WEIGHTS, (m_norm_w, m_w_in, m_ssm_a_re, m_ssm_a_im, m_ssm_log_dt, m_ssm_b_re, m_ssm_b_im, m_ssm_c_re, m_ssm_c_im, m_ssm_d, m_ssm_glu_w, m_ssm_glu_b, m_sg_ln_w, m_sg_ln_b, m_sg_w, m_sg_b, m_attn_sinks, m_w_branch_a, m_w_branch_b, m_w_branch_c, m_w_out, m_final_norm_w)))
    v = dict(zip(WEIGHTS, (v_norm_w, v_w_in, v_ssm_a_re, v_ssm_a_im, v_ssm_log_dt, v_ssm_b_re, v_ssm_b_im, v_ssm_c_re, v_ssm_c_im, v_ssm_d, v_ssm_glu_w, v_ssm_glu_b, v_sg_ln_w, v_sg_ln_b, v_sg_w, v_sg_b, v_attn_sinks, v_w_branch_a, v_w_branch_b, v_w_branch_c, v_w_out, v_final_norm_w)))

    keys = [BIG_KEY[n][0] for n in BIG]
    wv, mv, vv = ({n: _view(n, a) for n, a in d.items()} for d in (w, m, v))
    shards = [[(wv[n][l] if n in VIEWS else w[n][l].T if BIG_KEY[n][1] else w[n][l]).astype(BF16) for n in BIG]
              for l in range(DEPTH)]
    small_p = [{n: w[n][l] for n in SMALL} for l in range(DEPTH)]
    xv, tgt = x[0], loss_target[0]
    tabs = _rope_tables(xv.shape[0])

    lands = [[_allgather_place(shards[l][:1]), _allgather_place(shards[l][1:])] for l in range(DEPTH)]
    s5 = [_s5_prep(small_p[l], f"l{l}") for l in range(DEPTH)]
    wmv_packs = {cols: [_pack([[d[n] for n in names] for names in groups], cols) for d in (wv, mv, vv)]
                 for cols, groups in PACKS}
    ag0a = _allgather_start(lands[0][0], "ag_l0_win")
    got = {}

    def win_of0(h):
        early = [h, *lands[0][1], *lands[1][0], *lands[1][1], *s5[0][1], *s5[1][1]]
        early += [p for ps in wmv_packs.values() for p in ps]
        got["win0"] = _allgather_finish(ag0a, early, "ag_l0_win")[0]
        got["ag0b"] = _allgather_start(lands[0][1], "ag_l0_rest", after=got["win0"])
        got["ag1a"] = _allgather_start(lands[1][0], "ag_l1_win", after=got["ag0b"][4])
        got["ag1b"] = _allgather_start(lands[1][1], "ag_l1_rest", after=got["ag1a"][4])
        return got["win0"], got["ag1b"][4]

    def after_proj0(proj):
        got["w0"] = dict(zip(keys, [got["win0"]] + _allgather_finish(got["ag0b"], proj, "ag_l0_rest")))
        return got["w0"]

    x1, saved0 = _layer_fwd(xv, small_p[0], None, tabs, "l0", s5=s5[0], win_of=win_of0, after_proj=after_proj0)
    big_w0 = got["w0"]
    win1 = _allgather_finish(got["ag1a"], x1, "ag_l1_win")[0]

    def after_proj1(proj):
        got["w1"] = dict(zip(keys, [win1] + _allgather_finish(got["ag1b"], proj, "ag_l1_rest")))
        return got["w1"]

    x2, saved1 = _layer_fwd(x1, small_p[1], {"win_t": win1}, tabs, "l1", s5=s5[1], after_proj=after_proj1)
    big_w1 = got["w1"]
    loss_acc, dx2, dfw = _final(x2, w["final_norm_w"][None], tgt, "final_norm_loss")
    loss = lax.psum(loss_acc[0, 0], ("x", "y", "c"))
    dfw = dfw[0]

    dx1, big_g1, small_g1 = _layer_bwd(dx2, small_p[1], big_w1, tabs, saved1, "l1")
    rs1 = _reduce_scatter_start([big_g1[k] for k in keys], "l1")

    def before_win0(big):
        got["rs0b"] = _reduce_scatter_start([big[k] for k in keys[1:]], "l0_rest")
        return got["rs0b"][4]

    def after_win0(big):
        got["rs0a"] = _reduce_scatter_start([big["win_t"]], "l0_win")
        return got["rs0a"][4]

    dx, big_g0, small_g0 = _layer_bwd(dx1, small_p[0], big_w0, tabs, saved0, "l0", first_after=rs1[4],
                                      before_win=before_win0, after_win=after_win0)
    small_g = [small_g0, small_g1]
    grads, delta, new_m, new_v = {}, {}, {}, {}

    def big_adam(red, layer, carry):
        outs = {}
        for i, n in enumerate(BIG):
            g = red[i].T if BIG_KEY[n][1] and n not in VIEWS else red[i]
            outs[n] = _adamw_layer(wv[n], g, mv[n], vv[n], layer, None if carry is None else carry[n], f"adamw_{n}_l{layer}")
        return outs

    big1 = big_adam(_reduce_scatter_finish(rs1, dx, "l1"), 1, None)

    def small_grad(n):
        if n == "final_norm_w":
            return dfw
        if n in ("ssm_b_re", "ssm_b_im"):
            return jnp.stack([small_g[l][n.replace("ssm_b_", "ssm_bt_")].transpose(1, 0, 2) for l in range(DEPTH)])
        return jnp.stack([small_g[l][n] for l in range(DEPTH)])

    g_groups = [[[small_grad(n) for n in names] for names in groups] for _, groups in PACKS]
    reduced = _allreduce_small([_pack(gg, cols) for gg, (cols, _) in zip(g_groups, PACKS)], "allreduce_small",
                               after=[big1[n][1] for n in BIG])
    last = None
    for (cols, groups), gg, red in zip(PACKS, g_groups, reduced):
        names = [n for names in groups for n in names]
        grads.update(zip(names, _unpack(red, gg)))
        wp, mp, vp = wmv_packs[cols]
        outs = _adamw(wp, red, mp, vp, f"adamw_pack{cols}")
        last = outs[0]
        for res, o in zip((delta, new_m, new_v), outs):
            res.update(zip(names, _unpack(o, [[wv[n] for n in names] for names in groups])))

    red0 = (_reduce_scatter_finish(got["rs0a"], last, "l0_win")
            + _reduce_scatter_finish(got["rs0b"], last, "l0_rest"))
    for n, outs in big_adam(red0, 0, big1).items():
        grads[n], delta[n], new_m[n], new_v[n] = outs

    return (loss, dx[None], *[_view(n, d[n]) for d in (grads, delta, new_m, new_v) for n in WEIGHTS])
```

```python
import functools
import math

import jax
import jax.numpy as jnp
from jax import lax
from jax.experimental import pallas as pl
from jax.experimental.pallas import tpu as pltpu

F32 = jnp.float32
BF16 = jnp.bfloat16

D_MODEL = 2048
DEPTH = 2
EPS = 1e-6
NEG_INF = -1e30
N_DEV = 8

SSM_WIDTH = 1024
SSM_GROUP = 16
SSM_GROUPS = 64
SSM_STATE = 64
N_SLAB = 8
SLAB_CH = 128
SLAB_ST = 512
SUB = 8
N_GRP = 2
N_SEG = SUB * N_GRP

SG_HEADS = 8
CHUNK = 128
HEAD_DIM = 64
ATT_HEADS = 16
ROT_DIM = 16
ROPE_THETA = 500000.0

D_IN = 13568
OFF_UA, OFF_ZA, OFF_UB, OFF_VB, OFF_ZB, OFF_Q, OFF_KV, OFF_ZC, OFF_G = (
    0, 1024, 2048, 3072, 4096, 5120, 6144, 6400, 7424)

ADAM_LR, ADAM_B1, ADAM_B2, ADAM_EPS, ADAM_WD, ADAM_STEP = 0.001, 0.9, 0.999, 1e-08, 0.01, 10

VMEM_LIMIT = 56 * 1024 * 1024


def _cp(sem=None):
    return pltpu.CompilerParams(dimension_semantics=sem, vmem_limit_bytes=VMEM_LIMIT)


def _dot(a, b):
    return jnp.dot(a, b, preferred_element_type=F32)


def _dot_nt(a, b):
    return lax.dot_general(a, b, (((1,), (1,)), ((), ())), preferred_element_type=F32)


def _dot_tn(a, b):
    return lax.dot_general(a, b, (((0,), (0,)), ((), ())), preferred_element_type=F32)


def _mm(a, b, mode, out_dtype, tm, tn, tk, name, res=None, after=None):
    if mode == "nn":
        (m, k), (_, n) = a.shape, b.shape
    elif mode == "nt":
        (m, k), (n, _) = a.shape, b.shape
    else:
        (k, m), (_, n) = a.shape, b.shape
    tm, tn, tk = min(tm, m), min(tn, n), min(tk, k)
    assert m % tm == 0 and n % tn == 0 and k % tk == 0, (name, m, n, k, tm, tn, tk)
    nk = k // tk
    a_spec = {"nn": pl.BlockSpec((tm, tk), lambda i, j, kk: (i, kk)),
              "nt": pl.BlockSpec((tm, tk), lambda i, j, kk: (i, kk)),
              "tn": pl.BlockSpec((tk, tm), lambda i, j, kk: (kk, i))}[mode]
    b_spec = {"nn": pl.BlockSpec((tk, tn), lambda i, j, kk: (kk, j)),
              "nt": pl.BlockSpec((tn, tk), lambda i, j, kk: (j, kk)),
              "tn": pl.BlockSpec((tk, tn), lambda i, j, kk: (kk, j))}[mode]
    dot = {"nn": _dot, "nt": _dot_nt, "tn": _dot_tn}[mode]
    has_res = res is not None
    direct = out_dtype == F32 and not has_res

    def body(*refs):
        ins, outs = refs[:2 + has_res + (after is not None)], refs[2 + has_res + (after is not None):]
        a_ref, b_ref = ins[:2]
        r_ref = ins[2] if has_res else None
        o_ref = outs[0]
        acc = o_ref if direct else outs[1]
        kk = pl.program_id(2)

        @pl.when(kk == 0)
        def _():
            acc[...] = jnp.zeros_like(acc)

        acc[...] += dot(a_ref[...].astype(BF16), b_ref[...].astype(BF16))

        if not direct:
            @pl.when(kk == nk - 1)
            def _():
                r = acc[...]
                if has_res:
                    r = r + r_ref[...]
                o_ref[...] = r.astype(out_dtype)

    in_specs = [a_spec, b_spec]
    args = [a, b]
    if has_res:
        in_specs.append(pl.BlockSpec((tm, tn), lambda i, j, kk: (i, j)))
        args.append(res)
    if after is not None:
        in_specs.append(pl.BlockSpec(memory_space=pl.ANY))
        args.append(after)
    return pl.pallas_call(
        body, name=name,
        grid=(m // tm, n // tn, nk),
        in_specs=in_specs,
        out_specs=pl.BlockSpec((tm, tn), lambda i, j, kk: (i, j)),
        out_shape=jax.ShapeDtypeStruct((m, n), out_dtype),
        scratch_shapes=[] if direct else [pltpu.VMEM((tm, tn), F32)],
        compiler_params=_cp(("parallel", "parallel", "arbitrary")),
    )(*args)


PROJ_TN = 256


def _in_proj_tiles(h, win_t, tiles, carry, name, after=None):
    L, K = h.shape
    extra = [a for a in (carry, after) if a is not None]

    def body(t_ref, h_ref, w_ref, *rest):
        rest[len(extra)][...] = _dot_nt(h_ref[...], w_ref[...]).astype(BF16)

    return pl.pallas_call(
        body, name=name,
        grid_spec=pltpu.PrefetchScalarGridSpec(
            num_scalar_prefetch=1, grid=(tiles.shape[0],),
            in_specs=[pl.BlockSpec((L, K), lambda j, t: (0, 0)), pl.BlockSpec((PROJ_TN, K), lambda j, t: (t[j], 0))]
            + [pl.BlockSpec(memory_space=pl.ANY)] * len(extra),
            out_specs=pl.BlockSpec((L, PROJ_TN), lambda j, t: (0, t[j]))),
        out_shape=jax.ShapeDtypeStruct((L, win_t.shape[0]), BF16),
        input_output_aliases={} if carry is None else {3: 0},
        compiler_params=_cp(("arbitrary",)),
    )(tiles, h, win_t, *extra)


def _rms(x, w):
    return x * lax.rsqrt(jnp.mean(x * x, axis=-1, keepdims=True) + EPS) * w


def _rms_fwd(x, w, name):
    L, D = x.shape
    tm = min(L, 256)

    def body(x_ref, w_ref, h_ref):
        h_ref[...] = _rms(x_ref[...], w_ref[...]).astype(BF16)

    return pl.pallas_call(
        body, name=name, grid=(L // tm,),
        in_specs=[pl.BlockSpec((tm, D), lambda i: (i, 0)), pl.BlockSpec((1, D), lambda i: (0, 0))],
        out_specs=pl.BlockSpec((tm, D), lambda i: (i, 0)),
        out_shape=jax.ShapeDtypeStruct((L, D), BF16),
        compiler_params=_cp(("parallel",)),
    )(x, w)


def _rms_bwd(x, w, dh, dres, name):
    L, D = x.shape
    tm = min(L, 256)

    def body(x_ref, w_ref, dh_ref, dres_ref, dx_ref, dw_ref):
        _, vjp = jax.vjp(_rms, x_ref[...], w_ref[...])
        dx, dw = vjp(dh_ref[...])
        dx_ref[...] = dx + dres_ref[...]

        @pl.when(pl.program_id(0) == 0)
        def _():
            dw_ref[...] = jnp.zeros_like(dw_ref)

        dw_ref[...] += dw

    row = pl.BlockSpec((tm, D), lambda i: (i, 0))
    vec = pl.BlockSpec((1, D), lambda i: (0, 0))
    return pl.pallas_call(
        body, name=name, grid=(L // tm,),
        in_specs=[row, vec, row, row],
        out_specs=[row, vec],
        out_shape=[jax.ShapeDtypeStruct((L, D), F32), jax.ShapeDtypeStruct((1, D), F32)],
        compiler_params=_cp(("arbitrary",)),
    )(x, w, dh, dres)


def _final(x, fw, tgt, name):
    L, D = x.shape
    tm = min(L, 256)

    def loss_fn(xv, wv, tv):
        err = _rms(xv, wv) - tv
        return jnp.sum(err * err) * (0.5 / D)

    def body(x_ref, w_ref, t_ref, loss_ref, dx_ref, dw_ref):
        tv = t_ref[...]
        val, vjp = jax.vjp(lambda a, b: loss_fn(a, b, tv), x_ref[...], w_ref[...])
        dx, dw = vjp(jnp.ones((), F32))
        dx_ref[...] = dx

        @pl.when(pl.program_id(0) == 0)
        def _():
            dw_ref[...] = jnp.zeros_like(dw_ref)
            loss_ref[...] = jnp.zeros_like(loss_ref)

        dw_ref[...] += dw
        loss_ref[...] += jnp.full(loss_ref.shape, val, F32)

    row = pl.BlockSpec((tm, D), lambda i: (i, 0))
    vec = pl.BlockSpec((1, D), lambda i: (0, 0))
    return pl.pallas_call(
        body, name=name, grid=(L // tm,),
        in_specs=[row, vec, row],
        out_specs=[pl.BlockSpec((8, 128), lambda i: (0, 0)), row, vec],
        out_shape=[jax.ShapeDtypeStruct((8, 128), F32), jax.ShapeDtypeStruct((L, D), F32),
                   jax.ShapeDtypeStruct((1, D), F32)],
        compiler_params=_cp(("arbitrary",)),
    )(x, fw, tgt)


def _s5_param_fn(a_re, a_im, log_dt, bt_re, bt_im):
    dt = jnp.exp(log_dt)
    zr, zi = a_re * dt, a_im * dt
    er = jnp.exp(zr)
    lr, li = er * jnp.cos(zi), er * jnp.sin(zi)
    nr, ni = lr - 1.0, li
    den = a_re * a_re + a_im * a_im
    cr = (nr * a_re + ni * a_im) / den
    ci = (ni * a_re - nr * a_im) / den
    bbr = cr[None] * bt_re - ci[None] * bt_im
    bbi = cr[None] * bt_im + ci[None] * bt_re
    return lr, li, bbr, bbi


def _s5_params_fwd(a_re, a_im, log_dt, bt_re, bt_im, name):
    def body(ar, ai, ld, br, bi, lr, li, bbr, bbi):
        o = _s5_param_fn(ar[...], ai[...], ld[...], br[...], bi[...])
        lr[...], li[...], bbr[...], bbi[...] = o

    gp = jax.ShapeDtypeStruct(a_re.shape, F32)
    cgp = jax.ShapeDtypeStruct(bt_re.shape, F32)
    return pl.pallas_call(body, name=name, out_shape=[gp, gp, cgp, cgp])(a_re, a_im, log_dt, bt_re, bt_im)


def _s5_params_bwd(a_re, a_im, log_dt, bt_re, bt_im, dlr, dli, dbbr, dbbi, name):
    def body(ar, ai, ld, br, bi, g0, g1, g2, g3, o0, o1, o2, o3, o4):
        _, vjp = jax.vjp(_s5_param_fn, ar[...], ai[...], ld[...], br[...], bi[...])
        o0[...], o1[...], o2[...], o3[...], o4[...] = vjp((g0[...], g1[...], g2[...], g3[...]))

    gp = jax.ShapeDtypeStruct(a_re.shape, F32)
    cgp = jax.ShapeDtypeStruct(bt_re.shape, F32)
    return pl.pallas_call(body, name=name,
                          out_shape=[gp, gp, jax.ShapeDtypeStruct(log_dt.shape, F32), cgp, cgp])(
        a_re, a_im, log_dt, bt_re, bt_im, dlr, dli, dbbr, dbbi)


def _cmul(ar, ai, br, bi):
    return ar * br - ai * bi, ar * bi + ai * br


def _cpow(lr, li, n):
    rr, ri = None, None
    br, bi = lr, li
    while n:
        if n & 1:
            rr, ri = (br, bi) if rr is None else _cmul(rr, ri, br, bi)
        n >>= 1
        if n:
            br, bi = _cmul(br, bi, br, bi)
    return rr, ri


def _shift_rows(x, up):
    row = lax.broadcasted_iota(jnp.int32, x.shape, 0)
    if up:
        return jnp.where(row == SUB - 1, 0.0, pltpu.roll(x, SUB - 1, 0))
    return jnp.where(row == 0, 0.0, pltpu.roll(x, 1, 0))


NT = SLAB_ST // 128


def _lam_tiles(lr_ref, li_ref):
    return [(lr_ref[:, j * 128:(j + 1) * 128], li_ref[:, j * 128:(j + 1) * 128]) for j in range(NT)]


def _row_on_sublanes(ref, j, t):
    return ref[j, pl.ds(t, SUB, stride=0), :]


def _pow_table(pw_re, pw_im, lam_t, seg):
    assert seg % 8 == 0 and (seg // 8) & (seg // 8 - 1) == 0
    for j in range(NT):
        lr, li = lam_t[j][0][0:1], lam_t[j][1][0:1]
        r, i_ = lr, li
        for row in range(8):
            pw_re[j, row:row + 1, :] = r
            pw_im[j, row:row + 1, :] = i_
            if row < 7:
                r, i_ = _cmul(r, i_, lr, li)
        n = 8
        while n < seg:
            qr, qi = _cpow(lr, li, n)
            nr, ni = _cmul(pw_re[j, 0:n, :], pw_im[j, 0:n, :], qr, qi)
            pw_re[j, n:2 * n, :] = nr
            pw_im[j, n:2 * n, :] = ni
            n *= 2


def _seg_scan(s_re, s_im, lam_t, pw_re, pw_im, seg, reverse, prev=None):
    sgn = -1.0 if reverse else 1.0
    lt = [(lr, sgn * li) for lr, li in lam_t]
    tiles = [(g, j) for g in range(N_GRP) for j in range(NT)]
    zeros = jnp.zeros((SUB, 128), F32)

    def rows(g, i):
        return pl.ds(pl.multiple_of((g * seg + i) * SUB, SUB), SUB)

    def step1(t, carry):
        i = seg - 1 - t if reverse else t
        out = []
        for n, (g, j) in enumerate(tiles):
            nr, ni = _cmul(lt[j][0], lt[j][1], carry[2 * n], carry[2 * n + 1])
            nr = nr + s_re[j, rows(g, i), :]
            ni = ni + s_im[j, rows(g, i), :]
            s_re[j, rows(g, i), :] = nr
            s_im[j, rows(g, i), :] = ni
            out += [nr, ni]
        return tuple(out)

    zero = tuple(zeros for _ in range(2 * len(tiles)))
    ends = lax.fori_loop(0, seg, step1, zero)

    carries = [None] * (2 * len(tiles))
    row = lax.broadcasted_iota(jnp.int32, (SUB, 128), 0)
    dist = (SUB - 1 - row) if reverse else row
    edge = 0 if reverse else SUB - 1
    for j in range(NT):
        pr, pi = _cpow(lt[j][0], lt[j][1], seg)
        qr, qi = jnp.ones((SUB, 128), F32), zeros
        for s in range(1, SUB):
            tr, ti = _cmul(qr, qi, pr, pi)
            qr, qi = jnp.where(dist >= s, tr, qr), jnp.where(dist >= s, ti, qi)
        boundary = None
        for g in (reversed(range(N_GRP)) if reverse else range(N_GRP)):
            n = g * NT + j
            cr, ci = zeros, zeros
            for _ in range(SUB - 1):
                tr, ti = _cmul(pr, pi, cr, ci)
                cr = _shift_rows(tr + ends[2 * n], reverse)
                ci = _shift_rows(ti + ends[2 * n + 1], reverse)
            if boundary is not None:
                tr, ti = _cmul(qr, qi, boundary[0], boundary[1])
                cr, ci = cr + tr, ci + ti
            carries[2 * n], carries[2 * n + 1] = cr, ci
            fr, fi = _cmul(pr, pi, cr, ci)
            boundary = (jnp.broadcast_to((fr + ends[2 * n])[edge:edge + 1], (SUB, 128)),
                        jnp.broadcast_to((fi + ends[2 * n + 1])[edge:edge + 1], (SUB, 128)))

    def fix(t, i, acc, before):
        out = []
        pws = [(_row_on_sublanes(pw_re, j, t), sgn * _row_on_sublanes(pw_im, j, t)) for j in range(NT)]
        for n, (g, j) in enumerate(tiles):
            ar, ai = _cmul(pws[j][0], pws[j][1], carries[2 * n], carries[2 * n + 1])
            ar = ar + s_re[j, rows(g, i), :]
            ai = ai + s_im[j, rows(g, i), :]
            s_re[j, rows(g, i), :] = ar
            s_im[j, rows(g, i), :] = ai
            if before is not None:
                qr, qi = before(n)
                out += [acc[2 * n] + ar * qr + ai * qi, acc[2 * n + 1] + ai * qr - ar * qi]
        return tuple(out)

    if prev is None:
        lax.fori_loop(0, seg, lambda t, c: fix(t, seg - 1 - t if reverse else t, c, None), ())
        return carries
    assert reverse
    p_re, p_im, p_carries = prev

    def earlier(t):
        return lambda n: (p_re[tiles[n][1], rows(tiles[n][0], seg - 2 - t), :],
                          p_im[tiles[n][1], rows(tiles[n][0], seg - 2 - t), :])

    acc = lax.fori_loop(0, seg - 1, lambda t, c: fix(t, seg - 1 - t, c, earlier(t)), zero)
    acc = fix(seg - 1, 0, acc, lambda n: (p_carries[2 * n], p_carries[2 * n + 1]))
    return carries, [sum(acc[2 * (g * NT + j) + part] for g in range(N_GRP)) for j in range(NT) for part in range(2)]


def _seg_slice(k, seg):
    g, r = divmod(k, SUB)
    return pl.ds(g * seg * SUB + r, seg, stride=SUB)


def _seg_rows(ref, k, seg):
    return jnp.concatenate([ref[j, _seg_slice(k, seg), :] for j in range(NT)], axis=-1)


def _seg_store(ref, k, seg, val):
    for j in range(NT):
        ref[j, _seg_slice(k, seg), :] = val[:, j * 128:(j + 1) * 128]


def _s5_specs(L):
    col = lambda off: pl.BlockSpec((L, SLAB_CH), lambda j: (0, off + j))
    mat_b = pl.BlockSpec((None, SLAB_CH, SLAB_ST), lambda j: (j, 0, 0))
    mat_c = pl.BlockSpec((None, SLAB_ST, SLAB_CH), lambda j: (j, 0, 0))
    vec_s = pl.BlockSpec((None, SUB, SLAB_ST), lambda j: (j, 0, 0))
    vec_c = pl.BlockSpec((None, 1, SLAB_CH), lambda j: (j, 0, 0))
    return col, mat_b, mat_c, vec_s, vec_c


def _s5_states(u_ref, bre_ref, bim_ref, lam_t, pw_re, pw_im, s_re, s_im, seg):
    _pow_table(pw_re, pw_im, lam_t, seg)
    for k in range(N_SEG):
        uk = u_ref[pl.ds(k * seg, seg), :]
        _seg_store(s_re, k, seg, _dot(uk, bre_ref[...]))
        _seg_store(s_im, k, seg, _dot(uk, bim_ref[...]))
    return _seg_scan(s_re, s_im, lam_t, pw_re, pw_im, seg, reverse=False)


def _s5_fwd(proj, bre, bim, cre_t, cim_t, lam_re, lam_im, dvec, name):
    L = proj.shape[0]
    seg = L // N_SEG
    col, mat_b, mat_c, vec_s, vec_c = _s5_specs(L)
    rows = N_SEG * seg

    def body(u_ref, bre_ref, bim_ref, cre_ref, cim_ref, lr_ref, li_ref, d_ref, y_ref, s_re, s_im, pw_re, pw_im):
        _s5_states(u_ref, bre_ref, bim_ref, _lam_tiles(lr_ref, li_ref), pw_re, pw_im, s_re, s_im, seg)
        for k in range(N_SEG):
            y = (_dot(_seg_rows(s_re, k, seg).astype(BF16), cre_ref[...])
                 - _dot(_seg_rows(s_im, k, seg).astype(BF16), cim_ref[...]))
            y = y + d_ref[...] * u_ref[pl.ds(k * seg, seg), :].astype(F32)
            y_ref[pl.ds(k * seg, seg), :] = jax.nn.gelu(y).astype(BF16)

    return pl.pallas_call(
        body, name=name, grid=(N_SLAB,),
        in_specs=[col(OFF_UA // SLAB_CH), mat_b, mat_b, mat_c, mat_c, vec_s, vec_s, vec_c],
        out_specs=pl.BlockSpec((L, SLAB_CH), lambda j: (0, j)),
        out_shape=jax.ShapeDtypeStruct((L, SSM_WIDTH), BF16),
        scratch_shapes=[pltpu.VMEM((NT, rows, 128), F32)] * 2 + [pltpu.VMEM((NT, seg, 128), F32)] * 2,
        compiler_params=_cp(("parallel",)),
    )(proj, bre, bim, cre_t, cim_t, lam_re, lam_im, dvec)


def _s5_bwd(proj, dy, bre, bim, cre_t, cim_t, lam_re, lam_im, dvec, name):
    L = proj.shape[0]
    seg = L // N_SEG
    col, mat_b, mat_c, vec_s, vec_c = _s5_specs(L)
    rows = N_SEG * seg
    dlam_spec = pl.BlockSpec((None, 1, SLAB_ST), lambda j: (j, 0, 0))

    def body(u_ref, dy_ref, bre_ref, bim_ref, cre_ref, cim_ref, lr_ref, li_ref, d_ref,
             du_ref, dbre_ref, dbim_ref, dcre_ref, dcim_ref, dlr_ref, dli_ref, dd_ref,
             s_re, s_im, a_re, a_im, pw_re, pw_im, dyp):
        lam_t = _lam_tiles(lr_ref, li_ref)
        carry_s = _s5_states(u_ref, bre_ref, bim_ref, lam_t, pw_re, pw_im, s_re, s_im, seg)
        dcre = jnp.zeros((SLAB_ST, SLAB_CH), F32)
        dcim = jnp.zeros((SLAB_ST, SLAB_CH), F32)
        dd = jnp.zeros((1, SLAB_CH), F32)
        for k in range(N_SEG):
            sre = _seg_rows(s_re, k, seg).astype(BF16)
            sim = _seg_rows(s_im, k, seg).astype(BF16)
            uk = u_ref[pl.ds(k * seg, seg), :].astype(F32)
            ypre = _dot(sre, cre_ref[...]) - _dot(sim, cim_ref[...]) + d_ref[...] * uk
            _, vjp = jax.vjp(jax.nn.gelu, ypre)
            (dyk,) = vjp(dy_ref[pl.ds(k * seg, seg), :].astype(F32))
            dyp[pl.ds(k * seg, seg), :] = dyk
            dd = dd + jnp.sum(dyk * uk, axis=0, keepdims=True)
            dyb = dyk.astype(BF16)
            dcre = dcre + _dot_tn(sre, dyb)
            dcim = dcim - _dot_tn(sim, dyb)
            _seg_store(a_re, k, seg, _dot_nt(dyb, cre_ref[...]))
            _seg_store(a_im, k, seg, -_dot_nt(dyb, cim_ref[...]))
        dcre_ref[...] = dcre
        dcim_ref[...] = dcim
        dd_ref[...] = dd

        _, acc = _seg_scan(a_re, a_im, lam_t, pw_re, pw_im, seg, reverse=True, prev=(s_re, s_im, carry_s))
        dlr_ref[...] = jnp.concatenate([jnp.sum(acc[2 * j], axis=0, keepdims=True) for j in range(NT)], axis=-1)
        dli_ref[...] = jnp.concatenate([jnp.sum(acc[2 * j + 1], axis=0, keepdims=True) for j in range(NT)], axis=-1)

        dbre = jnp.zeros((SLAB_CH, SLAB_ST), F32)
        dbim = jnp.zeros((SLAB_CH, SLAB_ST), F32)
        for k in range(N_SEG):
            are = _seg_rows(a_re, k, seg).astype(BF16)
            aim = _seg_rows(a_im, k, seg).astype(BF16)
            uk = u_ref[pl.ds(k * seg, seg), :]
            du = _dot_nt(are, bre_ref[...]) + _dot_nt(aim, bim_ref[...]) + dyp[pl.ds(k * seg, seg), :] * d_ref[...]
            du_ref[pl.ds(k * seg, seg), :] = du.astype(BF16)
            dbre = dbre + _dot_tn(uk, are)
            dbim = dbim + _dot_tn(uk, aim)
        dbre_ref[...] = dbre
        dbim_ref[...] = dbim

    scan_buf = pltpu.VMEM((NT, rows, 128), F32)
    pow_buf = pltpu.VMEM((NT, seg, 128), F32)
    return pl.pallas_call(
        body, name=name, grid=(N_SLAB,),
        in_specs=[col(OFF_UA // SLAB_CH), pl.BlockSpec((L, SLAB_CH), lambda j: (0, j)),
                  mat_b, mat_b, mat_c, mat_c, vec_s, vec_s, vec_c],
        out_specs=[pl.BlockSpec((L, SLAB_CH), lambda j: (0, j)), mat_b, mat_b, mat_c, mat_c, dlam_spec, dlam_spec, vec_c],
        out_shape=[jax.ShapeDtypeStruct((L, SSM_WIDTH), BF16),
                   jax.ShapeDtypeStruct((N_SLAB, SLAB_CH, SLAB_ST), F32),
                   jax.ShapeDtypeStruct((N_SLAB, SLAB_CH, SLAB_ST), F32),
                   jax.ShapeDtypeStruct((N_SLAB, SLAB_ST, SLAB_CH), F32),
                   jax.ShapeDtypeStruct((N_SLAB, SLAB_ST, SLAB_CH), F32),
                   jax.ShapeDtypeStruct((N_SLAB, 1, SLAB_ST), F32),
                   jax.ShapeDtypeStruct((N_SLAB, 1, SLAB_ST), F32),
                   jax.ShapeDtypeStruct((N_SLAB, 1, SLAB_CH), F32)],
        scratch_shapes=[scan_buf, scan_buf, scan_buf, scan_buf, pow_buf, pow_buf, pltpu.VMEM((L, SLAB_CH), F32)],
        compiler_params=_cp(("parallel",)),
    )(proj, dy, bre, bim, cre_t, cim_t, lam_re, lam_im, dvec)


def _glu_point(y0, pre, za, b):
    return y0 * jax.nn.sigmoid(pre + b) * jax.nn.silu(za)


def _glu_specs(L, tm):
    row = pl.BlockSpec((tm, SSM_WIDTH), lambda i: (i, 0))
    za = pl.BlockSpec((tm, SSM_WIDTH), lambda i: (i, OFF_ZA // SSM_WIDTH))
    wmat = pl.BlockSpec((SSM_WIDTH, SSM_WIDTH), lambda i: (0, 0))
    vec = pl.BlockSpec((1, SSM_WIDTH), lambda i: (0, 0))
    return row, za, wmat, vec


def _glu_fwd(ya0, proj, w, b, name):
    L = ya0.shape[0]
    tm = min(L, 512)
    row, za, wmat, vec = _glu_specs(L, tm)

    def body(y_ref, z_ref, w_ref, b_ref, o_ref):
        y0 = y_ref[...]
        pre = _dot(y0, w_ref[...])
        o_ref[...] = _glu_point(y0.astype(F32), pre, z_ref[...].astype(F32), b_ref[...]).astype(BF16)

    return pl.pallas_call(
        body, name=name, grid=(L // tm,), in_specs=[row, za, wmat, vec], out_specs=row,
        out_shape=jax.ShapeDtypeStruct((L, SSM_WIDTH), BF16), compiler_params=_cp(("parallel",)),
    )(ya0, proj, w, b)


def _glu_bwd(ya0, proj, w, b, dya, name):
    L = ya0.shape[0]
    tm = min(L, 512)
    row, za, wmat, vec = _glu_specs(L, tm)

    def body(y_ref, z_ref, w_ref, b_ref, g_ref, dy0_ref, dza_ref, dw_ref, db_ref):
        y0 = y_ref[...]
        pre = _dot(y0, w_ref[...])
        _, vjp = jax.vjp(_glu_point, y0.astype(F32), pre, z_ref[...].astype(F32), b_ref[...])
        dy0, dpre, dza, db = vjp(g_ref[...].astype(F32))
        dpb = dpre.astype(BF16)
        dy0_ref[...] = (dy0 + _dot_nt(dpb, w_ref[...])).astype(BF16)
        dza_ref[...] = dza.astype(BF16)

        @pl.when(pl.program_id(0) == 0)
        def _():
            dw_ref[...] = jnp.zeros_like(dw_ref)
            db_ref[...] = jnp.zeros_like(db_ref)

        dw_ref[...] += _dot_tn(y0, dpb)
        db_ref[...] += db

    return pl.pallas_call(
        body, name=name, grid=(L // tm,), in_specs=[row, za, wmat, vec, row],
        out_specs=[row, row, wmat, vec],
        out_shape=[jax.ShapeDtypeStruct((L, SSM_WIDTH), BF16), jax.ShapeDtypeStruct((L, SSM_WIDTH), BF16),
                   jax.ShapeDtypeStruct((SSM_WIDTH, SSM_WIDTH), F32), jax.ShapeDtypeStruct((1, SSM_WIDTH), F32)],
        compiler_params=_cp(("arbitrary",)),
    )(ya0, proj, w, b, dya)


def _sg_norm(vb, ln_w, ln_b):
    v0 = jax.nn.gelu(vb)
    mu = jnp.mean(v0, axis=-1, keepdims=True)
    var = jnp.mean(jnp.square(v0 - mu), axis=-1, keepdims=True)
    return (v0 - mu) * lax.rsqrt(var + EPS) * ln_w + ln_b


def _sg_gate(ub, mixed, zb):
    return jax.nn.gelu(ub) * mixed * jax.nn.silu(zb)


def _sg_specs():
    W = SSM_WIDTH
    blk = lambda off: pl.BlockSpec((CHUNK, W), lambda n: (n, off // W))
    out = pl.BlockSpec((CHUNK, W), lambda n: (n, 0))
    vec = pl.BlockSpec((1, W), lambda n: (0, 0))
    wsp = pl.BlockSpec((SG_HEADS, CHUNK, CHUNK), lambda n: (0, 0, 0))
    bsp = pl.BlockSpec((SG_HEADS, CHUNK, 1), lambda n: (0, 0, 0))
    return blk, out, vec, wsp, bsp


def _sg_masked(w_ref):
    t = lax.broadcasted_iota(jnp.int32, (CHUNK, CHUNK), 0)
    s = lax.broadcasted_iota(jnp.int32, (CHUNK, CHUNK), 1)
    causal = s <= t
    return causal, [jnp.where(causal, w_ref[h], 0.0).astype(BF16) for h in range(SG_HEADS)]


def _sg_mix(wm, vnb, bias_ref):
    return jnp.concatenate(
        [_dot(wm[h], vnb[:, h * CHUNK:(h + 1) * CHUNK]) + bias_ref[h] for h in range(SG_HEADS)], axis=-1)


def _sg_fwd(proj, ln_w, ln_b, w, bias, name):
    L = proj.shape[0]
    blk, out, vec, wsp, bsp = _sg_specs()

    def body(ub_ref, vb_ref, zb_ref, lw_ref, lb_ref, w_ref, bias_ref, o_ref):
        _, wm = _sg_masked(w_ref)
        vnb = _sg_norm(vb_ref[...].astype(F32), lw_ref[...], lb_ref[...]).astype(BF16)
        mixed = _sg_mix(wm, vnb, bias_ref)
        o_ref[...] = _sg_gate(ub_ref[...].astype(F32), mixed, zb_ref[...].astype(F32)).astype(BF16)

    return pl.pallas_call(
        body, name=name, grid=(L // CHUNK,),
        in_specs=[blk(OFF_UB), blk(OFF_VB), blk(OFF_ZB), vec, vec, wsp, bsp], out_specs=out,
        out_shape=jax.ShapeDtypeStruct((L, SSM_WIDTH), BF16), compiler_params=_cp(("parallel",)),
    )(proj, proj, proj, ln_w, ln_b, w, bias)


def _sg_bwd(proj, ln_w, ln_b, w, bias, dyb, name):
    L = proj.shape[0]
    blk, out, vec, wsp, bsp = _sg_specs()

    def body(ub_ref, vb_ref, zb_ref, lw_ref, lb_ref, w_ref, bias_ref, g_ref,
             dub_ref, dvb_ref, dzb_ref, dlw_ref, dlb_ref, dw_ref, dbias_ref):
        causal, wm = _sg_masked(w_ref)
        vb = vb_ref[...].astype(F32)
        vn, vjp_norm = jax.vjp(_sg_norm, vb, lw_ref[...], lb_ref[...])
        vnb = vn.astype(BF16)
        mixed = _sg_mix(wm, vnb, bias_ref)
        _, vjp_gate = jax.vjp(_sg_gate, ub_ref[...].astype(F32), mixed, zb_ref[...].astype(F32))
        dub, dmixed, dzb = vjp_gate(g_ref[...].astype(F32))
        dub_ref[...] = dub.astype(BF16)
        dzb_ref[...] = dzb.astype(BF16)

        @pl.when(pl.program_id(0) == 0)
        def _():
            dlw_ref[...] = jnp.zeros_like(dlw_ref)
            dlb_ref[...] = jnp.zeros_like(dlb_ref)
            dw_ref[...] = jnp.zeros_like(dw_ref)
            dbias_ref[...] = jnp.zeros_like(dbias_ref)

        dvn = []
        for h in range(SG_HEADS):
            dm = dmixed[:, h * CHUNK:(h + 1) * CHUNK]
            dmb = dm.astype(BF16)
            dbias_ref[h] += jnp.sum(dm, axis=-1, keepdims=True)
            dw_ref[h] += jnp.where(causal, _dot_nt(dmb, vnb[:, h * CHUNK:(h + 1) * CHUNK]), 0.0)
            dvn.append(_dot_tn(wm[h], dmb))
        dvb, dlw, dlb = vjp_norm(jnp.concatenate(dvn, axis=-1))
        dvb_ref[...] = dvb.astype(BF16)
        dlw_ref[...] += dlw
        dlb_ref[...] += dlb

    act = jax.ShapeDtypeStruct((L, SSM_WIDTH), BF16)
    return pl.pallas_call(
        body, name=name, grid=(L // CHUNK,),
        in_specs=[blk(OFF_UB), blk(OFF_VB), blk(OFF_ZB), vec, vec, wsp, bsp, out],
        out_specs=[out, out, out, vec, vec, wsp, bsp],
        out_shape=[act, act, act, jax.ShapeDtypeStruct((1, SSM_WIDTH), F32), jax.ShapeDtypeStruct((1, SSM_WIDTH), F32),
                   jax.ShapeDtypeStruct((SG_HEADS, CHUNK, CHUNK), F32), jax.ShapeDtypeStruct((SG_HEADS, CHUNK, 1), F32)],
        compiler_params=_cp(("arbitrary",)),
    )(proj, proj, proj, ln_w, ln_b, w, bias, dyb)


def _rope_tables(L):
    half = ROT_DIM // 2
    inv_freq = ROPE_THETA ** (-jnp.arange(0, ROT_DIM, 2, dtype=F32) / ROT_DIM)
    ang = jnp.arange(L, dtype=F32)[:, None] * inv_freq[None, :]
    cos, sin = jnp.cos(ang), jnp.sin(ang)
    ones = jnp.ones((L, HEAD_DIM - ROT_DIM), F32)
    cos_h = jnp.concatenate([cos, cos, ones], axis=-1)
    sin_h = jnp.concatenate([-sin, sin, 0.0 * ones], axis=-1)
    src = jnp.arange(HEAD_DIM)[:, None]
    dst = jnp.arange(HEAD_DIM)[None, :]
    p_h = (((dst < half) & (src == dst + half)) | ((dst >= half) & (dst < ROT_DIM) & (src == dst - half))).astype(F32)
    p2 = jnp.kron(jnp.eye(2, dtype=F32), p_h).astype(BF16)
    return jnp.tile(cos_h, (1, 2)), jnp.tile(sin_h, (1, 2)), p2


def _rope(t, cos, sin, p2):
    n = t.shape[1] // 128
    tb = t.astype(BF16)
    sw = jnp.concatenate([_dot(tb[:, i * 128:(i + 1) * 128], p2) for i in range(n)], axis=-1) if n > 1 else _dot(tb, p2)
    return t * jnp.tile(cos, (1, n)) + sw * jnp.tile(sin, (1, n))


def _rope_t(g, cos, sin, p2):
    n = g.shape[1] // 128
    gs = (g * jnp.tile(sin, (1, n))).astype(BF16)
    sw = jnp.concatenate([_dot_nt(gs[:, i * 128:(i + 1) * 128], p2) for i in range(n)], axis=-1) if n > 1 else _dot_nt(gs, p2)
    return g * jnp.tile(cos, (1, n)) + sw


def _lane_lo(shape):
    return (lax.broadcasted_iota(jnp.int32, shape, len(shape) - 1) % 128) < HEAD_DIM


def _dup_halves(x):
    xr = pltpu.roll(x, HEAD_DIM, 1)
    lo = _lane_lo(x.shape)
    return jnp.where(lo, x, xr), jnp.where(lo, xr, x)


def _fold_halves(d0, d1):
    f0 = d0 + pltpu.roll(d0, HEAD_DIM, 1)
    f1 = d1 + pltpu.roll(d1, HEAD_DIM, 1)
    return jnp.where(_lane_lo(d0.shape), f0, f1)


def _attn_mask():
    qi = lax.broadcasted_iota(jnp.int32, (CHUNK, 2 * CHUNK), 0)
    kj = lax.broadcasted_iota(jnp.int32, (CHUNK, 2 * CHUNK), 1)
    return qi, kj


def _attn_specs():
    qsp = pl.BlockSpec((CHUNK, 1024), lambda n: (n, OFF_Q // 1024))
    kv_cur = pl.BlockSpec((CHUNK, 256), lambda n: (n, OFF_KV // 256))
    kv_prev = pl.BlockSpec((CHUNK, 256), lambda n: (jnp.maximum(n - 1, 0), OFF_KV // 256))
    zsp = [pl.BlockSpec((CHUNK, 256), functools.partial(lambda n, q: (n, OFF_ZC // 256 + q), q=q)) for q in range(4)]
    tab_cur = pl.BlockSpec((CHUNK, 128), lambda n: (n, 0))
    tab_prev = pl.BlockSpec((CHUNK, 128), lambda n: (jnp.maximum(n - 1, 0), 0))
    p2sp = pl.BlockSpec((128, 128), lambda n: (0, 0))
    sink = pl.BlockSpec(memory_space=pltpu.SMEM)
    wide = pl.BlockSpec((CHUNK, 1024), lambda n: (n, 0))
    return qsp, kv_cur, kv_prev, zsp, tab_cur, tab_prev, p2sp, sink, wide


def _attn_core(n, q_ref, kvc_ref, kvp_ref, cosc_ref, sinc_ref, cosp_ref, sinp_ref, p2_ref, sink_ref):
    p2 = p2_ref[...]
    qr = _rope(q_ref[...].astype(F32), cosc_ref[...], sinc_ref[...], p2).astype(BF16)
    kc = _rope(kvc_ref[:, 0:128].astype(F32), cosc_ref[...], sinc_ref[...], p2)
    kp = _rope(kvp_ref[:, 0:128].astype(F32), cosp_ref[...], sinp_ref[...], p2)
    k_all = jnp.concatenate([kp, kc], axis=0).astype(BF16)
    v_all = jnp.concatenate([kvp_ref[:, 128:256], kvc_ref[:, 128:256]], axis=0)
    kd = _dup_halves(k_all)
    vd = _dup_halves(v_all)
    qi, kj = _attn_mask()
    allowed = ((kj < CHUNK) & (kj > qi) & (n > 0)) | ((kj >= CHUNK) & (kj - CHUNK <= qi))
    lo = _lane_lo((CHUNK, 128))
    probs = []
    for h in range(ATT_HEADS):
        m, half, g = h // 2, h % 2, h // 8
        qp = qr[:, m * 128:(m + 1) * 128]
        qm = jnp.where(lo if half == 0 else ~lo, qp, jnp.zeros_like(qp))
        s = jnp.where(allowed, _dot_nt(qm, kd[g]) * (HEAD_DIM ** -0.5), NEG_INF)
        snk = sink_ref[h]
        mx = jnp.maximum(jnp.max(s, axis=-1, keepdims=True), snk)
        e = jnp.exp(s - mx)
        es = jnp.exp(snk - mx)
        inv = 1.0 / (jnp.sum(e, axis=-1, keepdims=True) + es)
        probs.append((qm, e * inv, es * inv))
    return qr, kd, vd, probs, lo


def _attn_out(vd, probs, lo):
    outs = []
    for m in range(ATT_HEADS // 2):
        g = m // 4
        o0 = _dot(probs[2 * m][1].astype(BF16), vd[g])
        o1 = _dot(probs[2 * m + 1][1].astype(BF16), vd[g])
        outs.append(jnp.where(lo, o0, o1))
    return jnp.concatenate(outs, axis=-1)


def _silu_gate(o, z):
    return o * jax.nn.silu(z)


def _attn_fwd(proj, sinks, tabs, name):
    L = proj.shape[0]
    cos2, sin2, p2 = tabs
    qsp, kv_cur, kv_prev, zsp, tab_cur, tab_prev, p2sp, sink, wide = _attn_specs()

    def body(q_ref, kvc_ref, kvp_ref, z0, z1, z2, z3, cosc, sinc, cosp, sinp, p2_ref, sink_ref, o_ref):
        n = pl.program_id(0)
        _, _, vd, probs, lo = _attn_core(n, q_ref, kvc_ref, kvp_ref, cosc, sinc, cosp, sinp, p2_ref, sink_ref)
        o = _attn_out(vd, probs, lo)
        z = jnp.concatenate([z0[...], z1[...], z2[...], z3[...]], axis=-1).astype(F32)
        o_ref[...] = _silu_gate(o, z).astype(BF16)

    return pl.pallas_call(
        body, name=name, grid=(L // CHUNK,),
        in_specs=[qsp, kv_cur, kv_prev, *zsp, tab_cur, tab_cur, tab_prev, tab_prev, p2sp, sink],
        out_specs=wide, out_shape=jax.ShapeDtypeStruct((L, 1024), BF16), compiler_params=_cp(("parallel",)),
    )(proj, proj, proj, proj, proj, proj, proj, cos2, sin2, cos2, sin2, p2, sinks)


def _attn_bwd(proj, sinks, tabs, dyc, name):
    L = proj.shape[0]
    cos2, sin2, p2 = tabs
    qsp, kv_cur, kv_prev, zsp, tab_cur, tab_prev, p2sp, sink, wide = _attn_specs()
    kvo = pl.BlockSpec((CHUNK, 256), lambda n: (n, 0))

    def body(q_ref, kvc_ref, kvp_ref, z0, z1, z2, z3, cosc, sinc, cosp, sinp, p2_ref, sink_ref, g_ref,
             dq_ref, dz_ref, dkvc_ref, dkvp_ref, dsink_ref):
        n = pl.program_id(0)
        _, kd, vd, probs, lo = _attn_core(n, q_ref, kvc_ref, kvp_ref, cosc, sinc, cosp, sinp, p2_ref, sink_ref)
        o = _attn_out(vd, probs, lo)
        z = jnp.concatenate([z0[...], z1[...], z2[...], z3[...]], axis=-1).astype(F32)
        _, vjp = jax.vjp(_silu_gate, o, z)
        do, dz = vjp(g_ref[...].astype(F32))
        dz_ref[...] = dz.astype(BF16)

        @pl.when(n == 0)
        def _():
            dsink_ref[...] = jnp.zeros_like(dsink_ref)

        dkd = [jnp.zeros((2 * CHUNK, 128), F32), jnp.zeros((2 * CHUNK, 128), F32)]
        dvd = [jnp.zeros((2 * CHUNK, 128), F32), jnp.zeros((2 * CHUNK, 128), F32)]
        dq_pairs = []
        for m in range(ATT_HEADS // 2):
            g = m // 4
            dop = do[:, m * 128:(m + 1) * 128].astype(BF16)
            dq_h = []
            for half in range(2):
                h = 2 * m + half
                qm, p, ps = probs[h]
                dom = jnp.where(lo if half == 0 else ~lo, dop, jnp.zeros_like(dop))
                dp = _dot_nt(dom, vd[g])
                rs = jnp.sum(p * dp, axis=-1, keepdims=True)
                ds = (p * (dp - rs) * (HEAD_DIM ** -0.5)).astype(BF16)
                dsink_ref[h:h + 1, :] += jnp.broadcast_to(jnp.sum(-ps * rs, axis=0, keepdims=True), (1, 128))
                dq_h.append(_dot(ds, kd[g]))
                dkd[g] = dkd[g] + _dot_tn(ds, qm)
                dvd[g] = dvd[g] + _dot_tn(p.astype(BF16), dom)
            dq_pairs.append(jnp.where(lo, dq_h[0], dq_h[1]))
        p2 = p2_ref[...]
        dq_ref[...] = _rope_t(jnp.concatenate(dq_pairs, axis=-1), cosc[...], sinc[...], p2).astype(BF16)
        dk_rot = _fold_halves(dkd[0], dkd[1])
        dv = _fold_halves(dvd[0], dvd[1])
        dkp = _rope_t(dk_rot[0:CHUNK], cosp[...], sinp[...], p2)
        dkc = _rope_t(dk_rot[CHUNK:2 * CHUNK], cosc[...], sinc[...], p2)
        dkvp_ref[...] = jnp.concatenate([dkp, dv[0:CHUNK]], axis=-1)
        dkvc_ref[...] = jnp.concatenate([dkc, dv[CHUNK:2 * CHUNK]], axis=-1)

    act = jax.ShapeDtypeStruct((L, 1024), BF16)
    kvs = jax.ShapeDtypeStruct((L, 256), F32)
    return pl.pallas_call(
        body, name=name, grid=(L // CHUNK,),
        in_specs=[qsp, kv_cur, kv_prev, *zsp, tab_cur, tab_cur, tab_prev, tab_prev, p2sp, sink, wide],
        out_specs=[wide, wide, kvo, kvo, pl.BlockSpec((ATT_HEADS, 128), lambda n: (0, 0))],
        out_shape=[act, act, kvs, kvs, jax.ShapeDtypeStruct((ATT_HEADS, 128), F32)],
        compiler_params=_cp(("arbitrary",)),
    )(proj, proj, proj, proj, proj, proj, proj, cos2, sin2, cos2, sin2, p2, sinks, dyc)


MERGE_TN = 256


def _merge_point(ta, tb, tc, ga, gb, gc):
    return jax.nn.sigmoid(ga) * ta + jax.nn.sigmoid(gb) * tb + jax.nn.sigmoid(gc) * tc


def _merge_specs(tm):
    nj = D_MODEL // MERGE_TN
    t = pl.BlockSpec((tm, MERGE_TN), lambda i, j: (i, j))
    gates = [pl.BlockSpec((tm, MERGE_TN), functools.partial(lambda i, j, b: (i, OFF_G // MERGE_TN + b * nj + j), b=b))
             for b in range(3)]
    return t, gates, nj


def _merge_fwd(ta, tb, tc, proj, name):
    L = ta.shape[0]
    tm = min(L, 1024)
    t, gates, nj = _merge_specs(tm)

    def body(ta_ref, tb_ref, tc_ref, ga_ref, gb_ref, gc_ref, o_ref):
        f = lambda r: r[...].astype(F32)
        o_ref[...] = _merge_point(f(ta_ref), f(tb_ref), f(tc_ref), f(ga_ref), f(gb_ref), f(gc_ref)).astype(BF16)

    return pl.pallas_call(
        body, name=name, grid=(L // tm, nj), in_specs=[t, t, t, *gates], out_specs=t,
        out_shape=jax.ShapeDtypeStruct((L, D_MODEL), BF16), compiler_params=_cp(("parallel", "parallel")),
    )(ta, tb, tc, proj, proj, proj)


def _merge_bwd(ta, tb, tc, proj, dm, name):
    L = ta.shape[0]
    tm = min(L, 1024)
    t, gates, nj = _merge_specs(tm)

    def body(ta_ref, tb_ref, tc_ref, ga_ref, gb_ref, gc_ref, dm_ref, dta_ref, dtb_ref, dtc_ref, dga_ref, dgb_ref, dgc_ref):
        f = lambda r: r[...].astype(F32)
        _, vjp = jax.vjp(_merge_point, f(ta_ref), f(tb_ref), f(tc_ref), f(ga_ref), f(gb_ref), f(gc_ref))
        outs = vjp(f(dm_ref))
        for r, v in zip((dta_ref, dtb_ref, dtc_ref, dga_ref, dgb_ref, dgc_ref), outs):
            r[...] = v.astype(BF16)

    act = jax.ShapeDtypeStruct((L, D_MODEL), BF16)
    return pl.pallas_call(
        body, name=name, grid=(L // tm, nj), in_specs=[t, t, t, *gates, t],
        out_specs=[t] * 6, out_shape=[act] * 6,
        compiler_params=_cp(("parallel", "parallel")),
    )(ta, tb, tc, proj, proj, proj, dm)


GRAD_DT = BF16
SMALL = ("norm_w", "ssm_a_re", "ssm_a_im", "ssm_log_dt", "ssm_b_re", "ssm_b_im", "ssm_c_re", "ssm_c_im", "ssm_d",
         "ssm_glu_b", "sg_ln_w", "sg_ln_b", "sg_w", "sg_b", "attn_sinks")
G8 = SSM_GROUPS // N_SLAB


def _diag_mask(rows_per_group, cols_per_group):
    r = jnp.arange(G8 * rows_per_group)[:, None] // rows_per_group
    c = jnp.arange(G8 * cols_per_group)[None, :] // cols_per_group
    return r == c


def _slab_b(bb_t):
    x = bb_t.transpose(1, 0, 2).reshape(N_SLAB, SLAB_CH, SSM_STATE)
    return jnp.where(_diag_mask(SSM_GROUP, SSM_STATE), jnp.tile(x, (1, 1, G8)), 0)


def _unslab_b(d):
    x = jnp.where(_diag_mask(SSM_GROUP, SSM_STATE), d, 0).reshape(N_SLAB, SLAB_CH, G8, SSM_STATE).sum(axis=2)
    return x.reshape(SSM_GROUPS, SSM_GROUP, SSM_STATE).transpose(1, 0, 2)


def _slab_c(c):
    x = c.transpose(0, 2, 1).reshape(N_SLAB, SLAB_ST, SSM_GROUP)
    return jnp.where(_diag_mask(SSM_STATE, SSM_GROUP), jnp.tile(x, (1, 1, G8)), 0)


def _unslab_c(d):
    x = jnp.where(_diag_mask(SSM_STATE, SSM_GROUP), d, 0).reshape(N_SLAB, SLAB_ST, G8, SSM_GROUP).sum(axis=2)
    return x.reshape(SSM_GROUPS, SSM_STATE, SSM_GROUP).transpose(0, 2, 1)


def _s5_prep(p, tag):
    bt_re = p["ssm_b_re"].transpose(2, 0, 1)
    bt_im = p["ssm_b_im"].transpose(2, 0, 1)
    raw = (p["ssm_a_re"], p["ssm_a_im"], p["ssm_log_dt"][:, None], bt_re, bt_im)
    lr, li, bbr, bbi = _s5_params_fwd(*raw, name=f"s5_params_{tag}")
    ops = (_slab_b(bbr).astype(BF16), _slab_b(bbi).astype(BF16),
           _slab_c(p["ssm_c_re"]).astype(BF16), _slab_c(p["ssm_c_im"]).astype(BF16),
           jnp.broadcast_to(lr.reshape(N_SLAB, 1, SLAB_ST), (N_SLAB, SUB, SLAB_ST)),
           jnp.broadcast_to(li.reshape(N_SLAB, 1, SLAB_ST), (N_SLAB, SUB, SLAB_ST)),
           p["ssm_d"].reshape(N_SLAB, 1, SLAB_CH))
    return raw, ops


def _layer_fwd(x, p, w, tabs, tag, s5=None, proj_of=None, after_proj=None):
    L = x.shape[0]
    h = _rms_fwd(x, p["norm_w"][None], f"rms_fwd_{tag}")
    if proj_of is not None:
        proj = proj_of(h)
    else:
        proj = _mm(h, w["win_t"], "nt", BF16, L, PROJ_TN, D_MODEL, f"in_proj_{tag}")
    if after_proj is not None:
        w = after_proj(proj)
    s5_raw, s5_ops = s5 if s5 is not None else _s5_prep(p, tag)
    ya0 = _s5_fwd(proj, *s5_ops, name=f"s5_fwd_{tag}")
    ya = _glu_fwd(ya0, proj, w["glu"], p["ssm_glu_b"][None], f"glu_fwd_{tag}")
    yb = _sg_fwd(proj, p["sg_ln_w"][None], p["sg_ln_b"][None], p["sg_w"], p["sg_b"][:, :, None], f"sg_fwd_{tag}")
    yc = _attn_fwd(proj, p["attn_sinks"], tabs, f"attn_fwd_{tag}")
    ta = _mm(ya, w["wba_t"], "nt", BF16, 1024, 1024, 1024, f"branch_a_{tag}")
    tb = _mm(yb, w["wbb_t"], "nt", BF16, 1024, 1024, 1024, f"branch_b_{tag}")
    tc = _mm(yc, w["wbc_t"], "nt", BF16, 1024, 1024, 1024, f"branch_c_{tag}")
    merged = _merge_fwd(ta, tb, tc, proj, f"merge_fwd_{tag}")
    x_new = _mm(merged, w["wout"], "nn", F32, 1024, 512, D_MODEL, f"out_proj_{tag}", res=x)
    saved = dict(x=x, h=h, proj=proj, s5_raw=s5_raw, s5_ops=s5_ops, ya0=ya0, ya=ya, yb=yb, yc=yc,
                 ta=ta, tb=tb, tc=tc, merged=merged)
    return x_new, saved


def _layer_bwd(dx_out, p, w, tabs, s, tag, first_after=None, after_merge=None, before_win=None, after_win=None):
    L = dx_out.shape[0]
    proj = s["proj"]
    big, small = {}, {}
    dmerged = _mm(dx_out, w["wout"], "nt", BF16, 1024, 512, D_MODEL, f"d_merged_{tag}", after=first_after)
    big["wout"] = _mm(s["merged"], dx_out, "tn", GRAD_DT, 512, 1024, L, f"d_wout_{tag}")
    dta, dtb, dtc, dga, dgb, dgc = _merge_bwd(s["ta"], s["tb"], s["tc"], proj, dmerged, f"merge_bwd_{tag}")
    tok = after_merge(dga) if after_merge is not None else None
    dy = {}
    for br, dt in (("a", dta), ("b", dtb), ("c", dtc)):
        dy[br] = _mm(dt, w[f"wb{br}_t"], "nn", BF16, 1024, 1024, D_MODEL, f"d_y{br}_{tag}", after=tok)
        big[f"wb{br}_t"] = _mm(dt, s[f"y{br}"], "tn", GRAD_DT, 512, 1024, L, f"d_wb{br}_{tag}")

    dq, dzc, dkvc, dkvp, dsink = _attn_bwd(proj, p["attn_sinks"], tabs, dy["c"], f"attn_bwd_{tag}")
    dkv = dkvc + jnp.concatenate([dkvp[CHUNK:], jnp.zeros((CHUNK, 256), F32)], axis=0)
    small["attn_sinks"] = dsink[:, 0]

    dub, dvb, dzb, dlw, dlb, dsgw, dsgb = _sg_bwd(
        proj, p["sg_ln_w"][None], p["sg_ln_b"][None], p["sg_w"], p["sg_b"][:, :, None], dy["b"], f"sg_bwd_{tag}")
    small.update(sg_ln_w=dlw[0], sg_ln_b=dlb[0], sg_w=dsgw, sg_b=dsgb[:, :, 0])

    dya0, dza, dglu, dglub = _glu_bwd(s["ya0"], proj, w["glu"], p["ssm_glu_b"][None], dy["a"], f"glu_bwd_{tag}")
    big["glu"] = dglu.astype(GRAD_DT)
    small["ssm_glu_b"] = dglub[0]

    dua, dbre, dbim, dcre, dcim, dlr, dli, dd = _s5_bwd(proj, dya0, *s["s5_ops"], name=f"s5_bwd_{tag}")
    da_re, da_im, dlog_dt, dbt_re, dbt_im = _s5_params_bwd(
        *s["s5_raw"], dlr.reshape(SSM_GROUPS, SSM_STATE), dli.reshape(SSM_GROUPS, SSM_STATE),
        _unslab_b(dbre), _unslab_b(dbim), name=f"s5_params_bwd_{tag}")
    small.update(ssm_a_re=da_re, ssm_a_im=da_im, ssm_log_dt=dlog_dt[:, 0],
                 ssm_bt_re=dbt_re, ssm_bt_im=dbt_im,
                 ssm_c_re=_unslab_c(dcre), ssm_c_im=_unslab_c(dcim), ssm_d=dd.reshape(SSM_WIDTH))

    dproj = jnp.concatenate([dua, dza, dub, dvb, dzb, dq, dkv.astype(BF16), dzc, dga, dgb, dgc], axis=-1)
    tok = before_win(big) if before_win is not None else None
    big["win_t"] = _mm(dproj, s["h"], "tn", GRAD_DT, 256, D_MODEL, L, f"d_win_{tag}", after=tok)
    tok = after_win(big) if after_win is not None else None
    dh = _mm(dproj, w["win_t"], "nn", F32, L, D_MODEL, 256, f"d_h_{tag}", after=tok)
    dx_in, dnw = _rms_bwd(s["x"], p["norm_w"][None], dh, dx_out, f"rms_bwd_{tag}")
    small["norm_w"] = dnw[0]
    return dx_in, big, small


def _local_step(x, tgt, small_p, final_w, big_w):
    L = x.shape[0]
    tabs = _rope_tables(L)
    saved = []
    for l in range(DEPTH):
        x, s = _layer_fwd(x, small_p[l], big_w[l], tabs, f"l{l}")
        saved.append(s)
    loss_acc, dx, dfw = _final(x, final_w[None], tgt, "final_norm_loss")
    big_g, small_g = [None] * DEPTH, [None] * DEPTH
    for l in reversed(range(DEPTH)):
        dx, big_g[l], small_g[l] = _layer_bwd(dx, small_p[l], big_w[l], tabs, saved[l], f"l{l}")
    return loss_acc[0, 0], dx, dfw[0], big_g, small_g


MESH = pl.DeviceIdType.MESH
ANY = pl.BlockSpec(memory_space=pl.ANY)
ROW_ALIGN = 16


def _place():
    return lax.axis_index("x"), lax.axis_index("y"), lax.axis_index("c")


HBM = pl.BlockSpec(memory_space=pltpu.HBM)
SEM = pl.BlockSpec(memory_space=pltpu.SEMAPHORE)
EFFECT = pltpu.SideEffectType.DATAFLOW_SIDE_EFFECTING


def _split_start(srcs, lands, n_copies, copies, name, after=None):
    n, m, k = len(srcs), len(lands), n_copies
    extra = [] if after is None else [after]

    def body(*refs):
        src_refs, land_refs = refs[:n], refs[n:n + m]
        sems = refs[n + m + len(extra):]
        send_sems, recv_sems, token = sems[:k], sems[k:2 * k], refs[-1]
        for cp in copies(src_refs, land_refs, send_sems, recv_sems):
            cp.start()
        token[...] = jnp.zeros_like(token)

    ops = list(srcs) + list(lands)
    outs = pl.pallas_call(
        body, name=name,
        out_shape=(*[pltpu.SemaphoreType.DMA(())] * (2 * k),
                   *[pltpu.HBM(a.shape, a.dtype) for a in ops], jax.ShapeDtypeStruct((8, 128), F32)),
        in_specs=[HBM] * (n + m) + [ANY] * len(extra),
        out_specs=(*[SEM] * (2 * k), *[HBM] * (n + m), pl.BlockSpec(memory_space=pltpu.VMEM)),
        input_output_aliases={i: 2 * k + i for i in range(n + m)},
        compiler_params=pltpu.CompilerParams(has_side_effects=EFFECT),
    )(*[pltpu.with_memory_space_constraint(a, pltpu.HBM) for a in ops], *extra)
    return (list(outs[:k]), list(outs[k:2 * k]), list(outs[2 * k:2 * k + n]), list(outs[2 * k + n:2 * k + n + m]),
            outs[-1])


def _split_wait(send_sems, recv_sems, srcs, lands, after, copies, name):
    n, m, k = len(srcs), len(lands), len(send_sems)
    after = list(after) if isinstance(after, (list, tuple)) else [after]

    def body(*refs):
        src_refs, land_refs = refs[:n], refs[n:n + m]
        for cp in copies(src_refs, land_refs, refs[n + m:n + m + k], refs[n + m + k:n + m + 2 * k]):
            cp.wait_send()
            cp.wait_recv()

    ops = list(srcs) + list(lands)
    outs = pl.pallas_call(
        body, name=name,
        out_shape=tuple(pltpu.HBM(a.shape, a.dtype) for a in ops),
        in_specs=[HBM] * (n + m) + [SEM] * (2 * k) + [ANY] * len(after),
        out_specs=tuple([HBM] * (n + m)),
        input_output_aliases={i: i for i in range(n + m)},
        compiler_params=pltpu.CompilerParams(has_side_effects=EFFECT),
    )(*ops, *send_sems, *recv_sems, *after)
    return list(outs[:n]), list(outs[n:])


def _ag_rows(land_ref, px, py, pc):
    r = land_ref.shape[0] // N_DEV
    start = pl.multiple_of((4 * px + 2 * py + pc) * r, ROW_ALIGN)
    return land_ref.at[pl.ds(start, r), :]


def _ag_copies_to(which):
    def copies(src_refs, land_refs, send_sems, recv_sems):
        x, y, c = _place()
        peers = [(x, y, 1 - c), (1 - x, y, c), (x, 1 - y, c), (1 - x, 1 - y, c)]
        return [pltpu.make_async_remote_copy(
            src_ref=_ag_rows(land_refs[a], x, y, c), dst_ref=_ag_rows(land_refs[a], x, y, c),
            send_sem=send_sems[len(which) * a + k], recv_sem=recv_sems[len(which) * a + k],
            device_id=peers[p], device_id_type=MESH)
            for a in range(len(land_refs)) for k, p in enumerate(which)]
    return copies


_ag_copies = _ag_copies_to((0, 1, 2, 3))
_ag_copies_near = _ag_copies_to((0, 1, 2))
_ag_copies_far = _ag_copies_to((3,))


def _ag_forward(lands, name, which=(0, 1, 2)):
    n = len(lands)

    def body(*refs):
        land_refs = refs[n:2 * n]
        send_sems, recv_sems = refs[2 * n:]
        x, y, c = _place()
        chips = [(1 - x, y), (x, 1 - y), (1 - x, 1 - y)]

        def copy(a, k, pc):
            px, py = chips[which[k]]
            return pltpu.make_async_remote_copy(
                src_ref=_ag_rows(land_refs[a], px, py, pc), dst_ref=_ag_rows(land_refs[a], px, py, pc),
                send_sem=send_sems.at[a, k], recv_sem=recv_sems.at[a, k], device_id=(x, y, 1 - c), device_id_type=MESH)

        passed = [copy(a, k, c) for a in range(n) for k in range(len(which))]
        for cp in passed:
            cp.start()
        for a in range(n):
            for k in range(len(which)):
                copy(a, k, 1 - c).wait_recv()
        for cp in passed:
            cp.wait_send()

    sems = pltpu.SemaphoreType.DMA((n, len(which)))
    return pl.pallas_call(
        body, name=name,
        in_specs=[ANY] * n, out_specs=[ANY] * n,
        out_shape=[jax.ShapeDtypeStruct(l.shape, l.dtype) for l in lands],
        input_output_aliases={i: i for i in range(n)},
        scratch_shapes=[sems, sems],
    )(*lands)


def _allgather_place(shards):
    x, y, c = _place()
    return [lax.dynamic_update_slice(lax.empty((N_DEV * s.shape[0], s.shape[1]), s.dtype), s,
                                     ((4 * x + 2 * y + c) * s.shape[0], 0)) for s in shards]


def _allgather_start(lands, name, after=None):
    return _split_start([], lands, 4 * len(lands), _ag_copies, name + "_start", after=after)


def _allgather_finish(started, after, name):
    send_sems, recv_sems, _, lands, _ = started
    _, lands = _split_wait(send_sems, recv_sems, [], lands, after, _ag_copies, name + "_wait")
    return list(_ag_forward(lands, name + "_forward"))


def _rs_swap_cores(grads, name):
    n = len(grads)

    def body(*refs):
        ins, outs = refs[:n], refs[n:2 * n]
        send_sems, recv_sems = refs[2 * n:]
        x, y, c = _place()
        cps = []
        for a in range(n):
            r = ins[a].shape[0] // N_DEV
            for q in range(4):
                start = pl.multiple_of((2 * q + 1 - c) * r, ROW_ALIGN)
                cps.append(pltpu.make_async_remote_copy(
                    src_ref=ins[a].at[pl.ds(start, r), :], dst_ref=outs[a].at[q],
                    send_sem=send_sems.at[a, q], recv_sem=recv_sems.at[a, q],
                    device_id=(x, y, 1 - c), device_id_type=MESH))
        for cp in cps:
            cp.start()
        for cp in cps:
            cp.wait()

    return pl.pallas_call(
        body, name=name, in_specs=[ANY] * n, out_specs=[ANY] * n,
        out_shape=[jax.ShapeDtypeStruct((4, g.shape[0] // N_DEV, g.shape[1]), g.dtype) for g in grads],
        scratch_shapes=[pltpu.SemaphoreType.DMA((n, 4)), pltpu.SemaphoreType.DMA((n, 4))],
    )(*grads)


def _rs_chip_copies(sum_refs, land_refs, send_sems, recv_sems):
    x, y, c = _place()
    chips = [(1 - x, y), (x, 1 - y), (1 - x, 1 - y)]
    return [pltpu.make_async_remote_copy(
        src_ref=sum_refs[a].at[2 * px + py], dst_ref=land_refs[a].at[2 * x + y],
        send_sem=send_sems[3 * a + j], recv_sem=recv_sems[3 * a + j], device_id=(px, py, c), device_id_type=MESH)
        for a in range(len(sum_refs)) for j, (px, py) in enumerate(chips)]


def _row_tile(r):
    return max(t for t in range(ROW_ALIGN, min(r, 1024) + 1, ROW_ALIGN) if r % t == 0)


def _rs_add_cores(grad, recv, cidx, name):
    r, cols = recv.shape[1], recv.shape[2]
    tr = _row_tile(r)
    nb = r // tr

    def body(c_ref, g_ref, r_ref, o_ref):
        o_ref[...] = (g_ref[...].astype(F32) + r_ref[...].astype(F32)).astype(o_ref.dtype)

    return pl.pallas_call(
        body, name=name,
        grid_spec=pltpu.PrefetchScalarGridSpec(
            num_scalar_prefetch=1, grid=(4, nb),
            in_specs=[pl.BlockSpec((tr, cols), lambda q, i, c_ref: ((2 * q + c_ref[0]) * nb + i, 0)),
                      pl.BlockSpec((None, tr, cols), lambda q, i, c_ref: (q, i, 0))],
            out_specs=pl.BlockSpec((None, tr, cols), lambda q, i, c_ref: (q, i, 0))),
        out_shape=jax.ShapeDtypeStruct(recv.shape, recv.dtype),
        compiler_params=_cp(("parallel", "parallel")),
    )(cidx, grad, recv)


def _rs_add_chips(own, recv, slots, name):
    r, cols = recv.shape[1], recv.shape[2]
    tr = _row_tile(r)

    def body(s_ref, o_ref, r0_ref, r1_ref, r2_ref, out_ref):
        acc = o_ref[...].astype(F32)
        for ref in (r0_ref, r1_ref, r2_ref):
            acc = acc + ref[...].astype(F32)
        out_ref[...] = acc

    pick = lambda k: pl.BlockSpec((None, tr, cols), functools.partial(lambda i, s_ref, k: (s_ref[k], i, 0), k=k))
    return pl.pallas_call(
        body, name=name,
        grid_spec=pltpu.PrefetchScalarGridSpec(
            num_scalar_prefetch=1, grid=(r // tr,),
            in_specs=[pick(0), pick(1), pick(2), pick(3)],
            out_specs=pl.BlockSpec((tr, cols), lambda i, s_ref: (i, 0))),
        out_shape=jax.ShapeDtypeStruct((r, cols), F32),
        compiler_params=_cp(("parallel",)),
    )(slots, own, recv, recv, recv)


def _rs_core_copies(grad_refs, land_refs, send_sems, recv_sems):
    x, y, c = _place()
    cps = []
    for a in range(len(grad_refs)):
        r = grad_refs[a].shape[0] // N_DEV
        for q in range(4):
            start = pl.multiple_of((2 * q + 1 - c) * r, ROW_ALIGN)
            cps.append(pltpu.make_async_remote_copy(
                src_ref=grad_refs[a].at[pl.ds(start, r), :], dst_ref=land_refs[a].at[q],
                send_sem=send_sems[4 * a + q], recv_sem=recv_sems[4 * a + q],
                device_id=(x, y, 1 - c), device_id_type=MESH))
    return cps


def _reduce_scatter_chips_start(grads, recv, tag):
    cidx = lax.axis_index("c").astype(jnp.int32)[None]
    sums = [_rs_add_cores(g, rv, cidx, f"rs_add_cores_{tag}_{i}") for i, (g, rv) in enumerate(zip(grads, recv))]
    lands = [lax.empty(s.shape, s.dtype) for s in sums]
    return _split_start(sums, lands, 3 * len(sums), _rs_chip_copies, f"rs_chips_{tag}_start")


def _reduce_scatter_start(grads, tag):
    return _reduce_scatter_chips_start(grads, _rs_swap_cores(grads, f"rs_swap_cores_{tag}"), tag)


def _reduce_scatter_cores_start(grads, tag):
    lands = [lax.empty((4, g.shape[0] // N_DEV, g.shape[1]), g.dtype) for g in grads]
    return _split_start(grads, lands, 4 * len(grads), _rs_core_copies, f"rs_cores_{tag}_start")


def _reduce_scatter_cores_finish(started, after, tag):
    send_sems, recv_sems, grads, lands, _ = started
    grads, recv = _split_wait(send_sems, recv_sems, grads, lands, after, _rs_core_copies, f"rs_cores_{tag}_wait")
    return _reduce_scatter_chips_start(grads, recv, tag)


def _reduce_scatter_finish(started, after, tag):
    send_sems, recv_sems, sums, lands, _ = started
    sums, lands = _split_wait(send_sems, recv_sems, sums, lands, after, _rs_chip_copies, f"rs_chips_{tag}_wait")
    x, y = lax.axis_index("x"), lax.axis_index("y")
    slots = jnp.stack([2 * x + y, 2 * (1 - x) + y, 2 * x + 1 - y, 2 * (1 - x) + 1 - y]).astype(jnp.int32)
    return [_rs_add_chips(s, l, slots, f"rs_add_chips_{tag}_{i}") for i, (s, l) in enumerate(zip(sums, lands))]


def _allreduce_small(packs, name, after=()):
    n = len(packs)
    after = list(after)
    assert all(p.shape[0] % (8 * N_DEV) == 0 for p in packs)

    def body(*refs):
        p_refs = refs[:n]
        refs = refs[n + len(after):]
        o_refs, part_refs = refs[:n], refs[n:2 * n]
        send1, recv1, send2, recv2 = refs[2 * n:]
        x, y, c = _place()
        me = 4 * x + 2 * y + c

        def block(ref, d):
            rs = ref.shape[0] // N_DEV
            return ref.at[pl.ds(pl.multiple_of(d * rs, 8), rs), :]

        peers = [(1 - x if k & 4 else x, 1 - y if k & 2 else y, 1 - c if k & 1 else c) for k in range(1, N_DEV)]
        scatter = [pltpu.make_async_remote_copy(
            src_ref=block(p_refs[a], 4 * px + 2 * py + pc), dst_ref=part_refs[a].at[me],
            send_sem=send1.at[a, k], recv_sem=recv1.at[a, k], device_id=(px, py, pc), device_id_type=MESH)
            for a in range(n) for k, (px, py, pc) in enumerate(peers)]
        for cp in scatter:
            cp.start()
        for a in range(n):
            part_refs[a][me] = block(p_refs[a], me)[...]
        for cp in scatter:
            cp.wait()
        for a in range(n):
            acc = part_refs[a][0]
            for d in range(1, N_DEV):
                acc = acc + part_refs[a][d]
            block(o_refs[a], me)[...] = acc
        gather = [pltpu.make_async_remote_copy(
            src_ref=block(o_refs[a], me), dst_ref=block(o_refs[a], me), send_sem=send2.at[a, k], recv_sem=recv2.at[a, k],
            device_id=peer, device_id_type=MESH) for a in range(n) for k, peer in enumerate(peers)]
        for cp in gather:
            cp.start()
        for a in range(n):
            for k, (px, py, pc) in enumerate(peers):
                theirs = block(o_refs[a], 4 * px + 2 * py + pc)
                pltpu.make_async_remote_copy(
                    src_ref=theirs, dst_ref=theirs, send_sem=send2.at[a, k], recv_sem=recv2.at[a, k],
                    device_id=(px, py, pc), device_id_type=MESH).wait_recv()
        for cp in gather:
            cp.wait_send()

    sems = pltpu.SemaphoreType.DMA((n, N_DEV - 1))
    vmem = pl.BlockSpec(memory_space=pltpu.VMEM)
    return pl.pallas_call(
        body, name=name,
        in_specs=[vmem] * n + [ANY] * len(after), out_specs=[vmem] * n,
        out_shape=[jax.ShapeDtypeStruct(p.shape, F32) for p in packs],
        scratch_shapes=[pltpu.VMEM((N_DEV, p.shape[0] // N_DEV, p.shape[1]), F32) for p in packs] + [sems] * 4,
        compiler_params=pltpu.CompilerParams(vmem_limit_bytes=VMEM_LIMIT),
    )(*packs, *after)


ADAM_TILE_BYTES = 2 * 1024 * 1024


def _adam_tiles(rows, cols):
    tc = cols // 2 if cols % 256 == 0 and cols >= 2048 else cols
    tr = max(t for t in range(8, rows + 1, 8) if rows % t == 0 and t * max(tc, 128) * 4 <= ADAM_TILE_BYTES) \
        if rows % 8 == 0 else rows
    return tr, tc


def _adam_math(w, g, m, v):
    nm = ADAM_B1 * m + (1.0 - ADAM_B1) * g
    nv = ADAM_B2 * v + (1.0 - ADAM_B2) * jnp.square(g)
    c1 = 1.0 - ADAM_B1 ** ADAM_STEP
    c2 = 1.0 - ADAM_B2 ** ADAM_STEP
    return -ADAM_LR * ((nm / c1) / (jnp.sqrt(nv / c2) + ADAM_EPS) + ADAM_WD * w), nm, nv


def _adamw_layer(w, g, m, v, layer, carry, name):
    _, rows, cols = w.shape
    tr, tc = _adam_tiles(rows, cols)

    def body(w_ref, g_ref, m_ref, v_ref, *rest):
        go_ref, d_ref, nm_ref, nv_ref = rest[-4:]
        gv = g_ref[...]
        go_ref[...] = gv
        d_ref[...], nm_ref[...], nv_ref[...] = _adam_math(w_ref[...], gv, m_ref[...], v_ref[...])

    blk = pl.BlockSpec((None, tr, tc), lambda i, j: (layer, i, j))
    flat = pl.BlockSpec((tr, tc), lambda i, j: (i, j))
    sh = jax.ShapeDtypeStruct(w.shape, F32)
    carry = [] if carry is None else list(carry)
    return pl.pallas_call(
        body, name=name, grid=(rows // tr, cols // tc),
        in_specs=[blk, flat, blk, blk] + [ANY] * len(carry), out_specs=[blk] * 4, out_shape=[sh] * 4,
        input_output_aliases={4 + k: k for k in range(len(carry))},
        compiler_params=_cp(("parallel", "parallel")),
    )(w, g, m, v, *carry)


def _adamw(w, g, m, v, name):
    shape = w.shape
    rows, cols = shape[-2:]
    lead = shape[:-2]
    nl = math.prod(lead)
    tr, tc = _adam_tiles(rows, cols)

    def body(w_ref, g_ref, m_ref, v_ref, d_ref, nm_ref, nv_ref):
        d_ref[...], nm_ref[...], nv_ref[...] = _adam_math(w_ref[...], g_ref[...], m_ref[...], v_ref[...])

    def index(b, i, j):
        return (*jnp.unravel_index(b, lead), i, j) if lead else (i, j)

    blk = pl.BlockSpec((*[None] * len(lead), tr, tc), index)
    sh = jax.ShapeDtypeStruct(shape, F32)
    return pl.pallas_call(
        body, name=name, grid=(nl, rows // tr, cols // tc), in_specs=[blk] * 4, out_specs=[blk] * 3,
        out_shape=[sh] * 3, compiler_params=_cp(("parallel", "parallel", "parallel")),
    )(w, g, m, v)


WEIGHTS = ("norm_w", "w_in", "ssm_a_re", "ssm_a_im", "ssm_log_dt", "ssm_b_re", "ssm_b_im", "ssm_c_re", "ssm_c_im",
           "ssm_d", "ssm_glu_w", "ssm_glu_b", "sg_ln_w", "sg_ln_b", "sg_w", "sg_b", "attn_sinks",
           "w_branch_a", "w_branch_b", "w_branch_c", "w_out", "final_norm_w")
BIG = ("w_in", "ssm_glu_w", "w_branch_a", "w_branch_b", "w_branch_c", "w_out")
BIG_KEY = {"w_in": ("win_t", True), "ssm_glu_w": ("glu", False), "w_branch_a": ("wba_t", True),
           "w_branch_b": ("wbb_t", True), "w_branch_c": ("wbc_t", True), "w_out": ("wout", False)}
VIEWS = {"w_in": (1, 2), "ssm_b_re": (2, 3), "ssm_b_im": (2, 3)}
PACKS = (
    (64, (("ssm_a_re",), ("ssm_a_im",), ("ssm_c_re",), ("ssm_c_im",), ("ssm_b_re",), ("ssm_b_im",))),
    (128, (("sg_w",),)),
    (1024, (("ssm_d", "ssm_glu_b", "sg_ln_w", "sg_ln_b"), ("norm_w", "final_norm_w", "sg_b"), ("ssm_log_dt", "attn_sinks"))),
)
PACK_ROWS = 8 * N_DEV


def _view(n, a):
    return jnp.swapaxes(a, *VIEWS[n]) if n in VIEWS else a


def _group_rows(arrs, cols):
    return -(-sum(-(-a.size // cols) for a in arrs) // 8) * 8


def _pack(groups, cols):
    parts = []
    for arrs in groups:
        if len(arrs) == 1 and arrs[0].shape[-1] == cols and arrs[0].size % (8 * cols) == 0:
            parts.append(arrs[0].reshape(-1, cols))
            continue
        flat = [jnp.pad(a.reshape(-1), (0, -a.size % cols)) for a in arrs]
        flat = jnp.concatenate(flat) if len(flat) > 1 else flat[0]
        nrow = _group_rows(arrs, cols)
        parts.append(jnp.pad(flat, (0, nrow * cols - flat.shape[0])).reshape(nrow, cols))
    pad = -sum(p.shape[0] for p in parts) % PACK_ROWS
    if pad:
        parts.append(jnp.zeros((pad, cols), F32))
    return jnp.concatenate(parts, axis=0)


def _unpack(pack, groups):
    cols = pack.shape[1]
    out, row = [], 0
    for arrs in groups:
        nrow = _group_rows(arrs, cols)
        rows = pack[row:row + nrow]
        row += nrow
        if len(arrs) == 1 and arrs[0].shape[-1] == cols and arrs[0].size == nrow * cols:
            out.append(rows.reshape(arrs[0].shape))
            continue
        flat, off = rows.reshape(-1), 0
        for a in arrs:
            out.append(flat[off:off + a.size].reshape(a.shape))
            off += -(-a.size // cols) * cols
    return out


def kernel(x, norm_w, w_in, ssm_a_re, ssm_a_im, ssm_log_dt, ssm_b_re, ssm_b_im, ssm_c_re, ssm_c_im, ssm_d, ssm_glu_w, ssm_glu_b, sg_ln_w, sg_ln_b, sg_w, sg_b, attn_sinks, w_branch_a, w_branch_b, w_branch_c, w_out, final_norm_w, loss_target, m_norm_w, m_w_in, m_ssm_a_re, m_ssm_a_im, m_ssm_log_dt, m_ssm_b_re, m_ssm_b_im, m_ssm_c_re, m_ssm_c_im, m_ssm_d, m_ssm_glu_w, m_ssm_glu_b, m_sg_ln_w, m_sg_ln_b, m_sg_w, m_sg_b, m_attn_sinks, m_w_branch_a, m_w_branch_b, m_w_branch_c, m_w_out, m_final_norm_w, v_norm_w, v_w_in, v_ssm_a_re, v_ssm_a_im, v_ssm_log_dt, v_ssm_b_re, v_ssm_b_im, v_ssm_c_re, v_ssm_c_im, v_ssm_d, v_ssm_glu_w, v_ssm_glu_b, v_sg_ln_w, v_sg_ln_b, v_sg_w, v_sg_b, v_attn_sinks, v_w_branch_a, v_w_branch_b, v_w_branch_c, v_w_out, v_final_norm_w):
    w = dict(zip(WEIGHTS, (norm_w, w_in, ssm_a_re, ssm_a_im, ssm_log_dt, ssm_b_re, ssm_b_im, ssm_c_re, ssm_c_im, ssm_d, ssm_glu_w, ssm_glu_b, sg_ln_w, sg_ln_b, sg_w, sg_b, attn_sinks, w_branch_a, w_branch_b, w_branch_c, w_out, final_norm_w)))
    m = dict(zip(WEIGHTS, (m_norm_w, m_w_in, m_ssm_a_re, m_ssm_a_im, m_ssm_log_dt, m_ssm_b_re, m_ssm_b_im, m_ssm_c_re, m_ssm_c_im, m_ssm_d, m_ssm_glu_w, m_ssm_glu_b, m_sg_ln_w, m_sg_ln_b, m_sg_w, m_sg_b, m_attn_sinks, m_w_branch_a, m_w_branch_b, m_w_branch_c, m_w_out, m_final_norm_w)))
    v = dict(zip(WEIGHTS, (v_norm_w, v_w_in, v_ssm_a_re, v_ssm_a_im, v_ssm_log_dt, v_ssm_b_re, v_ssm_b_im, v_ssm_c_re, v_ssm_c_im, v_ssm_d, v_ssm_glu_w, v_ssm_glu_b, v_sg_ln_w, v_sg_ln_b, v_sg_w, v_sg_b, v_attn_sinks, v_w_branch_a, v_w_branch_b, v_w_branch_c, v_w_out, v_final_norm_w)))

    keys = [BIG_KEY[n][0] for n in BIG]
    wv, mv, vv = ({n: _view(n, a) for n, a in d.items()} for d in (w, m, v))
    shards = [[(wv[n][l] if n in VIEWS else w[n][l].T if BIG_KEY[n][1] else w[n][l]).astype(BF16) for n in BIG]
              for l in range(DEPTH)]
    small_p = [{n: w[n][l] for n in SMALL} for l in range(DEPTH)]
    xv, tgt = x[0], loss_target[0]
    tabs = _rope_tables(xv.shape[0])

    lands = [[_allgather_place(shards[l][:1]), _allgather_place(shards[l][1:])] for l in range(DEPTH)]
    s5 = [_s5_prep(small_p[l], f"l{l}") for l in range(DEPTH)]
    wmv_packs = {cols: [_pack([[d[n] for n in names] for names in groups], cols) for d in (wv, mv, vv)]
                 for cols, groups in PACKS}
    near = _split_start([], lands[0][0], 3, _ag_copies_near, "ag_l0_win_near_start")
    got = {}
    x_, y_ = lax.axis_index("x"), lax.axis_index("y")
    n_tiles = D_IN // PROJ_TN
    far_first = (D_IN // 4 // PROJ_TN) * (2 * (1 - x_) + (1 - y_))
    n_far = -(-D_IN // 4 // PROJ_TN)
    tile_ids = jnp.arange(n_tiles, dtype=jnp.int32)
    is_far = (tile_ids >= far_first) & (tile_ids < far_first + n_far)
    near_tiles = jnp.sort(jnp.where(is_far, n_tiles, tile_ids))[:n_tiles - n_far]
    far_tiles = (far_first + jnp.arange(n_far)).astype(jnp.int32)

    def proj_of0(h):
        early = [h, *lands[0][1], *lands[1][0], *lands[1][1], *s5[0][1], *s5[1][1], near_tiles, far_tiles]
        early += [p for ps in wmv_packs.values() for p in ps]
        _, land = _split_wait(near[0], near[1], [], near[3], early, _ag_copies_near, "ag_l0_win_near_wait")
        land = _ag_forward(land, "ag_l0_win_near_forward", which=(0, 1))
        far = _split_start([], land, 1, _ag_copies_far, "ag_l0_win_far_start")
        got["ag0b"] = _allgather_start(lands[0][1], "ag_l0_rest", after=far[4])
        got["ag1a"] = _allgather_start(lands[1][0], "ag_l1_win", after=got["ag0b"][4])
        got["ag1b"] = _allgather_start(lands[1][1], "ag_l1_rest", after=got["ag1a"][4])
        proj = _in_proj_tiles(h, far[3][0], near_tiles, None, "in_proj_l0_near", after=got["ag1b"][4])
        _, land = _split_wait(far[0], far[1], [], far[3], proj, _ag_copies_far, "ag_l0_win_far_wait")
        got["win0"] = _ag_forward(land, "ag_l0_win_far_forward", which=(2,))[0]
        return _in_proj_tiles(h, got["win0"], far_tiles, proj, "in_proj_l0_far")

    def after_proj0(proj):
        got["w0"] = dict(zip(keys, [got["win0"]] + _allgather_finish(got["ag0b"], proj, "ag_l0_rest")))
        return got["w0"]

    x1, saved0 = _layer_fwd(xv, small_p[0], None, tabs, "l0", s5=s5[0], proj_of=proj_of0, after_proj=after_proj0)
    big_w0 = got["w0"]
    win1 = _allgather_finish(got["ag1a"], x1, "ag_l1_win")[0]

    def after_proj1(proj):
        got["w1"] = dict(zip(keys, [win1] + _allgather_finish(got["ag1b"], proj, "ag_l1_rest")))
        return got["w1"]

    x2, saved1 = _layer_fwd(x1, small_p[1], {"win_t": win1}, tabs, "l1", s5=s5[1], after_proj=after_proj1)
    big_w1 = got["w1"]
    loss_acc, dx2, dfw = _final(x2, w["final_norm_w"][None], tgt, "final_norm_loss")
    loss = lax.psum(loss_acc[0, 0], ("x", "y", "c"))
    dfw = dfw[0]

    dx1, big_g1, small_g1 = _layer_bwd(dx2, small_p[1], big_w1, tabs, saved1, "l1")
    rs1_cores = _reduce_scatter_cores_start([big_g1[k] for k in keys], "l1")

    def after_merge0(x):
        got["rs1"] = _reduce_scatter_cores_finish(rs1_cores, x, "l1")
        return got["rs1"][4]

    def before_win0(big):
        got["rs0b"] = _reduce_scatter_start([big[k] for k in keys[1:]], "l0_rest")
        return got["rs0b"][4]

    def after_win0(big):
        got["rs0a"] = _reduce_scatter_start([big["win_t"]], "l0_win")
        return got["rs0a"][4]

    dx, big_g0, small_g0 = _layer_bwd(dx1, small_p[0], big_w0, tabs, saved0, "l0", first_after=rs1_cores[4],
                                      after_merge=after_merge0, before_win=before_win0, after_win=after_win0)
    rs1 = got["rs1"]
    small_g = [small_g0, small_g1]
    grads, delta, new_m, new_v = {}, {}, {}, {}

    def big_adam(red, layer, carry):
        outs = {}
        for i, n in enumerate(BIG):
            g = red[i].T if BIG_KEY[n][1] and n not in VIEWS else red[i]
            outs[n] = _adamw_layer(wv[n], g, mv[n], vv[n], layer, None if carry is None else carry[n], f"adamw_{n}_l{layer}")
        return outs

    big1 = big_adam(_reduce_scatter_finish(rs1, dx, "l1"), 1, None)

    def small_grad(n):
        if n == "final_norm_w":
            return dfw
        if n in ("ssm_b_re", "ssm_b_im"):
            return jnp.stack([small_g[l][n.replace("ssm_b_", "ssm_bt_")].transpose(1, 0, 2) for l in range(DEPTH)])
        return jnp.stack([small_g[l][n] for l in range(DEPTH)])

    g_groups = [[[small_grad(n) for n in names] for names in groups] for _, groups in PACKS]
    reduced = _allreduce_small([_pack(gg, cols) for gg, (cols, _) in zip(g_groups, PACKS)], "allreduce_small",
                               after=[big1[n][1] for n in BIG])
    last = None
    for (cols, groups), gg, red in zip(PACKS, g_groups, reduced):
        names = [n for names in groups for n in names]
        grads.update(zip(names, _unpack(red, gg)))
        wp, mp, vp = wmv_packs[cols]
        outs = _adamw(wp, red, mp, vp, f"adamw_pack{cols}")
        last = outs[0]
        for res, o in zip((delta, new_m, new_v), outs):
            res.update(zip(names, _unpack(o, [[wv[n] for n in names] for names in groups])))

    red0 = (_reduce_scatter_finish(got["rs0a"], last, "l0_win")
            + _reduce_scatter_finish(got["rs0b"], last, "l0_rest"))
    for n, outs in big_adam(red0, 0, big1).items():
        grads[n], delta[n], new_m[n], new_v[n] = outs

    return (loss, dx[None], *[_view(n, d[n]) for d in (grads, delta, new_m, new_v) for n in WEIGHTS])
```

```python
import functools
import math

import jax
import jax.numpy as jnp
from jax import lax
from jax.experimental import pallas as pl
from jax.experimental.pallas import tpu as pltpu

F32 = jnp.float32
BF16 = jnp.bfloat16

D_MODEL = 2048
DEPTH = 2
EPS = 1e-6
NEG_INF = -1e30
N_DEV = 8

SSM_WIDTH = 1024
SSM_GROUP = 16
SSM_GROUPS = 64
SSM_STATE = 64
N_SLAB = 8
SLAB_CH = 128
SLAB_ST = 512
SUB = 8
N_GRP = 2
N_SEG = SUB * N_GRP

SG_HEADS = 8
CHUNK = 128
HEAD_DIM = 64
ATT_HEADS = 16
ROT_DIM = 16
ROPE_THETA = 500000.0

D_IN = 13568
OFF_UA, OFF_ZA, OFF_UB, OFF_VB, OFF_ZB, OFF_Q, OFF_KV, OFF_ZC, OFF_G = (
    0, 1024, 2048, 3072, 4096, 5120, 6144, 6400, 7424)

ADAM_LR, ADAM_B1, ADAM_B2, ADAM_EPS, ADAM_WD, ADAM_STEP = 0.001, 0.9, 0.999, 1e-08, 0.01, 10

VMEM_LIMIT = 56 * 1024 * 1024


def _cp(sem=None):
    return pltpu.CompilerParams(dimension_semantics=sem, vmem_limit_bytes=VMEM_LIMIT)


def _dot(a, b):
    return jnp.dot(a, b, preferred_element_type=F32)


def _dot_nt(a, b):
    return lax.dot_general(a, b, (((1,), (1,)), ((), ())), preferred_element_type=F32)


def _dot_tn(a, b):
    return lax.dot_general(a, b, (((0,), (0,)), ((), ())), preferred_element_type=F32)


def _mm(a, b, mode, out_dtype, tm, tn, tk, name, res=None, after=None):
    if mode == "nn":
        (m, k), (_, n) = a.shape, b.shape
    elif mode == "nt":
        (m, k), (n, _) = a.shape, b.shape
    else:
        (k, m), (_, n) = a.shape, b.shape
    tm, tn, tk = min(tm, m), min(tn, n), min(tk, k)
    assert m % tm == 0 and n % tn == 0 and k % tk == 0, (name, m, n, k, tm, tn, tk)
    nk = k // tk
    a_spec = {"nn": pl.BlockSpec((tm, tk), lambda i, j, kk: (i, kk)),
              "nt": pl.BlockSpec((tm, tk), lambda i, j, kk: (i, kk)),
              "tn": pl.BlockSpec((tk, tm), lambda i, j, kk: (kk, i))}[mode]
    b_spec = {"nn": pl.BlockSpec((tk, tn), lambda i, j, kk: (kk, j)),
              "nt": pl.BlockSpec((tn, tk), lambda i, j, kk: (j, kk)),
              "tn": pl.BlockSpec((tk, tn), lambda i, j, kk: (kk, j))}[mode]
    dot = {"nn": _dot, "nt": _dot_nt, "tn": _dot_tn}[mode]
    has_res = res is not None
    direct = out_dtype == F32 and not has_res

    def body(*refs):
        ins, outs = refs[:2 + has_res + (after is not None)], refs[2 + has_res + (after is not None):]
        a_ref, b_ref = ins[:2]
        r_ref = ins[2] if has_res else None
        o_ref = outs[0]
        acc = o_ref if direct else outs[1]
        kk = pl.program_id(2)

        @pl.when(kk == 0)
        def _():
            acc[...] = jnp.zeros_like(acc)

        acc[...] += dot(a_ref[...].astype(BF16), b_ref[...].astype(BF16))

        if not direct:
            @pl.when(kk == nk - 1)
            def _():
                r = acc[...]
                if has_res:
                    r = r + r_ref[...]
                o_ref[...] = r.astype(out_dtype)

    in_specs = [a_spec, b_spec]
    args = [a, b]
    if has_res:
        in_specs.append(pl.BlockSpec((tm, tn), lambda i, j, kk: (i, j)))
        args.append(res)
    if after is not None:
        in_specs.append(pl.BlockSpec(memory_space=pl.ANY))
        args.append(after)
    return pl.pallas_call(
        body, name=name,
        grid=(m // tm, n // tn, nk),
        in_specs=in_specs,
        out_specs=pl.BlockSpec((tm, tn), lambda i, j, kk: (i, j)),
        out_shape=jax.ShapeDtypeStruct((m, n), out_dtype),
        scratch_shapes=[] if direct else [pltpu.VMEM((tm, tn), F32)],
        compiler_params=_cp(("parallel", "parallel", "arbitrary")),
    )(*args)


PROJ_TN = 256


def _in_proj_tiles(h, win_t, tiles, carry, name, after=None):
    L, K = h.shape
    extra = [a for a in (carry, after) if a is not None]

    def body(t_ref, h_ref, w_ref, *rest):
        rest[len(extra)][...] = _dot_nt(h_ref[...], w_ref[...]).astype(BF16)

    return pl.pallas_call(
        body, name=name,
        grid_spec=pltpu.PrefetchScalarGridSpec(
            num_scalar_prefetch=1, grid=(tiles.shape[0],),
            in_specs=[pl.BlockSpec((L, K), lambda j, t: (0, 0)), pl.BlockSpec((PROJ_TN, K), lambda j, t: (t[j], 0))]
            + [pl.BlockSpec(memory_space=pl.ANY)] * len(extra),
            out_specs=pl.BlockSpec((L, PROJ_TN), lambda j, t: (0, t[j]))),
        out_shape=jax.ShapeDtypeStruct((L, win_t.shape[0]), BF16),
        input_output_aliases={} if carry is None else {3: 0},
        compiler_params=_cp(("arbitrary",)),
    )(tiles, h, win_t, *extra)


def _rms(x, w):
    return x * lax.rsqrt(jnp.mean(x * x, axis=-1, keepdims=True) + EPS) * w


def _rms_fwd(x, w, name):
    L, D = x.shape
    tm = min(L, 256)

    def body(x_ref, w_ref, h_ref):
        h_ref[...] = _rms(x_ref[...], w_ref[...]).astype(BF16)

    return pl.pallas_call(
        body, name=name, grid=(L // tm,),
        in_specs=[pl.BlockSpec((tm, D), lambda i: (i, 0)), pl.BlockSpec((1, D), lambda i: (0, 0))],
        out_specs=pl.BlockSpec((tm, D), lambda i: (i, 0)),
        out_shape=jax.ShapeDtypeStruct((L, D), BF16),
        compiler_params=_cp(("parallel",)),
    )(x, w)


def _rms_bwd(x, w, dh, dres, name):
    L, D = x.shape
    tm = min(L, 256)

    def body(x_ref, w_ref, dh_ref, dres_ref, dx_ref, dw_ref):
        _, vjp = jax.vjp(_rms, x_ref[...], w_ref[...])
        dx, dw = vjp(dh_ref[...])
        dx_ref[...] = dx + dres_ref[...]

        @pl.when(pl.program_id(0) == 0)
        def _():
            dw_ref[...] = jnp.zeros_like(dw_ref)

        dw_ref[...] += dw

    row = pl.BlockSpec((tm, D), lambda i: (i, 0))
    vec = pl.BlockSpec((1, D), lambda i: (0, 0))
    return pl.pallas_call(
        body, name=name, grid=(L // tm,),
        in_specs=[row, vec, row, row],
        out_specs=[row, vec],
        out_shape=[jax.ShapeDtypeStruct((L, D), F32), jax.ShapeDtypeStruct((1, D), F32)],
        compiler_params=_cp(("arbitrary",)),
    )(x, w, dh, dres)


def _final(x, fw, tgt, name):
    L, D = x.shape
    tm = min(L, 256)

    def loss_fn(xv, wv, tv):
        err = _rms(xv, wv) - tv
        return jnp.sum(err * err) * (0.5 / D)

    def body(x_ref, w_ref, t_ref, loss_ref, dx_ref, dw_ref):
        tv = t_ref[...]
        val, vjp = jax.vjp(lambda a, b: loss_fn(a, b, tv), x_ref[...], w_ref[...])
        dx, dw = vjp(jnp.ones((), F32))
        dx_ref[...] = dx

        @pl.when(pl.program_id(0) == 0)
        def _():
            dw_ref[...] = jnp.zeros_like(dw_ref)
            loss_ref[...] = jnp.zeros_like(loss_ref)

        dw_ref[...] += dw
        loss_ref[...] += jnp.full(loss_ref.shape, val, F32)

    row = pl.BlockSpec((tm, D), lambda i: (i, 0))
    vec = pl.BlockSpec((1, D), lambda i: (0, 0))
    return pl.pallas_call(
        body, name=name, grid=(L // tm,),
        in_specs=[row, vec, row],
        out_specs=[pl.BlockSpec((8, 128), lambda i: (0, 0)), row, vec],
        out_shape=[jax.ShapeDtypeStruct((8, 128), F32), jax.ShapeDtypeStruct((L, D), F32),
                   jax.ShapeDtypeStruct((1, D), F32)],
        compiler_params=_cp(("arbitrary",)),
    )(x, fw, tgt)


def _s5_param_fn(a_re, a_im, log_dt, bt_re, bt_im):
    dt = jnp.exp(log_dt)
    zr, zi = a_re * dt, a_im * dt
    er = jnp.exp(zr)
    lr, li = er * jnp.cos(zi), er * jnp.sin(zi)
    nr, ni = lr - 1.0, li
    den = a_re * a_re + a_im * a_im
    cr = (nr * a_re + ni * a_im) / den
    ci = (ni * a_re - nr * a_im) / den
    bbr = cr[None] * bt_re - ci[None] * bt_im
    bbi = cr[None] * bt_im + ci[None] * bt_re
    return lr, li, bbr, bbi


def _s5_params_fwd(a_re, a_im, log_dt, bt_re, bt_im, name):
    def body(ar, ai, ld, br, bi, lr, li, bbr, bbi):
        o = _s5_param_fn(ar[...], ai[...], ld[...], br[...], bi[...])
        lr[...], li[...], bbr[...], bbi[...] = o

    gp = jax.ShapeDtypeStruct(a_re.shape, F32)
    cgp = jax.ShapeDtypeStruct(bt_re.shape, F32)
    return pl.pallas_call(body, name=name, out_shape=[gp, gp, cgp, cgp])(a_re, a_im, log_dt, bt_re, bt_im)


def _s5_params_bwd(a_re, a_im, log_dt, bt_re, bt_im, dlr, dli, dbbr, dbbi, name):
    def body(ar, ai, ld, br, bi, g0, g1, g2, g3, o0, o1, o2, o3, o4):
        _, vjp = jax.vjp(_s5_param_fn, ar[...], ai[...], ld[...], br[...], bi[...])
        o0[...], o1[...], o2[...], o3[...], o4[...] = vjp((g0[...], g1[...], g2[...], g3[...]))

    gp = jax.ShapeDtypeStruct(a_re.shape, F32)
    cgp = jax.ShapeDtypeStruct(bt_re.shape, F32)
    return pl.pallas_call(body, name=name,
                          out_shape=[gp, gp, jax.ShapeDtypeStruct(log_dt.shape, F32), cgp, cgp])(
        a_re, a_im, log_dt, bt_re, bt_im, dlr, dli, dbbr, dbbi)


def _cmul(ar, ai, br, bi):
    return ar * br - ai * bi, ar * bi + ai * br


def _cpow(lr, li, n):
    rr, ri = None, None
    br, bi = lr, li
    while n:
        if n & 1:
            rr, ri = (br, bi) if rr is None else _cmul(rr, ri, br, bi)
        n >>= 1
        if n:
            br, bi = _cmul(br, bi, br, bi)
    return rr, ri


def _shift_rows(x, up):
    row = lax.broadcasted_iota(jnp.int32, x.shape, 0)
    if up:
        return jnp.where(row == SUB - 1, 0.0, pltpu.roll(x, SUB - 1, 0))
    return jnp.where(row == 0, 0.0, pltpu.roll(x, 1, 0))


NT = SLAB_ST // 128


def _lam_tiles(lr_ref, li_ref):
    return [(lr_ref[:, j * 128:(j + 1) * 128], li_ref[:, j * 128:(j + 1) * 128]) for j in range(NT)]


def _row_on_sublanes(ref, j, t):
    return ref[j, pl.ds(t, SUB, stride=0), :]


def _pow_table(pw_re, pw_im, lam_t, seg):
    assert seg % 8 == 0 and (seg // 8) & (seg // 8 - 1) == 0
    for j in range(NT):
        lr, li = lam_t[j][0][0:1], lam_t[j][1][0:1]
        r, i_ = lr, li
        for row in range(8):
            pw_re[j, row:row + 1, :] = r
            pw_im[j, row:row + 1, :] = i_
            if row < 7:
                r, i_ = _cmul(r, i_, lr, li)
        n = 8
        while n < seg:
            qr, qi = _cpow(lr, li, n)
            nr, ni = _cmul(pw_re[j, 0:n, :], pw_im[j, 0:n, :], qr, qi)
            pw_re[j, n:2 * n, :] = nr
            pw_im[j, n:2 * n, :] = ni
            n *= 2


def _seg_scan(s_re, s_im, lam_t, pw_re, pw_im, seg, reverse, prev=None):
    sgn = -1.0 if reverse else 1.0
    lt = [(lr, sgn * li) for lr, li in lam_t]
    tiles = [(g, j) for g in range(N_GRP) for j in range(NT)]
    zeros = jnp.zeros((SUB, 128), F32)

    def rows(g, i):
        return pl.ds(pl.multiple_of((g * seg + i) * SUB, SUB), SUB)

    def step1(t, carry):
        i = seg - 1 - t if reverse else t
        out = []
        for n, (g, j) in enumerate(tiles):
            nr, ni = _cmul(lt[j][0], lt[j][1], carry[2 * n], carry[2 * n + 1])
            nr = nr + s_re[j, rows(g, i), :]
            ni = ni + s_im[j, rows(g, i), :]
            s_re[j, rows(g, i), :] = nr
            s_im[j, rows(g, i), :] = ni
            out += [nr, ni]
        return tuple(out)

    zero = tuple(zeros for _ in range(2 * len(tiles)))
    ends = lax.fori_loop(0, seg, step1, zero)

    carries = [None] * (2 * len(tiles))
    row = lax.broadcasted_iota(jnp.int32, (SUB, 128), 0)
    dist = (SUB - 1 - row) if reverse else row
    edge = 0 if reverse else SUB - 1
    for j in range(NT):
        pr, pi = _cpow(lt[j][0], lt[j][1], seg)
        qr, qi = jnp.ones((SUB, 128), F32), zeros
        for s in range(1, SUB):
            tr, ti = _cmul(qr, qi, pr, pi)
            qr, qi = jnp.where(dist >= s, tr, qr), jnp.where(dist >= s, ti, qi)
        boundary = None
        for g in (reversed(range(N_GRP)) if reverse else range(N_GRP)):
            n = g * NT + j
            cr, ci = zeros, zeros
            for _ in range(SUB - 1):
                tr, ti = _cmul(pr, pi, cr, ci)
                cr = _shift_rows(tr + ends[2 * n], reverse)
                ci = _shift_rows(ti + ends[2 * n + 1], reverse)
            if boundary is not None:
                tr, ti = _cmul(qr, qi, boundary[0], boundary[1])
                cr, ci = cr + tr, ci + ti
            carries[2 * n], carries[2 * n + 1] = cr, ci
            fr, fi = _cmul(pr, pi, cr, ci)
            boundary = (jnp.broadcast_to((fr + ends[2 * n])[edge:edge + 1], (SUB, 128)),
                        jnp.broadcast_to((fi + ends[2 * n + 1])[edge:edge + 1], (SUB, 128)))

    def fix(t, i, acc, before):
        out = []
        pws = [(_row_on_sublanes(pw_re, j, t), sgn * _row_on_sublanes(pw_im, j, t)) for j in range(NT)]
        for n, (g, j) in enumerate(tiles):
            ar, ai = _cmul(pws[j][0], pws[j][1], carries[2 * n], carries[2 * n + 1])
            ar = ar + s_re[j, rows(g, i), :]
            ai = ai + s_im[j, rows(g, i), :]
            s_re[j, rows(g, i), :] = ar
            s_im[j, rows(g, i), :] = ai
            if before is not None:
                qr, qi = before(n)
                out += [acc[2 * n] + ar * qr + ai * qi, acc[2 * n + 1] + ai * qr - ar * qi]
        return tuple(out)

    if prev is None:
        lax.fori_loop(0, seg, lambda t, c: fix(t, seg - 1 - t if reverse else t, c, None), ())
        return carries
    assert reverse
    p_re, p_im, p_carries = prev

    def earlier(t):
        return lambda n: (p_re[tiles[n][1], rows(tiles[n][0], seg - 2 - t), :],
                          p_im[tiles[n][1], rows(tiles[n][0], seg - 2 - t), :])

    acc = lax.fori_loop(0, seg - 1, lambda t, c: fix(t, seg - 1 - t, c, earlier(t)), zero)
    acc = fix(seg - 1, 0, acc, lambda n: (p_carries[2 * n], p_carries[2 * n + 1]))
    return carries, [sum(acc[2 * (g * NT + j) + part] for g in range(N_GRP)) for j in range(NT) for part in range(2)]


def _seg_slice(k, seg):
    g, r = divmod(k, SUB)
    return pl.ds(g * seg * SUB + r, seg, stride=SUB)


def _seg_rows(ref, k, seg):
    return jnp.concatenate([ref[j, _seg_slice(k, seg), :] for j in range(NT)], axis=-1)


def _seg_store(ref, k, seg, val):
    for j in range(NT):
        ref[j, _seg_slice(k, seg), :] = val[:, j * 128:(j + 1) * 128]


def _s5_specs(L):
    col = lambda off: pl.BlockSpec((L, SLAB_CH), lambda j: (0, off + j))
    mat_b = pl.BlockSpec((None, SLAB_CH, SLAB_ST), lambda j: (j, 0, 0))
    mat_c = pl.BlockSpec((None, SLAB_ST, SLAB_CH), lambda j: (j, 0, 0))
    vec_s = pl.BlockSpec((None, SUB, SLAB_ST), lambda j: (j, 0, 0))
    vec_c = pl.BlockSpec((None, 1, SLAB_CH), lambda j: (j, 0, 0))
    return col, mat_b, mat_c, vec_s, vec_c


def _s5_states(u_ref, bre_ref, bim_ref, lam_t, pw_re, pw_im, s_re, s_im, seg):
    _pow_table(pw_re, pw_im, lam_t, seg)
    for k in range(N_SEG):
        uk = u_ref[pl.ds(k * seg, seg), :]
        _seg_store(s_re, k, seg, _dot(uk, bre_ref[...]))
        _seg_store(s_im, k, seg, _dot(uk, bim_ref[...]))
    return _seg_scan(s_re, s_im, lam_t, pw_re, pw_im, seg, reverse=False)


def _s5_fwd(proj, bre, bim, cre_t, cim_t, lam_re, lam_im, dvec, name):
    L = proj.shape[0]
    seg = L // N_SEG
    col, mat_b, mat_c, vec_s, vec_c = _s5_specs(L)
    rows = N_SEG * seg

    def body(u_ref, bre_ref, bim_ref, cre_ref, cim_ref, lr_ref, li_ref, d_ref, y_ref, s_re, s_im, pw_re, pw_im):
        _s5_states(u_ref, bre_ref, bim_ref, _lam_tiles(lr_ref, li_ref), pw_re, pw_im, s_re, s_im, seg)
        for k in range(N_SEG):
            y = (_dot(_seg_rows(s_re, k, seg).astype(BF16), cre_ref[...])
                 - _dot(_seg_rows(s_im, k, seg).astype(BF16), cim_ref[...]))
            y = y + d_ref[...] * u_ref[pl.ds(k * seg, seg), :].astype(F32)
            y_ref[pl.ds(k * seg, seg), :] = jax.nn.gelu(y).astype(BF16)

    return pl.pallas_call(
        body, name=name, grid=(N_SLAB,),
        in_specs=[col(OFF_UA // SLAB_CH), mat_b, mat_b, mat_c, mat_c, vec_s, vec_s, vec_c],
        out_specs=pl.BlockSpec((L, SLAB_CH), lambda j: (0, j)),
        out_shape=jax.ShapeDtypeStruct((L, SSM_WIDTH), BF16),
        scratch_shapes=[pltpu.VMEM((NT, rows, 128), F32)] * 2 + [pltpu.VMEM((NT, seg, 128), F32)] * 2,
        compiler_params=_cp(("parallel",)),
    )(proj, bre, bim, cre_t, cim_t, lam_re, lam_im, dvec)


def _s5_bwd(proj, dy, bre, bim, cre_t, cim_t, lam_re, lam_im, dvec, name):
    L = proj.shape[0]
    seg = L // N_SEG
    col, mat_b, mat_c, vec_s, vec_c = _s5_specs(L)
    rows = N_SEG * seg
    dlam_spec = pl.BlockSpec((None, 1, SLAB_ST), lambda j: (j, 0, 0))

    def body(u_ref, dy_ref, bre_ref, bim_ref, cre_ref, cim_ref, lr_ref, li_ref, d_ref,
             du_ref, dbre_ref, dbim_ref, dcre_ref, dcim_ref, dlr_ref, dli_ref, dd_ref,
             s_re, s_im, a_re, a_im, pw_re, pw_im, dyp):
        lam_t = _lam_tiles(lr_ref, li_ref)
        carry_s = _s5_states(u_ref, bre_ref, bim_ref, lam_t, pw_re, pw_im, s_re, s_im, seg)
        dcre = jnp.zeros((SLAB_ST, SLAB_CH), F32)
        dcim = jnp.zeros((SLAB_ST, SLAB_CH), F32)
        dd = jnp.zeros((1, SLAB_CH), F32)
        for k in range(N_SEG):
            sre = _seg_rows(s_re, k, seg).astype(BF16)
            sim = _seg_rows(s_im, k, seg).astype(BF16)
            uk = u_ref[pl.ds(k * seg, seg), :].astype(F32)
            ypre = _dot(sre, cre_ref[...]) - _dot(sim, cim_ref[...]) + d_ref[...] * uk
            _, vjp = jax.vjp(jax.nn.gelu, ypre)
            (dyk,) = vjp(dy_ref[pl.ds(k * seg, seg), :].astype(F32))
            dyp[pl.ds(k * seg, seg), :] = dyk
            dd = dd + jnp.sum(dyk * uk, axis=0, keepdims=True)
            dyb = dyk.astype(BF16)
            dcre = dcre + _dot_tn(sre, dyb)
            dcim = dcim - _dot_tn(sim, dyb)
            _seg_store(a_re, k, seg, _dot_nt(dyb, cre_ref[...]))
            _seg_store(a_im, k, seg, -_dot_nt(dyb, cim_ref[...]))
        dcre_ref[...] = dcre
        dcim_ref[...] = dcim
        dd_ref[...] = dd

        _, acc = _seg_scan(a_re, a_im, lam_t, pw_re, pw_im, seg, reverse=True, prev=(s_re, s_im, carry_s))
        dlr_ref[...] = jnp.concatenate([jnp.sum(acc[2 * j], axis=0, keepdims=True) for j in range(NT)], axis=-1)
        dli_ref[...] = jnp.concatenate([jnp.sum(acc[2 * j + 1], axis=0, keepdims=True) for j in range(NT)], axis=-1)

        dbre = jnp.zeros((SLAB_CH, SLAB_ST), F32)
        dbim = jnp.zeros((SLAB_CH, SLAB_ST), F32)
        for k in range(N_SEG):
            are = _seg_rows(a_re, k, seg).astype(BF16)
            aim = _seg_rows(a_im, k, seg).astype(BF16)
            uk = u_ref[pl.ds(k * seg, seg), :]
            du = _dot_nt(are, bre_ref[...]) + _dot_nt(aim, bim_ref[...]) + dyp[pl.ds(k * seg, seg), :] * d_ref[...]
            du_ref[pl.ds(k * seg, seg), :] = du.astype(BF16)
            dbre = dbre + _dot_tn(uk, are)
            dbim = dbim + _dot_tn(uk, aim)
        dbre_ref[...] = dbre
        dbim_ref[...] = dbim

    scan_buf = pltpu.VMEM((NT, rows, 128), F32)
    pow_buf = pltpu.VMEM((NT, seg, 128), F32)
    return pl.pallas_call(
        body, name=name, grid=(N_SLAB,),
        in_specs=[col(OFF_UA // SLAB_CH), pl.BlockSpec((L, SLAB_CH), lambda j: (0, j)),
                  mat_b, mat_b, mat_c, mat_c, vec_s, vec_s, vec_c],
        out_specs=[pl.BlockSpec((L, SLAB_CH), lambda j: (0, j)), mat_b, mat_b, mat_c, mat_c, dlam_spec, dlam_spec, vec_c],
        out_shape=[jax.ShapeDtypeStruct((L, SSM_WIDTH), BF16),
                   jax.ShapeDtypeStruct((N_SLAB, SLAB_CH, SLAB_ST), F32),
                   jax.ShapeDtypeStruct((N_SLAB, SLAB_CH, SLAB_ST), F32),
                   jax.ShapeDtypeStruct((N_SLAB, SLAB_ST, SLAB_CH), F32),
                   jax.ShapeDtypeStruct((N_SLAB, SLAB_ST, SLAB_CH), F32),
                   jax.ShapeDtypeStruct((N_SLAB, 1, SLAB_ST), F32),
                   jax.ShapeDtypeStruct((N_SLAB, 1, SLAB_ST), F32),
                   jax.ShapeDtypeStruct((N_SLAB, 1, SLAB_CH), F32)],
        scratch_shapes=[scan_buf, scan_buf, scan_buf, scan_buf, pow_buf, pow_buf, pltpu.VMEM((L, SLAB_CH), F32)],
        compiler_params=_cp(("parallel",)),
    )(proj, dy, bre, bim, cre_t, cim_t, lam_re, lam_im, dvec)


def _glu_point(y0, pre, za, b):
    return y0 * jax.nn.sigmoid(pre + b) * jax.nn.silu(za)


def _glu_specs(L, tm):
    row = pl.BlockSpec((tm, SSM_WIDTH), lambda i: (i, 0))
    za = pl.BlockSpec((tm, SSM_WIDTH), lambda i: (i, OFF_ZA // SSM_WIDTH))
    wmat = pl.BlockSpec((SSM_WIDTH, SSM_WIDTH), lambda i: (0, 0))
    vec = pl.BlockSpec((1, SSM_WIDTH), lambda i: (0, 0))
    return row, za, wmat, vec


def _glu_fwd(ya0, proj, w, b, name):
    L = ya0.shape[0]
    tm = min(L, 512)
    row, za, wmat, vec = _glu_specs(L, tm)

    def body(y_ref, z_ref, w_ref, b_ref, o_ref):
        y0 = y_ref[...]
        pre = _dot(y0, w_ref[...])
        o_ref[...] = _glu_point(y0.astype(F32), pre, z_ref[...].astype(F32), b_ref[...]).astype(BF16)

    return pl.pallas_call(
        body, name=name, grid=(L // tm,), in_specs=[row, za, wmat, vec], out_specs=row,
        out_shape=jax.ShapeDtypeStruct((L, SSM_WIDTH), BF16), compiler_params=_cp(("parallel",)),
    )(ya0, proj, w, b)


def _glu_bwd(ya0, proj, w, b, dya, name):
    L = ya0.shape[0]
    tm = min(L, 512)
    row, za, wmat, vec = _glu_specs(L, tm)

    def body(y_ref, z_ref, w_ref, b_ref, g_ref, dy0_ref, dza_ref, dw_ref, db_ref):
        y0 = y_ref[...]
        pre = _dot(y0, w_ref[...])
        _, vjp = jax.vjp(_glu_point, y0.astype(F32), pre, z_ref[...].astype(F32), b_ref[...])
        dy0, dpre, dza, db = vjp(g_ref[...].astype(F32))
        dpb = dpre.astype(BF16)
        dy0_ref[...] = (dy0 + _dot_nt(dpb, w_ref[...])).astype(BF16)
        dza_ref[...] = dza.astype(BF16)

        @pl.when(pl.program_id(0) == 0)
        def _():
            dw_ref[...] = jnp.zeros_like(dw_ref)
            db_ref[...] = jnp.zeros_like(db_ref)

        dw_ref[...] += _dot_tn(y0, dpb)
        db_ref[...] += db

    return pl.pallas_call(
        body, name=name, grid=(L // tm,), in_specs=[row, za, wmat, vec, row],
        out_specs=[row, row, wmat, vec],
        out_shape=[jax.ShapeDtypeStruct((L, SSM_WIDTH), BF16), jax.ShapeDtypeStruct((L, SSM_WIDTH), BF16),
                   jax.ShapeDtypeStruct((SSM_WIDTH, SSM_WIDTH), F32), jax.ShapeDtypeStruct((1, SSM_WIDTH), F32)],
        compiler_params=_cp(("arbitrary",)),
    )(ya0, proj, w, b, dya)


def _sg_norm(vb, ln_w, ln_b):
    v0 = jax.nn.gelu(vb)
    mu = jnp.mean(v0, axis=-1, keepdims=True)
    var = jnp.mean(jnp.square(v0 - mu), axis=-1, keepdims=True)
    return (v0 - mu) * lax.rsqrt(var + EPS) * ln_w + ln_b


def _sg_gate(ub, mixed, zb):
    return jax.nn.gelu(ub) * mixed * jax.nn.silu(zb)


def _sg_specs():
    W = SSM_WIDTH
    blk = lambda off: pl.BlockSpec((CHUNK, W), lambda n: (n, off // W))
    out = pl.BlockSpec((CHUNK, W), lambda n: (n, 0))
    vec = pl.BlockSpec((1, W), lambda n: (0, 0))
    wsp = pl.BlockSpec((SG_HEADS, CHUNK, CHUNK), lambda n: (0, 0, 0))
    bsp = pl.BlockSpec((SG_HEADS, CHUNK, 1), lambda n: (0, 0, 0))
    return blk, out, vec, wsp, bsp


def _sg_masked(w_ref):
    t = lax.broadcasted_iota(jnp.int32, (CHUNK, CHUNK), 0)
    s = lax.broadcasted_iota(jnp.int32, (CHUNK, CHUNK), 1)
    causal = s <= t
    return causal, [jnp.where(causal, w_ref[h], 0.0).astype(BF16) for h in range(SG_HEADS)]


def _sg_mix(wm, vnb, bias_ref):
    return jnp.concatenate(
        [_dot(wm[h], vnb[:, h * CHUNK:(h + 1) * CHUNK]) + bias_ref[h] for h in range(SG_HEADS)], axis=-1)


def _sg_fwd(proj, ln_w, ln_b, w, bias, name):
    L = proj.shape[0]
    blk, out, vec, wsp, bsp = _sg_specs()

    def body(ub_ref, vb_ref, zb_ref, lw_ref, lb_ref, w_ref, bias_ref, o_ref):
        _, wm = _sg_masked(w_ref)
        vnb = _sg_norm(vb_ref[...].astype(F32), lw_ref[...], lb_ref[...]).astype(BF16)
        mixed = _sg_mix(wm, vnb, bias_ref)
        o_ref[...] = _sg_gate(ub_ref[...].astype(F32), mixed, zb_ref[...].astype(F32)).astype(BF16)

    return pl.pallas_call(
        body, name=name, grid=(L // CHUNK,),
        in_specs=[blk(OFF_UB), blk(OFF_VB), blk(OFF_ZB), vec, vec, wsp, bsp], out_specs=out,
        out_shape=jax.ShapeDtypeStruct((L, SSM_WIDTH), BF16), compiler_params=_cp(("parallel",)),
    )(proj, proj, proj, ln_w, ln_b, w, bias)


def _sg_bwd(proj, ln_w, ln_b, w, bias, dyb, name):
    L = proj.shape[0]
    blk, out, vec, wsp, bsp = _sg_specs()

    def body(ub_ref, vb_ref, zb_ref, lw_ref, lb_ref, w_ref, bias_ref, g_ref,
             dub_ref, dvb_ref, dzb_ref, dlw_ref, dlb_ref, dw_ref, dbias_ref):
        causal, wm = _sg_masked(w_ref)
        vb = vb_ref[...].astype(F32)
        vn, vjp_norm = jax.vjp(_sg_norm, vb, lw_ref[...], lb_ref[...])
        vnb = vn.astype(BF16)
        mixed = _sg_mix(wm, vnb, bias_ref)
        _, vjp_gate = jax.vjp(_sg_gate, ub_ref[...].astype(F32), mixed, zb_ref[...].astype(F32))
        dub, dmixed, dzb = vjp_gate(g_ref[...].astype(F32))
        dub_ref[...] = dub.astype(BF16)
        dzb_ref[...] = dzb.astype(BF16)

        @pl.when(pl.program_id(0) == 0)
        def _():
            dlw_ref[...] = jnp.zeros_like(dlw_ref)
            dlb_ref[...] = jnp.zeros_like(dlb_ref)
            dw_ref[...] = jnp.zeros_like(dw_ref)
            dbias_ref[...] = jnp.zeros_like(dbias_ref)

        dvn = []
        for h in range(SG_HEADS):
            dm = dmixed[:, h * CHUNK:(h + 1) * CHUNK]
            dmb = dm.astype(BF16)
            dbias_ref[h] += jnp.sum(dm, axis=-1, keepdims=True)
            dw_ref[h] += jnp.where(causal, _dot_nt(dmb, vnb[:, h * CHUNK:(h + 1) * CHUNK]), 0.0)
            dvn.append(_dot_tn(wm[h], dmb))
        dvb, dlw, dlb = vjp_norm(jnp.concatenate(dvn, axis=-1))
        dvb_ref[...] = dvb.astype(BF16)
        dlw_ref[...] += dlw
        dlb_ref[...] += dlb

    act = jax.ShapeDtypeStruct((L, SSM_WIDTH), BF16)
    return pl.pallas_call(
        body, name=name, grid=(L // CHUNK,),
        in_specs=[blk(OFF_UB), blk(OFF_VB), blk(OFF_ZB), vec, vec, wsp, bsp, out],
        out_specs=[out, out, out, vec, vec, wsp, bsp],
        out_shape=[act, act, act, jax.ShapeDtypeStruct((1, SSM_WIDTH), F32), jax.ShapeDtypeStruct((1, SSM_WIDTH), F32),
                   jax.ShapeDtypeStruct((SG_HEADS, CHUNK, CHUNK), F32), jax.ShapeDtypeStruct((SG_HEADS, CHUNK, 1), F32)],
        compiler_params=_cp(("arbitrary",)),
    )(proj, proj, proj, ln_w, ln_b, w, bias, dyb)


def _rope_tables(L):
    half = ROT_DIM // 2
    inv_freq = ROPE_THETA ** (-jnp.arange(0, ROT_DIM, 2, dtype=F32) / ROT_DIM)
    ang = jnp.arange(L, dtype=F32)[:, None] * inv_freq[None, :]
    cos, sin = jnp.cos(ang), jnp.sin(ang)
    ones = jnp.ones((L, HEAD_DIM - ROT_DIM), F32)
    cos_h = jnp.concatenate([cos, cos, ones], axis=-1)
    sin_h = jnp.concatenate([-sin, sin, 0.0 * ones], axis=-1)
    src = jnp.arange(HEAD_DIM)[:, None]
    dst = jnp.arange(HEAD_DIM)[None, :]
    p_h = (((dst < half) & (src == dst + half)) | ((dst >= half) & (dst < ROT_DIM) & (src == dst - half))).astype(F32)
    p2 = jnp.kron(jnp.eye(2, dtype=F32), p_h).astype(BF16)
    return jnp.tile(cos_h, (1, 2)), jnp.tile(sin_h, (1, 2)), p2


def _rope(t, cos, sin, p2):
    n = t.shape[1] // 128
    tb = t.astype(BF16)
    sw = jnp.concatenate([_dot(tb[:, i * 128:(i + 1) * 128], p2) for i in range(n)], axis=-1) if n > 1 else _dot(tb, p2)
    return t * jnp.tile(cos, (1, n)) + sw * jnp.tile(sin, (1, n))


def _rope_t(g, cos, sin, p2):
    n = g.shape[1] // 128
    gs = (g * jnp.tile(sin, (1, n))).astype(BF16)
    sw = jnp.concatenate([_dot_nt(gs[:, i * 128:(i + 1) * 128], p2) for i in range(n)], axis=-1) if n > 1 else _dot_nt(gs, p2)
    return g * jnp.tile(cos, (1, n)) + sw


def _lane_lo(shape):
    return (lax.broadcasted_iota(jnp.int32, shape, len(shape) - 1) % 128) < HEAD_DIM


def _dup_halves(x):
    xr = pltpu.roll(x, HEAD_DIM, 1)
    lo = _lane_lo(x.shape)
    return jnp.where(lo, x, xr), jnp.where(lo, xr, x)


def _fold_halves(d0, d1):
    f0 = d0 + pltpu.roll(d0, HEAD_DIM, 1)
    f1 = d1 + pltpu.roll(d1, HEAD_DIM, 1)
    return jnp.where(_lane_lo(d0.shape), f0, f1)


def _attn_mask():
    qi = lax.broadcasted_iota(jnp.int32, (CHUNK, 2 * CHUNK), 0)
    kj = lax.broadcasted_iota(jnp.int32, (CHUNK, 2 * CHUNK), 1)
    return qi, kj


def _attn_specs():
    qsp = pl.BlockSpec((CHUNK, 1024), lambda n: (n, OFF_Q // 1024))
    kv_cur = pl.BlockSpec((CHUNK, 256), lambda n: (n, OFF_KV // 256))
    kv_prev = pl.BlockSpec((CHUNK, 256), lambda n: (jnp.maximum(n - 1, 0), OFF_KV // 256))
    zsp = [pl.BlockSpec((CHUNK, 256), functools.partial(lambda n, q: (n, OFF_ZC // 256 + q), q=q)) for q in range(4)]
    tab_cur = pl.BlockSpec((CHUNK, 128), lambda n: (n, 0))
    tab_prev = pl.BlockSpec((CHUNK, 128), lambda n: (jnp.maximum(n - 1, 0), 0))
    p2sp = pl.BlockSpec((128, 128), lambda n: (0, 0))
    sink = pl.BlockSpec(memory_space=pltpu.SMEM)
    wide = pl.BlockSpec((CHUNK, 1024), lambda n: (n, 0))
    return qsp, kv_cur, kv_prev, zsp, tab_cur, tab_prev, p2sp, sink, wide


def _attn_prep(n, q_ref, kvc_ref, kvp_ref, cosc_ref, sinc_ref, cosp_ref, sinp_ref, p2_ref):
    p2 = p2_ref[...]
    qr = _rope(q_ref[...].astype(F32), cosc_ref[...], sinc_ref[...], p2).astype(BF16)
    kc = _rope(kvc_ref[:, 0:128].astype(F32), cosc_ref[...], sinc_ref[...], p2)
    kp = _rope(kvp_ref[:, 0:128].astype(F32), cosp_ref[...], sinp_ref[...], p2)
    k_all = jnp.concatenate([kp, kc], axis=0).astype(BF16)
    v_all = jnp.concatenate([kvp_ref[:, 128:256], kvc_ref[:, 128:256]], axis=0)
    qi, kj = _attn_mask()
    allowed = ((kj < CHUNK) & (kj > qi) & (n > 0)) | ((kj >= CHUNK) & (kj - CHUNK <= qi))
    return qr, _dup_halves(k_all), _dup_halves(v_all), allowed, _lane_lo((CHUNK, 128))


def _attn_head(qr, kd, sink_ref, h, allowed, lo):
    m, half, g = h // 2, h % 2, h // 8
    qp = qr[:, m * 128:(m + 1) * 128]
    qm = jnp.where(lo if half == 0 else ~lo, qp, jnp.zeros_like(qp))
    s = jnp.where(allowed, _dot_nt(qm, kd[g]) * (HEAD_DIM ** -0.5), NEG_INF)
    snk = sink_ref[h]
    mx = jnp.maximum(jnp.max(s, axis=-1, keepdims=True), snk)
    e = jnp.exp(s - mx)
    es = jnp.exp(snk - mx)
    inv = 1.0 / (jnp.sum(e, axis=-1, keepdims=True) + es)
    return qm, e * inv, es * inv


def _silu_gate(o, z):
    return o * jax.nn.silu(z)


def _pair_lanes(refs, m):
    return refs[m // 2][:, (m % 2) * 128:(m % 2 + 1) * 128]


def _attn_fwd(proj, sinks, tabs, name):
    L = proj.shape[0]
    cos2, sin2, p2 = tabs
    qsp, kv_cur, kv_prev, zsp, tab_cur, tab_prev, p2sp, sink, wide = _attn_specs()

    def body(q_ref, kvc_ref, kvp_ref, z0, z1, z2, z3, cosc, sinc, cosp, sinp, p2_ref, sink_ref, y_ref, o_ref):
        n = pl.program_id(0)
        qr, kd, vd, allowed, lo = _attn_prep(n, q_ref, kvc_ref, kvp_ref, cosc, sinc, cosp, sinp, p2_ref)
        probs = [_attn_head(qr, kd, sink_ref, h, allowed, lo)[1].astype(BF16) for h in range(ATT_HEADS)]
        for m in range(ATT_HEADS // 2):
            g = m // 4
            o0 = _dot(probs[2 * m], vd[g])
            o1 = _dot(probs[2 * m + 1], vd[g])
            o = jnp.where(lo, o0, o1).astype(BF16)
            o_ref[:, m * 128:(m + 1) * 128] = o
            z = _pair_lanes((z0, z1, z2, z3), m).astype(F32)
            y_ref[:, m * 128:(m + 1) * 128] = _silu_gate(o.astype(F32), z).astype(BF16)

    act = jax.ShapeDtypeStruct((L, 1024), BF16)
    return pl.pallas_call(
        body, name=name, grid=(L // CHUNK,),
        in_specs=[qsp, kv_cur, kv_prev, *zsp, tab_cur, tab_cur, tab_prev, tab_prev, p2sp, sink],
        out_specs=[wide, wide], out_shape=[act, act], compiler_params=_cp(("parallel",)),
    )(proj, proj, proj, proj, proj, proj, proj, cos2, sin2, cos2, sin2, p2, sinks)


def _attn_bwd(proj, sinks, tabs, o_att, dyc, name):
    L = proj.shape[0]
    cos2, sin2, p2 = tabs
    qsp, kv_cur, kv_prev, zsp, tab_cur, tab_prev, p2sp, sink, wide = _attn_specs()
    kvo = pl.BlockSpec((CHUNK, 256), lambda n: (n, 0))

    def body(q_ref, kvc_ref, kvp_ref, z0, z1, z2, z3, cosc, sinc, cosp, sinp, p2_ref, sink_ref, o_ref, g_ref,
             dq_ref, dz_ref, dkvc_ref, dkvp_ref, dsink_ref):
        n = pl.program_id(0)
        qr, kd, vd, allowed, lo = _attn_prep(n, q_ref, kvc_ref, kvp_ref, cosc, sinc, cosp, sinp, p2_ref)
        p2 = p2_ref[...]

        @pl.when(n == 0)
        def _():
            dsink_ref[...] = jnp.zeros_like(dsink_ref)

        dkd = [jnp.zeros((2 * CHUNK, 128), F32), jnp.zeros((2 * CHUNK, 128), F32)]
        dvd = [jnp.zeros((2 * CHUNK, 128), F32), jnp.zeros((2 * CHUNK, 128), F32)]
        probs = [_attn_head(qr, kd, sink_ref, h, allowed, lo) for h in range(ATT_HEADS)]
        for m in range(ATT_HEADS // 2):
            g = m // 4
            lanes = slice(m * 128, (m + 1) * 128)
            z = _pair_lanes((z0, z1, z2, z3), m).astype(F32)
            _, vjp = jax.vjp(_silu_gate, o_ref[:, lanes].astype(F32), z)
            do, dz = vjp(g_ref[:, lanes].astype(F32))
            dz_ref[:, lanes] = dz.astype(BF16)
            dop = do.astype(BF16)
            dq_h = []
            for half in range(2):
                h = 2 * m + half
                qm, p, ps = probs[h]
                dom = jnp.where(lo if half == 0 else ~lo, dop, jnp.zeros_like(dop))
                dp = _dot_nt(dom, vd[g])
                rs = jnp.sum(p * dp, axis=-1, keepdims=True)
                ds = (p * (dp - rs) * (HEAD_DIM ** -0.5)).astype(BF16)
                dsink_ref[h:h + 1, :] += jnp.broadcast_to(jnp.sum(-ps * rs, axis=0, keepdims=True), (1, 128))
                dq_h.append(_dot(ds, kd[g]))
                dkd[g] = dkd[g] + _dot_tn(ds, qm)
                dvd[g] = dvd[g] + _dot_tn(p.astype(BF16), dom)
            dq_ref[:, lanes] = _rope_t(jnp.where(lo, dq_h[0], dq_h[1]), cosc[...], sinc[...], p2).astype(BF16)
        dk_rot = _fold_halves(dkd[0], dkd[1])
        dv = _fold_halves(dvd[0], dvd[1])
        dkp = _rope_t(dk_rot[0:CHUNK], cosp[...], sinp[...], p2)
        dkc = _rope_t(dk_rot[CHUNK:2 * CHUNK], cosc[...], sinc[...], p2)
        dkvp_ref[...] = jnp.concatenate([dkp, dv[0:CHUNK]], axis=-1)
        dkvc_ref[...] = jnp.concatenate([dkc, dv[CHUNK:2 * CHUNK]], axis=-1)

    act = jax.ShapeDtypeStruct((L, 1024), BF16)
    kvs = jax.ShapeDtypeStruct((L, 256), F32)
    return pl.pallas_call(
        body, name=name, grid=(L // CHUNK,),
        in_specs=[qsp, kv_cur, kv_prev, *zsp, tab_cur, tab_cur, tab_prev, tab_prev, p2sp, sink, wide, wide],
        out_specs=[wide, wide, kvo, kvo, pl.BlockSpec((ATT_HEADS, 128), lambda n: (0, 0))],
        out_shape=[act, act, kvs, kvs, jax.ShapeDtypeStruct((ATT_HEADS, 128), F32)],
        compiler_params=_cp(("arbitrary",)),
    )(proj, proj, proj, proj, proj, proj, proj, cos2, sin2, cos2, sin2, p2, sinks, o_att, dyc)


MERGE_TN = 256


def _merge_point(ta, tb, tc, ga, gb, gc):
    return jax.nn.sigmoid(ga) * ta + jax.nn.sigmoid(gb) * tb + jax.nn.sigmoid(gc) * tc


def _merge_specs(tm):
    nj = D_MODEL // MERGE_TN
    t = pl.BlockSpec((tm, MERGE_TN), lambda i, j: (i, j))
    gates = [pl.BlockSpec((tm, MERGE_TN), functools.partial(lambda i, j, b: (i, OFF_G // MERGE_TN + b * nj + j), b=b))
             for b in range(3)]
    return t, gates, nj


def _merge_fwd(ta, tb, tc, proj, name):
    L = ta.shape[0]
    tm = min(L, 1024)
    t, gates, nj = _merge_specs(tm)

    def body(ta_ref, tb_ref, tc_ref, ga_ref, gb_ref, gc_ref, o_ref):
        f = lambda r: r[...].astype(F32)
        o_ref[...] = _merge_point(f(ta_ref), f(tb_ref), f(tc_ref), f(ga_ref), f(gb_ref), f(gc_ref)).astype(BF16)

    return pl.pallas_call(
        body, name=name, grid=(L // tm, nj), in_specs=[t, t, t, *gates], out_specs=t,
        out_shape=jax.ShapeDtypeStruct((L, D_MODEL), BF16), compiler_params=_cp(("parallel", "parallel")),
    )(ta, tb, tc, proj, proj, proj)


def _merge_bwd(ta, tb, tc, proj, dm, name):
    L = ta.shape[0]
    tm = min(L, 1024)
    t, gates, nj = _merge_specs(tm)

    def body(ta_ref, tb_ref, tc_ref, ga_ref, gb_ref, gc_ref, dm_ref, dta_ref, dtb_ref, dtc_ref, dga_ref, dgb_ref, dgc_ref):
        f = lambda r: r[...].astype(F32)
        _, vjp = jax.vjp(_merge_point, f(ta_ref), f(tb_ref), f(tc_ref), f(ga_ref), f(gb_ref), f(gc_ref))
        outs = vjp(f(dm_ref))
        for r, v in zip((dta_ref, dtb_ref, dtc_ref, dga_ref, dgb_ref, dgc_ref), outs):
            r[...] = v.astype(BF16)

    act = jax.ShapeDtypeStruct((L, D_MODEL), BF16)
    return pl.pallas_call(
        body, name=name, grid=(L // tm, nj), in_specs=[t, t, t, *gates, t],
        out_specs=[t] * 6, out_shape=[act] * 6,
        compiler_params=_cp(("parallel", "parallel")),
    )(ta, tb, tc, proj, proj, proj, dm)


GRAD_DT = BF16
SMALL = ("norm_w", "ssm_a_re", "ssm_a_im", "ssm_log_dt", "ssm_b_re", "ssm_b_im", "ssm_c_re", "ssm_c_im", "ssm_d",
         "ssm_glu_b", "sg_ln_w", "sg_ln_b", "sg_w", "sg_b", "attn_sinks")
G8 = SSM_GROUPS // N_SLAB


def _diag_mask(rows_per_group, cols_per_group):
    r = jnp.arange(G8 * rows_per_group)[:, None] // rows_per_group
    c = jnp.arange(G8 * cols_per_group)[None, :] // cols_per_group
    return r == c


def _slab_b(bb_t):
    x = bb_t.transpose(1, 0, 2).reshape(N_SLAB, SLAB_CH, SSM_STATE)
    return jnp.where(_diag_mask(SSM_GROUP, SSM_STATE), jnp.tile(x, (1, 1, G8)), 0)


def _unslab_b(d):
    x = jnp.where(_diag_mask(SSM_GROUP, SSM_STATE), d, 0).reshape(N_SLAB, SLAB_CH, G8, SSM_STATE).sum(axis=2)
    return x.reshape(SSM_GROUPS, SSM_GROUP, SSM_STATE).transpose(1, 0, 2)


def _slab_c(c):
    x = c.transpose(0, 2, 1).reshape(N_SLAB, SLAB_ST, SSM_GROUP)
    return jnp.where(_diag_mask(SSM_STATE, SSM_GROUP), jnp.tile(x, (1, 1, G8)), 0)


def _unslab_c(d):
    x = jnp.where(_diag_mask(SSM_STATE, SSM_GROUP), d, 0).reshape(N_SLAB, SLAB_ST, G8, SSM_GROUP).sum(axis=2)
    return x.reshape(SSM_GROUPS, SSM_STATE, SSM_GROUP).transpose(0, 2, 1)


def _s5_prep(p, tag):
    bt_re = p["ssm_b_re"].transpose(2, 0, 1)
    bt_im = p["ssm_b_im"].transpose(2, 0, 1)
    raw = (p["ssm_a_re"], p["ssm_a_im"], p["ssm_log_dt"][:, None], bt_re, bt_im)
    lr, li, bbr, bbi = _s5_params_fwd(*raw, name=f"s5_params_{tag}")
    ops = (_slab_b(bbr).astype(BF16), _slab_b(bbi).astype(BF16),
           _slab_c(p["ssm_c_re"]).astype(BF16), _slab_c(p["ssm_c_im"]).astype(BF16),
           jnp.broadcast_to(lr.reshape(N_SLAB, 1, SLAB_ST), (N_SLAB, SUB, SLAB_ST)),
           jnp.broadcast_to(li.reshape(N_SLAB, 1, SLAB_ST), (N_SLAB, SUB, SLAB_ST)),
           p["ssm_d"].reshape(N_SLAB, 1, SLAB_CH))
    return raw, ops


def _layer_fwd(x, p, w, tabs, tag, s5=None, proj_of=None, after_proj=None):
    L = x.shape[0]
    h = _rms_fwd(x, p["norm_w"][None], f"rms_fwd_{tag}")
    if proj_of is not None:
        proj = proj_of(h)
    else:
        proj = _mm(h, w["win_t"], "nt", BF16, L, PROJ_TN, D_MODEL, f"in_proj_{tag}")
    if after_proj is not None:
        w = after_proj(proj)
    s5_raw, s5_ops = s5 if s5 is not None else _s5_prep(p, tag)
    ya0 = _s5_fwd(proj, *s5_ops, name=f"s5_fwd_{tag}")
    ya = _glu_fwd(ya0, proj, w["glu"], p["ssm_glu_b"][None], f"glu_fwd_{tag}")
    yb = _sg_fwd(proj, p["sg_ln_w"][None], p["sg_ln_b"][None], p["sg_w"], p["sg_b"][:, :, None], f"sg_fwd_{tag}")
    yc, o_att = _attn_fwd(proj, p["attn_sinks"], tabs, f"attn_fwd_{tag}")
    ta = _mm(ya, w["wba_t"], "nt", BF16, 1024, 1024, 1024, f"branch_a_{tag}")
    tb = _mm(yb, w["wbb_t"], "nt", BF16, 1024, 1024, 1024, f"branch_b_{tag}")
    tc = _mm(yc, w["wbc_t"], "nt", BF16, 1024, 1024, 1024, f"branch_c_{tag}")
    merged = _merge_fwd(ta, tb, tc, proj, f"merge_fwd_{tag}")
    x_new = _mm(merged, w["wout"], "nn", F32, 1024, 512, D_MODEL, f"out_proj_{tag}", res=x)
    saved = dict(x=x, h=h, proj=proj, s5_raw=s5_raw, s5_ops=s5_ops, ya0=ya0, ya=ya, yb=yb, yc=yc, o_att=o_att,
                 ta=ta, tb=tb, tc=tc, merged=merged)
    return x_new, saved


def _layer_bwd(dx_out, p, w, tabs, s, tag, first_after=None, after_merge=None, before_win=None, after_win=None):
    L = dx_out.shape[0]
    proj = s["proj"]
    big, small = {}, {}
    dmerged = _mm(dx_out, w["wout"], "nt", BF16, 1024, 512, D_MODEL, f"d_merged_{tag}", after=first_after)
    big["wout"] = _mm(s["merged"], dx_out, "tn", GRAD_DT, 512, 1024, L, f"d_wout_{tag}")
    dta, dtb, dtc, dga, dgb, dgc = _merge_bwd(s["ta"], s["tb"], s["tc"], proj, dmerged, f"merge_bwd_{tag}")
    tok = after_merge(dga) if after_merge is not None else None
    dy = {}
    for br, dt in (("a", dta), ("b", dtb), ("c", dtc)):
        dy[br] = _mm(dt, w[f"wb{br}_t"], "nn", BF16, 1024, 1024, D_MODEL, f"d_y{br}_{tag}", after=tok)
        big[f"wb{br}_t"] = _mm(dt, s[f"y{br}"], "tn", GRAD_DT, 512, 1024, L, f"d_wb{br}_{tag}")

    dq, dzc, dkvc, dkvp, dsink = _attn_bwd(proj, p["attn_sinks"], tabs, s["o_att"], dy["c"], f"attn_bwd_{tag}")
    dkv = dkvc + jnp.concatenate([dkvp[CHUNK:], jnp.zeros((CHUNK, 256), F32)], axis=0)
    small["attn_sinks"] = dsink[:, 0]

    dub, dvb, dzb, dlw, dlb, dsgw, dsgb = _sg_bwd(
        proj, p["sg_ln_w"][None], p["sg_ln_b"][None], p["sg_w"], p["sg_b"][:, :, None], dy["b"], f"sg_bwd_{tag}")
    small.update(sg_ln_w=dlw[0], sg_ln_b=dlb[0], sg_w=dsgw, sg_b=dsgb[:, :, 0])

    dya0, dza, dglu, dglub = _glu_bwd(s["ya0"], proj, w["glu"], p["ssm_glu_b"][None], dy["a"], f"glu_bwd_{tag}")
    big["glu"] = dglu.astype(GRAD_DT)
    small["ssm_glu_b"] = dglub[0]

    dua, dbre, dbim, dcre, dcim, dlr, dli, dd = _s5_bwd(proj, dya0, *s["s5_ops"], name=f"s5_bwd_{tag}")
    da_re, da_im, dlog_dt, dbt_re, dbt_im = _s5_params_bwd(
        *s["s5_raw"], dlr.reshape(SSM_GROUPS, SSM_STATE), dli.reshape(SSM_GROUPS, SSM_STATE),
        _unslab_b(dbre), _unslab_b(dbim), name=f"s5_params_bwd_{tag}")
    small.update(ssm_a_re=da_re, ssm_a_im=da_im, ssm_log_dt=dlog_dt[:, 0],
                 ssm_bt_re=dbt_re, ssm_bt_im=dbt_im,
                 ssm_c_re=_unslab_c(dcre), ssm_c_im=_unslab_c(dcim), ssm_d=dd.reshape(SSM_WIDTH))

    dproj = jnp.concatenate([dua, dza, dub, dvb, dzb, dq, dkv.astype(BF16), dzc, dga, dgb, dgc], axis=-1)
    tok = before_win(big) if before_win is not None else None
    big["win_t"] = _mm(dproj, s["h"], "tn", GRAD_DT, 256, D_MODEL, L, f"d_win_{tag}", after=tok)
    tok = after_win(big) if after_win is not None else None
    dh = _mm(dproj, w["win_t"], "nn", F32, L, D_MODEL, 256, f"d_h_{tag}", after=tok)
    dx_in, dnw = _rms_bwd(s["x"], p["norm_w"][None], dh, dx_out, f"rms_bwd_{tag}")
    small["norm_w"] = dnw[0]
    return dx_in, big, small


def _local_step(x, tgt, small_p, final_w, big_w):
    L = x.shape[0]
    tabs = _rope_tables(L)
    saved = []
    for l in range(DEPTH):
        x, s = _layer_fwd(x, small_p[l], big_w[l], tabs, f"l{l}")
        saved.append(s)
    loss_acc, dx, dfw = _final(x, final_w[None], tgt, "final_norm_loss")
    big_g, small_g = [None] * DEPTH, [None] * DEPTH
    for l in reversed(range(DEPTH)):
        dx, big_g[l], small_g[l] = _layer_bwd(dx, small_p[l], big_w[l], tabs, saved[l], f"l{l}")
    return loss_acc[0, 0], dx, dfw[0], big_g, small_g


MESH = pl.DeviceIdType.MESH
ANY = pl.BlockSpec(memory_space=pl.ANY)
ROW_ALIGN = 16


def _place():
    return lax.axis_index("x"), lax.axis_index("y"), lax.axis_index("c")


HBM = pl.BlockSpec(memory_space=pltpu.HBM)
SEM = pl.BlockSpec(memory_space=pltpu.SEMAPHORE)
EFFECT = pltpu.SideEffectType.DATAFLOW_SIDE_EFFECTING


def _split_start(srcs, lands, n_copies, copies, name, after=None):
    n, m, k = len(srcs), len(lands), n_copies
    extra = [] if after is None else [after]

    def body(*refs):
        src_refs, land_refs = refs[:n], refs[n:n + m]
        sems = refs[n + m + len(extra):]
        send_sems, recv_sems, token = sems[:k], sems[k:2 * k], refs[-1]
        for cp in copies(src_refs, land_refs, send_sems, recv_sems):
            cp.start()
        token[...] = jnp.zeros_like(token)

    ops = list(srcs) + list(lands)
    outs = pl.pallas_call(
        body, name=name,
        out_shape=(*[pltpu.SemaphoreType.DMA(())] * (2 * k),
                   *[pltpu.HBM(a.shape, a.dtype) for a in ops], jax.ShapeDtypeStruct((8, 128), F32)),
        in_specs=[HBM] * (n + m) + [ANY] * len(extra),
        out_specs=(*[SEM] * (2 * k), *[HBM] * (n + m), pl.BlockSpec(memory_space=pltpu.VMEM)),
        input_output_aliases={i: 2 * k + i for i in range(n + m)},
        compiler_params=pltpu.CompilerParams(has_side_effects=EFFECT),
    )(*[pltpu.with_memory_space_constraint(a, pltpu.HBM) for a in ops], *extra)
    return (list(outs[:k]), list(outs[k:2 * k]), list(outs[2 * k:2 * k + n]), list(outs[2 * k + n:2 * k + n + m]),
            outs[-1])


def _split_wait(send_sems, recv_sems, srcs, lands, after, copies, name):
    n, m, k = len(srcs), len(lands), len(send_sems)
    after = list(after) if isinstance(after, (list, tuple)) else [after]

    def body(*refs):
        src_refs, land_refs = refs[:n], refs[n:n + m]
        for cp in copies(src_refs, land_refs, refs[n + m:n + m + k], refs[n + m + k:n + m + 2 * k]):
            cp.wait_send()
            cp.wait_recv()

    ops = list(srcs) + list(lands)
    outs = pl.pallas_call(
        body, name=name,
        out_shape=tuple(pltpu.HBM(a.shape, a.dtype) for a in ops),
        in_specs=[HBM] * (n + m) + [SEM] * (2 * k) + [ANY] * len(after),
        out_specs=tuple([HBM] * (n + m)),
        input_output_aliases={i: i for i in range(n + m)},
        compiler_params=pltpu.CompilerParams(has_side_effects=EFFECT),
    )(*ops, *send_sems, *recv_sems, *after)
    return list(outs[:n]), list(outs[n:])


def _ag_rows(land_ref, px, py, pc):
    r = land_ref.shape[0] // N_DEV
    start = pl.multiple_of((4 * px + 2 * py + pc) * r, ROW_ALIGN)
    return land_ref.at[pl.ds(start, r), :]


def _ag_copies_to(which):
    def copies(src_refs, land_refs, send_sems, recv_sems):
        x, y, c = _place()
        peers = [(x, y, 1 - c), (1 - x, y, c), (x, 1 - y, c), (1 - x, 1 - y, c)]
        return [pltpu.make_async_remote_copy(
            src_ref=_ag_rows(land_refs[a], x, y, c), dst_ref=_ag_rows(land_refs[a], x, y, c),
            send_sem=send_sems[len(which) * a + k], recv_sem=recv_sems[len(which) * a + k],
            device_id=peers[p], device_id_type=MESH)
            for a in range(len(land_refs)) for k, p in enumerate(which)]
    return copies


_ag_copies = _ag_copies_to((0, 1, 2, 3))
_ag_copies_near = _ag_copies_to((0, 1, 2))
_ag_copies_far = _ag_copies_to((3,))


def _ag_forward(lands, name, which=(0, 1, 2)):
    n = len(lands)

    def body(*refs):
        land_refs = refs[n:2 * n]
        send_sems, recv_sems = refs[2 * n:]
        x, y, c = _place()
        chips = [(1 - x, y), (x, 1 - y), (1 - x, 1 - y)]

        def copy(a, k, pc):
            px, py = chips[which[k]]
            return pltpu.make_async_remote_copy(
                src_ref=_ag_rows(land_refs[a], px, py, pc), dst_ref=_ag_rows(land_refs[a], px, py, pc),
                send_sem=send_sems.at[a, k], recv_sem=recv_sems.at[a, k], device_id=(x, y, 1 - c), device_id_type=MESH)

        passed = [copy(a, k, c) for a in range(n) for k in range(len(which))]
        for cp in passed:
            cp.start()
        for a in range(n):
            for k in range(len(which)):
                copy(a, k, 1 - c).wait_recv()
        for cp in passed:
            cp.wait_send()

    sems = pltpu.SemaphoreType.DMA((n, len(which)))
    return pl.pallas_call(
        body, name=name,
        in_specs=[ANY] * n, out_specs=[ANY] * n,
        out_shape=[jax.ShapeDtypeStruct(l.shape, l.dtype) for l in lands],
        input_output_aliases={i: i for i in range(n)},
        scratch_shapes=[sems, sems],
    )(*lands)


def _allgather_place(shards):
    x, y, c = _place()
    return [lax.dynamic_update_slice(lax.empty((N_DEV * s.shape[0], s.shape[1]), s.dtype), s,
                                     ((4 * x + 2 * y + c) * s.shape[0], 0)) for s in shards]


def _allgather_start(lands, name, after=None):
    return _split_start([], lands, 4 * len(lands), _ag_copies, name + "_start", after=after)


def _allgather_finish(started, after, name):
    send_sems, recv_sems, _, lands, _ = started
    _, lands = _split_wait(send_sems, recv_sems, [], lands, after, _ag_copies, name + "_wait")
    return list(_ag_forward(lands, name + "_forward"))


def _rs_swap_cores(grads, name):
    n = len(grads)

    def body(*refs):
        ins, outs = refs[:n], refs[n:2 * n]
        send_sems, recv_sems = refs[2 * n:]
        x, y, c = _place()
        cps = []
        for a in range(n):
            r = ins[a].shape[0] // N_DEV
            for q in range(4):
                start = pl.multiple_of((2 * q + 1 - c) * r, ROW_ALIGN)
                cps.append(pltpu.make_async_remote_copy(
                    src_ref=ins[a].at[pl.ds(start, r), :], dst_ref=outs[a].at[q],
                    send_sem=send_sems.at[a, q], recv_sem=recv_sems.at[a, q],
                    device_id=(x, y, 1 - c), device_id_type=MESH))
        for cp in cps:
            cp.start()
        for cp in cps:
            cp.wait()

    return pl.pallas_call(
        body, name=name, in_specs=[ANY] * n, out_specs=[ANY] * n,
        out_shape=[jax.ShapeDtypeStruct((4, g.shape[0] // N_DEV, g.shape[1]), g.dtype) for g in grads],
        scratch_shapes=[pltpu.SemaphoreType.DMA((n, 4)), pltpu.SemaphoreType.DMA((n, 4))],
    )(*grads)


def _rs_chip_copies(sum_refs, land_refs, send_sems, recv_sems):
    x, y, c = _place()
    chips = [(1 - x, y), (x, 1 - y), (1 - x, 1 - y)]
    return [pltpu.make_async_remote_copy(
        src_ref=sum_refs[a].at[2 * px + py], dst_ref=land_refs[a].at[2 * x + y],
        send_sem=send_sems[3 * a + j], recv_sem=recv_sems[3 * a + j], device_id=(px, py, c), device_id_type=MESH)
        for a in range(len(sum_refs)) for j, (px, py) in enumerate(chips)]


def _row_tile(r):
    return max(t for t in range(ROW_ALIGN, min(r, 1024) + 1, ROW_ALIGN) if r % t == 0)


def _rs_add_cores(grad, recv, cidx, name):
    r, cols = recv.shape[1], recv.shape[2]
    tr = _row_tile(r)
    nb = r // tr

    def body(c_ref, g_ref, r_ref, o_ref):
        o_ref[...] = (g_ref[...].astype(F32) + r_ref[...].astype(F32)).astype(o_ref.dtype)

    return pl.pallas_call(
        body, name=name,
        grid_spec=pltpu.PrefetchScalarGridSpec(
            num_scalar_prefetch=1, grid=(4, nb),
            in_specs=[pl.BlockSpec((tr, cols), lambda q, i, c_ref: ((2 * q + c_ref[0]) * nb + i, 0)),
                      pl.BlockSpec((None, tr, cols), lambda q, i, c_ref: (q, i, 0))],
            out_specs=pl.BlockSpec((None, tr, cols), lambda q, i, c_ref: (q, i, 0))),
        out_shape=jax.ShapeDtypeStruct(recv.shape, recv.dtype),
        compiler_params=_cp(("parallel", "parallel")),
    )(cidx, grad, recv)


def _rs_add_chips(own, recv, slots, name):
    r, cols = recv.shape[1], recv.shape[2]
    tr = _row_tile(r)

    def body(s_ref, o_ref, r0_ref, r1_ref, r2_ref, out_ref):
        acc = o_ref[...].astype(F32)
        for ref in (r0_ref, r1_ref, r2_ref):
            acc = acc + ref[...].astype(F32)
        out_ref[...] = acc

    pick = lambda k: pl.BlockSpec((None, tr, cols), functools.partial(lambda i, s_ref, k: (s_ref[k], i, 0), k=k))
    return pl.pallas_call(
        body, name=name,
        grid_spec=pltpu.PrefetchScalarGridSpec(
            num_scalar_prefetch=1, grid=(r // tr,),
            in_specs=[pick(0), pick(1), pick(2), pick(3)],
            out_specs=pl.BlockSpec((tr, cols), lambda i, s_ref: (i, 0))),
        out_shape=jax.ShapeDtypeStruct((r, cols), F32),
        compiler_params=_cp(("parallel",)),
    )(slots, own, recv, recv, recv)


def _rs_core_copies(grad_refs, land_refs, send_sems, recv_sems):
    x, y, c = _place()
    cps = []
    for a in range(len(grad_refs)):
        r = grad_refs[a].shape[0] // N_DEV
        for q in range(4):
            start = pl.multiple_of((2 * q + 1 - c) * r, ROW_ALIGN)
            cps.append(pltpu.make_async_remote_copy(
                src_ref=grad_refs[a].at[pl.ds(start, r), :], dst_ref=land_refs[a].at[q],
                send_sem=send_sems[4 * a + q], recv_sem=recv_sems[4 * a + q],
                device_id=(x, y, 1 - c), device_id_type=MESH))
    return cps


def _reduce_scatter_chips_start(grads, recv, tag):
    cidx = lax.axis_index("c").astype(jnp.int32)[None]
    sums = [_rs_add_cores(g, rv, cidx, f"rs_add_cores_{tag}_{i}") for i, (g, rv) in enumerate(zip(grads, recv))]
    lands = [lax.empty(s.shape, s.dtype) for s in sums]
    return _split_start(sums, lands, 3 * len(sums), _rs_chip_copies, f"rs_chips_{tag}_start")


def _reduce_scatter_start(grads, tag):
    return _reduce_scatter_chips_start(grads, _rs_swap_cores(grads, f"rs_swap_cores_{tag}"), tag)


def _reduce_scatter_cores_start(grads, tag):
    lands = [lax.empty((4, g.shape[0] // N_DEV, g.shape[1]), g.dtype) for g in grads]
    return _split_start(grads, lands, 4 * len(grads), _rs_core_copies, f"rs_cores_{tag}_start")


def _reduce_scatter_cores_finish(started, after, tag):
    send_sems, recv_sems, grads, lands, _ = started
    grads, recv = _split_wait(send_sems, recv_sems, grads, lands, after, _rs_core_copies, f"rs_cores_{tag}_wait")
    return _reduce_scatter_chips_start(grads, recv, tag)


def _reduce_scatter_finish(started, after, tag):
    send_sems, recv_sems, sums, lands, _ = started
    sums, lands = _split_wait(send_sems, recv_sems, sums, lands, after, _rs_chip_copies, f"rs_chips_{tag}_wait")
    x, y = lax.axis_index("x"), lax.axis_index("y")
    slots = jnp.stack([2 * x + y, 2 * (1 - x) + y, 2 * x + 1 - y, 2 * (1 - x) + 1 - y]).astype(jnp.int32)
    return [_rs_add_chips(s, l, slots, f"rs_add_chips_{tag}_{i}") for i, (s, l) in enumerate(zip(sums, lands))]


def _allreduce_small(packs, name, after=()):
    n = len(packs)
    after = list(after)
    assert all(p.shape[0] % (8 * N_DEV) == 0 for p in packs)

    def body(*refs):
        p_refs = refs[:n]
        refs = refs[n + len(after):]
        o_refs, part_refs = refs[:n], refs[n:2 * n]
        send1, recv1, send2, recv2 = refs[2 * n:]
        x, y, c = _place()
        me = 4 * x + 2 * y + c

        def block(ref, d):
            rs = ref.shape[0] // N_DEV
            return ref.at[pl.ds(pl.multiple_of(d * rs, 8), rs), :]

        peers = [(1 - x if k & 4 else x, 1 - y if k & 2 else y, 1 - c if k & 1 else c) for k in range(1, N_DEV)]
        scatter = [pltpu.make_async_remote_copy(
            src_ref=block(p_refs[a], 4 * px + 2 * py + pc), dst_ref=part_refs[a].at[me],
            send_sem=send1.at[a, k], recv_sem=recv1.at[a, k], device_id=(px, py, pc), device_id_type=MESH)
            for a in range(n) for k, (px, py, pc) in enumerate(peers)]
        for cp in scatter:
            cp.start()
        for a in range(n):
            part_refs[a][me] = block(p_refs[a], me)[...]
        for cp in scatter:
            cp.wait()
        for a in range(n):
            acc = part_refs[a][0]
            for d in range(1, N_DEV):
                acc = acc + part_refs[a][d]
            block(o_refs[a], me)[...] = acc
        gather = [pltpu.make_async_remote_copy(
            src_ref=block(o_refs[a], me), dst_ref=block(o_refs[a], me), send_sem=send2.at[a, k], recv_sem=recv2.at[a, k],
            device_id=peer, device_id_type=MESH) for a in range(n) for k, peer in enumerate(peers)]
        for cp in gather:
            cp.start()
        for a in range(n):
            for k, (px, py, pc) in enumerate(peers):
                theirs = block(o_refs[a], 4 * px + 2 * py + pc)
                pltpu.make_async_remote_copy(
                    src_ref=theirs, dst_ref=theirs, send_sem=send2.at[a, k], recv_sem=recv2.at[a, k],
                    device_id=(px, py, pc), device_id_type=MESH).wait_recv()
        for cp in gather:
            cp.wait_send()

    sems = pltpu.SemaphoreType.DMA((n, N_DEV - 1))
    vmem = pl.BlockSpec(memory_space=pltpu.VMEM)
    return pl.pallas_call(
        body, name=name,
        in_specs=[vmem] * n + [ANY] * len(after), out_specs=[vmem] * n,
        out_shape=[jax.ShapeDtypeStruct(p.shape, F32) for p in packs],
        scratch_shapes=[pltpu.VMEM((N_DEV, p.shape[0] // N_DEV, p.shape[1]), F32) for p in packs] + [sems] * 4,
        compiler_params=pltpu.CompilerParams(vmem_limit_bytes=VMEM_LIMIT),
    )(*packs, *after)


ADAM_TILE_BYTES = 2 * 1024 * 1024


def _adam_tiles(rows, cols):
    tc = cols // 2 if cols % 256 == 0 and cols >= 2048 else cols
    tr = max(t for t in range(8, rows + 1, 8) if rows % t == 0 and t * max(tc, 128) * 4 <= ADAM_TILE_BYTES) \
        if rows % 8 == 0 else rows
    return tr, tc


def _adam_math(w, g, m, v):
    nm = ADAM_B1 * m + (1.0 - ADAM_B1) * g
    nv = ADAM_B2 * v + (1.0 - ADAM_B2) * jnp.square(g)
    c1 = 1.0 - ADAM_B1 ** ADAM_STEP
    c2 = 1.0 - ADAM_B2 ** ADAM_STEP
    return -ADAM_LR * ((nm / c1) / (jnp.sqrt(nv / c2) + ADAM_EPS) + ADAM_WD * w), nm, nv


def _adamw_layer(w, g, m, v, layer, carry, name):
    _, rows, cols = w.shape
    tr, tc = _adam_tiles(rows, cols)

    def body(w_ref, g_ref, m_ref, v_ref, *rest):
        go_ref, d_ref, nm_ref, nv_ref = rest[-4:]
        gv = g_ref[...]
        go_ref[...] = gv
        d_ref[...], nm_ref[...], nv_ref[...] = _adam_math(w_ref[...], gv, m_ref[...], v_ref[...])

    blk = pl.BlockSpec((None, tr, tc), lambda i, j: (layer, i, j))
    flat = pl.BlockSpec((tr, tc), lambda i, j: (i, j))
    sh = jax.ShapeDtypeStruct(w.shape, F32)
    carry = [] if carry is None else list(carry)
    return pl.pallas_call(
        body, name=name, grid=(rows // tr, cols // tc),
        in_specs=[blk, flat, blk, blk] + [ANY] * len(carry), out_specs=[blk] * 4, out_shape=[sh] * 4,
        input_output_aliases={4 + k: k for k in range(len(carry))},
        compiler_params=_cp(("parallel", "parallel")),
    )(w, g, m, v, *carry)


def _adamw(w, g, m, v, name):
    shape = w.shape
    rows, cols = shape[-2:]
    lead = shape[:-2]
    nl = math.prod(lead)
    tr, tc = _adam_tiles(rows, cols)

    def body(w_ref, g_ref, m_ref, v_ref, d_ref, nm_ref, nv_ref):
        d_ref[...], nm_ref[...], nv_ref[...] = _adam_math(w_ref[...], g_ref[...], m_ref[...], v_ref[...])

    def index(b, i, j):
        return (*jnp.unravel_index(b, lead), i, j) if lead else (i, j)

    blk = pl.BlockSpec((*[None] * len(lead), tr, tc), index)
    sh = jax.ShapeDtypeStruct(shape, F32)
    return pl.pallas_call(
        body, name=name, grid=(nl, rows // tr, cols // tc), in_specs=[blk] * 4, out_specs=[blk] * 3,
        out_shape=[sh] * 3, compiler_params=_cp(("parallel", "parallel", "parallel")),
    )(w, g, m, v)


WEIGHTS = ("norm_w", "w_in", "ssm_a_re", "ssm_a_im", "ssm_log_dt", "ssm_b_re", "ssm_b_im", "ssm_c_re", "ssm_c_im",
           "ssm_d", "ssm_glu_w", "ssm_glu_b", "sg_ln_w", "sg_ln_b", "sg_w", "sg_b", "attn_sinks",
           "w_branch_a", "w_branch_b", "w_branch_c", "w_out", "final_norm_w")
BIG = ("w_in", "ssm_glu_w", "w_branch_a", "w_branch_b", "w_branch_c", "w_out")
BIG_KEY = {"w_in": ("win_t", True), "ssm_glu_w": ("glu", False), "w_branch_a": ("wba_t", True),
           "w_branch_b": ("wbb_t", True), "w_branch_c": ("wbc_t", True), "w_out": ("wout", False)}
VIEWS = {"w_in": (1, 2), "ssm_b_re": (2, 3), "ssm_b_im": (2, 3)}
PACKS = (
    (64, (("ssm_a_re",), ("ssm_a_im",), ("ssm_c_re",), ("ssm_c_im",), ("ssm_b_re",), ("ssm_b_im",))),
    (128, (("sg_w",),)),
    (1024, (("ssm_d", "ssm_glu_b", "sg_ln_w", "sg_ln_b"), ("norm_w", "final_norm_w", "sg_b"), ("ssm_log_dt", "attn_sinks"))),
)
PACK_ROWS = 8 * N_DEV


def _view(n, a):
    return jnp.swapaxes(a, *VIEWS[n]) if n in VIEWS else a


def _group_rows(arrs, cols):
    return -(-sum(-(-a.size // cols) for a in arrs) // 8) * 8


def _pack(groups, cols):
    parts = []
    for arrs in groups:
        if len(arrs) == 1 and arrs[0].shape[-1] == cols and arrs[0].size % (8 * cols) == 0:
            parts.append(arrs[0].reshape(-1, cols))
            continue
        flat = [jnp.pad(a.reshape(-1), (0, -a.size % cols)) for a in arrs]
        flat = jnp.concatenate(flat) if len(flat) > 1 else flat[0]
        nrow = _group_rows(arrs, cols)
        parts.append(jnp.pad(flat, (0, nrow * cols - flat.shape[0])).reshape(nrow, cols))
    pad = -sum(p.shape[0] for p in parts) % PACK_ROWS
    if pad:
        parts.append(jnp.zeros((pad, cols), F32))
    return jnp.concatenate(parts, axis=0)


def _unpack(pack, groups):
    cols = pack.shape[1]
    out, row = [], 0
    for arrs in groups:
        nrow = _group_rows(arrs, cols)
        rows = pack[row:row + nrow]
        row += nrow
        if len(arrs) == 1 and arrs[0].shape[-1] == cols and arrs[0].size == nrow * cols:
            out.append(rows.reshape(arrs[0].shape))
            continue
        flat, off = rows.reshape(-1), 0
        for a in arrs:
            out.append(flat[off:off + a.size].reshape(a.shape))
            off += -(-a.size // cols) * cols
    return out


def kernel(x, norm_w, w_in, ssm_a_re, ssm_a_im, ssm_log_dt, ssm_b_re, ssm_b_im, ssm_c_re, ssm_c_im, ssm_d, ssm_glu_w, ssm_glu_b, sg_ln_w, sg_ln_b, sg_w, sg_b, attn_sinks, w_branch_a, w_branch_b, w_branch_c, w_out, final_norm_w, loss_target, m_norm_w, m_w_in, m_ssm_a_re, m_ssm_a_im, m_ssm_log_dt, m_ssm_b_re, m_ssm_b_im, m_ssm_c_re, m_ssm_c_im, m_ssm_d, m_ssm_glu_w, m_ssm_glu_b, m_sg_ln_w, m_sg_ln_b, m_sg_w, m_sg_b, m_attn_sinks, m_w_branch_a, m_w_branch_b, m_w_branch_c, m_w_out, m_final_norm_w, v_norm_w, v_w_in, v_ssm_a_re, v_ssm_a_im, v_ssm_log_dt, v_ssm_b_re, v_ssm_b_im, v_ssm_c_re, v_ssm_c_im, v_ssm_d, v_ssm_glu_w, v_ssm_glu_b, v_sg_ln_w, v_sg_ln_b, v_sg_w, v_sg_b, v_attn_sinks, v_w_branch_a, v_w_branch_b, v_w_branch_c, v_w_out, v_final_norm_w):
    w = dict(zip(WEIGHTS, (norm_w, w_in, ssm_a_re, ssm_a_im, ssm_log_dt, ssm_b_re, ssm_b_im, ssm_c_re, ssm_c_im, ssm_d, ssm_glu_w, ssm_glu_b, sg_ln_w, sg_ln_b, sg_w, sg_b, attn_sinks, w_branch_a, w_branch_b, w_branch_c, w_out, final_norm_w)))
    m = dict(zip(WEIGHTS, (m_norm_w, m_w_in, m_ssm_a_re, m_ssm_a_im, m_ssm_log_dt, m_ssm_b_re, m_ssm_b_im, m_ssm_c_re, m_ssm_c_im, m_ssm_d, m_ssm_glu_w, m_ssm_glu_b, m_sg_ln_w, m_sg_ln_b, m_sg_w, m_sg_b, m_attn_sinks, m_w_branch_a, m_w_branch_b, m_w_branch_c, m_w_out, m_final_norm_w)))
    v = dict(zip(WEIGHTS, (v_norm_w, v_w_in, v_ssm_a_re, v_ssm_a_im, v_ssm_log_dt, v_ssm_b_re, v_ssm_b_im, v_ssm_c_re, v_ssm_c_im, v_ssm_d, v_ssm_glu_w, v_ssm_glu_b, v_sg_ln_w, v_sg_ln_b, v_sg_w, v_sg_b, v_attn_sinks, v_w_branch_a, v_w_branch_b, v_w_branch_c, v_w_out, v_final_norm_w)))

    keys = [BIG_KEY[n][0] for n in BIG]
    wv, mv, vv = ({n: _view(n, a) for n, a in d.items()} for d in (w, m, v))
    shards = [[(wv[n][l] if n in VIEWS else w[n][l].T if BIG_KEY[n][1] else w[n][l]).astype(BF16) for n in BIG]
              for l in range(DEPTH)]
    small_p = [{n: w[n][l] for n in SMALL} for l in range(DEPTH)]
    xv, tgt = x[0], loss_target[0]
    tabs = _rope_tables(xv.shape[0])

    lands = [[_allgather_place(shards[l][:1]), _allgather_place(shards[l][1:])] for l in range(DEPTH)]
    s5 = [_s5_prep(small_p[l], f"l{l}") for l in range(DEPTH)]
    wmv_packs = {cols: [_pack([[d[n] for n in names] for names in groups], cols) for d in (wv, mv, vv)]
                 for cols, groups in PACKS}
    near = _split_start([], lands[0][0], 3, _ag_copies_near, "ag_l0_win_near_start")
    got = {}
    x_, y_ = lax.axis_index("x"), lax.axis_index("y")
    n_tiles = D_IN // PROJ_TN
    far_first = (D_IN // 4 // PROJ_TN) * (2 * (1 - x_) + (1 - y_))
    n_far = -(-D_IN // 4 // PROJ_TN)
    tile_ids = jnp.arange(n_tiles, dtype=jnp.int32)
    is_far = (tile_ids >= far_first) & (tile_ids < far_first + n_far)
    near_tiles = jnp.sort(jnp.where(is_far, n_tiles, tile_ids))[:n_tiles - n_far]
    far_tiles = (far_first + jnp.arange(n_far)).astype(jnp.int32)

    def proj_of0(h):
        early = [h, *lands[0][1], *lands[1][0], *lands[1][1], *s5[0][1], *s5[1][1], near_tiles, far_tiles]
        early += [p for ps in wmv_packs.values() for p in ps]
        _, land = _split_wait(near[0], near[1], [], near[3], early, _ag_copies_near, "ag_l0_win_near_wait")
        land = _ag_forward(land, "ag_l0_win_near_forward", which=(0, 1))
        far = _split_start([], land, 1, _ag_copies_far, "ag_l0_win_far_start")
        got["ag0b"] = _allgather_start(lands[0][1], "ag_l0_rest", after=far[4])
        got["near1"] = _split_start([], lands[1][0], 3, _ag_copies_near, "ag_l1_win_near_start", after=got["ag0b"][4])
        got["far1"] = _split_start([], got["near1"][3], 1, _ag_copies_far, "ag_l1_win_far_start", after=got["near1"][4])
        got["ag1b"] = _allgather_start(lands[1][1], "ag_l1_rest", after=got["far1"][4])
        proj = _in_proj_tiles(h, far[3][0], near_tiles, None, "in_proj_l0_near", after=got["ag1b"][4])
        _, land = _split_wait(far[0], far[1], [], far[3], proj, _ag_copies_far, "ag_l0_win_far_wait")
        got["win0"] = _ag_forward(land, "ag_l0_win_far_forward", which=(2,))[0]
        return _in_proj_tiles(h, got["win0"], far_tiles, proj, "in_proj_l0_far")

    def after_proj0(proj):
        got["w0"] = dict(zip(keys, [got["win0"]] + _allgather_finish(got["ag0b"], proj, "ag_l0_rest")))
        return got["w0"]

    x1, saved0 = _layer_fwd(xv, small_p[0], None, tabs, "l0", s5=s5[0], proj_of=proj_of0, after_proj=after_proj0)
    big_w0 = got["w0"]

    def proj_of1(h):
        near1, far1 = got["near1"], got["far1"]
        _, land = _split_wait(near1[0], near1[1], [], far1[3], h, _ag_copies_near, "ag_l1_win_near_wait")
        land = _ag_forward(land, "ag_l1_win_near_forward", which=(0, 1))
        proj = _in_proj_tiles(h, land[0], near_tiles, None, "in_proj_l1_near")
        _, land = _split_wait(far1[0], far1[1], [], land, proj, _ag_copies_far, "ag_l1_win_far_wait")
        got["win1"] = _ag_forward(land, "ag_l1_win_far_forward", which=(2,))[0]
        return _in_proj_tiles(h, got["win1"], far_tiles, proj, "in_proj_l1_far")

    def after_proj1(proj):
        got["w1"] = dict(zip(keys, [got["win1"]] + _allgather_finish(got["ag1b"], proj, "ag_l1_rest")))
        return got["w1"]

    x2, saved1 = _layer_fwd(x1, small_p[1], None, tabs, "l1", s5=s5[1], proj_of=proj_of1, after_proj=after_proj1)
    big_w1 = got["w1"]
    loss_acc, dx2, dfw = _final(x2, w["final_norm_w"][None], tgt, "final_norm_loss")
    loss = lax.psum(loss_acc[0, 0], ("x", "y", "c"))
    dfw = dfw[0]

    dx1, big_g1, small_g1 = _layer_bwd(dx2, small_p[1], big_w1, tabs, saved1, "l1")
    rs1_cores = _reduce_scatter_cores_start([big_g1[k] for k in keys], "l1")

    def after_merge0(x):
        got["rs1"] = _reduce_scatter_cores_finish(rs1_cores, x, "l1")
        return got["rs1"][4]

    def before_win0(big):
        got["rs0b"] = _reduce_scatter_start([big[k] for k in keys[1:]], "l0_rest")
        return got["rs0b"][4]

    def after_win0(big):
        got["rs0a"] = _reduce_scatter_start([big["win_t"]], "l0_win")
        return got["rs0a"][4]

    dx, big_g0, small_g0 = _layer_bwd(dx1, small_p[0], big_w0, tabs, saved0, "l0", first_after=rs1_cores[4],
                                      after_merge=after_merge0, before_win=before_win0, after_win=after_win0)
    rs1 = got["rs1"]
    small_g = [small_g0, small_g1]
    grads, delta, new_m, new_v = {}, {}, {}, {}

    def big_adam(red, layer, carry):
        outs = {}
        for i, n in enumerate(BIG):
            g = red[i].T if BIG_KEY[n][1] and n not in VIEWS else red[i]
            outs[n] = _adamw_layer(wv[n], g, mv[n], vv[n], layer, None if carry is None else carry[n], f"adamw_{n}_l{layer}")
        return outs

    big1 = big_adam(_reduce_scatter_finish(rs1, dx, "l1"), 1, None)

    def small_grad(n):
        if n == "final_norm_w":
            return dfw
        if n in ("ssm_b_re", "ssm_b_im"):
            return jnp.stack([small_g[l][n.replace("ssm_b_", "ssm_bt_")].transpose(1, 0, 2) for l in range(DEPTH)])
        return jnp.stack([small_g[l][n] for l in range(DEPTH)])

    g_groups = [[[small_grad(n) for n in names] for names in groups] for _, groups in PACKS]
    reduced = _allreduce_small([_pack(gg, cols) for gg, (cols, _) in zip(g_groups, PACKS)], "allreduce_small",
                               after=[big1[n][1] for n in BIG])
    last = None
    for (cols, groups), gg, red in zip(PACKS, g_groups, reduced):
        names = [n for names in groups for n in names]
        grads.update(zip(names, _unpack(red, gg)))
        wp, mp, vp = wmv_packs[cols]
        outs = _adamw(wp, red, mp, vp, f"adamw_pack{cols}")
        last = outs[0]
        for res, o in zip((delta, new_m, new_v), outs):
            res.update(zip(names, _unpack(o, [[wv[n] for n in names] for names in groups])))

    red0 = (_reduce_scatter_finish(got["rs0a"], last, "l0_win")
            + _reduce_scatter_finish(got["rs0b"], last, "l0_rest"))
    for n, outs in big_adam(red0, 0, big1).items():
        grads[n], delta[n], new_m[n], new_v[n] = outs

    return (loss, dx[None], *[_view(n, d[n]) for d in (grads, delta, new_m, new_v) for n in WEIGHTS])
```

```python
import functools
import math

import jax
import jax.numpy as jnp
from jax import lax
from jax.experimental import pallas as pl
from jax.experimental.pallas import tpu as pltpu

F32 = jnp.float32
BF16 = jnp.bfloat16

D_MODEL = 2048
DEPTH = 2
EPS = 1e-6
NEG_INF = -1e30
N_DEV = 8

SSM_WIDTH = 1024
SSM_GROUP = 16
SSM_GROUPS = 64
SSM_STATE = 64
N_SLAB = 8
SLAB_CH = 128
SLAB_ST = 512
SUB = 8
N_GRP = 2
N_SEG = SUB * N_GRP

SG_HEADS = 8
CHUNK = 128
HEAD_DIM = 64
ATT_HEADS = 16
ROT_DIM = 16
ROPE_THETA = 500000.0

D_IN = 13568
OFF_UA, OFF_ZA, OFF_UB, OFF_VB, OFF_ZB, OFF_Q, OFF_KV, OFF_ZC, OFF_G = (
    0, 1024, 2048, 3072, 4096, 5120, 6144, 6400, 7424)

ADAM_LR, ADAM_B1, ADAM_B2, ADAM_EPS, ADAM_WD, ADAM_STEP = 0.001, 0.9, 0.999, 1e-08, 0.01, 10

VMEM_LIMIT = 56 * 1024 * 1024


def _cp(sem=None):
    return pltpu.CompilerParams(dimension_semantics=sem, vmem_limit_bytes=VMEM_LIMIT)


def _dot(a, b):
    return jnp.dot(a, b, preferred_element_type=F32)


def _dot_nt(a, b):
    return lax.dot_general(a, b, (((1,), (1,)), ((), ())), preferred_element_type=F32)


def _dot_tn(a, b):
    return lax.dot_general(a, b, (((0,), (0,)), ((), ())), preferred_element_type=F32)


def _mm(a, b, mode, out_dtype, tm, tn, tk, name, res=None, after=None):
    if mode == "nn":
        (m, k), (_, n) = a.shape, b.shape
    elif mode == "nt":
        (m, k), (n, _) = a.shape, b.shape
    else:
        (k, m), (_, n) = a.shape, b.shape
    tm, tn, tk = min(tm, m), min(tn, n), min(tk, k)
    assert m % tm == 0 and n % tn == 0 and k % tk == 0, (name, m, n, k, tm, tn, tk)
    nk = k // tk
    a_spec = {"nn": pl.BlockSpec((tm, tk), lambda i, j, kk: (i, kk)),
              "nt": pl.BlockSpec((tm, tk), lambda i, j, kk: (i, kk)),
              "tn": pl.BlockSpec((tk, tm), lambda i, j, kk: (kk, i))}[mode]
    b_spec = {"nn": pl.BlockSpec((tk, tn), lambda i, j, kk: (kk, j)),
              "nt": pl.BlockSpec((tn, tk), lambda i, j, kk: (j, kk)),
              "tn": pl.BlockSpec((tk, tn), lambda i, j, kk: (kk, j))}[mode]
    dot = {"nn": _dot, "nt": _dot_nt, "tn": _dot_tn}[mode]
    has_res = res is not None
    direct = out_dtype == F32 and not has_res

    def body(*refs):
        ins, outs = refs[:2 + has_res + (after is not None)], refs[2 + has_res + (after is not None):]
        a_ref, b_ref = ins[:2]
        r_ref = ins[2] if has_res else None
        o_ref = outs[0]
        acc = o_ref if direct else outs[1]
        kk = pl.program_id(2)

        @pl.when(kk == 0)
        def _():
            acc[...] = jnp.zeros_like(acc)

        acc[...] += dot(a_ref[...].astype(BF16), b_ref[...].astype(BF16))

        if not direct:
            @pl.when(kk == nk - 1)
            def _():
                r = acc[...]
                if has_res:
                    r = r + r_ref[...]
                o_ref[...] = r.astype(out_dtype)

    in_specs = [a_spec, b_spec]
    args = [a, b]
    if has_res:
        in_specs.append(pl.BlockSpec((tm, tn), lambda i, j, kk: (i, j)))
        args.append(res)
    if after is not None:
        in_specs.append(pl.BlockSpec(memory_space=pl.ANY))
        args.append(after)
    return pl.pallas_call(
        body, name=name,
        grid=(m // tm, n // tn, nk),
        in_specs=in_specs,
        out_specs=pl.BlockSpec((tm, tn), lambda i, j, kk: (i, j)),
        out_shape=jax.ShapeDtypeStruct((m, n), out_dtype),
        scratch_shapes=[] if direct else [pltpu.VMEM((tm, tn), F32)],
        compiler_params=_cp(("parallel", "parallel", "arbitrary")),
    )(*args)


PROJ_TN = 256


def _in_proj_tiles(h, win_t, tiles, carry, name, after=None):
    L, K = h.shape
    extra = [a for a in (carry, after) if a is not None]

    def body(t_ref, h_ref, w_ref, *rest):
        rest[len(extra)][...] = _dot_nt(h_ref[...], w_ref[...]).astype(BF16)

    return pl.pallas_call(
        body, name=name,
        grid_spec=pltpu.PrefetchScalarGridSpec(
            num_scalar_prefetch=1, grid=(tiles.shape[0],),
            in_specs=[pl.BlockSpec((L, K), lambda j, t: (0, 0)), pl.BlockSpec((PROJ_TN, K), lambda j, t: (t[j], 0))]
            + [pl.BlockSpec(memory_space=pl.ANY)] * len(extra),
            out_specs=pl.BlockSpec((L, PROJ_TN), lambda j, t: (0, t[j]))),
        out_shape=jax.ShapeDtypeStruct((L, win_t.shape[0]), BF16),
        input_output_aliases={} if carry is None else {3: 0},
        compiler_params=_cp(("arbitrary",)),
    )(tiles, h, win_t, *extra)


def _rms(x, w):
    return x * lax.rsqrt(jnp.mean(x * x, axis=-1, keepdims=True) + EPS) * w


def _rms_fwd(x, w, name):
    L, D = x.shape
    tm = min(L, 256)

    def body(x_ref, w_ref, h_ref):
        h_ref[...] = _rms(x_ref[...], w_ref[...]).astype(BF16)

    return pl.pallas_call(
        body, name=name, grid=(L // tm,),
        in_specs=[pl.BlockSpec((tm, D), lambda i: (i, 0)), pl.BlockSpec((1, D), lambda i: (0, 0))],
        out_specs=pl.BlockSpec((tm, D), lambda i: (i, 0)),
        out_shape=jax.ShapeDtypeStruct((L, D), BF16),
        compiler_params=_cp(("parallel",)),
    )(x, w)


def _rms_bwd(x, w, dh, dres, name):
    L, D = x.shape
    tm = min(L, 256)

    def body(x_ref, w_ref, dh_ref, dres_ref, dx_ref, dw_ref):
        _, vjp = jax.vjp(_rms, x_ref[...], w_ref[...])
        dx, dw = vjp(dh_ref[...])
        dx_ref[...] = dx + dres_ref[...]

        @pl.when(pl.program_id(0) == 0)
        def _():
            dw_ref[...] = jnp.zeros_like(dw_ref)

        dw_ref[...] += dw

    row = pl.BlockSpec((tm, D), lambda i: (i, 0))
    vec = pl.BlockSpec((1, D), lambda i: (0, 0))
    return pl.pallas_call(
        body, name=name, grid=(L // tm,),
        in_specs=[row, vec, row, row],
        out_specs=[row, vec],
        out_shape=[jax.ShapeDtypeStruct((L, D), F32), jax.ShapeDtypeStruct((1, D), F32)],
        compiler_params=_cp(("arbitrary",)),
    )(x, w, dh, dres)


def _final(x, fw, tgt, name):
    L, D = x.shape
    tm = min(L, 256)

    def loss_fn(xv, wv, tv):
        err = _rms(xv, wv) - tv
        return jnp.sum(err * err) * (0.5 / D)

    def body(x_ref, w_ref, t_ref, loss_ref, dx_ref, dw_ref):
        tv = t_ref[...]
        val, vjp = jax.vjp(lambda a, b: loss_fn(a, b, tv), x_ref[...], w_ref[...])
        dx, dw = vjp(jnp.ones((), F32))
        dx_ref[...] = dx

        @pl.when(pl.program_id(0) == 0)
        def _():
            dw_ref[...] = jnp.zeros_like(dw_ref)
            loss_ref[...] = jnp.zeros_like(loss_ref)

        dw_ref[...] += dw
        loss_ref[...] += jnp.full(loss_ref.shape, val, F32)

    row = pl.BlockSpec((tm, D), lambda i: (i, 0))
    vec = pl.BlockSpec((1, D), lambda i: (0, 0))
    return pl.pallas_call(
        body, name=name, grid=(L // tm,),
        in_specs=[row, vec, row],
        out_specs=[pl.BlockSpec((8, 128), lambda i: (0, 0)), row, vec],
        out_shape=[jax.ShapeDtypeStruct((8, 128), F32), jax.ShapeDtypeStruct((L, D), F32),
                   jax.ShapeDtypeStruct((1, D), F32)],
        compiler_params=_cp(("arbitrary",)),
    )(x, fw, tgt)


def _s5_param_fn(a_re, a_im, log_dt, bt_re, bt_im):
    dt = jnp.exp(log_dt)
    zr, zi = a_re * dt, a_im * dt
    er = jnp.exp(zr)
    lr, li = er * jnp.cos(zi), er * jnp.sin(zi)
    nr, ni = lr - 1.0, li
    den = a_re * a_re + a_im * a_im
    cr = (nr * a_re + ni * a_im) / den
    ci = (ni * a_re - nr * a_im) / den
    bbr = cr[None] * bt_re - ci[None] * bt_im
    bbi = cr[None] * bt_im + ci[None] * bt_re
    return lr, li, bbr, bbi


def _s5_params_fwd(a_re, a_im, log_dt, bt_re, bt_im, name):
    def body(ar, ai, ld, br, bi, lr, li, bbr, bbi):
        o = _s5_param_fn(ar[...], ai[...], ld[...], br[...], bi[...])
        lr[...], li[...], bbr[...], bbi[...] = o

    gp = jax.ShapeDtypeStruct(a_re.shape, F32)
    cgp = jax.ShapeDtypeStruct(bt_re.shape, F32)
    return pl.pallas_call(body, name=name, out_shape=[gp, gp, cgp, cgp])(a_re, a_im, log_dt, bt_re, bt_im)


def _s5_params_bwd(a_re, a_im, log_dt, bt_re, bt_im, dlr, dli, dbbr, dbbi, name):
    def body(ar, ai, ld, br, bi, g0, g1, g2, g3, o0, o1, o2, o3, o4):
        _, vjp = jax.vjp(_s5_param_fn, ar[...], ai[...], ld[...], br[...], bi[...])
        o0[...], o1[...], o2[...], o3[...], o4[...] = vjp((g0[...], g1[...], g2[...], g3[...]))

    gp = jax.ShapeDtypeStruct(a_re.shape, F32)
    cgp = jax.ShapeDtypeStruct(bt_re.shape, F32)
    return pl.pallas_call(body, name=name,
                          out_shape=[gp, gp, jax.ShapeDtypeStruct(log_dt.shape, F32), cgp, cgp])(
        a_re, a_im, log_dt, bt_re, bt_im, dlr, dli, dbbr, dbbi)


def _cmul(ar, ai, br, bi):
    return ar * br - ai * bi, ar * bi + ai * br


def _cpow(lr, li, n):
    rr, ri = None, None
    br, bi = lr, li
    while n:
        if n & 1:
            rr, ri = (br, bi) if rr is None else _cmul(rr, ri, br, bi)
        n >>= 1
        if n:
            br, bi = _cmul(br, bi, br, bi)
    return rr, ri


def _shift_rows(x, up):
    row = lax.broadcasted_iota(jnp.int32, x.shape, 0)
    if up:
        return jnp.where(row == SUB - 1, 0.0, pltpu.roll(x, SUB - 1, 0))
    return jnp.where(row == 0, 0.0, pltpu.roll(x, 1, 0))


NT = SLAB_ST // 128


def _lam_tiles(lr_ref, li_ref):
    return [(lr_ref[:, j * 128:(j + 1) * 128], li_ref[:, j * 128:(j + 1) * 128]) for j in range(NT)]


def _row_on_sublanes(ref, j, t):
    return ref[j, pl.ds(t, SUB, stride=0), :]


def _pow_table(pw_re, pw_im, lam_t, seg):
    assert seg % 8 == 0 and (seg // 8) & (seg // 8 - 1) == 0
    for j in range(NT):
        lr, li = lam_t[j][0][0:1], lam_t[j][1][0:1]
        r, i_ = lr, li
        for row in range(8):
            pw_re[j, row:row + 1, :] = r
            pw_im[j, row:row + 1, :] = i_
            if row < 7:
                r, i_ = _cmul(r, i_, lr, li)
        n = 8
        while n < seg:
            qr, qi = _cpow(lr, li, n)
            nr, ni = _cmul(pw_re[j, 0:n, :], pw_im[j, 0:n, :], qr, qi)
            pw_re[j, n:2 * n, :] = nr
            pw_im[j, n:2 * n, :] = ni
            n *= 2


def _seg_scan(s_re, s_im, lam_t, pw_re, pw_im, seg, reverse, prev=None):
    sgn = -1.0 if reverse else 1.0
    lt = [(lr, sgn * li) for lr, li in lam_t]
    tiles = [(g, j) for g in range(N_GRP) for j in range(NT)]
    zeros = jnp.zeros((SUB, 128), F32)

    def rows(g, i):
        return pl.ds(pl.multiple_of((g * seg + i) * SUB, SUB), SUB)

    def step1(t, carry):
        i = seg - 1 - t if reverse else t
        out = []
        for n, (g, j) in enumerate(tiles):
            nr, ni = _cmul(lt[j][0], lt[j][1], carry[2 * n], carry[2 * n + 1])
            nr = nr + s_re[j, rows(g, i), :]
            ni = ni + s_im[j, rows(g, i), :]
            s_re[j, rows(g, i), :] = nr
            s_im[j, rows(g, i), :] = ni
            out += [nr, ni]
        return tuple(out)

    zero = tuple(zeros for _ in range(2 * len(tiles)))
    ends = lax.fori_loop(0, seg, step1, zero)

    carries = [None] * (2 * len(tiles))
    row = lax.broadcasted_iota(jnp.int32, (SUB, 128), 0)
    dist = (SUB - 1 - row) if reverse else row
    edge = 0 if reverse else SUB - 1
    for j in range(NT):
        pr, pi = _cpow(lt[j][0], lt[j][1], seg)
        qr, qi = jnp.ones((SUB, 128), F32), zeros
        for s in range(1, SUB):
            tr, ti = _cmul(qr, qi, pr, pi)
            qr, qi = jnp.where(dist >= s, tr, qr), jnp.where(dist >= s, ti, qi)
        boundary = None
        for g in (reversed(range(N_GRP)) if reverse else range(N_GRP)):
            n = g * NT + j
            cr, ci = zeros, zeros
            for _ in range(SUB - 1):
                tr, ti = _cmul(pr, pi, cr, ci)
                cr = _shift_rows(tr + ends[2 * n], reverse)
                ci = _shift_rows(ti + ends[2 * n + 1], reverse)
            if boundary is not None:
                tr, ti = _cmul(qr, qi, boundary[0], boundary[1])
                cr, ci = cr + tr, ci + ti
            carries[2 * n], carries[2 * n + 1] = cr, ci
            fr, fi = _cmul(pr, pi, cr, ci)
            boundary = (jnp.broadcast_to((fr + ends[2 * n])[edge:edge + 1], (SUB, 128)),
                        jnp.broadcast_to((fi + ends[2 * n + 1])[edge:edge + 1], (SUB, 128)))

    def fix(t, i, acc, before):
        out = []
        pws = [(_row_on_sublanes(pw_re, j, t), sgn * _row_on_sublanes(pw_im, j, t)) for j in range(NT)]
        for n, (g, j) in enumerate(tiles):
            ar, ai = _cmul(pws[j][0], pws[j][1], carries[2 * n], carries[2 * n + 1])
            ar = ar + s_re[j, rows(g, i), :]
            ai = ai + s_im[j, rows(g, i), :]
            s_re[j, rows(g, i), :] = ar
            s_im[j, rows(g, i), :] = ai
            if before is not None:
                qr, qi = before(n)
                out += [acc[2 * n] + ar * qr + ai * qi, acc[2 * n + 1] + ai * qr - ar * qi]
        return tuple(out)

    if prev is None:
        lax.fori_loop(0, seg, lambda t, c: fix(t, seg - 1 - t if reverse else t, c, None), ())
        return carries
    assert reverse
    p_re, p_im, p_carries = prev

    def earlier(t):
        return lambda n: (p_re[tiles[n][1], rows(tiles[n][0], seg - 2 - t), :],
                          p_im[tiles[n][1], rows(tiles[n][0], seg - 2 - t), :])

    acc = lax.fori_loop(0, seg - 1, lambda t, c: fix(t, seg - 1 - t, c, earlier(t)), zero)
    acc = fix(seg - 1, 0, acc, lambda n: (p_carries[2 * n], p_carries[2 * n + 1]))
    return carries, [sum(acc[2 * (g * NT + j) + part] for g in range(N_GRP)) for j in range(NT) for part in range(2)]


def _seg_slice(k, seg):
    g, r = divmod(k, SUB)
    return pl.ds(g * seg * SUB + r, seg, stride=SUB)


def _seg_rows(ref, k, seg):
    return jnp.concatenate([ref[j, _seg_slice(k, seg), :] for j in range(NT)], axis=-1)


def _seg_store(ref, k, seg, val):
    for j in range(NT):
        ref[j, _seg_slice(k, seg), :] = val[:, j * 128:(j + 1) * 128]


def _s5_specs(L):
    col = lambda off: pl.BlockSpec((L, SLAB_CH), lambda j: (0, off + j))
    mat_b = pl.BlockSpec((None, SLAB_CH, SLAB_ST), lambda j: (j, 0, 0))
    mat_c = pl.BlockSpec((None, SLAB_ST, SLAB_CH), lambda j: (j, 0, 0))
    vec_s = pl.BlockSpec((None, SUB, SLAB_ST), lambda j: (j, 0, 0))
    vec_c = pl.BlockSpec((None, 1, SLAB_CH), lambda j: (j, 0, 0))
    return col, mat_b, mat_c, vec_s, vec_c


def _s5_states(u_ref, bre_ref, bim_ref, lam_t, pw_re, pw_im, s_re, s_im, seg):
    _pow_table(pw_re, pw_im, lam_t, seg)
    for k in range(N_SEG):
        uk = u_ref[pl.ds(k * seg, seg), :]
        _seg_store(s_re, k, seg, _dot(uk, bre_ref[...]))
        _seg_store(s_im, k, seg, _dot(uk, bim_ref[...]))
    return _seg_scan(s_re, s_im, lam_t, pw_re, pw_im, seg, reverse=False)


def _s5_fwd(proj, bre, bim, cre_t, cim_t, lam_re, lam_im, dvec, name):
    L = proj.shape[0]
    seg = L // N_SEG
    col, mat_b, mat_c, vec_s, vec_c = _s5_specs(L)
    rows = N_SEG * seg

    def body(u_ref, bre_ref, bim_ref, cre_ref, cim_ref, lr_ref, li_ref, d_ref, y_ref, s_re, s_im, pw_re, pw_im):
        _s5_states(u_ref, bre_ref, bim_ref, _lam_tiles(lr_ref, li_ref), pw_re, pw_im, s_re, s_im, seg)
        for k in range(N_SEG):
            y = (_dot(_seg_rows(s_re, k, seg).astype(BF16), cre_ref[...])
                 - _dot(_seg_rows(s_im, k, seg).astype(BF16), cim_ref[...]))
            y = y + d_ref[...] * u_ref[pl.ds(k * seg, seg), :].astype(F32)
            y_ref[pl.ds(k * seg, seg), :] = jax.nn.gelu(y).astype(BF16)

    return pl.pallas_call(
        body, name=name, grid=(N_SLAB,),
        in_specs=[col(OFF_UA // SLAB_CH), mat_b, mat_b, mat_c, mat_c, vec_s, vec_s, vec_c],
        out_specs=pl.BlockSpec((L, SLAB_CH), lambda j: (0, j)),
        out_shape=jax.ShapeDtypeStruct((L, SSM_WIDTH), BF16),
        scratch_shapes=[pltpu.VMEM((NT, rows, 128), F32)] * 2 + [pltpu.VMEM((NT, seg, 128), F32)] * 2,
        compiler_params=_cp(("parallel",)),
    )(proj, bre, bim, cre_t, cim_t, lam_re, lam_im, dvec)


def _s5_bwd(proj, dy, bre, bim, cre_t, cim_t, lam_re, lam_im, dvec, name):
    L = proj.shape[0]
    seg = L // N_SEG
    col, mat_b, mat_c, vec_s, vec_c = _s5_specs(L)
    rows = N_SEG * seg
    dlam_spec = pl.BlockSpec((None, 1, SLAB_ST), lambda j: (j, 0, 0))

    def body(u_ref, dy_ref, bre_ref, bim_ref, cre_ref, cim_ref, lr_ref, li_ref, d_ref,
             du_ref, dbre_ref, dbim_ref, dcre_ref, dcim_ref, dlr_ref, dli_ref, dd_ref,
             s_re, s_im, a_re, a_im, pw_re, pw_im, dyp):
        lam_t = _lam_tiles(lr_ref, li_ref)
        carry_s = _s5_states(u_ref, bre_ref, bim_ref, lam_t, pw_re, pw_im, s_re, s_im, seg)
        dcre = jnp.zeros((SLAB_ST, SLAB_CH), F32)
        dcim = jnp.zeros((SLAB_ST, SLAB_CH), F32)
        dd = jnp.zeros((1, SLAB_CH), F32)
        for k in range(N_SEG):
            sre = _seg_rows(s_re, k, seg).astype(BF16)
            sim = _seg_rows(s_im, k, seg).astype(BF16)
            uk = u_ref[pl.ds(k * seg, seg), :].astype(F32)
            ypre = _dot(sre, cre_ref[...]) - _dot(sim, cim_ref[...]) + d_ref[...] * uk
            _, vjp = jax.vjp(jax.nn.gelu, ypre)
            (dyk,) = vjp(dy_ref[pl.ds(k * seg, seg), :].astype(F32))
            dyp[pl.ds(k * seg, seg), :] = dyk
            dd = dd + jnp.sum(dyk * uk, axis=0, keepdims=True)
            dyb = dyk.astype(BF16)
            dcre = dcre + _dot_tn(sre, dyb)
            dcim = dcim - _dot_tn(sim, dyb)
            _seg_store(a_re, k, seg, _dot_nt(dyb, cre_ref[...]))
            _seg_store(a_im, k, seg, -_dot_nt(dyb, cim_ref[...]))
        dcre_ref[...] = dcre
        dcim_ref[...] = dcim
        dd_ref[...] = dd

        _, acc = _seg_scan(a_re, a_im, lam_t, pw_re, pw_im, seg, reverse=True, prev=(s_re, s_im, carry_s))
        dlr_ref[...] = jnp.concatenate([jnp.sum(acc[2 * j], axis=0, keepdims=True) for j in range(NT)], axis=-1)
        dli_ref[...] = jnp.concatenate([jnp.sum(acc[2 * j + 1], axis=0, keepdims=True) for j in range(NT)], axis=-1)

        dbre = jnp.zeros((SLAB_CH, SLAB_ST), F32)
        dbim = jnp.zeros((SLAB_CH, SLAB_ST), F32)
        for k in range(N_SEG):
            are = _seg_rows(a_re, k, seg).astype(BF16)
            aim = _seg_rows(a_im, k, seg).astype(BF16)
            uk = u_ref[pl.ds(k * seg, seg), :]
            du = _dot_nt(are, bre_ref[...]) + _dot_nt(aim, bim_ref[...]) + dyp[pl.ds(k * seg, seg), :] * d_ref[...]
            du_ref[pl.ds(k * seg, seg), :] = du.astype(BF16)
            dbre = dbre + _dot_tn(uk, are)
            dbim = dbim + _dot_tn(uk, aim)
        dbre_ref[...] = dbre
        dbim_ref[...] = dbim

    scan_buf = pltpu.VMEM((NT, rows, 128), F32)
    pow_buf = pltpu.VMEM((NT, seg, 128), F32)
    return pl.pallas_call(
        body, name=name, grid=(N_SLAB,),
        in_specs=[col(OFF_UA // SLAB_CH), pl.BlockSpec((L, SLAB_CH), lambda j: (0, j)),
                  mat_b, mat_b, mat_c, mat_c, vec_s, vec_s, vec_c],
        out_specs=[pl.BlockSpec((L, SLAB_CH), lambda j: (0, j)), mat_b, mat_b, mat_c, mat_c, dlam_spec, dlam_spec, vec_c],
        out_shape=[jax.ShapeDtypeStruct((L, SSM_WIDTH), BF16),
                   jax.ShapeDtypeStruct((N_SLAB, SLAB_CH, SLAB_ST), F32),
                   jax.ShapeDtypeStruct((N_SLAB, SLAB_CH, SLAB_ST), F32),
                   jax.ShapeDtypeStruct((N_SLAB, SLAB_ST, SLAB_CH), F32),
                   jax.ShapeDtypeStruct((N_SLAB, SLAB_ST, SLAB_CH), F32),
                   jax.ShapeDtypeStruct((N_SLAB, 1, SLAB_ST), F32),
                   jax.ShapeDtypeStruct((N_SLAB, 1, SLAB_ST), F32),
                   jax.ShapeDtypeStruct((N_SLAB, 1, SLAB_CH), F32)],
        scratch_shapes=[scan_buf, scan_buf, scan_buf, scan_buf, pow_buf, pow_buf, pltpu.VMEM((L, SLAB_CH), F32)],
        compiler_params=_cp(("parallel",)),
    )(proj, dy, bre, bim, cre_t, cim_t, lam_re, lam_im, dvec)


def _glu_point(y0, pre, za, b):
    return y0 * jax.nn.sigmoid(pre + b) * jax.nn.silu(za)


def _glu_specs(L, tm):
    row = pl.BlockSpec((tm, SSM_WIDTH), lambda i: (i, 0))
    za = pl.BlockSpec((tm, SSM_WIDTH), lambda i: (i, OFF_ZA // SSM_WIDTH))
    wmat = pl.BlockSpec((SSM_WIDTH, SSM_WIDTH), lambda i: (0, 0))
    vec = pl.BlockSpec((1, SSM_WIDTH), lambda i: (0, 0))
    return row, za, wmat, vec


def _glu_fwd(ya0, proj, w, b, name):
    L = ya0.shape[0]
    tm = min(L, 512)
    row, za, wmat, vec = _glu_specs(L, tm)

    def body(y_ref, z_ref, w_ref, b_ref, o_ref):
        y0 = y_ref[...]
        pre = _dot(y0, w_ref[...])
        o_ref[...] = _glu_point(y0.astype(F32), pre, z_ref[...].astype(F32), b_ref[...]).astype(BF16)

    return pl.pallas_call(
        body, name=name, grid=(L // tm,), in_specs=[row, za, wmat, vec], out_specs=row,
        out_shape=jax.ShapeDtypeStruct((L, SSM_WIDTH), BF16), compiler_params=_cp(("parallel",)),
    )(ya0, proj, w, b)


def _glu_bwd(ya0, proj, w, b, dya, name):
    L = ya0.shape[0]
    tm = min(L, 512)
    row, za, wmat, vec = _glu_specs(L, tm)

    def body(y_ref, z_ref, w_ref, b_ref, g_ref, dy0_ref, dza_ref, dw_ref, db_ref):
        y0 = y_ref[...]
        pre = _dot(y0, w_ref[...])
        _, vjp = jax.vjp(_glu_point, y0.astype(F32), pre, z_ref[...].astype(F32), b_ref[...])
        dy0, dpre, dza, db = vjp(g_ref[...].astype(F32))
        dpb = dpre.astype(BF16)
        dy0_ref[...] = (dy0 + _dot_nt(dpb, w_ref[...])).astype(BF16)
        dza_ref[...] = dza.astype(BF16)

        @pl.when(pl.program_id(0) == 0)
        def _():
            dw_ref[...] = jnp.zeros_like(dw_ref)
            db_ref[...] = jnp.zeros_like(db_ref)

        dw_ref[...] += _dot_tn(y0, dpb)
        db_ref[...] += db

    return pl.pallas_call(
        body, name=name, grid=(L // tm,), in_specs=[row, za, wmat, vec, row],
        out_specs=[row, row, wmat, vec],
        out_shape=[jax.ShapeDtypeStruct((L, SSM_WIDTH), BF16), jax.ShapeDtypeStruct((L, SSM_WIDTH), BF16),
                   jax.ShapeDtypeStruct((SSM_WIDTH, SSM_WIDTH), F32), jax.ShapeDtypeStruct((1, SSM_WIDTH), F32)],
        compiler_params=_cp(("arbitrary",)),
    )(ya0, proj, w, b, dya)


def _sg_norm(vb, ln_w, ln_b):
    v0 = jax.nn.gelu(vb)
    mu = jnp.mean(v0, axis=-1, keepdims=True)
    var = jnp.mean(jnp.square(v0 - mu), axis=-1, keepdims=True)
    return (v0 - mu) * lax.rsqrt(var + EPS) * ln_w + ln_b


def _sg_gate(ub, mixed, zb):
    return jax.nn.gelu(ub) * mixed * jax.nn.silu(zb)


def _sg_specs():
    W = SSM_WIDTH
    blk = lambda off: pl.BlockSpec((CHUNK, W), lambda n: (n, off // W))
    out = pl.BlockSpec((CHUNK, W), lambda n: (n, 0))
    vec = pl.BlockSpec((1, W), lambda n: (0, 0))
    wsp = pl.BlockSpec((SG_HEADS, CHUNK, CHUNK), lambda n: (0, 0, 0))
    bsp = pl.BlockSpec((SG_HEADS, CHUNK, 1), lambda n: (0, 0, 0))
    return blk, out, vec, wsp, bsp


def _sg_masked(w_ref):
    t = lax.broadcasted_iota(jnp.int32, (CHUNK, CHUNK), 0)
    s = lax.broadcasted_iota(jnp.int32, (CHUNK, CHUNK), 1)
    causal = s <= t
    return causal, [jnp.where(causal, w_ref[h], 0.0).astype(BF16) for h in range(SG_HEADS)]


def _sg_mix(wm, vnb, bias_ref):
    return jnp.concatenate(
        [_dot(wm[h], vnb[:, h * CHUNK:(h + 1) * CHUNK]) + bias_ref[h] for h in range(SG_HEADS)], axis=-1)


def _sg_fwd(proj, ln_w, ln_b, w, bias, name):
    L = proj.shape[0]
    blk, out, vec, wsp, bsp = _sg_specs()

    def body(ub_ref, vb_ref, zb_ref, lw_ref, lb_ref, w_ref, bias_ref, o_ref):
        _, wm = _sg_masked(w_ref)
        vnb = _sg_norm(vb_ref[...].astype(F32), lw_ref[...], lb_ref[...]).astype(BF16)
        mixed = _sg_mix(wm, vnb, bias_ref)
        o_ref[...] = _sg_gate(ub_ref[...].astype(F32), mixed, zb_ref[...].astype(F32)).astype(BF16)

    return pl.pallas_call(
        body, name=name, grid=(L // CHUNK,),
        in_specs=[blk(OFF_UB), blk(OFF_VB), blk(OFF_ZB), vec, vec, wsp, bsp], out_specs=out,
        out_shape=jax.ShapeDtypeStruct((L, SSM_WIDTH), BF16), compiler_params=_cp(("parallel",)),
    )(proj, proj, proj, ln_w, ln_b, w, bias)


def _sg_bwd(proj, ln_w, ln_b, w, bias, dyb, name):
    L = proj.shape[0]
    blk, out, vec, wsp, bsp = _sg_specs()

    def body(ub_ref, vb_ref, zb_ref, lw_ref, lb_ref, w_ref, bias_ref, g_ref,
             dub_ref, dvb_ref, dzb_ref, dlw_ref, dlb_ref, dw_ref, dbias_ref):
        causal, wm = _sg_masked(w_ref)
        vb = vb_ref[...].astype(F32)
        vn, vjp_norm = jax.vjp(_sg_norm, vb, lw_ref[...], lb_ref[...])
        vnb = vn.astype(BF16)
        mixed = _sg_mix(wm, vnb, bias_ref)
        _, vjp_gate = jax.vjp(_sg_gate, ub_ref[...].astype(F32), mixed, zb_ref[...].astype(F32))
        dub, dmixed, dzb = vjp_gate(g_ref[...].astype(F32))
        dub_ref[...] = dub.astype(BF16)
        dzb_ref[...] = dzb.astype(BF16)

        @pl.when(pl.program_id(0) == 0)
        def _():
            dlw_ref[...] = jnp.zeros_like(dlw_ref)
            dlb_ref[...] = jnp.zeros_like(dlb_ref)
            dw_ref[...] = jnp.zeros_like(dw_ref)
            dbias_ref[...] = jnp.zeros_like(dbias_ref)

        dvn = []
        for h in range(SG_HEADS):
            dm = dmixed[:, h * CHUNK:(h + 1) * CHUNK]
            dmb = dm.astype(BF16)
            dbias_ref[h] += jnp.sum(dm, axis=-1, keepdims=True)
            dw_ref[h] += jnp.where(causal, _dot_nt(dmb, vnb[:, h * CHUNK:(h + 1) * CHUNK]), 0.0)
            dvn.append(_dot_tn(wm[h], dmb))
        dvb, dlw, dlb = vjp_norm(jnp.concatenate(dvn, axis=-1))
        dvb_ref[...] = dvb.astype(BF16)
        dlw_ref[...] += dlw
        dlb_ref[...] += dlb

    act = jax.ShapeDtypeStruct((L, SSM_WIDTH), BF16)
    return pl.pallas_call(
        body, name=name, grid=(L // CHUNK,),
        in_specs=[blk(OFF_UB), blk(OFF_VB), blk(OFF_ZB), vec, vec, wsp, bsp, out],
        out_specs=[out, out, out, vec, vec, wsp, bsp],
        out_shape=[act, act, act, jax.ShapeDtypeStruct((1, SSM_WIDTH), F32), jax.ShapeDtypeStruct((1, SSM_WIDTH), F32),
                   jax.ShapeDtypeStruct((SG_HEADS, CHUNK, CHUNK), F32), jax.ShapeDtypeStruct((SG_HEADS, CHUNK, 1), F32)],
        compiler_params=_cp(("arbitrary",)),
    )(proj, proj, proj, ln_w, ln_b, w, bias, dyb)


def _rope_tables(L):
    half = ROT_DIM // 2
    inv_freq = ROPE_THETA ** (-jnp.arange(0, ROT_DIM, 2, dtype=F32) / ROT_DIM)
    ang = jnp.arange(L, dtype=F32)[:, None] * inv_freq[None, :]
    cos, sin = jnp.cos(ang), jnp.sin(ang)
    ones = jnp.ones((L, HEAD_DIM - ROT_DIM), F32)
    cos_h = jnp.concatenate([cos, cos, ones], axis=-1)
    sin_h = jnp.concatenate([-sin, sin, 0.0 * ones], axis=-1)
    src = jnp.arange(HEAD_DIM)[:, None]
    dst = jnp.arange(HEAD_DIM)[None, :]
    p_h = (((dst < half) & (src == dst + half)) | ((dst >= half) & (dst < ROT_DIM) & (src == dst - half))).astype(F32)
    p2 = jnp.kron(jnp.eye(2, dtype=F32), p_h).astype(BF16)
    return jnp.tile(cos_h, (1, 2)), jnp.tile(sin_h, (1, 2)), p2


def _rope(t, cos, sin, p2):
    n = t.shape[1] // 128
    tb = t.astype(BF16)
    sw = jnp.concatenate([_dot(tb[:, i * 128:(i + 1) * 128], p2) for i in range(n)], axis=-1) if n > 1 else _dot(tb, p2)
    return t * jnp.tile(cos, (1, n)) + sw * jnp.tile(sin, (1, n))


def _rope_t(g, cos, sin, p2):
    n = g.shape[1] // 128
    gs = (g * jnp.tile(sin, (1, n))).astype(BF16)
    sw = jnp.concatenate([_dot_nt(gs[:, i * 128:(i + 1) * 128], p2) for i in range(n)], axis=-1) if n > 1 else _dot_nt(gs, p2)
    return g * jnp.tile(cos, (1, n)) + sw


def _lane_lo(shape):
    return (lax.broadcasted_iota(jnp.int32, shape, len(shape) - 1) % 128) < HEAD_DIM


def _dup_halves(x):
    xr = pltpu.roll(x, HEAD_DIM, 1)
    lo = _lane_lo(x.shape)
    return jnp.where(lo, x, xr), jnp.where(lo, xr, x)


def _fold_halves(d0, d1):
    f0 = d0 + pltpu.roll(d0, HEAD_DIM, 1)
    f1 = d1 + pltpu.roll(d1, HEAD_DIM, 1)
    return jnp.where(_lane_lo(d0.shape), f0, f1)


def _attn_mask():
    qi = lax.broadcasted_iota(jnp.int32, (CHUNK, 2 * CHUNK), 0)
    kj = lax.broadcasted_iota(jnp.int32, (CHUNK, 2 * CHUNK), 1)
    return qi, kj


def _attn_specs():
    qsp = pl.BlockSpec((CHUNK, 1024), lambda n: (n, OFF_Q // 1024))
    kv_cur = pl.BlockSpec((CHUNK, 256), lambda n: (n, OFF_KV // 256))
    kv_prev = pl.BlockSpec((CHUNK, 256), lambda n: (jnp.maximum(n - 1, 0), OFF_KV // 256))
    zsp = [pl.BlockSpec((CHUNK, 256), functools.partial(lambda n, q: (n, OFF_ZC // 256 + q), q=q)) for q in range(4)]
    tab_cur = pl.BlockSpec((CHUNK, 128), lambda n: (n, 0))
    tab_prev = pl.BlockSpec((CHUNK, 128), lambda n: (jnp.maximum(n - 1, 0), 0))
    p2sp = pl.BlockSpec((128, 128), lambda n: (0, 0))
    sink = pl.BlockSpec(memory_space=pltpu.SMEM)
    wide = pl.BlockSpec((CHUNK, 1024), lambda n: (n, 0))
    return qsp, kv_cur, kv_prev, zsp, tab_cur, tab_prev, p2sp, sink, wide


def _attn_prep(n, q_ref, kvc_ref, kvp_ref, cosc_ref, sinc_ref, cosp_ref, sinp_ref, p2_ref):
    p2 = p2_ref[...]
    qr = _rope(q_ref[...].astype(F32), cosc_ref[...], sinc_ref[...], p2).astype(BF16)
    kc = _rope(kvc_ref[:, 0:128].astype(F32), cosc_ref[...], sinc_ref[...], p2)
    kp = _rope(kvp_ref[:, 0:128].astype(F32), cosp_ref[...], sinp_ref[...], p2)
    k_all = jnp.concatenate([kp, kc], axis=0).astype(BF16)
    v_all = jnp.concatenate([kvp_ref[:, 128:256], kvc_ref[:, 128:256]], axis=0)
    qi, kj = _attn_mask()
    allowed = ((kj < CHUNK) & (kj > qi) & (n > 0)) | ((kj >= CHUNK) & (kj - CHUNK <= qi))
    return qr, _dup_halves(k_all), _dup_halves(v_all), allowed, _lane_lo((CHUNK, 128))


def _attn_head(qr, kd, sink_ref, h, allowed, lo):
    m, half, g = h // 2, h % 2, h // 8
    qp = qr[:, m * 128:(m + 1) * 128]
    qm = jnp.where(lo if half == 0 else ~lo, qp, jnp.zeros_like(qp))
    s = jnp.where(allowed, _dot_nt(qm, kd[g]) * (HEAD_DIM ** -0.5), NEG_INF)
    snk = sink_ref[h]
    mx = jnp.maximum(jnp.max(s, axis=-1, keepdims=True), snk)
    e = jnp.exp(s - mx)
    es = jnp.exp(snk - mx)
    inv = 1.0 / (jnp.sum(e, axis=-1, keepdims=True) + es)
    return qm, e * inv, es * inv


def _silu_gate(o, z):
    return o * jax.nn.silu(z)


def _pair_lanes(refs, m):
    return refs[m // 2][:, (m % 2) * 128:(m % 2 + 1) * 128]


def _attn_fwd(proj, sinks, tabs, name):
    L = proj.shape[0]
    cos2, sin2, p2 = tabs
    qsp, kv_cur, kv_prev, zsp, tab_cur, tab_prev, p2sp, sink, wide = _attn_specs()

    def body(q_ref, kvc_ref, kvp_ref, z0, z1, z2, z3, cosc, sinc, cosp, sinp, p2_ref, sink_ref, y_ref, o_ref):
        n = pl.program_id(0)
        qr, kd, vd, allowed, lo = _attn_prep(n, q_ref, kvc_ref, kvp_ref, cosc, sinc, cosp, sinp, p2_ref)
        probs = [_attn_head(qr, kd, sink_ref, h, allowed, lo)[1].astype(BF16) for h in range(ATT_HEADS)]
        for m in range(ATT_HEADS // 2):
            g = m // 4
            o0 = _dot(probs[2 * m], vd[g])
            o1 = _dot(probs[2 * m + 1], vd[g])
            o = jnp.where(lo, o0, o1).astype(BF16)
            o_ref[:, m * 128:(m + 1) * 128] = o
            z = _pair_lanes((z0, z1, z2, z3), m).astype(F32)
            y_ref[:, m * 128:(m + 1) * 128] = _silu_gate(o.astype(F32), z).astype(BF16)

    act = jax.ShapeDtypeStruct((L, 1024), BF16)
    return pl.pallas_call(
        body, name=name, grid=(L // CHUNK,),
        in_specs=[qsp, kv_cur, kv_prev, *zsp, tab_cur, tab_cur, tab_prev, tab_prev, p2sp, sink],
        out_specs=[wide, wide], out_shape=[act, act], compiler_params=_cp(("parallel",)),
    )(proj, proj, proj, proj, proj, proj, proj, cos2, sin2, cos2, sin2, p2, sinks)


def _attn_bwd(proj, sinks, tabs, o_att, dyc, name):
    L = proj.shape[0]
    cos2, sin2, p2 = tabs
    qsp, kv_cur, kv_prev, zsp, tab_cur, tab_prev, p2sp, sink, wide = _attn_specs()
    kvo = pl.BlockSpec((CHUNK, 256), lambda n: (n, 0))

    def body(q_ref, kvc_ref, kvp_ref, z0, z1, z2, z3, cosc, sinc, cosp, sinp, p2_ref, sink_ref, o_ref, g_ref,
             dq_ref, dz_ref, dkvc_ref, dkvp_ref, dsink_ref):
        n = pl.program_id(0)
        qr, kd, vd, allowed, lo = _attn_prep(n, q_ref, kvc_ref, kvp_ref, cosc, sinc, cosp, sinp, p2_ref)
        p2 = p2_ref[...]

        @pl.when(n == 0)
        def _():
            dsink_ref[...] = jnp.zeros_like(dsink_ref)

        dkd = [jnp.zeros((2 * CHUNK, 128), F32), jnp.zeros((2 * CHUNK, 128), F32)]
        dvd = [jnp.zeros((2 * CHUNK, 128), F32), jnp.zeros((2 * CHUNK, 128), F32)]
        probs = [_attn_head(qr, kd, sink_ref, h, allowed, lo) for h in range(ATT_HEADS)]
        for m in range(ATT_HEADS // 2):
            g = m // 4
            lanes = slice(m * 128, (m + 1) * 128)
            z = _pair_lanes((z0, z1, z2, z3), m).astype(F32)
            _, vjp = jax.vjp(_silu_gate, o_ref[:, lanes].astype(F32), z)
            do, dz = vjp(g_ref[:, lanes].astype(F32))
            dz_ref[:, lanes] = dz.astype(BF16)
            dop = do.astype(BF16)
            dq_h = []
            for half in range(2):
                h = 2 * m + half
                qm, p, ps = probs[h]
                dom = jnp.where(lo if half == 0 else ~lo, dop, jnp.zeros_like(dop))
                dp = _dot_nt(dom, vd[g])
                rs = jnp.sum(p * dp, axis=-1, keepdims=True)
                ds = (p * (dp - rs) * (HEAD_DIM ** -0.5)).astype(BF16)
                dsink_ref[h:h + 1, :] += jnp.broadcast_to(jnp.sum(-ps * rs, axis=0, keepdims=True), (1, 128))
                dq_h.append(_dot(ds, kd[g]))
                dkd[g] = dkd[g] + _dot_tn(ds, qm)
                dvd[g] = dvd[g] + _dot_tn(p.astype(BF16), dom)
            dq_ref[:, lanes] = _rope_t(jnp.where(lo, dq_h[0], dq_h[1]), cosc[...], sinc[...], p2).astype(BF16)
        dk_rot = _fold_halves(dkd[0], dkd[1])
        dv = _fold_halves(dvd[0], dvd[1])
        dkp = _rope_t(dk_rot[0:CHUNK], cosp[...], sinp[...], p2)
        dkc = _rope_t(dk_rot[CHUNK:2 * CHUNK], cosc[...], sinc[...], p2)
        dkvp_ref[...] = jnp.concatenate([dkp, dv[0:CHUNK]], axis=-1)
        dkvc_ref[...] = jnp.concatenate([dkc, dv[CHUNK:2 * CHUNK]], axis=-1)

    act = jax.ShapeDtypeStruct((L, 1024), BF16)
    kvs = jax.ShapeDtypeStruct((L, 256), F32)
    return pl.pallas_call(
        body, name=name, grid=(L // CHUNK,),
        in_specs=[qsp, kv_cur, kv_prev, *zsp, tab_cur, tab_cur, tab_prev, tab_prev, p2sp, sink, wide, wide],
        out_specs=[wide, wide, kvo, kvo, pl.BlockSpec((ATT_HEADS, 128), lambda n: (0, 0))],
        out_shape=[act, act, kvs, kvs, jax.ShapeDtypeStruct((ATT_HEADS, 128), F32)],
        compiler_params=_cp(("arbitrary",)),
    )(proj, proj, proj, proj, proj, proj, proj, cos2, sin2, cos2, sin2, p2, sinks, o_att, dyc)


MERGE_TN = 256


def _merge_point(ta, tb, tc, ga, gb, gc):
    return jax.nn.sigmoid(ga) * ta + jax.nn.sigmoid(gb) * tb + jax.nn.sigmoid(gc) * tc


def _merge_specs(tm):
    nj = D_MODEL // MERGE_TN
    t = pl.BlockSpec((tm, MERGE_TN), lambda i, j: (i, j))
    gates = [pl.BlockSpec((tm, MERGE_TN), functools.partial(lambda i, j, b: (i, OFF_G // MERGE_TN + b * nj + j), b=b))
             for b in range(3)]
    return t, gates, nj


def _merge_fwd(ta, tb, tc, proj, name):
    L = ta.shape[0]
    tm = min(L, 1024)
    t, gates, nj = _merge_specs(tm)

    def body(ta_ref, tb_ref, tc_ref, ga_ref, gb_ref, gc_ref, o_ref):
        f = lambda r: r[...].astype(F32)
        o_ref[...] = _merge_point(f(ta_ref), f(tb_ref), f(tc_ref), f(ga_ref), f(gb_ref), f(gc_ref)).astype(BF16)

    return pl.pallas_call(
        body, name=name, grid=(L // tm, nj), in_specs=[t, t, t, *gates], out_specs=t,
        out_shape=jax.ShapeDtypeStruct((L, D_MODEL), BF16), compiler_params=_cp(("parallel", "parallel")),
    )(ta, tb, tc, proj, proj, proj)


def _merge_bwd(ta, tb, tc, proj, dm, name):
    L = ta.shape[0]
    tm = min(L, 1024)
    t, gates, nj = _merge_specs(tm)

    def body(ta_ref, tb_ref, tc_ref, ga_ref, gb_ref, gc_ref, dm_ref, dta_ref, dtb_ref, dtc_ref, dga_ref, dgb_ref, dgc_ref):
        f = lambda r: r[...].astype(F32)
        _, vjp = jax.vjp(_merge_point, f(ta_ref), f(tb_ref), f(tc_ref), f(ga_ref), f(gb_ref), f(gc_ref))
        outs = vjp(f(dm_ref))
        for r, v in zip((dta_ref, dtb_ref, dtc_ref, dga_ref, dgb_ref, dgc_ref), outs):
            r[...] = v.astype(BF16)

    act = jax.ShapeDtypeStruct((L, D_MODEL), BF16)
    return pl.pallas_call(
        body, name=name, grid=(L // tm, nj), in_specs=[t, t, t, *gates, t],
        out_specs=[t] * 6, out_shape=[act] * 6,
        compiler_params=_cp(("parallel", "parallel")),
    )(ta, tb, tc, proj, proj, proj, dm)


GRAD_DT = BF16
SMALL = ("norm_w", "ssm_a_re", "ssm_a_im", "ssm_log_dt", "ssm_b_re", "ssm_b_im", "ssm_c_re", "ssm_c_im", "ssm_d",
         "ssm_glu_b", "sg_ln_w", "sg_ln_b", "sg_w", "sg_b", "attn_sinks")
G8 = SSM_GROUPS // N_SLAB


def _diag_mask(rows_per_group, cols_per_group):
    r = jnp.arange(G8 * rows_per_group)[:, None] // rows_per_group
    c = jnp.arange(G8 * cols_per_group)[None, :] // cols_per_group
    return r == c


def _slab_b(bb_t):
    x = bb_t.transpose(1, 0, 2).reshape(N_SLAB, SLAB_CH, SSM_STATE)
    return jnp.where(_diag_mask(SSM_GROUP, SSM_STATE), jnp.tile(x, (1, 1, G8)), 0)


def _unslab_b(d):
    x = jnp.where(_diag_mask(SSM_GROUP, SSM_STATE), d, 0).reshape(N_SLAB, SLAB_CH, G8, SSM_STATE).sum(axis=2)
    return x.reshape(SSM_GROUPS, SSM_GROUP, SSM_STATE).transpose(1, 0, 2)


def _slab_c(c):
    x = c.transpose(0, 2, 1).reshape(N_SLAB, SLAB_ST, SSM_GROUP)
    return jnp.where(_diag_mask(SSM_STATE, SSM_GROUP), jnp.tile(x, (1, 1, G8)), 0)


def _unslab_c(d):
    x = jnp.where(_diag_mask(SSM_STATE, SSM_GROUP), d, 0).reshape(N_SLAB, SLAB_ST, G8, SSM_GROUP).sum(axis=2)
    return x.reshape(SSM_GROUPS, SSM_STATE, SSM_GROUP).transpose(0, 2, 1)


def _s5_prep(p, tag):
    bt_re = p["ssm_b_re"].transpose(2, 0, 1)
    bt_im = p["ssm_b_im"].transpose(2, 0, 1)
    raw = (p["ssm_a_re"], p["ssm_a_im"], p["ssm_log_dt"][:, None], bt_re, bt_im)
    lr, li, bbr, bbi = _s5_params_fwd(*raw, name=f"s5_params_{tag}")
    ops = (_slab_b(bbr).astype(BF16), _slab_b(bbi).astype(BF16),
           _slab_c(p["ssm_c_re"]).astype(BF16), _slab_c(p["ssm_c_im"]).astype(BF16),
           jnp.broadcast_to(lr.reshape(N_SLAB, 1, SLAB_ST), (N_SLAB, SUB, SLAB_ST)),
           jnp.broadcast_to(li.reshape(N_SLAB, 1, SLAB_ST), (N_SLAB, SUB, SLAB_ST)),
           p["ssm_d"].reshape(N_SLAB, 1, SLAB_CH))
    return raw, ops


def _layer_fwd(x, p, w, tabs, tag, s5=None, proj_of=None, after_proj=None):
    L = x.shape[0]
    h = _rms_fwd(x, p["norm_w"][None], f"rms_fwd_{tag}")
    if proj_of is not None:
        proj = proj_of(h)
    else:
        proj = _mm(h, w["win_t"], "nt", BF16, L, PROJ_TN, D_MODEL, f"in_proj_{tag}")
    if after_proj is not None:
        w = after_proj(proj)
    s5_raw, s5_ops = s5 if s5 is not None else _s5_prep(p, tag)
    ya0 = _s5_fwd(proj, *s5_ops, name=f"s5_fwd_{tag}")
    ya = _glu_fwd(ya0, proj, w["glu"], p["ssm_glu_b"][None], f"glu_fwd_{tag}")
    yb = _sg_fwd(proj, p["sg_ln_w"][None], p["sg_ln_b"][None], p["sg_w"], p["sg_b"][:, :, None], f"sg_fwd_{tag}")
    yc, o_att = _attn_fwd(proj, p["attn_sinks"], tabs, f"attn_fwd_{tag}")
    ta = _mm(ya, w["wba_t"], "nt", BF16, 1024, 1024, 1024, f"branch_a_{tag}")
    tb = _mm(yb, w["wbb_t"], "nt", BF16, 1024, 1024, 1024, f"branch_b_{tag}")
    tc = _mm(yc, w["wbc_t"], "nt", BF16, 1024, 1024, 1024, f"branch_c_{tag}")
    merged = _merge_fwd(ta, tb, tc, proj, f"merge_fwd_{tag}")
    x_new = _mm(merged, w["wout"], "nn", F32, 1024, 512, D_MODEL, f"out_proj_{tag}", res=x)
    saved = dict(x=x, h=h, proj=proj, s5_raw=s5_raw, s5_ops=s5_ops, ya0=ya0, ya=ya, yb=yb, yc=yc, o_att=o_att,
                 ta=ta, tb=tb, tc=tc, merged=merged)
    return x_new, saved


def _layer_bwd(dx_out, p, w, tabs, s, tag, first_after=None, after_merge=None, before_win=None, after_win=None):
    L = dx_out.shape[0]
    proj = s["proj"]
    big, small = {}, {}
    dmerged = _mm(dx_out, w["wout"], "nt", BF16, 1024, 512, D_MODEL, f"d_merged_{tag}", after=first_after)
    big["wout"] = _mm(s["merged"], dx_out, "tn", GRAD_DT, 512, 1024, L, f"d_wout_{tag}")
    dta, dtb, dtc, dga, dgb, dgc = _merge_bwd(s["ta"], s["tb"], s["tc"], proj, dmerged, f"merge_bwd_{tag}")
    tok = after_merge(dga) if after_merge is not None else None
    dy = {}
    for br, dt in (("a", dta), ("b", dtb), ("c", dtc)):
        dy[br] = _mm(dt, w[f"wb{br}_t"], "nn", BF16, 1024, 1024, D_MODEL, f"d_y{br}_{tag}", after=tok)
        big[f"wb{br}_t"] = _mm(dt, s[f"y{br}"], "tn", GRAD_DT, 512, 1024, L, f"d_wb{br}_{tag}")

    dq, dzc, dkvc, dkvp, dsink = _attn_bwd(proj, p["attn_sinks"], tabs, s["o_att"], dy["c"], f"attn_bwd_{tag}")
    dkv = dkvc + jnp.concatenate([dkvp[CHUNK:], jnp.zeros((CHUNK, 256), F32)], axis=0)
    small["attn_sinks"] = dsink[:, 0]

    dub, dvb, dzb, dlw, dlb, dsgw, dsgb = _sg_bwd(
        proj, p["sg_ln_w"][None], p["sg_ln_b"][None], p["sg_w"], p["sg_b"][:, :, None], dy["b"], f"sg_bwd_{tag}")
    small.update(sg_ln_w=dlw[0], sg_ln_b=dlb[0], sg_w=dsgw, sg_b=dsgb[:, :, 0])

    dya0, dza, dglu, dglub = _glu_bwd(s["ya0"], proj, w["glu"], p["ssm_glu_b"][None], dy["a"], f"glu_bwd_{tag}")
    big["glu"] = dglu.astype(GRAD_DT)
    small["ssm_glu_b"] = dglub[0]

    dua, dbre, dbim, dcre, dcim, dlr, dli, dd = _s5_bwd(proj, dya0, *s["s5_ops"], name=f"s5_bwd_{tag}")
    da_re, da_im, dlog_dt, dbt_re, dbt_im = _s5_params_bwd(
        *s["s5_raw"], dlr.reshape(SSM_GROUPS, SSM_STATE), dli.reshape(SSM_GROUPS, SSM_STATE),
        _unslab_b(dbre), _unslab_b(dbim), name=f"s5_params_bwd_{tag}")
    small.update(ssm_a_re=da_re, ssm_a_im=da_im, ssm_log_dt=dlog_dt[:, 0],
                 ssm_bt_re=dbt_re, ssm_bt_im=dbt_im,
                 ssm_c_re=_unslab_c(dcre), ssm_c_im=_unslab_c(dcim), ssm_d=dd.reshape(SSM_WIDTH))

    dproj = jnp.concatenate([dua, dza, dub, dvb, dzb, dq, dkv.astype(BF16), dzc, dga, dgb, dgc], axis=-1)
    tok = before_win(big) if before_win is not None else None
    big["win_t"] = _mm(dproj, s["h"], "tn", GRAD_DT, 256, D_MODEL, L, f"d_win_{tag}", after=tok)
    tok = after_win(big) if after_win is not None else None
    dh = _mm(dproj, w["win_t"], "nn", F32, L, D_MODEL, 256, f"d_h_{tag}", after=tok)
    dx_in, dnw = _rms_bwd(s["x"], p["norm_w"][None], dh, dx_out, f"rms_bwd_{tag}")
    small["norm_w"] = dnw[0]
    return dx_in, big, small


def _local_step(x, tgt, small_p, final_w, big_w):
    L = x.shape[0]
    tabs = _rope_tables(L)
    saved = []
    for l in range(DEPTH):
        x, s = _layer_fwd(x, small_p[l], big_w[l], tabs, f"l{l}")
        saved.append(s)
    loss_acc, dx, dfw = _final(x, final_w[None], tgt, "final_norm_loss")
    big_g, small_g = [None] * DEPTH, [None] * DEPTH
    for l in reversed(range(DEPTH)):
        dx, big_g[l], small_g[l] = _layer_bwd(dx, small_p[l], big_w[l], tabs, saved[l], f"l{l}")
    return loss_acc[0, 0], dx, dfw[0], big_g, small_g


MESH = pl.DeviceIdType.MESH
ANY = pl.BlockSpec(memory_space=pl.ANY)
ROW_ALIGN = 16


def _place():
    return lax.axis_index("x"), lax.axis_index("y"), lax.axis_index("c")


HBM = pl.BlockSpec(memory_space=pltpu.HBM)
SEM = pl.BlockSpec(memory_space=pltpu.SEMAPHORE)
EFFECT = pltpu.SideEffectType.DATAFLOW_SIDE_EFFECTING


def _split_start(srcs, lands, n_copies, copies, name, after=None):
    n, m, k = len(srcs), len(lands), n_copies
    extra = [] if after is None else [after]

    def body(*refs):
        src_refs, land_refs = refs[:n], refs[n:n + m]
        sems = refs[n + m + len(extra):]
        send_sems, recv_sems, token = sems[:k], sems[k:2 * k], refs[-1]
        for cp in copies(src_refs, land_refs, send_sems, recv_sems):
            cp.start()
        token[...] = jnp.zeros_like(token)

    ops = list(srcs) + list(lands)
    outs = pl.pallas_call(
        body, name=name,
        out_shape=(*[pltpu.SemaphoreType.DMA(())] * (2 * k),
                   *[pltpu.HBM(a.shape, a.dtype) for a in ops], jax.ShapeDtypeStruct((8, 128), F32)),
        in_specs=[HBM] * (n + m) + [ANY] * len(extra),
        out_specs=(*[SEM] * (2 * k), *[HBM] * (n + m), pl.BlockSpec(memory_space=pltpu.VMEM)),
        input_output_aliases={i: 2 * k + i for i in range(n + m)},
        compiler_params=pltpu.CompilerParams(has_side_effects=EFFECT),
    )(*[pltpu.with_memory_space_constraint(a, pltpu.HBM) for a in ops], *extra)
    return (list(outs[:k]), list(outs[k:2 * k]), list(outs[2 * k:2 * k + n]), list(outs[2 * k + n:2 * k + n + m]),
            outs[-1])


def _split_wait(send_sems, recv_sems, srcs, lands, after, copies, name):
    n, m, k = len(srcs), len(lands), len(send_sems)
    after = list(after) if isinstance(after, (list, tuple)) else [after]

    def body(*refs):
        src_refs, land_refs = refs[:n], refs[n:n + m]
        for cp in copies(src_refs, land_refs, refs[n + m:n + m + k], refs[n + m + k:n + m + 2 * k]):
            cp.wait_send()
            cp.wait_recv()

    ops = list(srcs) + list(lands)
    outs = pl.pallas_call(
        body, name=name,
        out_shape=tuple(pltpu.HBM(a.shape, a.dtype) for a in ops),
        in_specs=[HBM] * (n + m) + [SEM] * (2 * k) + [ANY] * len(after),
        out_specs=tuple([HBM] * (n + m)),
        input_output_aliases={i: i for i in range(n + m)},
        compiler_params=pltpu.CompilerParams(has_side_effects=EFFECT),
    )(*ops, *send_sems, *recv_sems, *after)
    return list(outs[:n]), list(outs[n:])


def _ag_rows(land_ref, px, py, pc):
    r = land_ref.shape[0] // N_DEV
    start = pl.multiple_of((4 * px + 2 * py + pc) * r, ROW_ALIGN)
    return land_ref.at[pl.ds(start, r), :]


def _ag_copies_to(which):
    def copies(src_refs, land_refs, send_sems, recv_sems):
        x, y, c = _place()
        peers = [(x, y, 1 - c), (1 - x, y, c), (x, 1 - y, c), (1 - x, 1 - y, c)]
        return [pltpu.make_async_remote_copy(
            src_ref=_ag_rows(land_refs[a], x, y, c), dst_ref=_ag_rows(land_refs[a], x, y, c),
            send_sem=send_sems[len(which) * a + k], recv_sem=recv_sems[len(which) * a + k],
            device_id=peers[p], device_id_type=MESH)
            for a in range(len(land_refs)) for k, p in enumerate(which)]
    return copies


_ag_copies = _ag_copies_to((0, 1, 2, 3))
_ag_copies_near = _ag_copies_to((0, 1, 2))
_ag_copies_far = _ag_copies_to((3,))


def _ag_forward(lands, name, which=(0, 1, 2)):
    n = len(lands)

    def body(*refs):
        land_refs = refs[n:2 * n]
        send_sems, recv_sems = refs[2 * n:]
        x, y, c = _place()
        chips = [(1 - x, y), (x, 1 - y), (1 - x, 1 - y)]

        def copy(a, k, pc):
            px, py = chips[which[k]]
            return pltpu.make_async_remote_copy(
                src_ref=_ag_rows(land_refs[a], px, py, pc), dst_ref=_ag_rows(land_refs[a], px, py, pc),
                send_sem=send_sems.at[a, k], recv_sem=recv_sems.at[a, k], device_id=(x, y, 1 - c), device_id_type=MESH)

        passed = [copy(a, k, c) for a in range(n) for k in range(len(which))]
        for cp in passed:
            cp.start()
        for a in range(n):
            for k in range(len(which)):
                copy(a, k, 1 - c).wait_recv()
        for cp in passed:
            cp.wait_send()

    sems = pltpu.SemaphoreType.DMA((n, len(which)))
    return pl.pallas_call(
        body, name=name,
        in_specs=[ANY] * n, out_specs=[ANY] * n,
        out_shape=[jax.ShapeDtypeStruct(l.shape, l.dtype) for l in lands],
        input_output_aliases={i: i for i in range(n)},
        scratch_shapes=[sems, sems],
    )(*lands)


def _allgather_place(shards):
    x, y, c = _place()
    return [lax.dynamic_update_slice(lax.empty((N_DEV * s.shape[0], s.shape[1]), s.dtype), s,
                                     ((4 * x + 2 * y + c) * s.shape[0], 0)) for s in shards]


def _allgather_start(lands, name, after=None):
    return _split_start([], lands, 4 * len(lands), _ag_copies, name + "_start", after=after)


def _allgather_finish(started, after, name):
    send_sems, recv_sems, _, lands, _ = started
    _, lands = _split_wait(send_sems, recv_sems, [], lands, after, _ag_copies, name + "_wait")
    return list(_ag_forward(lands, name + "_forward"))


def _rs_swap_cores(grads, name):
    n = len(grads)

    def body(*refs):
        ins, outs = refs[:n], refs[n:2 * n]
        send_sems, recv_sems = refs[2 * n:]
        x, y, c = _place()
        cps = []
        for a in range(n):
            r = ins[a].shape[0] // N_DEV
            for q in range(4):
                start = pl.multiple_of((2 * q + 1 - c) * r, ROW_ALIGN)
                cps.append(pltpu.make_async_remote_copy(
                    src_ref=ins[a].at[pl.ds(start, r), :], dst_ref=outs[a].at[q],
                    send_sem=send_sems.at[a, q], recv_sem=recv_sems.at[a, q],
                    device_id=(x, y, 1 - c), device_id_type=MESH))
        for cp in cps:
            cp.start()
        for cp in cps:
            cp.wait()

    return pl.pallas_call(
        body, name=name, in_specs=[ANY] * n, out_specs=[ANY] * n,
        out_shape=[jax.ShapeDtypeStruct((4, g.shape[0] // N_DEV, g.shape[1]), g.dtype) for g in grads],
        scratch_shapes=[pltpu.SemaphoreType.DMA((n, 4)), pltpu.SemaphoreType.DMA((n, 4))],
    )(*grads)


def _rs_chip_copies(sum_refs, land_refs, send_sems, recv_sems):
    x, y, c = _place()
    chips = [(1 - x, y), (x, 1 - y), (1 - x, 1 - y)]
    return [pltpu.make_async_remote_copy(
        src_ref=sum_refs[a].at[2 * px + py], dst_ref=land_refs[a].at[2 * x + y],
        send_sem=send_sems[3 * a + j], recv_sem=recv_sems[3 * a + j], device_id=(px, py, c), device_id_type=MESH)
        for a in range(len(sum_refs)) for j, (px, py) in enumerate(chips)]


def _row_tile(r):
    return max(t for t in range(ROW_ALIGN, min(r, 1024) + 1, ROW_ALIGN) if r % t == 0)


def _rs_add_cores(grad, recv, cidx, name):
    r, cols = recv.shape[1], recv.shape[2]
    tr = _row_tile(r)
    nb = r // tr

    def body(c_ref, g_ref, r_ref, o_ref):
        o_ref[...] = (g_ref[...].astype(F32) + r_ref[...].astype(F32)).astype(o_ref.dtype)

    return pl.pallas_call(
        body, name=name,
        grid_spec=pltpu.PrefetchScalarGridSpec(
            num_scalar_prefetch=1, grid=(4, nb),
            in_specs=[pl.BlockSpec((tr, cols), lambda q, i, c_ref: ((2 * q + c_ref[0]) * nb + i, 0)),
                      pl.BlockSpec((None, tr, cols), lambda q, i, c_ref: (q, i, 0))],
            out_specs=pl.BlockSpec((None, tr, cols), lambda q, i, c_ref: (q, i, 0))),
        out_shape=jax.ShapeDtypeStruct(recv.shape, recv.dtype),
        compiler_params=_cp(("parallel", "parallel")),
    )(cidx, grad, recv)


def _rs_add_chips(own, recv, slots, name):
    r, cols = recv.shape[1], recv.shape[2]
    tr = _row_tile(r)

    def body(s_ref, o_ref, r0_ref, r1_ref, r2_ref, out_ref):
        acc = o_ref[...].astype(F32)
        for ref in (r0_ref, r1_ref, r2_ref):
            acc = acc + ref[...].astype(F32)
        out_ref[...] = acc

    pick = lambda k: pl.BlockSpec((None, tr, cols), functools.partial(lambda i, s_ref, k: (s_ref[k], i, 0), k=k))
    return pl.pallas_call(
        body, name=name,
        grid_spec=pltpu.PrefetchScalarGridSpec(
            num_scalar_prefetch=1, grid=(r // tr,),
            in_specs=[pick(0), pick(1), pick(2), pick(3)],
            out_specs=pl.BlockSpec((tr, cols), lambda i, s_ref: (i, 0))),
        out_shape=jax.ShapeDtypeStruct((r, cols), F32),
        compiler_params=_cp(("parallel",)),
    )(slots, own, recv, recv, recv)


def _rs_core_copies(grad_refs, land_refs, send_sems, recv_sems):
    x, y, c = _place()
    cps = []
    for a in range(len(grad_refs)):
        r = grad_refs[a].shape[0] // N_DEV
        for q in range(4):
            start = pl.multiple_of((2 * q + 1 - c) * r, ROW_ALIGN)
            cps.append(pltpu.make_async_remote_copy(
                src_ref=grad_refs[a].at[pl.ds(start, r), :], dst_ref=land_refs[a].at[q],
                send_sem=send_sems[4 * a + q], recv_sem=recv_sems[4 * a + q],
                device_id=(x, y, 1 - c), device_id_type=MESH))
    return cps


def _reduce_scatter_chips_start(grads, recv, tag):
    cidx = lax.axis_index("c").astype(jnp.int32)[None]
    sums = [_rs_add_cores(g, rv, cidx, f"rs_add_cores_{tag}_{i}") for i, (g, rv) in enumerate(zip(grads, recv))]
    lands = [lax.empty(s.shape, s.dtype) for s in sums]
    return _split_start(sums, lands, 3 * len(sums), _rs_chip_copies, f"rs_chips_{tag}_start")


def _reduce_scatter_start(grads, tag):
    return _reduce_scatter_chips_start(grads, _rs_swap_cores(grads, f"rs_swap_cores_{tag}"), tag)


def _reduce_scatter_cores_start(grads, tag):
    lands = [lax.empty((4, g.shape[0] // N_DEV, g.shape[1]), g.dtype) for g in grads]
    return _split_start(grads, lands, 4 * len(grads), _rs_core_copies, f"rs_cores_{tag}_start")


def _reduce_scatter_cores_finish(started, after, tag):
    send_sems, recv_sems, grads, lands, _ = started
    grads, recv = _split_wait(send_sems, recv_sems, grads, lands, after, _rs_core_copies, f"rs_cores_{tag}_wait")
    return _reduce_scatter_chips_start(grads, recv, tag)


def _reduce_scatter_finish(started, after, tag):
    send_sems, recv_sems, sums, lands, _ = started
    sums, lands = _split_wait(send_sems, recv_sems, sums, lands, after, _rs_chip_copies, f"rs_chips_{tag}_wait")
    x, y = lax.axis_index("x"), lax.axis_index("y")
    slots = jnp.stack([2 * x + y, 2 * (1 - x) + y, 2 * x + 1 - y, 2 * (1 - x) + 1 - y]).astype(jnp.int32)
    return [_rs_add_chips(s, l, slots, f"rs_add_chips_{tag}_{i}") for i, (s, l) in enumerate(zip(sums, lands))]


def _allreduce_small(packs, name, after=()):
    n = len(packs)
    after = list(after)
    assert all(p.shape[0] % (8 * N_DEV) == 0 for p in packs)

    def body(*refs):
        p_refs = refs[:n]
        refs = refs[n + len(after):]
        o_refs, part_refs = refs[:n], refs[n:2 * n]
        send1, recv1, send2, recv2 = refs[2 * n:]
        x, y, c = _place()
        me = 4 * x + 2 * y + c

        def block(ref, d):
            rs = ref.shape[0] // N_DEV
            return ref.at[pl.ds(pl.multiple_of(d * rs, 8), rs), :]

        peers = [(1 - x if k & 4 else x, 1 - y if k & 2 else y, 1 - c if k & 1 else c) for k in range(1, N_DEV)]
        scatter = [pltpu.make_async_remote_copy(
            src_ref=block(p_refs[a], 4 * px + 2 * py + pc), dst_ref=part_refs[a].at[me],
            send_sem=send1.at[a, k], recv_sem=recv1.at[a, k], device_id=(px, py, pc), device_id_type=MESH)
            for a in range(n) for k, (px, py, pc) in enumerate(peers)]
        for cp in scatter:
            cp.start()
        for a in range(n):
            part_refs[a][me] = block(p_refs[a], me)[...]
        for cp in scatter:
            cp.wait()
        for a in range(n):
            acc = part_refs[a][0]
            for d in range(1, N_DEV):
                acc = acc + part_refs[a][d]
            block(o_refs[a], me)[...] = acc
        gather = [pltpu.make_async_remote_copy(
            src_ref=block(o_refs[a], me), dst_ref=block(o_refs[a], me), send_sem=send2.at[a, k], recv_sem=recv2.at[a, k],
            device_id=peer, device_id_type=MESH) for a in range(n) for k, peer in enumerate(peers)]
        for cp in gather:
            cp.start()
        for a in range(n):
            for k, (px, py, pc) in enumerate(peers):
                theirs = block(o_refs[a], 4 * px + 2 * py + pc)
                pltpu.make_async_remote_copy(
                    src_ref=theirs, dst_ref=theirs, send_sem=send2.at[a, k], recv_sem=recv2.at[a, k],
                    device_id=(px, py, pc), device_id_type=MESH).wait_recv()
        for cp in gather:
            cp.wait_send()

    sems = pltpu.SemaphoreType.DMA((n, N_DEV - 1))
    vmem = pl.BlockSpec(memory_space=pltpu.VMEM)
    return pl.pallas_call(
        body, name=name,
        in_specs=[vmem] * n + [ANY] * len(after), out_specs=[vmem] * n,
        out_shape=[jax.ShapeDtypeStruct(p.shape, F32) for p in packs],
        scratch_shapes=[pltpu.VMEM((N_DEV, p.shape[0] // N_DEV, p.shape[1]), F32) for p in packs] + [sems] * 4,
        compiler_params=pltpu.CompilerParams(vmem_limit_bytes=VMEM_LIMIT),
    )(*packs, *after)


ADAM_TILE_BYTES = 2 * 1024 * 1024


def _adam_tiles(rows, cols):
    tc = cols // 2 if cols % 256 == 0 and cols >= 2048 else cols
    tr = max(t for t in range(8, rows + 1, 8) if rows % t == 0 and t * max(tc, 128) * 4 <= ADAM_TILE_BYTES) \
        if rows % 8 == 0 else rows
    return tr, tc


def _adam_math(w, g, m, v):
    nm = ADAM_B1 * m + (1.0 - ADAM_B1) * g
    nv = ADAM_B2 * v + (1.0 - ADAM_B2) * jnp.square(g)
    c1 = 1.0 - ADAM_B1 ** ADAM_STEP
    c2 = 1.0 - ADAM_B2 ** ADAM_STEP
    return -ADAM_LR * ((nm / c1) / (jnp.sqrt(nv / c2) + ADAM_EPS) + ADAM_WD * w), nm, nv


def _adamw_layer(w, g, m, v, layer, carry, name):
    _, rows, cols = w.shape
    tr, tc = _adam_tiles(rows, cols)

    def body(w_ref, g_ref, m_ref, v_ref, *rest):
        go_ref, d_ref, nm_ref, nv_ref = rest[-4:]
        gv = g_ref[...]
        go_ref[...] = gv
        d_ref[...], nm_ref[...], nv_ref[...] = _adam_math(w_ref[...], gv, m_ref[...], v_ref[...])

    blk = pl.BlockSpec((None, tr, tc), lambda i, j: (layer, i, j))
    flat = pl.BlockSpec((tr, tc), lambda i, j: (i, j))
    sh = jax.ShapeDtypeStruct(w.shape, F32)
    carry = [] if carry is None else list(carry)
    return pl.pallas_call(
        body, name=name, grid=(rows // tr, cols // tc),
        in_specs=[blk, flat, blk, blk] + [ANY] * len(carry), out_specs=[blk] * 4, out_shape=[sh] * 4,
        input_output_aliases={4 + k: k for k in range(len(carry))},
        compiler_params=_cp(("parallel", "parallel")),
    )(w, g, m, v, *carry)


def _adamw(w, g, m, v, name):
    shape = w.shape
    rows, cols = shape[-2:]
    lead = shape[:-2]
    nl = math.prod(lead)
    tr, tc = _adam_tiles(rows, cols)

    def body(w_ref, g_ref, m_ref, v_ref, d_ref, nm_ref, nv_ref):
        d_ref[...], nm_ref[...], nv_ref[...] = _adam_math(w_ref[...], g_ref[...], m_ref[...], v_ref[...])

    def index(b, i, j):
        return (*jnp.unravel_index(b, lead), i, j) if lead else (i, j)

    blk = pl.BlockSpec((*[None] * len(lead), tr, tc), index)
    sh = jax.ShapeDtypeStruct(shape, F32)
    return pl.pallas_call(
        body, name=name, grid=(nl, rows // tr, cols // tc), in_specs=[blk] * 4, out_specs=[blk] * 3,
        out_shape=[sh] * 3, compiler_params=_cp(("parallel", "parallel", "parallel")),
    )(w, g, m, v)


WEIGHTS = ("norm_w", "w_in", "ssm_a_re", "ssm_a_im", "ssm_log_dt", "ssm_b_re", "ssm_b_im", "ssm_c_re", "ssm_c_im",
           "ssm_d", "ssm_glu_w", "ssm_glu_b", "sg_ln_w", "sg_ln_b", "sg_w", "sg_b", "attn_sinks",
           "w_branch_a", "w_branch_b", "w_branch_c", "w_out", "final_norm_w")
BIG = ("w_in", "ssm_glu_w", "w_branch_a", "w_branch_b", "w_branch_c", "w_out")
BIG_KEY = {"w_in": ("win_t", True), "ssm_glu_w": ("glu", False), "w_branch_a": ("wba_t", True),
           "w_branch_b": ("wbb_t", True), "w_branch_c": ("wbc_t", True), "w_out": ("wout", False)}
VIEWS = {"w_in": (1, 2), "ssm_b_re": (2, 3), "ssm_b_im": (2, 3)}
PACKS = (
    (64, (("ssm_a_re",), ("ssm_a_im",), ("ssm_c_re",), ("ssm_c_im",), ("ssm_b_re",), ("ssm_b_im",))),
    (128, (("sg_w",),)),
    (1024, (("ssm_d", "ssm_glu_b", "sg_ln_w", "sg_ln_b"), ("norm_w", "final_norm_w", "sg_b"), ("ssm_log_dt", "attn_sinks"))),
)
PACK_ROWS = 8 * N_DEV


def _view(n, a):
    return jnp.swapaxes(a, *VIEWS[n]) if n in VIEWS else a


def _group_rows(arrs, cols):
    return -(-sum(-(-a.size // cols) for a in arrs) // 8) * 8


def _pack(groups, cols):
    parts = []
    for arrs in groups:
        if len(arrs) == 1 and arrs[0].shape[-1] == cols and arrs[0].size % (8 * cols) == 0:
            parts.append(arrs[0].reshape(-1, cols))
            continue
        flat = [jnp.pad(a.reshape(-1), (0, -a.size % cols)) for a in arrs]
        flat = jnp.concatenate(flat) if len(flat) > 1 else flat[0]
        nrow = _group_rows(arrs, cols)
        parts.append(jnp.pad(flat, (0, nrow * cols - flat.shape[0])).reshape(nrow, cols))
    pad = -sum(p.shape[0] for p in parts) % PACK_ROWS
    if pad:
        parts.append(jnp.zeros((pad, cols), F32))
    return jnp.concatenate(parts, axis=0)


def _unpack(pack, groups):
    cols = pack.shape[1]
    out, row = [], 0
    for arrs in groups:
        nrow = _group_rows(arrs, cols)
        rows = pack[row:row + nrow]
        row += nrow
        if len(arrs) == 1 and arrs[0].shape[-1] == cols and arrs[0].size == nrow * cols:
            out.append(rows.reshape(arrs[0].shape))
            continue
        flat, off = rows.reshape(-1), 0
        for a in arrs:
            out.append(flat[off:off + a.size].reshape(a.shape))
            off += -(-a.size // cols) * cols
    return out


def kernel(x, norm_w, w_in, ssm_a_re, ssm_a_im, ssm_log_dt, ssm_b_re, ssm_b_im, ssm_c_re, ssm_c_im, ssm_d, ssm_glu_w, ssm_glu_b, sg_ln_w, sg_ln_b, sg_w, sg_b, attn_sinks, w_branch_a, w_branch_b, w_branch_c, w_out, final_norm_w, loss_target, m_norm_w, m_w_in, m_ssm_a_re, m_ssm_a_im, m_ssm_log_dt, m_ssm_b_re, m_ssm_b_im, m_ssm_c_re, m_ssm_c_im, m_ssm_d, m_ssm_glu_w, m_ssm_glu_b, m_sg_ln_w, m_sg_ln_b, m_sg_w, m_sg_b, m_attn_sinks, m_w_branch_a, m_w_branch_b, m_w_branch_c, m_w_out, m_final_norm_w, v_norm_w, v_w_in, v_ssm_a_re, v_ssm_a_im, v_ssm_log_dt, v_ssm_b_re, v_ssm_b_im, v_ssm_c_re, v_ssm_c_im, v_ssm_d, v_ssm_glu_w, v_ssm_glu_b, v_sg_ln_w, v_sg_ln_b, v_sg_w, v_sg_b, v_attn_sinks, v_w_branch_a, v_w_branch_b, v_w_branch_c, v_w_out, v_final_norm_w):
    w = dict(zip(WEIGHTS, (norm_w, w_in, ssm_a_re, ssm_a_im, ssm_log_dt, ssm_b_re, ssm_b_im, ssm_c_re, ssm_c_im, ssm_d, ssm_glu_w, ssm_glu_b, sg_ln_w, sg_ln_b, sg_w, sg_b, attn_sinks, w_branch_a, w_branch_b, w_branch_c, w_out, final_norm_w)))
    m = dict(zip(WEIGHTS, (m_norm_w, m_w_in, m_ssm_a_re, m_ssm_a_im, m_ssm_log_dt, m_ssm_b_re, m_ssm_b_im, m_ssm_c_re, m_ssm_c_im, m_ssm_d, m_ssm_glu_w, m_ssm_glu_b, m_sg_ln_w, m_sg_ln_b, m_sg_w, m_sg_b, m_attn_sinks, m_w_branch_a, m_w_branch_b, m_w_branch_c, m_w_out, m_final_norm_w)))
    v = dict(zip(WEIGHTS, (v_norm_w, v_w_in, v_ssm_a_re, v_ssm_a_im, v_ssm_log_dt, v_ssm_b_re, v_ssm_b_im, v_ssm_c_re, v_ssm_c_im, v_ssm_d, v_ssm_glu_w, v_ssm_glu_b, v_sg_ln_w, v_sg_ln_b, v_sg_w, v_sg_b, v_attn_sinks, v_w_branch_a, v_w_branch_b, v_w_branch_c, v_w_out, v_final_norm_w)))

    keys = [BIG_KEY[n][0] for n in BIG]
    wv, mv, vv = ({n: _view(n, a) for n, a in d.items()} for d in (w, m, v))
    shards = [[(wv[n][l] if n in VIEWS else w[n][l].T if BIG_KEY[n][1] else w[n][l]).astype(BF16) for n in BIG]
              for l in range(DEPTH)]
    small_p = [{n: w[n][l] for n in SMALL} for l in range(DEPTH)]
    xv, tgt = x[0], loss_target[0]
    tabs = _rope_tables(xv.shape[0])

    lands = [[_allgather_place(shards[l][:1]), _allgather_place(shards[l][1:])] for l in range(DEPTH)]
    s5 = [_s5_prep(small_p[l], f"l{l}") for l in range(DEPTH)]
    wmv_packs = {cols: [_pack([[d[n] for n in names] for names in groups], cols) for d in (wv, mv, vv)]
                 for cols, groups in PACKS}
    near = _split_start([], lands[0][0], 3, _ag_copies_near, "ag_l0_win_near_start")
    got = {}
    x_, y_ = lax.axis_index("x"), lax.axis_index("y")
    n_tiles = D_IN // PROJ_TN
    far_first = (D_IN // 4 // PROJ_TN) * (2 * (1 - x_) + (1 - y_))
    n_far = -(-D_IN // 4 // PROJ_TN)
    tile_ids = jnp.arange(n_tiles, dtype=jnp.int32)
    is_far = (tile_ids >= far_first) & (tile_ids < far_first + n_far)
    near_tiles = jnp.sort(jnp.where(is_far, n_tiles, tile_ids))[:n_tiles - n_far]
    far_tiles = (far_first + jnp.arange(n_far)).astype(jnp.int32)

    def proj_of0(h):
        early = [h, *lands[0][1], *lands[1][0], *lands[1][1], *s5[0][1], *s5[1][1], near_tiles, far_tiles]
        early += [p for ps in wmv_packs.values() for p in ps]
        _, land = _split_wait(near[0], near[1], [], near[3], early, _ag_copies_near, "ag_l0_win_near_wait")
        land = _ag_forward(land, "ag_l0_win_near_forward", which=(0, 1))
        far = _split_start([], land, 1, _ag_copies_far, "ag_l0_win_far_start")
        got["ag0b"] = _allgather_start(lands[0][1], "ag_l0_rest", after=far[4])
        got["near1"] = _split_start([], lands[1][0], 3, _ag_copies_near, "ag_l1_win_near_start", after=got["ag0b"][4])
        proj = _in_proj_tiles(h, far[3][0], near_tiles, None, "in_proj_l0_near", after=got["near1"][4])
        _, land = _split_wait(far[0], far[1], [], far[3], proj, _ag_copies_far, "ag_l0_win_far_wait")
        got["win0"] = _ag_forward(land, "ag_l0_win_far_forward", which=(2,))[0]
        return _in_proj_tiles(h, got["win0"], far_tiles, proj, "in_proj_l0_far")

    def after_proj0(proj):
        got["w0"] = dict(zip(keys, [got["win0"]] + _allgather_finish(got["ag0b"], proj, "ag_l0_rest")))
        return got["w0"]

    x1, saved0 = _layer_fwd(xv, small_p[0], None, tabs, "l0", s5=s5[0], proj_of=proj_of0, after_proj=after_proj0)
    big_w0 = got["w0"]

    def proj_of1(h):
        near1 = got["near1"]
        _, land = _split_wait(near1[0], near1[1], [], near1[3], h, _ag_copies_near, "ag_l1_win_near_wait")
        land = _ag_forward(land, "ag_l1_win_near_forward", which=(0, 1))
        far1 = _split_start([], land, 1, _ag_copies_far, "ag_l1_win_far_start")
        got["ag1b"] = _allgather_start(lands[1][1], "ag_l1_rest", after=far1[4])
        proj = _in_proj_tiles(h, far1[3][0], near_tiles, None, "in_proj_l1_near", after=got["ag1b"][4])
        _, land = _split_wait(far1[0], far1[1], [], far1[3], proj, _ag_copies_far, "ag_l1_win_far_wait")
        got["win1"] = _ag_forward(land, "ag_l1_win_far_forward", which=(2,))[0]
        return _in_proj_tiles(h, got["win1"], far_tiles, proj, "in_proj_l1_far")

    def after_proj1(proj):
        got["w1"] = dict(zip(keys, [got["win1"]] + _allgather_finish(got["ag1b"], proj, "ag_l1_rest")))
        return got["w1"]

    x2, saved1 = _layer_fwd(x1, small_p[1], None, tabs, "l1", s5=s5[1], proj_of=proj_of1, after_proj=after_proj1)
    big_w1 = got["w1"]
    loss_acc, dx2, dfw = _final(x2, w["final_norm_w"][None], tgt, "final_norm_loss")
    loss = lax.psum(loss_acc[0, 0], ("x", "y", "c"))
    dfw = dfw[0]

    dx1, big_g1, small_g1 = _layer_bwd(dx2, small_p[1], big_w1, tabs, saved1, "l1")
    rs1_cores = _reduce_scatter_cores_start([big_g1[k] for k in keys], "l1")

    def after_merge0(x):
        got["rs1"] = _reduce_scatter_cores_finish(rs1_cores, x, "l1")
        return got["rs1"][4]

    def before_win0(big):
        got["rs0b"] = _reduce_scatter_start([big[k] for k in keys[1:]], "l0_rest")
        return got["rs0b"][4]

    def after_win0(big):
        got["rs0a"] = _reduce_scatter_start([big["win_t"]], "l0_win")
        return got["rs0a"][4]

    dx, big_g0, small_g0 = _layer_bwd(dx1, small_p[0], big_w0, tabs, saved0, "l0", first_after=rs1_cores[4],
                                      after_merge=after_merge0, before_win=before_win0, after_win=after_win0)
    rs1 = got["rs1"]
    small_g = [small_g0, small_g1]
    grads, delta, new_m, new_v = {}, {}, {}, {}

    def big_adam(red, layer, carry):
        outs = {}
        for i, n in enumerate(BIG):
            g = red[i].T if BIG_KEY[n][1] and n not in VIEWS else red[i]
            outs[n] = _adamw_layer(wv[n], g, mv[n], vv[n], layer, None if carry is None else carry[n], f"adamw_{n}_l{layer}")
        return outs

    big1 = big_adam(_reduce_scatter_finish(rs1, dx, "l1"), 1, None)

    def small_grad(n):
        if n == "final_norm_w":
            return dfw
        if n in ("ssm_b_re", "ssm_b_im"):
            return jnp.stack([small_g[l][n.replace("ssm_b_", "ssm_bt_")].transpose(1, 0, 2) for l in range(DEPTH)])
        return jnp.stack([small_g[l][n] for l in range(DEPTH)])

    g_groups = [[[small_grad(n) for n in names] for names in groups] for _, groups in PACKS]
    reduced = _allreduce_small([_pack(gg, cols) for gg, (cols, _) in zip(g_groups, PACKS)], "allreduce_small",
                               after=[big1[n][1] for n in BIG])
    last = None
    for (cols, groups), gg, red in zip(PACKS, g_groups, reduced):
        names = [n for names in groups for n in names]
        grads.update(zip(names, _unpack(red, gg)))
        wp, mp, vp = wmv_packs[cols]
        outs = _adamw(wp, red, mp, vp, f"adamw_pack{cols}")
        last = outs[0]
        for res, o in zip((delta, new_m, new_v), outs):
            res.update(zip(names, _unpack(o, [[wv[n] for n in names] for names in groups])))

    red0 = (_reduce_scatter_finish(got["rs0a"], last, "l0_win")
            + _reduce_scatter_finish(got["rs0b"], last, "l0_rest"))
    for n, outs in big_adam(red0, 0, big1).items():
        grads[n], delta[n], new_m[n], new_v[n] = outs

    return (loss, dx[None], *[_view(n, d[n]) for d in (grads, delta, new_m, new_v) for n in WEIGHTS])
```

```python
import functools
import math

import jax
import jax.numpy as jnp
from jax import lax
from jax.experimental import pallas as pl
from jax.experimental.pallas import tpu as pltpu

F32 = jnp.float32
BF16 = jnp.bfloat16

D_MODEL = 2048
DEPTH = 2
EPS = 1e-6
NEG_INF = -1e30
N_DEV = 8

SSM_WIDTH = 1024
SSM_GROUP = 16
SSM_GROUPS = 64
SSM_STATE = 64
N_SLAB = 8
SLAB_CH = 128
SLAB_ST = 512
SUB = 8
N_GRP = 2
N_SEG = SUB * N_GRP

SG_HEADS = 8
CHUNK = 128
HEAD_DIM = 64
ATT_HEADS = 16
ROT_DIM = 16
ROPE_THETA = 500000.0

D_IN = 13568
OFF_UA, OFF_ZA, OFF_UB, OFF_VB, OFF_ZB, OFF_Q, OFF_KV, OFF_ZC, OFF_G = (
    0, 1024, 2048, 3072, 4096, 5120, 6144, 6400, 7424)

ADAM_LR, ADAM_B1, ADAM_B2, ADAM_EPS, ADAM_WD, ADAM_STEP = 0.001, 0.9, 0.999, 1e-08, 0.01, 10

VMEM_LIMIT = 56 * 1024 * 1024


def _cp(sem=None):
    return pltpu.CompilerParams(dimension_semantics=sem, vmem_limit_bytes=VMEM_LIMIT)


def _dot(a, b):
    return jnp.dot(a, b, preferred_element_type=F32)


def _dot_nt(a, b):
    return lax.dot_general(a, b, (((1,), (1,)), ((), ())), preferred_element_type=F32)


def _dot_tn(a, b):
    return lax.dot_general(a, b, (((0,), (0,)), ((), ())), preferred_element_type=F32)


def _mm(a, b, mode, out_dtype, tm, tn, tk, name, res=None, after=None):
    if mode == "nn":
        (m, k), (_, n) = a.shape, b.shape
    elif mode == "nt":
        (m, k), (n, _) = a.shape, b.shape
    else:
        (k, m), (_, n) = a.shape, b.shape
    tm, tn, tk = min(tm, m), min(tn, n), min(tk, k)
    assert m % tm == 0 and n % tn == 0 and k % tk == 0, (name, m, n, k, tm, tn, tk)
    nk = k // tk
    a_spec = {"nn": pl.BlockSpec((tm, tk), lambda i, j, kk: (i, kk)),
              "nt": pl.BlockSpec((tm, tk), lambda i, j, kk: (i, kk)),
              "tn": pl.BlockSpec((tk, tm), lambda i, j, kk: (kk, i))}[mode]
    b_spec = {"nn": pl.BlockSpec((tk, tn), lambda i, j, kk: (kk, j)),
              "nt": pl.BlockSpec((tn, tk), lambda i, j, kk: (j, kk)),
              "tn": pl.BlockSpec((tk, tn), lambda i, j, kk: (kk, j))}[mode]
    dot = {"nn": _dot, "nt": _dot_nt, "tn": _dot_tn}[mode]
    has_res = res is not None
    direct = out_dtype == F32 and not has_res

    def body(*refs):
        ins, outs = refs[:2 + has_res + (after is not None)], refs[2 + has_res + (after is not None):]
        a_ref, b_ref = ins[:2]
        r_ref = ins[2] if has_res else None
        o_ref = outs[0]
        acc = o_ref if direct else outs[1]
        kk = pl.program_id(2)

        @pl.when(kk == 0)
        def _():
            acc[...] = jnp.zeros_like(acc)

        acc[...] += dot(a_ref[...].astype(BF16), b_ref[...].astype(BF16))

        if not direct:
            @pl.when(kk == nk - 1)
            def _():
                r = acc[...]
                if has_res:
                    r = r + r_ref[...]
                o_ref[...] = r.astype(out_dtype)

    in_specs = [a_spec, b_spec]
    args = [a, b]
    if has_res:
        in_specs.append(pl.BlockSpec((tm, tn), lambda i, j, kk: (i, j)))
        args.append(res)
    if after is not None:
        in_specs.append(pl.BlockSpec(memory_space=pl.ANY))
        args.append(after)
    return pl.pallas_call(
        body, name=name,
        grid=(m // tm, n // tn, nk),
        in_specs=in_specs,
        out_specs=pl.BlockSpec((tm, tn), lambda i, j, kk: (i, j)),
        out_shape=jax.ShapeDtypeStruct((m, n), out_dtype),
        scratch_shapes=[] if direct else [pltpu.VMEM((tm, tn), F32)],
        compiler_params=_cp(("parallel", "parallel", "arbitrary")),
    )(*args)


PROJ_TN = 256


def _in_proj_tiles(h, win_t, tiles, carry, name, after=None):
    L, K = h.shape
    extra = [a for a in (carry, after) if a is not None]

    def body(t_ref, h_ref, w_ref, *rest):
        rest[len(extra)][...] = _dot_nt(h_ref[...], w_ref[...]).astype(BF16)

    return pl.pallas_call(
        body, name=name,
        grid_spec=pltpu.PrefetchScalarGridSpec(
            num_scalar_prefetch=1, grid=(tiles.shape[0],),
            in_specs=[pl.BlockSpec((L, K), lambda j, t: (0, 0)), pl.BlockSpec((PROJ_TN, K), lambda j, t: (t[j], 0))]
            + [pl.BlockSpec(memory_space=pl.ANY)] * len(extra),
            out_specs=pl.BlockSpec((L, PROJ_TN), lambda j, t: (0, t[j]))),
        out_shape=jax.ShapeDtypeStruct((L, win_t.shape[0]), BF16),
        input_output_aliases={} if carry is None else {3: 0},
        compiler_params=_cp(("arbitrary",)),
    )(tiles, h, win_t, *extra)


def _rms(x, w):
    return x * lax.rsqrt(jnp.mean(x * x, axis=-1, keepdims=True) + EPS) * w


def _rms_fwd(x, w, name):
    L, D = x.shape
    tm = min(L, 256)

    def body(x_ref, w_ref, h_ref):
        h_ref[...] = _rms(x_ref[...], w_ref[...]).astype(BF16)

    return pl.pallas_call(
        body, name=name, grid=(L // tm,),
        in_specs=[pl.BlockSpec((tm, D), lambda i: (i, 0)), pl.BlockSpec((1, D), lambda i: (0, 0))],
        out_specs=pl.BlockSpec((tm, D), lambda i: (i, 0)),
        out_shape=jax.ShapeDtypeStruct((L, D), BF16),
        compiler_params=_cp(("parallel",)),
    )(x, w)


def _rms_bwd(x, w, dh, dres, name):
    L, D = x.shape
    tm = min(L, 256)

    def body(x_ref, w_ref, dh_ref, dres_ref, dx_ref, dw_ref):
        _, vjp = jax.vjp(_rms, x_ref[...], w_ref[...])
        dx, dw = vjp(dh_ref[...])
        dx_ref[...] = dx + dres_ref[...]

        @pl.when(pl.program_id(0) == 0)
        def _():
            dw_ref[...] = jnp.zeros_like(dw_ref)

        dw_ref[...] += dw

    row = pl.BlockSpec((tm, D), lambda i: (i, 0))
    vec = pl.BlockSpec((1, D), lambda i: (0, 0))
    return pl.pallas_call(
        body, name=name, grid=(L // tm,),
        in_specs=[row, vec, row, row],
        out_specs=[row, vec],
        out_shape=[jax.ShapeDtypeStruct((L, D), F32), jax.ShapeDtypeStruct((1, D), F32)],
        compiler_params=_cp(("arbitrary",)),
    )(x, w, dh, dres)


def _final(x, fw, tgt, name):
    L, D = x.shape
    tm = min(L, 256)

    def loss_fn(xv, wv, tv):
        err = _rms(xv, wv) - tv
        return jnp.sum(err * err) * (0.5 / D)

    def body(x_ref, w_ref, t_ref, loss_ref, dx_ref, dw_ref):
        tv = t_ref[...]
        val, vjp = jax.vjp(lambda a, b: loss_fn(a, b, tv), x_ref[...], w_ref[...])
        dx, dw = vjp(jnp.ones((), F32))
        dx_ref[...] = dx

        @pl.when(pl.program_id(0) == 0)
        def _():
            dw_ref[...] = jnp.zeros_like(dw_ref)
            loss_ref[...] = jnp.zeros_like(loss_ref)

        dw_ref[...] += dw
        loss_ref[...] += jnp.full(loss_ref.shape, val, F32)

    row = pl.BlockSpec((tm, D), lambda i: (i, 0))
    vec = pl.BlockSpec((1, D), lambda i: (0, 0))
    return pl.pallas_call(
        body, name=name, grid=(L // tm,),
        in_specs=[row, vec, row],
        out_specs=[pl.BlockSpec((8, 128), lambda i: (0, 0)), row, vec],
        out_shape=[jax.ShapeDtypeStruct((8, 128), F32), jax.ShapeDtypeStruct((L, D), F32),
                   jax.ShapeDtypeStruct((1, D), F32)],
        compiler_params=_cp(("arbitrary",)),
    )(x, fw, tgt)


def _s5_param_fn(a_re, a_im, log_dt, bt_re, bt_im):
    dt = jnp.exp(log_dt)
    zr, zi = a_re * dt, a_im * dt
    er = jnp.exp(zr)
    lr, li = er * jnp.cos(zi), er * jnp.sin(zi)
    nr, ni = lr - 1.0, li
    den = a_re * a_re + a_im * a_im
    cr = (nr * a_re + ni * a_im) / den
    ci = (ni * a_re - nr * a_im) / den
    bbr = cr[None] * bt_re - ci[None] * bt_im
    bbi = cr[None] * bt_im + ci[None] * bt_re
    return lr, li, bbr, bbi


def _s5_params_fwd(a_re, a_im, log_dt, bt_re, bt_im, name):
    def body(ar, ai, ld, br, bi, lr, li, bbr, bbi):
        o = _s5_param_fn(ar[...], ai[...], ld[...], br[...], bi[...])
        lr[...], li[...], bbr[...], bbi[...] = o

    gp = jax.ShapeDtypeStruct(a_re.shape, F32)
    cgp = jax.ShapeDtypeStruct(bt_re.shape, F32)
    return pl.pallas_call(body, name=name, out_shape=[gp, gp, cgp, cgp])(a_re, a_im, log_dt, bt_re, bt_im)


def _s5_params_bwd(a_re, a_im, log_dt, bt_re, bt_im, dlr, dli, dbbr, dbbi, name):
    def body(ar, ai, ld, br, bi, g0, g1, g2, g3, o0, o1, o2, o3, o4):
        _, vjp = jax.vjp(_s5_param_fn, ar[...], ai[...], ld[...], br[...], bi[...])
        o0[...], o1[...], o2[...], o3[...], o4[...] = vjp((g0[...], g1[...], g2[...], g3[...]))

    gp = jax.ShapeDtypeStruct(a_re.shape, F32)
    cgp = jax.ShapeDtypeStruct(bt_re.shape, F32)
    return pl.pallas_call(body, name=name,
                          out_shape=[gp, gp, jax.ShapeDtypeStruct(log_dt.shape, F32), cgp, cgp])(
        a_re, a_im, log_dt, bt_re, bt_im, dlr, dli, dbbr, dbbi)


def _cmul(ar, ai, br, bi):
    return ar * br - ai * bi, ar * bi + ai * br


def _cpow(lr, li, n):
    rr, ri = None, None
    br, bi = lr, li
    while n:
        if n & 1:
            rr, ri = (br, bi) if rr is None else _cmul(rr, ri, br, bi)
        n >>= 1
        if n:
            br, bi = _cmul(br, bi, br, bi)
    return rr, ri


def _shift_rows(x, up):
    row = lax.broadcasted_iota(jnp.int32, x.shape, 0)
    if up:
        return jnp.where(row == SUB - 1, 0.0, pltpu.roll(x, SUB - 1, 0))
    return jnp.where(row == 0, 0.0, pltpu.roll(x, 1, 0))


NT = SLAB_ST // 128


def _lam_tiles(lr_ref, li_ref):
    return [(lr_ref[:, j * 128:(j + 1) * 128], li_ref[:, j * 128:(j + 1) * 128]) for j in range(NT)]


def _row_on_sublanes(ref, j, t):
    return ref[j, pl.ds(t, SUB, stride=0), :]


def _pow_table(pw_re, pw_im, lam_t, seg):
    assert seg % 8 == 0 and (seg // 8) & (seg // 8 - 1) == 0
    for j in range(NT):
        lr, li = lam_t[j][0][0:1], lam_t[j][1][0:1]
        r, i_ = lr, li
        for row in range(8):
            pw_re[j, row:row + 1, :] = r
            pw_im[j, row:row + 1, :] = i_
            if row < 7:
                r, i_ = _cmul(r, i_, lr, li)
        n = 8
        while n < seg:
            qr, qi = _cpow(lr, li, n)
            nr, ni = _cmul(pw_re[j, 0:n, :], pw_im[j, 0:n, :], qr, qi)
            pw_re[j, n:2 * n, :] = nr
            pw_im[j, n:2 * n, :] = ni
            n *= 2


def _seg_scan(s_re, s_im, lam_t, pw_re, pw_im, seg, reverse, prev=None):
    sgn = -1.0 if reverse else 1.0
    lt = [(lr, sgn * li) for lr, li in lam_t]
    tiles = [(g, j) for g in range(N_GRP) for j in range(NT)]
    zeros = jnp.zeros((SUB, 128), F32)

    def rows(g, i):
        return pl.ds(pl.multiple_of((g * seg + i) * SUB, SUB), SUB)

    def step1(t, carry):
        i = seg - 1 - t if reverse else t
        out = []
        for n, (g, j) in enumerate(tiles):
            nr, ni = _cmul(lt[j][0], lt[j][1], carry[2 * n], carry[2 * n + 1])
            nr = nr + s_re[j, rows(g, i), :]
            ni = ni + s_im[j, rows(g, i), :]
            s_re[j, rows(g, i), :] = nr
            s_im[j, rows(g, i), :] = ni
            out += [nr, ni]
        return tuple(out)

    zero = tuple(zeros for _ in range(2 * len(tiles)))
    ends = lax.fori_loop(0, seg, step1, zero)

    carries = [None] * (2 * len(tiles))
    row = lax.broadcasted_iota(jnp.int32, (SUB, 128), 0)
    dist = (SUB - 1 - row) if reverse else row
    edge = 0 if reverse else SUB - 1
    for j in range(NT):
        pr, pi = _cpow(lt[j][0], lt[j][1], seg)
        qr, qi = jnp.ones((SUB, 128), F32), zeros
        for s in range(1, SUB):
            tr, ti = _cmul(qr, qi, pr, pi)
            qr, qi = jnp.where(dist >= s, tr, qr), jnp.where(dist >= s, ti, qi)
        boundary = None
        for g in (reversed(range(N_GRP)) if reverse else range(N_GRP)):
            n = g * NT + j
            cr, ci = zeros, zeros
            for _ in range(SUB - 1):
                tr, ti = _cmul(pr, pi, cr, ci)
                cr = _shift_rows(tr + ends[2 * n], reverse)
                ci = _shift_rows(ti + ends[2 * n + 1], reverse)
            if boundary is not None:
                tr, ti = _cmul(qr, qi, boundary[0], boundary[1])
                cr, ci = cr + tr, ci + ti
            carries[2 * n], carries[2 * n + 1] = cr, ci
            fr, fi = _cmul(pr, pi, cr, ci)
            boundary = (jnp.broadcast_to((fr + ends[2 * n])[edge:edge + 1], (SUB, 128)),
                        jnp.broadcast_to((fi + ends[2 * n + 1])[edge:edge + 1], (SUB, 128)))

    def fix(t, i, acc, before):
        out = []
        pws = [(_row_on_sublanes(pw_re, j, t), sgn * _row_on_sublanes(pw_im, j, t)) for j in range(NT)]
        for n, (g, j) in enumerate(tiles):
            ar, ai = _cmul(pws[j][0], pws[j][1], carries[2 * n], carries[2 * n + 1])
            ar = ar + s_re[j, rows(g, i), :]
            ai = ai + s_im[j, rows(g, i), :]
            s_re[j, rows(g, i), :] = ar
            s_im[j, rows(g, i), :] = ai
            if before is not None:
                qr, qi = before(n)
                out += [acc[2 * n] + ar * qr + ai * qi, acc[2 * n + 1] + ai * qr - ar * qi]
        return tuple(out)

    if prev is None:
        lax.fori_loop(0, seg, lambda t, c: fix(t, seg - 1 - t if reverse else t, c, None), ())
        return carries
    assert reverse
    p_re, p_im, p_carries = prev

    def earlier(t):
        return lambda n: (p_re[tiles[n][1], rows(tiles[n][0], seg - 2 - t), :],
                          p_im[tiles[n][1], rows(tiles[n][0], seg - 2 - t), :])

    acc = lax.fori_loop(0, seg - 1, lambda t, c: fix(t, seg - 1 - t, c, earlier(t)), zero)
    acc = fix(seg - 1, 0, acc, lambda n: (p_carries[2 * n], p_carries[2 * n + 1]))
    return carries, [sum(acc[2 * (g * NT + j) + part] for g in range(N_GRP)) for j in range(NT) for part in range(2)]


def _seg_slice(k, seg):
    g, r = divmod(k, SUB)
    return pl.ds(g * seg * SUB + r, seg, stride=SUB)


def _seg_rows(ref, k, seg):
    return jnp.concatenate([ref[j, _seg_slice(k, seg), :] for j in range(NT)], axis=-1)


def _seg_store(ref, k, seg, val):
    for j in range(NT):
        ref[j, _seg_slice(k, seg), :] = val[:, j * 128:(j + 1) * 128]


def _s5_specs(L):
    col = lambda off: pl.BlockSpec((L, SLAB_CH), lambda j: (0, off + j))
    mat_b = pl.BlockSpec((None, SLAB_CH, SLAB_ST), lambda j: (j, 0, 0))
    mat_c = pl.BlockSpec((None, SLAB_ST, SLAB_CH), lambda j: (j, 0, 0))
    vec_s = pl.BlockSpec((None, SUB, SLAB_ST), lambda j: (j, 0, 0))
    vec_c = pl.BlockSpec((None, 1, SLAB_CH), lambda j: (j, 0, 0))
    return col, mat_b, mat_c, vec_s, vec_c


def _s5_states(u_ref, bre_ref, bim_ref, lam_t, pw_re, pw_im, s_re, s_im, seg):
    _pow_table(pw_re, pw_im, lam_t, seg)
    for k in range(N_SEG):
        uk = u_ref[pl.ds(k * seg, seg), :]
        _seg_store(s_re, k, seg, _dot(uk, bre_ref[...]))
        _seg_store(s_im, k, seg, _dot(uk, bim_ref[...]))
    return _seg_scan(s_re, s_im, lam_t, pw_re, pw_im, seg, reverse=False)


def _s5_fwd(proj, bre, bim, cre_t, cim_t, lam_re, lam_im, dvec, name):
    L = proj.shape[0]
    seg = L // N_SEG
    col, mat_b, mat_c, vec_s, vec_c = _s5_specs(L)
    rows = N_SEG * seg

    def body(u_ref, bre_ref, bim_ref, cre_ref, cim_ref, lr_ref, li_ref, d_ref, y_ref, s_re, s_im, pw_re, pw_im):
        _s5_states(u_ref, bre_ref, bim_ref, _lam_tiles(lr_ref, li_ref), pw_re, pw_im, s_re, s_im, seg)
        for k in range(N_SEG):
            y = (_dot(_seg_rows(s_re, k, seg).astype(BF16), cre_ref[...])
                 - _dot(_seg_rows(s_im, k, seg).astype(BF16), cim_ref[...]))
            y = y + d_ref[...] * u_ref[pl.ds(k * seg, seg), :].astype(F32)
            y_ref[pl.ds(k * seg, seg), :] = jax.nn.gelu(y).astype(BF16)

    return pl.pallas_call(
        body, name=name, grid=(N_SLAB,),
        in_specs=[col(OFF_UA // SLAB_CH), mat_b, mat_b, mat_c, mat_c, vec_s, vec_s, vec_c],
        out_specs=pl.BlockSpec((L, SLAB_CH), lambda j: (0, j)),
        out_shape=jax.ShapeDtypeStruct((L, SSM_WIDTH), BF16),
        scratch_shapes=[pltpu.VMEM((NT, rows, 128), F32)] * 2 + [pltpu.VMEM((NT, seg, 128), F32)] * 2,
        compiler_params=_cp(("parallel",)),
    )(proj, bre, bim, cre_t, cim_t, lam_re, lam_im, dvec)


def _s5_bwd(proj, dy, bre, bim, cre_t, cim_t, lam_re, lam_im, dvec, name):
    L = proj.shape[0]
    seg = L // N_SEG
    col, mat_b, mat_c, vec_s, vec_c = _s5_specs(L)
    rows = N_SEG * seg
    dlam_spec = pl.BlockSpec((None, 1, SLAB_ST), lambda j: (j, 0, 0))

    def body(u_ref, dy_ref, bre_ref, bim_ref, cre_ref, cim_ref, lr_ref, li_ref, d_ref,
             du_ref, dbre_ref, dbim_ref, dcre_ref, dcim_ref, dlr_ref, dli_ref, dd_ref,
             s_re, s_im, a_re, a_im, pw_re, pw_im, dyp):
        lam_t = _lam_tiles(lr_ref, li_ref)
        carry_s = _s5_states(u_ref, bre_ref, bim_ref, lam_t, pw_re, pw_im, s_re, s_im, seg)
        dcre = jnp.zeros((SLAB_ST, SLAB_CH), F32)
        dcim = jnp.zeros((SLAB_ST, SLAB_CH), F32)
        dd = jnp.zeros((1, SLAB_CH), F32)
        for k in range(N_SEG):
            sre = _seg_rows(s_re, k, seg).astype(BF16)
            sim = _seg_rows(s_im, k, seg).astype(BF16)
            uk = u_ref[pl.ds(k * seg, seg), :].astype(F32)
            ypre = _dot(sre, cre_ref[...]) - _dot(sim, cim_ref[...]) + d_ref[...] * uk
            _, vjp = jax.vjp(jax.nn.gelu, ypre)
            (dyk,) = vjp(dy_ref[pl.ds(k * seg, seg), :].astype(F32))
            dyp[pl.ds(k * seg, seg), :] = dyk
            dd = dd + jnp.sum(dyk * uk, axis=0, keepdims=True)
            dyb = dyk.astype(BF16)
            dcre = dcre + _dot_tn(sre, dyb)
            dcim = dcim - _dot_tn(sim, dyb)
            _seg_store(a_re, k, seg, _dot_nt(dyb, cre_ref[...]))
            _seg_store(a_im, k, seg, -_dot_nt(dyb, cim_ref[...]))
        dcre_ref[...] = dcre
        dcim_ref[...] = dcim
        dd_ref[...] = dd

        _, acc = _seg_scan(a_re, a_im, lam_t, pw_re, pw_im, seg, reverse=True, prev=(s_re, s_im, carry_s))
        dlr_ref[...] = jnp.concatenate([jnp.sum(acc[2 * j], axis=0, keepdims=True) for j in range(NT)], axis=-1)
        dli_ref[...] = jnp.concatenate([jnp.sum(acc[2 * j + 1], axis=0, keepdims=True) for j in range(NT)], axis=-1)

        dbre = jnp.zeros((SLAB_CH, SLAB_ST), F32)
        dbim = jnp.zeros((SLAB_CH, SLAB_ST), F32)
        for k in range(N_SEG):
            are = _seg_rows(a_re, k, seg).astype(BF16)
            aim = _seg_rows(a_im, k, seg).astype(BF16)
            uk = u_ref[pl.ds(k * seg, seg), :]
            du = _dot_nt(are, bre_ref[...]) + _dot_nt(aim, bim_ref[...]) + dyp[pl.ds(k * seg, seg), :] * d_ref[...]
            du_ref[pl.ds(k * seg, seg), :] = du.astype(BF16)
            dbre = dbre + _dot_tn(uk, are)
            dbim = dbim + _dot_tn(uk, aim)
        dbre_ref[...] = dbre
        dbim_ref[...] = dbim

    scan_buf = pltpu.VMEM((NT, rows, 128), F32)
    pow_buf = pltpu.VMEM((NT, seg, 128), F32)
    return pl.pallas_call(
        body, name=name, grid=(N_SLAB,),
        in_specs=[col(OFF_UA // SLAB_CH), pl.BlockSpec((L, SLAB_CH), lambda j: (0, j)),
                  mat_b, mat_b, mat_c, mat_c, vec_s, vec_s, vec_c],
        out_specs=[pl.BlockSpec((L, SLAB_CH), lambda j: (0, j)), mat_b, mat_b, mat_c, mat_c, dlam_spec, dlam_spec, vec_c],
        out_shape=[jax.ShapeDtypeStruct((L, SSM_WIDTH), BF16),
                   jax.ShapeDtypeStruct((N_SLAB, SLAB_CH, SLAB_ST), F32),
                   jax.ShapeDtypeStruct((N_SLAB, SLAB_CH, SLAB_ST), F32),
                   jax.ShapeDtypeStruct((N_SLAB, SLAB_ST, SLAB_CH), F32),
                   jax.ShapeDtypeStruct((N_SLAB, SLAB_ST, SLAB_CH), F32),
                   jax.ShapeDtypeStruct((N_SLAB, 1, SLAB_ST), F32),
                   jax.ShapeDtypeStruct((N_SLAB, 1, SLAB_ST), F32),
                   jax.ShapeDtypeStruct((N_SLAB, 1, SLAB_CH), F32)],
        scratch_shapes=[scan_buf, scan_buf, scan_buf, scan_buf, pow_buf, pow_buf, pltpu.VMEM((L, SLAB_CH), F32)],
        compiler_params=_cp(("parallel",)),
    )(proj, dy, bre, bim, cre_t, cim_t, lam_re, lam_im, dvec)


def _glu_point(y0, pre, za, b):
    return y0 * jax.nn.sigmoid(pre + b) * jax.nn.silu(za)


def _glu_specs(L, tm):
    row = pl.BlockSpec((tm, SSM_WIDTH), lambda i: (i, 0))
    za = pl.BlockSpec((tm, SSM_WIDTH), lambda i: (i, OFF_ZA // SSM_WIDTH))
    wmat = pl.BlockSpec((SSM_WIDTH, SSM_WIDTH), lambda i: (0, 0))
    vec = pl.BlockSpec((1, SSM_WIDTH), lambda i: (0, 0))
    return row, za, wmat, vec


def _glu_fwd(ya0, proj, w, b, name):
    L = ya0.shape[0]
    tm = min(L, 512)
    row, za, wmat, vec = _glu_specs(L, tm)

    def body(y_ref, z_ref, w_ref, b_ref, o_ref):
        y0 = y_ref[...]
        pre = _dot(y0, w_ref[...])
        o_ref[...] = _glu_point(y0.astype(F32), pre, z_ref[...].astype(F32), b_ref[...]).astype(BF16)

    return pl.pallas_call(
        body, name=name, grid=(L // tm,), in_specs=[row, za, wmat, vec], out_specs=row,
        out_shape=jax.ShapeDtypeStruct((L, SSM_WIDTH), BF16), compiler_params=_cp(("parallel",)),
    )(ya0, proj, w, b)


def _glu_bwd(ya0, proj, w, b, dya, name):
    L = ya0.shape[0]
    tm = min(L, 512)
    row, za, wmat, vec = _glu_specs(L, tm)

    def body(y_ref, z_ref, w_ref, b_ref, g_ref, dy0_ref, dza_ref, dw_ref, db_ref):
        y0 = y_ref[...]
        pre = _dot(y0, w_ref[...])
        _, vjp = jax.vjp(_glu_point, y0.astype(F32), pre, z_ref[...].astype(F32), b_ref[...])
        dy0, dpre, dza, db = vjp(g_ref[...].astype(F32))
        dpb = dpre.astype(BF16)
        dy0_ref[...] = (dy0 + _dot_nt(dpb, w_ref[...])).astype(BF16)
        dza_ref[...] = dza.astype(BF16)

        @pl.when(pl.program_id(0) == 0)
        def _():
            dw_ref[...] = jnp.zeros_like(dw_ref)
            db_ref[...] = jnp.zeros_like(db_ref)

        dw_ref[...] += _dot_tn(y0, dpb)
        db_ref[...] += db

    return pl.pallas_call(
        body, name=name, grid=(L // tm,), in_specs=[row, za, wmat, vec, row],
        out_specs=[row, row, wmat, vec],
        out_shape=[jax.ShapeDtypeStruct((L, SSM_WIDTH), BF16), jax.ShapeDtypeStruct((L, SSM_WIDTH), BF16),
                   jax.ShapeDtypeStruct((SSM_WIDTH, SSM_WIDTH), F32), jax.ShapeDtypeStruct((1, SSM_WIDTH), F32)],
        compiler_params=_cp(("arbitrary",)),
    )(ya0, proj, w, b, dya)


def _sg_norm(vb, ln_w, ln_b):
    v0 = jax.nn.gelu(vb)
    mu = jnp.mean(v0, axis=-1, keepdims=True)
    var = jnp.mean(jnp.square(v0 - mu), axis=-1, keepdims=True)
    return (v0 - mu) * lax.rsqrt(var + EPS) * ln_w + ln_b


def _sg_gate(ub, mixed, zb):
    return jax.nn.gelu(ub) * mixed * jax.nn.silu(zb)


def _sg_specs():
    W = SSM_WIDTH
    blk = lambda off: pl.BlockSpec((CHUNK, W), lambda n: (n, off // W))
    out = pl.BlockSpec((CHUNK, W), lambda n: (n, 0))
    vec = pl.BlockSpec((1, W), lambda n: (0, 0))
    wsp = pl.BlockSpec((SG_HEADS, CHUNK, CHUNK), lambda n: (0, 0, 0))
    bsp = pl.BlockSpec((SG_HEADS, CHUNK, 1), lambda n: (0, 0, 0))
    return blk, out, vec, wsp, bsp


def _sg_masked(w_ref):
    t = lax.broadcasted_iota(jnp.int32, (CHUNK, CHUNK), 0)
    s = lax.broadcasted_iota(jnp.int32, (CHUNK, CHUNK), 1)
    causal = s <= t
    return causal, [jnp.where(causal, w_ref[h], 0.0).astype(BF16) for h in range(SG_HEADS)]


def _sg_mix(wm, vnb, bias_ref):
    return jnp.concatenate(
        [_dot(wm[h], vnb[:, h * CHUNK:(h + 1) * CHUNK]) + bias_ref[h] for h in range(SG_HEADS)], axis=-1)


def _sg_fwd(proj, ln_w, ln_b, w, bias, name):
    L = proj.shape[0]
    blk, out, vec, wsp, bsp = _sg_specs()

    def body(ub_ref, vb_ref, zb_ref, lw_ref, lb_ref, w_ref, bias_ref, o_ref):
        _, wm = _sg_masked(w_ref)
        vnb = _sg_norm(vb_ref[...].astype(F32), lw_ref[...], lb_ref[...]).astype(BF16)
        mixed = _sg_mix(wm, vnb, bias_ref)
        o_ref[...] = _sg_gate(ub_ref[...].astype(F32), mixed, zb_ref[...].astype(F32)).astype(BF16)

    return pl.pallas_call(
        body, name=name, grid=(L // CHUNK,),
        in_specs=[blk(OFF_UB), blk(OFF_VB), blk(OFF_ZB), vec, vec, wsp, bsp], out_specs=out,
        out_shape=jax.ShapeDtypeStruct((L, SSM_WIDTH), BF16), compiler_params=_cp(("parallel",)),
    )(proj, proj, proj, ln_w, ln_b, w, bias)


def _sg_bwd(proj, ln_w, ln_b, w, bias, dyb, name):
    L = proj.shape[0]
    blk, out, vec, wsp, bsp = _sg_specs()

    def body(ub_ref, vb_ref, zb_ref, lw_ref, lb_ref, w_ref, bias_ref, g_ref,
             dub_ref, dvb_ref, dzb_ref, dlw_ref, dlb_ref, dw_ref, dbias_ref):
        causal, wm = _sg_masked(w_ref)
        vb = vb_ref[...].astype(F32)
        vn, vjp_norm = jax.vjp(_sg_norm, vb, lw_ref[...], lb_ref[...])
        vnb = vn.astype(BF16)
        mixed = _sg_mix(wm, vnb, bias_ref)
        _, vjp_gate = jax.vjp(_sg_gate, ub_ref[...].astype(F32), mixed, zb_ref[...].astype(F32))
        dub, dmixed, dzb = vjp_gate(g_ref[...].astype(F32))
        dub_ref[...] = dub.astype(BF16)
        dzb_ref[...] = dzb.astype(BF16)

        @pl.when(pl.program_id(0) == 0)
        def _():
            dlw_ref[...] = jnp.zeros_like(dlw_ref)
            dlb_ref[...] = jnp.zeros_like(dlb_ref)
            dw_ref[...] = jnp.zeros_like(dw_ref)
            dbias_ref[...] = jnp.zeros_like(dbias_ref)

        dvn = []
        for h in range(SG_HEADS):
            dm = dmixed[:, h * CHUNK:(h + 1) * CHUNK]
            dmb = dm.astype(BF16)
            dbias_ref[h] += jnp.sum(dm, axis=-1, keepdims=True)
            dw_ref[h] += jnp.where(causal, _dot_nt(dmb, vnb[:, h * CHUNK:(h + 1) * CHUNK]), 0.0)
            dvn.append(_dot_tn(wm[h], dmb))
        dvb, dlw, dlb = vjp_norm(jnp.concatenate(dvn, axis=-1))
        dvb_ref[...] = dvb.astype(BF16)
        dlw_ref[...] += dlw
        dlb_ref[...] += dlb

    act = jax.ShapeDtypeStruct((L, SSM_WIDTH), BF16)
    return pl.pallas_call(
        body, name=name, grid=(L // CHUNK,),
        in_specs=[blk(OFF_UB), blk(OFF_VB), blk(OFF_ZB), vec, vec, wsp, bsp, out],
        out_specs=[out, out, out, vec, vec, wsp, bsp],
        out_shape=[act, act, act, jax.ShapeDtypeStruct((1, SSM_WIDTH), F32), jax.ShapeDtypeStruct((1, SSM_WIDTH), F32),
                   jax.ShapeDtypeStruct((SG_HEADS, CHUNK, CHUNK), F32), jax.ShapeDtypeStruct((SG_HEADS, CHUNK, 1), F32)],
        compiler_params=_cp(("arbitrary",)),
    )(proj, proj, proj, ln_w, ln_b, w, bias, dyb)


def _rope_tables(L):
    half = ROT_DIM // 2
    inv_freq = ROPE_THETA ** (-jnp.arange(0, ROT_DIM, 2, dtype=F32) / ROT_DIM)
    ang = jnp.arange(L, dtype=F32)[:, None] * inv_freq[None, :]
    cos, sin = jnp.cos(ang), jnp.sin(ang)
    ones = jnp.ones((L, HEAD_DIM - ROT_DIM), F32)
    cos_h = jnp.concatenate([cos, cos, ones], axis=-1)
    sin_h = jnp.concatenate([-sin, sin, 0.0 * ones], axis=-1)
    src = jnp.arange(HEAD_DIM)[:, None]
    dst = jnp.arange(HEAD_DIM)[None, :]
    p_h = (((dst < half) & (src == dst + half)) | ((dst >= half) & (dst < ROT_DIM) & (src == dst - half))).astype(F32)
    p2 = jnp.kron(jnp.eye(2, dtype=F32), p_h).astype(BF16)
    return jnp.tile(cos_h, (1, 2)), jnp.tile(sin_h, (1, 2)), p2


def _rope(t, cos, sin, p2):
    n = t.shape[1] // 128
    tb = t.astype(BF16)
    sw = jnp.concatenate([_dot(tb[:, i * 128:(i + 1) * 128], p2) for i in range(n)], axis=-1) if n > 1 else _dot(tb, p2)
    return t * jnp.tile(cos, (1, n)) + sw * jnp.tile(sin, (1, n))


def _rope_t(g, cos, sin, p2):
    n = g.shape[1] // 128
    gs = (g * jnp.tile(sin, (1, n))).astype(BF16)
    sw = jnp.concatenate([_dot_nt(gs[:, i * 128:(i + 1) * 128], p2) for i in range(n)], axis=-1) if n > 1 else _dot_nt(gs, p2)
    return g * jnp.tile(cos, (1, n)) + sw


def _lane_lo(shape):
    return (lax.broadcasted_iota(jnp.int32, shape, len(shape) - 1) % 128) < HEAD_DIM


def _dup_halves(x):
    xr = pltpu.roll(x, HEAD_DIM, 1)
    lo = _lane_lo(x.shape)
    return jnp.where(lo, x, xr), jnp.where(lo, xr, x)


def _fold_halves(d0, d1):
    f0 = d0 + pltpu.roll(d0, HEAD_DIM, 1)
    f1 = d1 + pltpu.roll(d1, HEAD_DIM, 1)
    return jnp.where(_lane_lo(d0.shape), f0, f1)


def _attn_mask():
    qi = lax.broadcasted_iota(jnp.int32, (CHUNK, 2 * CHUNK), 0)
    kj = lax.broadcasted_iota(jnp.int32, (CHUNK, 2 * CHUNK), 1)
    return qi, kj


def _attn_specs():
    qsp = pl.BlockSpec((CHUNK, 1024), lambda n: (n, OFF_Q // 1024))
    kv_cur = pl.BlockSpec((CHUNK, 256), lambda n: (n, OFF_KV // 256))
    kv_prev = pl.BlockSpec((CHUNK, 256), lambda n: (jnp.maximum(n - 1, 0), OFF_KV // 256))
    zsp = [pl.BlockSpec((CHUNK, 256), functools.partial(lambda n, q: (n, OFF_ZC // 256 + q), q=q)) for q in range(4)]
    tab_cur = pl.BlockSpec((CHUNK, 128), lambda n: (n, 0))
    tab_prev = pl.BlockSpec((CHUNK, 128), lambda n: (jnp.maximum(n - 1, 0), 0))
    p2sp = pl.BlockSpec((128, 128), lambda n: (0, 0))
    sink = pl.BlockSpec(memory_space=pltpu.SMEM)
    wide = pl.BlockSpec((CHUNK, 1024), lambda n: (n, 0))
    return qsp, kv_cur, kv_prev, zsp, tab_cur, tab_prev, p2sp, sink, wide


def _attn_prep(n, q_ref, kvc_ref, kvp_ref, cosc_ref, sinc_ref, cosp_ref, sinp_ref, p2_ref):
    p2 = p2_ref[...]
    qr = _rope(q_ref[...].astype(F32), cosc_ref[...], sinc_ref[...], p2).astype(BF16)
    kc = _rope(kvc_ref[:, 0:128].astype(F32), cosc_ref[...], sinc_ref[...], p2)
    kp = _rope(kvp_ref[:, 0:128].astype(F32), cosp_ref[...], sinp_ref[...], p2)
    k_all = jnp.concatenate([kp, kc], axis=0).astype(BF16)
    v_all = jnp.concatenate([kvp_ref[:, 128:256], kvc_ref[:, 128:256]], axis=0)
    qi, kj = _attn_mask()
    allowed = ((kj < CHUNK) & (kj > qi) & (n > 0)) | ((kj >= CHUNK) & (kj - CHUNK <= qi))
    return qr, _dup_halves(k_all), _dup_halves(v_all), allowed, _lane_lo((CHUNK, 128))


def _attn_head(qr, kd, sink_ref, h, allowed, lo):
    m, half, g = h // 2, h % 2, h // 8
    qp = qr[:, m * 128:(m + 1) * 128]
    qm = jnp.where(lo if half == 0 else ~lo, qp, jnp.zeros_like(qp))
    s = jnp.where(allowed, _dot_nt(qm, kd[g]) * (HEAD_DIM ** -0.5), NEG_INF)
    snk = sink_ref[h]
    mx = jnp.maximum(jnp.max(s, axis=-1, keepdims=True), snk)
    e = jnp.exp(s - mx)
    es = jnp.exp(snk - mx)
    inv = 1.0 / (jnp.sum(e, axis=-1, keepdims=True) + es)
    return qm, e * inv, es * inv


def _silu_gate(o, z):
    return o * jax.nn.silu(z)


def _pair_lanes(refs, m):
    return refs[m // 2][:, (m % 2) * 128:(m % 2 + 1) * 128]


def _attn_fwd(proj, sinks, tabs, name):
    L = proj.shape[0]
    cos2, sin2, p2 = tabs
    qsp, kv_cur, kv_prev, zsp, tab_cur, tab_prev, p2sp, sink, wide = _attn_specs()

    def body(q_ref, kvc_ref, kvp_ref, z0, z1, z2, z3, cosc, sinc, cosp, sinp, p2_ref, sink_ref, y_ref, o_ref):
        n = pl.program_id(0)
        qr, kd, vd, allowed, lo = _attn_prep(n, q_ref, kvc_ref, kvp_ref, cosc, sinc, cosp, sinp, p2_ref)
        probs = [_attn_head(qr, kd, sink_ref, h, allowed, lo)[1].astype(BF16) for h in range(ATT_HEADS)]
        for m in range(ATT_HEADS // 2):
            g = m // 4
            o0 = _dot(probs[2 * m], vd[g])
            o1 = _dot(probs[2 * m + 1], vd[g])
            o = jnp.where(lo, o0, o1).astype(BF16)
            o_ref[:, m * 128:(m + 1) * 128] = o
            z = _pair_lanes((z0, z1, z2, z3), m).astype(F32)
            y_ref[:, m * 128:(m + 1) * 128] = _silu_gate(o.astype(F32), z).astype(BF16)

    act = jax.ShapeDtypeStruct((L, 1024), BF16)
    return pl.pallas_call(
        body, name=name, grid=(L // CHUNK,),
        in_specs=[qsp, kv_cur, kv_prev, *zsp, tab_cur, tab_cur, tab_prev, tab_prev, p2sp, sink],
        out_specs=[wide, wide], out_shape=[act, act], compiler_params=_cp(("parallel",)),
    )(proj, proj, proj, proj, proj, proj, proj, cos2, sin2, cos2, sin2, p2, sinks)


def _attn_bwd(proj, sinks, tabs, o_att, dyc, name):
    L = proj.shape[0]
    cos2, sin2, p2 = tabs
    qsp, kv_cur, kv_prev, zsp, tab_cur, tab_prev, p2sp, sink, wide = _attn_specs()
    kvo = pl.BlockSpec((CHUNK, 256), lambda n: (n, 0))

    def body(q_ref, kvc_ref, kvp_ref, z0, z1, z2, z3, cosc, sinc, cosp, sinp, p2_ref, sink_ref, o_ref, g_ref,
             dq_ref, dz_ref, dkvc_ref, dkvp_ref, dsink_ref):
        n = pl.program_id(0)
        qr, kd, vd, allowed, lo = _attn_prep(n, q_ref, kvc_ref, kvp_ref, cosc, sinc, cosp, sinp, p2_ref)
        p2 = p2_ref[...]

        @pl.when(n == 0)
        def _():
            dsink_ref[...] = jnp.zeros_like(dsink_ref)

        dkd = [jnp.zeros((2 * CHUNK, 128), F32), jnp.zeros((2 * CHUNK, 128), F32)]
        dvd = [jnp.zeros((2 * CHUNK, 128), F32), jnp.zeros((2 * CHUNK, 128), F32)]
        probs = [_attn_head(qr, kd, sink_ref, h, allowed, lo) for h in range(ATT_HEADS)]
        for m in range(ATT_HEADS // 2):
            g = m // 4
            lanes = slice(m * 128, (m + 1) * 128)
            z = _pair_lanes((z0, z1, z2, z3), m).astype(F32)
            _, vjp = jax.vjp(_silu_gate, o_ref[:, lanes].astype(F32), z)
            do, dz = vjp(g_ref[:, lanes].astype(F32))
            dz_ref[:, lanes] = dz.astype(BF16)
            dop = do.astype(BF16)
            dq_h = []
            for half in range(2):
                h = 2 * m + half
                qm, p, ps = probs[h]
                dom = jnp.where(lo if half == 0 else ~lo, dop, jnp.zeros_like(dop))
                dp = _dot_nt(dom, vd[g])
                rs = jnp.sum(p * dp, axis=-1, keepdims=True)
                ds = (p * (dp - rs) * (HEAD_DIM ** -0.5)).astype(BF16)
                dsink_ref[h:h + 1, :] += jnp.broadcast_to(jnp.sum(-ps * rs, axis=0, keepdims=True), (1, 128))
                dq_h.append(_dot(ds, kd[g]))
                dkd[g] = dkd[g] + _dot_tn(ds, qm)
                dvd[g] = dvd[g] + _dot_tn(p.astype(BF16), dom)
            dq_ref[:, lanes] = _rope_t(jnp.where(lo, dq_h[0], dq_h[1]), cosc[...], sinc[...], p2).astype(BF16)
        dk_rot = _fold_halves(dkd[0], dkd[1])
        dv = _fold_halves(dvd[0], dvd[1])
        dkp = _rope_t(dk_rot[0:CHUNK], cosp[...], sinp[...], p2)
        dkc = _rope_t(dk_rot[CHUNK:2 * CHUNK], cosc[...], sinc[...], p2)
        dkvp_ref[...] = jnp.concatenate([dkp, dv[0:CHUNK]], axis=-1)
        dkvc_ref[...] = jnp.concatenate([dkc, dv[CHUNK:2 * CHUNK]], axis=-1)

    act = jax.ShapeDtypeStruct((L, 1024), BF16)
    kvs = jax.ShapeDtypeStruct((L, 256), F32)
    return pl.pallas_call(
        body, name=name, grid=(L // CHUNK,),
        in_specs=[qsp, kv_cur, kv_prev, *zsp, tab_cur, tab_cur, tab_prev, tab_prev, p2sp, sink, wide, wide],
        out_specs=[wide, wide, kvo, kvo, pl.BlockSpec((ATT_HEADS, 128), lambda n: (0, 0))],
        out_shape=[act, act, kvs, kvs, jax.ShapeDtypeStruct((ATT_HEADS, 128), F32)],
        compiler_params=_cp(("arbitrary",)),
    )(proj, proj, proj, proj, proj, proj, proj, cos2, sin2, cos2, sin2, p2, sinks, o_att, dyc)


MERGE_TN = 256


def _merge_point(ta, tb, tc, ga, gb, gc):
    return jax.nn.sigmoid(ga) * ta + jax.nn.sigmoid(gb) * tb + jax.nn.sigmoid(gc) * tc


def _merge_specs(tm):
    nj = D_MODEL // MERGE_TN
    t = pl.BlockSpec((tm, MERGE_TN), lambda i, j: (i, j))
    gates = [pl.BlockSpec((tm, MERGE_TN), functools.partial(lambda i, j, b: (i, OFF_G // MERGE_TN + b * nj + j), b=b))
             for b in range(3)]
    return t, gates, nj


def _merge_fwd(ta, tb, tc, proj, name):
    L = ta.shape[0]
    tm = min(L, 1024)
    t, gates, nj = _merge_specs(tm)

    def body(ta_ref, tb_ref, tc_ref, ga_ref, gb_ref, gc_ref, o_ref):
        f = lambda r: r[...].astype(F32)
        o_ref[...] = _merge_point(f(ta_ref), f(tb_ref), f(tc_ref), f(ga_ref), f(gb_ref), f(gc_ref)).astype(BF16)

    return pl.pallas_call(
        body, name=name, grid=(L // tm, nj), in_specs=[t, t, t, *gates], out_specs=t,
        out_shape=jax.ShapeDtypeStruct((L, D_MODEL), BF16), compiler_params=_cp(("parallel", "parallel")),
    )(ta, tb, tc, proj, proj, proj)


def _merge_bwd(ta, tb, tc, proj, dm, name):
    L = ta.shape[0]
    tm = min(L, 1024)
    t, gates, nj = _merge_specs(tm)

    def body(ta_ref, tb_ref, tc_ref, ga_ref, gb_ref, gc_ref, dm_ref, dta_ref, dtb_ref, dtc_ref, dga_ref, dgb_ref, dgc_ref):
        f = lambda r: r[...].astype(F32)
        _, vjp = jax.vjp(_merge_point, f(ta_ref), f(tb_ref), f(tc_ref), f(ga_ref), f(gb_ref), f(gc_ref))
        outs = vjp(f(dm_ref))
        for r, v in zip((dta_ref, dtb_ref, dtc_ref, dga_ref, dgb_ref, dgc_ref), outs):
            r[...] = v.astype(BF16)

    act = jax.ShapeDtypeStruct((L, D_MODEL), BF16)
    return pl.pallas_call(
        body, name=name, grid=(L // tm, nj), in_specs=[t, t, t, *gates, t],
        out_specs=[t] * 6, out_shape=[act] * 6,
        compiler_params=_cp(("parallel", "parallel")),
    )(ta, tb, tc, proj, proj, proj, dm)


GRAD_DT = BF16
SMALL = ("norm_w", "ssm_a_re", "ssm_a_im", "ssm_log_dt", "ssm_b_re", "ssm_b_im", "ssm_c_re", "ssm_c_im", "ssm_d",
         "ssm_glu_b", "sg_ln_w", "sg_ln_b", "sg_w", "sg_b", "attn_sinks")
G8 = SSM_GROUPS // N_SLAB


def _diag_mask(rows_per_group, cols_per_group):
    r = jnp.arange(G8 * rows_per_group)[:, None] // rows_per_group
    c = jnp.arange(G8 * cols_per_group)[None, :] // cols_per_group
    return r == c


def _slab_b(bb_t):
    x = bb_t.transpose(1, 0, 2).reshape(N_SLAB, SLAB_CH, SSM_STATE)
    return jnp.where(_diag_mask(SSM_GROUP, SSM_STATE), jnp.tile(x, (1, 1, G8)), 0)


def _unslab_b(d):
    x = jnp.where(_diag_mask(SSM_GROUP, SSM_STATE), d, 0).reshape(N_SLAB, SLAB_CH, G8, SSM_STATE).sum(axis=2)
    return x.reshape(SSM_GROUPS, SSM_GROUP, SSM_STATE).transpose(1, 0, 2)


def _slab_c(c):
    x = c.transpose(0, 2, 1).reshape(N_SLAB, SLAB_ST, SSM_GROUP)
    return jnp.where(_diag_mask(SSM_STATE, SSM_GROUP), jnp.tile(x, (1, 1, G8)), 0)


def _unslab_c(d):
    x = jnp.where(_diag_mask(SSM_STATE, SSM_GROUP), d, 0).reshape(N_SLAB, SLAB_ST, G8, SSM_GROUP).sum(axis=2)
    return x.reshape(SSM_GROUPS, SSM_STATE, SSM_GROUP).transpose(0, 2, 1)


def _s5_prep(p, tag):
    bt_re = p["ssm_b_re"].transpose(2, 0, 1)
    bt_im = p["ssm_b_im"].transpose(2, 0, 1)
    raw = (p["ssm_a_re"], p["ssm_a_im"], p["ssm_log_dt"][:, None], bt_re, bt_im)
    lr, li, bbr, bbi = _s5_params_fwd(*raw, name=f"s5_params_{tag}")
    ops = (_slab_b(bbr).astype(BF16), _slab_b(bbi).astype(BF16),
           _slab_c(p["ssm_c_re"]).astype(BF16), _slab_c(p["ssm_c_im"]).astype(BF16),
           jnp.broadcast_to(lr.reshape(N_SLAB, 1, SLAB_ST), (N_SLAB, SUB, SLAB_ST)),
           jnp.broadcast_to(li.reshape(N_SLAB, 1, SLAB_ST), (N_SLAB, SUB, SLAB_ST)),
           p["ssm_d"].reshape(N_SLAB, 1, SLAB_CH))
    return raw, ops


def _layer_fwd(x, p, w, tabs, tag, s5=None, proj_of=None, after_proj=None):
    L = x.shape[0]
    h = _rms_fwd(x, p["norm_w"][None], f"rms_fwd_{tag}")
    if proj_of is not None:
        proj = proj_of(h)
    else:
        proj = _mm(h, w["win_t"], "nt", BF16, L, PROJ_TN, D_MODEL, f"in_proj_{tag}")
    if after_proj is not None:
        w = after_proj(proj)
    s5_raw, s5_ops = s5 if s5 is not None else _s5_prep(p, tag)
    ya0 = _s5_fwd(proj, *s5_ops, name=f"s5_fwd_{tag}")
    ya = _glu_fwd(ya0, proj, w["glu"], p["ssm_glu_b"][None], f"glu_fwd_{tag}")
    yb = _sg_fwd(proj, p["sg_ln_w"][None], p["sg_ln_b"][None], p["sg_w"], p["sg_b"][:, :, None], f"sg_fwd_{tag}")
    yc, o_att = _attn_fwd(proj, p["attn_sinks"], tabs, f"attn_fwd_{tag}")
    ta = _mm(ya, w["wba_t"], "nt", BF16, 1024, 1024, 1024, f"branch_a_{tag}")
    tb = _mm(yb, w["wbb_t"], "nt", BF16, 1024, 1024, 1024, f"branch_b_{tag}")
    tc = _mm(yc, w["wbc_t"], "nt", BF16, 1024, 1024, 1024, f"branch_c_{tag}")
    merged = _merge_fwd(ta, tb, tc, proj, f"merge_fwd_{tag}")
    x_new = _mm(merged, w["wout"], "nn", F32, 1024, 512, D_MODEL, f"out_proj_{tag}", res=x)
    saved = dict(x=x, h=h, proj=proj, s5_raw=s5_raw, s5_ops=s5_ops, ya0=ya0, ya=ya, yb=yb, yc=yc, o_att=o_att,
                 ta=ta, tb=tb, tc=tc, merged=merged)
    return x_new, saved


def _layer_bwd(dx_out, p, w, tabs, s, tag, first_after=None, after_merge=None, before_win=None, after_win=None):
    L = dx_out.shape[0]
    proj = s["proj"]
    big, small = {}, {}
    dmerged = _mm(dx_out, w["wout"], "nt", BF16, 1024, 512, D_MODEL, f"d_merged_{tag}", after=first_after)
    big["wout"] = _mm(s["merged"], dx_out, "tn", GRAD_DT, 512, 1024, L, f"d_wout_{tag}")
    dta, dtb, dtc, dga, dgb, dgc = _merge_bwd(s["ta"], s["tb"], s["tc"], proj, dmerged, f"merge_bwd_{tag}")
    tok = after_merge(dga) if after_merge is not None else None
    dy = {}
    for br, dt in (("a", dta), ("b", dtb), ("c", dtc)):
        dy[br] = _mm(dt, w[f"wb{br}_t"], "nn", BF16, 1024, 1024, D_MODEL, f"d_y{br}_{tag}", after=tok)
        big[f"wb{br}_t"] = _mm(dt, s[f"y{br}"], "tn", GRAD_DT, 512, 1024, L, f"d_wb{br}_{tag}")

    dq, dzc, dkvc, dkvp, dsink = _attn_bwd(proj, p["attn_sinks"], tabs, s["o_att"], dy["c"], f"attn_bwd_{tag}")
    dkv = dkvc + jnp.concatenate([dkvp[CHUNK:], jnp.zeros((CHUNK, 256), F32)], axis=0)
    small["attn_sinks"] = dsink[:, 0]

    dub, dvb, dzb, dlw, dlb, dsgw, dsgb = _sg_bwd(
        proj, p["sg_ln_w"][None], p["sg_ln_b"][None], p["sg_w"], p["sg_b"][:, :, None], dy["b"], f"sg_bwd_{tag}")
    small.update(sg_ln_w=dlw[0], sg_ln_b=dlb[0], sg_w=dsgw, sg_b=dsgb[:, :, 0])

    dya0, dza, dglu, dglub = _glu_bwd(s["ya0"], proj, w["glu"], p["ssm_glu_b"][None], dy["a"], f"glu_bwd_{tag}")
    big["glu"] = dglu.astype(GRAD_DT)
    small["ssm_glu_b"] = dglub[0]

    dua, dbre, dbim, dcre, dcim, dlr, dli, dd = _s5_bwd(proj, dya0, *s["s5_ops"], name=f"s5_bwd_{tag}")
    da_re, da_im, dlog_dt, dbt_re, dbt_im = _s5_params_bwd(
        *s["s5_raw"], dlr.reshape(SSM_GROUPS, SSM_STATE), dli.reshape(SSM_GROUPS, SSM_STATE),
        _unslab_b(dbre), _unslab_b(dbim), name=f"s5_params_bwd_{tag}")
    small.update(ssm_a_re=da_re, ssm_a_im=da_im, ssm_log_dt=dlog_dt[:, 0],
                 ssm_bt_re=dbt_re, ssm_bt_im=dbt_im,
                 ssm_c_re=_unslab_c(dcre), ssm_c_im=_unslab_c(dcim), ssm_d=dd.reshape(SSM_WIDTH))

    dproj = jnp.concatenate([dua, dza, dub, dvb, dzb, dq, dkv.astype(BF16), dzc, dga, dgb, dgc], axis=-1)
    tok = before_win(big) if before_win is not None else None
    big["win_t"] = _mm(dproj, s["h"], "tn", GRAD_DT, 256, D_MODEL, L, f"d_win_{tag}", after=tok)
    tok = after_win(big) if after_win is not None else None
    dh = _mm(dproj, w["win_t"], "nn", F32, L, D_MODEL, 256, f"d_h_{tag}", after=tok)
    dx_in, dnw = _rms_bwd(s["x"], p["norm_w"][None], dh, dx_out, f"rms_bwd_{tag}")
    small["norm_w"] = dnw[0]
    return dx_in, big, small


def _local_step(x, tgt, small_p, final_w, big_w):
    L = x.shape[0]
    tabs = _rope_tables(L)
    saved = []
    for l in range(DEPTH):
        x, s = _layer_fwd(x, small_p[l], big_w[l], tabs, f"l{l}")
        saved.append(s)
    loss_acc, dx, dfw = _final(x, final_w[None], tgt, "final_norm_loss")
    big_g, small_g = [None] * DEPTH, [None] * DEPTH
    for l in reversed(range(DEPTH)):
        dx, big_g[l], small_g[l] = _layer_bwd(dx, small_p[l], big_w[l], tabs, saved[l], f"l{l}")
    return loss_acc[0, 0], dx, dfw[0], big_g, small_g


MESH = pl.DeviceIdType.MESH
ANY = pl.BlockSpec(memory_space=pl.ANY)
ROW_ALIGN = 16


def _place():
    return lax.axis_index("x"), lax.axis_index("y"), lax.axis_index("c")


HBM = pl.BlockSpec(memory_space=pltpu.HBM)
SEM = pl.BlockSpec(memory_space=pltpu.SEMAPHORE)
EFFECT = pltpu.SideEffectType.DATAFLOW_SIDE_EFFECTING


def _split_start(srcs, lands, n_copies, copies, name, after=None):
    n, m, k = len(srcs), len(lands), n_copies
    extra = [] if after is None else [after]

    def body(*refs):
        src_refs, land_refs = refs[:n], refs[n:n + m]
        sems = refs[n + m + len(extra):]
        send_sems, recv_sems, token = sems[:k], sems[k:2 * k], refs[-1]
        for cp in copies(src_refs, land_refs, send_sems, recv_sems):
            cp.start()
        token[...] = jnp.zeros_like(token)

    ops = list(srcs) + list(lands)
    outs = pl.pallas_call(
        body, name=name,
        out_shape=(*[pltpu.SemaphoreType.DMA(())] * (2 * k),
                   *[pltpu.HBM(a.shape, a.dtype) for a in ops], jax.ShapeDtypeStruct((8, 128), F32)),
        in_specs=[HBM] * (n + m) + [ANY] * len(extra),
        out_specs=(*[SEM] * (2 * k), *[HBM] * (n + m), pl.BlockSpec(memory_space=pltpu.VMEM)),
        input_output_aliases={i: 2 * k + i for i in range(n + m)},
        compiler_params=pltpu.CompilerParams(has_side_effects=EFFECT),
    )(*[pltpu.with_memory_space_constraint(a, pltpu.HBM) for a in ops], *extra)
    return (list(outs[:k]), list(outs[k:2 * k]), list(outs[2 * k:2 * k + n]), list(outs[2 * k + n:2 * k + n + m]),
            outs[-1])


def _split_wait(send_sems, recv_sems, srcs, lands, after, copies, name):
    n, m, k = len(srcs), len(lands), len(send_sems)
    after = list(after) if isinstance(after, (list, tuple)) else [after]

    def body(*refs):
        src_refs, land_refs = refs[:n], refs[n:n + m]
        for cp in copies(src_refs, land_refs, refs[n + m:n + m + k], refs[n + m + k:n + m + 2 * k]):
            cp.wait_send()
            cp.wait_recv()

    ops = list(srcs) + list(lands)
    outs = pl.pallas_call(
        body, name=name,
        out_shape=tuple(pltpu.HBM(a.shape, a.dtype) for a in ops),
        in_specs=[HBM] * (n + m) + [SEM] * (2 * k) + [ANY] * len(after),
        out_specs=tuple([HBM] * (n + m)),
        input_output_aliases={i: i for i in range(n + m)},
        compiler_params=pltpu.CompilerParams(has_side_effects=EFFECT),
    )(*ops, *send_sems, *recv_sems, *after)
    return list(outs[:n]), list(outs[n:])


def _ag_rows(land_ref, px, py, pc):
    r = land_ref.shape[0] // N_DEV
    start = pl.multiple_of((4 * px + 2 * py + pc) * r, ROW_ALIGN)
    return land_ref.at[pl.ds(start, r), :]


def _ag_copies_to(which):
    def copies(src_refs, land_refs, send_sems, recv_sems):
        x, y, c = _place()
        peers = [(x, y, 1 - c), (1 - x, y, c), (x, 1 - y, c), (1 - x, 1 - y, c)]
        return [pltpu.make_async_remote_copy(
            src_ref=_ag_rows(land_refs[a], x, y, c), dst_ref=_ag_rows(land_refs[a], x, y, c),
            send_sem=send_sems[len(which) * a + k], recv_sem=recv_sems[len(which) * a + k],
            device_id=peers[p], device_id_type=MESH)
            for a in range(len(land_refs)) for k, p in enumerate(which)]
    return copies


_ag_copies = _ag_copies_to((0, 1, 2, 3))
_ag_copies_near = _ag_copies_to((0, 1, 2))
_ag_copies_far = _ag_copies_to((3,))


def _ag_forward(lands, name, which=(0, 1, 2)):
    n = len(lands)

    def body(*refs):
        land_refs = refs[n:2 * n]
        send_sems, recv_sems = refs[2 * n:]
        x, y, c = _place()
        chips = [(1 - x, y), (x, 1 - y), (1 - x, 1 - y)]

        def copy(a, k, pc):
            px, py = chips[which[k]]
            return pltpu.make_async_remote_copy(
                src_ref=_ag_rows(land_refs[a], px, py, pc), dst_ref=_ag_rows(land_refs[a], px, py, pc),
                send_sem=send_sems.at[a, k], recv_sem=recv_sems.at[a, k], device_id=(x, y, 1 - c), device_id_type=MESH)

        passed = [copy(a, k, c) for a in range(n) for k in range(len(which))]
        for cp in passed:
            cp.start()
        for a in range(n):
            for k in range(len(which)):
                copy(a, k, 1 - c).wait_recv()
        for cp in passed:
            cp.wait_send()

    sems = pltpu.SemaphoreType.DMA((n, len(which)))
    return pl.pallas_call(
        body, name=name,
        in_specs=[ANY] * n, out_specs=[ANY] * n,
        out_shape=[jax.ShapeDtypeStruct(l.shape, l.dtype) for l in lands],
        input_output_aliases={i: i for i in range(n)},
        scratch_shapes=[sems, sems],
    )(*lands)


def _allgather_place(shards):
    x, y, c = _place()
    return [lax.dynamic_update_slice(lax.empty((N_DEV * s.shape[0], s.shape[1]), s.dtype), s,
                                     ((4 * x + 2 * y + c) * s.shape[0], 0)) for s in shards]


def _allgather_start(lands, name, after=None):
    return _split_start([], lands, 4 * len(lands), _ag_copies, name + "_start", after=after)


def _allgather_finish(started, after, name):
    send_sems, recv_sems, _, lands, _ = started
    _, lands = _split_wait(send_sems, recv_sems, [], lands, after, _ag_copies, name + "_wait")
    return list(_ag_forward(lands, name + "_forward"))


def _rs_swap_cores(grads, name):
    n = len(grads)

    def body(*refs):
        ins, outs = refs[:n], refs[n:2 * n]
        send_sems, recv_sems = refs[2 * n:]
        x, y, c = _place()
        cps = []
        for a in range(n):
            r = ins[a].shape[0] // N_DEV
            for q in range(4):
                start = pl.multiple_of((2 * q + 1 - c) * r, ROW_ALIGN)
                cps.append(pltpu.make_async_remote_copy(
                    src_ref=ins[a].at[pl.ds(start, r), :], dst_ref=outs[a].at[q],
                    send_sem=send_sems.at[a, q], recv_sem=recv_sems.at[a, q],
                    device_id=(x, y, 1 - c), device_id_type=MESH))
        for cp in cps:
            cp.start()
        for cp in cps:
            cp.wait()

    return pl.pallas_call(
        body, name=name, in_specs=[ANY] * n, out_specs=[ANY] * n,
        out_shape=[jax.ShapeDtypeStruct((4, g.shape[0] // N_DEV, g.shape[1]), g.dtype) for g in grads],
        scratch_shapes=[pltpu.SemaphoreType.DMA((n, 4)), pltpu.SemaphoreType.DMA((n, 4))],
    )(*grads)


def _rs_chip_copies(sum_refs, land_refs, send_sems, recv_sems):
    x, y, c = _place()
    chips = [(1 - x, y), (x, 1 - y), (1 - x, 1 - y)]
    return [pltpu.make_async_remote_copy(
        src_ref=sum_refs[a].at[2 * px + py], dst_ref=land_refs[a].at[2 * x + y],
        send_sem=send_sems[3 * a + j], recv_sem=recv_sems[3 * a + j], device_id=(px, py, c), device_id_type=MESH)
        for a in range(len(sum_refs)) for j, (px, py) in enumerate(chips)]


def _row_tile(r):
    return max(t for t in range(ROW_ALIGN, min(r, 1024) + 1, ROW_ALIGN) if r % t == 0)


def _rs_add_cores(grad, recv, cidx, name):
    r, cols = recv.shape[1], recv.shape[2]
    tr = _row_tile(r)
    nb = r // tr

    def body(c_ref, g_ref, r_ref, o_ref):
        o_ref[...] = (g_ref[...].astype(F32) + r_ref[...].astype(F32)).astype(o_ref.dtype)

    return pl.pallas_call(
        body, name=name,
        grid_spec=pltpu.PrefetchScalarGridSpec(
            num_scalar_prefetch=1, grid=(4, nb),
            in_specs=[pl.BlockSpec((tr, cols), lambda q, i, c_ref: ((2 * q + c_ref[0]) * nb + i, 0)),
                      pl.BlockSpec((None, tr, cols), lambda q, i, c_ref: (q, i, 0))],
            out_specs=pl.BlockSpec((None, tr, cols), lambda q, i, c_ref: (q, i, 0))),
        out_shape=jax.ShapeDtypeStruct(recv.shape, recv.dtype),
        compiler_params=_cp(("parallel", "parallel")),
    )(cidx, grad, recv)


def _rs_add_chips(own, recv, slots, name):
    r, cols = recv.shape[1], recv.shape[2]
    tr = _row_tile(r)

    def body(s_ref, o_ref, r0_ref, r1_ref, r2_ref, out_ref):
        acc = o_ref[...].astype(F32)
        for ref in (r0_ref, r1_ref, r2_ref):
            acc = acc + ref[...].astype(F32)
        out_ref[...] = acc

    pick = lambda k: pl.BlockSpec((None, tr, cols), functools.partial(lambda i, s_ref, k: (s_ref[k], i, 0), k=k))
    return pl.pallas_call(
        body, name=name,
        grid_spec=pltpu.PrefetchScalarGridSpec(
            num_scalar_prefetch=1, grid=(r // tr,),
            in_specs=[pick(0), pick(1), pick(2), pick(3)],
            out_specs=pl.BlockSpec((tr, cols), lambda i, s_ref: (i, 0))),
        out_shape=jax.ShapeDtypeStruct((r, cols), F32),
        compiler_params=_cp(("parallel",)),
    )(slots, own, recv, recv, recv)


def _rs_core_copies(grad_refs, land_refs, send_sems, recv_sems):
    x, y, c = _place()
    cps = []
    for a in range(len(grad_refs)):
        r = grad_refs[a].shape[0] // N_DEV
        for q in range(4):
            start = pl.multiple_of((2 * q + 1 - c) * r, ROW_ALIGN)
            cps.append(pltpu.make_async_remote_copy(
                src_ref=grad_refs[a].at[pl.ds(start, r), :], dst_ref=land_refs[a].at[q],
                send_sem=send_sems[4 * a + q], recv_sem=recv_sems[4 * a + q],
                device_id=(x, y, 1 - c), device_id_type=MESH))
    return cps


def _reduce_scatter_chips_start(grads, recv, tag):
    cidx = lax.axis_index("c").astype(jnp.int32)[None]
    sums = [_rs_add_cores(g, rv, cidx, f"rs_add_cores_{tag}_{i}") for i, (g, rv) in enumerate(zip(grads, recv))]
    lands = [lax.empty(s.shape, s.dtype) for s in sums]
    return _split_start(sums, lands, 3 * len(sums), _rs_chip_copies, f"rs_chips_{tag}_start")


def _reduce_scatter_start(grads, tag):
    return _reduce_scatter_chips_start(grads, _rs_swap_cores(grads, f"rs_swap_cores_{tag}"), tag)


def _reduce_scatter_cores_start(grads, tag):
    lands = [lax.empty((4, g.shape[0] // N_DEV, g.shape[1]), g.dtype) for g in grads]
    return _split_start(grads, lands, 4 * len(grads), _rs_core_copies, f"rs_cores_{tag}_start")


def _reduce_scatter_cores_finish(started, after, tag):
    send_sems, recv_sems, grads, lands, _ = started
    grads, recv = _split_wait(send_sems, recv_sems, grads, lands, after, _rs_core_copies, f"rs_cores_{tag}_wait")
    return _reduce_scatter_chips_start(grads, recv, tag)


def _reduce_scatter_finish(started, after, tag):
    send_sems, recv_sems, sums, lands, _ = started
    sums, lands = _split_wait(send_sems, recv_sems, sums, lands, after, _rs_chip_copies, f"rs_chips_{tag}_wait")
    x, y = lax.axis_index("x"), lax.axis_index("y")
    slots = jnp.stack([2 * x + y, 2 * (1 - x) + y, 2 * x + 1 - y, 2 * (1 - x) + 1 - y]).astype(jnp.int32)
    return [_rs_add_chips(s, l, slots, f"rs_add_chips_{tag}_{i}") for i, (s, l) in enumerate(zip(sums, lands))]


def _allreduce_small(packs, name, after=()):
    n = len(packs)
    after = list(after)
    assert all(p.shape[0] % (8 * N_DEV) == 0 for p in packs)

    def body(*refs):
        p_refs = refs[:n]
        refs = refs[n + len(after):]
        o_refs, part_refs = refs[:n], refs[n:2 * n]
        send1, recv1, send2, recv2 = refs[2 * n:]
        x, y, c = _place()
        me = 4 * x + 2 * y + c

        def block(ref, d):
            rs = ref.shape[0] // N_DEV
            return ref.at[pl.ds(pl.multiple_of(d * rs, 8), rs), :]

        peers = [(1 - x if k & 4 else x, 1 - y if k & 2 else y, 1 - c if k & 1 else c) for k in range(1, N_DEV)]
        scatter = [pltpu.make_async_remote_copy(
            src_ref=block(p_refs[a], 4 * px + 2 * py + pc), dst_ref=part_refs[a].at[me],
            send_sem=send1.at[a, k], recv_sem=recv1.at[a, k], device_id=(px, py, pc), device_id_type=MESH)
            for a in range(n) for k, (px, py, pc) in enumerate(peers)]
        for cp in scatter:
            cp.start()
        for a in range(n):
            part_refs[a][me] = block(p_refs[a], me)[...]
        for cp in scatter:
            cp.wait()
        for a in range(n):
            acc = part_refs[a][0]
            for d in range(1, N_DEV):
                acc = acc + part_refs[a][d]
            block(o_refs[a], me)[...] = acc
        gather = [pltpu.make_async_remote_copy(
            src_ref=block(o_refs[a], me), dst_ref=block(o_refs[a], me), send_sem=send2.at[a, k], recv_sem=recv2.at[a, k],
            device_id=peer, device_id_type=MESH) for a in range(n) for k, peer in enumerate(peers)]
        for cp in gather:
            cp.start()
        for a in range(n):
            for k, (px, py, pc) in enumerate(peers):
                theirs = block(o_refs[a], 4 * px + 2 * py + pc)
                pltpu.make_async_remote_copy(
                    src_ref=theirs, dst_ref=theirs, send_sem=send2.at[a, k], recv_sem=recv2.at[a, k],
                    device_id=(px, py, pc), device_id_type=MESH).wait_recv()
        for cp in gather:
            cp.wait_send()

    sems = pltpu.SemaphoreType.DMA((n, N_DEV - 1))
    vmem = pl.BlockSpec(memory_space=pltpu.VMEM)
    return pl.pallas_call(
        body, name=name,
        in_specs=[vmem] * n + [ANY] * len(after), out_specs=[vmem] * n,
        out_shape=[jax.ShapeDtypeStruct(p.shape, F32) for p in packs],
        scratch_shapes=[pltpu.VMEM((N_DEV, p.shape[0] // N_DEV, p.shape[1]), F32) for p in packs] + [sems] * 4,
        compiler_params=pltpu.CompilerParams(vmem_limit_bytes=VMEM_LIMIT),
    )(*packs, *after)


ADAM_TILE_BYTES = 2 * 1024 * 1024


def _adam_tiles(rows, cols):
    tc = cols // 2 if cols % 256 == 0 and cols >= 2048 else cols
    tr = max(t for t in range(8, rows + 1, 8) if rows % t == 0 and t * max(tc, 128) * 4 <= ADAM_TILE_BYTES) \
        if rows % 8 == 0 else rows
    return tr, tc


def _adam_math(w, g, m, v):
    nm = ADAM_B1 * m + (1.0 - ADAM_B1) * g
    nv = ADAM_B2 * v + (1.0 - ADAM_B2) * jnp.square(g)
    c1 = 1.0 - ADAM_B1 ** ADAM_STEP
    c2 = 1.0 - ADAM_B2 ** ADAM_STEP
    return -ADAM_LR * ((nm / c1) / (jnp.sqrt(nv / c2) + ADAM_EPS) + ADAM_WD * w), nm, nv


def _adamw_layer(w, g, m, v, layer, carry, name):
    _, rows, cols = w.shape
    tr, tc = _adam_tiles(rows, cols)

    def body(w_ref, g_ref, m_ref, v_ref, *rest):
        go_ref, d_ref, nm_ref, nv_ref = rest[-4:]
        gv = g_ref[...]
        go_ref[...] = gv
        d_ref[...], nm_ref[...], nv_ref[...] = _adam_math(w_ref[...], gv, m_ref[...], v_ref[...])

    blk = pl.BlockSpec((None, tr, tc), lambda i, j: (layer, i, j))
    flat = pl.BlockSpec((tr, tc), lambda i, j: (i, j))
    sh = jax.ShapeDtypeStruct(w.shape, F32)
    carry = [] if carry is None else list(carry)
    return pl.pallas_call(
        body, name=name, grid=(rows // tr, cols // tc),
        in_specs=[blk, flat, blk, blk] + [ANY] * len(carry), out_specs=[blk] * 4, out_shape=[sh] * 4,
        input_output_aliases={4 + k: k for k in range(len(carry))},
        compiler_params=_cp(("parallel", "parallel")),
    )(w, g, m, v, *carry)


def _adamw(w, g, m, v, name):
    shape = w.shape
    rows, cols = shape[-2:]
    lead = shape[:-2]
    nl = math.prod(lead)
    tr, tc = _adam_tiles(rows, cols)

    def body(w_ref, g_ref, m_ref, v_ref, d_ref, nm_ref, nv_ref):
        d_ref[...], nm_ref[...], nv_ref[...] = _adam_math(w_ref[...], g_ref[...], m_ref[...], v_ref[...])

    def index(b, i, j):
        return (*jnp.unravel_index(b, lead), i, j) if lead else (i, j)

    blk = pl.BlockSpec((*[None] * len(lead), tr, tc), index)
    sh = jax.ShapeDtypeStruct(shape, F32)
    return pl.pallas_call(
        body, name=name, grid=(nl, rows // tr, cols // tc), in_specs=[blk] * 4, out_specs=[blk] * 3,
        out_shape=[sh] * 3, compiler_params=_cp(("parallel", "parallel", "parallel")),
    )(w, g, m, v)


WEIGHTS = ("norm_w", "w_in", "ssm_a_re", "ssm_a_im", "ssm_log_dt", "ssm_b_re", "ssm_b_im", "ssm_c_re", "ssm_c_im",
           "ssm_d", "ssm_glu_w", "ssm_glu_b", "sg_ln_w", "sg_ln_b", "sg_w", "sg_b", "attn_sinks",
           "w_branch_a", "w_branch_b", "w_branch_c", "w_out", "final_norm_w")
BIG = ("w_in", "ssm_glu_w", "w_branch_a", "w_branch_b", "w_branch_c", "w_out")
BIG_KEY = {"w_in": ("win_t", True), "ssm_glu_w": ("glu", False), "w_branch_a": ("wba_t", True),
           "w_branch_b": ("wbb_t", True), "w_branch_c": ("wbc_t", True), "w_out": ("wout", False)}
VIEWS = {"w_in": (1, 2), "ssm_b_re": (2, 3), "ssm_b_im": (2, 3)}
PACKS = (
    (64, (("ssm_a_re",), ("ssm_a_im",), ("ssm_c_re",), ("ssm_c_im",), ("ssm_b_re",), ("ssm_b_im",))),
    (128, (("sg_w",),)),
    (1024, (("ssm_d", "ssm_glu_b", "sg_ln_w", "sg_ln_b"), ("norm_w", "final_norm_w", "sg_b"), ("ssm_log_dt", "attn_sinks"))),
)
PACK_ROWS = 8 * N_DEV


def _view(n, a):
    return jnp.swapaxes(a, *VIEWS[n]) if n in VIEWS else a


def _group_rows(arrs, cols):
    return -(-sum(-(-a.size // cols) for a in arrs) // 8) * 8


def _vec_moves(pack_ref, refs, to_pack):
    d, gb, lw, lb, nw, fw, sb, ld, sk = refs
    full = (slice(None), slice(None))
    moves = [((slice(2 * i, 2 * i + 2), slice(None)), r, full) for i, r in enumerate((d, gb, lw, lb))]
    moves += [((slice(8, 10), slice(None)), nw, (slice(None), slice(0, 1024))),
              ((slice(10, 12), slice(None)), nw, (slice(None), slice(1024, 2048))),
              ((slice(12, 13), slice(None)), fw, (slice(None), slice(0, 1024))),
              ((slice(13, 14), slice(None)), fw, (slice(None), slice(1024, 2048))),
              ((slice(16, 32), slice(0, 128)), sb, full),
              ((slice(32, 34), slice(0, 64)), ld, full),
              ((slice(34, 36), slice(0, 16)), sk, full)]
    for where, ref, part in moves:
        if to_pack:
            pack_ref[where] = ref[part]
        else:
            ref[part] = pack_ref[where]


def _vec_shapes(arrs):
    d, gb, lw, lb, nw, fw, sb, ld, sk = arrs
    return [d, gb, lw, lb, nw, fw.reshape(1, -1), sb.reshape(-1, sb.shape[-1]), ld, sk]


def _vec_pack(arrs, name):
    def body(*refs):
        refs[-1][...] = jnp.zeros_like(refs[-1])
        _vec_moves(refs[-1], refs[:-1], True)

    return pl.pallas_call(body, name=name, out_shape=jax.ShapeDtypeStruct((PACK_ROWS, 1024), F32))(*_vec_shapes(arrs))


def _vec_unpack(pack, like, name):
    shaped = _vec_shapes(like)

    def body(pack_ref, *refs):
        _vec_moves(pack_ref, refs, False)

    outs = pl.pallas_call(body, name=name, out_shape=[jax.ShapeDtypeStruct(a.shape, F32) for a in shaped])(pack)
    return [o.reshape(a.shape) for o, a in zip(outs, like)]


def _pack(groups, cols, name):
    if cols == 1024:
        return _vec_pack([a for arrs in groups for a in arrs], name)
    parts = []
    for arrs in groups:
        if len(arrs) == 1 and arrs[0].shape[-1] == cols and arrs[0].size % (8 * cols) == 0:
            parts.append(arrs[0].reshape(-1, cols))
            continue
        flat = [jnp.pad(a.reshape(-1), (0, -a.size % cols)) for a in arrs]
        flat = jnp.concatenate(flat) if len(flat) > 1 else flat[0]
        nrow = _group_rows(arrs, cols)
        parts.append(jnp.pad(flat, (0, nrow * cols - flat.shape[0])).reshape(nrow, cols))
    pad = -sum(p.shape[0] for p in parts) % PACK_ROWS
    if pad:
        parts.append(jnp.zeros((pad, cols), F32))
    return jnp.concatenate(parts, axis=0)


def _unpack(pack, groups, name):
    cols = pack.shape[1]
    if cols == 1024:
        return _vec_unpack(pack, [a for arrs in groups for a in arrs], name)
    out, row = [], 0
    for arrs in groups:
        nrow = _group_rows(arrs, cols)
        rows = pack[row:row + nrow]
        row += nrow
        if len(arrs) == 1 and arrs[0].shape[-1] == cols and arrs[0].size == nrow * cols:
            out.append(rows.reshape(arrs[0].shape))
            continue
        flat, off = rows.reshape(-1), 0
        for a in arrs:
            out.append(flat[off:off + a.size].reshape(a.shape))
            off += -(-a.size // cols) * cols
    return out


def kernel(x, norm_w, w_in, ssm_a_re, ssm_a_im, ssm_log_dt, ssm_b_re, ssm_b_im, ssm_c_re, ssm_c_im, ssm_d, ssm_glu_w, ssm_glu_b, sg_ln_w, sg_ln_b, sg_w, sg_b, attn_sinks, w_branch_a, w_branch_b, w_branch_c, w_out, final_norm_w, loss_target, m_norm_w, m_w_in, m_ssm_a_re, m_ssm_a_im, m_ssm_log_dt, m_ssm_b_re, m_ssm_b_im, m_ssm_c_re, m_ssm_c_im, m_ssm_d, m_ssm_glu_w, m_ssm_glu_b, m_sg_ln_w, m_sg_ln_b, m_sg_w, m_sg_b, m_attn_sinks, m_w_branch_a, m_w_branch_b, m_w_branch_c, m_w_out, m_final_norm_w, v_norm_w, v_w_in, v_ssm_a_re, v_ssm_a_im, v_ssm_log_dt, v_ssm_b_re, v_ssm_b_im, v_ssm_c_re, v_ssm_c_im, v_ssm_d, v_ssm_glu_w, v_ssm_glu_b, v_sg_ln_w, v_sg_ln_b, v_sg_w, v_sg_b, v_attn_sinks, v_w_branch_a, v_w_branch_b, v_w_branch_c, v_w_out, v_final_norm_w):
    w = dict(zip(WEIGHTS, (norm_w, w_in, ssm_a_re, ssm_a_im, ssm_log_dt, ssm_b_re, ssm_b_im, ssm_c_re, ssm_c_im, ssm_d, ssm_glu_w, ssm_glu_b, sg_ln_w, sg_ln_b, sg_w, sg_b, attn_sinks, w_branch_a, w_branch_b, w_branch_c, w_out, final_norm_w)))
    m = dict(zip(WEIGHTS, (m_norm_w, m_w_in, m_ssm_a_re, m_ssm_a_im, m_ssm_log_dt, m_ssm_b_re, m_ssm_b_im, m_ssm_c_re, m_ssm_c_im, m_ssm_d, m_ssm_glu_w, m_ssm_glu_b, m_sg_ln_w, m_sg_ln_b, m_sg_w, m_sg_b, m_attn_sinks, m_w_branch_a, m_w_branch_b, m_w_branch_c, m_w_out, m_final_norm_w)))
    v = dict(zip(WEIGHTS, (v_norm_w, v_w_in, v_ssm_a_re, v_ssm_a_im, v_ssm_log_dt, v_ssm_b_re, v_ssm_b_im, v_ssm_c_re, v_ssm_c_im, v_ssm_d, v_ssm_glu_w, v_ssm_glu_b, v_sg_ln_w, v_sg_ln_b, v_sg_w, v_sg_b, v_attn_sinks, v_w_branch_a, v_w_branch_b, v_w_branch_c, v_w_out, v_final_norm_w)))

    keys = [BIG_KEY[n][0] for n in BIG]
    wv, mv, vv = ({n: _view(n, a) for n, a in d.items()} for d in (w, m, v))
    shards = [[(wv[n][l] if n in VIEWS else w[n][l].T if BIG_KEY[n][1] else w[n][l]).astype(BF16) for n in BIG]
              for l in range(DEPTH)]
    small_p = [{n: w[n][l] for n in SMALL} for l in range(DEPTH)]
    xv, tgt = x[0], loss_target[0]
    tabs = _rope_tables(xv.shape[0])

    lands = [[_allgather_place(shards[l][:1]), _allgather_place(shards[l][1:])] for l in range(DEPTH)]
    s5 = [_s5_prep(small_p[l], f"l{l}") for l in range(DEPTH)]
    wmv_packs = {cols: [_pack([[d[n] for n in names] for names in groups], cols, f"pack{cols}_{tag}")
                        for tag, d in (("w", wv), ("m", mv), ("v", vv))] for cols, groups in PACKS}
    near = _split_start([], lands[0][0], 3, _ag_copies_near, "ag_l0_win_near_start")
    got = {}
    x_, y_ = lax.axis_index("x"), lax.axis_index("y")
    n_tiles = D_IN // PROJ_TN
    far_first = (D_IN // 4 // PROJ_TN) * (2 * (1 - x_) + (1 - y_))
    n_far = -(-D_IN // 4 // PROJ_TN)
    tile_ids = jnp.arange(n_tiles, dtype=jnp.int32)
    is_far = (tile_ids >= far_first) & (tile_ids < far_first + n_far)
    near_tiles = jnp.sort(jnp.where(is_far, n_tiles, tile_ids))[:n_tiles - n_far]
    far_tiles = (far_first + jnp.arange(n_far)).astype(jnp.int32)

    def proj_of0(h):
        early = [h, *lands[0][1], *lands[1][0], *lands[1][1], *s5[0][1], *s5[1][1], near_tiles, far_tiles]
        early += [p for ps in wmv_packs.values() for p in ps]
        _, land = _split_wait(near[0], near[1], [], near[3], early, _ag_copies_near, "ag_l0_win_near_wait")
        far = _split_start([], land, 1, _ag_copies_far, "ag_l0_win_far_start")
        land = _ag_forward(far[3], "ag_l0_win_near_forward", which=(0, 1))
        got["ag0b"] = _allgather_start(lands[0][1], "ag_l0_rest", after=land[0])
        got["near1"] = _split_start([], lands[1][0], 3, _ag_copies_near, "ag_l1_win_near_start", after=got["ag0b"][4])
        proj = _in_proj_tiles(h, land[0], near_tiles, None, "in_proj_l0_near", after=got["near1"][4])
        _, land = _split_wait(far[0], far[1], [], land, proj, _ag_copies_far, "ag_l0_win_far_wait")
        got["win0"] = _ag_forward(land, "ag_l0_win_far_forward", which=(2,))[0]
        return _in_proj_tiles(h, got["win0"], far_tiles, proj, "in_proj_l0_far")

    def after_proj0(proj):
        got["w0"] = dict(zip(keys, [got["win0"]] + _allgather_finish(got["ag0b"], proj, "ag_l0_rest")))
        return got["w0"]

    x1, saved0 = _layer_fwd(xv, small_p[0], None, tabs, "l0", s5=s5[0], proj_of=proj_of0, after_proj=after_proj0)
    big_w0 = got["w0"]

    def proj_of1(h):
        near1 = got["near1"]
        _, land = _split_wait(near1[0], near1[1], [], near1[3], h, _ag_copies_near, "ag_l1_win_near_wait")
        far1 = _split_start([], land, 1, _ag_copies_far, "ag_l1_win_far_start")
        land = _ag_forward(far1[3], "ag_l1_win_near_forward", which=(0, 1))
        got["ag1b"] = _allgather_start(lands[1][1], "ag_l1_rest", after=land[0])
        proj = _in_proj_tiles(h, land[0], near_tiles, None, "in_proj_l1_near", after=got["ag1b"][4])
        _, land = _split_wait(far1[0], far1[1], [], land, proj, _ag_copies_far, "ag_l1_win_far_wait")
        got["win1"] = _ag_forward(land, "ag_l1_win_far_forward", which=(2,))[0]
        return _in_proj_tiles(h, got["win1"], far_tiles, proj, "in_proj_l1_far")

    def after_proj1(proj):
        got["w1"] = dict(zip(keys, [got["win1"]] + _allgather_finish(got["ag1b"], proj, "ag_l1_rest")))
        return got["w1"]

    x2, saved1 = _layer_fwd(x1, small_p[1], None, tabs, "l1", s5=s5[1], proj_of=proj_of1, after_proj=after_proj1)
    big_w1 = got["w1"]
    loss_acc, dx2, dfw = _final(x2, w["final_norm_w"][None], tgt, "final_norm_loss")
    loss = lax.psum(loss_acc[0, 0], ("x", "y", "c"))
    dfw = dfw[0]

    dx1, big_g1, small_g1 = _layer_bwd(dx2, small_p[1], big_w1, tabs, saved1, "l1")
    rs1_cores = _reduce_scatter_cores_start([big_g1[k] for k in keys], "l1")

    def after_merge0(x):
        got["rs1"] = _reduce_scatter_cores_finish(rs1_cores, x, "l1")
        return got["rs1"][4]

    def before_win0(big):
        got["rs0b"] = _reduce_scatter_start([big[k] for k in keys[1:]], "l0_rest")
        return got["rs0b"][4]

    def after_win0(big):
        got["rs0a"] = _reduce_scatter_start([big["win_t"]], "l0_win")
        return got["rs0a"][4]

    dx, big_g0, small_g0 = _layer_bwd(dx1, small_p[0], big_w0, tabs, saved0, "l0", first_after=rs1_cores[4],
                                      after_merge=after_merge0, before_win=before_win0, after_win=after_win0)
    rs1 = got["rs1"]
    small_g = [small_g0, small_g1]
    grads, delta, new_m, new_v = {}, {}, {}, {}

    def big_adam(red, layer, carry):
        outs = {}
        for i, n in enumerate(BIG):
            g = red[i].T if BIG_KEY[n][1] and n not in VIEWS else red[i]
            outs[n] = _adamw_layer(wv[n], g, mv[n], vv[n], layer, None if carry is None else carry[n], f"adamw_{n}_l{layer}")
        return outs

    big1 = big_adam(_reduce_scatter_finish(rs1, dx, "l1"), 1, None)

    def small_grad(n):
        if n == "final_norm_w":
            return dfw
        if n in ("ssm_b_re", "ssm_b_im"):
            return jnp.stack([small_g[l][n.replace("ssm_b_", "ssm_bt_")].transpose(1, 0, 2) for l in range(DEPTH)])
        return jnp.stack([small_g[l][n] for l in range(DEPTH)])

    g_groups = [[[small_grad(n) for n in names] for names in groups] for _, groups in PACKS]
    reduced = _allreduce_small([_pack(gg, cols, f"pack{cols}_g") for gg, (cols, _) in zip(g_groups, PACKS)], "allreduce_small",
                               after=[big1[n][1] for n in BIG])
    last = None
    for (cols, groups), gg, red in zip(PACKS, g_groups, reduced):
        names = [n for names in groups for n in names]
        grads.update(zip(names, _unpack(red, gg, f"unpack{cols}_g")))
        wp, mp, vp = wmv_packs[cols]
        outs = _adamw(wp, red, mp, vp, f"adamw_pack{cols}")
        last = outs[0]
        for tag, res, o in zip("dmv", (delta, new_m, new_v), outs):
            res.update(zip(names, _unpack(o, [[wv[n] for n in names] for names in groups], f"unpack{cols}_{tag}")))

    red0 = (_reduce_scatter_finish(got["rs0a"], last, "l0_win")
            + _reduce_scatter_finish(got["rs0b"], last, "l0_rest"))
    for n, outs in big_adam(red0, 0, big1).items():
        grads[n], delta[n], new_m[n], new_v[n] = outs

    return (loss, dx[None], *[_view(n, d[n]) for d in (grads, delta, new_m, new_v) for n in WEIGHTS])
```

```python
import functools
import math

import jax
import jax.numpy as jnp
from jax import lax
from jax.experimental import pallas as pl
from jax.experimental.pallas import tpu as pltpu

F32 = jnp.float32
BF16 = jnp.bfloat16

D_MODEL = 2048
DEPTH = 2
EPS = 1e-6
NEG_INF = -1e30
N_DEV = 8

SSM_WIDTH = 1024
SSM_GROUP = 16
SSM_GROUPS = 64
SSM_STATE = 64
N_SLAB = 8
SLAB_CH = 128
SLAB_ST = 512
SUB = 8
N_GRP = 2
N_SEG = SUB * N_GRP

SG_HEADS = 8
CHUNK = 128
HEAD_DIM = 64
ATT_HEADS = 16
ROT_DIM = 16
ROPE_THETA = 500000.0

D_IN = 13568
OFF_UA, OFF_ZA, OFF_UB, OFF_VB, OFF_ZB, OFF_Q, OFF_KV, OFF_ZC, OFF_G = (
    0, 1024, 2048, 3072, 4096, 5120, 6144, 6400, 7424)

ADAM_LR, ADAM_B1, ADAM_B2, ADAM_EPS, ADAM_WD, ADAM_STEP = 0.001, 0.9, 0.999, 1e-08, 0.01, 10

VMEM_LIMIT = 56 * 1024 * 1024


def _cp(sem=None):
    return pltpu.CompilerParams(dimension_semantics=sem, vmem_limit_bytes=VMEM_LIMIT)


def _dot(a, b):
    return jnp.dot(a, b, preferred_element_type=F32)


def _dot_nt(a, b):
    return lax.dot_general(a, b, (((1,), (1,)), ((), ())), preferred_element_type=F32)


def _dot_tn(a, b):
    return lax.dot_general(a, b, (((0,), (0,)), ((), ())), preferred_element_type=F32)


def _mm(a, b, mode, out_dtype, tm, tn, tk, name, res=None, after=None):
    if mode == "nn":
        (m, k), (_, n) = a.shape, b.shape
    elif mode == "nt":
        (m, k), (n, _) = a.shape, b.shape
    else:
        (k, m), (_, n) = a.shape, b.shape
    tm, tn, tk = min(tm, m), min(tn, n), min(tk, k)
    assert m % tm == 0 and n % tn == 0 and k % tk == 0, (name, m, n, k, tm, tn, tk)
    nk = k // tk
    a_spec = {"nn": pl.BlockSpec((tm, tk), lambda i, j, kk: (i, kk)),
              "nt": pl.BlockSpec((tm, tk), lambda i, j, kk: (i, kk)),
              "tn": pl.BlockSpec((tk, tm), lambda i, j, kk: (kk, i))}[mode]
    b_spec = {"nn": pl.BlockSpec((tk, tn), lambda i, j, kk: (kk, j)),
              "nt": pl.BlockSpec((tn, tk), lambda i, j, kk: (j, kk)),
              "tn": pl.BlockSpec((tk, tn), lambda i, j, kk: (kk, j))}[mode]
    dot = {"nn": _dot, "nt": _dot_nt, "tn": _dot_tn}[mode]
    has_res = res is not None
    direct = out_dtype == F32 and not has_res

    def body(*refs):
        ins, outs = refs[:2 + has_res + (after is not None)], refs[2 + has_res + (after is not None):]
        a_ref, b_ref = ins[:2]
        r_ref = ins[2] if has_res else None
        o_ref = outs[0]
        acc = o_ref if direct else outs[1]
        kk = pl.program_id(2)

        @pl.when(kk == 0)
        def _():
            acc[...] = jnp.zeros_like(acc)

        acc[...] += dot(a_ref[...].astype(BF16), b_ref[...].astype(BF16))

        if not direct:
            @pl.when(kk == nk - 1)
            def _():
                r = acc[...]
                if has_res:
                    r = r + r_ref[...]
                o_ref[...] = r.astype(out_dtype)

    in_specs = [a_spec, b_spec]
    args = [a, b]
    if has_res:
        in_specs.append(pl.BlockSpec((tm, tn), lambda i, j, kk: (i, j)))
        args.append(res)
    if after is not None:
        in_specs.append(pl.BlockSpec(memory_space=pl.ANY))
        args.append(after)
    return pl.pallas_call(
        body, name=name,
        grid=(m // tm, n // tn, nk),
        in_specs=in_specs,
        out_specs=pl.BlockSpec((tm, tn), lambda i, j, kk: (i, j)),
        out_shape=jax.ShapeDtypeStruct((m, n), out_dtype),
        scratch_shapes=[] if direct else [pltpu.VMEM((tm, tn), F32)],
        compiler_params=_cp(("parallel", "parallel", "arbitrary")),
    )(*args)


PROJ_TN = 256


def _in_proj_tiles(h, win_t, tiles, carry, name, after=None):
    L, K = h.shape
    extra = [a for a in (carry, after) if a is not None]

    def body(t_ref, h_ref, w_ref, *rest):
        rest[len(extra)][...] = _dot_nt(h_ref[...], w_ref[...]).astype(BF16)

    return pl.pallas_call(
        body, name=name,
        grid_spec=pltpu.PrefetchScalarGridSpec(
            num_scalar_prefetch=1, grid=(tiles.shape[0],),
            in_specs=[pl.BlockSpec((L, K), lambda j, t: (0, 0)), pl.BlockSpec((PROJ_TN, K), lambda j, t: (t[j], 0))]
            + [pl.BlockSpec(memory_space=pl.ANY)] * len(extra),
            out_specs=pl.BlockSpec((L, PROJ_TN), lambda j, t: (0, t[j]))),
        out_shape=jax.ShapeDtypeStruct((L, win_t.shape[0]), BF16),
        input_output_aliases={} if carry is None else {3: 0},
        compiler_params=_cp(("arbitrary",)),
    )(tiles, h, win_t, *extra)


def _rms(x, w):
    return x * lax.rsqrt(jnp.mean(x * x, axis=-1, keepdims=True) + EPS) * w


def _rms_fwd(x, w, name):
    L, D = x.shape
    tm = min(L, 256)

    def body(x_ref, w_ref, h_ref):
        h_ref[...] = _rms(x_ref[...], w_ref[...]).astype(BF16)

    return pl.pallas_call(
        body, name=name, grid=(L // tm,),
        in_specs=[pl.BlockSpec((tm, D), lambda i: (i, 0)), pl.BlockSpec((1, D), lambda i: (0, 0))],
        out_specs=pl.BlockSpec((tm, D), lambda i: (i, 0)),
        out_shape=jax.ShapeDtypeStruct((L, D), BF16),
        compiler_params=_cp(("parallel",)),
    )(x, w)


def _rms_bwd(x, w, dh, dres, name):
    L, D = x.shape
    tm = min(L, 256)

    def body(x_ref, w_ref, dh_ref, dres_ref, dx_ref, dw_ref):
        _, vjp = jax.vjp(_rms, x_ref[...], w_ref[...])
        dx, dw = vjp(dh_ref[...])
        dx_ref[...] = dx + dres_ref[...]

        @pl.when(pl.program_id(0) == 0)
        def _():
            dw_ref[...] = jnp.zeros_like(dw_ref)

        dw_ref[...] += dw

    row = pl.BlockSpec((tm, D), lambda i: (i, 0))
    vec = pl.BlockSpec((1, D), lambda i: (0, 0))
    return pl.pallas_call(
        body, name=name, grid=(L // tm,),
        in_specs=[row, vec, row, row],
        out_specs=[row, vec],
        out_shape=[jax.ShapeDtypeStruct((L, D), F32), jax.ShapeDtypeStruct((1, D), F32)],
        compiler_params=_cp(("arbitrary",)),
    )(x, w, dh, dres)


def _final(x, fw, tgt, name):
    L, D = x.shape
    tm = min(L, 256)

    def loss_fn(xv, wv, tv):
        err = _rms(xv, wv) - tv
        return jnp.sum(err * err) * (0.5 / D)

    def body(x_ref, w_ref, t_ref, loss_ref, dx_ref, dw_ref):
        tv = t_ref[...]
        val, vjp = jax.vjp(lambda a, b: loss_fn(a, b, tv), x_ref[...], w_ref[...])
        dx, dw = vjp(jnp.ones((), F32))
        dx_ref[...] = dx

        @pl.when(pl.program_id(0) == 0)
        def _():
            dw_ref[...] = jnp.zeros_like(dw_ref)
            loss_ref[...] = jnp.zeros_like(loss_ref)

        dw_ref[...] += dw
        loss_ref[...] += jnp.full(loss_ref.shape, val, F32)

    row = pl.BlockSpec((tm, D), lambda i: (i, 0))
    vec = pl.BlockSpec((1, D), lambda i: (0, 0))
    return pl.pallas_call(
        body, name=name, grid=(L // tm,),
        in_specs=[row, vec, row],
        out_specs=[pl.BlockSpec((8, 128), lambda i: (0, 0)), row, vec],
        out_shape=[jax.ShapeDtypeStruct((8, 128), F32), jax.ShapeDtypeStruct((L, D), F32),
                   jax.ShapeDtypeStruct((1, D), F32)],
        compiler_params=_cp(("arbitrary",)),
    )(x, fw, tgt)


def _s5_param_fn(a_re, a_im, log_dt, bt_re, bt_im):
    dt = jnp.exp(log_dt)
    zr, zi = a_re * dt, a_im * dt
    er = jnp.exp(zr)
    lr, li = er * jnp.cos(zi), er * jnp.sin(zi)
    nr, ni = lr - 1.0, li
    den = a_re * a_re + a_im * a_im
    cr = (nr * a_re + ni * a_im) / den
    ci = (ni * a_re - nr * a_im) / den
    bbr = cr[None] * bt_re - ci[None] * bt_im
    bbi = cr[None] * bt_im + ci[None] * bt_re
    return lr, li, bbr, bbi


def _s5_params_fwd(a_re, a_im, log_dt, bt_re, bt_im, name):
    def body(ar, ai, ld, br, bi, lr, li, bbr, bbi):
        o = _s5_param_fn(ar[...], ai[...], ld[...], br[...], bi[...])
        lr[...], li[...], bbr[...], bbi[...] = o

    gp = jax.ShapeDtypeStruct(a_re.shape, F32)
    cgp = jax.ShapeDtypeStruct(bt_re.shape, F32)
    return pl.pallas_call(body, name=name, out_shape=[gp, gp, cgp, cgp])(a_re, a_im, log_dt, bt_re, bt_im)


def _s5_params_bwd(a_re, a_im, log_dt, bt_re, bt_im, dlr, dli, dbbr, dbbi, name):
    def body(ar, ai, ld, br, bi, g0, g1, g2, g3, o0, o1, o2, o3, o4):
        _, vjp = jax.vjp(_s5_param_fn, ar[...], ai[...], ld[...], br[...], bi[...])
        o0[...], o1[...], o2[...], o3[...], o4[...] = vjp((g0[...], g1[...], g2[...], g3[...]))

    gp = jax.ShapeDtypeStruct(a_re.shape, F32)
    cgp = jax.ShapeDtypeStruct(bt_re.shape, F32)
    return pl.pallas_call(body, name=name,
                          out_shape=[gp, gp, jax.ShapeDtypeStruct(log_dt.shape, F32), cgp, cgp])(
        a_re, a_im, log_dt, bt_re, bt_im, dlr, dli, dbbr, dbbi)


def _cmul(ar, ai, br, bi):
    return ar * br - ai * bi, ar * bi + ai * br


def _cpow(lr, li, n):
    rr, ri = None, None
    br, bi = lr, li
    while n:
        if n & 1:
            rr, ri = (br, bi) if rr is None else _cmul(rr, ri, br, bi)
        n >>= 1
        if n:
            br, bi = _cmul(br, bi, br, bi)
    return rr, ri


def _shift_rows(x, up):
    row = lax.broadcasted_iota(jnp.int32, x.shape, 0)
    if up:
        return jnp.where(row == SUB - 1, 0.0, pltpu.roll(x, SUB - 1, 0))
    return jnp.where(row == 0, 0.0, pltpu.roll(x, 1, 0))


NT = SLAB_ST // 128


def _lam_tiles(lr_ref, li_ref):
    return [(lr_ref[:, j * 128:(j + 1) * 128], li_ref[:, j * 128:(j + 1) * 128]) for j in range(NT)]


def _row_on_sublanes(ref, j, t):
    return ref[j, pl.ds(t, SUB, stride=0), :]


def _pow_table(pw_re, pw_im, lam_t, seg):
    assert seg % 8 == 0 and (seg // 8) & (seg // 8 - 1) == 0
    for j in range(NT):
        lr, li = lam_t[j][0][0:1], lam_t[j][1][0:1]
        r, i_ = lr, li
        for row in range(8):
            pw_re[j, row:row + 1, :] = r
            pw_im[j, row:row + 1, :] = i_
            if row < 7:
                r, i_ = _cmul(r, i_, lr, li)
        n = 8
        while n < seg:
            qr, qi = _cpow(lr, li, n)
            nr, ni = _cmul(pw_re[j, 0:n, :], pw_im[j, 0:n, :], qr, qi)
            pw_re[j, n:2 * n, :] = nr
            pw_im[j, n:2 * n, :] = ni
            n *= 2


def _seg_scan(s_re, s_im, lam_t, pw_re, pw_im, seg, reverse, prev=None):
    sgn = -1.0 if reverse else 1.0
    lt = [(lr, sgn * li) for lr, li in lam_t]
    tiles = [(g, j) for g in range(N_GRP) for j in range(NT)]
    zeros = jnp.zeros((SUB, 128), F32)

    def rows(g, i):
        return pl.ds(pl.multiple_of((g * seg + i) * SUB, SUB), SUB)

    def step1(t, carry):
        i = seg - 1 - t if reverse else t
        out = []
        for n, (g, j) in enumerate(tiles):
            nr, ni = _cmul(lt[j][0], lt[j][1], carry[2 * n], carry[2 * n + 1])
            nr = nr + s_re[j, rows(g, i), :]
            ni = ni + s_im[j, rows(g, i), :]
            s_re[j, rows(g, i), :] = nr
            s_im[j, rows(g, i), :] = ni
            out += [nr, ni]
        return tuple(out)

    zero = tuple(zeros for _ in range(2 * len(tiles)))
    ends = lax.fori_loop(0, seg, step1, zero)

    carries = [None] * (2 * len(tiles))
    row = lax.broadcasted_iota(jnp.int32, (SUB, 128), 0)
    dist = (SUB - 1 - row) if reverse else row
    edge = 0 if reverse else SUB - 1
    for j in range(NT):
        pr, pi = _cpow(lt[j][0], lt[j][1], seg)
        qr, qi = jnp.ones((SUB, 128), F32), zeros
        for s in range(1, SUB):
            tr, ti = _cmul(qr, qi, pr, pi)
            qr, qi = jnp.where(dist >= s, tr, qr), jnp.where(dist >= s, ti, qi)
        boundary = None
        for g in (reversed(range(N_GRP)) if reverse else range(N_GRP)):
            n = g * NT + j
            cr, ci = zeros, zeros
            for _ in range(SUB - 1):
                tr, ti = _cmul(pr, pi, cr, ci)
                cr = _shift_rows(tr + ends[2 * n], reverse)
                ci = _shift_rows(ti + ends[2 * n + 1], reverse)
            if boundary is not None:
                tr, ti = _cmul(qr, qi, boundary[0], boundary[1])
                cr, ci = cr + tr, ci + ti
            carries[2 * n], carries[2 * n + 1] = cr, ci
            fr, fi = _cmul(pr, pi, cr, ci)
            boundary = (jnp.broadcast_to((fr + ends[2 * n])[edge:edge + 1], (SUB, 128)),
                        jnp.broadcast_to((fi + ends[2 * n + 1])[edge:edge + 1], (SUB, 128)))

    def fix(t, i, acc, before):
        out = []
        pws = [(_row_on_sublanes(pw_re, j, t), sgn * _row_on_sublanes(pw_im, j, t)) for j in range(NT)]
        for n, (g, j) in enumerate(tiles):
            ar, ai = _cmul(pws[j][0], pws[j][1], carries[2 * n], carries[2 * n + 1])
            ar = ar + s_re[j, rows(g, i), :]
            ai = ai + s_im[j, rows(g, i), :]
            s_re[j, rows(g, i), :] = ar
            s_im[j, rows(g, i), :] = ai
            if before is not None:
                qr, qi = before(n)
                out += [acc[2 * n] + ar * qr + ai * qi, acc[2 * n + 1] + ai * qr - ar * qi]
        return tuple(out)

    if prev is None:
        lax.fori_loop(0, seg, lambda t, c: fix(t, seg - 1 - t if reverse else t, c, None), ())
        return carries
    assert reverse
    p_re, p_im, p_carries = prev

    def earlier(t):
        return lambda n: (p_re[tiles[n][1], rows(tiles[n][0], seg - 2 - t), :],
                          p_im[tiles[n][1], rows(tiles[n][0], seg - 2 - t), :])

    acc = lax.fori_loop(0, seg - 1, lambda t, c: fix(t, seg - 1 - t, c, earlier(t)), zero)
    acc = fix(seg - 1, 0, acc, lambda n: (p_carries[2 * n], p_carries[2 * n + 1]))
    return carries, [sum(acc[2 * (g * NT + j) + part] for g in range(N_GRP)) for j in range(NT) for part in range(2)]


def _seg_slice(k, seg):
    g, r = divmod(k, SUB)
    return pl.ds(g * seg * SUB + r, seg, stride=SUB)


def _seg_rows(ref, k, seg):
    return jnp.concatenate([ref[j, _seg_slice(k, seg), :] for j in range(NT)], axis=-1)


def _seg_store(ref, k, seg, val):
    for j in range(NT):
        ref[j, _seg_slice(k, seg), :] = val[:, j * 128:(j + 1) * 128]


def _s5_specs(L):
    col = lambda off: pl.BlockSpec((L, SLAB_CH), lambda j: (0, off + j))
    mat_b = pl.BlockSpec((None, SLAB_CH, SLAB_ST), lambda j: (j, 0, 0))
    mat_c = pl.BlockSpec((None, SLAB_ST, SLAB_CH), lambda j: (j, 0, 0))
    vec_s = pl.BlockSpec((None, SUB, SLAB_ST), lambda j: (j, 0, 0))
    vec_c = pl.BlockSpec((None, 1, SLAB_CH), lambda j: (j, 0, 0))
    return col, mat_b, mat_c, vec_s, vec_c


def _s5_states(u_ref, bre_ref, bim_ref, lam_t, pw_re, pw_im, s_re, s_im, seg):
    _pow_table(pw_re, pw_im, lam_t, seg)
    for k in range(N_SEG):
        uk = u_ref[pl.ds(k * seg, seg), :]
        _seg_store(s_re, k, seg, _dot(uk, bre_ref[...]))
        _seg_store(s_im, k, seg, _dot(uk, bim_ref[...]))
    return _seg_scan(s_re, s_im, lam_t, pw_re, pw_im, seg, reverse=False)


def _s5_fwd(proj, bre, bim, cre_t, cim_t, lam_re, lam_im, dvec, name):
    L = proj.shape[0]
    seg = L // N_SEG
    col, mat_b, mat_c, vec_s, vec_c = _s5_specs(L)
    rows = N_SEG * seg

    def body(u_ref, bre_ref, bim_ref, cre_ref, cim_ref, lr_ref, li_ref, d_ref, y_ref, s_re, s_im, pw_re, pw_im):
        _s5_states(u_ref, bre_ref, bim_ref, _lam_tiles(lr_ref, li_ref), pw_re, pw_im, s_re, s_im, seg)
        for k in range(N_SEG):
            y = (_dot(_seg_rows(s_re, k, seg).astype(BF16), cre_ref[...])
                 - _dot(_seg_rows(s_im, k, seg).astype(BF16), cim_ref[...]))
            y = y + d_ref[...] * u_ref[pl.ds(k * seg, seg), :].astype(F32)
            y_ref[pl.ds(k * seg, seg), :] = jax.nn.gelu(y).astype(BF16)

    return pl.pallas_call(
        body, name=name, grid=(N_SLAB,),
        in_specs=[col(OFF_UA // SLAB_CH), mat_b, mat_b, mat_c, mat_c, vec_s, vec_s, vec_c],
        out_specs=pl.BlockSpec((L, SLAB_CH), lambda j: (0, j)),
        out_shape=jax.ShapeDtypeStruct((L, SSM_WIDTH), BF16),
        scratch_shapes=[pltpu.VMEM((NT, rows, 128), F32)] * 2 + [pltpu.VMEM((NT, seg, 128), F32)] * 2,
        compiler_params=_cp(("parallel",)),
    )(proj, bre, bim, cre_t, cim_t, lam_re, lam_im, dvec)


def _s5_bwd(proj, dy, bre, bim, cre_t, cim_t, lam_re, lam_im, dvec, name):
    L = proj.shape[0]
    seg = L // N_SEG
    col, mat_b, mat_c, vec_s, vec_c = _s5_specs(L)
    rows = N_SEG * seg
    dlam_spec = pl.BlockSpec((None, 1, SLAB_ST), lambda j: (j, 0, 0))

    def body(u_ref, dy_ref, bre_ref, bim_ref, cre_ref, cim_ref, lr_ref, li_ref, d_ref,
             du_ref, dbre_ref, dbim_ref, dcre_ref, dcim_ref, dlr_ref, dli_ref, dd_ref,
             s_re, s_im, a_re, a_im, pw_re, pw_im, dyp):
        lam_t = _lam_tiles(lr_ref, li_ref)
        carry_s = _s5_states(u_ref, bre_ref, bim_ref, lam_t, pw_re, pw_im, s_re, s_im, seg)
        dcre = jnp.zeros((SLAB_ST, SLAB_CH), F32)
        dcim = jnp.zeros((SLAB_ST, SLAB_CH), F32)
        dd = jnp.zeros((1, SLAB_CH), F32)
        for k in range(N_SEG):
            sre = _seg_rows(s_re, k, seg).astype(BF16)
            sim = _seg_rows(s_im, k, seg).astype(BF16)
            uk = u_ref[pl.ds(k * seg, seg), :].astype(F32)
            ypre = _dot(sre, cre_ref[...]) - _dot(sim, cim_ref[...]) + d_ref[...] * uk
            _, vjp = jax.vjp(jax.nn.gelu, ypre)
            (dyk,) = vjp(dy_ref[pl.ds(k * seg, seg), :].astype(F32))
            dyp[pl.ds(k * seg, seg), :] = dyk
            dd = dd + jnp.sum(dyk * uk, axis=0, keepdims=True)
            dyb = dyk.astype(BF16)
            dcre = dcre + _dot_tn(sre, dyb)
            dcim = dcim - _dot_tn(sim, dyb)
            _seg_store(a_re, k, seg, _dot_nt(dyb, cre_ref[...]))
            _seg_store(a_im, k, seg, -_dot_nt(dyb, cim_ref[...]))
        dcre_ref[...] = dcre
        dcim_ref[...] = dcim
        dd_ref[...] = dd

        _, acc = _seg_scan(a_re, a_im, lam_t, pw_re, pw_im, seg, reverse=True, prev=(s_re, s_im, carry_s))
        dlr_ref[...] = jnp.concatenate([jnp.sum(acc[2 * j], axis=0, keepdims=True) for j in range(NT)], axis=-1)
        dli_ref[...] = jnp.concatenate([jnp.sum(acc[2 * j + 1], axis=0, keepdims=True) for j in range(NT)], axis=-1)

        dbre = jnp.zeros((SLAB_CH, SLAB_ST), F32)
        dbim = jnp.zeros((SLAB_CH, SLAB_ST), F32)
        for k in range(N_SEG):
            are = _seg_rows(a_re, k, seg).astype(BF16)
            aim = _seg_rows(a_im, k, seg).astype(BF16)
            uk = u_ref[pl.ds(k * seg, seg), :]
            du = _dot_nt(are, bre_ref[...]) + _dot_nt(aim, bim_ref[...]) + dyp[pl.ds(k * seg, seg), :] * d_ref[...]
            du_ref[pl.ds(k * seg, seg), :] = du.astype(BF16)
            dbre = dbre + _dot_tn(uk, are)
            dbim = dbim + _dot_tn(uk, aim)
        dbre_ref[...] = dbre
        dbim_ref[...] = dbim

    scan_buf = pltpu.VMEM((NT, rows, 128), F32)
    pow_buf = pltpu.VMEM((NT, seg, 128), F32)
    return pl.pallas_call(
        body, name=name, grid=(N_SLAB,),
        in_specs=[col(OFF_UA // SLAB_CH), pl.BlockSpec((L, SLAB_CH), lambda j: (0, j)),
                  mat_b, mat_b, mat_c, mat_c, vec_s, vec_s, vec_c],
        out_specs=[pl.BlockSpec((L, SLAB_CH), lambda j: (0, j)), mat_b, mat_b, mat_c, mat_c, dlam_spec, dlam_spec, vec_c],
        out_shape=[jax.ShapeDtypeStruct((L, SSM_WIDTH), BF16),
                   jax.ShapeDtypeStruct((N_SLAB, SLAB_CH, SLAB_ST), F32),
                   jax.ShapeDtypeStruct((N_SLAB, SLAB_CH, SLAB_ST), F32),
                   jax.ShapeDtypeStruct((N_SLAB, SLAB_ST, SLAB_CH), F32),
                   jax.ShapeDtypeStruct((N_SLAB, SLAB_ST, SLAB_CH), F32),
                   jax.ShapeDtypeStruct((N_SLAB, 1, SLAB_ST), F32),
                   jax.ShapeDtypeStruct((N_SLAB, 1, SLAB_ST), F32),
                   jax.ShapeDtypeStruct((N_SLAB, 1, SLAB_CH), F32)],
        scratch_shapes=[scan_buf, scan_buf, scan_buf, scan_buf, pow_buf, pow_buf, pltpu.VMEM((L, SLAB_CH), F32)],
        compiler_params=_cp(("parallel",)),
    )(proj, dy, bre, bim, cre_t, cim_t, lam_re, lam_im, dvec)


def _glu_point(y0, pre, za, b):
    return y0 * jax.nn.sigmoid(pre + b) * jax.nn.silu(za)


def _glu_specs(L, tm):
    row = pl.BlockSpec((tm, SSM_WIDTH), lambda i: (i, 0))
    za = pl.BlockSpec((tm, SSM_WIDTH), lambda i: (i, OFF_ZA // SSM_WIDTH))
    wmat = pl.BlockSpec((SSM_WIDTH, SSM_WIDTH), lambda i: (0, 0))
    vec = pl.BlockSpec((1, SSM_WIDTH), lambda i: (0, 0))
    return row, za, wmat, vec


def _glu_fwd(ya0, proj, w, b, name):
    L = ya0.shape[0]
    tm = min(L, 512)
    row, za, wmat, vec = _glu_specs(L, tm)

    def body(y_ref, z_ref, w_ref, b_ref, o_ref):
        y0 = y_ref[...]
        pre = _dot(y0, w_ref[...])
        o_ref[...] = _glu_point(y0.astype(F32), pre, z_ref[...].astype(F32), b_ref[...]).astype(BF16)

    return pl.pallas_call(
        body, name=name, grid=(L // tm,), in_specs=[row, za, wmat, vec], out_specs=row,
        out_shape=jax.ShapeDtypeStruct((L, SSM_WIDTH), BF16), compiler_params=_cp(("parallel",)),
    )(ya0, proj, w, b)


def _glu_bwd(ya0, proj, w, b, dya, name):
    L = ya0.shape[0]
    tm = min(L, 512)
    row, za, wmat, vec = _glu_specs(L, tm)

    def body(y_ref, z_ref, w_ref, b_ref, g_ref, dy0_ref, dza_ref, dw_ref, db_ref):
        y0 = y_ref[...]
        pre = _dot(y0, w_ref[...])
        _, vjp = jax.vjp(_glu_point, y0.astype(F32), pre, z_ref[...].astype(F32), b_ref[...])
        dy0, dpre, dza, db = vjp(g_ref[...].astype(F32))
        dpb = dpre.astype(BF16)
        dy0_ref[...] = (dy0 + _dot_nt(dpb, w_ref[...])).astype(BF16)
        dza_ref[...] = dza.astype(BF16)

        @pl.when(pl.program_id(0) == 0)
        def _():
            dw_ref[...] = jnp.zeros_like(dw_ref)
            db_ref[...] = jnp.zeros_like(db_ref)

        dw_ref[...] += _dot_tn(y0, dpb)
        db_ref[...] += db

    return pl.pallas_call(
        body, name=name, grid=(L // tm,), in_specs=[row, za, wmat, vec, row],
        out_specs=[row, row, wmat, vec],
        out_shape=[jax.ShapeDtypeStruct((L, SSM_WIDTH), BF16), jax.ShapeDtypeStruct((L, SSM_WIDTH), BF16),
                   jax.ShapeDtypeStruct((SSM_WIDTH, SSM_WIDTH), F32), jax.ShapeDtypeStruct((1, SSM_WIDTH), F32)],
        compiler_params=_cp(("arbitrary",)),
    )(ya0, proj, w, b, dya)


def _sg_norm(vb, ln_w, ln_b):
    v0 = jax.nn.gelu(vb)
    mu = jnp.mean(v0, axis=-1, keepdims=True)
    var = jnp.mean(jnp.square(v0 - mu), axis=-1, keepdims=True)
    return (v0 - mu) * lax.rsqrt(var + EPS) * ln_w + ln_b


def _sg_gate(ub, mixed, zb):
    return jax.nn.gelu(ub) * mixed * jax.nn.silu(zb)


def _sg_specs():
    W = SSM_WIDTH
    blk = lambda off: pl.BlockSpec((CHUNK, W), lambda n: (n, off // W))
    out = pl.BlockSpec((CHUNK, W), lambda n: (n, 0))
    vec = pl.BlockSpec((1, W), lambda n: (0, 0))
    wsp = pl.BlockSpec((SG_HEADS, CHUNK, CHUNK), lambda n: (0, 0, 0))
    bsp = pl.BlockSpec((SG_HEADS, CHUNK, 1), lambda n: (0, 0, 0))
    return blk, out, vec, wsp, bsp


def _sg_masked(w_ref):
    t = lax.broadcasted_iota(jnp.int32, (CHUNK, CHUNK), 0)
    s = lax.broadcasted_iota(jnp.int32, (CHUNK, CHUNK), 1)
    causal = s <= t
    return causal, [jnp.where(causal, w_ref[h], 0.0).astype(BF16) for h in range(SG_HEADS)]


def _sg_mix(wm, vnb, bias_ref):
    return jnp.concatenate(
        [_dot(wm[h], vnb[:, h * CHUNK:(h + 1) * CHUNK]) + bias_ref[h] for h in range(SG_HEADS)], axis=-1)


def _sg_fwd(proj, ln_w, ln_b, w, bias, name):
    L = proj.shape[0]
    blk, out, vec, wsp, bsp = _sg_specs()

    def body(ub_ref, vb_ref, zb_ref, lw_ref, lb_ref, w_ref, bias_ref, o_ref):
        _, wm = _sg_masked(w_ref)
        vnb = _sg_norm(vb_ref[...].astype(F32), lw_ref[...], lb_ref[...]).astype(BF16)
        mixed = _sg_mix(wm, vnb, bias_ref)
        o_ref[...] = _sg_gate(ub_ref[...].astype(F32), mixed, zb_ref[...].astype(F32)).astype(BF16)

    return pl.pallas_call(
        body, name=name, grid=(L // CHUNK,),
        in_specs=[blk(OFF_UB), blk(OFF_VB), blk(OFF_ZB), vec, vec, wsp, bsp], out_specs=out,
        out_shape=jax.ShapeDtypeStruct((L, SSM_WIDTH), BF16), compiler_params=_cp(("parallel",)),
    )(proj, proj, proj, ln_w, ln_b, w, bias)


def _sg_bwd(proj, ln_w, ln_b, w, bias, dyb, name):
    L = proj.shape[0]
    blk, out, vec, wsp, bsp = _sg_specs()

    def body(ub_ref, vb_ref, zb_ref, lw_ref, lb_ref, w_ref, bias_ref, g_ref,
             dub_ref, dvb_ref, dzb_ref, dlw_ref, dlb_ref, dw_ref, dbias_ref):
        causal, wm = _sg_masked(w_ref)
        vb = vb_ref[...].astype(F32)
        vn, vjp_norm = jax.vjp(_sg_norm, vb, lw_ref[...], lb_ref[...])
        vnb = vn.astype(BF16)
        mixed = _sg_mix(wm, vnb, bias_ref)
        _, vjp_gate = jax.vjp(_sg_gate, ub_ref[...].astype(F32), mixed, zb_ref[...].astype(F32))
        dub, dmixed, dzb = vjp_gate(g_ref[...].astype(F32))
        dub_ref[...] = dub.astype(BF16)
        dzb_ref[...] = dzb.astype(BF16)

        @pl.when(pl.program_id(0) == 0)
        def _():
            dlw_ref[...] = jnp.zeros_like(dlw_ref)
            dlb_ref[...] = jnp.zeros_like(dlb_ref)
            dw_ref[...] = jnp.zeros_like(dw_ref)
            dbias_ref[...] = jnp.zeros_like(dbias_ref)

        dvn = []
        for h in range(SG_HEADS):
            dm = dmixed[:, h * CHUNK:(h + 1) * CHUNK]
            dmb = dm.astype(BF16)
            dbias_ref[h] += jnp.sum(dm, axis=-1, keepdims=True)
            dw_ref[h] += jnp.where(causal, _dot_nt(dmb, vnb[:, h * CHUNK:(h + 1) * CHUNK]), 0.0)
            dvn.append(_dot_tn(wm[h], dmb))
        dvb, dlw, dlb = vjp_norm(jnp.concatenate(dvn, axis=-1))
        dvb_ref[...] = dvb.astype(BF16)
        dlw_ref[...] += dlw
        dlb_ref[...] += dlb

    act = jax.ShapeDtypeStruct((L, SSM_WIDTH), BF16)
    return pl.pallas_call(
        body, name=name, grid=(L // CHUNK,),
        in_specs=[blk(OFF_UB), blk(OFF_VB), blk(OFF_ZB), vec, vec, wsp, bsp, out],
        out_specs=[out, out, out, vec, vec, wsp, bsp],
        out_shape=[act, act, act, jax.ShapeDtypeStruct((1, SSM_WIDTH), F32), jax.ShapeDtypeStruct((1, SSM_WIDTH), F32),
                   jax.ShapeDtypeStruct((SG_HEADS, CHUNK, CHUNK), F32), jax.ShapeDtypeStruct((SG_HEADS, CHUNK, 1), F32)],
        compiler_params=_cp(("arbitrary",)),
    )(proj, proj, proj, ln_w, ln_b, w, bias, dyb)


def _rope_tables(L):
    half = ROT_DIM // 2
    inv_freq = ROPE_THETA ** (-jnp.arange(0, ROT_DIM, 2, dtype=F32) / ROT_DIM)
    ang = jnp.arange(L, dtype=F32)[:, None] * inv_freq[None, :]
    cos, sin = jnp.cos(ang), jnp.sin(ang)
    ones = jnp.ones((L, HEAD_DIM - ROT_DIM), F32)
    cos_h = jnp.concatenate([cos, cos, ones], axis=-1)
    sin_h = jnp.concatenate([-sin, sin, 0.0 * ones], axis=-1)
    src = jnp.arange(HEAD_DIM)[:, None]
    dst = jnp.arange(HEAD_DIM)[None, :]
    p_h = (((dst < half) & (src == dst + half)) | ((dst >= half) & (dst < ROT_DIM) & (src == dst - half))).astype(F32)
    p2 = jnp.kron(jnp.eye(2, dtype=F32), p_h).astype(BF16)
    return jnp.tile(cos_h, (1, 2)), jnp.tile(sin_h, (1, 2)), p2


def _rope(t, cos, sin, p2):
    n = t.shape[1] // 128
    tb = t.astype(BF16)
    sw = jnp.concatenate([_dot(tb[:, i * 128:(i + 1) * 128], p2) for i in range(n)], axis=-1) if n > 1 else _dot(tb, p2)
    return t * jnp.tile(cos, (1, n)) + sw * jnp.tile(sin, (1, n))


def _rope_t(g, cos, sin, p2):
    n = g.shape[1] // 128
    gs = (g * jnp.tile(sin, (1, n))).astype(BF16)
    sw = jnp.concatenate([_dot_nt(gs[:, i * 128:(i + 1) * 128], p2) for i in range(n)], axis=-1) if n > 1 else _dot_nt(gs, p2)
    return g * jnp.tile(cos, (1, n)) + sw


def _lane_lo(shape):
    return (lax.broadcasted_iota(jnp.int32, shape, len(shape) - 1) % 128) < HEAD_DIM


def _dup_halves(x):
    xr = pltpu.roll(x, HEAD_DIM, 1)
    lo = _lane_lo(x.shape)
    return jnp.where(lo, x, xr), jnp.where(lo, xr, x)


def _fold_halves(d0, d1):
    f0 = d0 + pltpu.roll(d0, HEAD_DIM, 1)
    f1 = d1 + pltpu.roll(d1, HEAD_DIM, 1)
    return jnp.where(_lane_lo(d0.shape), f0, f1)


def _attn_mask():
    qi = lax.broadcasted_iota(jnp.int32, (CHUNK, 2 * CHUNK), 0)
    kj = lax.broadcasted_iota(jnp.int32, (CHUNK, 2 * CHUNK), 1)
    return qi, kj


def _attn_specs():
    qsp = pl.BlockSpec((CHUNK, 1024), lambda n: (n, OFF_Q // 1024))
    kv_cur = pl.BlockSpec((CHUNK, 256), lambda n: (n, OFF_KV // 256))
    kv_prev = pl.BlockSpec((CHUNK, 256), lambda n: (jnp.maximum(n - 1, 0), OFF_KV // 256))
    zsp = [pl.BlockSpec((CHUNK, 256), functools.partial(lambda n, q: (n, OFF_ZC // 256 + q), q=q)) for q in range(4)]
    tab_cur = pl.BlockSpec((CHUNK, 128), lambda n: (n, 0))
    tab_prev = pl.BlockSpec((CHUNK, 128), lambda n: (jnp.maximum(n - 1, 0), 0))
    p2sp = pl.BlockSpec((128, 128), lambda n: (0, 0))
    sink = pl.BlockSpec(memory_space=pltpu.SMEM)
    wide = pl.BlockSpec((CHUNK, 1024), lambda n: (n, 0))
    return qsp, kv_cur, kv_prev, zsp, tab_cur, tab_prev, p2sp, sink, wide


def _attn_prep(n, q_ref, kvc_ref, kvp_ref, cosc_ref, sinc_ref, cosp_ref, sinp_ref, p2_ref):
    p2 = p2_ref[...]
    qr = _rope(q_ref[...].astype(F32), cosc_ref[...], sinc_ref[...], p2).astype(BF16)
    kc = _rope(kvc_ref[:, 0:128].astype(F32), cosc_ref[...], sinc_ref[...], p2)
    kp = _rope(kvp_ref[:, 0:128].astype(F32), cosp_ref[...], sinp_ref[...], p2)
    k_all = jnp.concatenate([kp, kc], axis=0).astype(BF16)
    v_all = jnp.concatenate([kvp_ref[:, 128:256], kvc_ref[:, 128:256]], axis=0)
    qi, kj = _attn_mask()
    allowed = ((kj < CHUNK) & (kj > qi) & (n > 0)) | ((kj >= CHUNK) & (kj - CHUNK <= qi))
    return qr, _dup_halves(k_all), _dup_halves(v_all), allowed, _lane_lo((CHUNK, 128))


def _attn_head(qr, kd, sink_ref, h, allowed, lo):
    m, half, g = h // 2, h % 2, h // 8
    qp = qr[:, m * 128:(m + 1) * 128]
    qm = jnp.where(lo if half == 0 else ~lo, qp, jnp.zeros_like(qp))
    s = jnp.where(allowed, _dot_nt(qm, kd[g]) * (HEAD_DIM ** -0.5), NEG_INF)
    snk = sink_ref[h]
    mx = jnp.maximum(jnp.max(s, axis=-1, keepdims=True), snk)
    e = jnp.exp(s - mx)
    es = jnp.exp(snk - mx)
    inv = 1.0 / (jnp.sum(e, axis=-1, keepdims=True) + es)
    return qm, e * inv, es * inv


def _silu_gate(o, z):
    return o * jax.nn.silu(z)


def _pair_lanes(refs, m):
    return refs[m // 2][:, (m % 2) * 128:(m % 2 + 1) * 128]


def _attn_fwd(proj, sinks, tabs, name):
    L = proj.shape[0]
    cos2, sin2, p2 = tabs
    qsp, kv_cur, kv_prev, zsp, tab_cur, tab_prev, p2sp, sink, wide = _attn_specs()

    def body(q_ref, kvc_ref, kvp_ref, z0, z1, z2, z3, cosc, sinc, cosp, sinp, p2_ref, sink_ref, y_ref, o_ref):
        n = pl.program_id(0)
        qr, kd, vd, allowed, lo = _attn_prep(n, q_ref, kvc_ref, kvp_ref, cosc, sinc, cosp, sinp, p2_ref)
        probs = [_attn_head(qr, kd, sink_ref, h, allowed, lo)[1].astype(BF16) for h in range(ATT_HEADS)]
        for m in range(ATT_HEADS // 2):
            g = m // 4
            o0 = _dot(probs[2 * m], vd[g])
            o1 = _dot(probs[2 * m + 1], vd[g])
            o = jnp.where(lo, o0, o1).astype(BF16)
            o_ref[:, m * 128:(m + 1) * 128] = o
            z = _pair_lanes((z0, z1, z2, z3), m).astype(F32)
            y_ref[:, m * 128:(m + 1) * 128] = _silu_gate(o.astype(F32), z).astype(BF16)

    act = jax.ShapeDtypeStruct((L, 1024), BF16)
    return pl.pallas_call(
        body, name=name, grid=(L // CHUNK,),
        in_specs=[qsp, kv_cur, kv_prev, *zsp, tab_cur, tab_cur, tab_prev, tab_prev, p2sp, sink],
        out_specs=[wide, wide], out_shape=[act, act], compiler_params=_cp(("parallel",)),
    )(proj, proj, proj, proj, proj, proj, proj, cos2, sin2, cos2, sin2, p2, sinks)


def _attn_bwd(proj, sinks, tabs, o_att, dyc, name):
    L = proj.shape[0]
    cos2, sin2, p2 = tabs
    qsp, kv_cur, kv_prev, zsp, tab_cur, tab_prev, p2sp, sink, wide = _attn_specs()
    kvo = pl.BlockSpec((CHUNK, 256), lambda n: (n, 0))

    def body(q_ref, kvc_ref, kvp_ref, z0, z1, z2, z3, cosc, sinc, cosp, sinp, p2_ref, sink_ref, o_ref, g_ref,
             dq_ref, dz_ref, dkvc_ref, dkvp_ref, dsink_ref):
        n = pl.program_id(0)
        qr, kd, vd, allowed, lo = _attn_prep(n, q_ref, kvc_ref, kvp_ref, cosc, sinc, cosp, sinp, p2_ref)
        p2 = p2_ref[...]

        @pl.when(n == 0)
        def _():
            dsink_ref[...] = jnp.zeros_like(dsink_ref)

        dkd = [jnp.zeros((2 * CHUNK, 128), F32), jnp.zeros((2 * CHUNK, 128), F32)]
        dvd = [jnp.zeros((2 * CHUNK, 128), F32), jnp.zeros((2 * CHUNK, 128), F32)]
        probs = [_attn_head(qr, kd, sink_ref, h, allowed, lo) for h in range(ATT_HEADS)]
        for m in range(ATT_HEADS // 2):
            g = m // 4
            lanes = slice(m * 128, (m + 1) * 128)
            z = _pair_lanes((z0, z1, z2, z3), m).astype(F32)
            _, vjp = jax.vjp(_silu_gate, o_ref[:, lanes].astype(F32), z)
            do, dz = vjp(g_ref[:, lanes].astype(F32))
            dz_ref[:, lanes] = dz.astype(BF16)
            dop = do.astype(BF16)
            dq_h = []
            for half in range(2):
                h = 2 * m + half
                qm, p, ps = probs[h]
                dom = jnp.where(lo if half == 0 else ~lo, dop, jnp.zeros_like(dop))
                dp = _dot_nt(dom, vd[g])
                rs = jnp.sum(p * dp, axis=-1, keepdims=True)
                ds = (p * (dp - rs) * (HEAD_DIM ** -0.5)).astype(BF16)
                dsink_ref[h:h + 1, :] += jnp.broadcast_to(jnp.sum(-ps * rs, axis=0, keepdims=True), (1, 128))
                dq_h.append(_dot(ds, kd[g]))
                dkd[g] = dkd[g] + _dot_tn(ds, qm)
                dvd[g] = dvd[g] + _dot_tn(p.astype(BF16), dom)
            dq_ref[:, lanes] = _rope_t(jnp.where(lo, dq_h[0], dq_h[1]), cosc[...], sinc[...], p2).astype(BF16)
        dk_rot = _fold_halves(dkd[0], dkd[1])
        dv = _fold_halves(dvd[0], dvd[1])
        dkp = _rope_t(dk_rot[0:CHUNK], cosp[...], sinp[...], p2)
        dkc = _rope_t(dk_rot[CHUNK:2 * CHUNK], cosc[...], sinc[...], p2)
        dkvp_ref[...] = jnp.concatenate([dkp, dv[0:CHUNK]], axis=-1)
        dkvc_ref[...] = jnp.concatenate([dkc, dv[CHUNK:2 * CHUNK]], axis=-1)

    act = jax.ShapeDtypeStruct((L, 1024), BF16)
    kvs = jax.ShapeDtypeStruct((L, 256), F32)
    return pl.pallas_call(
        body, name=name, grid=(L // CHUNK,),
        in_specs=[qsp, kv_cur, kv_prev, *zsp, tab_cur, tab_cur, tab_prev, tab_prev, p2sp, sink, wide, wide],
        out_specs=[wide, wide, kvo, kvo, pl.BlockSpec((ATT_HEADS, 128), lambda n: (0, 0))],
        out_shape=[act, act, kvs, kvs, jax.ShapeDtypeStruct((ATT_HEADS, 128), F32)],
        compiler_params=_cp(("arbitrary",)),
    )(proj, proj, proj, proj, proj, proj, proj, cos2, sin2, cos2, sin2, p2, sinks, o_att, dyc)


MERGE_TN = 256


def _merge_point(ta, tb, tc, ga, gb, gc):
    return jax.nn.sigmoid(ga) * ta + jax.nn.sigmoid(gb) * tb + jax.nn.sigmoid(gc) * tc


def _merge_specs(tm):
    nj = D_MODEL // MERGE_TN
    t = pl.BlockSpec((tm, MERGE_TN), lambda i, j: (i, j))
    gates = [pl.BlockSpec((tm, MERGE_TN), functools.partial(lambda i, j, b: (i, OFF_G // MERGE_TN + b * nj + j), b=b))
             for b in range(3)]
    return t, gates, nj


def _merge_fwd(ta, tb, tc, proj, name):
    L = ta.shape[0]
    tm = min(L, 1024)
    t, gates, nj = _merge_specs(tm)

    def body(ta_ref, tb_ref, tc_ref, ga_ref, gb_ref, gc_ref, o_ref):
        f = lambda r: r[...].astype(F32)
        o_ref[...] = _merge_point(f(ta_ref), f(tb_ref), f(tc_ref), f(ga_ref), f(gb_ref), f(gc_ref)).astype(BF16)

    return pl.pallas_call(
        body, name=name, grid=(L // tm, nj), in_specs=[t, t, t, *gates], out_specs=t,
        out_shape=jax.ShapeDtypeStruct((L, D_MODEL), BF16), compiler_params=_cp(("parallel", "parallel")),
    )(ta, tb, tc, proj, proj, proj)


def _merge_bwd(ta, tb, tc, proj, dm, name):
    L = ta.shape[0]
    tm = min(L, 1024)
    t, gates, nj = _merge_specs(tm)

    def body(ta_ref, tb_ref, tc_ref, ga_ref, gb_ref, gc_ref, dm_ref, dta_ref, dtb_ref, dtc_ref, dga_ref, dgb_ref, dgc_ref):
        f = lambda r: r[...].astype(F32)
        _, vjp = jax.vjp(_merge_point, f(ta_ref), f(tb_ref), f(tc_ref), f(ga_ref), f(gb_ref), f(gc_ref))
        outs = vjp(f(dm_ref))
        for r, v in zip((dta_ref, dtb_ref, dtc_ref, dga_ref, dgb_ref, dgc_ref), outs):
            r[...] = v.astype(BF16)

    act = jax.ShapeDtypeStruct((L, D_MODEL), BF16)
    return pl.pallas_call(
        body, name=name, grid=(L // tm, nj), in_specs=[t, t, t, *gates, t],
        out_specs=[t] * 6, out_shape=[act] * 6,
        compiler_params=_cp(("parallel", "parallel")),
    )(ta, tb, tc, proj, proj, proj, dm)


GRAD_DT = BF16
SMALL = ("norm_w", "ssm_a_re", "ssm_a_im", "ssm_log_dt", "ssm_b_re", "ssm_b_im", "ssm_c_re", "ssm_c_im", "ssm_d",
         "ssm_glu_b", "sg_ln_w", "sg_ln_b", "sg_w", "sg_b", "attn_sinks")
G8 = SSM_GROUPS // N_SLAB


def _diag_mask(rows_per_group, cols_per_group):
    r = jnp.arange(G8 * rows_per_group)[:, None] // rows_per_group
    c = jnp.arange(G8 * cols_per_group)[None, :] // cols_per_group
    return r == c


def _slab_b(bb_t):
    x = bb_t.transpose(1, 0, 2).reshape(N_SLAB, SLAB_CH, SSM_STATE)
    return jnp.where(_diag_mask(SSM_GROUP, SSM_STATE), jnp.tile(x, (1, 1, G8)), 0)


def _unslab_b(d):
    x = jnp.where(_diag_mask(SSM_GROUP, SSM_STATE), d, 0).reshape(N_SLAB, SLAB_CH, G8, SSM_STATE).sum(axis=2)
    return x.reshape(SSM_GROUPS, SSM_GROUP, SSM_STATE).transpose(1, 0, 2)


def _slab_c(c):
    x = c.transpose(0, 2, 1).reshape(N_SLAB, SLAB_ST, SSM_GROUP)
    return jnp.where(_diag_mask(SSM_STATE, SSM_GROUP), jnp.tile(x, (1, 1, G8)), 0)


def _unslab_c(d):
    x = jnp.where(_diag_mask(SSM_STATE, SSM_GROUP), d, 0).reshape(N_SLAB, SLAB_ST, G8, SSM_GROUP).sum(axis=2)
    return x.reshape(SSM_GROUPS, SSM_STATE, SSM_GROUP).transpose(0, 2, 1)


def _s5_prep(p, tag):
    bt_re = p["ssm_b_re"].transpose(2, 0, 1)
    bt_im = p["ssm_b_im"].transpose(2, 0, 1)
    raw = (p["ssm_a_re"], p["ssm_a_im"], p["ssm_log_dt"][:, None], bt_re, bt_im)
    lr, li, bbr, bbi = _s5_params_fwd(*raw, name=f"s5_params_{tag}")
    ops = (_slab_b(bbr).astype(BF16), _slab_b(bbi).astype(BF16),
           _slab_c(p["ssm_c_re"]).astype(BF16), _slab_c(p["ssm_c_im"]).astype(BF16),
           jnp.broadcast_to(lr.reshape(N_SLAB, 1, SLAB_ST), (N_SLAB, SUB, SLAB_ST)),
           jnp.broadcast_to(li.reshape(N_SLAB, 1, SLAB_ST), (N_SLAB, SUB, SLAB_ST)),
           p["ssm_d"].reshape(N_SLAB, 1, SLAB_CH))
    return raw, ops


def _layer_fwd(x, p, w, tabs, tag, s5=None, proj_of=None, after_proj=None):
    L = x.shape[0]
    h = _rms_fwd(x, p["norm_w"][None], f"rms_fwd_{tag}")
    if proj_of is not None:
        proj = proj_of(h)
    else:
        proj = _mm(h, w["win_t"], "nt", BF16, L, PROJ_TN, D_MODEL, f"in_proj_{tag}")
    if after_proj is not None:
        w = after_proj(proj)
    s5_raw, s5_ops = s5 if s5 is not None else _s5_prep(p, tag)
    ya0 = _s5_fwd(proj, *s5_ops, name=f"s5_fwd_{tag}")
    ya = _glu_fwd(ya0, proj, w["glu"], p["ssm_glu_b"][None], f"glu_fwd_{tag}")
    yb = _sg_fwd(proj, p["sg_ln_w"][None], p["sg_ln_b"][None], p["sg_w"], p["sg_b"][:, :, None], f"sg_fwd_{tag}")
    yc, o_att = _attn_fwd(proj, p["attn_sinks"], tabs, f"attn_fwd_{tag}")
    ta = _mm(ya, w["wba_t"], "nt", BF16, 1024, 1024, 1024, f"branch_a_{tag}")
    tb = _mm(yb, w["wbb_t"], "nt", BF16, 1024, 1024, 1024, f"branch_b_{tag}")
    tc = _mm(yc, w["wbc_t"], "nt", BF16, 1024, 1024, 1024, f"branch_c_{tag}")
    merged = _merge_fwd(ta, tb, tc, proj, f"merge_fwd_{tag}")
    x_new = _mm(merged, w["wout"], "nn", F32, 1024, 512, D_MODEL, f"out_proj_{tag}", res=x)
    saved = dict(x=x, h=h, proj=proj, s5_raw=s5_raw, s5_ops=s5_ops, ya0=ya0, ya=ya, yb=yb, yc=yc, o_att=o_att,
                 ta=ta, tb=tb, tc=tc, merged=merged)
    return x_new, saved


def _layer_bwd(dx_out, p, w, tabs, s, tag, first_after=None, after_merge=None, before_win=None, after_win=None):
    L = dx_out.shape[0]
    proj = s["proj"]
    big, small = {}, {}
    dmerged = _mm(dx_out, w["wout"], "nt", BF16, 1024, 512, D_MODEL, f"d_merged_{tag}", after=first_after)
    big["wout"] = _mm(s["merged"], dx_out, "tn", GRAD_DT, 512, 1024, L, f"d_wout_{tag}")
    dta, dtb, dtc, dga, dgb, dgc = _merge_bwd(s["ta"], s["tb"], s["tc"], proj, dmerged, f"merge_bwd_{tag}")
    tok = after_merge(dga) if after_merge is not None else None
    dy = {}
    for br, dt in (("a", dta), ("b", dtb), ("c", dtc)):
        dy[br] = _mm(dt, w[f"wb{br}_t"], "nn", BF16, 1024, 1024, D_MODEL, f"d_y{br}_{tag}", after=tok)
        big[f"wb{br}_t"] = _mm(dt, s[f"y{br}"], "tn", GRAD_DT, 512, 1024, L, f"d_wb{br}_{tag}")

    dq, dzc, dkvc, dkvp, dsink = _attn_bwd(proj, p["attn_sinks"], tabs, s["o_att"], dy["c"], f"attn_bwd_{tag}")
    dkv = dkvc + jnp.concatenate([dkvp[CHUNK:], jnp.zeros((CHUNK, 256), F32)], axis=0)
    small["attn_sinks"] = dsink[:, 0]

    dub, dvb, dzb, dlw, dlb, dsgw, dsgb = _sg_bwd(
        proj, p["sg_ln_w"][None], p["sg_ln_b"][None], p["sg_w"], p["sg_b"][:, :, None], dy["b"], f"sg_bwd_{tag}")
    small.update(sg_ln_w=dlw[0], sg_ln_b=dlb[0], sg_w=dsgw, sg_b=dsgb[:, :, 0])

    dya0, dza, dglu, dglub = _glu_bwd(s["ya0"], proj, w["glu"], p["ssm_glu_b"][None], dy["a"], f"glu_bwd_{tag}")
    big["glu"] = dglu.astype(GRAD_DT)
    small["ssm_glu_b"] = dglub[0]

    dua, dbre, dbim, dcre, dcim, dlr, dli, dd = _s5_bwd(proj, dya0, *s["s5_ops"], name=f"s5_bwd_{tag}")
    da_re, da_im, dlog_dt, dbt_re, dbt_im = _s5_params_bwd(
        *s["s5_raw"], dlr.reshape(SSM_GROUPS, SSM_STATE), dli.reshape(SSM_GROUPS, SSM_STATE),
        _unslab_b(dbre), _unslab_b(dbim), name=f"s5_params_bwd_{tag}")
    small.update(ssm_a_re=da_re, ssm_a_im=da_im, ssm_log_dt=dlog_dt[:, 0],
                 ssm_bt_re=dbt_re, ssm_bt_im=dbt_im,
                 ssm_c_re=_unslab_c(dcre), ssm_c_im=_unslab_c(dcim), ssm_d=dd.reshape(SSM_WIDTH))

    dproj = jnp.concatenate([dua, dza, dub, dvb, dzb, dq, dkv.astype(BF16), dzc, dga, dgb, dgc], axis=-1)
    tok = before_win(big) if before_win is not None else None
    big["win_t"] = _mm(dproj, s["h"], "tn", GRAD_DT, 256, D_MODEL, L, f"d_win_{tag}", after=tok)
    tok = after_win(big) if after_win is not None else None
    dh = _mm(dproj, w["win_t"], "nn", F32, L, D_MODEL, 256, f"d_h_{tag}", after=tok)
    dx_in, dnw = _rms_bwd(s["x"], p["norm_w"][None], dh, dx_out, f"rms_bwd_{tag}")
    small["norm_w"] = dnw[0]
    return dx_in, big, small


def _local_step(x, tgt, small_p, final_w, big_w):
    L = x.shape[0]
    tabs = _rope_tables(L)
    saved = []
    for l in range(DEPTH):
        x, s = _layer_fwd(x, small_p[l], big_w[l], tabs, f"l{l}")
        saved.append(s)
    loss_acc, dx, dfw = _final(x, final_w[None], tgt, "final_norm_loss")
    big_g, small_g = [None] * DEPTH, [None] * DEPTH
    for l in reversed(range(DEPTH)):
        dx, big_g[l], small_g[l] = _layer_bwd(dx, small_p[l], big_w[l], tabs, saved[l], f"l{l}")
    return loss_acc[0, 0], dx, dfw[0], big_g, small_g


MESH = pl.DeviceIdType.MESH
ANY = pl.BlockSpec(memory_space=pl.ANY)
ROW_ALIGN = 16


def _place():
    return lax.axis_index("x"), lax.axis_index("y"), lax.axis_index("c")


HBM = pl.BlockSpec(memory_space=pltpu.HBM)
SEM = pl.BlockSpec(memory_space=pltpu.SEMAPHORE)
EFFECT = pltpu.SideEffectType.DATAFLOW_SIDE_EFFECTING


def _split_start(srcs, lands, n_copies, copies, name, after=None):
    n, m, k = len(srcs), len(lands), n_copies
    extra = [] if after is None else [after]

    def body(*refs):
        src_refs, land_refs = refs[:n], refs[n:n + m]
        sems = refs[n + m + len(extra):]
        send_sems, recv_sems, token = sems[:k], sems[k:2 * k], refs[-1]
        for cp in copies(src_refs, land_refs, send_sems, recv_sems):
            cp.start()
        token[...] = jnp.zeros_like(token)

    ops = list(srcs) + list(lands)
    outs = pl.pallas_call(
        body, name=name,
        out_shape=(*[pltpu.SemaphoreType.DMA(())] * (2 * k),
                   *[pltpu.HBM(a.shape, a.dtype) for a in ops], jax.ShapeDtypeStruct((8, 128), F32)),
        in_specs=[HBM] * (n + m) + [ANY] * len(extra),
        out_specs=(*[SEM] * (2 * k), *[HBM] * (n + m), pl.BlockSpec(memory_space=pltpu.VMEM)),
        input_output_aliases={i: 2 * k + i for i in range(n + m)},
        compiler_params=pltpu.CompilerParams(has_side_effects=EFFECT),
    )(*[pltpu.with_memory_space_constraint(a, pltpu.HBM) for a in ops], *extra)
    return (list(outs[:k]), list(outs[k:2 * k]), list(outs[2 * k:2 * k + n]), list(outs[2 * k + n:2 * k + n + m]),
            outs[-1])


def _split_wait(send_sems, recv_sems, srcs, lands, after, copies, name):
    n, m, k = len(srcs), len(lands), len(send_sems)
    after = list(after) if isinstance(after, (list, tuple)) else [after]

    def body(*refs):
        src_refs, land_refs = refs[:n], refs[n:n + m]
        for cp in copies(src_refs, land_refs, refs[n + m:n + m + k], refs[n + m + k:n + m + 2 * k]):
            cp.wait_send()
            cp.wait_recv()

    ops = list(srcs) + list(lands)
    outs = pl.pallas_call(
        body, name=name,
        out_shape=tuple(pltpu.HBM(a.shape, a.dtype) for a in ops),
        in_specs=[HBM] * (n + m) + [SEM] * (2 * k) + [ANY] * len(after),
        out_specs=tuple([HBM] * (n + m)),
        input_output_aliases={i: i for i in range(n + m)},
        compiler_params=pltpu.CompilerParams(has_side_effects=EFFECT),
    )(*ops, *send_sems, *recv_sems, *after)
    return list(outs[:n]), list(outs[n:])


def _ag_rows(land_ref, px, py, pc):
    r = land_ref.shape[0] // N_DEV
    start = pl.multiple_of((4 * px + 2 * py + pc) * r, ROW_ALIGN)
    return land_ref.at[pl.ds(start, r), :]


def _ag_copies_to(which):
    def copies(src_refs, land_refs, send_sems, recv_sems):
        x, y, c = _place()
        peers = [(x, y, 1 - c), (1 - x, y, c), (x, 1 - y, c), (1 - x, 1 - y, c)]
        return [pltpu.make_async_remote_copy(
            src_ref=_ag_rows(land_refs[a], x, y, c), dst_ref=_ag_rows(land_refs[a], x, y, c),
            send_sem=send_sems[len(which) * a + k], recv_sem=recv_sems[len(which) * a + k],
            device_id=peers[p], device_id_type=MESH)
            for a in range(len(land_refs)) for k, p in enumerate(which)]
    return copies


_ag_copies = _ag_copies_to((0, 1, 2, 3))
_ag_copies_near = _ag_copies_to((0, 1, 2))
_ag_copies_far = _ag_copies_to((3,))


def _ag_forward(lands, name, which=(0, 1, 2)):
    n = len(lands)

    def body(*refs):
        land_refs = refs[n:2 * n]
        send_sems, recv_sems = refs[2 * n:]
        x, y, c = _place()
        chips = [(1 - x, y), (x, 1 - y), (1 - x, 1 - y)]

        def copy(a, k, pc):
            px, py = chips[which[k]]
            return pltpu.make_async_remote_copy(
                src_ref=_ag_rows(land_refs[a], px, py, pc), dst_ref=_ag_rows(land_refs[a], px, py, pc),
                send_sem=send_sems.at[a, k], recv_sem=recv_sems.at[a, k], device_id=(x, y, 1 - c), device_id_type=MESH)

        passed = [copy(a, k, c) for a in range(n) for k in range(len(which))]
        for cp in passed:
            cp.start()
        for a in range(n):
            for k in range(len(which)):
                copy(a, k, 1 - c).wait_recv()
        for cp in passed:
            cp.wait_send()

    sems = pltpu.SemaphoreType.DMA((n, len(which)))
    return pl.pallas_call(
        body, name=name,
        in_specs=[ANY] * n, out_specs=[ANY] * n,
        out_shape=[jax.ShapeDtypeStruct(l.shape, l.dtype) for l in lands],
        input_output_aliases={i: i for i in range(n)},
        scratch_shapes=[sems, sems],
    )(*lands)


def _allgather_place(shards):
    x, y, c = _place()
    return [lax.dynamic_update_slice(lax.empty((N_DEV * s.shape[0], s.shape[1]), s.dtype), s,
                                     ((4 * x + 2 * y + c) * s.shape[0], 0)) for s in shards]


def _allgather_start(lands, name, after=None):
    return _split_start([], lands, 4 * len(lands), _ag_copies, name + "_start", after=after)


def _allgather_finish(started, after, name):
    send_sems, recv_sems, _, lands, _ = started
    _, lands = _split_wait(send_sems, recv_sems, [], lands, after, _ag_copies, name + "_wait")
    return list(_ag_forward(lands, name + "_forward"))


def _rs_swap_cores(grads, name):
    n = len(grads)

    def body(*refs):
        ins, outs = refs[:n], refs[n:2 * n]
        send_sems, recv_sems = refs[2 * n:]
        x, y, c = _place()
        cps = []
        for a in range(n):
            r = ins[a].shape[0] // N_DEV
            for q in range(4):
                start = pl.multiple_of((2 * q + 1 - c) * r, ROW_ALIGN)
                cps.append(pltpu.make_async_remote_copy(
                    src_ref=ins[a].at[pl.ds(start, r), :], dst_ref=outs[a].at[q],
                    send_sem=send_sems.at[a, q], recv_sem=recv_sems.at[a, q],
                    device_id=(x, y, 1 - c), device_id_type=MESH))
        for cp in cps:
            cp.start()
        for cp in cps:
            cp.wait()

    return pl.pallas_call(
        body, name=name, in_specs=[ANY] * n, out_specs=[ANY] * n,
        out_shape=[jax.ShapeDtypeStruct((4, g.shape[0] // N_DEV, g.shape[1]), g.dtype) for g in grads],
        scratch_shapes=[pltpu.SemaphoreType.DMA((n, 4)), pltpu.SemaphoreType.DMA((n, 4))],
    )(*grads)


def _rs_chip_copies(sum_refs, land_refs, send_sems, recv_sems):
    x, y, c = _place()
    chips = [(1 - x, y), (x, 1 - y), (1 - x, 1 - y)]
    return [pltpu.make_async_remote_copy(
        src_ref=sum_refs[a].at[2 * px + py], dst_ref=land_refs[a].at[2 * x + y],
        send_sem=send_sems[3 * a + j], recv_sem=recv_sems[3 * a + j], device_id=(px, py, c), device_id_type=MESH)
        for a in range(len(sum_refs)) for j, (px, py) in enumerate(chips)]


def _row_tile(r):
    return max(t for t in range(ROW_ALIGN, min(r, 1024) + 1, ROW_ALIGN) if r % t == 0)


def _rs_add_cores(grad, recv, cidx, name):
    r, cols = recv.shape[1], recv.shape[2]
    tr = _row_tile(r)
    nb = r // tr

    def body(c_ref, g_ref, r_ref, o_ref):
        o_ref[...] = (g_ref[...].astype(F32) + r_ref[...].astype(F32)).astype(o_ref.dtype)

    return pl.pallas_call(
        body, name=name,
        grid_spec=pltpu.PrefetchScalarGridSpec(
            num_scalar_prefetch=1, grid=(4, nb),
            in_specs=[pl.BlockSpec((tr, cols), lambda q, i, c_ref: ((2 * q + c_ref[0]) * nb + i, 0)),
                      pl.BlockSpec((None, tr, cols), lambda q, i, c_ref: (q, i, 0))],
            out_specs=pl.BlockSpec((None, tr, cols), lambda q, i, c_ref: (q, i, 0))),
        out_shape=jax.ShapeDtypeStruct(recv.shape, recv.dtype),
        compiler_params=_cp(("parallel", "parallel")),
    )(cidx, grad, recv)


def _rs_add_chips(own, recv, slots, name):
    r, cols = recv.shape[1], recv.shape[2]
    tr = _row_tile(r)

    def body(s_ref, o_ref, r0_ref, r1_ref, r2_ref, out_ref):
        acc = o_ref[...].astype(F32)
        for ref in (r0_ref, r1_ref, r2_ref):
            acc = acc + ref[...].astype(F32)
        out_ref[...] = acc

    pick = lambda k: pl.BlockSpec((None, tr, cols), functools.partial(lambda i, s_ref, k: (s_ref[k], i, 0), k=k))
    return pl.pallas_call(
        body, name=name,
        grid_spec=pltpu.PrefetchScalarGridSpec(
            num_scalar_prefetch=1, grid=(r // tr,),
            in_specs=[pick(0), pick(1), pick(2), pick(3)],
            out_specs=pl.BlockSpec((tr, cols), lambda i, s_ref: (i, 0))),
        out_shape=jax.ShapeDtypeStruct((r, cols), F32),
        compiler_params=_cp(("parallel",)),
    )(slots, own, recv, recv, recv)


def _rs_core_copies(grad_refs, land_refs, send_sems, recv_sems):
    x, y, c = _place()
    cps = []
    for a in range(len(grad_refs)):
        r = grad_refs[a].shape[0] // N_DEV
        for q in range(4):
            start = pl.multiple_of((2 * q + 1 - c) * r, ROW_ALIGN)
            cps.append(pltpu.make_async_remote_copy(
                src_ref=grad_refs[a].at[pl.ds(start, r), :], dst_ref=land_refs[a].at[q],
                send_sem=send_sems[4 * a + q], recv_sem=recv_sems[4 * a + q],
                device_id=(x, y, 1 - c), device_id_type=MESH))
    return cps


def _reduce_scatter_chips_start(grads, recv, tag):
    cidx = lax.axis_index("c").astype(jnp.int32)[None]
    sums = [_rs_add_cores(g, rv, cidx, f"rs_add_cores_{tag}_{i}") for i, (g, rv) in enumerate(zip(grads, recv))]
    lands = [lax.empty(s.shape, s.dtype) for s in sums]
    return _split_start(sums, lands, 3 * len(sums), _rs_chip_copies, f"rs_chips_{tag}_start")


def _reduce_scatter_start(grads, tag):
    return _reduce_scatter_chips_start(grads, _rs_swap_cores(grads, f"rs_swap_cores_{tag}"), tag)


def _reduce_scatter_cores_start(grads, tag):
    lands = [lax.empty((4, g.shape[0] // N_DEV, g.shape[1]), g.dtype) for g in grads]
    return _split_start(grads, lands, 4 * len(grads), _rs_core_copies, f"rs_cores_{tag}_start")


def _reduce_scatter_cores_finish(started, after, tag):
    send_sems, recv_sems, grads, lands, _ = started
    grads, recv = _split_wait(send_sems, recv_sems, grads, lands, after, _rs_core_copies, f"rs_cores_{tag}_wait")
    return _reduce_scatter_chips_start(grads, recv, tag)


def _reduce_scatter_finish(started, after, tag):
    send_sems, recv_sems, sums, lands, _ = started
    sums, lands = _split_wait(send_sems, recv_sems, sums, lands, after, _rs_chip_copies, f"rs_chips_{tag}_wait")
    x, y = lax.axis_index("x"), lax.axis_index("y")
    slots = jnp.stack([2 * x + y, 2 * (1 - x) + y, 2 * x + 1 - y, 2 * (1 - x) + 1 - y]).astype(jnp.int32)
    return [_rs_add_chips(s, l, slots, f"rs_add_chips_{tag}_{i}") for i, (s, l) in enumerate(zip(sums, lands))]


def _allreduce_small(packs, name, after=()):
    n = len(packs)
    after = list(after)
    assert all(p.shape[0] % (8 * N_DEV) == 0 for p in packs)

    def body(*refs):
        p_refs = refs[:n]
        refs = refs[n + len(after):]
        o_refs, part_refs = refs[:n], refs[n:2 * n]
        send1, recv1, send2, recv2 = refs[2 * n:]
        x, y, c = _place()
        me = 4 * x + 2 * y + c

        def block(ref, d):
            rs = ref.shape[0] // N_DEV
            return ref.at[pl.ds(pl.multiple_of(d * rs, 8), rs), :]

        peers = [(1 - x if k & 4 else x, 1 - y if k & 2 else y, 1 - c if k & 1 else c) for k in range(1, N_DEV)]
        scatter = [pltpu.make_async_remote_copy(
            src_ref=block(p_refs[a], 4 * px + 2 * py + pc), dst_ref=part_refs[a].at[me],
            send_sem=send1.at[a, k], recv_sem=recv1.at[a, k], device_id=(px, py, pc), device_id_type=MESH)
            for a in range(n) for k, (px, py, pc) in enumerate(peers)]
        for cp in scatter:
            cp.start()
        for a in range(n):
            part_refs[a][me] = block(p_refs[a], me)[...]
        for cp in scatter:
            cp.wait()
        for a in range(n):
            acc = part_refs[a][0]
            for d in range(1, N_DEV):
                acc = acc + part_refs[a][d]
            block(o_refs[a], me)[...] = acc
        gather = [pltpu.make_async_remote_copy(
            src_ref=block(o_refs[a], me), dst_ref=block(o_refs[a], me), send_sem=send2.at[a, k], recv_sem=recv2.at[a, k],
            device_id=peer, device_id_type=MESH) for a in range(n) for k, peer in enumerate(peers)]
        for cp in gather:
            cp.start()
        for a in range(n):
            for k, (px, py, pc) in enumerate(peers):
                theirs = block(o_refs[a], 4 * px + 2 * py + pc)
                pltpu.make_async_remote_copy(
                    src_ref=theirs, dst_ref=theirs, send_sem=send2.at[a, k], recv_sem=recv2.at[a, k],
                    device_id=(px, py, pc), device_id_type=MESH).wait_recv()
        for cp in gather:
            cp.wait_send()

    sems = pltpu.SemaphoreType.DMA((n, N_DEV - 1))
    vmem = pl.BlockSpec(memory_space=pltpu.VMEM)
    return pl.pallas_call(
        body, name=name,
        in_specs=[vmem] * n + [ANY] * len(after), out_specs=[vmem] * n,
        out_shape=[jax.ShapeDtypeStruct(p.shape, F32) for p in packs],
        scratch_shapes=[pltpu.VMEM((N_DEV, p.shape[0] // N_DEV, p.shape[1]), F32) for p in packs] + [sems] * 4,
        compiler_params=pltpu.CompilerParams(vmem_limit_bytes=VMEM_LIMIT),
    )(*packs, *after)


ADAM_TILE_BYTES = 2 * 1024 * 1024


def _adam_tiles(rows, cols):
    tc = cols // 2 if cols % 256 == 0 and cols >= 2048 else cols
    tr = max(t for t in range(8, rows + 1, 8) if rows % t == 0 and t * max(tc, 128) * 4 <= ADAM_TILE_BYTES) \
        if rows % 8 == 0 else rows
    return tr, tc


def _adam_math(w, g, m, v):
    nm = ADAM_B1 * m + (1.0 - ADAM_B1) * g
    nv = ADAM_B2 * v + (1.0 - ADAM_B2) * jnp.square(g)
    c1 = 1.0 - ADAM_B1 ** ADAM_STEP
    c2 = 1.0 - ADAM_B2 ** ADAM_STEP
    return -ADAM_LR * ((nm / c1) / (jnp.sqrt(nv / c2) + ADAM_EPS) + ADAM_WD * w), nm, nv


def _adamw_layer(w, g, m, v, layer, carry, name):
    _, rows, cols = w.shape
    tr, tc = _adam_tiles(rows, cols)

    def body(w_ref, g_ref, m_ref, v_ref, *rest):
        go_ref, d_ref, nm_ref, nv_ref = rest[-4:]
        gv = g_ref[...]
        go_ref[...] = gv
        d_ref[...], nm_ref[...], nv_ref[...] = _adam_math(w_ref[...], gv, m_ref[...], v_ref[...])

    blk = pl.BlockSpec((None, tr, tc), lambda i, j: (layer, i, j))
    flat = pl.BlockSpec((tr, tc), lambda i, j: (i, j))
    sh = jax.ShapeDtypeStruct(w.shape, F32)
    carry = [] if carry is None else list(carry)
    return pl.pallas_call(
        body, name=name, grid=(rows // tr, cols // tc),
        in_specs=[blk, flat, blk, blk] + [ANY] * len(carry), out_specs=[blk] * 4, out_shape=[sh] * 4,
        input_output_aliases={4 + k: k for k in range(len(carry))},
        compiler_params=_cp(("parallel", "parallel")),
    )(w, g, m, v, *carry)


def _adamw(w, g, m, v, name):
    shape = w.shape
    rows, cols = shape[-2:]
    lead = shape[:-2]
    nl = math.prod(lead)
    tr, tc = _adam_tiles(rows, cols)

    def body(w_ref, g_ref, m_ref, v_ref, d_ref, nm_ref, nv_ref):
        d_ref[...], nm_ref[...], nv_ref[...] = _adam_math(w_ref[...], g_ref[...], m_ref[...], v_ref[...])

    def index(b, i, j):
        return (*jnp.unravel_index(b, lead), i, j) if lead else (i, j)

    blk = pl.BlockSpec((*[None] * len(lead), tr, tc), index)
    sh = jax.ShapeDtypeStruct(shape, F32)
    return pl.pallas_call(
        body, name=name, grid=(nl, rows // tr, cols // tc), in_specs=[blk] * 4, out_specs=[blk] * 3,
        out_shape=[sh] * 3, compiler_params=_cp(("parallel", "parallel", "parallel")),
    )(w, g, m, v)


WEIGHTS = ("norm_w", "w_in", "ssm_a_re", "ssm_a_im", "ssm_log_dt", "ssm_b_re", "ssm_b_im", "ssm_c_re", "ssm_c_im",
           "ssm_d", "ssm_glu_w", "ssm_glu_b", "sg_ln_w", "sg_ln_b", "sg_w", "sg_b", "attn_sinks",
           "w_branch_a", "w_branch_b", "w_branch_c", "w_out", "final_norm_w")
BIG = ("w_in", "ssm_glu_w", "w_branch_a", "w_branch_b", "w_branch_c", "w_out")
BIG_KEY = {"w_in": ("win_t", True), "ssm_glu_w": ("glu", False), "w_branch_a": ("wba_t", True),
           "w_branch_b": ("wbb_t", True), "w_branch_c": ("wbc_t", True), "w_out": ("wout", False)}
VIEWS = {"w_in": (1, 2), "ssm_b_re": (2, 3), "ssm_b_im": (2, 3)}
MATS = ("ssm_a_re", "ssm_a_im", "ssm_c_re", "ssm_c_im", "ssm_b_re", "ssm_b_im", "sg_w")
VEC_GROUPS = (("ssm_d", "ssm_glu_b", "sg_ln_w", "sg_ln_b"), ("norm_w", "final_norm_w", "sg_b"), ("ssm_log_dt", "attn_sinks"))
PACK_ROWS = 8 * N_DEV


def _view(n, a):
    return jnp.swapaxes(a, *VIEWS[n]) if n in VIEWS else a


def _vec_moves(pack_ref, refs, to_pack):
    d, gb, lw, lb, nw, fw, sb, ld, sk = refs
    full = (slice(None), slice(None))
    moves = [((slice(2 * i, 2 * i + 2), slice(None)), r, full) for i, r in enumerate((d, gb, lw, lb))]
    moves += [((slice(8, 10), slice(None)), nw, (slice(None), slice(0, 1024))),
              ((slice(10, 12), slice(None)), nw, (slice(None), slice(1024, 2048))),
              ((slice(12, 13), slice(None)), fw, (slice(None), slice(0, 1024))),
              ((slice(13, 14), slice(None)), fw, (slice(None), slice(1024, 2048))),
              ((slice(16, 32), slice(0, 128)), sb, full),
              ((slice(32, 34), slice(0, 64)), ld, full),
              ((slice(34, 36), slice(0, 16)), sk, full)]
    for where, ref, part in moves:
        if to_pack:
            pack_ref[where] = ref[part]
        else:
            ref[part] = pack_ref[where]


def _vec_shapes(arrs):
    d, gb, lw, lb, nw, fw, sb, ld, sk = arrs
    return [d, gb, lw, lb, nw, fw.reshape(1, -1), sb.reshape(-1, sb.shape[-1]), ld, sk]


def _vec_pack(arrs, name):
    def body(*refs):
        refs[-1][...] = jnp.zeros_like(refs[-1])
        _vec_moves(refs[-1], refs[:-1], True)

    return pl.pallas_call(body, name=name, out_shape=jax.ShapeDtypeStruct((PACK_ROWS, 1024), F32))(*_vec_shapes(arrs))


def _vec_unpack(pack, like, name):
    shaped = _vec_shapes(like)

    def body(pack_ref, *refs):
        _vec_moves(pack_ref, refs, False)

    outs = pl.pallas_call(body, name=name, out_shape=[jax.ShapeDtypeStruct(a.shape, F32) for a in shaped])(pack)
    return [o.reshape(a.shape) for o, a in zip(outs, like)]


def _pack(groups, cols, name):
    assert cols == 1024
    return _vec_pack([a for arrs in groups for a in arrs], name)


def _unpack(pack, groups, name):
    return _vec_unpack(pack, [a for arrs in groups for a in arrs], name)


def kernel(x, norm_w, w_in, ssm_a_re, ssm_a_im, ssm_log_dt, ssm_b_re, ssm_b_im, ssm_c_re, ssm_c_im, ssm_d, ssm_glu_w, ssm_glu_b, sg_ln_w, sg_ln_b, sg_w, sg_b, attn_sinks, w_branch_a, w_branch_b, w_branch_c, w_out, final_norm_w, loss_target, m_norm_w, m_w_in, m_ssm_a_re, m_ssm_a_im, m_ssm_log_dt, m_ssm_b_re, m_ssm_b_im, m_ssm_c_re, m_ssm_c_im, m_ssm_d, m_ssm_glu_w, m_ssm_glu_b, m_sg_ln_w, m_sg_ln_b, m_sg_w, m_sg_b, m_attn_sinks, m_w_branch_a, m_w_branch_b, m_w_branch_c, m_w_out, m_final_norm_w, v_norm_w, v_w_in, v_ssm_a_re, v_ssm_a_im, v_ssm_log_dt, v_ssm_b_re, v_ssm_b_im, v_ssm_c_re, v_ssm_c_im, v_ssm_d, v_ssm_glu_w, v_ssm_glu_b, v_sg_ln_w, v_sg_ln_b, v_sg_w, v_sg_b, v_attn_sinks, v_w_branch_a, v_w_branch_b, v_w_branch_c, v_w_out, v_final_norm_w):
    w = dict(zip(WEIGHTS, (norm_w, w_in, ssm_a_re, ssm_a_im, ssm_log_dt, ssm_b_re, ssm_b_im, ssm_c_re, ssm_c_im, ssm_d, ssm_glu_w, ssm_glu_b, sg_ln_w, sg_ln_b, sg_w, sg_b, attn_sinks, w_branch_a, w_branch_b, w_branch_c, w_out, final_norm_w)))
    m = dict(zip(WEIGHTS, (m_norm_w, m_w_in, m_ssm_a_re, m_ssm_a_im, m_ssm_log_dt, m_ssm_b_re, m_ssm_b_im, m_ssm_c_re, m_ssm_c_im, m_ssm_d, m_ssm_glu_w, m_ssm_glu_b, m_sg_ln_w, m_sg_ln_b, m_sg_w, m_sg_b, m_attn_sinks, m_w_branch_a, m_w_branch_b, m_w_branch_c, m_w_out, m_final_norm_w)))
    v = dict(zip(WEIGHTS, (v_norm_w, v_w_in, v_ssm_a_re, v_ssm_a_im, v_ssm_log_dt, v_ssm_b_re, v_ssm_b_im, v_ssm_c_re, v_ssm_c_im, v_ssm_d, v_ssm_glu_w, v_ssm_glu_b, v_sg_ln_w, v_sg_ln_b, v_sg_w, v_sg_b, v_attn_sinks, v_w_branch_a, v_w_branch_b, v_w_branch_c, v_w_out, v_final_norm_w)))

    keys = [BIG_KEY[n][0] for n in BIG]
    wv, mv, vv = ({n: _view(n, a) for n, a in d.items()} for d in (w, m, v))
    shards = [[(wv[n][l] if n in VIEWS else w[n][l].T if BIG_KEY[n][1] else w[n][l]).astype(BF16) for n in BIG]
              for l in range(DEPTH)]
    small_p = [{n: w[n][l] for n in SMALL} for l in range(DEPTH)]
    xv, tgt = x[0], loss_target[0]
    tabs = _rope_tables(xv.shape[0])

    lands = [[_allgather_place(shards[l][:1]), _allgather_place(shards[l][1:])] for l in range(DEPTH)]
    s5 = [_s5_prep(small_p[l], f"l{l}") for l in range(DEPTH)]
    vec_packs = [_pack([[d[n] for n in names] for names in VEC_GROUPS], 1024, f"pack_vec_{tag}")
                 for tag, d in (("w", wv), ("m", mv), ("v", vv))]
    near = _split_start([], lands[0][0], 3, _ag_copies_near, "ag_l0_win_near_start")
    got = {}
    x_, y_ = lax.axis_index("x"), lax.axis_index("y")
    n_tiles = D_IN // PROJ_TN
    far_first = (D_IN // 4 // PROJ_TN) * (2 * (1 - x_) + (1 - y_))
    n_far = -(-D_IN // 4 // PROJ_TN)
    tile_ids = jnp.arange(n_tiles, dtype=jnp.int32)
    is_far = (tile_ids >= far_first) & (tile_ids < far_first + n_far)
    near_tiles = jnp.sort(jnp.where(is_far, n_tiles, tile_ids))[:n_tiles - n_far]
    far_tiles = (far_first + jnp.arange(n_far)).astype(jnp.int32)

    def proj_of0(h):
        early = [h, *lands[0][1], *lands[1][0], *lands[1][1], *s5[0][1], *s5[1][1], near_tiles, far_tiles]
        early += vec_packs
        _, land = _split_wait(near[0], near[1], [], near[3], early, _ag_copies_near, "ag_l0_win_near_wait")
        far = _split_start([], land, 1, _ag_copies_far, "ag_l0_win_far_start")
        land = _ag_forward(far[3], "ag_l0_win_near_forward", which=(0, 1))
        got["ag0b"] = _allgather_start(lands[0][1], "ag_l0_rest", after=land[0])
        got["near1"] = _split_start([], lands[1][0], 3, _ag_copies_near, "ag_l1_win_near_start", after=got["ag0b"][4])
        proj = _in_proj_tiles(h, land[0], near_tiles, None, "in_proj_l0_near", after=got["near1"][4])
        _, land = _split_wait(far[0], far[1], [], land, proj, _ag_copies_far, "ag_l0_win_far_wait")
        got["win0"] = _ag_forward(land, "ag_l0_win_far_forward", which=(2,))[0]
        return _in_proj_tiles(h, got["win0"], far_tiles, proj, "in_proj_l0_far")

    def after_proj0(proj):
        got["w0"] = dict(zip(keys, [got["win0"]] + _allgather_finish(got["ag0b"], proj, "ag_l0_rest")))
        return got["w0"]

    x1, saved0 = _layer_fwd(xv, small_p[0], None, tabs, "l0", s5=s5[0], proj_of=proj_of0, after_proj=after_proj0)
    big_w0 = got["w0"]

    def proj_of1(h):
        near1 = got["near1"]
        _, land = _split_wait(near1[0], near1[1], [], near1[3], h, _ag_copies_near, "ag_l1_win_near_wait")
        far1 = _split_start([], land, 1, _ag_copies_far, "ag_l1_win_far_start")
        land = _ag_forward(far1[3], "ag_l1_win_near_forward", which=(0, 1))
        got["ag1b"] = _allgather_start(lands[1][1], "ag_l1_rest", after=land[0])
        proj = _in_proj_tiles(h, land[0], near_tiles, None, "in_proj_l1_near", after=got["ag1b"][4])
        _, land = _split_wait(far1[0], far1[1], [], land, proj, _ag_copies_far, "ag_l1_win_far_wait")
        got["win1"] = _ag_forward(land, "ag_l1_win_far_forward", which=(2,))[0]
        return _in_proj_tiles(h, got["win1"], far_tiles, proj, "in_proj_l1_far")

    def after_proj1(proj):
        got["w1"] = dict(zip(keys, [got["win1"]] + _allgather_finish(got["ag1b"], proj, "ag_l1_rest")))
        return got["w1"]

    x2, saved1 = _layer_fwd(x1, small_p[1], None, tabs, "l1", s5=s5[1], proj_of=proj_of1, after_proj=after_proj1)
    big_w1 = got["w1"]
    loss_acc, dx2, dfw = _final(x2, w["final_norm_w"][None], tgt, "final_norm_loss")
    loss = lax.psum(loss_acc[0, 0], ("x", "y", "c"))
    dfw = dfw[0]

    dx1, big_g1, small_g1 = _layer_bwd(dx2, small_p[1], big_w1, tabs, saved1, "l1")
    rs1_cores = _reduce_scatter_cores_start([big_g1[k] for k in keys], "l1")

    def after_merge0(x):
        got["rs1"] = _reduce_scatter_cores_finish(rs1_cores, x, "l1")
        return got["rs1"][4]

    def before_win0(big):
        got["rs0b"] = _reduce_scatter_start([big[k] for k in keys[1:]], "l0_rest")
        return got["rs0b"][4]

    def after_win0(big):
        got["rs0a"] = _reduce_scatter_start([big["win_t"]], "l0_win")
        return got["rs0a"][4]

    dx, big_g0, small_g0 = _layer_bwd(dx1, small_p[0], big_w0, tabs, saved0, "l0", first_after=rs1_cores[4],
                                      after_merge=after_merge0, before_win=before_win0, after_win=after_win0)
    rs1 = got["rs1"]
    small_g = [small_g0, small_g1]
    grads, delta, new_m, new_v = {}, {}, {}, {}

    def big_adam(red, layer, carry):
        outs = {}
        for i, n in enumerate(BIG):
            g = red[i].T if BIG_KEY[n][1] and n not in VIEWS else red[i]
            outs[n] = _adamw_layer(wv[n], g, mv[n], vv[n], layer, None if carry is None else carry[n], f"adamw_{n}_l{layer}")
        return outs

    big1 = big_adam(_reduce_scatter_finish(rs1, dx, "l1"), 1, None)

    def small_grad(n):
        if n == "final_norm_w":
            return dfw
        if n in ("ssm_b_re", "ssm_b_im"):
            return jnp.stack([small_g[l][n.replace("ssm_b_", "ssm_bt_")].transpose(1, 0, 2) for l in range(DEPTH)])
        return jnp.stack([small_g[l][n] for l in range(DEPTH)])

    rows_of = lambda a: a.reshape(-1, a.shape[-1])
    g_mats = [rows_of(small_grad(n)) for n in MATS]
    g_vecs = [[small_grad(n) for n in names] for names in VEC_GROUPS]
    reduced = _allreduce_small(g_mats + [_pack(g_vecs, 1024, "pack_vec_g")], "allreduce_small",
                               after=[big1[n][1] for n in BIG])
    for n, red in zip(MATS, reduced):
        outs = _adamw(rows_of(wv[n]), red, rows_of(mv[n]), rows_of(vv[n]), f"adamw_{n}")
        grads[n], delta[n], new_m[n], new_v[n] = (o.reshape(wv[n].shape) for o in (red, *outs))
    vec_names = [n for names in VEC_GROUPS for n in names]
    grads.update(zip(vec_names, _unpack(reduced[-1], g_vecs, "unpack_vec_g")))
    outs = _adamw(vec_packs[0], reduced[-1], vec_packs[1], vec_packs[2], "adamw_vec")
    last = outs[0]
    for tag, res, o in zip("dmv", (delta, new_m, new_v), outs):
        res.update(zip(vec_names, _unpack(o, [[wv[n] for n in names] for names in VEC_GROUPS], f"unpack_vec_{tag}")))

    red0 = (_reduce_scatter_finish(got["rs0a"], last, "l0_win")
            + _reduce_scatter_finish(got["rs0b"], last, "l0_rest"))
    for n, outs in big_adam(red0, 0, big1).items():
        grads[n], delta[n], new_m[n], new_v[n] = outs

    return (loss, dx[None], *[_view(n, d[n]) for d in (grads, delta, new_m, new_v) for n in WEIGHTS])
```

```python
import functools
import math

import jax
import jax.numpy as jnp
from jax import lax
from jax.experimental import pallas as pl
from jax.experimental.pallas import tpu as pltpu

F32 = jnp.float32
BF16 = jnp.bfloat16

D_MODEL = 2048
DEPTH = 2
EPS = 1e-6
NEG_INF = -1e30
N_DEV = 8

SSM_WIDTH = 1024
SSM_GROUP = 16
SSM_GROUPS = 64
SSM_STATE = 64
N_SLAB = 8
SLAB_CH = 128
SLAB_ST = 512
SUB = 8
N_GRP = 2
N_SEG = SUB * N_GRP

SG_HEADS = 8
CHUNK = 128
HEAD_DIM = 64
ATT_HEADS = 16
ROT_DIM = 16
ROPE_THETA = 500000.0

D_IN = 13568
OFF_UA, OFF_ZA, OFF_UB, OFF_VB, OFF_ZB, OFF_Q, OFF_KV, OFF_ZC, OFF_G = (
    0, 1024, 2048, 3072, 4096, 5120, 6144, 6400, 7424)

ADAM_LR, ADAM_B1, ADAM_B2, ADAM_EPS, ADAM_WD, ADAM_STEP = 0.001, 0.9, 0.999, 1e-08, 0.01, 10

VMEM_LIMIT = 56 * 1024 * 1024


def _cp(sem=None):
    return pltpu.CompilerParams(dimension_semantics=sem, vmem_limit_bytes=VMEM_LIMIT)


def _dot(a, b):
    return jnp.dot(a, b, preferred_element_type=F32)


def _dot_nt(a, b):
    return lax.dot_general(a, b, (((1,), (1,)), ((), ())), preferred_element_type=F32)


def _dot_tn(a, b):
    return lax.dot_general(a, b, (((0,), (0,)), ((), ())), preferred_element_type=F32)


def _mm(a, b, mode, out_dtype, tm, tn, tk, name, res=None, after=None):
    if mode == "nn":
        (m, k), (_, n) = a.shape, b.shape
    elif mode == "nt":
        (m, k), (n, _) = a.shape, b.shape
    else:
        (k, m), (_, n) = a.shape, b.shape
    tm, tn, tk = min(tm, m), min(tn, n), min(tk, k)
    assert m % tm == 0 and n % tn == 0 and k % tk == 0, (name, m, n, k, tm, tn, tk)
    nk = k // tk
    a_spec = {"nn": pl.BlockSpec((tm, tk), lambda i, j, kk: (i, kk)),
              "nt": pl.BlockSpec((tm, tk), lambda i, j, kk: (i, kk)),
              "tn": pl.BlockSpec((tk, tm), lambda i, j, kk: (kk, i))}[mode]
    b_spec = {"nn": pl.BlockSpec((tk, tn), lambda i, j, kk: (kk, j)),
              "nt": pl.BlockSpec((tn, tk), lambda i, j, kk: (j, kk)),
              "tn": pl.BlockSpec((tk, tn), lambda i, j, kk: (kk, j))}[mode]
    dot = {"nn": _dot, "nt": _dot_nt, "tn": _dot_tn}[mode]
    has_res = res is not None
    direct = out_dtype == F32 and not has_res

    def body(*refs):
        ins, outs = refs[:2 + has_res + (after is not None)], refs[2 + has_res + (after is not None):]
        a_ref, b_ref = ins[:2]
        r_ref = ins[2] if has_res else None
        o_ref = outs[0]
        acc = o_ref if direct else outs[1]
        kk = pl.program_id(2)

        @pl.when(kk == 0)
        def _():
            acc[...] = jnp.zeros_like(acc)

        acc[...] += dot(a_ref[...].astype(BF16), b_ref[...].astype(BF16))

        if not direct:
            @pl.when(kk == nk - 1)
            def _():
                r = acc[...]
                if has_res:
                    r = r + r_ref[...]
                o_ref[...] = r.astype(out_dtype)

    in_specs = [a_spec, b_spec]
    args = [a, b]
    if has_res:
        in_specs.append(pl.BlockSpec((tm, tn), lambda i, j, kk: (i, j)))
        args.append(res)
    if after is not None:
        in_specs.append(pl.BlockSpec(memory_space=pl.ANY))
        args.append(after)
    return pl.pallas_call(
        body, name=name,
        grid=(m // tm, n // tn, nk),
        in_specs=in_specs,
        out_specs=pl.BlockSpec((tm, tn), lambda i, j, kk: (i, j)),
        out_shape=jax.ShapeDtypeStruct((m, n), out_dtype),
        scratch_shapes=[] if direct else [pltpu.VMEM((tm, tn), F32)],
        compiler_params=_cp(("parallel", "parallel", "arbitrary")),
    )(*args)


PROJ_TN = 256


def _in_proj_tiles(h, win_t, tiles, carry, name, after=None):
    L, K = h.shape
    extra = [a for a in (carry, after) if a is not None]

    def body(t_ref, h_ref, w_ref, *rest):
        rest[len(extra)][...] = _dot_nt(h_ref[...], w_ref[...]).astype(BF16)

    return pl.pallas_call(
        body, name=name,
        grid_spec=pltpu.PrefetchScalarGridSpec(
            num_scalar_prefetch=1, grid=(tiles.shape[0],),
            in_specs=[pl.BlockSpec((L, K), lambda j, t: (0, 0)), pl.BlockSpec((PROJ_TN, K), lambda j, t: (t[j], 0))]
            + [pl.BlockSpec(memory_space=pl.ANY)] * len(extra),
            out_specs=pl.BlockSpec((L, PROJ_TN), lambda j, t: (0, t[j]))),
        out_shape=jax.ShapeDtypeStruct((L, win_t.shape[0]), BF16),
        input_output_aliases={} if carry is None else {3: 0},
        compiler_params=_cp(("arbitrary",)),
    )(tiles, h, win_t, *extra)


def _rms(x, w):
    return x * lax.rsqrt(jnp.mean(x * x, axis=-1, keepdims=True) + EPS) * w


def _rms_fwd(x, w, name):
    L, D = x.shape
    tm = min(L, 256)

    def body(x_ref, w_ref, h_ref):
        h_ref[...] = _rms(x_ref[...], w_ref[...]).astype(BF16)

    return pl.pallas_call(
        body, name=name, grid=(L // tm,),
        in_specs=[pl.BlockSpec((tm, D), lambda i: (i, 0)), pl.BlockSpec((1, D), lambda i: (0, 0))],
        out_specs=pl.BlockSpec((tm, D), lambda i: (i, 0)),
        out_shape=jax.ShapeDtypeStruct((L, D), BF16),
        compiler_params=_cp(("parallel",)),
    )(x, w)


def _rms_bwd(x, w, dh, dres, name):
    L, D = x.shape
    tm = min(L, 256)

    def body(x_ref, w_ref, dh_ref, dres_ref, dx_ref, dw_ref):
        _, vjp = jax.vjp(_rms, x_ref[...], w_ref[...])
        dx, dw = vjp(dh_ref[...])
        dx_ref[...] = dx + dres_ref[...]

        @pl.when(pl.program_id(0) == 0)
        def _():
            dw_ref[...] = jnp.zeros_like(dw_ref)

        dw_ref[...] += dw

    row = pl.BlockSpec((tm, D), lambda i: (i, 0))
    vec = pl.BlockSpec((1, D), lambda i: (0, 0))
    return pl.pallas_call(
        body, name=name, grid=(L // tm,),
        in_specs=[row, vec, row, row],
        out_specs=[row, vec],
        out_shape=[jax.ShapeDtypeStruct((L, D), F32), jax.ShapeDtypeStruct((1, D), F32)],
        compiler_params=_cp(("arbitrary",)),
    )(x, w, dh, dres)


def _final(x, fw, tgt, name):
    L, D = x.shape
    tm = min(L, 256)

    def loss_fn(xv, wv, tv):
        err = _rms(xv, wv) - tv
        return jnp.sum(err * err) * (0.5 / D)

    def body(x_ref, w_ref, t_ref, loss_ref, dx_ref, dw_ref):
        tv = t_ref[...]
        val, vjp = jax.vjp(lambda a, b: loss_fn(a, b, tv), x_ref[...], w_ref[...])
        dx, dw = vjp(jnp.ones((), F32))
        dx_ref[...] = dx

        @pl.when(pl.program_id(0) == 0)
        def _():
            dw_ref[...] = jnp.zeros_like(dw_ref)
            loss_ref[...] = jnp.zeros_like(loss_ref)

        dw_ref[...] += dw
        loss_ref[...] += jnp.full(loss_ref.shape, val, F32)

    row = pl.BlockSpec((tm, D), lambda i: (i, 0))
    vec = pl.BlockSpec((1, D), lambda i: (0, 0))
    return pl.pallas_call(
        body, name=name, grid=(L // tm,),
        in_specs=[row, vec, row],
        out_specs=[pl.BlockSpec((8, 128), lambda i: (0, 0)), row, vec],
        out_shape=[jax.ShapeDtypeStruct((8, 128), F32), jax.ShapeDtypeStruct((L, D), F32),
                   jax.ShapeDtypeStruct((1, D), F32)],
        compiler_params=_cp(("arbitrary",)),
    )(x, fw, tgt)


def _s5_param_fn(a_re, a_im, log_dt, bt_re, bt_im):
    dt = jnp.exp(log_dt)
    zr, zi = a_re * dt, a_im * dt
    er = jnp.exp(zr)
    lr, li = er * jnp.cos(zi), er * jnp.sin(zi)
    nr, ni = lr - 1.0, li
    den = a_re * a_re + a_im * a_im
    cr = (nr * a_re + ni * a_im) / den
    ci = (ni * a_re - nr * a_im) / den
    bbr = cr[None] * bt_re - ci[None] * bt_im
    bbi = cr[None] * bt_im + ci[None] * bt_re
    return lr, li, bbr, bbi


def _s5_params_fwd(a_re, a_im, log_dt, bt_re, bt_im, name):
    def body(ar, ai, ld, br, bi, lr, li, bbr, bbi):
        o = _s5_param_fn(ar[...], ai[...], ld[...], br[...], bi[...])
        lr[...], li[...], bbr[...], bbi[...] = o

    gp = jax.ShapeDtypeStruct(a_re.shape, F32)
    cgp = jax.ShapeDtypeStruct(bt_re.shape, F32)
    return pl.pallas_call(body, name=name, out_shape=[gp, gp, cgp, cgp])(a_re, a_im, log_dt, bt_re, bt_im)


def _s5_params_bwd(a_re, a_im, log_dt, bt_re, bt_im, dlr, dli, dbbr, dbbi, name):
    def body(ar, ai, ld, br, bi, g0, g1, g2, g3, o0, o1, o2, o3, o4):
        _, vjp = jax.vjp(_s5_param_fn, ar[...], ai[...], ld[...], br[...], bi[...])
        o0[...], o1[...], o2[...], o3[...], o4[...] = vjp((g0[...], g1[...], g2[...], g3[...]))

    gp = jax.ShapeDtypeStruct(a_re.shape, F32)
    cgp = jax.ShapeDtypeStruct(bt_re.shape, F32)
    return pl.pallas_call(body, name=name,
                          out_shape=[gp, gp, jax.ShapeDtypeStruct(log_dt.shape, F32), cgp, cgp])(
        a_re, a_im, log_dt, bt_re, bt_im, dlr, dli, dbbr, dbbi)


def _cmul(ar, ai, br, bi):
    return ar * br - ai * bi, ar * bi + ai * br


def _cpow(lr, li, n):
    rr, ri = None, None
    br, bi = lr, li
    while n:
        if n & 1:
            rr, ri = (br, bi) if rr is None else _cmul(rr, ri, br, bi)
        n >>= 1
        if n:
            br, bi = _cmul(br, bi, br, bi)
    return rr, ri


def _shift_rows(x, up):
    row = lax.broadcasted_iota(jnp.int32, x.shape, 0)
    if up:
        return jnp.where(row == SUB - 1, 0.0, pltpu.roll(x, SUB - 1, 0))
    return jnp.where(row == 0, 0.0, pltpu.roll(x, 1, 0))


NT = SLAB_ST // 128


def _lam_tiles(lr_ref, li_ref):
    return [(lr_ref[:, j * 128:(j + 1) * 128], li_ref[:, j * 128:(j + 1) * 128]) for j in range(NT)]


def _row_on_sublanes(ref, j, t):
    return ref[j, pl.ds(t, SUB, stride=0), :]


def _pow_table(pw_re, pw_im, lam_t, seg):
    assert seg % 8 == 0 and (seg // 8) & (seg // 8 - 1) == 0
    for j in range(NT):
        lr, li = lam_t[j][0][0:1], lam_t[j][1][0:1]
        r, i_ = lr, li
        for row in range(8):
            pw_re[j, row:row + 1, :] = r
            pw_im[j, row:row + 1, :] = i_
            if row < 7:
                r, i_ = _cmul(r, i_, lr, li)
        n = 8
        while n < seg:
            qr, qi = _cpow(lr, li, n)
            nr, ni = _cmul(pw_re[j, 0:n, :], pw_im[j, 0:n, :], qr, qi)
            pw_re[j, n:2 * n, :] = nr
            pw_im[j, n:2 * n, :] = ni
            n *= 2


def _seg_scan(s_re, s_im, lam_t, pw_re, pw_im, seg, reverse, prev=None):
    sgn = -1.0 if reverse else 1.0
    lt = [(lr, sgn * li) for lr, li in lam_t]
    tiles = [(g, j) for g in range(N_GRP) for j in range(NT)]
    zeros = jnp.zeros((SUB, 128), F32)

    def rows(g, i):
        return pl.ds(pl.multiple_of((g * seg + i) * SUB, SUB), SUB)

    def step1(t, carry):
        i = seg - 1 - t if reverse else t
        out = []
        for n, (g, j) in enumerate(tiles):
            nr, ni = _cmul(lt[j][0], lt[j][1], carry[2 * n], carry[2 * n + 1])
            nr = nr + s_re[j, rows(g, i), :]
            ni = ni + s_im[j, rows(g, i), :]
            s_re[j, rows(g, i), :] = nr
            s_im[j, rows(g, i), :] = ni
            out += [nr, ni]
        return tuple(out)

    zero = tuple(zeros for _ in range(2 * len(tiles)))
    ends = lax.fori_loop(0, seg, step1, zero)

    carries = [None] * (2 * len(tiles))
    row = lax.broadcasted_iota(jnp.int32, (SUB, 128), 0)
    dist = (SUB - 1 - row) if reverse else row
    edge = 0 if reverse else SUB - 1
    for j in range(NT):
        pr, pi = _cpow(lt[j][0], lt[j][1], seg)
        qr, qi = jnp.ones((SUB, 128), F32), zeros
        for s in range(1, SUB):
            tr, ti = _cmul(qr, qi, pr, pi)
            qr, qi = jnp.where(dist >= s, tr, qr), jnp.where(dist >= s, ti, qi)
        boundary = None
        for g in (reversed(range(N_GRP)) if reverse else range(N_GRP)):
            n = g * NT + j
            cr, ci = zeros, zeros
            for _ in range(SUB - 1):
                tr, ti = _cmul(pr, pi, cr, ci)
                cr = _shift_rows(tr + ends[2 * n], reverse)
                ci = _shift_rows(ti + ends[2 * n + 1], reverse)
            if boundary is not None:
                tr, ti = _cmul(qr, qi, boundary[0], boundary[1])
                cr, ci = cr + tr, ci + ti
            carries[2 * n], carries[2 * n + 1] = cr, ci
            fr, fi = _cmul(pr, pi, cr, ci)
            boundary = (jnp.broadcast_to((fr + ends[2 * n])[edge:edge + 1], (SUB, 128)),
                        jnp.broadcast_to((fi + ends[2 * n + 1])[edge:edge + 1], (SUB, 128)))

    def fix(t, i, acc, before):
        out = []
        pws = [(_row_on_sublanes(pw_re, j, t), sgn * _row_on_sublanes(pw_im, j, t)) for j in range(NT)]
        for n, (g, j) in enumerate(tiles):
            ar, ai = _cmul(pws[j][0], pws[j][1], carries[2 * n], carries[2 * n + 1])
            ar = ar + s_re[j, rows(g, i), :]
            ai = ai + s_im[j, rows(g, i), :]
            s_re[j, rows(g, i), :] = ar
            s_im[j, rows(g, i), :] = ai
            if before is not None:
                qr, qi = before(n)
                out += [acc[2 * n] + ar * qr + ai * qi, acc[2 * n + 1] + ai * qr - ar * qi]
        return tuple(out)

    if prev is None:
        lax.fori_loop(0, seg, lambda t, c: fix(t, seg - 1 - t if reverse else t, c, None), ())
        return carries
    assert reverse
    p_re, p_im, p_carries = prev

    def earlier(t):
        return lambda n: (p_re[tiles[n][1], rows(tiles[n][0], seg - 2 - t), :],
                          p_im[tiles[n][1], rows(tiles[n][0], seg - 2 - t), :])

    acc = lax.fori_loop(0, seg - 1, lambda t, c: fix(t, seg - 1 - t, c, earlier(t)), zero)
    acc = fix(seg - 1, 0, acc, lambda n: (p_carries[2 * n], p_carries[2 * n + 1]))
    return carries, [sum(acc[2 * (g * NT + j) + part] for g in range(N_GRP)) for j in range(NT) for part in range(2)]


def _seg_slice(k, seg):
    g, r = divmod(k, SUB)
    return pl.ds(g * seg * SUB + r, seg, stride=SUB)


def _seg_rows(ref, k, seg):
    return jnp.concatenate([ref[j, _seg_slice(k, seg), :] for j in range(NT)], axis=-1)


def _seg_store(ref, k, seg, val):
    for j in range(NT):
        ref[j, _seg_slice(k, seg), :] = val[:, j * 128:(j + 1) * 128]


def _s5_specs(L):
    col = lambda off: pl.BlockSpec((L, SLAB_CH), lambda j: (0, off + j))
    mat_b = pl.BlockSpec((None, SLAB_CH, SLAB_ST), lambda j: (j, 0, 0))
    mat_c = pl.BlockSpec((None, SLAB_ST, SLAB_CH), lambda j: (j, 0, 0))
    vec_s = pl.BlockSpec((None, SUB, SLAB_ST), lambda j: (j, 0, 0))
    vec_c = pl.BlockSpec((None, 1, SLAB_CH), lambda j: (j, 0, 0))
    return col, mat_b, mat_c, vec_s, vec_c


def _s5_states(u_ref, bre_ref, bim_ref, lam_t, pw_re, pw_im, s_re, s_im, seg):
    _pow_table(pw_re, pw_im, lam_t, seg)
    for k in range(N_SEG):
        uk = u_ref[pl.ds(k * seg, seg), :]
        _seg_store(s_re, k, seg, _dot(uk, bre_ref[...]))
        _seg_store(s_im, k, seg, _dot(uk, bim_ref[...]))
    return _seg_scan(s_re, s_im, lam_t, pw_re, pw_im, seg, reverse=False)


def _s5_fwd(proj, bre, bim, cre_t, cim_t, lam_re, lam_im, dvec, name):
    L = proj.shape[0]
    seg = L // N_SEG
    col, mat_b, mat_c, vec_s, vec_c = _s5_specs(L)
    rows = N_SEG * seg

    def body(u_ref, bre_ref, bim_ref, cre_ref, cim_ref, lr_ref, li_ref, d_ref, y_ref, s_re, s_im, pw_re, pw_im):
        _s5_states(u_ref, bre_ref, bim_ref, _lam_tiles(lr_ref, li_ref), pw_re, pw_im, s_re, s_im, seg)
        for k in range(N_SEG):
            y = (_dot(_seg_rows(s_re, k, seg).astype(BF16), cre_ref[...])
                 - _dot(_seg_rows(s_im, k, seg).astype(BF16), cim_ref[...]))
            y = y + d_ref[...] * u_ref[pl.ds(k * seg, seg), :].astype(F32)
            y_ref[pl.ds(k * seg, seg), :] = jax.nn.gelu(y).astype(BF16)

    return pl.pallas_call(
        body, name=name, grid=(N_SLAB,),
        in_specs=[col(OFF_UA // SLAB_CH), mat_b, mat_b, mat_c, mat_c, vec_s, vec_s, vec_c],
        out_specs=pl.BlockSpec((L, SLAB_CH), lambda j: (0, j)),
        out_shape=jax.ShapeDtypeStruct((L, SSM_WIDTH), BF16),
        scratch_shapes=[pltpu.VMEM((NT, rows, 128), F32)] * 2 + [pltpu.VMEM((NT, seg, 128), F32)] * 2,
        compiler_params=_cp(("parallel",)),
    )(proj, bre, bim, cre_t, cim_t, lam_re, lam_im, dvec)


def _s5_bwd(proj, dy, bre, bim, cre_t, cim_t, lam_re, lam_im, dvec, name):
    L = proj.shape[0]
    seg = L // N_SEG
    col, mat_b, mat_c, vec_s, vec_c = _s5_specs(L)
    rows = N_SEG * seg
    dlam_spec = pl.BlockSpec((None, 1, SLAB_ST), lambda j: (j, 0, 0))

    def body(u_ref, dy_ref, bre_ref, bim_ref, cre_ref, cim_ref, lr_ref, li_ref, d_ref,
             du_ref, dbre_ref, dbim_ref, dcre_ref, dcim_ref, dlr_ref, dli_ref, dd_ref,
             s_re, s_im, a_re, a_im, pw_re, pw_im, dyp):
        lam_t = _lam_tiles(lr_ref, li_ref)
        carry_s = _s5_states(u_ref, bre_ref, bim_ref, lam_t, pw_re, pw_im, s_re, s_im, seg)
        dcre = jnp.zeros((SLAB_ST, SLAB_CH), F32)
        dcim = jnp.zeros((SLAB_ST, SLAB_CH), F32)
        dd = jnp.zeros((1, SLAB_CH), F32)
        for k in range(N_SEG):
            sre = _seg_rows(s_re, k, seg).astype(BF16)
            sim = _seg_rows(s_im, k, seg).astype(BF16)
            uk = u_ref[pl.ds(k * seg, seg), :].astype(F32)
            ypre = _dot(sre, cre_ref[...]) - _dot(sim, cim_ref[...]) + d_ref[...] * uk
            _, vjp = jax.vjp(jax.nn.gelu, ypre)
            (dyk,) = vjp(dy_ref[pl.ds(k * seg, seg), :].astype(F32))
            dyp[pl.ds(k * seg, seg), :] = dyk
            dd = dd + jnp.sum(dyk * uk, axis=0, keepdims=True)
            dyb = dyk.astype(BF16)
            dcre = dcre + _dot_tn(sre, dyb)
            dcim = dcim - _dot_tn(sim, dyb)
            _seg_store(a_re, k, seg, _dot_nt(dyb, cre_ref[...]))
            _seg_store(a_im, k, seg, -_dot_nt(dyb, cim_ref[...]))
        dcre_ref[...] = dcre
        dcim_ref[...] = dcim
        dd_ref[...] = dd

        _, acc = _seg_scan(a_re, a_im, lam_t, pw_re, pw_im, seg, reverse=True, prev=(s_re, s_im, carry_s))
        dlr_ref[...] = jnp.concatenate([jnp.sum(acc[2 * j], axis=0, keepdims=True) for j in range(NT)], axis=-1)
        dli_ref[...] = jnp.concatenate([jnp.sum(acc[2 * j + 1], axis=0, keepdims=True) for j in range(NT)], axis=-1)

        dbre = jnp.zeros((SLAB_CH, SLAB_ST), F32)
        dbim = jnp.zeros((SLAB_CH, SLAB_ST), F32)
        for k in range(N_SEG):
            are = _seg_rows(a_re, k, seg).astype(BF16)
            aim = _seg_rows(a_im, k, seg).astype(BF16)
            uk = u_ref[pl.ds(k * seg, seg), :]
            du = _dot_nt(are, bre_ref[...]) + _dot_nt(aim, bim_ref[...]) + dyp[pl.ds(k * seg, seg), :] * d_ref[...]
            du_ref[pl.ds(k * seg, seg), :] = du.astype(BF16)
            dbre = dbre + _dot_tn(uk, are)
            dbim = dbim + _dot_tn(uk, aim)
        dbre_ref[...] = dbre
        dbim_ref[...] = dbim

    scan_buf = pltpu.VMEM((NT, rows, 128), F32)
    pow_buf = pltpu.VMEM((NT, seg, 128), F32)
    return pl.pallas_call(
        body, name=name, grid=(N_SLAB,),
        in_specs=[col(OFF_UA // SLAB_CH), pl.BlockSpec((L, SLAB_CH), lambda j: (0, j)),
                  mat_b, mat_b, mat_c, mat_c, vec_s, vec_s, vec_c],
        out_specs=[pl.BlockSpec((L, SLAB_CH), lambda j: (0, j)), mat_b, mat_b, mat_c, mat_c, dlam_spec, dlam_spec, vec_c],
        out_shape=[jax.ShapeDtypeStruct((L, SSM_WIDTH), BF16),
                   jax.ShapeDtypeStruct((N_SLAB, SLAB_CH, SLAB_ST), F32),
                   jax.ShapeDtypeStruct((N_SLAB, SLAB_CH, SLAB_ST), F32),
                   jax.ShapeDtypeStruct((N_SLAB, SLAB_ST, SLAB_CH), F32),
                   jax.ShapeDtypeStruct((N_SLAB, SLAB_ST, SLAB_CH), F32),
                   jax.ShapeDtypeStruct((N_SLAB, 1, SLAB_ST), F32),
                   jax.ShapeDtypeStruct((N_SLAB, 1, SLAB_ST), F32),
                   jax.ShapeDtypeStruct((N_SLAB, 1, SLAB_CH), F32)],
        scratch_shapes=[scan_buf, scan_buf, scan_buf, scan_buf, pow_buf, pow_buf, pltpu.VMEM((L, SLAB_CH), F32)],
        compiler_params=_cp(("parallel",)),
    )(proj, dy, bre, bim, cre_t, cim_t, lam_re, lam_im, dvec)


def _glu_point(y0, pre, za, b):
    return y0 * jax.nn.sigmoid(pre + b) * jax.nn.silu(za)


def _glu_specs(L, tm):
    row = pl.BlockSpec((tm, SSM_WIDTH), lambda i: (i, 0))
    za = pl.BlockSpec((tm, SSM_WIDTH), lambda i: (i, OFF_ZA // SSM_WIDTH))
    wmat = pl.BlockSpec((SSM_WIDTH, SSM_WIDTH), lambda i: (0, 0))
    vec = pl.BlockSpec((1, SSM_WIDTH), lambda i: (0, 0))
    return row, za, wmat, vec


def _glu_fwd(ya0, proj, w, b, name):
    L = ya0.shape[0]
    tm = min(L, 512)
    row, za, wmat, vec = _glu_specs(L, tm)

    def body(y_ref, z_ref, w_ref, b_ref, o_ref):
        y0 = y_ref[...]
        pre = _dot(y0, w_ref[...])
        o_ref[...] = _glu_point(y0.astype(F32), pre, z_ref[...].astype(F32), b_ref[...]).astype(BF16)

    return pl.pallas_call(
        body, name=name, grid=(L // tm,), in_specs=[row, za, wmat, vec], out_specs=row,
        out_shape=jax.ShapeDtypeStruct((L, SSM_WIDTH), BF16), compiler_params=_cp(("parallel",)),
    )(ya0, proj, w, b)


def _glu_bwd(ya0, proj, w, b, dya, name):
    L = ya0.shape[0]
    tm = min(L, 512)
    row, za, wmat, vec = _glu_specs(L, tm)

    def body(y_ref, z_ref, w_ref, b_ref, g_ref, dy0_ref, dza_ref, dw_ref, db_ref):
        y0 = y_ref[...]
        pre = _dot(y0, w_ref[...])
        _, vjp = jax.vjp(_glu_point, y0.astype(F32), pre, z_ref[...].astype(F32), b_ref[...])
        dy0, dpre, dza, db = vjp(g_ref[...].astype(F32))
        dpb = dpre.astype(BF16)
        dy0_ref[...] = (dy0 + _dot_nt(dpb, w_ref[...])).astype(BF16)
        dza_ref[...] = dza.astype(BF16)

        @pl.when(pl.program_id(0) == 0)
        def _():
            dw_ref[...] = jnp.zeros_like(dw_ref)
            db_ref[...] = jnp.zeros_like(db_ref)

        dw_ref[...] += _dot_tn(y0, dpb)
        db_ref[...] += db

    return pl.pallas_call(
        body, name=name, grid=(L // tm,), in_specs=[row, za, wmat, vec, row],
        out_specs=[row, row, wmat, vec],
        out_shape=[jax.ShapeDtypeStruct((L, SSM_WIDTH), BF16), jax.ShapeDtypeStruct((L, SSM_WIDTH), BF16),
                   jax.ShapeDtypeStruct((SSM_WIDTH, SSM_WIDTH), F32), jax.ShapeDtypeStruct((1, SSM_WIDTH), F32)],
        compiler_params=_cp(("arbitrary",)),
    )(ya0, proj, w, b, dya)


def _sg_norm(vb, ln_w, ln_b):
    v0 = jax.nn.gelu(vb)
    mu = jnp.mean(v0, axis=-1, keepdims=True)
    var = jnp.mean(jnp.square(v0 - mu), axis=-1, keepdims=True)
    return (v0 - mu) * lax.rsqrt(var + EPS) * ln_w + ln_b


def _sg_gate(ub, mixed, zb):
    return jax.nn.gelu(ub) * mixed * jax.nn.silu(zb)


def _sg_specs():
    W = SSM_WIDTH
    blk = lambda off: pl.BlockSpec((CHUNK, W), lambda n: (n, off // W))
    out = pl.BlockSpec((CHUNK, W), lambda n: (n, 0))
    vec = pl.BlockSpec((1, W), lambda n: (0, 0))
    wsp = pl.BlockSpec((SG_HEADS, CHUNK, CHUNK), lambda n: (0, 0, 0))
    bsp = pl.BlockSpec((SG_HEADS, CHUNK, 1), lambda n: (0, 0, 0))
    return blk, out, vec, wsp, bsp


def _sg_masked(w_ref):
    t = lax.broadcasted_iota(jnp.int32, (CHUNK, CHUNK), 0)
    s = lax.broadcasted_iota(jnp.int32, (CHUNK, CHUNK), 1)
    causal = s <= t
    return causal, [jnp.where(causal, w_ref[h], 0.0).astype(BF16) for h in range(SG_HEADS)]


def _sg_mix(wm, vnb, bias_ref):
    return jnp.concatenate(
        [_dot(wm[h], vnb[:, h * CHUNK:(h + 1) * CHUNK]) + bias_ref[h] for h in range(SG_HEADS)], axis=-1)


def _sg_fwd(proj, ln_w, ln_b, w, bias, name):
    L = proj.shape[0]
    blk, out, vec, wsp, bsp = _sg_specs()

    def body(ub_ref, vb_ref, zb_ref, lw_ref, lb_ref, w_ref, bias_ref, o_ref):
        _, wm = _sg_masked(w_ref)
        vnb = _sg_norm(vb_ref[...].astype(F32), lw_ref[...], lb_ref[...]).astype(BF16)
        mixed = _sg_mix(wm, vnb, bias_ref)
        o_ref[...] = _sg_gate(ub_ref[...].astype(F32), mixed, zb_ref[...].astype(F32)).astype(BF16)

    return pl.pallas_call(
        body, name=name, grid=(L // CHUNK,),
        in_specs=[blk(OFF_UB), blk(OFF_VB), blk(OFF_ZB), vec, vec, wsp, bsp], out_specs=out,
        out_shape=jax.ShapeDtypeStruct((L, SSM_WIDTH), BF16), compiler_params=_cp(("parallel",)),
    )(proj, proj, proj, ln_w, ln_b, w, bias)


def _sg_bwd(proj, ln_w, ln_b, w, bias, dyb, name):
    L = proj.shape[0]
    blk, out, vec, wsp, bsp = _sg_specs()

    def body(ub_ref, vb_ref, zb_ref, lw_ref, lb_ref, w_ref, bias_ref, g_ref,
             dub_ref, dvb_ref, dzb_ref, dlw_ref, dlb_ref, dw_ref, dbias_ref):
        causal, wm = _sg_masked(w_ref)
        vb = vb_ref[...].astype(F32)
        vn, vjp_norm = jax.vjp(_sg_norm, vb, lw_ref[...], lb_ref[...])
        vnb = vn.astype(BF16)
        mixed = _sg_mix(wm, vnb, bias_ref)
        _, vjp_gate = jax.vjp(_sg_gate, ub_ref[...].astype(F32), mixed, zb_ref[...].astype(F32))
        dub, dmixed, dzb = vjp_gate(g_ref[...].astype(F32))
        dub_ref[...] = dub.astype(BF16)
        dzb_ref[...] = dzb.astype(BF16)

        @pl.when(pl.program_id(0) == 0)
        def _():
            dlw_ref[...] = jnp.zeros_like(dlw_ref)
            dlb_ref[...] = jnp.zeros_like(dlb_ref)
            dw_ref[...] = jnp.zeros_like(dw_ref)
            dbias_ref[...] = jnp.zeros_like(dbias_ref)

        dvn = []
        for h in range(SG_HEADS):
            dm = dmixed[:, h * CHUNK:(h + 1) * CHUNK]
            dmb = dm.astype(BF16)
            dbias_ref[h] += jnp.sum(dm, axis=-1, keepdims=True)
            dw_ref[h] += jnp.where(causal, _dot_nt(dmb, vnb[:, h * CHUNK:(h + 1) * CHUNK]), 0.0)
            dvn.append(_dot_tn(wm[h], dmb))
        dvb, dlw, dlb = vjp_norm(jnp.concatenate(dvn, axis=-1))
        dvb_ref[...] = dvb.astype(BF16)
        dlw_ref[...] += dlw
        dlb_ref[...] += dlb

    act = jax.ShapeDtypeStruct((L, SSM_WIDTH), BF16)
    return pl.pallas_call(
        body, name=name, grid=(L // CHUNK,),
        in_specs=[blk(OFF_UB), blk(OFF_VB), blk(OFF_ZB), vec, vec, wsp, bsp, out],
        out_specs=[out, out, out, vec, vec, wsp, bsp],
        out_shape=[act, act, act, jax.ShapeDtypeStruct((1, SSM_WIDTH), F32), jax.ShapeDtypeStruct((1, SSM_WIDTH), F32),
                   jax.ShapeDtypeStruct((SG_HEADS, CHUNK, CHUNK), F32), jax.ShapeDtypeStruct((SG_HEADS, CHUNK, 1), F32)],
        compiler_params=_cp(("arbitrary",)),
    )(proj, proj, proj, ln_w, ln_b, w, bias, dyb)


def _rope_tables(L):
    half = ROT_DIM // 2
    inv_freq = ROPE_THETA ** (-jnp.arange(0, ROT_DIM, 2, dtype=F32) / ROT_DIM)
    ang = jnp.arange(L, dtype=F32)[:, None] * inv_freq[None, :]
    cos, sin = jnp.cos(ang), jnp.sin(ang)
    ones = jnp.ones((L, HEAD_DIM - ROT_DIM), F32)
    cos_h = jnp.concatenate([cos, cos, ones], axis=-1)
    sin_h = jnp.concatenate([-sin, sin, 0.0 * ones], axis=-1)
    src = jnp.arange(HEAD_DIM)[:, None]
    dst = jnp.arange(HEAD_DIM)[None, :]
    p_h = (((dst < half) & (src == dst + half)) | ((dst >= half) & (dst < ROT_DIM) & (src == dst - half))).astype(F32)
    p2 = jnp.kron(jnp.eye(2, dtype=F32), p_h).astype(BF16)
    return jnp.tile(cos_h, (1, 2)), jnp.tile(sin_h, (1, 2)), p2


def _rope(t, cos, sin, p2):
    n = t.shape[1] // 128
    tb = t.astype(BF16)
    sw = jnp.concatenate([_dot(tb[:, i * 128:(i + 1) * 128], p2) for i in range(n)], axis=-1) if n > 1 else _dot(tb, p2)
    return t * jnp.tile(cos, (1, n)) + sw * jnp.tile(sin, (1, n))


def _rope_t(g, cos, sin, p2):
    n = g.shape[1] // 128
    gs = (g * jnp.tile(sin, (1, n))).astype(BF16)
    sw = jnp.concatenate([_dot_nt(gs[:, i * 128:(i + 1) * 128], p2) for i in range(n)], axis=-1) if n > 1 else _dot_nt(gs, p2)
    return g * jnp.tile(cos, (1, n)) + sw


def _lane_lo(shape):
    return (lax.broadcasted_iota(jnp.int32, shape, len(shape) - 1) % 128) < HEAD_DIM


def _dup_halves(x):
    xr = pltpu.roll(x, HEAD_DIM, 1)
    lo = _lane_lo(x.shape)
    return jnp.where(lo, x, xr), jnp.where(lo, xr, x)


def _fold_halves(d0, d1):
    f0 = d0 + pltpu.roll(d0, HEAD_DIM, 1)
    f1 = d1 + pltpu.roll(d1, HEAD_DIM, 1)
    return jnp.where(_lane_lo(d0.shape), f0, f1)


def _attn_mask():
    qi = lax.broadcasted_iota(jnp.int32, (CHUNK, 2 * CHUNK), 0)
    kj = lax.broadcasted_iota(jnp.int32, (CHUNK, 2 * CHUNK), 1)
    return qi, kj


def _attn_specs():
    qsp = pl.BlockSpec((CHUNK, 1024), lambda n: (n, OFF_Q // 1024))
    kv_cur = pl.BlockSpec((CHUNK, 256), lambda n: (n, OFF_KV // 256))
    kv_prev = pl.BlockSpec((CHUNK, 256), lambda n: (jnp.maximum(n - 1, 0), OFF_KV // 256))
    zsp = [pl.BlockSpec((CHUNK, 256), functools.partial(lambda n, q: (n, OFF_ZC // 256 + q), q=q)) for q in range(4)]
    tab_cur = pl.BlockSpec((CHUNK, 128), lambda n: (n, 0))
    tab_prev = pl.BlockSpec((CHUNK, 128), lambda n: (jnp.maximum(n - 1, 0), 0))
    p2sp = pl.BlockSpec((128, 128), lambda n: (0, 0))
    sink = pl.BlockSpec(memory_space=pltpu.SMEM)
    wide = pl.BlockSpec((CHUNK, 1024), lambda n: (n, 0))
    return qsp, kv_cur, kv_prev, zsp, tab_cur, tab_prev, p2sp, sink, wide


def _attn_prep(n, q_ref, kvc_ref, kvp_ref, cosc_ref, sinc_ref, cosp_ref, sinp_ref, p2_ref):
    p2 = p2_ref[...]
    qr = _rope(q_ref[...].astype(F32), cosc_ref[...], sinc_ref[...], p2).astype(BF16)
    kc = _rope(kvc_ref[:, 0:128].astype(F32), cosc_ref[...], sinc_ref[...], p2)
    kp = _rope(kvp_ref[:, 0:128].astype(F32), cosp_ref[...], sinp_ref[...], p2)
    k_all = jnp.concatenate([kp, kc], axis=0).astype(BF16)
    v_all = jnp.concatenate([kvp_ref[:, 128:256], kvc_ref[:, 128:256]], axis=0)
    qi, kj = _attn_mask()
    allowed = ((kj < CHUNK) & (kj > qi) & (n > 0)) | ((kj >= CHUNK) & (kj - CHUNK <= qi))
    return qr, _dup_halves(k_all), _dup_halves(v_all), allowed, _lane_lo((CHUNK, 128))


def _attn_head(qr, kd, sink_ref, h, allowed, lo):
    m, half, g = h // 2, h % 2, h // 8
    qp = qr[:, m * 128:(m + 1) * 128]
    qm = jnp.where(lo if half == 0 else ~lo, qp, jnp.zeros_like(qp))
    s = jnp.where(allowed, _dot_nt(qm, kd[g]) * (HEAD_DIM ** -0.5), NEG_INF)
    snk = sink_ref[h]
    mx = jnp.maximum(jnp.max(s, axis=-1, keepdims=True), snk)
    e = jnp.exp(s - mx)
    es = jnp.exp(snk - mx)
    inv = 1.0 / (jnp.sum(e, axis=-1, keepdims=True) + es)
    return qm, e * inv, es * inv


def _silu_gate(o, z):
    return o * jax.nn.silu(z)


def _pair_lanes(refs, m):
    return refs[m // 2][:, (m % 2) * 128:(m % 2 + 1) * 128]


def _attn_fwd(proj, sinks, tabs, name):
    L = proj.shape[0]
    cos2, sin2, p2 = tabs
    qsp, kv_cur, kv_prev, zsp, tab_cur, tab_prev, p2sp, sink, wide = _attn_specs()

    def body(q_ref, kvc_ref, kvp_ref, z0, z1, z2, z3, cosc, sinc, cosp, sinp, p2_ref, sink_ref, y_ref, o_ref):
        n = pl.program_id(0)
        qr, kd, vd, allowed, lo = _attn_prep(n, q_ref, kvc_ref, kvp_ref, cosc, sinc, cosp, sinp, p2_ref)
        probs = [_attn_head(qr, kd, sink_ref, h, allowed, lo)[1].astype(BF16) for h in range(ATT_HEADS)]
        for m in range(ATT_HEADS // 2):
            g = m // 4
            o0 = _dot(probs[2 * m], vd[g])
            o1 = _dot(probs[2 * m + 1], vd[g])
            o = jnp.where(lo, o0, o1).astype(BF16)
            o_ref[:, m * 128:(m + 1) * 128] = o
            z = _pair_lanes((z0, z1, z2, z3), m).astype(F32)
            y_ref[:, m * 128:(m + 1) * 128] = _silu_gate(o.astype(F32), z).astype(BF16)

    act = jax.ShapeDtypeStruct((L, 1024), BF16)
    return pl.pallas_call(
        body, name=name, grid=(L // CHUNK,),
        in_specs=[qsp, kv_cur, kv_prev, *zsp, tab_cur, tab_cur, tab_prev, tab_prev, p2sp, sink],
        out_specs=[wide, wide], out_shape=[act, act], compiler_params=_cp(("parallel",)),
    )(proj, proj, proj, proj, proj, proj, proj, cos2, sin2, cos2, sin2, p2, sinks)


def _attn_bwd(proj, sinks, tabs, o_att, dyc, name):
    L = proj.shape[0]
    cos2, sin2, p2 = tabs
    qsp, kv_cur, kv_prev, zsp, tab_cur, tab_prev, p2sp, sink, wide = _attn_specs()
    kvo = pl.BlockSpec((CHUNK, 256), lambda n: (n, 0))

    def body(q_ref, kvc_ref, kvp_ref, z0, z1, z2, z3, cosc, sinc, cosp, sinp, p2_ref, sink_ref, o_ref, g_ref,
             dq_ref, dz_ref, dkvc_ref, dkvp_ref, dsink_ref):
        n = pl.program_id(0)
        qr, kd, vd, allowed, lo = _attn_prep(n, q_ref, kvc_ref, kvp_ref, cosc, sinc, cosp, sinp, p2_ref)
        p2 = p2_ref[...]

        @pl.when(n == 0)
        def _():
            dsink_ref[...] = jnp.zeros_like(dsink_ref)

        dkd = [jnp.zeros((2 * CHUNK, 128), F32), jnp.zeros((2 * CHUNK, 128), F32)]
        dvd = [jnp.zeros((2 * CHUNK, 128), F32), jnp.zeros((2 * CHUNK, 128), F32)]
        probs = [_attn_head(qr, kd, sink_ref, h, allowed, lo) for h in range(ATT_HEADS)]
        for m in range(ATT_HEADS // 2):
            g = m // 4
            lanes = slice(m * 128, (m + 1) * 128)
            z = _pair_lanes((z0, z1, z2, z3), m).astype(F32)
            _, vjp = jax.vjp(_silu_gate, o_ref[:, lanes].astype(F32), z)
            do, dz = vjp(g_ref[:, lanes].astype(F32))
            dz_ref[:, lanes] = dz.astype(BF16)
            dop = do.astype(BF16)
            dq_h = []
            for half in range(2):
                h = 2 * m + half
                qm, p, ps = probs[h]
                dom = jnp.where(lo if half == 0 else ~lo, dop, jnp.zeros_like(dop))
                dp = _dot_nt(dom, vd[g])
                rs = jnp.sum(p * dp, axis=-1, keepdims=True)
                ds = (p * (dp - rs) * (HEAD_DIM ** -0.5)).astype(BF16)
                dsink_ref[h:h + 1, :] += jnp.broadcast_to(jnp.sum(-ps * rs, axis=0, keepdims=True), (1, 128))
                dq_h.append(_dot(ds, kd[g]))
                dkd[g] = dkd[g] + _dot_tn(ds, qm)
                dvd[g] = dvd[g] + _dot_tn(p.astype(BF16), dom)
            dq_ref[:, lanes] = _rope_t(jnp.where(lo, dq_h[0], dq_h[1]), cosc[...], sinc[...], p2).astype(BF16)
        dk_rot = _fold_halves(dkd[0], dkd[1])
        dv = _fold_halves(dvd[0], dvd[1])
        dkp = _rope_t(dk_rot[0:CHUNK], cosp[...], sinp[...], p2)
        dkc = _rope_t(dk_rot[CHUNK:2 * CHUNK], cosc[...], sinc[...], p2)
        dkvp_ref[...] = jnp.concatenate([dkp, dv[0:CHUNK]], axis=-1)
        dkvc_ref[...] = jnp.concatenate([dkc, dv[CHUNK:2 * CHUNK]], axis=-1)

    act = jax.ShapeDtypeStruct((L, 1024), BF16)
    kvs = jax.ShapeDtypeStruct((L, 256), F32)
    return pl.pallas_call(
        body, name=name, grid=(L // CHUNK,),
        in_specs=[qsp, kv_cur, kv_prev, *zsp, tab_cur, tab_cur, tab_prev, tab_prev, p2sp, sink, wide, wide],
        out_specs=[wide, wide, kvo, kvo, pl.BlockSpec((ATT_HEADS, 128), lambda n: (0, 0))],
        out_shape=[act, act, kvs, kvs, jax.ShapeDtypeStruct((ATT_HEADS, 128), F32)],
        compiler_params=_cp(("arbitrary",)),
    )(proj, proj, proj, proj, proj, proj, proj, cos2, sin2, cos2, sin2, p2, sinks, o_att, dyc)


MERGE_TN = 256


def _merge_point(ta, tb, tc, ga, gb, gc):
    return jax.nn.sigmoid(ga) * ta + jax.nn.sigmoid(gb) * tb + jax.nn.sigmoid(gc) * tc


def _merge_specs(tm):
    nj = D_MODEL // MERGE_TN
    t = pl.BlockSpec((tm, MERGE_TN), lambda i, j: (i, j))
    gates = [pl.BlockSpec((tm, MERGE_TN), functools.partial(lambda i, j, b: (i, OFF_G // MERGE_TN + b * nj + j), b=b))
             for b in range(3)]
    return t, gates, nj


def _merge_fwd(ta, tb, tc, proj, name):
    L = ta.shape[0]
    tm = min(L, 1024)
    t, gates, nj = _merge_specs(tm)

    def body(ta_ref, tb_ref, tc_ref, ga_ref, gb_ref, gc_ref, o_ref):
        f = lambda r: r[...].astype(F32)
        o_ref[...] = _merge_point(f(ta_ref), f(tb_ref), f(tc_ref), f(ga_ref), f(gb_ref), f(gc_ref)).astype(BF16)

    return pl.pallas_call(
        body, name=name, grid=(L // tm, nj), in_specs=[t, t, t, *gates], out_specs=t,
        out_shape=jax.ShapeDtypeStruct((L, D_MODEL), BF16), compiler_params=_cp(("parallel", "parallel")),
    )(ta, tb, tc, proj, proj, proj)


def _merge_bwd(ta, tb, tc, proj, dm, name):
    L = ta.shape[0]
    tm = min(L, 1024)
    t, gates, nj = _merge_specs(tm)

    def body(ta_ref, tb_ref, tc_ref, ga_ref, gb_ref, gc_ref, dm_ref, dta_ref, dtb_ref, dtc_ref, dga_ref, dgb_ref, dgc_ref):
        f = lambda r: r[...].astype(F32)
        _, vjp = jax.vjp(_merge_point, f(ta_ref), f(tb_ref), f(tc_ref), f(ga_ref), f(gb_ref), f(gc_ref))
        outs = vjp(f(dm_ref))
        for r, v in zip((dta_ref, dtb_ref, dtc_ref, dga_ref, dgb_ref, dgc_ref), outs):
            r[...] = v.astype(BF16)

    act = jax.ShapeDtypeStruct((L, D_MODEL), BF16)
    return pl.pallas_call(
        body, name=name, grid=(L // tm, nj), in_specs=[t, t, t, *gates, t],
        out_specs=[t] * 6, out_shape=[act] * 6,
        compiler_params=_cp(("parallel", "parallel")),
    )(ta, tb, tc, proj, proj, proj, dm)


GRAD_DT = BF16
SMALL = ("norm_w", "ssm_a_re", "ssm_a_im", "ssm_log_dt", "ssm_b_re", "ssm_b_im", "ssm_c_re", "ssm_c_im", "ssm_d",
         "ssm_glu_b", "sg_ln_w", "sg_ln_b", "sg_w", "sg_b", "attn_sinks")
G8 = SSM_GROUPS // N_SLAB


def _diag_mask(rows_per_group, cols_per_group):
    r = jnp.arange(G8 * rows_per_group)[:, None] // rows_per_group
    c = jnp.arange(G8 * cols_per_group)[None, :] // cols_per_group
    return r == c


def _slab_b(bb_t):
    x = bb_t.transpose(1, 0, 2).reshape(N_SLAB, SLAB_CH, SSM_STATE)
    return jnp.where(_diag_mask(SSM_GROUP, SSM_STATE), jnp.tile(x, (1, 1, G8)), 0)


def _unslab_b(d):
    x = jnp.where(_diag_mask(SSM_GROUP, SSM_STATE), d, 0).reshape(N_SLAB, SLAB_CH, G8, SSM_STATE).sum(axis=2)
    return x.reshape(SSM_GROUPS, SSM_GROUP, SSM_STATE).transpose(1, 0, 2)


def _slab_c(c):
    x = c.transpose(0, 2, 1).reshape(N_SLAB, SLAB_ST, SSM_GROUP)
    return jnp.where(_diag_mask(SSM_STATE, SSM_GROUP), jnp.tile(x, (1, 1, G8)), 0)


def _unslab_c(d):
    x = jnp.where(_diag_mask(SSM_STATE, SSM_GROUP), d, 0).reshape(N_SLAB, SLAB_ST, G8, SSM_GROUP).sum(axis=2)
    return x.reshape(SSM_GROUPS, SSM_STATE, SSM_GROUP).transpose(0, 2, 1)


def _s5_prep(p, tag):
    bt_re = p["ssm_b_re"].transpose(2, 0, 1)
    bt_im = p["ssm_b_im"].transpose(2, 0, 1)
    raw = (p["ssm_a_re"], p["ssm_a_im"], p["ssm_log_dt"][:, None], bt_re, bt_im)
    lr, li, bbr, bbi = _s5_params_fwd(*raw, name=f"s5_params_{tag}")
    ops = (_slab_b(bbr).astype(BF16), _slab_b(bbi).astype(BF16),
           _slab_c(p["ssm_c_re"]).astype(BF16), _slab_c(p["ssm_c_im"]).astype(BF16),
           jnp.broadcast_to(lr.reshape(N_SLAB, 1, SLAB_ST), (N_SLAB, SUB, SLAB_ST)),
           jnp.broadcast_to(li.reshape(N_SLAB, 1, SLAB_ST), (N_SLAB, SUB, SLAB_ST)),
           p["ssm_d"].reshape(N_SLAB, 1, SLAB_CH))
    return raw, ops


def _layer_fwd(x, p, w, tabs, tag, s5=None, proj_of=None, after_proj=None):
    L = x.shape[0]
    h = _rms_fwd(x, p["norm_w"][None], f"rms_fwd_{tag}")
    if proj_of is not None:
        proj = proj_of(h)
    else:
        proj = _mm(h, w["win_t"], "nt", BF16, L, PROJ_TN, D_MODEL, f"in_proj_{tag}")
    if after_proj is not None:
        w = after_proj(proj)
    s5_raw, s5_ops = s5 if s5 is not None else _s5_prep(p, tag)
    ya0 = _s5_fwd(proj, *s5_ops, name=f"s5_fwd_{tag}")
    ya = _glu_fwd(ya0, proj, w["glu"], p["ssm_glu_b"][None], f"glu_fwd_{tag}")
    yb = _sg_fwd(proj, p["sg_ln_w"][None], p["sg_ln_b"][None], p["sg_w"], p["sg_b"][:, :, None], f"sg_fwd_{tag}")
    yc, o_att = _attn_fwd(proj, p["attn_sinks"], tabs, f"attn_fwd_{tag}")
    ta = _mm(ya, w["wba_t"], "nt", BF16, 1024, 1024, 1024, f"branch_a_{tag}")
    tb = _mm(yb, w["wbb_t"], "nt", BF16, 1024, 1024, 1024, f"branch_b_{tag}")
    tc = _mm(yc, w["wbc_t"], "nt", BF16, 1024, 1024, 1024, f"branch_c_{tag}")
    merged = _merge_fwd(ta, tb, tc, proj, f"merge_fwd_{tag}")
    x_new = _mm(merged, w["wout"], "nn", F32, 1024, 512, D_MODEL, f"out_proj_{tag}", res=x)
    saved = dict(x=x, h=h, proj=proj, s5_raw=s5_raw, s5_ops=s5_ops, ya0=ya0, ya=ya, yb=yb, yc=yc, o_att=o_att,
                 ta=ta, tb=tb, tc=tc, merged=merged)
    return x_new, saved


def _layer_bwd(dx_out, p, w, tabs, s, tag, first_after=None, after_merge=None, before_win=None, after_win=None):
    L = dx_out.shape[0]
    proj = s["proj"]
    big, small = {}, {}
    dmerged = _mm(dx_out, w["wout"], "nt", BF16, 1024, 512, D_MODEL, f"d_merged_{tag}", after=first_after)
    big["wout"] = _mm(s["merged"], dx_out, "tn", GRAD_DT, 512, 1024, L, f"d_wout_{tag}")
    dta, dtb, dtc, dga, dgb, dgc = _merge_bwd(s["ta"], s["tb"], s["tc"], proj, dmerged, f"merge_bwd_{tag}")
    tok = after_merge(dga) if after_merge is not None else None
    dy = {}
    for br, dt in (("a", dta), ("b", dtb), ("c", dtc)):
        dy[br] = _mm(dt, w[f"wb{br}_t"], "nn", BF16, 1024, 1024, D_MODEL, f"d_y{br}_{tag}", after=tok)
        big[f"wb{br}_t"] = _mm(dt, s[f"y{br}"], "tn", GRAD_DT, 512, 1024, L, f"d_wb{br}_{tag}")

    dq, dzc, dkvc, dkvp, dsink = _attn_bwd(proj, p["attn_sinks"], tabs, s["o_att"], dy["c"], f"attn_bwd_{tag}")
    dkv = dkvc + jnp.concatenate([dkvp[CHUNK:], jnp.zeros((CHUNK, 256), F32)], axis=0)
    small["attn_sinks"] = dsink[:, 0]

    dub, dvb, dzb, dlw, dlb, dsgw, dsgb = _sg_bwd(
        proj, p["sg_ln_w"][None], p["sg_ln_b"][None], p["sg_w"], p["sg_b"][:, :, None], dy["b"], f"sg_bwd_{tag}")
    small.update(sg_ln_w=dlw[0], sg_ln_b=dlb[0], sg_w=dsgw, sg_b=dsgb[:, :, 0])

    dya0, dza, dglu, dglub = _glu_bwd(s["ya0"], proj, w["glu"], p["ssm_glu_b"][None], dy["a"], f"glu_bwd_{tag}")
    big["glu"] = dglu.astype(GRAD_DT)
    small["ssm_glu_b"] = dglub[0]

    dua, dbre, dbim, dcre, dcim, dlr, dli, dd = _s5_bwd(proj, dya0, *s["s5_ops"], name=f"s5_bwd_{tag}")
    da_re, da_im, dlog_dt, dbt_re, dbt_im = _s5_params_bwd(
        *s["s5_raw"], dlr.reshape(SSM_GROUPS, SSM_STATE), dli.reshape(SSM_GROUPS, SSM_STATE),
        _unslab_b(dbre), _unslab_b(dbim), name=f"s5_params_bwd_{tag}")
    small.update(ssm_a_re=da_re, ssm_a_im=da_im, ssm_log_dt=dlog_dt[:, 0],
                 ssm_bt_re=dbt_re, ssm_bt_im=dbt_im,
                 ssm_c_re=_unslab_c(dcre), ssm_c_im=_unslab_c(dcim), ssm_d=dd.reshape(SSM_WIDTH))

    dproj = jnp.concatenate([dua, dza, dub, dvb, dzb, dq, dkv.astype(BF16), dzc, dga, dgb, dgc], axis=-1)
    tok = before_win(big) if before_win is not None else None
    big["win_t"] = _mm(dproj, s["h"], "tn", GRAD_DT, 256, D_MODEL, L, f"d_win_{tag}", after=tok)
    tok = after_win(big) if after_win is not None else None
    dh = _mm(dproj, w["win_t"], "nn", F32, L, D_MODEL, 256, f"d_h_{tag}", after=tok)
    dx_in, dnw = _rms_bwd(s["x"], p["norm_w"][None], dh, dx_out, f"rms_bwd_{tag}")
    small["norm_w"] = dnw[0]
    return dx_in, big, small


def _local_step(x, tgt, small_p, final_w, big_w):
    L = x.shape[0]
    tabs = _rope_tables(L)
    saved = []
    for l in range(DEPTH):
        x, s = _layer_fwd(x, small_p[l], big_w[l], tabs, f"l{l}")
        saved.append(s)
    loss_acc, dx, dfw = _final(x, final_w[None], tgt, "final_norm_loss")
    big_g, small_g = [None] * DEPTH, [None] * DEPTH
    for l in reversed(range(DEPTH)):
        dx, big_g[l], small_g[l] = _layer_bwd(dx, small_p[l], big_w[l], tabs, saved[l], f"l{l}")
    return loss_acc[0, 0], dx, dfw[0], big_g, small_g


MESH = pl.DeviceIdType.MESH
ANY = pl.BlockSpec(memory_space=pl.ANY)
ROW_ALIGN = 16


def _place():
    return lax.axis_index("x"), lax.axis_index("y"), lax.axis_index("c")


HBM = pl.BlockSpec(memory_space=pltpu.HBM)
SEM = pl.BlockSpec(memory_space=pltpu.SEMAPHORE)
EFFECT = pltpu.SideEffectType.DATAFLOW_SIDE_EFFECTING


def _split_start(srcs, lands, n_copies, copies, name, after=None):
    n, m, k = len(srcs), len(lands), n_copies
    extra = [] if after is None else [after]

    def body(*refs):
        src_refs, land_refs = refs[:n], refs[n:n + m]
        sems = refs[n + m + len(extra):]
        send_sems, recv_sems, token = sems[:k], sems[k:2 * k], refs[-1]
        for cp in copies(src_refs, land_refs, send_sems, recv_sems):
            cp.start()
        token[...] = jnp.zeros_like(token)

    ops = list(srcs) + list(lands)
    outs = pl.pallas_call(
        body, name=name,
        out_shape=(*[pltpu.SemaphoreType.DMA(())] * (2 * k),
                   *[pltpu.HBM(a.shape, a.dtype) for a in ops], jax.ShapeDtypeStruct((8, 128), F32)),
        in_specs=[HBM] * (n + m) + [ANY] * len(extra),
        out_specs=(*[SEM] * (2 * k), *[HBM] * (n + m), pl.BlockSpec(memory_space=pltpu.VMEM)),
        input_output_aliases={i: 2 * k + i for i in range(n + m)},
        compiler_params=pltpu.CompilerParams(has_side_effects=EFFECT),
    )(*[pltpu.with_memory_space_constraint(a, pltpu.HBM) for a in ops], *extra)
    return (list(outs[:k]), list(outs[k:2 * k]), list(outs[2 * k:2 * k + n]), list(outs[2 * k + n:2 * k + n + m]),
            outs[-1])


def _split_wait(send_sems, recv_sems, srcs, lands, after, copies, name):
    n, m, k = len(srcs), len(lands), len(send_sems)
    after = list(after) if isinstance(after, (list, tuple)) else [after]

    def body(*refs):
        src_refs, land_refs = refs[:n], refs[n:n + m]
        for cp in copies(src_refs, land_refs, refs[n + m:n + m + k], refs[n + m + k:n + m + 2 * k]):
            cp.wait_send()
            cp.wait_recv()

    ops = list(srcs) + list(lands)
    outs = pl.pallas_call(
        body, name=name,
        out_shape=tuple(pltpu.HBM(a.shape, a.dtype) for a in ops),
        in_specs=[HBM] * (n + m) + [SEM] * (2 * k) + [ANY] * len(after),
        out_specs=tuple([HBM] * (n + m)),
        input_output_aliases={i: i for i in range(n + m)},
        compiler_params=pltpu.CompilerParams(has_side_effects=EFFECT),
    )(*ops, *send_sems, *recv_sems, *after)
    return list(outs[:n]), list(outs[n:])


def _ag_rows(land_ref, px, py, pc):
    r = land_ref.shape[0] // N_DEV
    start = pl.multiple_of((4 * px + 2 * py + pc) * r, ROW_ALIGN)
    return land_ref.at[pl.ds(start, r), :]


def _ag_copies_to(which):
    def copies(src_refs, land_refs, send_sems, recv_sems):
        x, y, c = _place()
        peers = [(x, y, 1 - c), (1 - x, y, c), (x, 1 - y, c), (1 - x, 1 - y, c)]
        return [pltpu.make_async_remote_copy(
            src_ref=_ag_rows(land_refs[a], x, y, c), dst_ref=_ag_rows(land_refs[a], x, y, c),
            send_sem=send_sems[len(which) * a + k], recv_sem=recv_sems[len(which) * a + k],
            device_id=peers[p], device_id_type=MESH)
            for a in range(len(land_refs)) for k, p in enumerate(which)]
    return copies


_ag_copies = _ag_copies_to((0, 1, 2, 3))
_ag_copies_near = _ag_copies_to((0, 1, 2))
_ag_copies_far = _ag_copies_to((3,))


def _ag_forward(lands, name, which=(0, 1, 2)):
    n = len(lands)

    def body(*refs):
        land_refs = refs[n:2 * n]
        send_sems, recv_sems = refs[2 * n:]
        x, y, c = _place()
        chips = [(1 - x, y), (x, 1 - y), (1 - x, 1 - y)]

        def copy(a, k, pc):
            px, py = chips[which[k]]
            return pltpu.make_async_remote_copy(
                src_ref=_ag_rows(land_refs[a], px, py, pc), dst_ref=_ag_rows(land_refs[a], px, py, pc),
                send_sem=send_sems.at[a, k], recv_sem=recv_sems.at[a, k], device_id=(x, y, 1 - c), device_id_type=MESH)

        passed = [copy(a, k, c) for a in range(n) for k in range(len(which))]
        for cp in passed:
            cp.start()
        for a in range(n):
            for k in range(len(which)):
                copy(a, k, 1 - c).wait_recv()
        for cp in passed:
            cp.wait_send()

    sems = pltpu.SemaphoreType.DMA((n, len(which)))
    return pl.pallas_call(
        body, name=name,
        in_specs=[ANY] * n, out_specs=[ANY] * n,
        out_shape=[jax.ShapeDtypeStruct(l.shape, l.dtype) for l in lands],
        input_output_aliases={i: i for i in range(n)},
        scratch_shapes=[sems, sems],
    )(*lands)


def _allgather_place(shards):
    x, y, c = _place()
    return [lax.dynamic_update_slice(lax.empty((N_DEV * s.shape[0], s.shape[1]), s.dtype), s,
                                     ((4 * x + 2 * y + c) * s.shape[0], 0)) for s in shards]


def _allgather_start(lands, name, after=None):
    return _split_start([], lands, 4 * len(lands), _ag_copies, name + "_start", after=after)


def _allgather_finish(started, after, name):
    send_sems, recv_sems, _, lands, _ = started
    _, lands = _split_wait(send_sems, recv_sems, [], lands, after, _ag_copies, name + "_wait")
    return list(_ag_forward(lands, name + "_forward"))


def _rs_swap_cores(grads, name):
    n = len(grads)

    def body(*refs):
        ins, outs = refs[:n], refs[n:2 * n]
        send_sems, recv_sems = refs[2 * n:]
        x, y, c = _place()
        cps = []
        for a in range(n):
            r = ins[a].shape[0] // N_DEV
            for q in range(4):
                start = pl.multiple_of((2 * q + 1 - c) * r, ROW_ALIGN)
                cps.append(pltpu.make_async_remote_copy(
                    src_ref=ins[a].at[pl.ds(start, r), :], dst_ref=outs[a].at[q],
                    send_sem=send_sems.at[a, q], recv_sem=recv_sems.at[a, q],
                    device_id=(x, y, 1 - c), device_id_type=MESH))
        for cp in cps:
            cp.start()
        for cp in cps:
            cp.wait()

    return pl.pallas_call(
        body, name=name, in_specs=[ANY] * n, out_specs=[ANY] * n,
        out_shape=[jax.ShapeDtypeStruct((4, g.shape[0] // N_DEV, g.shape[1]), g.dtype) for g in grads],
        scratch_shapes=[pltpu.SemaphoreType.DMA((n, 4)), pltpu.SemaphoreType.DMA((n, 4))],
    )(*grads)


def _rs_chip_copies(sum_refs, land_refs, send_sems, recv_sems):
    x, y, c = _place()
    chips = [(1 - x, y), (x, 1 - y), (1 - x, 1 - y)]
    return [pltpu.make_async_remote_copy(
        src_ref=sum_refs[a].at[2 * px + py], dst_ref=land_refs[a].at[2 * x + y],
        send_sem=send_sems[3 * a + j], recv_sem=recv_sems[3 * a + j], device_id=(px, py, c), device_id_type=MESH)
        for a in range(len(sum_refs)) for j, (px, py) in enumerate(chips)]


def _row_tile(r):
    return max(t for t in range(ROW_ALIGN, min(r, 1024) + 1, ROW_ALIGN) if r % t == 0)


def _rs_add_cores(grad, recv, cidx, name):
    r, cols = recv.shape[1], recv.shape[2]
    tr = _row_tile(r)
    nb = r // tr

    def body(c_ref, g_ref, r_ref, o_ref):
        o_ref[...] = (g_ref[...].astype(F32) + r_ref[...].astype(F32)).astype(o_ref.dtype)

    return pl.pallas_call(
        body, name=name,
        grid_spec=pltpu.PrefetchScalarGridSpec(
            num_scalar_prefetch=1, grid=(4, nb),
            in_specs=[pl.BlockSpec((tr, cols), lambda q, i, c_ref: ((2 * q + c_ref[0]) * nb + i, 0)),
                      pl.BlockSpec((None, tr, cols), lambda q, i, c_ref: (q, i, 0))],
            out_specs=pl.BlockSpec((None, tr, cols), lambda q, i, c_ref: (q, i, 0))),
        out_shape=jax.ShapeDtypeStruct(recv.shape, recv.dtype),
        compiler_params=_cp(("parallel", "parallel")),
    )(cidx, grad, recv)


def _rs_add_chips(own, recv, slots, name):
    r, cols = recv.shape[1], recv.shape[2]
    tr = _row_tile(r)

    def body(s_ref, o_ref, r0_ref, r1_ref, r2_ref, out_ref):
        acc = o_ref[...].astype(F32)
        for ref in (r0_ref, r1_ref, r2_ref):
            acc = acc + ref[...].astype(F32)
        out_ref[...] = acc

    pick = lambda k: pl.BlockSpec((None, tr, cols), functools.partial(lambda i, s_ref, k: (s_ref[k], i, 0), k=k))
    return pl.pallas_call(
        body, name=name,
        grid_spec=pltpu.PrefetchScalarGridSpec(
            num_scalar_prefetch=1, grid=(r // tr,),
            in_specs=[pick(0), pick(1), pick(2), pick(3)],
            out_specs=pl.BlockSpec((tr, cols), lambda i, s_ref: (i, 0))),
        out_shape=jax.ShapeDtypeStruct((r, cols), F32),
        compiler_params=_cp(("parallel",)),
    )(slots, own, recv, recv, recv)


def _rs_core_copies(grad_refs, land_refs, send_sems, recv_sems):
    x, y, c = _place()
    cps = []
    for a in range(len(grad_refs)):
        r = grad_refs[a].shape[0] // N_DEV
        for q in range(4):
            start = pl.multiple_of((2 * q + 1 - c) * r, ROW_ALIGN)
            cps.append(pltpu.make_async_remote_copy(
                src_ref=grad_refs[a].at[pl.ds(start, r), :], dst_ref=land_refs[a].at[q],
                send_sem=send_sems[4 * a + q], recv_sem=recv_sems[4 * a + q],
                device_id=(x, y, 1 - c), device_id_type=MESH))
    return cps


def _reduce_scatter_chips_start(grads, recv, tag):
    cidx = lax.axis_index("c").astype(jnp.int32)[None]
    sums = [_rs_add_cores(g, rv, cidx, f"rs_add_cores_{tag}_{i}") for i, (g, rv) in enumerate(zip(grads, recv))]
    lands = [lax.empty(s.shape, s.dtype) for s in sums]
    return _split_start(sums, lands, 3 * len(sums), _rs_chip_copies, f"rs_chips_{tag}_start")


def _reduce_scatter_start(grads, tag):
    return _reduce_scatter_chips_start(grads, _rs_swap_cores(grads, f"rs_swap_cores_{tag}"), tag)


def _reduce_scatter_cores_start(grads, tag):
    lands = [lax.empty((4, g.shape[0] // N_DEV, g.shape[1]), g.dtype) for g in grads]
    return _split_start(grads, lands, 4 * len(grads), _rs_core_copies, f"rs_cores_{tag}_start")


def _reduce_scatter_cores_finish(started, after, tag):
    send_sems, recv_sems, grads, lands, _ = started
    grads, recv = _split_wait(send_sems, recv_sems, grads, lands, after, _rs_core_copies, f"rs_cores_{tag}_wait")
    return _reduce_scatter_chips_start(grads, recv, tag)


def _reduce_scatter_finish(started, after, tag):
    send_sems, recv_sems, sums, lands, _ = started
    sums, lands = _split_wait(send_sems, recv_sems, sums, lands, after, _rs_chip_copies, f"rs_chips_{tag}_wait")
    x, y = lax.axis_index("x"), lax.axis_index("y")
    slots = jnp.stack([2 * x + y, 2 * (1 - x) + y, 2 * x + 1 - y, 2 * (1 - x) + 1 - y]).astype(jnp.int32)
    return [_rs_add_chips(s, l, slots, f"rs_add_chips_{tag}_{i}") for i, (s, l) in enumerate(zip(sums, lands))]


def _ar_peers():
    x, y, c = _place()
    return [(1 - x if k & 4 else x, 1 - y if k & 2 else y, 1 - c if k & 1 else c) for k in range(1, N_DEV)]


def _ar_scatter_copies(src_refs, land_refs, send_sems, recv_sems):
    x, y, c = _place()
    cps = []
    for a in range(len(src_refs)):
        rs = src_refs[a].shape[0] // N_DEV
        for k, (px, py, pc) in enumerate(_ar_peers()):
            start = pl.multiple_of((4 * px + 2 * py + pc) * rs, 8)
            cps.append(pltpu.make_async_remote_copy(
                src_ref=src_refs[a].at[pl.ds(start, rs), :], dst_ref=land_refs[a].at[4 * x + 2 * y + c],
                send_sem=send_sems[7 * a + k], recv_sem=recv_sems[7 * a + k],
                device_id=(px, py, pc), device_id_type=MESH))
    return cps


def _ar_gather_copies(src_refs, land_refs, send_sems, recv_sems):
    x, y, c = _place()
    cps = []
    for a in range(len(land_refs)):
        rs = land_refs[a].shape[0] // N_DEV
        mine = land_refs[a].at[pl.ds(pl.multiple_of((4 * x + 2 * y + c) * rs, 8), rs), :]
        for k, peer in enumerate(_ar_peers()):
            cps.append(pltpu.make_async_remote_copy(
                src_ref=mine, dst_ref=mine, send_sem=send_sems[7 * a + k], recv_sem=recv_sems[7 * a + k],
                device_id=peer, device_id_type=MESH))
    return cps


def _allreduce_start(packs, name, after=None):
    assert all(p.shape[0] % (8 * N_DEV) == 0 for p in packs)
    x, y, c = _place()
    me = 4 * x + 2 * y + c
    lands = []
    for p in packs:
        rs = p.shape[0] // N_DEV
        own = lax.dynamic_slice(p, (me * rs, 0), (rs, p.shape[1]))
        lands.append(lax.dynamic_update_slice(lax.empty((N_DEV, rs, p.shape[1]), F32), own[None], (me, 0, 0)))
    return _split_start(packs, lands, 7 * len(packs), _ar_scatter_copies, name + "_scatter_start", after=after)


def _allreduce_middle(started, after, name):
    n = len(started[2])
    _, parts = _split_wait(started[0], started[1], started[2], started[3], after, _ar_scatter_copies, name + "_scatter_wait")

    def body(*refs):
        for p_ref, o_ref in zip(refs[:n], refs[n:]):
            acc = p_ref[0]
            for d in range(1, N_DEV):
                acc = acc + p_ref[d]
            o_ref[...] = acc

    sums = pl.pallas_call(body, name=name + "_add",
                          out_shape=[jax.ShapeDtypeStruct(p.shape[1:], F32) for p in parts])(*parts)
    x, y, c = _place()
    me = 4 * x + 2 * y + c
    lands = [lax.dynamic_update_slice(lax.empty((N_DEV * s.shape[0], s.shape[1]), F32), s, (me * s.shape[0], 0))
             for s in sums]
    return _split_start([], lands, 7 * n, _ar_gather_copies, name + "_gather_start")


def _allreduce_finish(started, after, name):
    _, lands = _split_wait(started[0], started[1], [], started[3], after, _ar_gather_copies, name + "_gather_wait")
    return lands


ADAM_TILE_BYTES = 2 * 1024 * 1024


def _adam_tiles(rows, cols):
    tc = cols // 2 if cols % 256 == 0 and cols >= 2048 else cols
    tr = max(t for t in range(8, rows + 1, 8) if rows % t == 0 and t * max(tc, 128) * 4 <= ADAM_TILE_BYTES) \
        if rows % 8 == 0 else rows
    return tr, tc


def _adam_math(w, g, m, v):
    nm = ADAM_B1 * m + (1.0 - ADAM_B1) * g
    nv = ADAM_B2 * v + (1.0 - ADAM_B2) * jnp.square(g)
    c1 = 1.0 - ADAM_B1 ** ADAM_STEP
    c2 = 1.0 - ADAM_B2 ** ADAM_STEP
    return -ADAM_LR * ((nm / c1) / (jnp.sqrt(nv / c2) + ADAM_EPS) + ADAM_WD * w), nm, nv


def _adamw_layer(w, g, m, v, layer, carry, name):
    _, rows, cols = w.shape
    tr, tc = _adam_tiles(rows, cols)

    def body(w_ref, g_ref, m_ref, v_ref, *rest):
        go_ref, d_ref, nm_ref, nv_ref = rest[-4:]
        gv = g_ref[...]
        go_ref[...] = gv
        d_ref[...], nm_ref[...], nv_ref[...] = _adam_math(w_ref[...], gv, m_ref[...], v_ref[...])

    blk = pl.BlockSpec((None, tr, tc), lambda i, j: (layer, i, j))
    flat = pl.BlockSpec((tr, tc), lambda i, j: (i, j))
    sh = jax.ShapeDtypeStruct(w.shape, F32)
    carry = [] if carry is None else list(carry)
    return pl.pallas_call(
        body, name=name, grid=(rows // tr, cols // tc),
        in_specs=[blk, flat, blk, blk] + [ANY] * len(carry), out_specs=[blk] * 4, out_shape=[sh] * 4,
        input_output_aliases={4 + k: k for k in range(len(carry))},
        compiler_params=_cp(("parallel", "parallel")),
    )(w, g, m, v, *carry)


def _adamw(w, g, m, v, name):
    shape = w.shape
    rows, cols = shape[-2:]
    lead = shape[:-2]
    nl = math.prod(lead)
    tr, tc = _adam_tiles(rows, cols)

    def body(w_ref, g_ref, m_ref, v_ref, d_ref, nm_ref, nv_ref):
        d_ref[...], nm_ref[...], nv_ref[...] = _adam_math(w_ref[...], g_ref[...], m_ref[...], v_ref[...])

    def index(b, i, j):
        return (*jnp.unravel_index(b, lead), i, j) if lead else (i, j)

    blk = pl.BlockSpec((*[None] * len(lead), tr, tc), index)
    sh = jax.ShapeDtypeStruct(shape, F32)
    return pl.pallas_call(
        body, name=name, grid=(nl, rows // tr, cols // tc), in_specs=[blk] * 4, out_specs=[blk] * 3,
        out_shape=[sh] * 3, compiler_params=_cp(("parallel", "parallel", "parallel")),
    )(w, g, m, v)


WEIGHTS = ("norm_w", "w_in", "ssm_a_re", "ssm_a_im", "ssm_log_dt", "ssm_b_re", "ssm_b_im", "ssm_c_re", "ssm_c_im",
           "ssm_d", "ssm_glu_w", "ssm_glu_b", "sg_ln_w", "sg_ln_b", "sg_w", "sg_b", "attn_sinks",
           "w_branch_a", "w_branch_b", "w_branch_c", "w_out", "final_norm_w")
BIG = ("w_in", "ssm_glu_w", "w_branch_a", "w_branch_b", "w_branch_c", "w_out")
BIG_KEY = {"w_in": ("win_t", True), "ssm_glu_w": ("glu", False), "w_branch_a": ("wba_t", True),
           "w_branch_b": ("wbb_t", True), "w_branch_c": ("wbc_t", True), "w_out": ("wout", False)}
VIEWS = {"w_in": (1, 2), "ssm_b_re": (2, 3), "ssm_b_im": (2, 3)}
MATS = ("ssm_a_re", "ssm_a_im", "ssm_c_re", "ssm_c_im", "ssm_b_re", "ssm_b_im", "sg_w")
VEC_GROUPS = (("ssm_d", "ssm_glu_b", "sg_ln_w", "sg_ln_b"), ("norm_w", "final_norm_w", "sg_b"), ("ssm_log_dt", "attn_sinks"))
PACK_ROWS = 8 * N_DEV


def _view(n, a):
    return jnp.swapaxes(a, *VIEWS[n]) if n in VIEWS else a


def _vec_moves(pack_ref, refs, to_pack):
    d, gb, lw, lb, nw, fw, sb, ld, sk = refs
    full = (slice(None), slice(None))
    moves = [((slice(2 * i, 2 * i + 2), slice(None)), r, full) for i, r in enumerate((d, gb, lw, lb))]
    moves += [((slice(8, 10), slice(None)), nw, (slice(None), slice(0, 1024))),
              ((slice(10, 12), slice(None)), nw, (slice(None), slice(1024, 2048))),
              ((slice(12, 13), slice(None)), fw, (slice(None), slice(0, 1024))),
              ((slice(13, 14), slice(None)), fw, (slice(None), slice(1024, 2048))),
              ((slice(16, 32), slice(0, 128)), sb, full),
              ((slice(32, 34), slice(0, 64)), ld, full),
              ((slice(34, 36), slice(0, 16)), sk, full)]
    for where, ref, part in moves:
        if to_pack:
            pack_ref[where] = ref[part]
        else:
            ref[part] = pack_ref[where]


def _vec_shapes(arrs):
    d, gb, lw, lb, nw, fw, sb, ld, sk = arrs
    return [d, gb, lw, lb, nw, fw.reshape(1, -1), sb.reshape(-1, sb.shape[-1]), ld, sk]


def _vec_pack(arrs, name):
    def body(*refs):
        refs[-1][...] = jnp.zeros_like(refs[-1])
        _vec_moves(refs[-1], refs[:-1], True)

    return pl.pallas_call(body, name=name, out_shape=jax.ShapeDtypeStruct((PACK_ROWS, 1024), F32))(*_vec_shapes(arrs))


def _vec_unpack(pack, like, name):
    shaped = _vec_shapes(like)

    def body(pack_ref, *refs):
        _vec_moves(pack_ref, refs, False)

    outs = pl.pallas_call(body, name=name, out_shape=[jax.ShapeDtypeStruct(a.shape, F32) for a in shaped])(pack)
    return [o.reshape(a.shape) for o, a in zip(outs, like)]


def _pack(groups, cols, name):
    assert cols == 1024
    return _vec_pack([a for arrs in groups for a in arrs], name)


def _unpack(pack, groups, name):
    return _vec_unpack(pack, [a for arrs in groups for a in arrs], name)


def kernel(x, norm_w, w_in, ssm_a_re, ssm_a_im, ssm_log_dt, ssm_b_re, ssm_b_im, ssm_c_re, ssm_c_im, ssm_d, ssm_glu_w, ssm_glu_b, sg_ln_w, sg_ln_b, sg_w, sg_b, attn_sinks, w_branch_a, w_branch_b, w_branch_c, w_out, final_norm_w, loss_target, m_norm_w, m_w_in, m_ssm_a_re, m_ssm_a_im, m_ssm_log_dt, m_ssm_b_re, m_ssm_b_im, m_ssm_c_re, m_ssm_c_im, m_ssm_d, m_ssm_glu_w, m_ssm_glu_b, m_sg_ln_w, m_sg_ln_b, m_sg_w, m_sg_b, m_attn_sinks, m_w_branch_a, m_w_branch_b, m_w_branch_c, m_w_out, m_final_norm_w, v_norm_w, v_w_in, v_ssm_a_re, v_ssm_a_im, v_ssm_log_dt, v_ssm_b_re, v_ssm_b_im, v_ssm_c_re, v_ssm_c_im, v_ssm_d, v_ssm_glu_w, v_ssm_glu_b, v_sg_ln_w, v_sg_ln_b, v_sg_w, v_sg_b, v_attn_sinks, v_w_branch_a, v_w_branch_b, v_w_branch_c, v_w_out, v_final_norm_w):
    w = dict(zip(WEIGHTS, (norm_w, w_in, ssm_a_re, ssm_a_im, ssm_log_dt, ssm_b_re, ssm_b_im, ssm_c_re, ssm_c_im, ssm_d, ssm_glu_w, ssm_glu_b, sg_ln_w, sg_ln_b, sg_w, sg_b, attn_sinks, w_branch_a, w_branch_b, w_branch_c, w_out, final_norm_w)))
    m = dict(zip(WEIGHTS, (m_norm_w, m_w_in, m_ssm_a_re, m_ssm_a_im, m_ssm_log_dt, m_ssm_b_re, m_ssm_b_im, m_ssm_c_re, m_ssm_c_im, m_ssm_d, m_ssm_glu_w, m_ssm_glu_b, m_sg_ln_w, m_sg_ln_b, m_sg_w, m_sg_b, m_attn_sinks, m_w_branch_a, m_w_branch_b, m_w_branch_c, m_w_out, m_final_norm_w)))
    v = dict(zip(WEIGHTS, (v_norm_w, v_w_in, v_ssm_a_re, v_ssm_a_im, v_ssm_log_dt, v_ssm_b_re, v_ssm_b_im, v_ssm_c_re, v_ssm_c_im, v_ssm_d, v_ssm_glu_w, v_ssm_glu_b, v_sg_ln_w, v_sg_ln_b, v_sg_w, v_sg_b, v_attn_sinks, v_w_branch_a, v_w_branch_b, v_w_branch_c, v_w_out, v_final_norm_w)))

    keys = [BIG_KEY[n][0] for n in BIG]
    wv, mv, vv = ({n: _view(n, a) for n, a in d.items()} for d in (w, m, v))
    shards = [[(wv[n][l] if n in VIEWS else w[n][l].T if BIG_KEY[n][1] else w[n][l]).astype(BF16) for n in BIG]
              for l in range(DEPTH)]
    small_p = [{n: w[n][l] for n in SMALL} for l in range(DEPTH)]
    xv, tgt = x[0], loss_target[0]
    tabs = _rope_tables(xv.shape[0])

    lands = [[_allgather_place(shards[l][:1]), _allgather_place(shards[l][1:])] for l in range(DEPTH)]
    s5 = [_s5_prep(small_p[l], f"l{l}") for l in range(DEPTH)]
    vec_packs = [_pack([[d[n] for n in names] for names in VEC_GROUPS], 1024, f"pack_vec_{tag}")
                 for tag, d in (("w", wv), ("m", mv), ("v", vv))]
    near = _split_start([], lands[0][0], 3, _ag_copies_near, "ag_l0_win_near_start")
    got = {}
    x_, y_ = lax.axis_index("x"), lax.axis_index("y")
    n_tiles = D_IN // PROJ_TN
    far_first = (D_IN // 4 // PROJ_TN) * (2 * (1 - x_) + (1 - y_))
    n_far = -(-D_IN // 4 // PROJ_TN)
    tile_ids = jnp.arange(n_tiles, dtype=jnp.int32)
    is_far = (tile_ids >= far_first) & (tile_ids < far_first + n_far)
    near_tiles = jnp.sort(jnp.where(is_far, n_tiles, tile_ids))[:n_tiles - n_far]
    far_tiles = (far_first + jnp.arange(n_far)).astype(jnp.int32)

    def proj_of0(h):
        early = [h, *lands[0][1], *lands[1][0], *lands[1][1], *s5[0][1], *s5[1][1], near_tiles, far_tiles]
        early += vec_packs
        _, land = _split_wait(near[0], near[1], [], near[3], early, _ag_copies_near, "ag_l0_win_near_wait")
        far = _split_start([], land, 1, _ag_copies_far, "ag_l0_win_far_start")
        land = _ag_forward(far[3], "ag_l0_win_near_forward", which=(0, 1))
        got["ag0b"] = _allgather_start(lands[0][1], "ag_l0_rest", after=land[0])
        got["near1"] = _split_start([], lands[1][0], 3, _ag_copies_near, "ag_l1_win_near_start", after=got["ag0b"][4])
        proj = _in_proj_tiles(h, land[0], near_tiles, None, "in_proj_l0_near", after=got["near1"][4])
        _, land = _split_wait(far[0], far[1], [], land, proj, _ag_copies_far, "ag_l0_win_far_wait")
        got["win0"] = _ag_forward(land, "ag_l0_win_far_forward", which=(2,))[0]
        return _in_proj_tiles(h, got["win0"], far_tiles, proj, "in_proj_l0_far")

    def after_proj0(proj):
        got["w0"] = dict(zip(keys, [got["win0"]] + _allgather_finish(got["ag0b"], proj, "ag_l0_rest")))
        return got["w0"]

    x1, saved0 = _layer_fwd(xv, small_p[0], None, tabs, "l0", s5=s5[0], proj_of=proj_of0, after_proj=after_proj0)
    big_w0 = got["w0"]

    def proj_of1(h):
        near1 = got["near1"]
        _, land = _split_wait(near1[0], near1[1], [], near1[3], h, _ag_copies_near, "ag_l1_win_near_wait")
        far1 = _split_start([], land, 1, _ag_copies_far, "ag_l1_win_far_start")
        land = _ag_forward(far1[3], "ag_l1_win_near_forward", which=(0, 1))
        got["ag1b"] = _allgather_start(lands[1][1], "ag_l1_rest", after=land[0])
        proj = _in_proj_tiles(h, land[0], near_tiles, None, "in_proj_l1_near", after=got["ag1b"][4])
        _, land = _split_wait(far1[0], far1[1], [], land, proj, _ag_copies_far, "ag_l1_win_far_wait")
        got["win1"] = _ag_forward(land, "ag_l1_win_far_forward", which=(2,))[0]
        return _in_proj_tiles(h, got["win1"], far_tiles, proj, "in_proj_l1_far")

    def after_proj1(proj):
        got["w1"] = dict(zip(keys, [got["win1"]] + _allgather_finish(got["ag1b"], proj, "ag_l1_rest")))
        return got["w1"]

    x2, saved1 = _layer_fwd(x1, small_p[1], None, tabs, "l1", s5=s5[1], proj_of=proj_of1, after_proj=after_proj1)
    big_w1 = got["w1"]
    loss_acc, dx2, dfw = _final(x2, w["final_norm_w"][None], tgt, "final_norm_loss")
    loss = lax.psum(loss_acc[0, 0], ("x", "y", "c"))
    dfw = dfw[0]

    dx1, big_g1, small_g1 = _layer_bwd(dx2, small_p[1], big_w1, tabs, saved1, "l1")
    rs1_cores = _reduce_scatter_cores_start([big_g1[k] for k in keys], "l1")

    def after_merge0(x):
        got["rs1"] = _reduce_scatter_cores_finish(rs1_cores, x, "l1")
        return got["rs1"][4]

    def before_win0(big):
        got["rs0b"] = _reduce_scatter_start([big[k] for k in keys[1:]], "l0_rest")
        return got["rs0b"][4]

    def after_win0(big):
        got["rs0a"] = _reduce_scatter_start([big["win_t"]], "l0_win")
        return got["rs0a"][4]

    dx, big_g0, small_g0 = _layer_bwd(dx1, small_p[0], big_w0, tabs, saved0, "l0", first_after=rs1_cores[4],
                                      after_merge=after_merge0, before_win=before_win0, after_win=after_win0)
    rs1 = got["rs1"]
    small_g = [small_g0, small_g1]
    grads, delta, new_m, new_v = {}, {}, {}, {}

    def big_adam(red, layer, carry):
        outs = {}
        for i, n in enumerate(BIG):
            g = red[i].T if BIG_KEY[n][1] and n not in VIEWS else red[i]
            outs[n] = _adamw_layer(wv[n], g, mv[n], vv[n], layer, None if carry is None else carry[n], f"adamw_{n}_l{layer}")
        return outs

    def small_grad(n):
        if n == "final_norm_w":
            return dfw
        if n in ("ssm_b_re", "ssm_b_im"):
            return jnp.stack([small_g[l][n.replace("ssm_b_", "ssm_bt_")].transpose(1, 0, 2) for l in range(DEPTH)])
        return jnp.stack([small_g[l][n] for l in range(DEPTH)])

    rows_of = lambda a: a.reshape(-1, a.shape[-1])
    g_mats = [rows_of(small_grad(n)) for n in MATS]
    g_vecs = [[small_grad(n) for n in names] for names in VEC_GROUPS]
    ar = _allreduce_start(g_mats + [_pack(g_vecs, 1024, "pack_vec_g")], "allreduce_small")
    big1 = big_adam(_reduce_scatter_finish(rs1, [dx, ar[4]], "l1"), 1, None)
    ar = _allreduce_middle(ar, [big1[n][1] for n in BIG], "allreduce_small")
    red0 = (_reduce_scatter_finish(got["rs0a"], ar[4], "l0_win")
            + _reduce_scatter_finish(got["rs0b"], ar[4], "l0_rest"))
    big0 = big_adam(red0, 0, big1)
    for n, outs in big0.items():
        grads[n], delta[n], new_m[n], new_v[n] = outs
    reduced = _allreduce_finish(ar, [big0[n][1] for n in BIG], "allreduce_small")
    for n, red in zip(MATS, reduced):
        outs = _adamw(rows_of(wv[n]), red, rows_of(mv[n]), rows_of(vv[n]), f"adamw_{n}")
        grads[n], delta[n], new_m[n], new_v[n] = (o.reshape(wv[n].shape) for o in (red, *outs))
    vec_names = [n for names in VEC_GROUPS for n in names]
    grads.update(zip(vec_names, _unpack(reduced[-1], g_vecs, "unpack_vec_g")))
    outs = _adamw(vec_packs[0], reduced[-1], vec_packs[1], vec_packs[2], "adamw_vec")
    for tag, res, o in zip("dmv", (delta, new_m, new_v), outs):
        res.update(zip(vec_names, _unpack(o, [[wv[n] for n in names] for names in VEC_GROUPS], f"unpack_vec_{tag}")))

    return (loss, dx[None], *[_view(n, d[n]) for d in (grads, delta, new_m, new_v) for n in WEIGHTS])
```

```python
import functools
import math

import jax
import jax.numpy as jnp
from jax import lax
from jax.experimental import pallas as pl
from jax.experimental.pallas import tpu as pltpu

F32 = jnp.float32
BF16 = jnp.bfloat16

D_MODEL = 2048
DEPTH = 2
EPS = 1e-6
NEG_INF = -1e30
N_DEV = 8

SSM_WIDTH = 1024
SSM_GROUP = 16
SSM_GROUPS = 64
SSM_STATE = 64
N_SLAB = 8
SLAB_CH = 128
SLAB_ST = 512
SUB = 8
N_GRP = 2
N_SEG = SUB * N_GRP

SG_HEADS = 8
CHUNK = 128
HEAD_DIM = 64
ATT_HEADS = 16
ROT_DIM = 16
ROPE_THETA = 500000.0

D_IN = 13568
OFF_UA, OFF_ZA, OFF_UB, OFF_VB, OFF_ZB, OFF_Q, OFF_KV, OFF_ZC, OFF_G = (
    0, 1024, 2048, 3072, 4096, 5120, 6144, 6400, 7424)

ADAM_LR, ADAM_B1, ADAM_B2, ADAM_EPS, ADAM_WD, ADAM_STEP = 0.001, 0.9, 0.999, 1e-08, 0.01, 10

VMEM_LIMIT = 56 * 1024 * 1024


def _cp(sem=None):
    return pltpu.CompilerParams(dimension_semantics=sem, vmem_limit_bytes=VMEM_LIMIT)


def _dot(a, b):
    return jnp.dot(a, b, preferred_element_type=F32)


def _dot_nt(a, b):
    return lax.dot_general(a, b, (((1,), (1,)), ((), ())), preferred_element_type=F32)


def _dot_tn(a, b):
    return lax.dot_general(a, b, (((0,), (0,)), ((), ())), preferred_element_type=F32)


def _mm(a, b, mode, out_dtype, tm, tn, tk, name, res=None, after=None):
    if mode == "nn":
        (m, k), (_, n) = a.shape, b.shape
    elif mode == "nt":
        (m, k), (n, _) = a.shape, b.shape
    else:
        (k, m), (_, n) = a.shape, b.shape
    tm, tn, tk = min(tm, m), min(tn, n), min(tk, k)
    assert m % tm == 0 and n % tn == 0 and k % tk == 0, (name, m, n, k, tm, tn, tk)
    nk = k // tk
    a_spec = {"nn": pl.BlockSpec((tm, tk), lambda i, j, kk: (i, kk)),
              "nt": pl.BlockSpec((tm, tk), lambda i, j, kk: (i, kk)),
              "tn": pl.BlockSpec((tk, tm), lambda i, j, kk: (kk, i))}[mode]
    b_spec = {"nn": pl.BlockSpec((tk, tn), lambda i, j, kk: (kk, j)),
              "nt": pl.BlockSpec((tn, tk), lambda i, j, kk: (j, kk)),
              "tn": pl.BlockSpec((tk, tn), lambda i, j, kk: (kk, j))}[mode]
    dot = {"nn": _dot, "nt": _dot_nt, "tn": _dot_tn}[mode]
    has_res = res is not None
    direct = out_dtype == F32 and not has_res

    def body(*refs):
        ins, outs = refs[:2 + has_res + (after is not None)], refs[2 + has_res + (after is not None):]
        a_ref, b_ref = ins[:2]
        r_ref = ins[2] if has_res else None
        o_ref = outs[0]
        acc = o_ref if direct else outs[1]
        kk = pl.program_id(2)

        @pl.when(kk == 0)
        def _():
            acc[...] = jnp.zeros_like(acc)

        acc[...] += dot(a_ref[...].astype(BF16), b_ref[...].astype(BF16))

        if not direct:
            @pl.when(kk == nk - 1)
            def _():
                r = acc[...]
                if has_res:
                    r = r + r_ref[...]
                o_ref[...] = r.astype(out_dtype)

    in_specs = [a_spec, b_spec]
    args = [a, b]
    if has_res:
        in_specs.append(pl.BlockSpec((tm, tn), lambda i, j, kk: (i, j)))
        args.append(res)
    if after is not None:
        in_specs.append(pl.BlockSpec(memory_space=pl.ANY))
        args.append(after)
    return pl.pallas_call(
        body, name=name,
        grid=(m // tm, n // tn, nk),
        in_specs=in_specs,
        out_specs=pl.BlockSpec((tm, tn), lambda i, j, kk: (i, j)),
        out_shape=jax.ShapeDtypeStruct((m, n), out_dtype),
        scratch_shapes=[] if direct else [pltpu.VMEM((tm, tn), F32)],
        compiler_params=_cp(("parallel", "parallel", "arbitrary")),
    )(*args)


PROJ_TN = 256


def _in_proj_tiles(h, win_t, tiles, carry, name, after=None):
    L, K = h.shape
    extra = [a for a in (carry, after) if a is not None]

    def body(t_ref, h_ref, w_ref, *rest):
        rest[len(extra)][...] = _dot_nt(h_ref[...], w_ref[...]).astype(BF16)

    return pl.pallas_call(
        body, name=name,
        grid_spec=pltpu.PrefetchScalarGridSpec(
            num_scalar_prefetch=1, grid=(tiles.shape[0],),
            in_specs=[pl.BlockSpec((L, K), lambda j, t: (0, 0)), pl.BlockSpec((PROJ_TN, K), lambda j, t: (t[j], 0))]
            + [pl.BlockSpec(memory_space=pl.ANY)] * len(extra),
            out_specs=pl.BlockSpec((L, PROJ_TN), lambda j, t: (0, t[j]))),
        out_shape=jax.ShapeDtypeStruct((L, win_t.shape[0]), BF16),
        input_output_aliases={} if carry is None else {3: 0},
        compiler_params=_cp(("arbitrary",)),
    )(tiles, h, win_t, *extra)


def _rms(x, w):
    return x * lax.rsqrt(jnp.mean(x * x, axis=-1, keepdims=True) + EPS) * w


def _rms_fwd(x, w, name):
    L, D = x.shape
    tm = min(L, 256)

    def body(x_ref, w_ref, h_ref):
        h_ref[...] = _rms(x_ref[...], w_ref[...]).astype(BF16)

    return pl.pallas_call(
        body, name=name, grid=(L // tm,),
        in_specs=[pl.BlockSpec((tm, D), lambda i: (i, 0)), pl.BlockSpec((1, D), lambda i: (0, 0))],
        out_specs=pl.BlockSpec((tm, D), lambda i: (i, 0)),
        out_shape=jax.ShapeDtypeStruct((L, D), BF16),
        compiler_params=_cp(("parallel",)),
    )(x, w)


def _rms_bwd(x, w, dh, dres, name):
    L, D = x.shape
    tm = min(L, 256)

    def body(x_ref, w_ref, dh_ref, dres_ref, dx_ref, dw_ref):
        _, vjp = jax.vjp(_rms, x_ref[...], w_ref[...])
        dx, dw = vjp(dh_ref[...])
        dx_ref[...] = dx + dres_ref[...]

        @pl.when(pl.program_id(0) == 0)
        def _():
            dw_ref[...] = jnp.zeros_like(dw_ref)

        dw_ref[...] += dw

    row = pl.BlockSpec((tm, D), lambda i: (i, 0))
    vec = pl.BlockSpec((1, D), lambda i: (0, 0))
    return pl.pallas_call(
        body, name=name, grid=(L // tm,),
        in_specs=[row, vec, row, row],
        out_specs=[row, vec],
        out_shape=[jax.ShapeDtypeStruct((L, D), F32), jax.ShapeDtypeStruct((1, D), F32)],
        compiler_params=_cp(("arbitrary",)),
    )(x, w, dh, dres)


def _final(x, fw, tgt, name):
    L, D = x.shape
    tm = min(L, 256)

    def loss_fn(xv, wv, tv):
        err = _rms(xv, wv) - tv
        return jnp.sum(err * err) * (0.5 / D)

    def body(x_ref, w_ref, t_ref, loss_ref, dx_ref, dw_ref):
        tv = t_ref[...]
        val, vjp = jax.vjp(lambda a, b: loss_fn(a, b, tv), x_ref[...], w_ref[...])
        dx, dw = vjp(jnp.ones((), F32))
        dx_ref[...] = dx

        @pl.when(pl.program_id(0) == 0)
        def _():
            dw_ref[...] = jnp.zeros_like(dw_ref)
            loss_ref[...] = jnp.zeros_like(loss_ref)

        dw_ref[...] += dw
        loss_ref[...] += jnp.full(loss_ref.shape, val, F32)

    row = pl.BlockSpec((tm, D), lambda i: (i, 0))
    vec = pl.BlockSpec((1, D), lambda i: (0, 0))
    return pl.pallas_call(
        body, name=name, grid=(L // tm,),
        in_specs=[row, vec, row],
        out_specs=[pl.BlockSpec((8, 128), lambda i: (0, 0)), row, vec],
        out_shape=[jax.ShapeDtypeStruct((8, 128), F32), jax.ShapeDtypeStruct((L, D), F32),
                   jax.ShapeDtypeStruct((1, D), F32)],
        compiler_params=_cp(("arbitrary",)),
    )(x, fw, tgt)


def _s5_param_fn(a_re, a_im, log_dt, bt_re, bt_im):
    dt = jnp.exp(log_dt)
    zr, zi = a_re * dt, a_im * dt
    er = jnp.exp(zr)
    lr, li = er * jnp.cos(zi), er * jnp.sin(zi)
    nr, ni = lr - 1.0, li
    den = a_re * a_re + a_im * a_im
    cr = (nr * a_re + ni * a_im) / den
    ci = (ni * a_re - nr * a_im) / den
    bbr = cr[None] * bt_re - ci[None] * bt_im
    bbi = cr[None] * bt_im + ci[None] * bt_re
    return lr, li, bbr, bbi


def _s5_params_fwd(a_re, a_im, log_dt, bt_re, bt_im, name):
    def body(ar, ai, ld, br, bi, lr, li, bbr, bbi):
        o = _s5_param_fn(ar[...], ai[...], ld[...], br[...], bi[...])
        lr[...], li[...], bbr[...], bbi[...] = o

    gp = jax.ShapeDtypeStruct(a_re.shape, F32)
    cgp = jax.ShapeDtypeStruct(bt_re.shape, F32)
    return pl.pallas_call(body, name=name, out_shape=[gp, gp, cgp, cgp])(a_re, a_im, log_dt, bt_re, bt_im)


def _s5_params_bwd(a_re, a_im, log_dt, bt_re, bt_im, dlr, dli, dbbr, dbbi, name):
    def body(ar, ai, ld, br, bi, g0, g1, g2, g3, o0, o1, o2, o3, o4):
        _, vjp = jax.vjp(_s5_param_fn, ar[...], ai[...], ld[...], br[...], bi[...])
        o0[...], o1[...], o2[...], o3[...], o4[...] = vjp((g0[...], g1[...], g2[...], g3[...]))

    gp = jax.ShapeDtypeStruct(a_re.shape, F32)
    cgp = jax.ShapeDtypeStruct(bt_re.shape, F32)
    return pl.pallas_call(body, name=name,
                          out_shape=[gp, gp, jax.ShapeDtypeStruct(log_dt.shape, F32), cgp, cgp])(
        a_re, a_im, log_dt, bt_re, bt_im, dlr, dli, dbbr, dbbi)


def _cmul(ar, ai, br, bi):
    return ar * br - ai * bi, ar * bi + ai * br


def _cpow(lr, li, n):
    rr, ri = None, None
    br, bi = lr, li
    while n:
        if n & 1:
            rr, ri = (br, bi) if rr is None else _cmul(rr, ri, br, bi)
        n >>= 1
        if n:
            br, bi = _cmul(br, bi, br, bi)
    return rr, ri


def _shift_rows(x, up):
    row = lax.broadcasted_iota(jnp.int32, x.shape, 0)
    if up:
        return jnp.where(row == SUB - 1, 0.0, pltpu.roll(x, SUB - 1, 0))
    return jnp.where(row == 0, 0.0, pltpu.roll(x, 1, 0))


NT = SLAB_ST // 128


def _lam_tiles(lr_ref, li_ref):
    return [(lr_ref[:, j * 128:(j + 1) * 128], li_ref[:, j * 128:(j + 1) * 128]) for j in range(NT)]


def _row_on_sublanes(ref, j, t):
    return ref[j, pl.ds(t, SUB, stride=0), :]


def _pow_table(pw_re, pw_im, lam_t, seg):
    assert seg % 8 == 0 and (seg // 8) & (seg // 8 - 1) == 0
    for j in range(NT):
        lr, li = lam_t[j][0][0:1], lam_t[j][1][0:1]
        r, i_ = lr, li
        for row in range(8):
            pw_re[j, row:row + 1, :] = r
            pw_im[j, row:row + 1, :] = i_
            if row < 7:
                r, i_ = _cmul(r, i_, lr, li)
        n = 8
        while n < seg:
            qr, qi = _cpow(lr, li, n)
            nr, ni = _cmul(pw_re[j, 0:n, :], pw_im[j, 0:n, :], qr, qi)
            pw_re[j, n:2 * n, :] = nr
            pw_im[j, n:2 * n, :] = ni
            n *= 2


def _seg_scan(s_re, s_im, lam_t, pw_re, pw_im, seg, reverse, prev=None):
    sgn = -1.0 if reverse else 1.0
    lt = [(lr, sgn * li) for lr, li in lam_t]
    tiles = [(g, j) for g in range(N_GRP) for j in range(NT)]
    zeros = jnp.zeros((SUB, 128), F32)

    def rows(g, i):
        return pl.ds(pl.multiple_of((g * seg + i) * SUB, SUB), SUB)

    def step1(t, carry):
        i = seg - 1 - t if reverse else t
        out = []
        for n, (g, j) in enumerate(tiles):
            nr, ni = _cmul(lt[j][0], lt[j][1], carry[2 * n], carry[2 * n + 1])
            nr = nr + s_re[j, rows(g, i), :]
            ni = ni + s_im[j, rows(g, i), :]
            s_re[j, rows(g, i), :] = nr
            s_im[j, rows(g, i), :] = ni
            out += [nr, ni]
        return tuple(out)

    zero = tuple(zeros for _ in range(2 * len(tiles)))
    ends = lax.fori_loop(0, seg, step1, zero)

    carries = [None] * (2 * len(tiles))
    row = lax.broadcasted_iota(jnp.int32, (SUB, 128), 0)
    dist = (SUB - 1 - row) if reverse else row
    edge = 0 if reverse else SUB - 1
    for j in range(NT):
        pr, pi = _cpow(lt[j][0], lt[j][1], seg)
        qr, qi = jnp.ones((SUB, 128), F32), zeros
        for s in range(1, SUB):
            tr, ti = _cmul(qr, qi, pr, pi)
            qr, qi = jnp.where(dist >= s, tr, qr), jnp.where(dist >= s, ti, qi)
        boundary = None
        for g in (reversed(range(N_GRP)) if reverse else range(N_GRP)):
            n = g * NT + j
            cr, ci = zeros, zeros
            for _ in range(SUB - 1):
                tr, ti = _cmul(pr, pi, cr, ci)
                cr = _shift_rows(tr + ends[2 * n], reverse)
                ci = _shift_rows(ti + ends[2 * n + 1], reverse)
            if boundary is not None:
                tr, ti = _cmul(qr, qi, boundary[0], boundary[1])
                cr, ci = cr + tr, ci + ti
            carries[2 * n], carries[2 * n + 1] = cr, ci
            fr, fi = _cmul(pr, pi, cr, ci)
            boundary = (jnp.broadcast_to((fr + ends[2 * n])[edge:edge + 1], (SUB, 128)),
                        jnp.broadcast_to((fi + ends[2 * n + 1])[edge:edge + 1], (SUB, 128)))

    def fix(t, i, acc, before):
        out = []
        pws = [(_row_on_sublanes(pw_re, j, t), sgn * _row_on_sublanes(pw_im, j, t)) for j in range(NT)]
        for n, (g, j) in enumerate(tiles):
            ar, ai = _cmul(pws[j][0], pws[j][1], carries[2 * n], carries[2 * n + 1])
            ar = ar + s_re[j, rows(g, i), :]
            ai = ai + s_im[j, rows(g, i), :]
            s_re[j, rows(g, i), :] = ar
            s_im[j, rows(g, i), :] = ai
            if before is not None:
                qr, qi = before(n)
                out += [acc[2 * n] + ar * qr + ai * qi, acc[2 * n + 1] + ai * qr - ar * qi]
        return tuple(out)

    if prev is None:
        lax.fori_loop(0, seg, lambda t, c: fix(t, seg - 1 - t if reverse else t, c, None), ())
        return carries
    assert reverse
    p_re, p_im, p_carries = prev

    def earlier(t):
        return lambda n: (p_re[tiles[n][1], rows(tiles[n][0], seg - 2 - t), :],
                          p_im[tiles[n][1], rows(tiles[n][0], seg - 2 - t), :])

    acc = lax.fori_loop(0, seg - 1, lambda t, c: fix(t, seg - 1 - t, c, earlier(t)), zero)
    acc = fix(seg - 1, 0, acc, lambda n: (p_carries[2 * n], p_carries[2 * n + 1]))
    return carries, [sum(acc[2 * (g * NT + j) + part] for g in range(N_GRP)) for j in range(NT) for part in range(2)]


S5_RB = 256


def _seg_slice(k, seg):
    g, r = divmod(k, SUB)
    return pl.ds(g * seg * SUB + r, seg, stride=SUB)


def _to_step_major(src_ref, dst_ref, seg):
    for k in range(N_SEG):
        dst_ref[_seg_slice(k, seg), :] = src_ref[pl.ds(k * seg, seg), :].astype(F32)


def _from_step_major(src_ref, dst_ref, seg):
    for k in range(N_SEG):
        dst_ref[pl.ds(k * seg, seg), :] = src_ref[_seg_slice(k, seg), :].astype(dst_ref.dtype)


def _blocks(L):
    rb = min(S5_RB, L)
    return [pl.ds(b * rb, rb) for b in range(L // rb)]


def _lanes_of(ref, rows):
    return jnp.concatenate([ref[j, rows, :] for j in range(NT)], axis=-1)


def _lanes_to(ref, rows, val):
    for j in range(NT):
        ref[j, rows, :] = val[:, j * 128:(j + 1) * 128]


def _s5_specs(L):
    col = lambda off: pl.BlockSpec((L, SLAB_CH), lambda j: (0, off + j))
    mat_b = pl.BlockSpec((None, SLAB_CH, SLAB_ST), lambda j: (j, 0, 0))
    mat_c = pl.BlockSpec((None, SLAB_ST, SLAB_CH), lambda j: (j, 0, 0))
    vec_s = pl.BlockSpec((None, SUB, SLAB_ST), lambda j: (j, 0, 0))
    vec_c = pl.BlockSpec((None, 1, SLAB_CH), lambda j: (j, 0, 0))
    return col, mat_b, mat_c, vec_s, vec_c


def _s5_states(u_ref, u_sm, bre_ref, bim_ref, lam_t, pw_re, pw_im, s_re, s_im, seg):
    _pow_table(pw_re, pw_im, lam_t, seg)
    _to_step_major(u_ref, u_sm, seg)
    for rows in _blocks(u_sm.shape[0]):
        ub = u_sm[rows, :].astype(BF16)
        _lanes_to(s_re, rows, _dot(ub, bre_ref[...]))
        _lanes_to(s_im, rows, _dot(ub, bim_ref[...]))
    return _seg_scan(s_re, s_im, lam_t, pw_re, pw_im, seg, reverse=False)


def _s5_fwd(proj, bre, bim, cre_t, cim_t, lam_re, lam_im, dvec, name):
    L = proj.shape[0]
    seg = L // N_SEG
    col, mat_b, mat_c, vec_s, vec_c = _s5_specs(L)

    def body(u_ref, bre_ref, bim_ref, cre_ref, cim_ref, lr_ref, li_ref, d_ref, y_ref, s_re, s_im, pw_re, pw_im, u_sm, y_sm):
        _s5_states(u_ref, u_sm, bre_ref, bim_ref, _lam_tiles(lr_ref, li_ref), pw_re, pw_im, s_re, s_im, seg)
        for rows in _blocks(L):
            y = (_dot(_lanes_of(s_re, rows).astype(BF16), cre_ref[...])
                 - _dot(_lanes_of(s_im, rows).astype(BF16), cim_ref[...]))
            y_sm[rows, :] = jax.nn.gelu(y + d_ref[...] * u_sm[rows, :])
        _from_step_major(y_sm, y_ref, seg)

    lane_tile = pltpu.VMEM((L, SLAB_CH), F32)
    return pl.pallas_call(
        body, name=name, grid=(N_SLAB,),
        in_specs=[col(OFF_UA // SLAB_CH), mat_b, mat_b, mat_c, mat_c, vec_s, vec_s, vec_c],
        out_specs=pl.BlockSpec((L, SLAB_CH), lambda j: (0, j)),
        out_shape=jax.ShapeDtypeStruct((L, SSM_WIDTH), BF16),
        scratch_shapes=[pltpu.VMEM((NT, L, 128), F32)] * 2 + [pltpu.VMEM((NT, seg, 128), F32)] * 2 + [lane_tile] * 2,
        compiler_params=_cp(("parallel",)),
    )(proj, bre, bim, cre_t, cim_t, lam_re, lam_im, dvec)


def _s5_bwd(proj, dy, bre, bim, cre_t, cim_t, lam_re, lam_im, dvec, name):
    L = proj.shape[0]
    seg = L // N_SEG
    col, mat_b, mat_c, vec_s, vec_c = _s5_specs(L)
    dlam_spec = pl.BlockSpec((None, 1, SLAB_ST), lambda j: (j, 0, 0))

    def body(u_ref, dy_ref, bre_ref, bim_ref, cre_ref, cim_ref, lr_ref, li_ref, d_ref,
             du_ref, dbre_ref, dbim_ref, dcre_ref, dcim_ref, dlr_ref, dli_ref, dd_ref,
             s_re, s_im, a_re, a_im, pw_re, pw_im, u_sm, dyp, io_sm):
        lam_t = _lam_tiles(lr_ref, li_ref)
        carry_s = _s5_states(u_ref, u_sm, bre_ref, bim_ref, lam_t, pw_re, pw_im, s_re, s_im, seg)
        _to_step_major(dy_ref, io_sm, seg)
        dcre = jnp.zeros((SLAB_ST, SLAB_CH), F32)
        dcim = jnp.zeros((SLAB_ST, SLAB_CH), F32)
        dd = jnp.zeros((1, SLAB_CH), F32)
        for rows in _blocks(L):
            sre = _lanes_of(s_re, rows).astype(BF16)
            sim = _lanes_of(s_im, rows).astype(BF16)
            uk = u_sm[rows, :]
            ypre = _dot(sre, cre_ref[...]) - _dot(sim, cim_ref[...]) + d_ref[...] * uk
            _, vjp = jax.vjp(jax.nn.gelu, ypre)
            (dyk,) = vjp(io_sm[rows, :])
            dyp[rows, :] = dyk
            dd = dd + jnp.sum(dyk * uk, axis=0, keepdims=True)
            dyb = dyk.astype(BF16)
            dcre = dcre + _dot_tn(sre, dyb)
            dcim = dcim - _dot_tn(sim, dyb)
            _lanes_to(a_re, rows, _dot_nt(dyb, cre_ref[...]))
            _lanes_to(a_im, rows, -_dot_nt(dyb, cim_ref[...]))
        dcre_ref[...] = dcre
        dcim_ref[...] = dcim
        dd_ref[...] = dd

        _, acc = _seg_scan(a_re, a_im, lam_t, pw_re, pw_im, seg, reverse=True, prev=(s_re, s_im, carry_s))
        dlr_ref[...] = jnp.concatenate([jnp.sum(acc[2 * j], axis=0, keepdims=True) for j in range(NT)], axis=-1)
        dli_ref[...] = jnp.concatenate([jnp.sum(acc[2 * j + 1], axis=0, keepdims=True) for j in range(NT)], axis=-1)

        dbre = jnp.zeros((SLAB_CH, SLAB_ST), F32)
        dbim = jnp.zeros((SLAB_CH, SLAB_ST), F32)
        for rows in _blocks(L):
            are = _lanes_of(a_re, rows).astype(BF16)
            aim = _lanes_of(a_im, rows).astype(BF16)
            ub = u_sm[rows, :].astype(BF16)
            io_sm[rows, :] = _dot_nt(are, bre_ref[...]) + _dot_nt(aim, bim_ref[...]) + dyp[rows, :] * d_ref[...]
            dbre = dbre + _dot_tn(ub, are)
            dbim = dbim + _dot_tn(ub, aim)
        _from_step_major(io_sm, du_ref, seg)
        dbre_ref[...] = dbre
        dbim_ref[...] = dbim

    scan_buf = pltpu.VMEM((NT, L, 128), F32)
    pow_buf = pltpu.VMEM((NT, seg, 128), F32)
    lane_tile = pltpu.VMEM((L, SLAB_CH), F32)
    return pl.pallas_call(
        body, name=name, grid=(N_SLAB,),
        in_specs=[col(OFF_UA // SLAB_CH), pl.BlockSpec((L, SLAB_CH), lambda j: (0, j)),
                  mat_b, mat_b, mat_c, mat_c, vec_s, vec_s, vec_c],
        out_specs=[pl.BlockSpec((L, SLAB_CH), lambda j: (0, j)), mat_b, mat_b, mat_c, mat_c, dlam_spec, dlam_spec, vec_c],
        out_shape=[jax.ShapeDtypeStruct((L, SSM_WIDTH), BF16),
                   jax.ShapeDtypeStruct((N_SLAB, SLAB_CH, SLAB_ST), F32),
                   jax.ShapeDtypeStruct((N_SLAB, SLAB_CH, SLAB_ST), F32),
                   jax.ShapeDtypeStruct((N_SLAB, SLAB_ST, SLAB_CH), F32),
                   jax.ShapeDtypeStruct((N_SLAB, SLAB_ST, SLAB_CH), F32),
                   jax.ShapeDtypeStruct((N_SLAB, 1, SLAB_ST), F32),
                   jax.ShapeDtypeStruct((N_SLAB, 1, SLAB_ST), F32),
                   jax.ShapeDtypeStruct((N_SLAB, 1, SLAB_CH), F32)],
        scratch_shapes=[scan_buf, scan_buf, scan_buf, scan_buf, pow_buf, pow_buf, lane_tile, lane_tile, lane_tile],
        compiler_params=_cp(("parallel",)),
    )(proj, dy, bre, bim, cre_t, cim_t, lam_re, lam_im, dvec)


def _glu_point(y0, pre, za, b):
    return y0 * jax.nn.sigmoid(pre + b) * jax.nn.silu(za)


def _glu_specs(L, tm):
    row = pl.BlockSpec((tm, SSM_WIDTH), lambda i: (i, 0))
    za = pl.BlockSpec((tm, SSM_WIDTH), lambda i: (i, OFF_ZA // SSM_WIDTH))
    wmat = pl.BlockSpec((SSM_WIDTH, SSM_WIDTH), lambda i: (0, 0))
    vec = pl.BlockSpec((1, SSM_WIDTH), lambda i: (0, 0))
    return row, za, wmat, vec


def _glu_fwd(ya0, proj, w, b, name):
    L = ya0.shape[0]
    tm = min(L, 512)
    row, za, wmat, vec = _glu_specs(L, tm)

    def body(y_ref, z_ref, w_ref, b_ref, o_ref):
        y0 = y_ref[...]
        pre = _dot(y0, w_ref[...])
        o_ref[...] = _glu_point(y0.astype(F32), pre, z_ref[...].astype(F32), b_ref[...]).astype(BF16)

    return pl.pallas_call(
        body, name=name, grid=(L // tm,), in_specs=[row, za, wmat, vec], out_specs=row,
        out_shape=jax.ShapeDtypeStruct((L, SSM_WIDTH), BF16), compiler_params=_cp(("parallel",)),
    )(ya0, proj, w, b)


def _glu_bwd(ya0, proj, w, b, dya, name):
    L = ya0.shape[0]
    tm = min(L, 512)
    row, za, wmat, vec = _glu_specs(L, tm)

    def body(y_ref, z_ref, w_ref, b_ref, g_ref, dy0_ref, dza_ref, dw_ref, db_ref):
        y0 = y_ref[...]
        pre = _dot(y0, w_ref[...])
        _, vjp = jax.vjp(_glu_point, y0.astype(F32), pre, z_ref[...].astype(F32), b_ref[...])
        dy0, dpre, dza, db = vjp(g_ref[...].astype(F32))
        dpb = dpre.astype(BF16)
        dy0_ref[...] = (dy0 + _dot_nt(dpb, w_ref[...])).astype(BF16)
        dza_ref[...] = dza.astype(BF16)

        @pl.when(pl.program_id(0) == 0)
        def _():
            dw_ref[...] = jnp.zeros_like(dw_ref)
            db_ref[...] = jnp.zeros_like(db_ref)

        dw_ref[...] += _dot_tn(y0, dpb)
        db_ref[...] += db

    return pl.pallas_call(
        body, name=name, grid=(L // tm,), in_specs=[row, za, wmat, vec, row],
        out_specs=[row, row, wmat, vec],
        out_shape=[jax.ShapeDtypeStruct((L, SSM_WIDTH), BF16), jax.ShapeDtypeStruct((L, SSM_WIDTH), BF16),
                   jax.ShapeDtypeStruct((SSM_WIDTH, SSM_WIDTH), F32), jax.ShapeDtypeStruct((1, SSM_WIDTH), F32)],
        compiler_params=_cp(("arbitrary",)),
    )(ya0, proj, w, b, dya)


def _sg_norm(vb, ln_w, ln_b):
    v0 = jax.nn.gelu(vb)
    mu = jnp.mean(v0, axis=-1, keepdims=True)
    var = jnp.mean(jnp.square(v0 - mu), axis=-1, keepdims=True)
    return (v0 - mu) * lax.rsqrt(var + EPS) * ln_w + ln_b


def _sg_gate(ub, mixed, zb):
    return jax.nn.gelu(ub) * mixed * jax.nn.silu(zb)


def _sg_specs():
    W = SSM_WIDTH
    blk = lambda off: pl.BlockSpec((CHUNK, W), lambda n: (n, off // W))
    out = pl.BlockSpec((CHUNK, W), lambda n: (n, 0))
    vec = pl.BlockSpec((1, W), lambda n: (0, 0))
    wsp = pl.BlockSpec((SG_HEADS, CHUNK, CHUNK), lambda n: (0, 0, 0))
    bsp = pl.BlockSpec((SG_HEADS, CHUNK, 1), lambda n: (0, 0, 0))
    return blk, out, vec, wsp, bsp


def _sg_masked(w_ref):
    t = lax.broadcasted_iota(jnp.int32, (CHUNK, CHUNK), 0)
    s = lax.broadcasted_iota(jnp.int32, (CHUNK, CHUNK), 1)
    causal = s <= t
    return causal, [jnp.where(causal, w_ref[h], 0.0).astype(BF16) for h in range(SG_HEADS)]


def _sg_mix(wm, vnb, bias_ref):
    return jnp.concatenate(
        [_dot(wm[h], vnb[:, h * CHUNK:(h + 1) * CHUNK]) + bias_ref[h] for h in range(SG_HEADS)], axis=-1)


def _sg_fwd(proj, ln_w, ln_b, w, bias, name):
    L = proj.shape[0]
    blk, out, vec, wsp, bsp = _sg_specs()

    def body(ub_ref, vb_ref, zb_ref, lw_ref, lb_ref, w_ref, bias_ref, o_ref):
        _, wm = _sg_masked(w_ref)
        vnb = _sg_norm(vb_ref[...].astype(F32), lw_ref[...], lb_ref[...]).astype(BF16)
        mixed = _sg_mix(wm, vnb, bias_ref)
        o_ref[...] = _sg_gate(ub_ref[...].astype(F32), mixed, zb_ref[...].astype(F32)).astype(BF16)

    return pl.pallas_call(
        body, name=name, grid=(L // CHUNK,),
        in_specs=[blk(OFF_UB), blk(OFF_VB), blk(OFF_ZB), vec, vec, wsp, bsp], out_specs=out,
        out_shape=jax.ShapeDtypeStruct((L, SSM_WIDTH), BF16), compiler_params=_cp(("parallel",)),
    )(proj, proj, proj, ln_w, ln_b, w, bias)


def _sg_bwd(proj, ln_w, ln_b, w, bias, dyb, name):
    L = proj.shape[0]
    blk, out, vec, wsp, bsp = _sg_specs()

    def body(ub_ref, vb_ref, zb_ref, lw_ref, lb_ref, w_ref, bias_ref, g_ref,
             dub_ref, dvb_ref, dzb_ref, dlw_ref, dlb_ref, dw_ref, dbias_ref):
        causal, wm = _sg_masked(w_ref)
        vb = vb_ref[...].astype(F32)
        vn, vjp_norm = jax.vjp(_sg_norm, vb, lw_ref[...], lb_ref[...])
        vnb = vn.astype(BF16)
        mixed = _sg_mix(wm, vnb, bias_ref)
        _, vjp_gate = jax.vjp(_sg_gate, ub_ref[...].astype(F32), mixed, zb_ref[...].astype(F32))
        dub, dmixed, dzb = vjp_gate(g_ref[...].astype(F32))
        dub_ref[...] = dub.astype(BF16)
        dzb_ref[...] = dzb.astype(BF16)

        @pl.when(pl.program_id(0) == 0)
        def _():
            dlw_ref[...] = jnp.zeros_like(dlw_ref)
            dlb_ref[...] = jnp.zeros_like(dlb_ref)
            dw_ref[...] = jnp.zeros_like(dw_ref)
            dbias_ref[...] = jnp.zeros_like(dbias_ref)

        dvn = []
        for h in range(SG_HEADS):
            dm = dmixed[:, h * CHUNK:(h + 1) * CHUNK]
            dmb = dm.astype(BF16)
            dbias_ref[h] += jnp.sum(dm, axis=-1, keepdims=True)
            dw_ref[h] += jnp.where(causal, _dot_nt(dmb, vnb[:, h * CHUNK:(h + 1) * CHUNK]), 0.0)
            dvn.append(_dot_tn(wm[h], dmb))
        dvb, dlw, dlb = vjp_norm(jnp.concatenate(dvn, axis=-1))
        dvb_ref[...] = dvb.astype(BF16)
        dlw_ref[...] += dlw
        dlb_ref[...] += dlb

    act = jax.ShapeDtypeStruct((L, SSM_WIDTH), BF16)
    return pl.pallas_call(
        body, name=name, grid=(L // CHUNK,),
        in_specs=[blk(OFF_UB), blk(OFF_VB), blk(OFF_ZB), vec, vec, wsp, bsp, out],
        out_specs=[out, out, out, vec, vec, wsp, bsp],
        out_shape=[act, act, act, jax.ShapeDtypeStruct((1, SSM_WIDTH), F32), jax.ShapeDtypeStruct((1, SSM_WIDTH), F32),
                   jax.ShapeDtypeStruct((SG_HEADS, CHUNK, CHUNK), F32), jax.ShapeDtypeStruct((SG_HEADS, CHUNK, 1), F32)],
        compiler_params=_cp(("arbitrary",)),
    )(proj, proj, proj, ln_w, ln_b, w, bias, dyb)


def _rope_tables(L):
    half = ROT_DIM // 2
    inv_freq = ROPE_THETA ** (-jnp.arange(0, ROT_DIM, 2, dtype=F32) / ROT_DIM)
    ang = jnp.arange(L, dtype=F32)[:, None] * inv_freq[None, :]
    cos, sin = jnp.cos(ang), jnp.sin(ang)
    ones = jnp.ones((L, HEAD_DIM - ROT_DIM), F32)
    cos_h = jnp.concatenate([cos, cos, ones], axis=-1)
    sin_h = jnp.concatenate([-sin, sin, 0.0 * ones], axis=-1)
    src = jnp.arange(HEAD_DIM)[:, None]
    dst = jnp.arange(HEAD_DIM)[None, :]
    p_h = (((dst < half) & (src == dst + half)) | ((dst >= half) & (dst < ROT_DIM) & (src == dst - half))).astype(F32)
    p2 = jnp.kron(jnp.eye(2, dtype=F32), p_h).astype(BF16)
    return jnp.tile(cos_h, (1, 2)), jnp.tile(sin_h, (1, 2)), p2


def _rope(t, cos, sin, p2):
    n = t.shape[1] // 128
    tb = t.astype(BF16)
    sw = jnp.concatenate([_dot(tb[:, i * 128:(i + 1) * 128], p2) for i in range(n)], axis=-1) if n > 1 else _dot(tb, p2)
    return t * jnp.tile(cos, (1, n)) + sw * jnp.tile(sin, (1, n))


def _rope_t(g, cos, sin, p2):
    n = g.shape[1] // 128
    gs = (g * jnp.tile(sin, (1, n))).astype(BF16)
    sw = jnp.concatenate([_dot_nt(gs[:, i * 128:(i + 1) * 128], p2) for i in range(n)], axis=-1) if n > 1 else _dot_nt(gs, p2)
    return g * jnp.tile(cos, (1, n)) + sw


def _lane_lo(shape):
    return (lax.broadcasted_iota(jnp.int32, shape, len(shape) - 1) % 128) < HEAD_DIM


def _dup_halves(x):
    xr = pltpu.roll(x, HEAD_DIM, 1)
    lo = _lane_lo(x.shape)
    return jnp.where(lo, x, xr), jnp.where(lo, xr, x)


def _fold_halves(d0, d1):
    f0 = d0 + pltpu.roll(d0, HEAD_DIM, 1)
    f1 = d1 + pltpu.roll(d1, HEAD_DIM, 1)
    return jnp.where(_lane_lo(d0.shape), f0, f1)


def _attn_mask():
    qi = lax.broadcasted_iota(jnp.int32, (CHUNK, 2 * CHUNK), 0)
    kj = lax.broadcasted_iota(jnp.int32, (CHUNK, 2 * CHUNK), 1)
    return qi, kj


def _attn_specs():
    qsp = pl.BlockSpec((CHUNK, 1024), lambda n: (n, OFF_Q // 1024))
    kv_cur = pl.BlockSpec((CHUNK, 256), lambda n: (n, OFF_KV // 256))
    kv_prev = pl.BlockSpec((CHUNK, 256), lambda n: (jnp.maximum(n - 1, 0), OFF_KV // 256))
    zsp = [pl.BlockSpec((CHUNK, 256), functools.partial(lambda n, q: (n, OFF_ZC // 256 + q), q=q)) for q in range(4)]
    tab_cur = pl.BlockSpec((CHUNK, 128), lambda n: (n, 0))
    tab_prev = pl.BlockSpec((CHUNK, 128), lambda n: (jnp.maximum(n - 1, 0), 0))
    p2sp = pl.BlockSpec((128, 128), lambda n: (0, 0))
    sink = pl.BlockSpec(memory_space=pltpu.SMEM)
    wide = pl.BlockSpec((CHUNK, 1024), lambda n: (n, 0))
    return qsp, kv_cur, kv_prev, zsp, tab_cur, tab_prev, p2sp, sink, wide


def _attn_prep(n, q_ref, kvc_ref, kvp_ref, cosc_ref, sinc_ref, cosp_ref, sinp_ref, p2_ref):
    p2 = p2_ref[...]
    qr = _rope(q_ref[...].astype(F32), cosc_ref[...], sinc_ref[...], p2).astype(BF16)
    kc = _rope(kvc_ref[:, 0:128].astype(F32), cosc_ref[...], sinc_ref[...], p2)
    kp = _rope(kvp_ref[:, 0:128].astype(F32), cosp_ref[...], sinp_ref[...], p2)
    k_all = jnp.concatenate([kp, kc], axis=0).astype(BF16)
    v_all = jnp.concatenate([kvp_ref[:, 128:256], kvc_ref[:, 128:256]], axis=0)
    qi, kj = _attn_mask()
    allowed = ((kj < CHUNK) & (kj > qi) & (n > 0)) | ((kj >= CHUNK) & (kj - CHUNK <= qi))
    return qr, _dup_halves(k_all), _dup_halves(v_all), allowed, _lane_lo((CHUNK, 128))


def _attn_head(qr, kd, sink_ref, h, allowed, lo):
    m, half, g = h // 2, h % 2, h // 8
    qp = qr[:, m * 128:(m + 1) * 128]
    qm = jnp.where(lo if half == 0 else ~lo, qp, jnp.zeros_like(qp))
    s = jnp.where(allowed, _dot_nt(qm, kd[g]) * (HEAD_DIM ** -0.5), NEG_INF)
    snk = sink_ref[h]
    mx = jnp.maximum(jnp.max(s, axis=-1, keepdims=True), snk)
    e = jnp.exp(s - mx)
    es = jnp.exp(snk - mx)
    inv = 1.0 / (jnp.sum(e, axis=-1, keepdims=True) + es)
    return qm, e * inv, es * inv


def _silu_gate(o, z):
    return o * jax.nn.silu(z)


def _pair_lanes(refs, m):
    return refs[m // 2][:, (m % 2) * 128:(m % 2 + 1) * 128]


def _attn_fwd(proj, sinks, tabs, name):
    L = proj.shape[0]
    cos2, sin2, p2 = tabs
    qsp, kv_cur, kv_prev, zsp, tab_cur, tab_prev, p2sp, sink, wide = _attn_specs()

    def body(q_ref, kvc_ref, kvp_ref, z0, z1, z2, z3, cosc, sinc, cosp, sinp, p2_ref, sink_ref, y_ref, o_ref):
        n = pl.program_id(0)
        qr, kd, vd, allowed, lo = _attn_prep(n, q_ref, kvc_ref, kvp_ref, cosc, sinc, cosp, sinp, p2_ref)
        probs = [_attn_head(qr, kd, sink_ref, h, allowed, lo)[1].astype(BF16) for h in range(ATT_HEADS)]
        for m in range(ATT_HEADS // 2):
            g = m // 4
            o0 = _dot(probs[2 * m], vd[g])
            o1 = _dot(probs[2 * m + 1], vd[g])
            o = jnp.where(lo, o0, o1).astype(BF16)
            o_ref[:, m * 128:(m + 1) * 128] = o
            z = _pair_lanes((z0, z1, z2, z3), m).astype(F32)
            y_ref[:, m * 128:(m + 1) * 128] = _silu_gate(o.astype(F32), z).astype(BF16)

    act = jax.ShapeDtypeStruct((L, 1024), BF16)
    return pl.pallas_call(
        body, name=name, grid=(L // CHUNK,),
        in_specs=[qsp, kv_cur, kv_prev, *zsp, tab_cur, tab_cur, tab_prev, tab_prev, p2sp, sink],
        out_specs=[wide, wide], out_shape=[act, act], compiler_params=_cp(("parallel",)),
    )(proj, proj, proj, proj, proj, proj, proj, cos2, sin2, cos2, sin2, p2, sinks)


def _attn_bwd(proj, sinks, tabs, o_att, dyc, name):
    L = proj.shape[0]
    cos2, sin2, p2 = tabs
    qsp, kv_cur, kv_prev, zsp, tab_cur, tab_prev, p2sp, sink, wide = _attn_specs()
    kvo = pl.BlockSpec((CHUNK, 256), lambda n: (n, 0))

    def body(q_ref, kvc_ref, kvp_ref, z0, z1, z2, z3, cosc, sinc, cosp, sinp, p2_ref, sink_ref, o_ref, g_ref,
             dq_ref, dz_ref, dkvc_ref, dkvp_ref, dsink_ref):
        n = pl.program_id(0)
        qr, kd, vd, allowed, lo = _attn_prep(n, q_ref, kvc_ref, kvp_ref, cosc, sinc, cosp, sinp, p2_ref)
        p2 = p2_ref[...]

        @pl.when(n == 0)
        def _():
            dsink_ref[...] = jnp.zeros_like(dsink_ref)

        dkd = [jnp.zeros((2 * CHUNK, 128), F32), jnp.zeros((2 * CHUNK, 128), F32)]
        dvd = [jnp.zeros((2 * CHUNK, 128), F32), jnp.zeros((2 * CHUNK, 128), F32)]
        probs = [_attn_head(qr, kd, sink_ref, h, allowed, lo) for h in range(ATT_HEADS)]
        for m in range(ATT_HEADS // 2):
            g = m // 4
            lanes = slice(m * 128, (m + 1) * 128)
            z = _pair_lanes((z0, z1, z2, z3), m).astype(F32)
            _, vjp = jax.vjp(_silu_gate, o_ref[:, lanes].astype(F32), z)
            do, dz = vjp(g_ref[:, lanes].astype(F32))
            dz_ref[:, lanes] = dz.astype(BF16)
            dop = do.astype(BF16)
            dq_h = []
            for half in range(2):
                h = 2 * m + half
                qm, p, ps = probs[h]
                dom = jnp.where(lo if half == 0 else ~lo, dop, jnp.zeros_like(dop))
                dp = _dot_nt(dom, vd[g])
                rs = jnp.sum(p * dp, axis=-1, keepdims=True)
                ds = (p * (dp - rs) * (HEAD_DIM ** -0.5)).astype(BF16)
                dsink_ref[h:h + 1, :] += jnp.broadcast_to(jnp.sum(-ps * rs, axis=0, keepdims=True), (1, 128))
                dq_h.append(_dot(ds, kd[g]))
                dkd[g] = dkd[g] + _dot_tn(ds, qm)
                dvd[g] = dvd[g] + _dot_tn(p.astype(BF16), dom)
            dq_ref[:, lanes] = _rope_t(jnp.where(lo, dq_h[0], dq_h[1]), cosc[...], sinc[...], p2).astype(BF16)
        dk_rot = _fold_halves(dkd[0], dkd[1])
        dv = _fold_halves(dvd[0], dvd[1])
        dkp = _rope_t(dk_rot[0:CHUNK], cosp[...], sinp[...], p2)
        dkc = _rope_t(dk_rot[CHUNK:2 * CHUNK], cosc[...], sinc[...], p2)
        dkvp_ref[...] = jnp.concatenate([dkp, dv[0:CHUNK]], axis=-1)
        dkvc_ref[...] = jnp.concatenate([dkc, dv[CHUNK:2 * CHUNK]], axis=-1)

    act = jax.ShapeDtypeStruct((L, 1024), BF16)
    kvs = jax.ShapeDtypeStruct((L, 256), F32)
    return pl.pallas_call(
        body, name=name, grid=(L // CHUNK,),
        in_specs=[qsp, kv_cur, kv_prev, *zsp, tab_cur, tab_cur, tab_prev, tab_prev, p2sp, sink, wide, wide],
        out_specs=[wide, wide, kvo, kvo, pl.BlockSpec((ATT_HEADS, 128), lambda n: (0, 0))],
        out_shape=[act, act, kvs, kvs, jax.ShapeDtypeStruct((ATT_HEADS, 128), F32)],
        compiler_params=_cp(("arbitrary",)),
    )(proj, proj, proj, proj, proj, proj, proj, cos2, sin2, cos2, sin2, p2, sinks, o_att, dyc)


MERGE_TN = 256


def _merge_point(ta, tb, tc, ga, gb, gc):
    return jax.nn.sigmoid(ga) * ta + jax.nn.sigmoid(gb) * tb + jax.nn.sigmoid(gc) * tc


def _merge_specs(tm):
    nj = D_MODEL // MERGE_TN
    t = pl.BlockSpec((tm, MERGE_TN), lambda i, j: (i, j))
    gates = [pl.BlockSpec((tm, MERGE_TN), functools.partial(lambda i, j, b: (i, OFF_G // MERGE_TN + b * nj + j), b=b))
             for b in range(3)]
    return t, gates, nj


def _merge_fwd(ta, tb, tc, proj, name):
    L = ta.shape[0]
    tm = min(L, 1024)
    t, gates, nj = _merge_specs(tm)

    def body(ta_ref, tb_ref, tc_ref, ga_ref, gb_ref, gc_ref, o_ref):
        f = lambda r: r[...].astype(F32)
        o_ref[...] = _merge_point(f(ta_ref), f(tb_ref), f(tc_ref), f(ga_ref), f(gb_ref), f(gc_ref)).astype(BF16)

    return pl.pallas_call(
        body, name=name, grid=(L // tm, nj), in_specs=[t, t, t, *gates], out_specs=t,
        out_shape=jax.ShapeDtypeStruct((L, D_MODEL), BF16), compiler_params=_cp(("parallel", "parallel")),
    )(ta, tb, tc, proj, proj, proj)


def _merge_bwd(ta, tb, tc, proj, dm, name):
    L = ta.shape[0]
    tm = min(L, 1024)
    t, gates, nj = _merge_specs(tm)

    def body(ta_ref, tb_ref, tc_ref, ga_ref, gb_ref, gc_ref, dm_ref, dta_ref, dtb_ref, dtc_ref, dga_ref, dgb_ref, dgc_ref):
        f = lambda r: r[...].astype(F32)
        _, vjp = jax.vjp(_merge_point, f(ta_ref), f(tb_ref), f(tc_ref), f(ga_ref), f(gb_ref), f(gc_ref))
        outs = vjp(f(dm_ref))
        for r, v in zip((dta_ref, dtb_ref, dtc_ref, dga_ref, dgb_ref, dgc_ref), outs):
            r[...] = v.astype(BF16)

    act = jax.ShapeDtypeStruct((L, D_MODEL), BF16)
    return pl.pallas_call(
        body, name=name, grid=(L // tm, nj), in_specs=[t, t, t, *gates, t],
        out_specs=[t] * 6, out_shape=[act] * 6,
        compiler_params=_cp(("parallel", "parallel")),
    )(ta, tb, tc, proj, proj, proj, dm)


GRAD_DT = BF16
SMALL = ("norm_w", "ssm_a_re", "ssm_a_im", "ssm_log_dt", "ssm_b_re", "ssm_b_im", "ssm_c_re", "ssm_c_im", "ssm_d",
         "ssm_glu_b", "sg_ln_w", "sg_ln_b", "sg_w", "sg_b", "attn_sinks")
G8 = SSM_GROUPS // N_SLAB


def _diag_mask(rows_per_group, cols_per_group):
    r = jnp.arange(G8 * rows_per_group)[:, None] // rows_per_group
    c = jnp.arange(G8 * cols_per_group)[None, :] // cols_per_group
    return r == c


def _slab_b(bb_t):
    x = bb_t.transpose(1, 0, 2).reshape(N_SLAB, SLAB_CH, SSM_STATE)
    return jnp.where(_diag_mask(SSM_GROUP, SSM_STATE), jnp.tile(x, (1, 1, G8)), 0)


def _unslab_b(d):
    x = jnp.where(_diag_mask(SSM_GROUP, SSM_STATE), d, 0).reshape(N_SLAB, SLAB_CH, G8, SSM_STATE).sum(axis=2)
    return x.reshape(SSM_GROUPS, SSM_GROUP, SSM_STATE).transpose(1, 0, 2)


def _slab_c(c):
    x = c.transpose(0, 2, 1).reshape(N_SLAB, SLAB_ST, SSM_GROUP)
    return jnp.where(_diag_mask(SSM_STATE, SSM_GROUP), jnp.tile(x, (1, 1, G8)), 0)


def _unslab_c(d):
    x = jnp.where(_diag_mask(SSM_STATE, SSM_GROUP), d, 0).reshape(N_SLAB, SLAB_ST, G8, SSM_GROUP).sum(axis=2)
    return x.reshape(SSM_GROUPS, SSM_STATE, SSM_GROUP).transpose(0, 2, 1)


def _s5_prep(p, tag):
    bt_re = p["ssm_b_re"].transpose(2, 0, 1)
    bt_im = p["ssm_b_im"].transpose(2, 0, 1)
    raw = (p["ssm_a_re"], p["ssm_a_im"], p["ssm_log_dt"][:, None], bt_re, bt_im)
    lr, li, bbr, bbi = _s5_params_fwd(*raw, name=f"s5_params_{tag}")
    ops = (_slab_b(bbr).astype(BF16), _slab_b(bbi).astype(BF16),
           _slab_c(p["ssm_c_re"]).astype(BF16), _slab_c(p["ssm_c_im"]).astype(BF16),
           jnp.broadcast_to(lr.reshape(N_SLAB, 1, SLAB_ST), (N_SLAB, SUB, SLAB_ST)),
           jnp.broadcast_to(li.reshape(N_SLAB, 1, SLAB_ST), (N_SLAB, SUB, SLAB_ST)),
           p["ssm_d"].reshape(N_SLAB, 1, SLAB_CH))
    return raw, ops


def _layer_fwd(x, p, w, tabs, tag, s5=None, proj_of=None, after_proj=None):
    L = x.shape[0]
    h = _rms_fwd(x, p["norm_w"][None], f"rms_fwd_{tag}")
    if proj_of is not None:
        proj = proj_of(h)
    else:
        proj = _mm(h, w["win_t"], "nt", BF16, L, PROJ_TN, D_MODEL, f"in_proj_{tag}")
    if after_proj is not None:
        w = after_proj(proj)
    s5_raw, s5_ops = s5 if s5 is not None else _s5_prep(p, tag)
    ya0 = _s5_fwd(proj, *s5_ops, name=f"s5_fwd_{tag}")
    ya = _glu_fwd(ya0, proj, w["glu"], p["ssm_glu_b"][None], f"glu_fwd_{tag}")
    yb = _sg_fwd(proj, p["sg_ln_w"][None], p["sg_ln_b"][None], p["sg_w"], p["sg_b"][:, :, None], f"sg_fwd_{tag}")
    yc, o_att = _attn_fwd(proj, p["attn_sinks"], tabs, f"attn_fwd_{tag}")
    ta = _mm(ya, w["wba_t"], "nt", BF16, 1024, 1024, 1024, f"branch_a_{tag}")
    tb = _mm(yb, w["wbb_t"], "nt", BF16, 1024, 1024, 1024, f"branch_b_{tag}")
    tc = _mm(yc, w["wbc_t"], "nt", BF16, 1024, 1024, 1024, f"branch_c_{tag}")
    merged = _merge_fwd(ta, tb, tc, proj, f"merge_fwd_{tag}")
    x_new = _mm(merged, w["wout"], "nn", F32, 1024, 512, D_MODEL, f"out_proj_{tag}", res=x)
    saved = dict(x=x, h=h, proj=proj, s5_raw=s5_raw, s5_ops=s5_ops, ya0=ya0, ya=ya, yb=yb, yc=yc, o_att=o_att,
                 ta=ta, tb=tb, tc=tc, merged=merged)
    return x_new, saved


def _layer_bwd(dx_out, p, w, tabs, s, tag, first_after=None, after_merge=None, before_win=None, after_win=None):
    L = dx_out.shape[0]
    proj = s["proj"]
    big, small = {}, {}
    dmerged = _mm(dx_out, w["wout"], "nt", BF16, 1024, 512, D_MODEL, f"d_merged_{tag}", after=first_after)
    big["wout"] = _mm(s["merged"], dx_out, "tn", GRAD_DT, 512, 1024, L, f"d_wout_{tag}")
    dta, dtb, dtc, dga, dgb, dgc = _merge_bwd(s["ta"], s["tb"], s["tc"], proj, dmerged, f"merge_bwd_{tag}")
    tok = after_merge(dga) if after_merge is not None else None
    dy = {}
    for br, dt in (("a", dta), ("b", dtb), ("c", dtc)):
        dy[br] = _mm(dt, w[f"wb{br}_t"], "nn", BF16, 1024, 1024, D_MODEL, f"d_y{br}_{tag}", after=tok)
        big[f"wb{br}_t"] = _mm(dt, s[f"y{br}"], "tn", GRAD_DT, 512, 1024, L, f"d_wb{br}_{tag}")

    dq, dzc, dkvc, dkvp, dsink = _attn_bwd(proj, p["attn_sinks"], tabs, s["o_att"], dy["c"], f"attn_bwd_{tag}")
    dkv = dkvc + jnp.concatenate([dkvp[CHUNK:], jnp.zeros((CHUNK, 256), F32)], axis=0)
    small["attn_sinks"] = dsink[:, 0]

    dub, dvb, dzb, dlw, dlb, dsgw, dsgb = _sg_bwd(
        proj, p["sg_ln_w"][None], p["sg_ln_b"][None], p["sg_w"], p["sg_b"][:, :, None], dy["b"], f"sg_bwd_{tag}")
    small.update(sg_ln_w=dlw[0], sg_ln_b=dlb[0], sg_w=dsgw, sg_b=dsgb[:, :, 0])

    dya0, dza, dglu, dglub = _glu_bwd(s["ya0"], proj, w["glu"], p["ssm_glu_b"][None], dy["a"], f"glu_bwd_{tag}")
    big["glu"] = dglu.astype(GRAD_DT)
    small["ssm_glu_b"] = dglub[0]

    dua, dbre, dbim, dcre, dcim, dlr, dli, dd = _s5_bwd(proj, dya0, *s["s5_ops"], name=f"s5_bwd_{tag}")
    da_re, da_im, dlog_dt, dbt_re, dbt_im = _s5_params_bwd(
        *s["s5_raw"], dlr.reshape(SSM_GROUPS, SSM_STATE), dli.reshape(SSM_GROUPS, SSM_STATE),
        _unslab_b(dbre), _unslab_b(dbim), name=f"s5_params_bwd_{tag}")
    small.update(ssm_a_re=da_re, ssm_a_im=da_im, ssm_log_dt=dlog_dt[:, 0],
                 ssm_bt_re=dbt_re, ssm_bt_im=dbt_im,
                 ssm_c_re=_unslab_c(dcre), ssm_c_im=_unslab_c(dcim), ssm_d=dd.reshape(SSM_WIDTH))

    dproj = jnp.concatenate([dua, dza, dub, dvb, dzb, dq, dkv.astype(BF16), dzc, dga, dgb, dgc], axis=-1)
    tok = before_win(big) if before_win is not None else None
    big["win_t"] = _mm(dproj, s["h"], "tn", GRAD_DT, 256, D_MODEL, L, f"d_win_{tag}", after=tok)
    tok = after_win(big) if after_win is not None else None
    dh = _mm(dproj, w["win_t"], "nn", F32, L, D_MODEL, 256, f"d_h_{tag}", after=tok)
    dx_in, dnw = _rms_bwd(s["x"], p["norm_w"][None], dh, dx_out, f"rms_bwd_{tag}")
    small["norm_w"] = dnw[0]
    return dx_in, big, small


def _local_step(x, tgt, small_p, final_w, big_w):
    L = x.shape[0]
    tabs = _rope_tables(L)
    saved = []
    for l in range(DEPTH):
        x, s = _layer_fwd(x, small_p[l], big_w[l], tabs, f"l{l}")
        saved.append(s)
    loss_acc, dx, dfw = _final(x, final_w[None], tgt, "final_norm_loss")
    big_g, small_g = [None] * DEPTH, [None] * DEPTH
    for l in reversed(range(DEPTH)):
        dx, big_g[l], small_g[l] = _layer_bwd(dx, small_p[l], big_w[l], tabs, saved[l], f"l{l}")
    return loss_acc[0, 0], dx, dfw[0], big_g, small_g


MESH = pl.DeviceIdType.MESH
ANY = pl.BlockSpec(memory_space=pl.ANY)
ROW_ALIGN = 16


def _place():
    return lax.axis_index("x"), lax.axis_index("y"), lax.axis_index("c")


HBM = pl.BlockSpec(memory_space=pltpu.HBM)
SEM = pl.BlockSpec(memory_space=pltpu.SEMAPHORE)
EFFECT = pltpu.SideEffectType.DATAFLOW_SIDE_EFFECTING


def _split_start(srcs, lands, n_copies, copies, name, after=None):
    n, m, k = len(srcs), len(lands), n_copies
    extra = [] if after is None else [after]

    def body(*refs):
        src_refs, land_refs = refs[:n], refs[n:n + m]
        sems = refs[n + m + len(extra):]
        send_sems, recv_sems, token = sems[:k], sems[k:2 * k], refs[-1]
        for cp in copies(src_refs, land_refs, send_sems, recv_sems):
            cp.start()
        token[...] = jnp.zeros_like(token)

    ops = list(srcs) + list(lands)
    outs = pl.pallas_call(
        body, name=name,
        out_shape=(*[pltpu.SemaphoreType.DMA(())] * (2 * k),
                   *[pltpu.HBM(a.shape, a.dtype) for a in ops], jax.ShapeDtypeStruct((8, 128), F32)),
        in_specs=[HBM] * (n + m) + [ANY] * len(extra),
        out_specs=(*[SEM] * (2 * k), *[HBM] * (n + m), pl.BlockSpec(memory_space=pltpu.VMEM)),
        input_output_aliases={i: 2 * k + i for i in range(n + m)},
        compiler_params=pltpu.CompilerParams(has_side_effects=EFFECT),
    )(*[pltpu.with_memory_space_constraint(a, pltpu.HBM) for a in ops], *extra)
    return (list(outs[:k]), list(outs[k:2 * k]), list(outs[2 * k:2 * k + n]), list(outs[2 * k + n:2 * k + n + m]),
            outs[-1])


def _split_wait(send_sems, recv_sems, srcs, lands, after, copies, name):
    n, m, k = len(srcs), len(lands), len(send_sems)
    after = list(after) if isinstance(after, (list, tuple)) else [after]

    def body(*refs):
        src_refs, land_refs = refs[:n], refs[n:n + m]
        for cp in copies(src_refs, land_refs, refs[n + m:n + m + k], refs[n + m + k:n + m + 2 * k]):
            cp.wait_send()
            cp.wait_recv()

    ops = list(srcs) + list(lands)
    outs = pl.pallas_call(
        body, name=name,
        out_shape=tuple(pltpu.HBM(a.shape, a.dtype) for a in ops),
        in_specs=[HBM] * (n + m) + [SEM] * (2 * k) + [ANY] * len(after),
        out_specs=tuple([HBM] * (n + m)),
        input_output_aliases={i: i for i in range(n + m)},
        compiler_params=pltpu.CompilerParams(has_side_effects=EFFECT),
    )(*ops, *send_sems, *recv_sems, *after)
    return list(outs[:n]), list(outs[n:])


def _ag_rows(land_ref, px, py, pc):
    r = land_ref.shape[0] // N_DEV
    start = pl.multiple_of((4 * px + 2 * py + pc) * r, ROW_ALIGN)
    return land_ref.at[pl.ds(start, r), :]


def _ag_copies_to(which):
    def copies(src_refs, land_refs, send_sems, recv_sems):
        x, y, c = _place()
        peers = [(x, y, 1 - c), (1 - x, y, c), (x, 1 - y, c), (1 - x, 1 - y, c)]
        return [pltpu.make_async_remote_copy(
            src_ref=_ag_rows(land_refs[a], x, y, c), dst_ref=_ag_rows(land_refs[a], x, y, c),
            send_sem=send_sems[len(which) * a + k], recv_sem=recv_sems[len(which) * a + k],
            device_id=peers[p], device_id_type=MESH)
            for a in range(len(land_refs)) for k, p in enumerate(which)]
    return copies


_ag_copies = _ag_copies_to((0, 1, 2, 3))
_ag_copies_near = _ag_copies_to((0, 1, 2))
_ag_copies_far = _ag_copies_to((3,))


def _ag_forward(lands, name, which=(0, 1, 2)):
    n = len(lands)

    def body(*refs):
        land_refs = refs[n:2 * n]
        send_sems, recv_sems = refs[2 * n:]
        x, y, c = _place()
        chips = [(1 - x, y), (x, 1 - y), (1 - x, 1 - y)]

        def copy(a, k, pc):
            px, py = chips[which[k]]
            return pltpu.make_async_remote_copy(
                src_ref=_ag_rows(land_refs[a], px, py, pc), dst_ref=_ag_rows(land_refs[a], px, py, pc),
                send_sem=send_sems.at[a, k], recv_sem=recv_sems.at[a, k], device_id=(x, y, 1 - c), device_id_type=MESH)

        passed = [copy(a, k, c) for a in range(n) for k in range(len(which))]
        for cp in passed:
            cp.start()
        for a in range(n):
            for k in range(len(which)):
                copy(a, k, 1 - c).wait_recv()
        for cp in passed:
            cp.wait_send()

    sems = pltpu.SemaphoreType.DMA((n, len(which)))
    return pl.pallas_call(
        body, name=name,
        in_specs=[ANY] * n, out_specs=[ANY] * n,
        out_shape=[jax.ShapeDtypeStruct(l.shape, l.dtype) for l in lands],
        input_output_aliases={i: i for i in range(n)},
        scratch_shapes=[sems, sems],
    )(*lands)


def _allgather_place(shards):
    x, y, c = _place()
    return [lax.dynamic_update_slice(lax.empty((N_DEV * s.shape[0], s.shape[1]), s.dtype), s,
                                     ((4 * x + 2 * y + c) * s.shape[0], 0)) for s in shards]


def _allgather_start(lands, name, after=None):
    return _split_start([], lands, 4 * len(lands), _ag_copies, name + "_start", after=after)


def _allgather_finish(started, after, name):
    send_sems, recv_sems, _, lands, _ = started
    _, lands = _split_wait(send_sems, recv_sems, [], lands, after, _ag_copies, name + "_wait")
    return list(_ag_forward(lands, name + "_forward"))


def _rs_swap_cores(grads, name):
    n = len(grads)

    def body(*refs):
        ins, outs = refs[:n], refs[n:2 * n]
        send_sems, recv_sems = refs[2 * n:]
        x, y, c = _place()
        cps = []
        for a in range(n):
            r = ins[a].shape[0] // N_DEV
            for q in range(4):
                start = pl.multiple_of((2 * q + 1 - c) * r, ROW_ALIGN)
                cps.append(pltpu.make_async_remote_copy(
                    src_ref=ins[a].at[pl.ds(start, r), :], dst_ref=outs[a].at[q],
                    send_sem=send_sems.at[a, q], recv_sem=recv_sems.at[a, q],
                    device_id=(x, y, 1 - c), device_id_type=MESH))
        for cp in cps:
            cp.start()
        for cp in cps:
            cp.wait()

    return pl.pallas_call(
        body, name=name, in_specs=[ANY] * n, out_specs=[ANY] * n,
        out_shape=[jax.ShapeDtypeStruct((4, g.shape[0] // N_DEV, g.shape[1]), g.dtype) for g in grads],
        scratch_shapes=[pltpu.SemaphoreType.DMA((n, 4)), pltpu.SemaphoreType.DMA((n, 4))],
    )(*grads)


def _rs_chip_copies(sum_refs, land_refs, send_sems, recv_sems):
    x, y, c = _place()
    chips = [(1 - x, y), (x, 1 - y), (1 - x, 1 - y)]
    return [pltpu.make_async_remote_copy(
        src_ref=sum_refs[a].at[2 * px + py], dst_ref=land_refs[a].at[2 * x + y],
        send_sem=send_sems[3 * a + j], recv_sem=recv_sems[3 * a + j], device_id=(px, py, c), device_id_type=MESH)
        for a in range(len(sum_refs)) for j, (px, py) in enumerate(chips)]


def _row_tile(r):
    return max(t for t in range(ROW_ALIGN, min(r, 1024) + 1, ROW_ALIGN) if r % t == 0)


def _rs_add_cores(grad, recv, cidx, name):
    r, cols = recv.shape[1], recv.shape[2]
    tr = _row_tile(r)
    nb = r // tr

    def body(c_ref, g_ref, r_ref, o_ref):
        o_ref[...] = (g_ref[...].astype(F32) + r_ref[...].astype(F32)).astype(o_ref.dtype)

    return pl.pallas_call(
        body, name=name,
        grid_spec=pltpu.PrefetchScalarGridSpec(
            num_scalar_prefetch=1, grid=(4, nb),
            in_specs=[pl.BlockSpec((tr, cols), lambda q, i, c_ref: ((2 * q + c_ref[0]) * nb + i, 0)),
                      pl.BlockSpec((None, tr, cols), lambda q, i, c_ref: (q, i, 0))],
            out_specs=pl.BlockSpec((None, tr, cols), lambda q, i, c_ref: (q, i, 0))),
        out_shape=jax.ShapeDtypeStruct(recv.shape, recv.dtype),
        compiler_params=_cp(("parallel", "parallel")),
    )(cidx, grad, recv)


def _rs_add_chips(own, recv, slots, name):
    r, cols = recv.shape[1], recv.shape[2]
    tr = _row_tile(r)

    def body(s_ref, o_ref, r0_ref, r1_ref, r2_ref, out_ref):
        acc = o_ref[...].astype(F32)
        for ref in (r0_ref, r1_ref, r2_ref):
            acc = acc + ref[...].astype(F32)
        out_ref[...] = acc

    pick = lambda k: pl.BlockSpec((None, tr, cols), functools.partial(lambda i, s_ref, k: (s_ref[k], i, 0), k=k))
    return pl.pallas_call(
        body, name=name,
        grid_spec=pltpu.PrefetchScalarGridSpec(
            num_scalar_prefetch=1, grid=(r // tr,),
            in_specs=[pick(0), pick(1), pick(2), pick(3)],
            out_specs=pl.BlockSpec((tr, cols), lambda i, s_ref: (i, 0))),
        out_shape=jax.ShapeDtypeStruct((r, cols), F32),
        compiler_params=_cp(("parallel",)),
    )(slots, own, recv, recv, recv)


def _rs_core_copies(grad_refs, land_refs, send_sems, recv_sems):
    x, y, c = _place()
    cps = []
    for a in range(len(grad_refs)):
        r = grad_refs[a].shape[0] // N_DEV
        for q in range(4):
            start = pl.multiple_of((2 * q + 1 - c) * r, ROW_ALIGN)
            cps.append(pltpu.make_async_remote_copy(
                src_ref=grad_refs[a].at[pl.ds(start, r), :], dst_ref=land_refs[a].at[q],
                send_sem=send_sems[4 * a + q], recv_sem=recv_sems[4 * a + q],
                device_id=(x, y, 1 - c), device_id_type=MESH))
    return cps


def _reduce_scatter_chips_start(grads, recv, tag):
    cidx = lax.axis_index("c").astype(jnp.int32)[None]
    sums = [_rs_add_cores(g, rv, cidx, f"rs_add_cores_{tag}_{i}") for i, (g, rv) in enumerate(zip(grads, recv))]
    lands = [lax.empty(s.shape, s.dtype) for s in sums]
    return _split_start(sums, lands, 3 * len(sums), _rs_chip_copies, f"rs_chips_{tag}_start")


def _reduce_scatter_start(grads, tag):
    return _reduce_scatter_chips_start(grads, _rs_swap_cores(grads, f"rs_swap_cores_{tag}"), tag)


def _reduce_scatter_cores_start(grads, tag):
    lands = [lax.empty((4, g.shape[0] // N_DEV, g.shape[1]), g.dtype) for g in grads]
    return _split_start(grads, lands, 4 * len(grads), _rs_core_copies, f"rs_cores_{tag}_start")


def _reduce_scatter_cores_finish(started, after, tag):
    send_sems, recv_sems, grads, lands, _ = started
    grads, recv = _split_wait(send_sems, recv_sems, grads, lands, after, _rs_core_copies, f"rs_cores_{tag}_wait")
    return _reduce_scatter_chips_start(grads, recv, tag)


def _reduce_scatter_finish(started, after, tag):
    send_sems, recv_sems, sums, lands, _ = started
    sums, lands = _split_wait(send_sems, recv_sems, sums, lands, after, _rs_chip_copies, f"rs_chips_{tag}_wait")
    x, y = lax.axis_index("x"), lax.axis_index("y")
    slots = jnp.stack([2 * x + y, 2 * (1 - x) + y, 2 * x + 1 - y, 2 * (1 - x) + 1 - y]).astype(jnp.int32)
    return [_rs_add_chips(s, l, slots, f"rs_add_chips_{tag}_{i}") for i, (s, l) in enumerate(zip(sums, lands))]


def _ar_peers():
    x, y, c = _place()
    return [(1 - x if k & 4 else x, 1 - y if k & 2 else y, 1 - c if k & 1 else c) for k in range(1, N_DEV)]


def _ar_scatter_copies(src_refs, land_refs, send_sems, recv_sems):
    x, y, c = _place()
    cps = []
    for a in range(len(src_refs)):
        rs = src_refs[a].shape[0] // N_DEV
        for k, (px, py, pc) in enumerate(_ar_peers()):
            start = pl.multiple_of((4 * px + 2 * py + pc) * rs, 8)
            cps.append(pltpu.make_async_remote_copy(
                src_ref=src_refs[a].at[pl.ds(start, rs), :], dst_ref=land_refs[a].at[4 * x + 2 * y + c],
                send_sem=send_sems[7 * a + k], recv_sem=recv_sems[7 * a + k],
                device_id=(px, py, pc), device_id_type=MESH))
    return cps


def _ar_gather_copies(src_refs, land_refs, send_sems, recv_sems):
    x, y, c = _place()
    cps = []
    for a in range(len(land_refs)):
        rs = land_refs[a].shape[0] // N_DEV
        mine = land_refs[a].at[pl.ds(pl.multiple_of((4 * x + 2 * y + c) * rs, 8), rs), :]
        for k, peer in enumerate(_ar_peers()):
            cps.append(pltpu.make_async_remote_copy(
                src_ref=mine, dst_ref=mine, send_sem=send_sems[7 * a + k], recv_sem=recv_sems[7 * a + k],
                device_id=peer, device_id_type=MESH))
    return cps


def _allreduce_start(packs, name, after=None):
    assert all(p.shape[0] % (8 * N_DEV) == 0 for p in packs)
    x, y, c = _place()
    me = 4 * x + 2 * y + c
    lands = []
    for p in packs:
        rs = p.shape[0] // N_DEV
        own = lax.dynamic_slice(p, (me * rs, 0), (rs, p.shape[1]))
        lands.append(lax.dynamic_update_slice(lax.empty((N_DEV, rs, p.shape[1]), F32), own[None], (me, 0, 0)))
    return _split_start(packs, lands, 7 * len(packs), _ar_scatter_copies, name + "_scatter_start", after=after)


def _allreduce_middle(started, after, name):
    n = len(started[2])
    _, parts = _split_wait(started[0], started[1], started[2], started[3], after, _ar_scatter_copies, name + "_scatter_wait")

    def body(*refs):
        for p_ref, o_ref in zip(refs[:n], refs[n:]):
            acc = p_ref[0]
            for d in range(1, N_DEV):
                acc = acc + p_ref[d]
            o_ref[...] = acc

    sums = pl.pallas_call(body, name=name + "_add",
                          out_shape=[jax.ShapeDtypeStruct(p.shape[1:], F32) for p in parts])(*parts)
    x, y, c = _place()
    me = 4 * x + 2 * y + c
    lands = [lax.dynamic_update_slice(lax.empty((N_DEV * s.shape[0], s.shape[1]), F32), s, (me * s.shape[0], 0))
             for s in sums]
    return _split_start([], lands, 7 * n, _ar_gather_copies, name + "_gather_start")


def _allreduce_finish(started, after, name):
    _, lands = _split_wait(started[0], started[1], [], started[3], after, _ar_gather_copies, name + "_gather_wait")
    return lands


ADAM_TILE_BYTES = 2 * 1024 * 1024


def _adam_tiles(rows, cols):
    tc = cols // 2 if cols % 256 == 0 and cols >= 2048 else cols
    tr = max(t for t in range(8, rows + 1, 8) if rows % t == 0 and t * max(tc, 128) * 4 <= ADAM_TILE_BYTES) \
        if rows % 8 == 0 else rows
    return tr, tc


def _adam_math(w, g, m, v):
    nm = ADAM_B1 * m + (1.0 - ADAM_B1) * g
    nv = ADAM_B2 * v + (1.0 - ADAM_B2) * jnp.square(g)
    c1 = 1.0 - ADAM_B1 ** ADAM_STEP
    c2 = 1.0 - ADAM_B2 ** ADAM_STEP
    return -ADAM_LR * ((nm / c1) / (jnp.sqrt(nv / c2) + ADAM_EPS) + ADAM_WD * w), nm, nv


def _adamw_layer(w, g, m, v, layer, carry, name):
    _, rows, cols = w.shape
    tr, tc = _adam_tiles(rows, cols)

    def body(w_ref, g_ref, m_ref, v_ref, *rest):
        go_ref, d_ref, nm_ref, nv_ref = rest[-4:]
        gv = g_ref[...]
        go_ref[...] = gv
        d_ref[...], nm_ref[...], nv_ref[...] = _adam_math(w_ref[...], gv, m_ref[...], v_ref[...])

    blk = pl.BlockSpec((None, tr, tc), lambda i, j: (layer, i, j))
    flat = pl.BlockSpec((tr, tc), lambda i, j: (i, j))
    sh = jax.ShapeDtypeStruct(w.shape, F32)
    carry = [] if carry is None else list(carry)
    return pl.pallas_call(
        body, name=name, grid=(rows // tr, cols // tc),
        in_specs=[blk, flat, blk, blk] + [ANY] * len(carry), out_specs=[blk] * 4, out_shape=[sh] * 4,
        input_output_aliases={4 + k: k for k in range(len(carry))},
        compiler_params=_cp(("parallel", "parallel")),
    )(w, g, m, v, *carry)


def _adamw(w, g, m, v, name):
    shape = w.shape
    rows, cols = shape[-2:]
    lead = shape[:-2]
    nl = math.prod(lead)
    tr, tc = _adam_tiles(rows, cols)

    def body(w_ref, g_ref, m_ref, v_ref, d_ref, nm_ref, nv_ref):
        d_ref[...], nm_ref[...], nv_ref[...] = _adam_math(w_ref[...], g_ref[...], m_ref[...], v_ref[...])

    def index(b, i, j):
        return (*jnp.unravel_index(b, lead), i, j) if lead else (i, j)

    blk = pl.BlockSpec((*[None] * len(lead), tr, tc), index)
    sh = jax.ShapeDtypeStruct(shape, F32)
    return pl.pallas_call(
        body, name=name, grid=(nl, rows // tr, cols // tc), in_specs=[blk] * 4, out_specs=[blk] * 3,
        out_shape=[sh] * 3, compiler_params=_cp(("parallel", "parallel", "parallel")),
    )(w, g, m, v)


WEIGHTS = ("norm_w", "w_in", "ssm_a_re", "ssm_a_im", "ssm_log_dt", "ssm_b_re", "ssm_b_im", "ssm_c_re", "ssm_c_im",
           "ssm_d", "ssm_glu_w", "ssm_glu_b", "sg_ln_w", "sg_ln_b", "sg_w", "sg_b", "attn_sinks",
           "w_branch_a", "w_branch_b", "w_branch_c", "w_out", "final_norm_w")
BIG = ("w_in", "ssm_glu_w", "w_branch_a", "w_branch_b", "w_branch_c", "w_out")
BIG_KEY = {"w_in": ("win_t", True), "ssm_glu_w": ("glu", False), "w_branch_a": ("wba_t", True),
           "w_branch_b": ("wbb_t", True), "w_branch_c": ("wbc_t", True), "w_out": ("wout", False)}
VIEWS = {"w_in": (1, 2), "ssm_b_re": (2, 3), "ssm_b_im": (2, 3)}
MATS = ("ssm_a_re", "ssm_a_im", "ssm_c_re", "ssm_c_im", "ssm_b_re", "ssm_b_im", "sg_w")
VEC_GROUPS = (("ssm_d", "ssm_glu_b", "sg_ln_w", "sg_ln_b"), ("norm_w", "final_norm_w", "sg_b"), ("ssm_log_dt", "attn_sinks"))
PACK_ROWS = 8 * N_DEV


def _view(n, a):
    return jnp.swapaxes(a, *VIEWS[n]) if n in VIEWS else a


def _vec_moves(pack_ref, refs, to_pack):
    d, gb, lw, lb, nw, fw, sb, ld, sk = refs
    full = (slice(None), slice(None))
    moves = [((slice(2 * i, 2 * i + 2), slice(None)), r, full) for i, r in enumerate((d, gb, lw, lb))]
    moves += [((slice(8, 10), slice(None)), nw, (slice(None), slice(0, 1024))),
              ((slice(10, 12), slice(None)), nw, (slice(None), slice(1024, 2048))),
              ((slice(12, 13), slice(None)), fw, (slice(None), slice(0, 1024))),
              ((slice(13, 14), slice(None)), fw, (slice(None), slice(1024, 2048))),
              ((slice(16, 32), slice(0, 128)), sb, full),
              ((slice(32, 34), slice(0, 64)), ld, full),
              ((slice(34, 36), slice(0, 16)), sk, full)]
    for where, ref, part in moves:
        if to_pack:
            pack_ref[where] = ref[part]
        else:
            ref[part] = pack_ref[where]


def _vec_shapes(arrs):
    d, gb, lw, lb, nw, fw, sb, ld, sk = arrs
    return [d, gb, lw, lb, nw, fw.reshape(1, -1), sb.reshape(-1, sb.shape[-1]), ld, sk]


def _vec_pack(arrs, name):
    def body(*refs):
        refs[-1][...] = jnp.zeros_like(refs[-1])
        _vec_moves(refs[-1], refs[:-1], True)

    return pl.pallas_call(body, name=name, out_shape=jax.ShapeDtypeStruct((PACK_ROWS, 1024), F32))(*_vec_shapes(arrs))


def _vec_unpack(pack, like, name):
    shaped = _vec_shapes(like)

    def body(pack_ref, *refs):
        _vec_moves(pack_ref, refs, False)

    outs = pl.pallas_call(body, name=name, out_shape=[jax.ShapeDtypeStruct(a.shape, F32) for a in shaped])(pack)
    return [o.reshape(a.shape) for o, a in zip(outs, like)]


def _pack(groups, cols, name):
    assert cols == 1024
    return _vec_pack([a for arrs in groups for a in arrs], name)


def _unpack(pack, groups, name):
    return _vec_unpack(pack, [a for arrs in groups for a in arrs], name)


def kernel(x, norm_w, w_in, ssm_a_re, ssm_a_im, ssm_log_dt, ssm_b_re, ssm_b_im, ssm_c_re, ssm_c_im, ssm_d, ssm_glu_w, ssm_glu_b, sg_ln_w, sg_ln_b, sg_w, sg_b, attn_sinks, w_branch_a, w_branch_b, w_branch_c, w_out, final_norm_w, loss_target, m_norm_w, m_w_in, m_ssm_a_re, m_ssm_a_im, m_ssm_log_dt, m_ssm_b_re, m_ssm_b_im, m_ssm_c_re, m_ssm_c_im, m_ssm_d, m_ssm_glu_w, m_ssm_glu_b, m_sg_ln_w, m_sg_ln_b, m_sg_w, m_sg_b, m_attn_sinks, m_w_branch_a, m_w_branch_b, m_w_branch_c, m_w_out, m_final_norm_w, v_norm_w, v_w_in, v_ssm_a_re, v_ssm_a_im, v_ssm_log_dt, v_ssm_b_re, v_ssm_b_im, v_ssm_c_re, v_ssm_c_im, v_ssm_d, v_ssm_glu_w, v_ssm_glu_b, v_sg_ln_w, v_sg_ln_b, v_sg_w, v_sg_b, v_attn_sinks, v_w_branch_a, v_w_branch_b, v_w_branch_c, v_w_out, v_final_norm_w):
    w = dict(zip(WEIGHTS, (norm_w, w_in, ssm_a_re, ssm_a_im, ssm_log_dt, ssm_b_re, ssm_b_im, ssm_c_re, ssm_c_im, ssm_d, ssm_glu_w, ssm_glu_b, sg_ln_w, sg_ln_b, sg_w, sg_b, attn_sinks, w_branch_a, w_branch_b, w_branch_c, w_out, final_norm_w)))
    m = dict(zip(WEIGHTS, (m_norm_w, m_w_in, m_ssm_a_re, m_ssm_a_im, m_ssm_log_dt, m_ssm_b_re, m_ssm_b_im, m_ssm_c_re, m_ssm_c_im, m_ssm_d, m_ssm_glu_w, m_ssm_glu_b, m_sg_ln_w, m_sg_ln_b, m_sg_w, m_sg_b, m_attn_sinks, m_w_branch_a, m_w_branch_b, m_w_branch_c, m_w_out, m_final_norm_w)))
    v = dict(zip(WEIGHTS, (v_norm_w, v_w_in, v_ssm_a_re, v_ssm_a_im, v_ssm_log_dt, v_ssm_b_re, v_ssm_b_im, v_ssm_c_re, v_ssm_c_im, v_ssm_d, v_ssm_glu_w, v_ssm_glu_b, v_sg_ln_w, v_sg_ln_b, v_sg_w, v_sg_b, v_attn_sinks, v_w_branch_a, v_w_branch_b, v_w_branch_c, v_w_out, v_final_norm_w)))

    keys = [BIG_KEY[n][0] for n in BIG]
    wv, mv, vv = ({n: _view(n, a) for n, a in d.items()} for d in (w, m, v))
    shards = [[(wv[n][l] if n in VIEWS else w[n][l].T if BIG_KEY[n][1] else w[n][l]).astype(BF16) for n in BIG]
              for l in range(DEPTH)]
    small_p = [{n: w[n][l] for n in SMALL} for l in range(DEPTH)]
    xv, tgt = x[0], loss_target[0]
    tabs = _rope_tables(xv.shape[0])

    lands = [[_allgather_place(shards[l][:1]), _allgather_place(shards[l][1:])] for l in range(DEPTH)]
    s5 = [_s5_prep(small_p[l], f"l{l}") for l in range(DEPTH)]
    vec_packs = [_pack([[d[n] for n in names] for names in VEC_GROUPS], 1024, f"pack_vec_{tag}")
                 for tag, d in (("w", wv), ("m", mv), ("v", vv))]
    near = _split_start([], lands[0][0], 3, _ag_copies_near, "ag_l0_win_near_start")
    got = {}
    x_, y_ = lax.axis_index("x"), lax.axis_index("y")
    n_tiles = D_IN // PROJ_TN
    far_first = (D_IN // 4 // PROJ_TN) * (2 * (1 - x_) + (1 - y_))
    n_far = -(-D_IN // 4 // PROJ_TN)
    tile_ids = jnp.arange(n_tiles, dtype=jnp.int32)
    is_far = (tile_ids >= far_first) & (tile_ids < far_first + n_far)
    near_tiles = jnp.sort(jnp.where(is_far, n_tiles, tile_ids))[:n_tiles - n_far]
    far_tiles = (far_first + jnp.arange(n_far)).astype(jnp.int32)

    def proj_of0(h):
        early = [h, *lands[0][1], *lands[1][0], *lands[1][1], *s5[0][1], *s5[1][1], near_tiles, far_tiles]
        early += vec_packs
        _, land = _split_wait(near[0], near[1], [], near[3], early, _ag_copies_near, "ag_l0_win_near_wait")
        far = _split_start([], land, 1, _ag_copies_far, "ag_l0_win_far_start")
        land = _ag_forward(far[3], "ag_l0_win_near_forward", which=(0, 1))
        got["ag0b"] = _allgather_start(lands[0][1], "ag_l0_rest", after=land[0])
        got["near1"] = _split_start([], lands[1][0], 3, _ag_copies_near, "ag_l1_win_near_start", after=got["ag0b"][4])
        proj = _in_proj_tiles(h, land[0], near_tiles, None, "in_proj_l0_near", after=got["near1"][4])
        _, land = _split_wait(far[0], far[1], [], land, proj, _ag_copies_far, "ag_l0_win_far_wait")
        got["win0"] = _ag_forward(land, "ag_l0_win_far_forward", which=(2,))[0]
        return _in_proj_tiles(h, got["win0"], far_tiles, proj, "in_proj_l0_far")

    def after_proj0(proj):
        got["w0"] = dict(zip(keys, [got["win0"]] + _allgather_finish(got["ag0b"], proj, "ag_l0_rest")))
        return got["w0"]

    x1, saved0 = _layer_fwd(xv, small_p[0], None, tabs, "l0", s5=s5[0], proj_of=proj_of0, after_proj=after_proj0)
    big_w0 = got["w0"]

    def proj_of1(h):
        near1 = got["near1"]
        _, land = _split_wait(near1[0], near1[1], [], near1[3], h, _ag_copies_near, "ag_l1_win_near_wait")
        far1 = _split_start([], land, 1, _ag_copies_far, "ag_l1_win_far_start")
        land = _ag_forward(far1[3], "ag_l1_win_near_forward", which=(0, 1))
        got["ag1b"] = _allgather_start(lands[1][1], "ag_l1_rest", after=land[0])
        proj = _in_proj_tiles(h, land[0], near_tiles, None, "in_proj_l1_near", after=got["ag1b"][4])
        _, land = _split_wait(far1[0], far1[1], [], land, proj, _ag_copies_far, "ag_l1_win_far_wait")
        got["win1"] = _ag_forward(land, "ag_l1_win_far_forward", which=(2,))[0]
        return _in_proj_tiles(h, got["win1"], far_tiles, proj, "in_proj_l1_far")

    def after_proj1(proj):
        got["w1"] = dict(zip(keys, [got["win1"]] + _allgather_finish(got["ag1b"], proj, "ag_l1_rest")))
        return got["w1"]

    x2, saved1 = _layer_fwd(x1, small_p[1], None, tabs, "l1", s5=s5[1], proj_of=proj_of1, after_proj=after_proj1)
    big_w1 = got["w1"]
    loss_acc, dx2, dfw = _final(x2, w["final_norm_w"][None], tgt, "final_norm_loss")
    loss = lax.psum(loss_acc[0, 0], ("x", "y", "c"))
    dfw = dfw[0]

    dx1, big_g1, small_g1 = _layer_bwd(dx2, small_p[1], big_w1, tabs, saved1, "l1")
    rs1_cores = _reduce_scatter_cores_start([big_g1[k] for k in keys], "l1")

    def after_merge0(x):
        got["rs1"] = _reduce_scatter_cores_finish(rs1_cores, x, "l1")
        return got["rs1"][4]

    def before_win0(big):
        got["rs0b"] = _reduce_scatter_start([big[k] for k in keys[1:]], "l0_rest")
        return got["rs0b"][4]

    def after_win0(big):
        got["rs0a"] = _reduce_scatter_start([big["win_t"]], "l0_win")
        return got["rs0a"][4]

    dx, big_g0, small_g0 = _layer_bwd(dx1, small_p[0], big_w0, tabs, saved0, "l0", first_after=rs1_cores[4],
                                      after_merge=after_merge0, before_win=before_win0, after_win=after_win0)
    rs1 = got["rs1"]
    small_g = [small_g0, small_g1]
    grads, delta, new_m, new_v = {}, {}, {}, {}

    def big_adam(red, layer, carry):
        outs = {}
        for i, n in enumerate(BIG):
            g = red[i].T if BIG_KEY[n][1] and n not in VIEWS else red[i]
            outs[n] = _adamw_layer(wv[n], g, mv[n], vv[n], layer, None if carry is None else carry[n], f"adamw_{n}_l{layer}")
        return outs

    def small_grad(n):
        if n == "final_norm_w":
            return dfw
        if n in ("ssm_b_re", "ssm_b_im"):
            return jnp.stack([small_g[l][n.replace("ssm_b_", "ssm_bt_")].transpose(1, 0, 2) for l in range(DEPTH)])
        return jnp.stack([small_g[l][n] for l in range(DEPTH)])

    rows_of = lambda a: a.reshape(-1, a.shape[-1])
    g_mats = [rows_of(small_grad(n)) for n in MATS]
    g_vecs = [[small_grad(n) for n in names] for names in VEC_GROUPS]
    ar = _allreduce_start(g_mats + [_pack(g_vecs, 1024, "pack_vec_g")], "allreduce_small")
    big1 = big_adam(_reduce_scatter_finish(rs1, [dx, ar[4]], "l1"), 1, None)
    ar = _allreduce_middle(ar, [big1[n][1] for n in BIG], "allreduce_small")
    red0 = (_reduce_scatter_finish(got["rs0a"], ar[4], "l0_win")
            + _reduce_scatter_finish(got["rs0b"], ar[4], "l0_rest"))
    big0 = big_adam(red0, 0, big1)
    for n, outs in big0.items():
        grads[n], delta[n], new_m[n], new_v[n] = outs
    reduced = _allreduce_finish(ar, [big0[n][1] for n in BIG], "allreduce_small")
    for n, red in zip(MATS, reduced):
        outs = _adamw(rows_of(wv[n]), red, rows_of(mv[n]), rows_of(vv[n]), f"adamw_{n}")
        grads[n], delta[n], new_m[n], new_v[n] = (o.reshape(wv[n].shape) for o in (red, *outs))
    vec_names = [n for names in VEC_GROUPS for n in names]
    grads.update(zip(vec_names, _unpack(reduced[-1], g_vecs, "unpack_vec_g")))
    outs = _adamw(vec_packs[0], reduced[-1], vec_packs[1], vec_packs[2], "adamw_vec")
    for tag, res, o in zip("dmv", (delta, new_m, new_v), outs):
        res.update(zip(vec_names, _unpack(o, [[wv[n] for n in names] for names in VEC_GROUPS], f"unpack_vec_{tag}")))

    return (loss, dx[None], *[_view(n, d[n]) for d in (grads, delta, new_m, new_v) for n in WEIGHTS])
```

```python
import functools
import math

import jax
import jax.numpy as jnp
from jax import lax
from jax.experimental import pallas as pl
from jax.experimental.pallas import tpu as pltpu

F32 = jnp.float32
BF16 = jnp.bfloat16

D_MODEL = 2048
DEPTH = 2
EPS = 1e-6
NEG_INF = -1e30
N_DEV = 8

SSM_WIDTH = 1024
SSM_GROUP = 16
SSM_GROUPS = 64
SSM_STATE = 64
N_SLAB = 8
SLAB_CH = 128
SLAB_ST = 512
SUB = 8
N_GRP = 2
N_SEG = SUB * N_GRP

SG_HEADS = 8
CHUNK = 128
HEAD_DIM = 64
ATT_HEADS = 16
ROT_DIM = 16
ROPE_THETA = 500000.0

D_IN = 13568
OFF_UA, OFF_ZA, OFF_UB, OFF_VB, OFF_ZB, OFF_Q, OFF_KV, OFF_ZC, OFF_G = (
    0, 1024, 2048, 3072, 4096, 5120, 6144, 6400, 7424)

ADAM_LR, ADAM_B1, ADAM_B2, ADAM_EPS, ADAM_WD, ADAM_STEP = 0.001, 0.9, 0.999, 1e-08, 0.01, 10

VMEM_LIMIT = 56 * 1024 * 1024


def _cp(sem=None):
    return pltpu.CompilerParams(dimension_semantics=sem, vmem_limit_bytes=VMEM_LIMIT)


def _dot(a, b):
    return jnp.dot(a, b, preferred_element_type=F32)


def _dot_nt(a, b):
    return lax.dot_general(a, b, (((1,), (1,)), ((), ())), preferred_element_type=F32)


def _dot_tn(a, b):
    return lax.dot_general(a, b, (((0,), (0,)), ((), ())), preferred_element_type=F32)


def _mm(a, b, mode, out_dtype, tm, tn, tk, name, res=None, after=None):
    if mode == "nn":
        (m, k), (_, n) = a.shape, b.shape
    elif mode == "nt":
        (m, k), (n, _) = a.shape, b.shape
    else:
        (k, m), (_, n) = a.shape, b.shape
    tm, tn, tk = min(tm, m), min(tn, n), min(tk, k)
    assert m % tm == 0 and n % tn == 0 and k % tk == 0, (name, m, n, k, tm, tn, tk)
    nk = k // tk
    a_spec = {"nn": pl.BlockSpec((tm, tk), lambda i, j, kk: (i, kk)),
              "nt": pl.BlockSpec((tm, tk), lambda i, j, kk: (i, kk)),
              "tn": pl.BlockSpec((tk, tm), lambda i, j, kk: (kk, i))}[mode]
    b_spec = {"nn": pl.BlockSpec((tk, tn), lambda i, j, kk: (kk, j)),
              "nt": pl.BlockSpec((tn, tk), lambda i, j, kk: (j, kk)),
              "tn": pl.BlockSpec((tk, tn), lambda i, j, kk: (kk, j))}[mode]
    dot = {"nn": _dot, "nt": _dot_nt, "tn": _dot_tn}[mode]
    has_res = res is not None
    direct = out_dtype == F32 and not has_res

    def body(*refs):
        ins, outs = refs[:2 + has_res + (after is not None)], refs[2 + has_res + (after is not None):]
        a_ref, b_ref = ins[:2]
        r_ref = ins[2] if has_res else None
        o_ref = outs[0]
        acc = o_ref if direct else outs[1]
        kk = pl.program_id(2)

        @pl.when(kk == 0)
        def _():
            acc[...] = jnp.zeros_like(acc)

        acc[...] += dot(a_ref[...].astype(BF16), b_ref[...].astype(BF16))

        if not direct:
            @pl.when(kk == nk - 1)
            def _():
                r = acc[...]
                if has_res:
                    r = r + r_ref[...]
                o_ref[...] = r.astype(out_dtype)

    in_specs = [a_spec, b_spec]
    args = [a, b]
    if has_res:
        in_specs.append(pl.BlockSpec((tm, tn), lambda i, j, kk: (i, j)))
        args.append(res)
    if after is not None:
        in_specs.append(pl.BlockSpec(memory_space=pl.ANY))
        args.append(after)
    return pl.pallas_call(
        body, name=name,
        grid=(m // tm, n // tn, nk),
        in_specs=in_specs,
        out_specs=pl.BlockSpec((tm, tn), lambda i, j, kk: (i, j)),
        out_shape=jax.ShapeDtypeStruct((m, n), out_dtype),
        scratch_shapes=[] if direct else [pltpu.VMEM((tm, tn), F32)],
        compiler_params=_cp(("parallel", "parallel", "arbitrary")),
    )(*args)


PROJ_TN = 256


def _in_proj_tiles(h, win_t, tiles, carry, name, after=None):
    L, K = h.shape
    extra = [a for a in (carry, after) if a is not None]

    def body(t_ref, h_ref, w_ref, *rest):
        rest[len(extra)][...] = _dot_nt(h_ref[...], w_ref[...]).astype(BF16)

    return pl.pallas_call(
        body, name=name,
        grid_spec=pltpu.PrefetchScalarGridSpec(
            num_scalar_prefetch=1, grid=(tiles.shape[0],),
            in_specs=[pl.BlockSpec((L, K), lambda j, t: (0, 0)), pl.BlockSpec((PROJ_TN, K), lambda j, t: (t[j], 0))]
            + [pl.BlockSpec(memory_space=pl.ANY)] * len(extra),
            out_specs=pl.BlockSpec((L, PROJ_TN), lambda j, t: (0, t[j]))),
        out_shape=jax.ShapeDtypeStruct((L, win_t.shape[0]), BF16),
        input_output_aliases={} if carry is None else {3: 0},
        compiler_params=_cp(("arbitrary",)),
    )(tiles, h, win_t, *extra)


def _rms(x, w):
    return x * lax.rsqrt(jnp.mean(x * x, axis=-1, keepdims=True) + EPS) * w


def _rms_fwd(x, w, name):
    L, D = x.shape
    tm = min(L, 256)

    def body(x_ref, w_ref, h_ref):
        h_ref[...] = _rms(x_ref[...], w_ref[...]).astype(BF16)

    return pl.pallas_call(
        body, name=name, grid=(L // tm,),
        in_specs=[pl.BlockSpec((tm, D), lambda i: (i, 0)), pl.BlockSpec((1, D), lambda i: (0, 0))],
        out_specs=pl.BlockSpec((tm, D), lambda i: (i, 0)),
        out_shape=jax.ShapeDtypeStruct((L, D), BF16),
        compiler_params=_cp(("parallel",)),
    )(x, w)


def _rms_bwd(x, w, dh, dres, name):
    L, D = x.shape
    tm = min(L, 256)

    def body(x_ref, w_ref, dh_ref, dres_ref, dx_ref, dw_ref):
        _, vjp = jax.vjp(_rms, x_ref[...], w_ref[...])
        dx, dw = vjp(dh_ref[...])
        dx_ref[...] = dx + dres_ref[...]

        @pl.when(pl.program_id(0) == 0)
        def _():
            dw_ref[...] = jnp.zeros_like(dw_ref)

        dw_ref[...] += dw

    row = pl.BlockSpec((tm, D), lambda i: (i, 0))
    vec = pl.BlockSpec((1, D), lambda i: (0, 0))
    return pl.pallas_call(
        body, name=name, grid=(L // tm,),
        in_specs=[row, vec, row, row],
        out_specs=[row, vec],
        out_shape=[jax.ShapeDtypeStruct((L, D), F32), jax.ShapeDtypeStruct((1, D), F32)],
        compiler_params=_cp(("arbitrary",)),
    )(x, w, dh, dres)


def _final(x, fw, tgt, name):
    L, D = x.shape
    tm = min(L, 256)

    def loss_fn(xv, wv, tv):
        err = _rms(xv, wv) - tv
        return jnp.sum(err * err) * (0.5 / D)

    def body(x_ref, w_ref, t_ref, loss_ref, dx_ref, dw_ref):
        tv = t_ref[...]
        val, vjp = jax.vjp(lambda a, b: loss_fn(a, b, tv), x_ref[...], w_ref[...])
        dx, dw = vjp(jnp.ones((), F32))
        dx_ref[...] = dx

        @pl.when(pl.program_id(0) == 0)
        def _():
            dw_ref[...] = jnp.zeros_like(dw_ref)
            loss_ref[...] = jnp.zeros_like(loss_ref)

        dw_ref[...] += dw
        loss_ref[...] += jnp.full(loss_ref.shape, val, F32)

    row = pl.BlockSpec((tm, D), lambda i: (i, 0))
    vec = pl.BlockSpec((1, D), lambda i: (0, 0))
    return pl.pallas_call(
        body, name=name, grid=(L // tm,),
        in_specs=[row, vec, row],
        out_specs=[pl.BlockSpec((8, 128), lambda i: (0, 0)), row, vec],
        out_shape=[jax.ShapeDtypeStruct((8, 128), F32), jax.ShapeDtypeStruct((L, D), F32),
                   jax.ShapeDtypeStruct((1, D), F32)],
        compiler_params=_cp(("arbitrary",)),
    )(x, fw, tgt)


def _s5_param_fn(a_re, a_im, log_dt, bt_re, bt_im):
    dt = jnp.exp(log_dt)
    zr, zi = a_re * dt, a_im * dt
    er = jnp.exp(zr)
    lr, li = er * jnp.cos(zi), er * jnp.sin(zi)
    nr, ni = lr - 1.0, li
    den = a_re * a_re + a_im * a_im
    cr = (nr * a_re + ni * a_im) / den
    ci = (ni * a_re - nr * a_im) / den
    bbr = cr[None] * bt_re - ci[None] * bt_im
    bbi = cr[None] * bt_im + ci[None] * bt_re
    return lr, li, bbr, bbi


def _s5_params_fwd(a_re, a_im, log_dt, bt_re, bt_im, name):
    def body(ar, ai, ld, br, bi, lr, li, bbr, bbi):
        o = _s5_param_fn(ar[...], ai[...], ld[...], br[...], bi[...])
        lr[...], li[...], bbr[...], bbi[...] = o

    gp = jax.ShapeDtypeStruct(a_re.shape, F32)
    cgp = jax.ShapeDtypeStruct(bt_re.shape, F32)
    return pl.pallas_call(body, name=name, out_shape=[gp, gp, cgp, cgp])(a_re, a_im, log_dt, bt_re, bt_im)


def _s5_params_bwd(a_re, a_im, log_dt, bt_re, bt_im, dlr, dli, dbbr, dbbi, name):
    def body(ar, ai, ld, br, bi, g0, g1, g2, g3, o0, o1, o2, o3, o4):
        _, vjp = jax.vjp(_s5_param_fn, ar[...], ai[...], ld[...], br[...], bi[...])
        o0[...], o1[...], o2[...], o3[...], o4[...] = vjp((g0[...], g1[...], g2[...], g3[...]))

    gp = jax.ShapeDtypeStruct(a_re.shape, F32)
    cgp = jax.ShapeDtypeStruct(bt_re.shape, F32)
    return pl.pallas_call(body, name=name,
                          out_shape=[gp, gp, jax.ShapeDtypeStruct(log_dt.shape, F32), cgp, cgp])(
        a_re, a_im, log_dt, bt_re, bt_im, dlr, dli, dbbr, dbbi)


def _cmul(ar, ai, br, bi):
    return ar * br - ai * bi, ar * bi + ai * br


def _cpow(lr, li, n):
    rr, ri = None, None
    br, bi = lr, li
    while n:
        if n & 1:
            rr, ri = (br, bi) if rr is None else _cmul(rr, ri, br, bi)
        n >>= 1
        if n:
            br, bi = _cmul(br, bi, br, bi)
    return rr, ri


def _shift_rows(x, up):
    row = lax.broadcasted_iota(jnp.int32, x.shape, 0)
    if up:
        return jnp.where(row == SUB - 1, 0.0, pltpu.roll(x, SUB - 1, 0))
    return jnp.where(row == 0, 0.0, pltpu.roll(x, 1, 0))


NT = SLAB_ST // 128


def _lam_tiles(lr_ref, li_ref):
    return [(lr_ref[:, j * 128:(j + 1) * 128], li_ref[:, j * 128:(j + 1) * 128]) for j in range(NT)]


def _row_on_sublanes(ref, j, t):
    return ref[j, pl.ds(t, SUB, stride=0), :]


def _pow_table(pw_re, pw_im, lam_t, seg):
    assert seg % 8 == 0 and (seg // 8) & (seg // 8 - 1) == 0
    for j in range(NT):
        lr, li = lam_t[j][0][0:1], lam_t[j][1][0:1]
        r, i_ = lr, li
        for row in range(8):
            pw_re[j, row:row + 1, :] = r
            pw_im[j, row:row + 1, :] = i_
            if row < 7:
                r, i_ = _cmul(r, i_, lr, li)
        n = 8
        while n < seg:
            qr, qi = _cpow(lr, li, n)
            nr, ni = _cmul(pw_re[j, 0:n, :], pw_im[j, 0:n, :], qr, qi)
            pw_re[j, n:2 * n, :] = nr
            pw_im[j, n:2 * n, :] = ni
            n *= 2


def _seg_scan(s_re, s_im, lam_t, pw_re, pw_im, seg, reverse, prev=None):
    sgn = -1.0 if reverse else 1.0
    lt = [(lr, sgn * li) for lr, li in lam_t]
    tiles = [(g, j) for g in range(N_GRP) for j in range(NT)]
    zeros = jnp.zeros((SUB, 128), F32)

    def rows(g, i):
        return pl.ds(pl.multiple_of((g * seg + i) * SUB, SUB), SUB)

    def step1(t, carry):
        i = seg - 1 - t if reverse else t
        out = []
        for n, (g, j) in enumerate(tiles):
            nr, ni = _cmul(lt[j][0], lt[j][1], carry[2 * n], carry[2 * n + 1])
            nr = nr + s_re[j, rows(g, i), :]
            ni = ni + s_im[j, rows(g, i), :]
            s_re[j, rows(g, i), :] = nr
            s_im[j, rows(g, i), :] = ni
            out += [nr, ni]
        return tuple(out)

    zero = tuple(zeros for _ in range(2 * len(tiles)))
    ends = lax.fori_loop(0, seg, step1, zero, unroll=2)

    carries = [None] * (2 * len(tiles))
    row = lax.broadcasted_iota(jnp.int32, (SUB, 128), 0)
    dist = (SUB - 1 - row) if reverse else row
    edge = 0 if reverse else SUB - 1
    for j in range(NT):
        pr, pi = _cpow(lt[j][0], lt[j][1], seg)
        qr, qi = jnp.ones((SUB, 128), F32), zeros
        for s in range(1, SUB):
            tr, ti = _cmul(qr, qi, pr, pi)
            qr, qi = jnp.where(dist >= s, tr, qr), jnp.where(dist >= s, ti, qi)
        boundary = None
        for g in (reversed(range(N_GRP)) if reverse else range(N_GRP)):
            n = g * NT + j
            cr, ci = zeros, zeros
            for _ in range(SUB - 1):
                tr, ti = _cmul(pr, pi, cr, ci)
                cr = _shift_rows(tr + ends[2 * n], reverse)
                ci = _shift_rows(ti + ends[2 * n + 1], reverse)
            if boundary is not None:
                tr, ti = _cmul(qr, qi, boundary[0], boundary[1])
                cr, ci = cr + tr, ci + ti
            carries[2 * n], carries[2 * n + 1] = cr, ci
            fr, fi = _cmul(pr, pi, cr, ci)
            boundary = (jnp.broadcast_to((fr + ends[2 * n])[edge:edge + 1], (SUB, 128)),
                        jnp.broadcast_to((fi + ends[2 * n + 1])[edge:edge + 1], (SUB, 128)))

    def fix(t, i, acc, before):
        out = []
        pws = [(_row_on_sublanes(pw_re, j, t), sgn * _row_on_sublanes(pw_im, j, t)) for j in range(NT)]
        for n, (g, j) in enumerate(tiles):
            ar, ai = _cmul(pws[j][0], pws[j][1], carries[2 * n], carries[2 * n + 1])
            ar = ar + s_re[j, rows(g, i), :]
            ai = ai + s_im[j, rows(g, i), :]
            s_re[j, rows(g, i), :] = ar
            s_im[j, rows(g, i), :] = ai
            if before is not None:
                qr, qi = before(n)
                out += [acc[2 * n] + ar * qr + ai * qi, acc[2 * n + 1] + ai * qr - ar * qi]
        return tuple(out)

    if prev is None:
        lax.fori_loop(0, seg, lambda t, c: fix(t, seg - 1 - t if reverse else t, c, None), (), unroll=2)
        return carries
    assert reverse
    p_re, p_im, p_carries = prev

    def earlier(t):
        return lambda n: (p_re[tiles[n][1], rows(tiles[n][0], seg - 2 - t), :],
                          p_im[tiles[n][1], rows(tiles[n][0], seg - 2 - t), :])

    acc = lax.fori_loop(0, seg - 1, lambda t, c: fix(t, seg - 1 - t, c, earlier(t)), zero)
    acc = fix(seg - 1, 0, acc, lambda n: (p_carries[2 * n], p_carries[2 * n + 1]))
    return carries, [sum(acc[2 * (g * NT + j) + part] for g in range(N_GRP)) for j in range(NT) for part in range(2)]


S5_RB = 256


def _seg_slice(k, seg):
    g, r = divmod(k, SUB)
    return pl.ds(g * seg * SUB + r, seg, stride=SUB)


def _to_step_major(src_ref, dst_ref, seg):
    for k in range(N_SEG):
        dst_ref[_seg_slice(k, seg), :] = src_ref[pl.ds(k * seg, seg), :].astype(F32)


def _from_step_major(src_ref, dst_ref, seg):
    for k in range(N_SEG):
        dst_ref[pl.ds(k * seg, seg), :] = src_ref[_seg_slice(k, seg), :].astype(dst_ref.dtype)


def _blocks(L):
    rb = min(S5_RB, L)
    return [pl.ds(b * rb, rb) for b in range(L // rb)]


def _lanes_of(ref, rows):
    return jnp.concatenate([ref[j, rows, :] for j in range(NT)], axis=-1)


def _lanes_to(ref, rows, val):
    for j in range(NT):
        ref[j, rows, :] = val[:, j * 128:(j + 1) * 128]


def _s5_specs(L):
    col = lambda off: pl.BlockSpec((L, SLAB_CH), lambda j: (0, off + j))
    mat_b = pl.BlockSpec((None, SLAB_CH, SLAB_ST), lambda j: (j, 0, 0))
    mat_c = pl.BlockSpec((None, SLAB_ST, SLAB_CH), lambda j: (j, 0, 0))
    vec_s = pl.BlockSpec((None, SUB, SLAB_ST), lambda j: (j, 0, 0))
    vec_c = pl.BlockSpec((None, 1, SLAB_CH), lambda j: (j, 0, 0))
    return col, mat_b, mat_c, vec_s, vec_c


def _s5_states(u_ref, u_sm, bre_ref, bim_ref, lam_t, pw_re, pw_im, s_re, s_im, seg):
    _pow_table(pw_re, pw_im, lam_t, seg)
    _to_step_major(u_ref, u_sm, seg)
    for rows in _blocks(u_sm.shape[0]):
        ub = u_sm[rows, :].astype(BF16)
        _lanes_to(s_re, rows, _dot(ub, bre_ref[...]))
        _lanes_to(s_im, rows, _dot(ub, bim_ref[...]))
    return _seg_scan(s_re, s_im, lam_t, pw_re, pw_im, seg, reverse=False)


def _s5_fwd(proj, bre, bim, cre_t, cim_t, lam_re, lam_im, dvec, name):
    L = proj.shape[0]
    seg = L // N_SEG
    col, mat_b, mat_c, vec_s, vec_c = _s5_specs(L)

    def body(u_ref, bre_ref, bim_ref, cre_ref, cim_ref, lr_ref, li_ref, d_ref, y_ref, s_re, s_im, pw_re, pw_im, u_sm, y_sm):
        _s5_states(u_ref, u_sm, bre_ref, bim_ref, _lam_tiles(lr_ref, li_ref), pw_re, pw_im, s_re, s_im, seg)
        for rows in _blocks(L):
            y = (_dot(_lanes_of(s_re, rows).astype(BF16), cre_ref[...])
                 - _dot(_lanes_of(s_im, rows).astype(BF16), cim_ref[...]))
            y_sm[rows, :] = jax.nn.gelu(y + d_ref[...] * u_sm[rows, :])
        _from_step_major(y_sm, y_ref, seg)

    lane_tile = pltpu.VMEM((L, SLAB_CH), F32)
    return pl.pallas_call(
        body, name=name, grid=(N_SLAB,),
        in_specs=[col(OFF_UA // SLAB_CH), mat_b, mat_b, mat_c, mat_c, vec_s, vec_s, vec_c],
        out_specs=pl.BlockSpec((L, SLAB_CH), lambda j: (0, j)),
        out_shape=jax.ShapeDtypeStruct((L, SSM_WIDTH), BF16),
        scratch_shapes=[pltpu.VMEM((NT, L, 128), F32)] * 2 + [pltpu.VMEM((NT, seg, 128), F32)] * 2 + [lane_tile] * 2,
        compiler_params=_cp(("parallel",)),
    )(proj, bre, bim, cre_t, cim_t, lam_re, lam_im, dvec)


def _s5_bwd(proj, dy, bre, bim, cre_t, cim_t, lam_re, lam_im, dvec, name):
    L = proj.shape[0]
    seg = L // N_SEG
    col, mat_b, mat_c, vec_s, vec_c = _s5_specs(L)
    dlam_spec = pl.BlockSpec((None, 1, SLAB_ST), lambda j: (j, 0, 0))

    def body(u_ref, dy_ref, bre_ref, bim_ref, cre_ref, cim_ref, lr_ref, li_ref, d_ref,
             du_ref, dbre_ref, dbim_ref, dcre_ref, dcim_ref, dlr_ref, dli_ref, dd_ref,
             s_re, s_im, a_re, a_im, pw_re, pw_im, u_sm, dyp, io_sm):
        lam_t = _lam_tiles(lr_ref, li_ref)
        carry_s = _s5_states(u_ref, u_sm, bre_ref, bim_ref, lam_t, pw_re, pw_im, s_re, s_im, seg)
        _to_step_major(dy_ref, io_sm, seg)
        dcre = jnp.zeros((SLAB_ST, SLAB_CH), F32)
        dcim = jnp.zeros((SLAB_ST, SLAB_CH), F32)
        dd = jnp.zeros((1, SLAB_CH), F32)
        for rows in _blocks(L):
            sre = _lanes_of(s_re, rows).astype(BF16)
            sim = _lanes_of(s_im, rows).astype(BF16)
            uk = u_sm[rows, :]
            ypre = _dot(sre, cre_ref[...]) - _dot(sim, cim_ref[...]) + d_ref[...] * uk
            _, vjp = jax.vjp(jax.nn.gelu, ypre)
            (dyk,) = vjp(io_sm[rows, :])
            dyp[rows, :] = dyk
            dd = dd + jnp.sum(dyk * uk, axis=0, keepdims=True)
            dyb = dyk.astype(BF16)
            dcre = dcre + _dot_tn(sre, dyb)
            dcim = dcim - _dot_tn(sim, dyb)
            _lanes_to(a_re, rows, _dot_nt(dyb, cre_ref[...]))
            _lanes_to(a_im, rows, -_dot_nt(dyb, cim_ref[...]))
        dcre_ref[...] = dcre
        dcim_ref[...] = dcim
        dd_ref[...] = dd

        _, acc = _seg_scan(a_re, a_im, lam_t, pw_re, pw_im, seg, reverse=True, prev=(s_re, s_im, carry_s))
        dlr_ref[...] = jnp.concatenate([jnp.sum(acc[2 * j], axis=0, keepdims=True) for j in range(NT)], axis=-1)
        dli_ref[...] = jnp.concatenate([jnp.sum(acc[2 * j + 1], axis=0, keepdims=True) for j in range(NT)], axis=-1)

        dbre = jnp.zeros((SLAB_CH, SLAB_ST), F32)
        dbim = jnp.zeros((SLAB_CH, SLAB_ST), F32)
        for rows in _blocks(L):
            are = _lanes_of(a_re, rows).astype(BF16)
            aim = _lanes_of(a_im, rows).astype(BF16)
            ub = u_sm[rows, :].astype(BF16)
            io_sm[rows, :] = _dot_nt(are, bre_ref[...]) + _dot_nt(aim, bim_ref[...]) + dyp[rows, :] * d_ref[...]
            dbre = dbre + _dot_tn(ub, are)
            dbim = dbim + _dot_tn(ub, aim)
        _from_step_major(io_sm, du_ref, seg)
        dbre_ref[...] = dbre
        dbim_ref[...] = dbim

    scan_buf = pltpu.VMEM((NT, L, 128), F32)
    pow_buf = pltpu.VMEM((NT, seg, 128), F32)
    lane_tile = pltpu.VMEM((L, SLAB_CH), F32)
    return pl.pallas_call(
        body, name=name, grid=(N_SLAB,),
        in_specs=[col(OFF_UA // SLAB_CH), pl.BlockSpec((L, SLAB_CH), lambda j: (0, j)),
                  mat_b, mat_b, mat_c, mat_c, vec_s, vec_s, vec_c],
        out_specs=[pl.BlockSpec((L, SLAB_CH), lambda j: (0, j)), mat_b, mat_b, mat_c, mat_c, dlam_spec, dlam_spec, vec_c],
        out_shape=[jax.ShapeDtypeStruct((L, SSM_WIDTH), BF16),
                   jax.ShapeDtypeStruct((N_SLAB, SLAB_CH, SLAB_ST), F32),
                   jax.ShapeDtypeStruct((N_SLAB, SLAB_CH, SLAB_ST), F32),
                   jax.ShapeDtypeStruct((N_SLAB, SLAB_ST, SLAB_CH), F32),
                   jax.ShapeDtypeStruct((N_SLAB, SLAB_ST, SLAB_CH), F32),
                   jax.ShapeDtypeStruct((N_SLAB, 1, SLAB_ST), F32),
                   jax.ShapeDtypeStruct((N_SLAB, 1, SLAB_ST), F32),
                   jax.ShapeDtypeStruct((N_SLAB, 1, SLAB_CH), F32)],
        scratch_shapes=[scan_buf, scan_buf, scan_buf, scan_buf, pow_buf, pow_buf, lane_tile, lane_tile, lane_tile],
        compiler_params=_cp(("parallel",)),
    )(proj, dy, bre, bim, cre_t, cim_t, lam_re, lam_im, dvec)


def _glu_point(y0, pre, za, b):
    return y0 * jax.nn.sigmoid(pre + b) * jax.nn.silu(za)


def _glu_specs(L, tm):
    row = pl.BlockSpec((tm, SSM_WIDTH), lambda i: (i, 0))
    za = pl.BlockSpec((tm, SSM_WIDTH), lambda i: (i, OFF_ZA // SSM_WIDTH))
    wmat = pl.BlockSpec((SSM_WIDTH, SSM_WIDTH), lambda i: (0, 0))
    vec = pl.BlockSpec((1, SSM_WIDTH), lambda i: (0, 0))
    return row, za, wmat, vec


def _glu_fwd(ya0, proj, w, b, name):
    L = ya0.shape[0]
    tm = min(L, 512)
    row, za, wmat, vec = _glu_specs(L, tm)

    def body(y_ref, z_ref, w_ref, b_ref, o_ref):
        y0 = y_ref[...]
        pre = _dot(y0, w_ref[...])
        o_ref[...] = _glu_point(y0.astype(F32), pre, z_ref[...].astype(F32), b_ref[...]).astype(BF16)

    return pl.pallas_call(
        body, name=name, grid=(L // tm,), in_specs=[row, za, wmat, vec], out_specs=row,
        out_shape=jax.ShapeDtypeStruct((L, SSM_WIDTH), BF16), compiler_params=_cp(("parallel",)),
    )(ya0, proj, w, b)


def _glu_bwd(ya0, proj, w, b, dya, name):
    L = ya0.shape[0]
    tm = min(L, 512)
    row, za, wmat, vec = _glu_specs(L, tm)

    def body(y_ref, z_ref, w_ref, b_ref, g_ref, dy0_ref, dza_ref, dw_ref, db_ref):
        y0 = y_ref[...]
        pre = _dot(y0, w_ref[...])
        _, vjp = jax.vjp(_glu_point, y0.astype(F32), pre, z_ref[...].astype(F32), b_ref[...])
        dy0, dpre, dza, db = vjp(g_ref[...].astype(F32))
        dpb = dpre.astype(BF16)
        dy0_ref[...] = (dy0 + _dot_nt(dpb, w_ref[...])).astype(BF16)
        dza_ref[...] = dza.astype(BF16)

        @pl.when(pl.program_id(0) == 0)
        def _():
            dw_ref[...] = jnp.zeros_like(dw_ref)
            db_ref[...] = jnp.zeros_like(db_ref)

        dw_ref[...] += _dot_tn(y0, dpb)
        db_ref[...] += db

    return pl.pallas_call(
        body, name=name, grid=(L // tm,), in_specs=[row, za, wmat, vec, row],
        out_specs=[row, row, wmat, vec],
        out_shape=[jax.ShapeDtypeStruct((L, SSM_WIDTH), BF16), jax.ShapeDtypeStruct((L, SSM_WIDTH), BF16),
                   jax.ShapeDtypeStruct((SSM_WIDTH, SSM_WIDTH), F32), jax.ShapeDtypeStruct((1, SSM_WIDTH), F32)],
        compiler_params=_cp(("arbitrary",)),
    )(ya0, proj, w, b, dya)


def _sg_norm(vb, ln_w, ln_b):
    v0 = jax.nn.gelu(vb)
    mu = jnp.mean(v0, axis=-1, keepdims=True)
    var = jnp.mean(jnp.square(v0 - mu), axis=-1, keepdims=True)
    return (v0 - mu) * lax.rsqrt(var + EPS) * ln_w + ln_b


def _sg_gate(ub, mixed, zb):
    return jax.nn.gelu(ub) * mixed * jax.nn.silu(zb)


def _sg_specs():
    W = SSM_WIDTH
    blk = lambda off: pl.BlockSpec((CHUNK, W), lambda n: (n, off // W))
    out = pl.BlockSpec((CHUNK, W), lambda n: (n, 0))
    vec = pl.BlockSpec((1, W), lambda n: (0, 0))
    wsp = pl.BlockSpec((SG_HEADS, CHUNK, CHUNK), lambda n: (0, 0, 0))
    bsp = pl.BlockSpec((SG_HEADS, CHUNK, 1), lambda n: (0, 0, 0))
    return blk, out, vec, wsp, bsp


def _sg_masked(w_ref):
    t = lax.broadcasted_iota(jnp.int32, (CHUNK, CHUNK), 0)
    s = lax.broadcasted_iota(jnp.int32, (CHUNK, CHUNK), 1)
    causal = s <= t
    return causal, [jnp.where(causal, w_ref[h], 0.0).astype(BF16) for h in range(SG_HEADS)]


def _sg_mix(wm, vnb, bias_ref):
    return jnp.concatenate(
        [_dot(wm[h], vnb[:, h * CHUNK:(h + 1) * CHUNK]) + bias_ref[h] for h in range(SG_HEADS)], axis=-1)


def _sg_fwd(proj, ln_w, ln_b, w, bias, name):
    L = proj.shape[0]
    blk, out, vec, wsp, bsp = _sg_specs()

    def body(ub_ref, vb_ref, zb_ref, lw_ref, lb_ref, w_ref, bias_ref, o_ref):
        _, wm = _sg_masked(w_ref)
        vnb = _sg_norm(vb_ref[...].astype(F32), lw_ref[...], lb_ref[...]).astype(BF16)
        mixed = _sg_mix(wm, vnb, bias_ref)
        o_ref[...] = _sg_gate(ub_ref[...].astype(F32), mixed, zb_ref[...].astype(F32)).astype(BF16)

    return pl.pallas_call(
        body, name=name, grid=(L // CHUNK,),
        in_specs=[blk(OFF_UB), blk(OFF_VB), blk(OFF_ZB), vec, vec, wsp, bsp], out_specs=out,
        out_shape=jax.ShapeDtypeStruct((L, SSM_WIDTH), BF16), compiler_params=_cp(("parallel",)),
    )(proj, proj, proj, ln_w, ln_b, w, bias)


def _sg_bwd(proj, ln_w, ln_b, w, bias, dyb, name):
    L = proj.shape[0]
    blk, out, vec, wsp, bsp = _sg_specs()

    def body(ub_ref, vb_ref, zb_ref, lw_ref, lb_ref, w_ref, bias_ref, g_ref,
             dub_ref, dvb_ref, dzb_ref, dlw_ref, dlb_ref, dw_ref, dbias_ref):
        causal, wm = _sg_masked(w_ref)
        vb = vb_ref[...].astype(F32)
        vn, vjp_norm = jax.vjp(_sg_norm, vb, lw_ref[...], lb_ref[...])
        vnb = vn.astype(BF16)
        mixed = _sg_mix(wm, vnb, bias_ref)
        _, vjp_gate = jax.vjp(_sg_gate, ub_ref[...].astype(F32), mixed, zb_ref[...].astype(F32))
        dub, dmixed, dzb = vjp_gate(g_ref[...].astype(F32))
        dub_ref[...] = dub.astype(BF16)
        dzb_ref[...] = dzb.astype(BF16)

        @pl.when(pl.program_id(0) == 0)
        def _():
            dlw_ref[...] = jnp.zeros_like(dlw_ref)
            dlb_ref[...] = jnp.zeros_like(dlb_ref)
            dw_ref[...] = jnp.zeros_like(dw_ref)
            dbias_ref[...] = jnp.zeros_like(dbias_ref)

        dvn = []
        for h in range(SG_HEADS):
            dm = dmixed[:, h * CHUNK:(h + 1) * CHUNK]
            dmb = dm.astype(BF16)
            dbias_ref[h] += jnp.sum(dm, axis=-1, keepdims=True)
            dw_ref[h] += jnp.where(causal, _dot_nt(dmb, vnb[:, h * CHUNK:(h + 1) * CHUNK]), 0.0)
            dvn.append(_dot_tn(wm[h], dmb))
        dvb, dlw, dlb = vjp_norm(jnp.concatenate(dvn, axis=-1))
        dvb_ref[...] = dvb.astype(BF16)
        dlw_ref[...] += dlw
        dlb_ref[...] += dlb

    act = jax.ShapeDtypeStruct((L, SSM_WIDTH), BF16)
    return pl.pallas_call(
        body, name=name, grid=(L // CHUNK,),
        in_specs=[blk(OFF_UB), blk(OFF_VB), blk(OFF_ZB), vec, vec, wsp, bsp, out],
        out_specs=[out, out, out, vec, vec, wsp, bsp],
        out_shape=[act, act, act, jax.ShapeDtypeStruct((1, SSM_WIDTH), F32), jax.ShapeDtypeStruct((1, SSM_WIDTH), F32),
                   jax.ShapeDtypeStruct((SG_HEADS, CHUNK, CHUNK), F32), jax.ShapeDtypeStruct((SG_HEADS, CHUNK, 1), F32)],
        compiler_params=_cp(("arbitrary",)),
    )(proj, proj, proj, ln_w, ln_b, w, bias, dyb)


def _rope_tables(L):
    half = ROT_DIM // 2
    inv_freq = ROPE_THETA ** (-jnp.arange(0, ROT_DIM, 2, dtype=F32) / ROT_DIM)
    ang = jnp.arange(L, dtype=F32)[:, None] * inv_freq[None, :]
    cos, sin = jnp.cos(ang), jnp.sin(ang)
    ones = jnp.ones((L, HEAD_DIM - ROT_DIM), F32)
    cos_h = jnp.concatenate([cos, cos, ones], axis=-1)
    sin_h = jnp.concatenate([-sin, sin, 0.0 * ones], axis=-1)
    src = jnp.arange(HEAD_DIM)[:, None]
    dst = jnp.arange(HEAD_DIM)[None, :]
    p_h = (((dst < half) & (src == dst + half)) | ((dst >= half) & (dst < ROT_DIM) & (src == dst - half))).astype(F32)
    p2 = jnp.kron(jnp.eye(2, dtype=F32), p_h).astype(BF16)
    return jnp.tile(cos_h, (1, 2)), jnp.tile(sin_h, (1, 2)), p2


def _rope(t, cos, sin, p2):
    n = t.shape[1] // 128
    tb = t.astype(BF16)
    sw = jnp.concatenate([_dot(tb[:, i * 128:(i + 1) * 128], p2) for i in range(n)], axis=-1) if n > 1 else _dot(tb, p2)
    return t * jnp.tile(cos, (1, n)) + sw * jnp.tile(sin, (1, n))


def _rope_t(g, cos, sin, p2):
    n = g.shape[1] // 128
    gs = (g * jnp.tile(sin, (1, n))).astype(BF16)
    sw = jnp.concatenate([_dot_nt(gs[:, i * 128:(i + 1) * 128], p2) for i in range(n)], axis=-1) if n > 1 else _dot_nt(gs, p2)
    return g * jnp.tile(cos, (1, n)) + sw


def _lane_lo(shape):
    return (lax.broadcasted_iota(jnp.int32, shape, len(shape) - 1) % 128) < HEAD_DIM


def _dup_halves(x):
    xr = pltpu.roll(x, HEAD_DIM, 1)
    lo = _lane_lo(x.shape)
    return jnp.where(lo, x, xr), jnp.where(lo, xr, x)


def _fold_halves(d0, d1):
    f0 = d0 + pltpu.roll(d0, HEAD_DIM, 1)
    f1 = d1 + pltpu.roll(d1, HEAD_DIM, 1)
    return jnp.where(_lane_lo(d0.shape), f0, f1)


def _attn_mask():
    qi = lax.broadcasted_iota(jnp.int32, (CHUNK, 2 * CHUNK), 0)
    kj = lax.broadcasted_iota(jnp.int32, (CHUNK, 2 * CHUNK), 1)
    return qi, kj


def _attn_specs():
    qsp = pl.BlockSpec((CHUNK, 1024), lambda n: (n, OFF_Q // 1024))
    kv_cur = pl.BlockSpec((CHUNK, 256), lambda n: (n, OFF_KV // 256))
    kv_prev = pl.BlockSpec((CHUNK, 256), lambda n: (jnp.maximum(n - 1, 0), OFF_KV // 256))
    zsp = [pl.BlockSpec((CHUNK, 256), functools.partial(lambda n, q: (n, OFF_ZC // 256 + q), q=q)) for q in range(4)]
    tab_cur = pl.BlockSpec((CHUNK, 128), lambda n: (n, 0))
    tab_prev = pl.BlockSpec((CHUNK, 128), lambda n: (jnp.maximum(n - 1, 0), 0))
    p2sp = pl.BlockSpec((128, 128), lambda n: (0, 0))
    sink = pl.BlockSpec(memory_space=pltpu.SMEM)
    wide = pl.BlockSpec((CHUNK, 1024), lambda n: (n, 0))
    return qsp, kv_cur, kv_prev, zsp, tab_cur, tab_prev, p2sp, sink, wide


def _attn_prep(n, q_ref, kvc_ref, kvp_ref, cosc_ref, sinc_ref, cosp_ref, sinp_ref, p2_ref):
    p2 = p2_ref[...]
    qr = _rope(q_ref[...].astype(F32), cosc_ref[...], sinc_ref[...], p2).astype(BF16)
    kc = _rope(kvc_ref[:, 0:128].astype(F32), cosc_ref[...], sinc_ref[...], p2)
    kp = _rope(kvp_ref[:, 0:128].astype(F32), cosp_ref[...], sinp_ref[...], p2)
    k_all = jnp.concatenate([kp, kc], axis=0).astype(BF16)
    v_all = jnp.concatenate([kvp_ref[:, 128:256], kvc_ref[:, 128:256]], axis=0)
    qi, kj = _attn_mask()
    allowed = ((kj < CHUNK) & (kj > qi) & (n > 0)) | ((kj >= CHUNK) & (kj - CHUNK <= qi))
    return qr, _dup_halves(k_all), _dup_halves(v_all), allowed, _lane_lo((CHUNK, 128))


def _attn_head(qr, kd, sink_ref, h, allowed, lo):
    m, half, g = h // 2, h % 2, h // 8
    qp = qr[:, m * 128:(m + 1) * 128]
    qm = jnp.where(lo if half == 0 else ~lo, qp, jnp.zeros_like(qp))
    s = jnp.where(allowed, _dot_nt(qm, kd[g]) * (HEAD_DIM ** -0.5), NEG_INF)
    snk = sink_ref[h]
    mx = jnp.maximum(jnp.max(s, axis=-1, keepdims=True), snk)
    e = jnp.exp(s - mx)
    es = jnp.exp(snk - mx)
    inv = 1.0 / (jnp.sum(e, axis=-1, keepdims=True) + es)
    return qm, e * inv, es * inv


def _silu_gate(o, z):
    return o * jax.nn.silu(z)


def _pair_lanes(refs, m):
    return refs[m // 2][:, (m % 2) * 128:(m % 2 + 1) * 128]


def _attn_fwd(proj, sinks, tabs, name):
    L = proj.shape[0]
    cos2, sin2, p2 = tabs
    qsp, kv_cur, kv_prev, zsp, tab_cur, tab_prev, p2sp, sink, wide = _attn_specs()

    def body(q_ref, kvc_ref, kvp_ref, z0, z1, z2, z3, cosc, sinc, cosp, sinp, p2_ref, sink_ref, y_ref, o_ref):
        n = pl.program_id(0)
        qr, kd, vd, allowed, lo = _attn_prep(n, q_ref, kvc_ref, kvp_ref, cosc, sinc, cosp, sinp, p2_ref)
        probs = [_attn_head(qr, kd, sink_ref, h, allowed, lo)[1].astype(BF16) for h in range(ATT_HEADS)]
        for m in range(ATT_HEADS // 2):
            g = m // 4
            o0 = _dot(probs[2 * m], vd[g])
            o1 = _dot(probs[2 * m + 1], vd[g])
            o = jnp.where(lo, o0, o1).astype(BF16)
            o_ref[:, m * 128:(m + 1) * 128] = o
            z = _pair_lanes((z0, z1, z2, z3), m).astype(F32)
            y_ref[:, m * 128:(m + 1) * 128] = _silu_gate(o.astype(F32), z).astype(BF16)

    act = jax.ShapeDtypeStruct((L, 1024), BF16)
    return pl.pallas_call(
        body, name=name, grid=(L // CHUNK,),
        in_specs=[qsp, kv_cur, kv_prev, *zsp, tab_cur, tab_cur, tab_prev, tab_prev, p2sp, sink],
        out_specs=[wide, wide], out_shape=[act, act], compiler_params=_cp(("parallel",)),
    )(proj, proj, proj, proj, proj, proj, proj, cos2, sin2, cos2, sin2, p2, sinks)


def _attn_bwd(proj, sinks, tabs, o_att, dyc, name):
    L = proj.shape[0]
    cos2, sin2, p2 = tabs
    qsp, kv_cur, kv_prev, zsp, tab_cur, tab_prev, p2sp, sink, wide = _attn_specs()
    kvo = pl.BlockSpec((CHUNK, 256), lambda n: (n, 0))

    def body(q_ref, kvc_ref, kvp_ref, z0, z1, z2, z3, cosc, sinc, cosp, sinp, p2_ref, sink_ref, o_ref, g_ref,
             dq_ref, dz_ref, dkvc_ref, dkvp_ref, dsink_ref):
        n = pl.program_id(0)
        qr, kd, vd, allowed, lo = _attn_prep(n, q_ref, kvc_ref, kvp_ref, cosc, sinc, cosp, sinp, p2_ref)
        p2 = p2_ref[...]

        @pl.when(n == 0)
        def _():
            dsink_ref[...] = jnp.zeros_like(dsink_ref)

        dkd = [jnp.zeros((2 * CHUNK, 128), F32), jnp.zeros((2 * CHUNK, 128), F32)]
        dvd = [jnp.zeros((2 * CHUNK, 128), F32), jnp.zeros((2 * CHUNK, 128), F32)]
        probs = [_attn_head(qr, kd, sink_ref, h, allowed, lo) for h in range(ATT_HEADS)]
        for m in range(ATT_HEADS // 2):
            g = m // 4
            lanes = slice(m * 128, (m + 1) * 128)
            z = _pair_lanes((z0, z1, z2, z3), m).astype(F32)
            _, vjp = jax.vjp(_silu_gate, o_ref[:, lanes].astype(F32), z)
            do, dz = vjp(g_ref[:, lanes].astype(F32))
            dz_ref[:, lanes] = dz.astype(BF16)
            dop = do.astype(BF16)
            dq_h = []
            for half in range(2):
                h = 2 * m + half
                qm, p, ps = probs[h]
                dom = jnp.where(lo if half == 0 else ~lo, dop, jnp.zeros_like(dop))
                dp = _dot_nt(dom, vd[g])
                rs = jnp.sum(p * dp, axis=-1, keepdims=True)
                ds = (p * (dp - rs) * (HEAD_DIM ** -0.5)).astype(BF16)
                dsink_ref[h:h + 1, :] += jnp.broadcast_to(jnp.sum(-ps * rs, axis=0, keepdims=True), (1, 128))
                dq_h.append(_dot(ds, kd[g]))
                dkd[g] = dkd[g] + _dot_tn(ds, qm)
                dvd[g] = dvd[g] + _dot_tn(p.astype(BF16), dom)
            dq_ref[:, lanes] = _rope_t(jnp.where(lo, dq_h[0], dq_h[1]), cosc[...], sinc[...], p2).astype(BF16)
        dk_rot = _fold_halves(dkd[0], dkd[1])
        dv = _fold_halves(dvd[0], dvd[1])
        dkp = _rope_t(dk_rot[0:CHUNK], cosp[...], sinp[...], p2)
        dkc = _rope_t(dk_rot[CHUNK:2 * CHUNK], cosc[...], sinc[...], p2)
        dkvp_ref[...] = jnp.concatenate([dkp, dv[0:CHUNK]], axis=-1)
        dkvc_ref[...] = jnp.concatenate([dkc, dv[CHUNK:2 * CHUNK]], axis=-1)

    act = jax.ShapeDtypeStruct((L, 1024), BF16)
    kvs = jax.ShapeDtypeStruct((L, 256), F32)
    return pl.pallas_call(
        body, name=name, grid=(L // CHUNK,),
        in_specs=[qsp, kv_cur, kv_prev, *zsp, tab_cur, tab_cur, tab_prev, tab_prev, p2sp, sink, wide, wide],
        out_specs=[wide, wide, kvo, kvo, pl.BlockSpec((ATT_HEADS, 128), lambda n: (0, 0))],
        out_shape=[act, act, kvs, kvs, jax.ShapeDtypeStruct((ATT_HEADS, 128), F32)],
        compiler_params=_cp(("arbitrary",)),
    )(proj, proj, proj, proj, proj, proj, proj, cos2, sin2, cos2, sin2, p2, sinks, o_att, dyc)


MERGE_TN = 256


def _merge_point(ta, tb, tc, ga, gb, gc):
    return jax.nn.sigmoid(ga) * ta + jax.nn.sigmoid(gb) * tb + jax.nn.sigmoid(gc) * tc


def _merge_specs(tm):
    nj = D_MODEL // MERGE_TN
    t = pl.BlockSpec((tm, MERGE_TN), lambda i, j: (i, j))
    gates = [pl.BlockSpec((tm, MERGE_TN), functools.partial(lambda i, j, b: (i, OFF_G // MERGE_TN + b * nj + j), b=b))
             for b in range(3)]
    return t, gates, nj


def _merge_fwd(ta, tb, tc, proj, name):
    L = ta.shape[0]
    tm = min(L, 1024)
    t, gates, nj = _merge_specs(tm)

    def body(ta_ref, tb_ref, tc_ref, ga_ref, gb_ref, gc_ref, o_ref):
        f = lambda r: r[...].astype(F32)
        o_ref[...] = _merge_point(f(ta_ref), f(tb_ref), f(tc_ref), f(ga_ref), f(gb_ref), f(gc_ref)).astype(BF16)

    return pl.pallas_call(
        body, name=name, grid=(L // tm, nj), in_specs=[t, t, t, *gates], out_specs=t,
        out_shape=jax.ShapeDtypeStruct((L, D_MODEL), BF16), compiler_params=_cp(("parallel", "parallel")),
    )(ta, tb, tc, proj, proj, proj)


def _merge_bwd(ta, tb, tc, proj, dm, name):
    L = ta.shape[0]
    tm = min(L, 1024)
    t, gates, nj = _merge_specs(tm)

    def body(ta_ref, tb_ref, tc_ref, ga_ref, gb_ref, gc_ref, dm_ref, dta_ref, dtb_ref, dtc_ref, dga_ref, dgb_ref, dgc_ref):
        f = lambda r: r[...].astype(F32)
        _, vjp = jax.vjp(_merge_point, f(ta_ref), f(tb_ref), f(tc_ref), f(ga_ref), f(gb_ref), f(gc_ref))
        outs = vjp(f(dm_ref))
        for r, v in zip((dta_ref, dtb_ref, dtc_ref, dga_ref, dgb_ref, dgc_ref), outs):
            r[...] = v.astype(BF16)

    act = jax.ShapeDtypeStruct((L, D_MODEL), BF16)
    return pl.pallas_call(
        body, name=name, grid=(L // tm, nj), in_specs=[t, t, t, *gates, t],
        out_specs=[t] * 6, out_shape=[act] * 6,
        compiler_params=_cp(("parallel", "parallel")),
    )(ta, tb, tc, proj, proj, proj, dm)


GRAD_DT = BF16
SMALL = ("norm_w", "ssm_a_re", "ssm_a_im", "ssm_log_dt", "ssm_b_re", "ssm_b_im", "ssm_c_re", "ssm_c_im", "ssm_d",
         "ssm_glu_b", "sg_ln_w", "sg_ln_b", "sg_w", "sg_b", "attn_sinks")
G8 = SSM_GROUPS // N_SLAB


def _diag_mask(rows_per_group, cols_per_group):
    r = jnp.arange(G8 * rows_per_group)[:, None] // rows_per_group
    c = jnp.arange(G8 * cols_per_group)[None, :] // cols_per_group
    return r == c


def _slab_b(bb_t):
    x = bb_t.transpose(1, 0, 2).reshape(N_SLAB, SLAB_CH, SSM_STATE)
    return jnp.where(_diag_mask(SSM_GROUP, SSM_STATE), jnp.tile(x, (1, 1, G8)), 0)


def _unslab_b(d):
    x = jnp.where(_diag_mask(SSM_GROUP, SSM_STATE), d, 0).reshape(N_SLAB, SLAB_CH, G8, SSM_STATE).sum(axis=2)
    return x.reshape(SSM_GROUPS, SSM_GROUP, SSM_STATE).transpose(1, 0, 2)


def _slab_c(c):
    x = c.transpose(0, 2, 1).reshape(N_SLAB, SLAB_ST, SSM_GROUP)
    return jnp.where(_diag_mask(SSM_STATE, SSM_GROUP), jnp.tile(x, (1, 1, G8)), 0)


def _unslab_c(d):
    x = jnp.where(_diag_mask(SSM_STATE, SSM_GROUP), d, 0).reshape(N_SLAB, SLAB_ST, G8, SSM_GROUP).sum(axis=2)
    return x.reshape(SSM_GROUPS, SSM_STATE, SSM_GROUP).transpose(0, 2, 1)


def _s5_prep(p, tag):
    bt_re = p["ssm_b_re"].transpose(2, 0, 1)
    bt_im = p["ssm_b_im"].transpose(2, 0, 1)
    raw = (p["ssm_a_re"], p["ssm_a_im"], p["ssm_log_dt"][:, None], bt_re, bt_im)
    lr, li, bbr, bbi = _s5_params_fwd(*raw, name=f"s5_params_{tag}")
    ops = (_slab_b(bbr).astype(BF16), _slab_b(bbi).astype(BF16),
           _slab_c(p["ssm_c_re"]).astype(BF16), _slab_c(p["ssm_c_im"]).astype(BF16),
           jnp.broadcast_to(lr.reshape(N_SLAB, 1, SLAB_ST), (N_SLAB, SUB, SLAB_ST)),
           jnp.broadcast_to(li.reshape(N_SLAB, 1, SLAB_ST), (N_SLAB, SUB, SLAB_ST)),
           p["ssm_d"].reshape(N_SLAB, 1, SLAB_CH))
    return raw, ops


def _layer_fwd(x, p, w, tabs, tag, s5=None, proj_of=None, after_proj=None):
    L = x.shape[0]
    h = _rms_fwd(x, p["norm_w"][None], f"rms_fwd_{tag}")
    if proj_of is not None:
        proj = proj_of(h)
    else:
        proj = _mm(h, w["win_t"], "nt", BF16, L, PROJ_TN, D_MODEL, f"in_proj_{tag}")
    if after_proj is not None:
        w = after_proj(proj)
    s5_raw, s5_ops = s5 if s5 is not None else _s5_prep(p, tag)
    ya0 = _s5_fwd(proj, *s5_ops, name=f"s5_fwd_{tag}")
    ya = _glu_fwd(ya0, proj, w["glu"], p["ssm_glu_b"][None], f"glu_fwd_{tag}")
    yb = _sg_fwd(proj, p["sg_ln_w"][None], p["sg_ln_b"][None], p["sg_w"], p["sg_b"][:, :, None], f"sg_fwd_{tag}")
    yc, o_att = _attn_fwd(proj, p["attn_sinks"], tabs, f"attn_fwd_{tag}")
    ta = _mm(ya, w["wba_t"], "nt", BF16, 1024, 1024, 1024, f"branch_a_{tag}")
    tb = _mm(yb, w["wbb_t"], "nt", BF16, 1024, 1024, 1024, f"branch_b_{tag}")
    tc = _mm(yc, w["wbc_t"], "nt", BF16, 1024, 1024, 1024, f"branch_c_{tag}")
    merged = _merge_fwd(ta, tb, tc, proj, f"merge_fwd_{tag}")
    x_new = _mm(merged, w["wout"], "nn", F32, 1024, 512, D_MODEL, f"out_proj_{tag}", res=x)
    saved = dict(x=x, h=h, proj=proj, s5_raw=s5_raw, s5_ops=s5_ops, ya0=ya0, ya=ya, yb=yb, yc=yc, o_att=o_att,
                 ta=ta, tb=tb, tc=tc, merged=merged)
    return x_new, saved


def _layer_bwd(dx_out, p, w, tabs, s, tag, first_after=None, after_merge=None, before_win=None, after_win=None):
    L = dx_out.shape[0]
    proj = s["proj"]
    big, small = {}, {}
    dmerged = _mm(dx_out, w["wout"], "nt", BF16, 1024, 512, D_MODEL, f"d_merged_{tag}", after=first_after)
    big["wout"] = _mm(s["merged"], dx_out, "tn", GRAD_DT, 512, 1024, L, f"d_wout_{tag}")
    dta, dtb, dtc, dga, dgb, dgc = _merge_bwd(s["ta"], s["tb"], s["tc"], proj, dmerged, f"merge_bwd_{tag}")
    tok = after_merge(dga) if after_merge is not None else None
    dy = {}
    for br, dt in (("a", dta), ("b", dtb), ("c", dtc)):
        dy[br] = _mm(dt, w[f"wb{br}_t"], "nn", BF16, 1024, 1024, D_MODEL, f"d_y{br}_{tag}", after=tok)
        big[f"wb{br}_t"] = _mm(dt, s[f"y{br}"], "tn", GRAD_DT, 512, 1024, L, f"d_wb{br}_{tag}")

    dq, dzc, dkvc, dkvp, dsink = _attn_bwd(proj, p["attn_sinks"], tabs, s["o_att"], dy["c"], f"attn_bwd_{tag}")
    dkv = dkvc + jnp.concatenate([dkvp[CHUNK:], jnp.zeros((CHUNK, 256), F32)], axis=0)
    small["attn_sinks"] = dsink[:, 0]

    dub, dvb, dzb, dlw, dlb, dsgw, dsgb = _sg_bwd(
        proj, p["sg_ln_w"][None], p["sg_ln_b"][None], p["sg_w"], p["sg_b"][:, :, None], dy["b"], f"sg_bwd_{tag}")
    small.update(sg_ln_w=dlw[0], sg_ln_b=dlb[0], sg_w=dsgw, sg_b=dsgb[:, :, 0])

    dya0, dza, dglu, dglub = _glu_bwd(s["ya0"], proj, w["glu"], p["ssm_glu_b"][None], dy["a"], f"glu_bwd_{tag}")
    big["glu"] = dglu.astype(GRAD_DT)
    small["ssm_glu_b"] = dglub[0]

    dua, dbre, dbim, dcre, dcim, dlr, dli, dd = _s5_bwd(proj, dya0, *s["s5_ops"], name=f"s5_bwd_{tag}")
    da_re, da_im, dlog_dt, dbt_re, dbt_im = _s5_params_bwd(
        *s["s5_raw"], dlr.reshape(SSM_GROUPS, SSM_STATE), dli.reshape(SSM_GROUPS, SSM_STATE),
        _unslab_b(dbre), _unslab_b(dbim), name=f"s5_params_bwd_{tag}")
    small.update(ssm_a_re=da_re, ssm_a_im=da_im, ssm_log_dt=dlog_dt[:, 0],
                 ssm_bt_re=dbt_re, ssm_bt_im=dbt_im,
                 ssm_c_re=_unslab_c(dcre), ssm_c_im=_unslab_c(dcim), ssm_d=dd.reshape(SSM_WIDTH))

    dproj = jnp.concatenate([dua, dza, dub, dvb, dzb, dq, dkv.astype(BF16), dzc, dga, dgb, dgc], axis=-1)
    tok = before_win(big) if before_win is not None else None
    big["win_t"] = _mm(dproj, s["h"], "tn", GRAD_DT, 256, D_MODEL, L, f"d_win_{tag}", after=tok)
    tok = after_win(big) if after_win is not None else None
    dh = _mm(dproj, w["win_t"], "nn", F32, L, D_MODEL, 256, f"d_h_{tag}", after=tok)
    dx_in, dnw = _rms_bwd(s["x"], p["norm_w"][None], dh, dx_out, f"rms_bwd_{tag}")
    small["norm_w"] = dnw[0]
    return dx_in, big, small


def _local_step(x, tgt, small_p, final_w, big_w):
    L = x.shape[0]
    tabs = _rope_tables(L)
    saved = []
    for l in range(DEPTH):
        x, s = _layer_fwd(x, small_p[l], big_w[l], tabs, f"l{l}")
        saved.append(s)
    loss_acc, dx, dfw = _final(x, final_w[None], tgt, "final_norm_loss")
    big_g, small_g = [None] * DEPTH, [None] * DEPTH
    for l in reversed(range(DEPTH)):
        dx, big_g[l], small_g[l] = _layer_bwd(dx, small_p[l], big_w[l], tabs, saved[l], f"l{l}")
    return loss_acc[0, 0], dx, dfw[0], big_g, small_g


MESH = pl.DeviceIdType.MESH
ANY = pl.BlockSpec(memory_space=pl.ANY)
ROW_ALIGN = 16


def _place():
    return lax.axis_index("x"), lax.axis_index("y"), lax.axis_index("c")


HBM = pl.BlockSpec(memory_space=pltpu.HBM)
SEM = pl.BlockSpec(memory_space=pltpu.SEMAPHORE)
EFFECT = pltpu.SideEffectType.DATAFLOW_SIDE_EFFECTING


def _split_start(srcs, lands, n_copies, copies, name, after=None):
    n, m, k = len(srcs), len(lands), n_copies
    extra = [] if after is None else [after]

    def body(*refs):
        src_refs, land_refs = refs[:n], refs[n:n + m]
        sems = refs[n + m + len(extra):]
        send_sems, recv_sems, token = sems[:k], sems[k:2 * k], refs[-1]
        for cp in copies(src_refs, land_refs, send_sems, recv_sems):
            cp.start()
        token[...] = jnp.zeros_like(token)

    ops = list(srcs) + list(lands)
    outs = pl.pallas_call(
        body, name=name,
        out_shape=(*[pltpu.SemaphoreType.DMA(())] * (2 * k),
                   *[pltpu.HBM(a.shape, a.dtype) for a in ops], jax.ShapeDtypeStruct((8, 128), F32)),
        in_specs=[HBM] * (n + m) + [ANY] * len(extra),
        out_specs=(*[SEM] * (2 * k), *[HBM] * (n + m), pl.BlockSpec(memory_space=pltpu.VMEM)),
        input_output_aliases={i: 2 * k + i for i in range(n + m)},
        compiler_params=pltpu.CompilerParams(has_side_effects=EFFECT),
    )(*[pltpu.with_memory_space_constraint(a, pltpu.HBM) for a in ops], *extra)
    return (list(outs[:k]), list(outs[k:2 * k]), list(outs[2 * k:2 * k + n]), list(outs[2 * k + n:2 * k + n + m]),
            outs[-1])


def _split_wait(send_sems, recv_sems, srcs, lands, after, copies, name):
    n, m, k = len(srcs), len(lands), len(send_sems)
    after = list(after) if isinstance(after, (list, tuple)) else [after]

    def body(*refs):
        src_refs, land_refs = refs[:n], refs[n:n + m]
        for cp in copies(src_refs, land_refs, refs[n + m:n + m + k], refs[n + m + k:n + m + 2 * k]):
            cp.wait_send()
            cp.wait_recv()

    ops = list(srcs) + list(lands)
    outs = pl.pallas_call(
        body, name=name,
        out_shape=tuple(pltpu.HBM(a.shape, a.dtype) for a in ops),
        in_specs=[HBM] * (n + m) + [SEM] * (2 * k) + [ANY] * len(after),
        out_specs=tuple([HBM] * (n + m)),
        input_output_aliases={i: i for i in range(n + m)},
        compiler_params=pltpu.CompilerParams(has_side_effects=EFFECT),
    )(*ops, *send_sems, *recv_sems, *after)
    return list(outs[:n]), list(outs[n:])


def _ag_rows(land_ref, px, py, pc):
    r = land_ref.shape[0] // N_DEV
    start = pl.multiple_of((4 * px + 2 * py + pc) * r, ROW_ALIGN)
    return land_ref.at[pl.ds(start, r), :]


def _ag_copies_to(which):
    def copies(src_refs, land_refs, send_sems, recv_sems):
        x, y, c = _place()
        peers = [(x, y, 1 - c), (1 - x, y, c), (x, 1 - y, c), (1 - x, 1 - y, c)]
        return [pltpu.make_async_remote_copy(
            src_ref=_ag_rows(land_refs[a], x, y, c), dst_ref=_ag_rows(land_refs[a], x, y, c),
            send_sem=send_sems[len(which) * a + k], recv_sem=recv_sems[len(which) * a + k],
            device_id=peers[p], device_id_type=MESH)
            for a in range(len(land_refs)) for k, p in enumerate(which)]
    return copies


_ag_copies = _ag_copies_to((0, 1, 2, 3))
_ag_copies_near = _ag_copies_to((0, 1, 2))
_ag_copies_far = _ag_copies_to((3,))


def _ag_forward(lands, name, which=(0, 1, 2)):
    n = len(lands)

    def body(*refs):
        land_refs = refs[n:2 * n]
        send_sems, recv_sems = refs[2 * n:]
        x, y, c = _place()
        chips = [(1 - x, y), (x, 1 - y), (1 - x, 1 - y)]

        def copy(a, k, pc):
            px, py = chips[which[k]]
            return pltpu.make_async_remote_copy(
                src_ref=_ag_rows(land_refs[a], px, py, pc), dst_ref=_ag_rows(land_refs[a], px, py, pc),
                send_sem=send_sems.at[a, k], recv_sem=recv_sems.at[a, k], device_id=(x, y, 1 - c), device_id_type=MESH)

        passed = [copy(a, k, c) for a in range(n) for k in range(len(which))]
        for cp in passed:
            cp.start()
        for a in range(n):
            for k in range(len(which)):
                copy(a, k, 1 - c).wait_recv()
        for cp in passed:
            cp.wait_send()

    sems = pltpu.SemaphoreType.DMA((n, len(which)))
    return pl.pallas_call(
        body, name=name,
        in_specs=[ANY] * n, out_specs=[ANY] * n,
        out_shape=[jax.ShapeDtypeStruct(l.shape, l.dtype) for l in lands],
        input_output_aliases={i: i for i in range(n)},
        scratch_shapes=[sems, sems],
    )(*lands)


def _allgather_place(shards):
    x, y, c = _place()
    return [lax.dynamic_update_slice(lax.empty((N_DEV * s.shape[0], s.shape[1]), s.dtype), s,
                                     ((4 * x + 2 * y + c) * s.shape[0], 0)) for s in shards]


def _allgather_place_rows(w, layer, name):
    _, r, cols = w.shape
    tr = _row_tile(r)
    nb = r // tr
    x, y, c = _place()
    me = (4 * x + 2 * y + c).astype(jnp.int32)[None]

    def body(me_ref, s_ref, o_ref):
        o_ref[...] = s_ref[...].astype(BF16)

    return pl.pallas_call(
        body, name=name,
        grid_spec=pltpu.PrefetchScalarGridSpec(
            num_scalar_prefetch=1, grid=(nb,),
            in_specs=[pl.BlockSpec((None, tr, cols), lambda i, me_ref: (layer, i, 0))],
            out_specs=pl.BlockSpec((tr, cols), lambda i, me_ref: (me_ref[0] * nb + i, 0))),
        out_shape=jax.ShapeDtypeStruct((N_DEV * r, cols), BF16),
        compiler_params=_cp(("parallel",)),
    )(me, w)


def _allgather_start(lands, name, after=None):
    return _split_start([], lands, 4 * len(lands), _ag_copies, name + "_start", after=after)


def _allgather_finish(started, after, name):
    send_sems, recv_sems, _, lands, _ = started
    _, lands = _split_wait(send_sems, recv_sems, [], lands, after, _ag_copies, name + "_wait")
    return list(_ag_forward(lands, name + "_forward"))


def _rs_swap_cores(grads, name):
    n = len(grads)

    def body(*refs):
        ins, outs = refs[:n], refs[n:2 * n]
        send_sems, recv_sems = refs[2 * n:]
        x, y, c = _place()
        cps = []
        for a in range(n):
            r = ins[a].shape[0] // N_DEV
            for q in range(4):
                start = pl.multiple_of((2 * q + 1 - c) * r, ROW_ALIGN)
                cps.append(pltpu.make_async_remote_copy(
                    src_ref=ins[a].at[pl.ds(start, r), :], dst_ref=outs[a].at[q],
                    send_sem=send_sems.at[a, q], recv_sem=recv_sems.at[a, q],
                    device_id=(x, y, 1 - c), device_id_type=MESH))
        for cp in cps:
            cp.start()
        for cp in cps:
            cp.wait()

    return pl.pallas_call(
        body, name=name, in_specs=[ANY] * n, out_specs=[ANY] * n,
        out_shape=[jax.ShapeDtypeStruct((4, g.shape[0] // N_DEV, g.shape[1]), g.dtype) for g in grads],
        scratch_shapes=[pltpu.SemaphoreType.DMA((n, 4)), pltpu.SemaphoreType.DMA((n, 4))],
    )(*grads)


def _rs_chip_copies(sum_refs, land_refs, send_sems, recv_sems):
    x, y, c = _place()
    chips = [(1 - x, y), (x, 1 - y), (1 - x, 1 - y)]
    return [pltpu.make_async_remote_copy(
        src_ref=sum_refs[a].at[2 * px + py], dst_ref=land_refs[a].at[2 * x + y],
        send_sem=send_sems[3 * a + j], recv_sem=recv_sems[3 * a + j], device_id=(px, py, c), device_id_type=MESH)
        for a in range(len(sum_refs)) for j, (px, py) in enumerate(chips)]


def _row_tile(r):
    return max(t for t in range(ROW_ALIGN, min(r, 1024) + 1, ROW_ALIGN) if r % t == 0)


def _rs_add_cores(grad, recv, cidx, name):
    r, cols = recv.shape[1], recv.shape[2]
    tr = _row_tile(r)
    nb = r // tr

    def body(c_ref, g_ref, r_ref, o_ref):
        o_ref[...] = (g_ref[...].astype(F32) + r_ref[...].astype(F32)).astype(o_ref.dtype)

    return pl.pallas_call(
        body, name=name,
        grid_spec=pltpu.PrefetchScalarGridSpec(
            num_scalar_prefetch=1, grid=(4, nb),
            in_specs=[pl.BlockSpec((tr, cols), lambda q, i, c_ref: ((2 * q + c_ref[0]) * nb + i, 0)),
                      pl.BlockSpec((None, tr, cols), lambda q, i, c_ref: (q, i, 0))],
            out_specs=pl.BlockSpec((None, tr, cols), lambda q, i, c_ref: (q, i, 0))),
        out_shape=jax.ShapeDtypeStruct(recv.shape, recv.dtype),
        compiler_params=_cp(("parallel", "parallel")),
    )(cidx, grad, recv)


def _rs_add_chips(own, recv, slots, name):
    r, cols = recv.shape[1], recv.shape[2]
    tr = _row_tile(r)

    def body(s_ref, o_ref, r0_ref, r1_ref, r2_ref, out_ref):
        acc = o_ref[...].astype(F32)
        for ref in (r0_ref, r1_ref, r2_ref):
            acc = acc + ref[...].astype(F32)
        out_ref[...] = acc

    pick = lambda k: pl.BlockSpec((None, tr, cols), functools.partial(lambda i, s_ref, k: (s_ref[k], i, 0), k=k))
    return pl.pallas_call(
        body, name=name,
        grid_spec=pltpu.PrefetchScalarGridSpec(
            num_scalar_prefetch=1, grid=(r // tr,),
            in_specs=[pick(0), pick(1), pick(2), pick(3)],
            out_specs=pl.BlockSpec((tr, cols), lambda i, s_ref: (i, 0))),
        out_shape=jax.ShapeDtypeStruct((r, cols), F32),
        compiler_params=_cp(("parallel",)),
    )(slots, own, recv, recv, recv)


def _rs_core_copies(grad_refs, land_refs, send_sems, recv_sems):
    x, y, c = _place()
    cps = []
    for a in range(len(grad_refs)):
        r = grad_refs[a].shape[0] // N_DEV
        for q in range(4):
            start = pl.multiple_of((2 * q + 1 - c) * r, ROW_ALIGN)
            cps.append(pltpu.make_async_remote_copy(
                src_ref=grad_refs[a].at[pl.ds(start, r), :], dst_ref=land_refs[a].at[q],
                send_sem=send_sems[4 * a + q], recv_sem=recv_sems[4 * a + q],
                device_id=(x, y, 1 - c), device_id_type=MESH))
    return cps


def _reduce_scatter_chips_start(grads, recv, tag):
    cidx = lax.axis_index("c").astype(jnp.int32)[None]
    sums = [_rs_add_cores(g, rv, cidx, f"rs_add_cores_{tag}_{i}") for i, (g, rv) in enumerate(zip(grads, recv))]
    lands = [lax.empty(s.shape, s.dtype) for s in sums]
    return _split_start(sums, lands, 3 * len(sums), _rs_chip_copies, f"rs_chips_{tag}_start")


def _reduce_scatter_start(grads, tag):
    return _reduce_scatter_chips_start(grads, _rs_swap_cores(grads, f"rs_swap_cores_{tag}"), tag)


def _reduce_scatter_cores_start(grads, tag):
    lands = [lax.empty((4, g.shape[0] // N_DEV, g.shape[1]), g.dtype) for g in grads]
    return _split_start(grads, lands, 4 * len(grads), _rs_core_copies, f"rs_cores_{tag}_start")


def _reduce_scatter_cores_finish(started, after, tag):
    send_sems, recv_sems, grads, lands, _ = started
    grads, recv = _split_wait(send_sems, recv_sems, grads, lands, after, _rs_core_copies, f"rs_cores_{tag}_wait")
    return _reduce_scatter_chips_start(grads, recv, tag)


def _reduce_scatter_finish(started, after, tag):
    send_sems, recv_sems, sums, lands, _ = started
    sums, lands = _split_wait(send_sems, recv_sems, sums, lands, after, _rs_chip_copies, f"rs_chips_{tag}_wait")
    x, y = lax.axis_index("x"), lax.axis_index("y")
    slots = jnp.stack([2 * x + y, 2 * (1 - x) + y, 2 * x + 1 - y, 2 * (1 - x) + 1 - y]).astype(jnp.int32)
    return [(s, l, slots) for s, l in zip(sums, lands)]


def _ar_peers():
    x, y, c = _place()
    return [(1 - x if k & 4 else x, 1 - y if k & 2 else y, 1 - c if k & 1 else c) for k in range(1, N_DEV)]


def _ar_scatter_copies(src_refs, land_refs, send_sems, recv_sems):
    x, y, c = _place()
    cps = []
    for a in range(len(src_refs)):
        rs = src_refs[a].shape[0] // N_DEV
        for k, (px, py, pc) in enumerate(_ar_peers()):
            start = pl.multiple_of((4 * px + 2 * py + pc) * rs, 8)
            cps.append(pltpu.make_async_remote_copy(
                src_ref=src_refs[a].at[pl.ds(start, rs), :], dst_ref=land_refs[a].at[4 * x + 2 * y + c],
                send_sem=send_sems[7 * a + k], recv_sem=recv_sems[7 * a + k],
                device_id=(px, py, pc), device_id_type=MESH))
    return cps


def _ar_gather_copies(src_refs, land_refs, send_sems, recv_sems):
    x, y, c = _place()
    cps = []
    for a in range(len(land_refs)):
        rs = land_refs[a].shape[0] // N_DEV
        mine = land_refs[a].at[pl.ds(pl.multiple_of((4 * x + 2 * y + c) * rs, 8), rs), :]
        for k, peer in enumerate(_ar_peers()):
            cps.append(pltpu.make_async_remote_copy(
                src_ref=mine, dst_ref=mine, send_sem=send_sems[7 * a + k], recv_sem=recv_sems[7 * a + k],
                device_id=peer, device_id_type=MESH))
    return cps


def _allreduce_start(packs, name, after=None):
    assert all(p.shape[0] % (8 * N_DEV) == 0 for p in packs)
    x, y, c = _place()
    me = 4 * x + 2 * y + c
    lands = []
    for p in packs:
        rs = p.shape[0] // N_DEV
        own = lax.dynamic_slice(p, (me * rs, 0), (rs, p.shape[1]))
        lands.append(lax.dynamic_update_slice(lax.empty((N_DEV, rs, p.shape[1]), F32), own[None], (me, 0, 0)))
    return _split_start(packs, lands, 7 * len(packs), _ar_scatter_copies, name + "_scatter_start", after=after)


def _allreduce_middle(started, after, name):
    n = len(started[2])
    _, parts = _split_wait(started[0], started[1], started[2], started[3], after, _ar_scatter_copies, name + "_scatter_wait")

    def body(*refs):
        for p_ref, o_ref in zip(refs[:n], refs[n:]):
            acc = p_ref[0]
            for d in range(1, N_DEV):
                acc = acc + p_ref[d]
            o_ref[...] = acc

    sums = pl.pallas_call(body, name=name + "_add",
                          out_shape=[jax.ShapeDtypeStruct(p.shape[1:], F32) for p in parts])(*parts)
    x, y, c = _place()
    me = 4 * x + 2 * y + c
    lands = [lax.dynamic_update_slice(lax.empty((N_DEV * s.shape[0], s.shape[1]), F32), s, (me * s.shape[0], 0))
             for s in sums]
    return _split_start([], lands, 7 * n, _ar_gather_copies, name + "_gather_start")


def _allreduce_finish(started, after, name):
    _, lands = _split_wait(started[0], started[1], [], started[3], after, _ar_gather_copies, name + "_gather_wait")
    return lands


ADAM_TILE_BYTES = 2 * 1024 * 1024


def _adam_tiles(rows, cols, align=8):
    if rows % align:
        return rows, cols
    best = None
    for tc in {cols, cols // 2, cols // 4}:
        if tc != cols and (tc % 128 or cols % tc):
            continue
        fits = [t for t in range(align, rows + 1, align) if rows % t == 0 and t * max(tc, 128) * 4 <= ADAM_TILE_BYTES]
        if fits and (best is None or max(fits) * tc > best[0] * best[1]):
            best = (max(fits), tc)
    return best


def _adam_math(w, g, m, v):
    nm = ADAM_B1 * m + (1.0 - ADAM_B1) * g
    nv = ADAM_B2 * v + (1.0 - ADAM_B2) * jnp.square(g)
    c1 = 1.0 - ADAM_B1 ** ADAM_STEP
    c2 = 1.0 - ADAM_B2 ** ADAM_STEP
    return -ADAM_LR * ((nm / c1) / (jnp.sqrt(nv / c2) + ADAM_EPS) + ADAM_WD * w), nm, nv


def _adamw_layer(w, g, m, v, layer, carry, name):
    _, rows, cols = w.shape
    tr, tc = _adam_tiles(rows, cols)

    def body(w_ref, g_ref, m_ref, v_ref, *rest):
        go_ref, d_ref, nm_ref, nv_ref = rest[-4:]
        gv = g_ref[...]
        go_ref[...] = gv
        d_ref[...], nm_ref[...], nv_ref[...] = _adam_math(w_ref[...], gv, m_ref[...], v_ref[...])

    blk = pl.BlockSpec((None, tr, tc), lambda i, j: (layer, i, j))
    flat = pl.BlockSpec((tr, tc), lambda i, j: (i, j))
    sh = jax.ShapeDtypeStruct(w.shape, F32)
    carry = [] if carry is None else list(carry)
    return pl.pallas_call(
        body, name=name, grid=(rows // tr, cols // tc),
        in_specs=[blk, flat, blk, blk] + [ANY] * len(carry), out_specs=[blk] * 4, out_shape=[sh] * 4,
        input_output_aliases={4 + k: k for k in range(len(carry))},
        compiler_params=_cp(("parallel", "parallel")),
    )(w, g, m, v, *carry)


def _adamw_layer_sum(w, own, recv, slots, m, v, layer, carry, name):
    _, rows, cols = w.shape
    tr, tc = _adam_tiles(rows, cols, align=ROW_ALIGN)

    def body(s_ref, w_ref, o_ref, r0_ref, r1_ref, r2_ref, m_ref, v_ref, *rest):
        go_ref, d_ref, nm_ref, nv_ref = rest[-4:]
        gv = o_ref[...].astype(F32)
        for ref in (r0_ref, r1_ref, r2_ref):
            gv = gv + ref[...].astype(F32)
        go_ref[...] = gv
        d_ref[...], nm_ref[...], nv_ref[...] = _adam_math(w_ref[...], gv, m_ref[...], v_ref[...])

    blk = pl.BlockSpec((None, tr, tc), lambda i, j, s: (layer, i, j))
    pick = lambda k: pl.BlockSpec((None, tr, tc), functools.partial(lambda i, j, s, k: (s[k], i, j), k=k))
    sh = jax.ShapeDtypeStruct(w.shape, F32)
    carry = [] if carry is None else list(carry)
    return pl.pallas_call(
        body, name=name,
        grid_spec=pltpu.PrefetchScalarGridSpec(
            num_scalar_prefetch=1, grid=(rows // tr, cols // tc),
            in_specs=[blk, pick(0), pick(1), pick(2), pick(3), blk, blk] + [ANY] * len(carry),
            out_specs=[blk] * 4),
        out_shape=[sh] * 4,
        input_output_aliases={8 + k: k for k in range(len(carry))},
        compiler_params=_cp(("parallel", "parallel")),
    )(slots, w, own, recv, recv, recv, m, v, *carry)


def _adamw(w, g, m, v, name):
    shape = w.shape
    rows, cols = shape[-2:]
    lead = shape[:-2]
    nl = math.prod(lead)
    tr, tc = _adam_tiles(rows, cols)

    def body(w_ref, g_ref, m_ref, v_ref, d_ref, nm_ref, nv_ref):
        d_ref[...], nm_ref[...], nv_ref[...] = _adam_math(w_ref[...], g_ref[...], m_ref[...], v_ref[...])

    def index(b, i, j):
        return (*jnp.unravel_index(b, lead), i, j) if lead else (i, j)

    blk = pl.BlockSpec((*[None] * len(lead), tr, tc), index)
    sh = jax.ShapeDtypeStruct(shape, F32)
    return pl.pallas_call(
        body, name=name, grid=(nl, rows // tr, cols // tc), in_specs=[blk] * 4, out_specs=[blk] * 3,
        out_shape=[sh] * 3, compiler_params=_cp(("parallel", "parallel", "parallel")),
    )(w, g, m, v)


WEIGHTS = ("norm_w", "w_in", "ssm_a_re", "ssm_a_im", "ssm_log_dt", "ssm_b_re", "ssm_b_im", "ssm_c_re", "ssm_c_im",
           "ssm_d", "ssm_glu_w", "ssm_glu_b", "sg_ln_w", "sg_ln_b", "sg_w", "sg_b", "attn_sinks",
           "w_branch_a", "w_branch_b", "w_branch_c", "w_out", "final_norm_w")
BIG = ("w_in", "ssm_glu_w", "w_branch_a", "w_branch_b", "w_branch_c", "w_out")
BIG_KEY = {"w_in": ("win_t", True), "ssm_glu_w": ("glu", False), "w_branch_a": ("wba_t", True),
           "w_branch_b": ("wbb_t", True), "w_branch_c": ("wbc_t", True), "w_out": ("wout", False)}
VIEWS = {"w_in": (1, 2), "ssm_b_re": (2, 3), "ssm_b_im": (2, 3)}
MATS = ("ssm_a_re", "ssm_a_im", "ssm_c_re", "ssm_c_im", "ssm_b_re", "ssm_b_im", "sg_w")
VEC_GROUPS = (("ssm_d", "ssm_glu_b", "sg_ln_w", "sg_ln_b"), ("norm_w", "final_norm_w", "sg_b"), ("ssm_log_dt", "attn_sinks"))
PACK_ROWS = 8 * N_DEV


def _view(n, a):
    return jnp.swapaxes(a, *VIEWS[n]) if n in VIEWS else a


def _vec_moves(pack_ref, refs, to_pack):
    d, gb, lw, lb, nw, fw, sb, ld, sk = refs
    full = (slice(None), slice(None))
    moves = [((slice(2 * i, 2 * i + 2), slice(None)), r, full) for i, r in enumerate((d, gb, lw, lb))]
    moves += [((slice(8, 10), slice(None)), nw, (slice(None), slice(0, 1024))),
              ((slice(10, 12), slice(None)), nw, (slice(None), slice(1024, 2048))),
              ((slice(12, 13), slice(None)), fw, (slice(None), slice(0, 1024))),
              ((slice(13, 14), slice(None)), fw, (slice(None), slice(1024, 2048))),
              ((slice(16, 32), slice(0, 128)), sb, full),
              ((slice(32, 34), slice(0, 64)), ld, full),
              ((slice(34, 36), slice(0, 16)), sk, full)]
    for where, ref, part in moves:
        if to_pack:
            pack_ref[where] = ref[part]
        else:
            ref[part] = pack_ref[where]


def _vec_shapes(arrs):
    d, gb, lw, lb, nw, fw, sb, ld, sk = arrs
    return [d, gb, lw, lb, nw, fw.reshape(1, -1), sb.reshape(-1, sb.shape[-1]), ld, sk]


def _vec_pack(arrs, name):
    def body(*refs):
        refs[-1][...] = jnp.zeros_like(refs[-1])
        _vec_moves(refs[-1], refs[:-1], True)

    return pl.pallas_call(body, name=name, out_shape=jax.ShapeDtypeStruct((PACK_ROWS, 1024), F32))(*_vec_shapes(arrs))


def _vec_unpack(pack, like, name):
    shaped = _vec_shapes(like)

    def body(pack_ref, *refs):
        _vec_moves(pack_ref, refs, False)

    outs = pl.pallas_call(body, name=name, out_shape=[jax.ShapeDtypeStruct(a.shape, F32) for a in shaped])(pack)
    return [o.reshape(a.shape) for o, a in zip(outs, like)]


def _pack(groups, cols, name):
    assert cols == 1024
    return _vec_pack([a for arrs in groups for a in arrs], name)


def _unpack(pack, groups, name):
    return _vec_unpack(pack, [a for arrs in groups for a in arrs], name)


def kernel(x, norm_w, w_in, ssm_a_re, ssm_a_im, ssm_log_dt, ssm_b_re, ssm_b_im, ssm_c_re, ssm_c_im, ssm_d, ssm_glu_w, ssm_glu_b, sg_ln_w, sg_ln_b, sg_w, sg_b, attn_sinks, w_branch_a, w_branch_b, w_branch_c, w_out, final_norm_w, loss_target, m_norm_w, m_w_in, m_ssm_a_re, m_ssm_a_im, m_ssm_log_dt, m_ssm_b_re, m_ssm_b_im, m_ssm_c_re, m_ssm_c_im, m_ssm_d, m_ssm_glu_w, m_ssm_glu_b, m_sg_ln_w, m_sg_ln_b, m_sg_w, m_sg_b, m_attn_sinks, m_w_branch_a, m_w_branch_b, m_w_branch_c, m_w_out, m_final_norm_w, v_norm_w, v_w_in, v_ssm_a_re, v_ssm_a_im, v_ssm_log_dt, v_ssm_b_re, v_ssm_b_im, v_ssm_c_re, v_ssm_c_im, v_ssm_d, v_ssm_glu_w, v_ssm_glu_b, v_sg_ln_w, v_sg_ln_b, v_sg_w, v_sg_b, v_attn_sinks, v_w_branch_a, v_w_branch_b, v_w_branch_c, v_w_out, v_final_norm_w):
    w = dict(zip(WEIGHTS, (norm_w, w_in, ssm_a_re, ssm_a_im, ssm_log_dt, ssm_b_re, ssm_b_im, ssm_c_re, ssm_c_im, ssm_d, ssm_glu_w, ssm_glu_b, sg_ln_w, sg_ln_b, sg_w, sg_b, attn_sinks, w_branch_a, w_branch_b, w_branch_c, w_out, final_norm_w)))
    m = dict(zip(WEIGHTS, (m_norm_w, m_w_in, m_ssm_a_re, m_ssm_a_im, m_ssm_log_dt, m_ssm_b_re, m_ssm_b_im, m_ssm_c_re, m_ssm_c_im, m_ssm_d, m_ssm_glu_w, m_ssm_glu_b, m_sg_ln_w, m_sg_ln_b, m_sg_w, m_sg_b, m_attn_sinks, m_w_branch_a, m_w_branch_b, m_w_branch_c, m_w_out, m_final_norm_w)))
    v = dict(zip(WEIGHTS, (v_norm_w, v_w_in, v_ssm_a_re, v_ssm_a_im, v_ssm_log_dt, v_ssm_b_re, v_ssm_b_im, v_ssm_c_re, v_ssm_c_im, v_ssm_d, v_ssm_glu_w, v_ssm_glu_b, v_sg_ln_w, v_sg_ln_b, v_sg_w, v_sg_b, v_attn_sinks, v_w_branch_a, v_w_branch_b, v_w_branch_c, v_w_out, v_final_norm_w)))

    keys = [BIG_KEY[n][0] for n in BIG]
    wv, mv, vv = ({n: _view(n, a) for n, a in d.items()} for d in (w, m, v))
    shards = [[(wv[n][l] if n in VIEWS else w[n][l].T if BIG_KEY[n][1] else w[n][l]).astype(BF16) for n in BIG]
              for l in range(DEPTH)]
    small_p = [{n: w[n][l] for n in SMALL} for l in range(DEPTH)]
    xv, tgt = x[0], loss_target[0]
    tabs = _rope_tables(xv.shape[0])

    lands = [[[_allgather_place_rows(wv["w_in"], l, f"place_win_l{l}")], _allgather_place(shards[l][1:])]
             for l in range(DEPTH)]
    s5 = [_s5_prep(small_p[l], f"l{l}") for l in range(DEPTH)]
    vec_packs = [_pack([[d[n] for n in names] for names in VEC_GROUPS], 1024, f"pack_vec_{tag}")
                 for tag, d in (("w", wv), ("m", mv), ("v", vv))]
    near = _split_start([], lands[0][0], 3, _ag_copies_near, "ag_l0_win_near_start")
    got = {}
    x_, y_ = lax.axis_index("x"), lax.axis_index("y")
    n_tiles = D_IN // PROJ_TN
    far_first = (D_IN // 4 // PROJ_TN) * (2 * (1 - x_) + (1 - y_))
    n_far = -(-D_IN // 4 // PROJ_TN)
    tile_ids = jnp.arange(n_tiles, dtype=jnp.int32)
    is_far = (tile_ids >= far_first) & (tile_ids < far_first + n_far)
    near_tiles = jnp.sort(jnp.where(is_far, n_tiles, tile_ids))[:n_tiles - n_far]
    far_tiles = (far_first + jnp.arange(n_far)).astype(jnp.int32)

    def proj_of0(h):
        early = [h, *lands[0][1], *lands[1][0], *lands[1][1], *s5[0][1], *s5[1][1], near_tiles, far_tiles]
        early += vec_packs
        _, land = _split_wait(near[0], near[1], [], near[3], early, _ag_copies_near, "ag_l0_win_near_wait")
        far = _split_start([], land, 1, _ag_copies_far, "ag_l0_win_far_start")
        land = _ag_forward(far[3], "ag_l0_win_near_forward", which=(0, 1))
        got["ag0b"] = _allgather_start(lands[0][1], "ag_l0_rest", after=land[0])
        got["near1"] = _split_start([], lands[1][0], 3, _ag_copies_near, "ag_l1_win_near_start", after=got["ag0b"][4])
        proj = _in_proj_tiles(h, land[0], near_tiles, None, "in_proj_l0_near", after=got["near1"][4])
        _, land = _split_wait(far[0], far[1], [], land, proj, _ag_copies_far, "ag_l0_win_far_wait")
        got["win0"] = _ag_forward(land, "ag_l0_win_far_forward", which=(2,))[0]
        return _in_proj_tiles(h, got["win0"], far_tiles, proj, "in_proj_l0_far")

    def after_proj0(proj):
        got["w0"] = dict(zip(keys, [got["win0"]] + _allgather_finish(got["ag0b"], proj, "ag_l0_rest")))
        return got["w0"]

    x1, saved0 = _layer_fwd(xv, small_p[0], None, tabs, "l0", s5=s5[0], proj_of=proj_of0, after_proj=after_proj0)
    big_w0 = got["w0"]

    def proj_of1(h):
        near1 = got["near1"]
        _, land = _split_wait(near1[0], near1[1], [], near1[3], h, _ag_copies_near, "ag_l1_win_near_wait")
        far1 = _split_start([], land, 1, _ag_copies_far, "ag_l1_win_far_start")
        land = _ag_forward(far1[3], "ag_l1_win_near_forward", which=(0, 1))
        got["ag1b"] = _allgather_start(lands[1][1], "ag_l1_rest", after=land[0])
        proj = _in_proj_tiles(h, land[0], near_tiles, None, "in_proj_l1_near", after=got["ag1b"][4])
        _, land = _split_wait(far1[0], far1[1], [], land, proj, _ag_copies_far, "ag_l1_win_far_wait")
        got["win1"] = _ag_forward(land, "ag_l1_win_far_forward", which=(2,))[0]
        return _in_proj_tiles(h, got["win1"], far_tiles, proj, "in_proj_l1_far")

    def after_proj1(proj):
        got["w1"] = dict(zip(keys, [got["win1"]] + _allgather_finish(got["ag1b"], proj, "ag_l1_rest")))
        return got["w1"]

    x2, saved1 = _layer_fwd(x1, small_p[1], None, tabs, "l1", s5=s5[1], proj_of=proj_of1, after_proj=after_proj1)
    big_w1 = got["w1"]
    loss_acc, dx2, dfw = _final(x2, w["final_norm_w"][None], tgt, "final_norm_loss")
    loss = lax.psum(loss_acc[0, 0], ("x", "y", "c"))
    dfw = dfw[0]

    dx1, big_g1, small_g1 = _layer_bwd(dx2, small_p[1], big_w1, tabs, saved1, "l1")
    rs1_cores = _reduce_scatter_cores_start([big_g1[k] for k in keys], "l1")

    def after_merge0(x):
        got["rs1"] = _reduce_scatter_cores_finish(rs1_cores, x, "l1")
        return got["rs1"][4]

    def before_win0(big):
        got["rs0b"] = _reduce_scatter_start([big[k] for k in keys[1:]], "l0_rest")
        return got["rs0b"][4]

    def after_win0(big):
        got["rs0a"] = _reduce_scatter_start([big["win_t"]], "l0_win")
        return got["rs0a"][4]

    dx, big_g0, small_g0 = _layer_bwd(dx1, small_p[0], big_w0, tabs, saved0, "l0", first_after=rs1_cores[4],
                                      after_merge=after_merge0, before_win=before_win0, after_win=after_win0)
    rs1 = got["rs1"]
    small_g = [small_g0, small_g1]
    grads, delta, new_m, new_v = {}, {}, {}, {}

    def big_adam(red, layer, carry):
        outs = {}
        for i, n in enumerate(BIG):
            own, recv, slots = red[i]
            prev = None if carry is None else carry[n]
            if BIG_KEY[n][1] and n not in VIEWS:
                g = _rs_add_chips(own, recv, slots, f"rs_add_chips_{n}_l{layer}").T
                outs[n] = _adamw_layer(wv[n], g, mv[n], vv[n], layer, prev, f"adamw_{n}_l{layer}")
            else:
                outs[n] = _adamw_layer_sum(wv[n], own, recv, slots, mv[n], vv[n], layer, prev, f"adamw_{n}_l{layer}")
        return outs

    def small_grad(n):
        if n == "final_norm_w":
            return dfw
        if n in ("ssm_b_re", "ssm_b_im"):
            return jnp.stack([small_g[l][n.replace("ssm_b_", "ssm_bt_")].transpose(1, 0, 2) for l in range(DEPTH)])
        return jnp.stack([small_g[l][n] for l in range(DEPTH)])

    rows_of = lambda a: a.reshape(-1, a.shape[-1])
    g_mats = [rows_of(small_grad(n)) for n in MATS]
    g_vecs = [[small_grad(n) for n in names] for names in VEC_GROUPS]
    ar = _allreduce_start(g_mats + [_pack(g_vecs, 1024, "pack_vec_g")], "allreduce_small")
    big1 = big_adam(_reduce_scatter_finish(rs1, [dx, ar[4]], "l1"), 1, None)
    ar = _allreduce_middle(ar, [big1[n][1] for n in BIG], "allreduce_small")
    red0 = (_reduce_scatter_finish(got["rs0a"], ar[4], "l0_win")
            + _reduce_scatter_finish(got["rs0b"], ar[4], "l0_rest"))
    big0 = big_adam(red0, 0, big1)
    for n, outs in big0.items():
        grads[n], delta[n], new_m[n], new_v[n] = outs
    reduced = _allreduce_finish(ar, [big0[n][1] for n in BIG], "allreduce_small")
    for n, red in zip(MATS, reduced):
        outs = _adamw(rows_of(wv[n]), red, rows_of(mv[n]), rows_of(vv[n]), f"adamw_{n}")
        grads[n], delta[n], new_m[n], new_v[n] = (o.reshape(wv[n].shape) for o in (red, *outs))
    vec_names = [n for names in VEC_GROUPS for n in names]
    grads.update(zip(vec_names, _unpack(reduced[-1], g_vecs, "unpack_vec_g")))
    outs = _adamw(vec_packs[0], reduced[-1], vec_packs[1], vec_packs[2], "adamw_vec")
    for tag, res, o in zip("dmv", (delta, new_m, new_v), outs):
        res.update(zip(vec_names, _unpack(o, [[wv[n] for n in names] for names in VEC_GROUPS], f"unpack_vec_{tag}")))

    return (loss, dx[None], *[_view(n, d[n]) for d in (grads, delta, new_m, new_v) for n in WEIGHTS])
```

```python
import functools
import math

import jax
import jax.numpy as jnp
from jax import lax
from jax.experimental import pallas as pl
from jax.experimental.pallas import tpu as pltpu

F32 = jnp.float32
BF16 = jnp.bfloat16

D_MODEL = 2048
DEPTH = 2
EPS = 1e-6
NEG_INF = -1e30
N_DEV = 8

SSM_WIDTH = 1024
SSM_GROUP = 16
SSM_GROUPS = 64
SSM_STATE = 64
N_SLAB = 8
SLAB_CH = 128
SLAB_ST = 512
SUB = 8
N_GRP = 2
N_SEG = SUB * N_GRP

SG_HEADS = 8
CHUNK = 128
HEAD_DIM = 64
ATT_HEADS = 16
ROT_DIM = 16
ROPE_THETA = 500000.0

D_IN = 13568
OFF_UA, OFF_ZA, OFF_UB, OFF_VB, OFF_ZB, OFF_Q, OFF_KV, OFF_ZC, OFF_G = (
    0, 1024, 2048, 3072, 4096, 5120, 6144, 6400, 7424)

ADAM_LR, ADAM_B1, ADAM_B2, ADAM_EPS, ADAM_WD, ADAM_STEP = 0.001, 0.9, 0.999, 1e-08, 0.01, 10

VMEM_LIMIT = 56 * 1024 * 1024


def _cp(sem=None):
    return pltpu.CompilerParams(dimension_semantics=sem, vmem_limit_bytes=VMEM_LIMIT)


def _dot(a, b):
    return jnp.dot(a, b, preferred_element_type=F32)


def _dot_nt(a, b):
    return lax.dot_general(a, b, (((1,), (1,)), ((), ())), preferred_element_type=F32)


def _dot_tn(a, b):
    return lax.dot_general(a, b, (((0,), (0,)), ((), ())), preferred_element_type=F32)


def _mm(a, b, mode, out_dtype, tm, tn, tk, name, res=None, after=None):
    if mode == "nn":
        (m, k), (_, n) = a.shape, b.shape
    elif mode == "nt":
        (m, k), (n, _) = a.shape, b.shape
    else:
        (k, m), (_, n) = a.shape, b.shape
    tm, tn, tk = min(tm, m), min(tn, n), min(tk, k)
    assert m % tm == 0 and n % tn == 0 and k % tk == 0, (name, m, n, k, tm, tn, tk)
    nk = k // tk
    a_spec = {"nn": pl.BlockSpec((tm, tk), lambda i, j, kk: (i, kk)),
              "nt": pl.BlockSpec((tm, tk), lambda i, j, kk: (i, kk)),
              "tn": pl.BlockSpec((tk, tm), lambda i, j, kk: (kk, i))}[mode]
    b_spec = {"nn": pl.BlockSpec((tk, tn), lambda i, j, kk: (kk, j)),
              "nt": pl.BlockSpec((tn, tk), lambda i, j, kk: (j, kk)),
              "tn": pl.BlockSpec((tk, tn), lambda i, j, kk: (kk, j))}[mode]
    dot = {"nn": _dot, "nt": _dot_nt, "tn": _dot_tn}[mode]
    has_res = res is not None
    direct = out_dtype == F32 and not has_res

    def body(*refs):
        ins, outs = refs[:2 + has_res + (after is not None)], refs[2 + has_res + (after is not None):]
        a_ref, b_ref = ins[:2]
        r_ref = ins[2] if has_res else None
        o_ref = outs[0]
        acc = o_ref if direct else outs[1]
        kk = pl.program_id(2)

        @pl.when(kk == 0)
        def _():
            acc[...] = jnp.zeros_like(acc)

        acc[...] += dot(a_ref[...].astype(BF16), b_ref[...].astype(BF16))

        if not direct:
            @pl.when(kk == nk - 1)
            def _():
                r = acc[...]
                if has_res:
                    r = r + r_ref[...]
                o_ref[...] = r.astype(out_dtype)

    in_specs = [a_spec, b_spec]
    args = [a, b]
    if has_res:
        in_specs.append(pl.BlockSpec((tm, tn), lambda i, j, kk: (i, j)))
        args.append(res)
    if after is not None:
        in_specs.append(pl.BlockSpec(memory_space=pl.ANY))
        args.append(after)
    return pl.pallas_call(
        body, name=name,
        grid=(m // tm, n // tn, nk),
        in_specs=in_specs,
        out_specs=pl.BlockSpec((tm, tn), lambda i, j, kk: (i, j)),
        out_shape=jax.ShapeDtypeStruct((m, n), out_dtype),
        scratch_shapes=[] if direct else [pltpu.VMEM((tm, tn), F32)],
        compiler_params=_cp(("parallel", "parallel", "arbitrary")),
    )(*args)


PROJ_TN = 256


def _in_proj_tiles(h, win_t, tiles, carry, name, after=None):
    L, K = h.shape
    extra = [a for a in (carry, after) if a is not None]

    def body(t_ref, h_ref, w_ref, *rest):
        rest[len(extra)][...] = _dot_nt(h_ref[...], w_ref[...]).astype(BF16)

    return pl.pallas_call(
        body, name=name,
        grid_spec=pltpu.PrefetchScalarGridSpec(
            num_scalar_prefetch=1, grid=(tiles.shape[0],),
            in_specs=[pl.BlockSpec((L, K), lambda j, t: (0, 0)), pl.BlockSpec((PROJ_TN, K), lambda j, t: (t[j], 0))]
            + [pl.BlockSpec(memory_space=pl.ANY)] * len(extra),
            out_specs=pl.BlockSpec((L, PROJ_TN), lambda j, t: (0, t[j]))),
        out_shape=jax.ShapeDtypeStruct((L, win_t.shape[0]), BF16),
        input_output_aliases={} if carry is None else {3: 0},
        compiler_params=_cp(("arbitrary",)),
    )(tiles, h, win_t, *extra)


def _rms(x, w):
    return x * lax.rsqrt(jnp.mean(x * x, axis=-1, keepdims=True) + EPS) * w


def _rms_fwd(x, w, name):
    L, D = x.shape
    tm = min(L, 256)

    def body(x_ref, w_ref, h_ref):
        h_ref[...] = _rms(x_ref[...], w_ref[...]).astype(BF16)

    return pl.pallas_call(
        body, name=name, grid=(L // tm,),
        in_specs=[pl.BlockSpec((tm, D), lambda i: (i, 0)), pl.BlockSpec((1, D), lambda i: (0, 0))],
        out_specs=pl.BlockSpec((tm, D), lambda i: (i, 0)),
        out_shape=jax.ShapeDtypeStruct((L, D), BF16),
        compiler_params=_cp(("parallel",)),
    )(x, w)


def _rms_bwd(x, w, dh, dres, name):
    L, D = x.shape
    tm = min(L, 256)

    def body(x_ref, w_ref, dh_ref, dres_ref, dx_ref, dw_ref):
        _, vjp = jax.vjp(_rms, x_ref[...], w_ref[...])
        dx, dw = vjp(dh_ref[...])
        dx_ref[...] = dx + dres_ref[...]

        @pl.when(pl.program_id(0) == 0)
        def _():
            dw_ref[...] = jnp.zeros_like(dw_ref)

        dw_ref[...] += dw

    row = pl.BlockSpec((tm, D), lambda i: (i, 0))
    vec = pl.BlockSpec((1, D), lambda i: (0, 0))
    return pl.pallas_call(
        body, name=name, grid=(L // tm,),
        in_specs=[row, vec, row, row],
        out_specs=[row, vec],
        out_shape=[jax.ShapeDtypeStruct((L, D), F32), jax.ShapeDtypeStruct((1, D), F32)],
        compiler_params=_cp(("arbitrary",)),
    )(x, w, dh, dres)


def _final(x, fw, tgt, name):
    L, D = x.shape
    tm = min(L, 256)

    def loss_fn(xv, wv, tv):
        err = _rms(xv, wv) - tv
        return jnp.sum(err * err) * (0.5 / D)

    def body(x_ref, w_ref, t_ref, loss_ref, dx_ref, dw_ref):
        tv = t_ref[...]
        val, vjp = jax.vjp(lambda a, b: loss_fn(a, b, tv), x_ref[...], w_ref[...])
        dx, dw = vjp(jnp.ones((), F32))
        dx_ref[...] = dx

        @pl.when(pl.program_id(0) == 0)
        def _():
            dw_ref[...] = jnp.zeros_like(dw_ref)
            loss_ref[...] = jnp.zeros_like(loss_ref)

        dw_ref[...] += dw
        loss_ref[...] += jnp.full(loss_ref.shape, val, F32)

    row = pl.BlockSpec((tm, D), lambda i: (i, 0))
    vec = pl.BlockSpec((1, D), lambda i: (0, 0))
    return pl.pallas_call(
        body, name=name, grid=(L // tm,),
        in_specs=[row, vec, row],
        out_specs=[pl.BlockSpec((8, 128), lambda i: (0, 0)), row, vec],
        out_shape=[jax.ShapeDtypeStruct((8, 128), F32), jax.ShapeDtypeStruct((L, D), F32),
                   jax.ShapeDtypeStruct((1, D), F32)],
        compiler_params=_cp(("arbitrary",)),
    )(x, fw, tgt)


def _s5_param_fn(a_re, a_im, log_dt, bt_re, bt_im):
    dt = jnp.exp(log_dt)
    zr, zi = a_re * dt, a_im * dt
    er = jnp.exp(zr)
    lr, li = er * jnp.cos(zi), er * jnp.sin(zi)
    nr, ni = lr - 1.0, li
    den = a_re * a_re + a_im * a_im
    cr = (nr * a_re + ni * a_im) / den
    ci = (ni * a_re - nr * a_im) / den
    bbr = cr[None] * bt_re - ci[None] * bt_im
    bbi = cr[None] * bt_im + ci[None] * bt_re
    return lr, li, bbr, bbi


def _s5_params_fwd(a_re, a_im, log_dt, bt_re, bt_im, name):
    def body(ar, ai, ld, br, bi, lr, li, bbr, bbi):
        o = _s5_param_fn(ar[...], ai[...], ld[...], br[...], bi[...])
        lr[...], li[...], bbr[...], bbi[...] = o

    gp = jax.ShapeDtypeStruct(a_re.shape, F32)
    cgp = jax.ShapeDtypeStruct(bt_re.shape, F32)
    return pl.pallas_call(body, name=name, out_shape=[gp, gp, cgp, cgp])(a_re, a_im, log_dt, bt_re, bt_im)


def _s5_params_bwd(a_re, a_im, log_dt, bt_re, bt_im, dlr, dli, dbbr, dbbi, name):
    def body(ar, ai, ld, br, bi, g0, g1, g2, g3, o0, o1, o2, o3, o4):
        _, vjp = jax.vjp(_s5_param_fn, ar[...], ai[...], ld[...], br[...], bi[...])
        o0[...], o1[...], o2[...], o3[...], o4[...] = vjp((g0[...], g1[...], g2[...], g3[...]))

    gp = jax.ShapeDtypeStruct(a_re.shape, F32)
    cgp = jax.ShapeDtypeStruct(bt_re.shape, F32)
    return pl.pallas_call(body, name=name,
                          out_shape=[gp, gp, jax.ShapeDtypeStruct(log_dt.shape, F32), cgp, cgp])(
        a_re, a_im, log_dt, bt_re, bt_im, dlr, dli, dbbr, dbbi)


def _cmul(ar, ai, br, bi):
    return ar * br - ai * bi, ar * bi + ai * br


def _cpow(lr, li, n):
    rr, ri = None, None
    br, bi = lr, li
    while n:
        if n & 1:
            rr, ri = (br, bi) if rr is None else _cmul(rr, ri, br, bi)
        n >>= 1
        if n:
            br, bi = _cmul(br, bi, br, bi)
    return rr, ri


def _shift_rows(x, up):
    row = lax.broadcasted_iota(jnp.int32, x.shape, 0)
    if up:
        return jnp.where(row == SUB - 1, 0.0, pltpu.roll(x, SUB - 1, 0))
    return jnp.where(row == 0, 0.0, pltpu.roll(x, 1, 0))


NT = SLAB_ST // 128


def _lam_tiles(lr_ref, li_ref):
    return [(lr_ref[:, j * 128:(j + 1) * 128], li_ref[:, j * 128:(j + 1) * 128]) for j in range(NT)]


def _row_on_sublanes(ref, j, t):
    return ref[j, pl.ds(t, SUB, stride=0), :]


def _pow_table(pw_re, pw_im, lam_t, seg):
    assert seg % 8 == 0 and (seg // 8) & (seg // 8 - 1) == 0
    for j in range(NT):
        lr, li = lam_t[j][0][0:1], lam_t[j][1][0:1]
        r, i_ = lr, li
        for row in range(8):
            pw_re[j, row:row + 1, :] = r
            pw_im[j, row:row + 1, :] = i_
            if row < 7:
                r, i_ = _cmul(r, i_, lr, li)
        n = 8
        while n < seg:
            qr, qi = _cpow(lr, li, n)
            nr, ni = _cmul(pw_re[j, 0:n, :], pw_im[j, 0:n, :], qr, qi)
            pw_re[j, n:2 * n, :] = nr
            pw_im[j, n:2 * n, :] = ni
            n *= 2


def _seg_scan(s_re, s_im, lam_t, pw_re, pw_im, seg, reverse, prev=None):
    sgn = -1.0 if reverse else 1.0
    lt = [(lr, sgn * li) for lr, li in lam_t]
    tiles = [(g, j) for g in range(N_GRP) for j in range(NT)]
    zeros = jnp.zeros((SUB, 128), F32)

    def rows(g, i):
        return pl.ds(pl.multiple_of((g * seg + i) * SUB, SUB), SUB)

    def step1(t, carry):
        i = seg - 1 - t if reverse else t
        out = []
        for n, (g, j) in enumerate(tiles):
            nr, ni = _cmul(lt[j][0], lt[j][1], carry[2 * n], carry[2 * n + 1])
            nr = nr + s_re[j, rows(g, i), :]
            ni = ni + s_im[j, rows(g, i), :]
            s_re[j, rows(g, i), :] = nr
            s_im[j, rows(g, i), :] = ni
            out += [nr, ni]
        return tuple(out)

    zero = tuple(zeros for _ in range(2 * len(tiles)))
    ends = lax.fori_loop(0, seg, step1, zero, unroll=2)

    carries = [None] * (2 * len(tiles))
    row = lax.broadcasted_iota(jnp.int32, (SUB, 128), 0)
    dist = (SUB - 1 - row) if reverse else row
    edge = 0 if reverse else SUB - 1
    for j in range(NT):
        pr, pi = _cpow(lt[j][0], lt[j][1], seg)
        qr, qi = jnp.ones((SUB, 128), F32), zeros
        for s in range(1, SUB):
            tr, ti = _cmul(qr, qi, pr, pi)
            qr, qi = jnp.where(dist >= s, tr, qr), jnp.where(dist >= s, ti, qi)
        boundary = None
        for g in (reversed(range(N_GRP)) if reverse else range(N_GRP)):
            n = g * NT + j
            cr, ci = zeros, zeros
            for _ in range(SUB - 1):
                tr, ti = _cmul(pr, pi, cr, ci)
                cr = _shift_rows(tr + ends[2 * n], reverse)
                ci = _shift_rows(ti + ends[2 * n + 1], reverse)
            if boundary is not None:
                tr, ti = _cmul(qr, qi, boundary[0], boundary[1])
                cr, ci = cr + tr, ci + ti
            carries[2 * n], carries[2 * n + 1] = cr, ci
            fr, fi = _cmul(pr, pi, cr, ci)
            boundary = (jnp.broadcast_to((fr + ends[2 * n])[edge:edge + 1], (SUB, 128)),
                        jnp.broadcast_to((fi + ends[2 * n + 1])[edge:edge + 1], (SUB, 128)))

    def fix(t, i, acc, before):
        out = []
        pws = [(_row_on_sublanes(pw_re, j, t), sgn * _row_on_sublanes(pw_im, j, t)) for j in range(NT)]
        for n, (g, j) in enumerate(tiles):
            ar, ai = _cmul(pws[j][0], pws[j][1], carries[2 * n], carries[2 * n + 1])
            ar = ar + s_re[j, rows(g, i), :]
            ai = ai + s_im[j, rows(g, i), :]
            s_re[j, rows(g, i), :] = ar
            s_im[j, rows(g, i), :] = ai
            if before is not None:
                qr, qi = before(n)
                out += [acc[2 * n] + ar * qr + ai * qi, acc[2 * n + 1] + ai * qr - ar * qi]
        return tuple(out)

    if prev is None:
        lax.fori_loop(0, seg, lambda t, c: fix(t, seg - 1 - t if reverse else t, c, None), (), unroll=2)
        return carries
    assert reverse
    p_re, p_im, p_carries = prev

    def earlier(t):
        return lambda n: (p_re[tiles[n][1], rows(tiles[n][0], seg - 2 - t), :],
                          p_im[tiles[n][1], rows(tiles[n][0], seg - 2 - t), :])

    acc = lax.fori_loop(0, seg - 1, lambda t, c: fix(t, seg - 1 - t, c, earlier(t)), zero)
    acc = fix(seg - 1, 0, acc, lambda n: (p_carries[2 * n], p_carries[2 * n + 1]))
    return carries, [sum(acc[2 * (g * NT + j) + part] for g in range(N_GRP)) for j in range(NT) for part in range(2)]


S5_RB = 256


def _seg_slice(k, seg):
    g, r = divmod(k, SUB)
    return pl.ds(g * seg * SUB + r, seg, stride=SUB)


def _to_step_major(src_ref, dst_ref, seg):
    for k in range(N_SEG):
        dst_ref[_seg_slice(k, seg), :] = src_ref[pl.ds(k * seg, seg), :].astype(F32)


def _from_step_major(src_ref, dst_ref, seg):
    for k in range(N_SEG):
        dst_ref[pl.ds(k * seg, seg), :] = src_ref[_seg_slice(k, seg), :].astype(dst_ref.dtype)


def _blocks(L):
    rb = min(S5_RB, L)
    return [pl.ds(b * rb, rb) for b in range(L // rb)]


def _lanes_of(ref, rows):
    return jnp.concatenate([ref[j, rows, :] for j in range(NT)], axis=-1)


def _lanes_to(ref, rows, val):
    for j in range(NT):
        ref[j, rows, :] = val[:, j * 128:(j + 1) * 128]


def _s5_specs(L):
    col = lambda off: pl.BlockSpec((L, SLAB_CH), lambda j: (0, off + j))
    mat_b = pl.BlockSpec((None, SLAB_CH, SLAB_ST), lambda j: (j, 0, 0))
    mat_c = pl.BlockSpec((None, SLAB_ST, SLAB_CH), lambda j: (j, 0, 0))
    vec_s = pl.BlockSpec((None, SUB, SLAB_ST), lambda j: (j, 0, 0))
    vec_c = pl.BlockSpec((None, 1, SLAB_CH), lambda j: (j, 0, 0))
    return col, mat_b, mat_c, vec_s, vec_c


def _s5_states(u_ref, u_sm, bre_ref, bim_ref, lam_t, pw_re, pw_im, s_re, s_im, seg):
    _pow_table(pw_re, pw_im, lam_t, seg)
    _to_step_major(u_ref, u_sm, seg)
    for rows in _blocks(u_sm.shape[0]):
        ub = u_sm[rows, :].astype(BF16)
        _lanes_to(s_re, rows, _dot(ub, bre_ref[...]))
        _lanes_to(s_im, rows, _dot(ub, bim_ref[...]))
    return _seg_scan(s_re, s_im, lam_t, pw_re, pw_im, seg, reverse=False)


def _s5_fwd(proj, bre, bim, cre_t, cim_t, lam_re, lam_im, dvec, name):
    L = proj.shape[0]
    seg = L // N_SEG
    col, mat_b, mat_c, vec_s, vec_c = _s5_specs(L)

    def body(u_ref, bre_ref, bim_ref, cre_ref, cim_ref, lr_ref, li_ref, d_ref, y_ref, s_re, s_im, pw_re, pw_im, u_sm, y_sm):
        _s5_states(u_ref, u_sm, bre_ref, bim_ref, _lam_tiles(lr_ref, li_ref), pw_re, pw_im, s_re, s_im, seg)
        for rows in _blocks(L):
            y = (_dot(_lanes_of(s_re, rows).astype(BF16), cre_ref[...])
                 - _dot(_lanes_of(s_im, rows).astype(BF16), cim_ref[...]))
            y_sm[rows, :] = jax.nn.gelu(y + d_ref[...] * u_sm[rows, :])
        _from_step_major(y_sm, y_ref, seg)

    lane_tile = pltpu.VMEM((L, SLAB_CH), F32)
    return pl.pallas_call(
        body, name=name, grid=(N_SLAB,),
        in_specs=[col(OFF_UA // SLAB_CH), mat_b, mat_b, mat_c, mat_c, vec_s, vec_s, vec_c],
        out_specs=pl.BlockSpec((L, SLAB_CH), lambda j: (0, j)),
        out_shape=jax.ShapeDtypeStruct((L, SSM_WIDTH), BF16),
        scratch_shapes=[pltpu.VMEM((NT, L, 128), F32)] * 2 + [pltpu.VMEM((NT, seg, 128), F32)] * 2 + [lane_tile] * 2,
        compiler_params=_cp(("parallel",)),
    )(proj, bre, bim, cre_t, cim_t, lam_re, lam_im, dvec)


def _s5_bwd(proj, dy, bre, bim, cre_t, cim_t, lam_re, lam_im, dvec, name):
    L = proj.shape[0]
    seg = L // N_SEG
    col, mat_b, mat_c, vec_s, vec_c = _s5_specs(L)
    dlam_spec = pl.BlockSpec((None, 1, SLAB_ST), lambda j: (j, 0, 0))

    def body(u_ref, dy_ref, bre_ref, bim_ref, cre_ref, cim_ref, lr_ref, li_ref, d_ref,
             du_ref, dbre_ref, dbim_ref, dcre_ref, dcim_ref, dlr_ref, dli_ref, dd_ref,
             s_re, s_im, a_re, a_im, pw_re, pw_im, u_sm, dyp, io_sm):
        lam_t = _lam_tiles(lr_ref, li_ref)
        carry_s = _s5_states(u_ref, u_sm, bre_ref, bim_ref, lam_t, pw_re, pw_im, s_re, s_im, seg)
        _to_step_major(dy_ref, io_sm, seg)
        dcre = jnp.zeros((SLAB_ST, SLAB_CH), F32)
        dcim = jnp.zeros((SLAB_ST, SLAB_CH), F32)
        dd = jnp.zeros((1, SLAB_CH), F32)
        for rows in _blocks(L):
            sre = _lanes_of(s_re, rows).astype(BF16)
            sim = _lanes_of(s_im, rows).astype(BF16)
            uk = u_sm[rows, :]
            ypre = _dot(sre, cre_ref[...]) - _dot(sim, cim_ref[...]) + d_ref[...] * uk
            _, vjp = jax.vjp(jax.nn.gelu, ypre)
            (dyk,) = vjp(io_sm[rows, :])
            dyp[rows, :] = dyk
            dd = dd + jnp.sum(dyk * uk, axis=0, keepdims=True)
            dyb = dyk.astype(BF16)
            dcre = dcre + _dot_tn(sre, dyb)
            dcim = dcim - _dot_tn(sim, dyb)
            _lanes_to(a_re, rows, _dot_nt(dyb, cre_ref[...]))
            _lanes_to(a_im, rows, -_dot_nt(dyb, cim_ref[...]))
        dcre_ref[...] = dcre
        dcim_ref[...] = dcim
        dd_ref[...] = dd

        _, acc = _seg_scan(a_re, a_im, lam_t, pw_re, pw_im, seg, reverse=True, prev=(s_re, s_im, carry_s))
        dlr_ref[...] = jnp.concatenate([jnp.sum(acc[2 * j], axis=0, keepdims=True) for j in range(NT)], axis=-1)
        dli_ref[...] = jnp.concatenate([jnp.sum(acc[2 * j + 1], axis=0, keepdims=True) for j in range(NT)], axis=-1)

        dbre = jnp.zeros((SLAB_CH, SLAB_ST), F32)
        dbim = jnp.zeros((SLAB_CH, SLAB_ST), F32)
        for rows in _blocks(L):
            are = _lanes_of(a_re, rows).astype(BF16)
            aim = _lanes_of(a_im, rows).astype(BF16)
            ub = u_sm[rows, :].astype(BF16)
            io_sm[rows, :] = _dot_nt(are, bre_ref[...]) + _dot_nt(aim, bim_ref[...]) + dyp[rows, :] * d_ref[...]
            dbre = dbre + _dot_tn(ub, are)
            dbim = dbim + _dot_tn(ub, aim)
        _from_step_major(io_sm, du_ref, seg)
        dbre_ref[...] = dbre
        dbim_ref[...] = dbim

    scan_buf = pltpu.VMEM((NT, L, 128), F32)
    pow_buf = pltpu.VMEM((NT, seg, 128), F32)
    lane_tile = pltpu.VMEM((L, SLAB_CH), F32)
    return pl.pallas_call(
        body, name=name, grid=(N_SLAB,),
        in_specs=[col(OFF_UA // SLAB_CH), pl.BlockSpec((L, SLAB_CH), lambda j: (0, j)),
                  mat_b, mat_b, mat_c, mat_c, vec_s, vec_s, vec_c],
        out_specs=[pl.BlockSpec((L, SLAB_CH), lambda j: (0, j)), mat_b, mat_b, mat_c, mat_c, dlam_spec, dlam_spec, vec_c],
        out_shape=[jax.ShapeDtypeStruct((L, SSM_WIDTH), BF16),
                   jax.ShapeDtypeStruct((N_SLAB, SLAB_CH, SLAB_ST), F32),
                   jax.ShapeDtypeStruct((N_SLAB, SLAB_CH, SLAB_ST), F32),
                   jax.ShapeDtypeStruct((N_SLAB, SLAB_ST, SLAB_CH), F32),
                   jax.ShapeDtypeStruct((N_SLAB, SLAB_ST, SLAB_CH), F32),
                   jax.ShapeDtypeStruct((N_SLAB, 1, SLAB_ST), F32),
                   jax.ShapeDtypeStruct((N_SLAB, 1, SLAB_ST), F32),
                   jax.ShapeDtypeStruct((N_SLAB, 1, SLAB_CH), F32)],
        scratch_shapes=[scan_buf, scan_buf, scan_buf, scan_buf, pow_buf, pow_buf, lane_tile, lane_tile, lane_tile],
        compiler_params=_cp(("parallel",)),
    )(proj, dy, bre, bim, cre_t, cim_t, lam_re, lam_im, dvec)


def _glu_point(y0, pre, za, b):
    return y0 * jax.nn.sigmoid(pre + b) * jax.nn.silu(za)


def _glu_specs(L, tm):
    row = pl.BlockSpec((tm, SSM_WIDTH), lambda i: (i, 0))
    za = pl.BlockSpec((tm, SSM_WIDTH), lambda i: (i, OFF_ZA // SSM_WIDTH))
    wmat = pl.BlockSpec((SSM_WIDTH, SSM_WIDTH), lambda i: (0, 0))
    vec = pl.BlockSpec((1, SSM_WIDTH), lambda i: (0, 0))
    return row, za, wmat, vec


def _glu_fwd(ya0, proj, w, b, name):
    L = ya0.shape[0]
    tm = min(L, 512)
    row, za, wmat, vec = _glu_specs(L, tm)

    def body(y_ref, z_ref, w_ref, b_ref, o_ref):
        y0 = y_ref[...]
        pre = _dot(y0, w_ref[...])
        o_ref[...] = _glu_point(y0.astype(F32), pre, z_ref[...].astype(F32), b_ref[...]).astype(BF16)

    return pl.pallas_call(
        body, name=name, grid=(L // tm,), in_specs=[row, za, wmat, vec], out_specs=row,
        out_shape=jax.ShapeDtypeStruct((L, SSM_WIDTH), BF16), compiler_params=_cp(("parallel",)),
    )(ya0, proj, w, b)


def _glu_bwd(ya0, proj, w, b, dya, name):
    L = ya0.shape[0]
    tm = min(L, 512)
    row, za, wmat, vec = _glu_specs(L, tm)

    def body(y_ref, z_ref, w_ref, b_ref, g_ref, dy0_ref, dza_ref, dw_ref, db_ref):
        y0 = y_ref[...]
        pre = _dot(y0, w_ref[...])
        _, vjp = jax.vjp(_glu_point, y0.astype(F32), pre, z_ref[...].astype(F32), b_ref[...])
        dy0, dpre, dza, db = vjp(g_ref[...].astype(F32))
        dpb = dpre.astype(BF16)
        dy0_ref[...] = (dy0 + _dot_nt(dpb, w_ref[...])).astype(BF16)
        dza_ref[...] = dza.astype(BF16)

        @pl.when(pl.program_id(0) == 0)
        def _():
            dw_ref[...] = jnp.zeros_like(dw_ref)
            db_ref[...] = jnp.zeros_like(db_ref)

        dw_ref[...] += _dot_tn(y0, dpb)
        db_ref[...] += db

    return pl.pallas_call(
        body, name=name, grid=(L // tm,), in_specs=[row, za, wmat, vec, row],
        out_specs=[row, row, wmat, vec],
        out_shape=[jax.ShapeDtypeStruct((L, SSM_WIDTH), BF16), jax.ShapeDtypeStruct((L, SSM_WIDTH), BF16),
                   jax.ShapeDtypeStruct((SSM_WIDTH, SSM_WIDTH), F32), jax.ShapeDtypeStruct((1, SSM_WIDTH), F32)],
        compiler_params=_cp(("arbitrary",)),
    )(ya0, proj, w, b, dya)


def _sg_norm(vb, ln_w, ln_b):
    v0 = jax.nn.gelu(vb)
    mu = jnp.mean(v0, axis=-1, keepdims=True)
    var = jnp.mean(jnp.square(v0 - mu), axis=-1, keepdims=True)
    return (v0 - mu) * lax.rsqrt(var + EPS) * ln_w + ln_b


def _sg_gate(ub, mixed, zb):
    return jax.nn.gelu(ub) * mixed * jax.nn.silu(zb)


def _sg_specs():
    W = SSM_WIDTH
    blk = lambda off: pl.BlockSpec((CHUNK, W), lambda n: (n, off // W))
    out = pl.BlockSpec((CHUNK, W), lambda n: (n, 0))
    vec = pl.BlockSpec((1, W), lambda n: (0, 0))
    wsp = pl.BlockSpec((SG_HEADS, CHUNK, CHUNK), lambda n: (0, 0, 0))
    bsp = pl.BlockSpec((SG_HEADS, CHUNK, 1), lambda n: (0, 0, 0))
    return blk, out, vec, wsp, bsp


def _sg_masked(w_ref):
    t = lax.broadcasted_iota(jnp.int32, (CHUNK, CHUNK), 0)
    s = lax.broadcasted_iota(jnp.int32, (CHUNK, CHUNK), 1)
    causal = s <= t
    return causal, [jnp.where(causal, w_ref[h], 0.0).astype(BF16) for h in range(SG_HEADS)]


def _sg_mix(wm, vnb, bias_ref):
    return jnp.concatenate(
        [_dot(wm[h], vnb[:, h * CHUNK:(h + 1) * CHUNK]) + bias_ref[h] for h in range(SG_HEADS)], axis=-1)


def _sg_fwd(proj, ln_w, ln_b, w, bias, name):
    L = proj.shape[0]
    blk, out, vec, wsp, bsp = _sg_specs()

    def body(ub_ref, vb_ref, zb_ref, lw_ref, lb_ref, w_ref, bias_ref, o_ref):
        _, wm = _sg_masked(w_ref)
        vnb = _sg_norm(vb_ref[...].astype(F32), lw_ref[...], lb_ref[...]).astype(BF16)
        mixed = _sg_mix(wm, vnb, bias_ref)
        o_ref[...] = _sg_gate(ub_ref[...].astype(F32), mixed, zb_ref[...].astype(F32)).astype(BF16)

    return pl.pallas_call(
        body, name=name, grid=(L // CHUNK,),
        in_specs=[blk(OFF_UB), blk(OFF_VB), blk(OFF_ZB), vec, vec, wsp, bsp], out_specs=out,
        out_shape=jax.ShapeDtypeStruct((L, SSM_WIDTH), BF16), compiler_params=_cp(("parallel",)),
    )(proj, proj, proj, ln_w, ln_b, w, bias)


def _sg_bwd(proj, ln_w, ln_b, w, bias, dyb, name):
    L = proj.shape[0]
    blk, out, vec, wsp, bsp = _sg_specs()

    def body(ub_ref, vb_ref, zb_ref, lw_ref, lb_ref, w_ref, bias_ref, g_ref,
             dub_ref, dvb_ref, dzb_ref, dlw_ref, dlb_ref, dw_ref, dbias_ref):
        causal, wm = _sg_masked(w_ref)
        vb = vb_ref[...].astype(F32)
        vn, vjp_norm = jax.vjp(_sg_norm, vb, lw_ref[...], lb_ref[...])
        vnb = vn.astype(BF16)
        mixed = _sg_mix(wm, vnb, bias_ref)
        _, vjp_gate = jax.vjp(_sg_gate, ub_ref[...].astype(F32), mixed, zb_ref[...].astype(F32))
        dub, dmixed, dzb = vjp_gate(g_ref[...].astype(F32))
        dub_ref[...] = dub.astype(BF16)
        dzb_ref[...] = dzb.astype(BF16)

        @pl.when(pl.program_id(0) == 0)
        def _():
            dlw_ref[...] = jnp.zeros_like(dlw_ref)
            dlb_ref[...] = jnp.zeros_like(dlb_ref)
            dw_ref[...] = jnp.zeros_like(dw_ref)
            dbias_ref[...] = jnp.zeros_like(dbias_ref)

        dvn = []
        for h in range(SG_HEADS):
            dm = dmixed[:, h * CHUNK:(h + 1) * CHUNK]
            dmb = dm.astype(BF16)
            dbias_ref[h] += jnp.sum(dm, axis=-1, keepdims=True)
            dw_ref[h] += jnp.where(causal, _dot_nt(dmb, vnb[:, h * CHUNK:(h + 1) * CHUNK]), 0.0)
            dvn.append(_dot_tn(wm[h], dmb))
        dvb, dlw, dlb = vjp_norm(jnp.concatenate(dvn, axis=-1))
        dvb_ref[...] = dvb.astype(BF16)
        dlw_ref[...] += dlw
        dlb_ref[...] += dlb

    act = jax.ShapeDtypeStruct((L, SSM_WIDTH), BF16)
    return pl.pallas_call(
        body, name=name, grid=(L // CHUNK,),
        in_specs=[blk(OFF_UB), blk(OFF_VB), blk(OFF_ZB), vec, vec, wsp, bsp, out],
        out_specs=[out, out, out, vec, vec, wsp, bsp],
        out_shape=[act, act, act, jax.ShapeDtypeStruct((1, SSM_WIDTH), F32), jax.ShapeDtypeStruct((1, SSM_WIDTH), F32),
                   jax.ShapeDtypeStruct((SG_HEADS, CHUNK, CHUNK), F32), jax.ShapeDtypeStruct((SG_HEADS, CHUNK, 1), F32)],
        compiler_params=_cp(("arbitrary",)),
    )(proj, proj, proj, ln_w, ln_b, w, bias, dyb)


def _rope_tables(L):
    half = ROT_DIM // 2
    inv_freq = ROPE_THETA ** (-jnp.arange(0, ROT_DIM, 2, dtype=F32) / ROT_DIM)
    ang = jnp.arange(L, dtype=F32)[:, None] * inv_freq[None, :]
    cos, sin = jnp.cos(ang), jnp.sin(ang)
    ones = jnp.ones((L, HEAD_DIM - ROT_DIM), F32)
    cos_h = jnp.concatenate([cos, cos, ones], axis=-1)
    sin_h = jnp.concatenate([-sin, sin, 0.0 * ones], axis=-1)
    src = jnp.arange(HEAD_DIM)[:, None]
    dst = jnp.arange(HEAD_DIM)[None, :]
    p_h = (((dst < half) & (src == dst + half)) | ((dst >= half) & (dst < ROT_DIM) & (src == dst - half))).astype(F32)
    p2 = jnp.kron(jnp.eye(2, dtype=F32), p_h).astype(BF16)
    return jnp.tile(cos_h, (1, 2)), jnp.tile(sin_h, (1, 2)), p2


def _rope(t, cos, sin, p2):
    n = t.shape[1] // 128
    tb = t.astype(BF16)
    sw = jnp.concatenate([_dot(tb[:, i * 128:(i + 1) * 128], p2) for i in range(n)], axis=-1) if n > 1 else _dot(tb, p2)
    return t * jnp.tile(cos, (1, n)) + sw * jnp.tile(sin, (1, n))


def _rope_t(g, cos, sin, p2):
    n = g.shape[1] // 128
    gs = (g * jnp.tile(sin, (1, n))).astype(BF16)
    sw = jnp.concatenate([_dot_nt(gs[:, i * 128:(i + 1) * 128], p2) for i in range(n)], axis=-1) if n > 1 else _dot_nt(gs, p2)
    return g * jnp.tile(cos, (1, n)) + sw


def _lane_lo(shape):
    return (lax.broadcasted_iota(jnp.int32, shape, len(shape) - 1) % 128) < HEAD_DIM


def _dup_halves(x):
    xr = pltpu.roll(x, HEAD_DIM, 1)
    lo = _lane_lo(x.shape)
    return jnp.where(lo, x, xr), jnp.where(lo, xr, x)


def _fold_halves(d0, d1):
    f0 = d0 + pltpu.roll(d0, HEAD_DIM, 1)
    f1 = d1 + pltpu.roll(d1, HEAD_DIM, 1)
    return jnp.where(_lane_lo(d0.shape), f0, f1)


def _attn_mask():
    qi = lax.broadcasted_iota(jnp.int32, (CHUNK, 2 * CHUNK), 0)
    kj = lax.broadcasted_iota(jnp.int32, (CHUNK, 2 * CHUNK), 1)
    return qi, kj


def _attn_specs():
    qsp = pl.BlockSpec((CHUNK, 1024), lambda n: (n, OFF_Q // 1024))
    kv_cur = pl.BlockSpec((CHUNK, 256), lambda n: (n, OFF_KV // 256))
    kv_prev = pl.BlockSpec((CHUNK, 256), lambda n: (jnp.maximum(n - 1, 0), OFF_KV // 256))
    zsp = [pl.BlockSpec((CHUNK, 256), functools.partial(lambda n, q: (n, OFF_ZC // 256 + q), q=q)) for q in range(4)]
    tab_cur = pl.BlockSpec((CHUNK, 128), lambda n: (n, 0))
    tab_prev = pl.BlockSpec((CHUNK, 128), lambda n: (jnp.maximum(n - 1, 0), 0))
    p2sp = pl.BlockSpec((128, 128), lambda n: (0, 0))
    sink = pl.BlockSpec(memory_space=pltpu.SMEM)
    wide = pl.BlockSpec((CHUNK, 1024), lambda n: (n, 0))
    return qsp, kv_cur, kv_prev, zsp, tab_cur, tab_prev, p2sp, sink, wide


def _attn_prep(n, q_ref, kvc_ref, kvp_ref, cosc_ref, sinc_ref, cosp_ref, sinp_ref, p2_ref):
    p2 = p2_ref[...]
    qr = _rope(q_ref[...].astype(F32), cosc_ref[...], sinc_ref[...], p2).astype(BF16)
    kc = _rope(kvc_ref[:, 0:128].astype(F32), cosc_ref[...], sinc_ref[...], p2)
    kp = _rope(kvp_ref[:, 0:128].astype(F32), cosp_ref[...], sinp_ref[...], p2)
    k_all = jnp.concatenate([kp, kc], axis=0).astype(BF16)
    v_all = jnp.concatenate([kvp_ref[:, 128:256], kvc_ref[:, 128:256]], axis=0)
    qi, kj = _attn_mask()
    allowed = ((kj < CHUNK) & (kj > qi) & (n > 0)) | ((kj >= CHUNK) & (kj - CHUNK <= qi))
    return qr, _dup_halves(k_all), _dup_halves(v_all), allowed, _lane_lo((CHUNK, 128))


def _attn_head(qr, kd, sink_ref, h, allowed, lo):
    m, half, g = h // 2, h % 2, h // 8
    qp = qr[:, m * 128:(m + 1) * 128]
    qm = jnp.where(lo if half == 0 else ~lo, qp, jnp.zeros_like(qp))
    s = jnp.where(allowed, _dot_nt(qm, kd[g]) * (HEAD_DIM ** -0.5), NEG_INF)
    snk = sink_ref[h]
    mx = jnp.maximum(jnp.max(s, axis=-1, keepdims=True), snk)
    e = jnp.exp(s - mx)
    es = jnp.exp(snk - mx)
    inv = 1.0 / (jnp.sum(e, axis=-1, keepdims=True) + es)
    return qm, e * inv, es * inv


def _silu_gate(o, z):
    return o * jax.nn.silu(z)


def _pair_lanes(refs, m):
    return refs[m // 2][:, (m % 2) * 128:(m % 2 + 1) * 128]


def _attn_fwd(proj, sinks, tabs, name):
    L = proj.shape[0]
    cos2, sin2, p2 = tabs
    qsp, kv_cur, kv_prev, zsp, tab_cur, tab_prev, p2sp, sink, wide = _attn_specs()

    def body(q_ref, kvc_ref, kvp_ref, z0, z1, z2, z3, cosc, sinc, cosp, sinp, p2_ref, sink_ref, y_ref, o_ref):
        n = pl.program_id(0)
        qr, kd, vd, allowed, lo = _attn_prep(n, q_ref, kvc_ref, kvp_ref, cosc, sinc, cosp, sinp, p2_ref)
        probs = [_attn_head(qr, kd, sink_ref, h, allowed, lo)[1].astype(BF16) for h in range(ATT_HEADS)]
        for m in range(ATT_HEADS // 2):
            g = m // 4
            o0 = _dot(probs[2 * m], vd[g])
            o1 = _dot(probs[2 * m + 1], vd[g])
            o = jnp.where(lo, o0, o1).astype(BF16)
            o_ref[:, m * 128:(m + 1) * 128] = o
            z = _pair_lanes((z0, z1, z2, z3), m).astype(F32)
            y_ref[:, m * 128:(m + 1) * 128] = _silu_gate(o.astype(F32), z).astype(BF16)

    act = jax.ShapeDtypeStruct((L, 1024), BF16)
    return pl.pallas_call(
        body, name=name, grid=(L // CHUNK,),
        in_specs=[qsp, kv_cur, kv_prev, *zsp, tab_cur, tab_cur, tab_prev, tab_prev, p2sp, sink],
        out_specs=[wide, wide], out_shape=[act, act], compiler_params=_cp(("parallel",)),
    )(proj, proj, proj, proj, proj, proj, proj, cos2, sin2, cos2, sin2, p2, sinks)


def _attn_bwd(proj, sinks, tabs, o_att, dyc, name):
    L = proj.shape[0]
    cos2, sin2, p2 = tabs
    qsp, kv_cur, kv_prev, zsp, tab_cur, tab_prev, p2sp, sink, wide = _attn_specs()
    kvo = pl.BlockSpec((CHUNK, 256), lambda n: (n, 0))

    def body(q_ref, kvc_ref, kvp_ref, z0, z1, z2, z3, cosc, sinc, cosp, sinp, p2_ref, sink_ref, o_ref, g_ref,
             dq_ref, dz_ref, dkvc_ref, dkvp_ref, dsink_ref):
        n = pl.program_id(0)
        qr, kd, vd, allowed, lo = _attn_prep(n, q_ref, kvc_ref, kvp_ref, cosc, sinc, cosp, sinp, p2_ref)
        p2 = p2_ref[...]

        @pl.when(n == 0)
        def _():
            dsink_ref[...] = jnp.zeros_like(dsink_ref)

        dkd = [jnp.zeros((2 * CHUNK, 128), F32), jnp.zeros((2 * CHUNK, 128), F32)]
        dvd = [jnp.zeros((2 * CHUNK, 128), F32), jnp.zeros((2 * CHUNK, 128), F32)]
        probs = [_attn_head(qr, kd, sink_ref, h, allowed, lo) for h in range(ATT_HEADS)]
        for m in range(ATT_HEADS // 2):
            g = m // 4
            lanes = slice(m * 128, (m + 1) * 128)
            z = _pair_lanes((z0, z1, z2, z3), m).astype(F32)
            _, vjp = jax.vjp(_silu_gate, o_ref[:, lanes].astype(F32), z)
            do, dz = vjp(g_ref[:, lanes].astype(F32))
            dz_ref[:, lanes] = dz.astype(BF16)
            dop = do.astype(BF16)
            dq_h = []
            for half in range(2):
                h = 2 * m + half
                qm, p, ps = probs[h]
                dom = jnp.where(lo if half == 0 else ~lo, dop, jnp.zeros_like(dop))
                dp = _dot_nt(dom, vd[g])
                rs = jnp.sum(p * dp, axis=-1, keepdims=True)
                ds = (p * (dp - rs) * (HEAD_DIM ** -0.5)).astype(BF16)
                dsink_ref[h:h + 1, :] += jnp.broadcast_to(jnp.sum(-ps * rs, axis=0, keepdims=True), (1, 128))
                dq_h.append(_dot(ds, kd[g]))
                dkd[g] = dkd[g] + _dot_tn(ds, qm)
                dvd[g] = dvd[g] + _dot_tn(p.astype(BF16), dom)
            dq_ref[:, lanes] = _rope_t(jnp.where(lo, dq_h[0], dq_h[1]), cosc[...], sinc[...], p2).astype(BF16)
        dk_rot = _fold_halves(dkd[0], dkd[1])
        dv = _fold_halves(dvd[0], dvd[1])
        dkp = _rope_t(dk_rot[0:CHUNK], cosp[...], sinp[...], p2)
        dkc = _rope_t(dk_rot[CHUNK:2 * CHUNK], cosc[...], sinc[...], p2)
        dkvp_ref[...] = jnp.concatenate([dkp, dv[0:CHUNK]], axis=-1)
        dkvc_ref[...] = jnp.concatenate([dkc, dv[CHUNK:2 * CHUNK]], axis=-1)

    act = jax.ShapeDtypeStruct((L, 1024), BF16)
    kvs = jax.ShapeDtypeStruct((L, 256), F32)
    return pl.pallas_call(
        body, name=name, grid=(L // CHUNK,),
        in_specs=[qsp, kv_cur, kv_prev, *zsp, tab_cur, tab_cur, tab_prev, tab_prev, p2sp, sink, wide, wide],
        out_specs=[wide, wide, kvo, kvo, pl.BlockSpec((ATT_HEADS, 128), lambda n: (0, 0))],
        out_shape=[act, act, kvs, kvs, jax.ShapeDtypeStruct((ATT_HEADS, 128), F32)],
        compiler_params=_cp(("arbitrary",)),
    )(proj, proj, proj, proj, proj, proj, proj, cos2, sin2, cos2, sin2, p2, sinks, o_att, dyc)


MERGE_TN = 256


def _merge_point(ta, tb, tc, ga, gb, gc):
    return jax.nn.sigmoid(ga) * ta + jax.nn.sigmoid(gb) * tb + jax.nn.sigmoid(gc) * tc


def _merge_specs(tm):
    nj = D_MODEL // MERGE_TN
    t = pl.BlockSpec((tm, MERGE_TN), lambda i, j: (i, j))
    gates = [pl.BlockSpec((tm, MERGE_TN), functools.partial(lambda i, j, b: (i, OFF_G // MERGE_TN + b * nj + j), b=b))
             for b in range(3)]
    return t, gates, nj


def _merge_fwd(ta, tb, tc, proj, name):
    L = ta.shape[0]
    tm = min(L, 1024)
    t, gates, nj = _merge_specs(tm)

    def body(ta_ref, tb_ref, tc_ref, ga_ref, gb_ref, gc_ref, o_ref):
        f = lambda r: r[...].astype(F32)
        o_ref[...] = _merge_point(f(ta_ref), f(tb_ref), f(tc_ref), f(ga_ref), f(gb_ref), f(gc_ref)).astype(BF16)

    return pl.pallas_call(
        body, name=name, grid=(L // tm, nj), in_specs=[t, t, t, *gates], out_specs=t,
        out_shape=jax.ShapeDtypeStruct((L, D_MODEL), BF16), compiler_params=_cp(("parallel", "parallel")),
    )(ta, tb, tc, proj, proj, proj)


def _merge_bwd(ta, tb, tc, proj, dm, name):
    L = ta.shape[0]
    tm = min(L, 1024)
    t, gates, nj = _merge_specs(tm)

    def body(ta_ref, tb_ref, tc_ref, ga_ref, gb_ref, gc_ref, dm_ref, dta_ref, dtb_ref, dtc_ref, dga_ref, dgb_ref, dgc_ref):
        f = lambda r: r[...].astype(F32)
        _, vjp = jax.vjp(_merge_point, f(ta_ref), f(tb_ref), f(tc_ref), f(ga_ref), f(gb_ref), f(gc_ref))
        outs = vjp(f(dm_ref))
        for r, v in zip((dta_ref, dtb_ref, dtc_ref, dga_ref, dgb_ref, dgc_ref), outs):
            r[...] = v.astype(BF16)

    act = jax.ShapeDtypeStruct((L, D_MODEL), BF16)
    return pl.pallas_call(
        body, name=name, grid=(L // tm, nj), in_specs=[t, t, t, *gates, t],
        out_specs=[t] * 6, out_shape=[act] * 6,
        compiler_params=_cp(("parallel", "parallel")),
    )(ta, tb, tc, proj, proj, proj, dm)


def _concat_cols(parts, name):
    L = parts[0].shape[0]
    tm = min(L, 256)
    widths = [p.shape[1] for p in parts]

    def body(*refs):
        off = 0
        for ref, wd in zip(refs[:-1], widths):
            refs[-1][:, off:off + wd] = ref[...].astype(BF16)
            off += wd

    return pl.pallas_call(
        body, name=name, grid=(L // tm,),
        in_specs=[pl.BlockSpec((tm, wd), lambda i: (i, 0)) for wd in widths],
        out_specs=pl.BlockSpec((tm, sum(widths)), lambda i: (i, 0)),
        out_shape=jax.ShapeDtypeStruct((L, sum(widths)), BF16),
        compiler_params=_cp(("parallel",)),
    )(*parts)


GRAD_DT = BF16
SMALL = ("norm_w", "ssm_a_re", "ssm_a_im", "ssm_log_dt", "ssm_b_re", "ssm_b_im", "ssm_c_re", "ssm_c_im", "ssm_d",
         "ssm_glu_b", "sg_ln_w", "sg_ln_b", "sg_w", "sg_b", "attn_sinks")
G8 = SSM_GROUPS // N_SLAB


def _diag_mask(rows_per_group, cols_per_group):
    r = jnp.arange(G8 * rows_per_group)[:, None] // rows_per_group
    c = jnp.arange(G8 * cols_per_group)[None, :] // cols_per_group
    return r == c


def _slab_b(bb_t):
    x = bb_t.transpose(1, 0, 2).reshape(N_SLAB, SLAB_CH, SSM_STATE)
    return jnp.where(_diag_mask(SSM_GROUP, SSM_STATE), jnp.tile(x, (1, 1, G8)), 0)


def _unslab_b(d):
    x = jnp.where(_diag_mask(SSM_GROUP, SSM_STATE), d, 0).reshape(N_SLAB, SLAB_CH, G8, SSM_STATE).sum(axis=2)
    return x.reshape(SSM_GROUPS, SSM_GROUP, SSM_STATE).transpose(1, 0, 2)


def _slab_c(c):
    x = c.transpose(0, 2, 1).reshape(N_SLAB, SLAB_ST, SSM_GROUP)
    return jnp.where(_diag_mask(SSM_STATE, SSM_GROUP), jnp.tile(x, (1, 1, G8)), 0)


def _unslab_c(d):
    x = jnp.where(_diag_mask(SSM_STATE, SSM_GROUP), d, 0).reshape(N_SLAB, SLAB_ST, G8, SSM_GROUP).sum(axis=2)
    return x.reshape(SSM_GROUPS, SSM_STATE, SSM_GROUP).transpose(0, 2, 1)


def _s5_prep(p, tag):
    bt_re = p["ssm_b_re"].transpose(2, 0, 1)
    bt_im = p["ssm_b_im"].transpose(2, 0, 1)
    raw = (p["ssm_a_re"], p["ssm_a_im"], p["ssm_log_dt"][:, None], bt_re, bt_im)
    lr, li, bbr, bbi = _s5_params_fwd(*raw, name=f"s5_params_{tag}")
    ops = (_slab_b(bbr).astype(BF16), _slab_b(bbi).astype(BF16),
           _slab_c(p["ssm_c_re"]).astype(BF16), _slab_c(p["ssm_c_im"]).astype(BF16),
           jnp.broadcast_to(lr.reshape(N_SLAB, 1, SLAB_ST), (N_SLAB, SUB, SLAB_ST)),
           jnp.broadcast_to(li.reshape(N_SLAB, 1, SLAB_ST), (N_SLAB, SUB, SLAB_ST)),
           p["ssm_d"].reshape(N_SLAB, 1, SLAB_CH))
    return raw, ops


def _layer_fwd(x, p, w, tabs, tag, s5=None, proj_of=None, after_proj=None):
    L = x.shape[0]
    h = _rms_fwd(x, p["norm_w"][None], f"rms_fwd_{tag}")
    if proj_of is not None:
        proj = proj_of(h)
    else:
        proj = _mm(h, w["win_t"], "nt", BF16, L, PROJ_TN, D_MODEL, f"in_proj_{tag}")
    if after_proj is not None:
        w = after_proj(proj)
    s5_raw, s5_ops = s5 if s5 is not None else _s5_prep(p, tag)
    ya0 = _s5_fwd(proj, *s5_ops, name=f"s5_fwd_{tag}")
    ya = _glu_fwd(ya0, proj, w["glu"], p["ssm_glu_b"][None], f"glu_fwd_{tag}")
    yb = _sg_fwd(proj, p["sg_ln_w"][None], p["sg_ln_b"][None], p["sg_w"], p["sg_b"][:, :, None], f"sg_fwd_{tag}")
    yc, o_att = _attn_fwd(proj, p["attn_sinks"], tabs, f"attn_fwd_{tag}")
    ta = _mm(ya, w["wba_t"], "nt", BF16, 1024, 1024, 1024, f"branch_a_{tag}")
    tb = _mm(yb, w["wbb_t"], "nt", BF16, 1024, 1024, 1024, f"branch_b_{tag}")
    tc = _mm(yc, w["wbc_t"], "nt", BF16, 1024, 1024, 1024, f"branch_c_{tag}")
    merged = _merge_fwd(ta, tb, tc, proj, f"merge_fwd_{tag}")
    x_new = _mm(merged, w["wout"], "nn", F32, 1024, 512, D_MODEL, f"out_proj_{tag}", res=x)
    saved = dict(x=x, h=h, proj=proj, s5_raw=s5_raw, s5_ops=s5_ops, ya0=ya0, ya=ya, yb=yb, yc=yc, o_att=o_att,
                 ta=ta, tb=tb, tc=tc, merged=merged)
    return x_new, saved


def _layer_bwd(dx_out, p, w, tabs, s, tag, first_after=None, after_merge=None, before_win=None, after_win=None):
    L = dx_out.shape[0]
    proj = s["proj"]
    big, small = {}, {}
    dmerged = _mm(dx_out, w["wout"], "nt", BF16, 1024, 512, D_MODEL, f"d_merged_{tag}", after=first_after)
    big["wout"] = _mm(s["merged"], dx_out, "tn", GRAD_DT, 512, 1024, L, f"d_wout_{tag}")
    dta, dtb, dtc, dga, dgb, dgc = _merge_bwd(s["ta"], s["tb"], s["tc"], proj, dmerged, f"merge_bwd_{tag}")
    tok = after_merge(dga) if after_merge is not None else None
    dy = {}
    for br, dt in (("a", dta), ("b", dtb), ("c", dtc)):
        dy[br] = _mm(dt, w[f"wb{br}_t"], "nn", BF16, 1024, 1024, D_MODEL, f"d_y{br}_{tag}", after=tok)
        big[f"wb{br}_t"] = _mm(dt, s[f"y{br}"], "tn", GRAD_DT, 512, 1024, L, f"d_wb{br}_{tag}")

    dq, dzc, dkvc, dkvp, dsink = _attn_bwd(proj, p["attn_sinks"], tabs, s["o_att"], dy["c"], f"attn_bwd_{tag}")
    dkv = dkvc + jnp.concatenate([dkvp[CHUNK:], jnp.zeros((CHUNK, 256), F32)], axis=0)
    small["attn_sinks"] = dsink[:, 0]

    dub, dvb, dzb, dlw, dlb, dsgw, dsgb = _sg_bwd(
        proj, p["sg_ln_w"][None], p["sg_ln_b"][None], p["sg_w"], p["sg_b"][:, :, None], dy["b"], f"sg_bwd_{tag}")
    small.update(sg_ln_w=dlw[0], sg_ln_b=dlb[0], sg_w=dsgw, sg_b=dsgb[:, :, 0])

    dya0, dza, dglu, dglub = _glu_bwd(s["ya0"], proj, w["glu"], p["ssm_glu_b"][None], dy["a"], f"glu_bwd_{tag}")
    big["glu"] = dglu.astype(GRAD_DT)
    small["ssm_glu_b"] = dglub[0]

    dua, dbre, dbim, dcre, dcim, dlr, dli, dd = _s5_bwd(proj, dya0, *s["s5_ops"], name=f"s5_bwd_{tag}")
    da_re, da_im, dlog_dt, dbt_re, dbt_im = _s5_params_bwd(
        *s["s5_raw"], dlr.reshape(SSM_GROUPS, SSM_STATE), dli.reshape(SSM_GROUPS, SSM_STATE),
        _unslab_b(dbre), _unslab_b(dbim), name=f"s5_params_bwd_{tag}")
    small.update(ssm_a_re=da_re, ssm_a_im=da_im, ssm_log_dt=dlog_dt[:, 0],
                 ssm_bt_re=dbt_re, ssm_bt_im=dbt_im,
                 ssm_c_re=_unslab_c(dcre), ssm_c_im=_unslab_c(dcim), ssm_d=dd.reshape(SSM_WIDTH))

    dproj = _concat_cols([dua, dza, dub, dvb, dzb, dq, dkv, dzc, dga, dgb, dgc], f"d_proj_{tag}")
    tok = before_win(big) if before_win is not None else None
    big["win_t"] = _mm(dproj, s["h"], "tn", GRAD_DT, 256, D_MODEL, L, f"d_win_{tag}", after=tok)
    tok = after_win(big) if after_win is not None else None
    dh = _mm(dproj, w["win_t"], "nn", F32, L, D_MODEL, 256, f"d_h_{tag}", after=tok)
    dx_in, dnw = _rms_bwd(s["x"], p["norm_w"][None], dh, dx_out, f"rms_bwd_{tag}")
    small["norm_w"] = dnw[0]
    return dx_in, big, small


def _local_step(x, tgt, small_p, final_w, big_w):
    L = x.shape[0]
    tabs = _rope_tables(L)
    saved = []
    for l in range(DEPTH):
        x, s = _layer_fwd(x, small_p[l], big_w[l], tabs, f"l{l}")
        saved.append(s)
    loss_acc, dx, dfw = _final(x, final_w[None], tgt, "final_norm_loss")
    big_g, small_g = [None] * DEPTH, [None] * DEPTH
    for l in reversed(range(DEPTH)):
        dx, big_g[l], small_g[l] = _layer_bwd(dx, small_p[l], big_w[l], tabs, saved[l], f"l{l}")
    return loss_acc[0, 0], dx, dfw[0], big_g, small_g


MESH = pl.DeviceIdType.MESH
ANY = pl.BlockSpec(memory_space=pl.ANY)
ROW_ALIGN = 16


def _place():
    return lax.axis_index("x"), lax.axis_index("y"), lax.axis_index("c")


HBM = pl.BlockSpec(memory_space=pltpu.HBM)
SEM = pl.BlockSpec(memory_space=pltpu.SEMAPHORE)
EFFECT = pltpu.SideEffectType.DATAFLOW_SIDE_EFFECTING


def _split_start(srcs, lands, n_copies, copies, name, after=None):
    n, m, k = len(srcs), len(lands), n_copies
    extra = [] if after is None else [after]

    def body(*refs):
        src_refs, land_refs = refs[:n], refs[n:n + m]
        sems = refs[n + m + len(extra):]
        send_sems, recv_sems, token = sems[:k], sems[k:2 * k], refs[-1]
        for cp in copies(src_refs, land_refs, send_sems, recv_sems):
            cp.start()
        token[...] = jnp.zeros_like(token)

    ops = list(srcs) + list(lands)
    outs = pl.pallas_call(
        body, name=name,
        out_shape=(*[pltpu.SemaphoreType.DMA(())] * (2 * k),
                   *[pltpu.HBM(a.shape, a.dtype) for a in ops], jax.ShapeDtypeStruct((8, 128), F32)),
        in_specs=[HBM] * (n + m) + [ANY] * len(extra),
        out_specs=(*[SEM] * (2 * k), *[HBM] * (n + m), pl.BlockSpec(memory_space=pltpu.VMEM)),
        input_output_aliases={i: 2 * k + i for i in range(n + m)},
        compiler_params=pltpu.CompilerParams(has_side_effects=EFFECT),
    )(*[pltpu.with_memory_space_constraint(a, pltpu.HBM) for a in ops], *extra)
    return (list(outs[:k]), list(outs[k:2 * k]), list(outs[2 * k:2 * k + n]), list(outs[2 * k + n:2 * k + n + m]),
            outs[-1])


def _split_wait(send_sems, recv_sems, srcs, lands, after, copies, name):
    n, m, k = len(srcs), len(lands), len(send_sems)
    after = list(after) if isinstance(after, (list, tuple)) else [after]

    def body(*refs):
        src_refs, land_refs = refs[:n], refs[n:n + m]
        for cp in copies(src_refs, land_refs, refs[n + m:n + m + k], refs[n + m + k:n + m + 2 * k]):
            cp.wait_send()
            cp.wait_recv()

    ops = list(srcs) + list(lands)
    outs = pl.pallas_call(
        body, name=name,
        out_shape=tuple(pltpu.HBM(a.shape, a.dtype) for a in ops),
        in_specs=[HBM] * (n + m) + [SEM] * (2 * k) + [ANY] * len(after),
        out_specs=tuple([HBM] * (n + m)),
        input_output_aliases={i: i for i in range(n + m)},
        compiler_params=pltpu.CompilerParams(has_side_effects=EFFECT),
    )(*ops, *send_sems, *recv_sems, *after)
    return list(outs[:n]), list(outs[n:])


def _ag_rows(land_ref, px, py, pc):
    r = land_ref.shape[0] // N_DEV
    start = pl.multiple_of((4 * px + 2 * py + pc) * r, ROW_ALIGN)
    return land_ref.at[pl.ds(start, r), :]


def _ag_copies_to(which):
    def copies(src_refs, land_refs, send_sems, recv_sems):
        x, y, c = _place()
        peers = [(x, y, 1 - c), (1 - x, y, c), (x, 1 - y, c), (1 - x, 1 - y, c)]
        return [pltpu.make_async_remote_copy(
            src_ref=_ag_rows(land_refs[a], x, y, c), dst_ref=_ag_rows(land_refs[a], x, y, c),
            send_sem=send_sems[len(which) * a + k], recv_sem=recv_sems[len(which) * a + k],
            device_id=peers[p], device_id_type=MESH)
            for a in range(len(land_refs)) for k, p in enumerate(which)]
    return copies


_ag_copies = _ag_copies_to((0, 1, 2, 3))
_ag_copies_near = _ag_copies_to((0, 1, 2))
_ag_copies_far = _ag_copies_to((3,))


def _ag_forward(lands, name, which=(0, 1, 2)):
    n = len(lands)

    def body(*refs):
        land_refs = refs[n:2 * n]
        send_sems, recv_sems = refs[2 * n:]
        x, y, c = _place()
        chips = [(1 - x, y), (x, 1 - y), (1 - x, 1 - y)]

        def copy(a, k, pc):
            px, py = chips[which[k]]
            return pltpu.make_async_remote_copy(
                src_ref=_ag_rows(land_refs[a], px, py, pc), dst_ref=_ag_rows(land_refs[a], px, py, pc),
                send_sem=send_sems.at[a, k], recv_sem=recv_sems.at[a, k], device_id=(x, y, 1 - c), device_id_type=MESH)

        passed = [copy(a, k, c) for a in range(n) for k in range(len(which))]
        for cp in passed:
            cp.start()
        for a in range(n):
            for k in range(len(which)):
                copy(a, k, 1 - c).wait_recv()
        for cp in passed:
            cp.wait_send()

    sems = pltpu.SemaphoreType.DMA((n, len(which)))
    return pl.pallas_call(
        body, name=name,
        in_specs=[ANY] * n, out_specs=[ANY] * n,
        out_shape=[jax.ShapeDtypeStruct(l.shape, l.dtype) for l in lands],
        input_output_aliases={i: i for i in range(n)},
        scratch_shapes=[sems, sems],
    )(*lands)


def _allgather_place(shards):
    x, y, c = _place()
    return [lax.dynamic_update_slice(lax.empty((N_DEV * s.shape[0], s.shape[1]), s.dtype), s,
                                     ((4 * x + 2 * y + c) * s.shape[0], 0)) for s in shards]


def _allgather_place_rows(w, layer, name):
    _, r, cols = w.shape
    tr = _row_tile(r)
    nb = r // tr
    x, y, c = _place()
    me = (4 * x + 2 * y + c).astype(jnp.int32)[None]

    def body(me_ref, s_ref, o_ref):
        o_ref[...] = s_ref[...].astype(BF16)

    return pl.pallas_call(
        body, name=name,
        grid_spec=pltpu.PrefetchScalarGridSpec(
            num_scalar_prefetch=1, grid=(nb,),
            in_specs=[pl.BlockSpec((None, tr, cols), lambda i, me_ref: (layer, i, 0))],
            out_specs=pl.BlockSpec((tr, cols), lambda i, me_ref: (me_ref[0] * nb + i, 0))),
        out_shape=jax.ShapeDtypeStruct((N_DEV * r, cols), BF16),
        compiler_params=_cp(("parallel",)),
    )(me, w)


def _allgather_start(lands, name, after=None):
    return _split_start([], lands, 4 * len(lands), _ag_copies, name + "_start", after=after)


def _allgather_finish(started, after, name):
    send_sems, recv_sems, _, lands, _ = started
    _, lands = _split_wait(send_sems, recv_sems, [], lands, after, _ag_copies, name + "_wait")
    return list(_ag_forward(lands, name + "_forward"))


def _rs_swap_cores(grads, name):
    n = len(grads)

    def body(*refs):
        ins, outs = refs[:n], refs[n:2 * n]
        send_sems, recv_sems = refs[2 * n:]
        x, y, c = _place()
        cps = []
        for a in range(n):
            r = ins[a].shape[0] // N_DEV
            for q in range(4):
                start = pl.multiple_of((2 * q + 1 - c) * r, ROW_ALIGN)
                cps.append(pltpu.make_async_remote_copy(
                    src_ref=ins[a].at[pl.ds(start, r), :], dst_ref=outs[a].at[q],
                    send_sem=send_sems.at[a, q], recv_sem=recv_sems.at[a, q],
                    device_id=(x, y, 1 - c), device_id_type=MESH))
        for cp in cps:
            cp.start()
        for cp in cps:
            cp.wait()

    return pl.pallas_call(
        body, name=name, in_specs=[ANY] * n, out_specs=[ANY] * n,
        out_shape=[jax.ShapeDtypeStruct((4, g.shape[0] // N_DEV, g.shape[1]), g.dtype) for g in grads],
        scratch_shapes=[pltpu.SemaphoreType.DMA((n, 4)), pltpu.SemaphoreType.DMA((n, 4))],
    )(*grads)


def _rs_chip_copies(sum_refs, land_refs, send_sems, recv_sems):
    x, y, c = _place()
    chips = [(1 - x, y), (x, 1 - y), (1 - x, 1 - y)]
    return [pltpu.make_async_remote_copy(
        src_ref=sum_refs[a].at[2 * px + py], dst_ref=land_refs[a].at[2 * x + y],
        send_sem=send_sems[3 * a + j], recv_sem=recv_sems[3 * a + j], device_id=(px, py, c), device_id_type=MESH)
        for a in range(len(sum_refs)) for j, (px, py) in enumerate(chips)]


def _row_tile(r):
    return max(t for t in range(ROW_ALIGN, min(r, 1024) + 1, ROW_ALIGN) if r % t == 0)


def _rs_add_cores(grad, recv, cidx, name):
    r, cols = recv.shape[1], recv.shape[2]
    tr = _row_tile(r)
    nb = r // tr

    def body(c_ref, g_ref, r_ref, o_ref):
        o_ref[...] = (g_ref[...].astype(F32) + r_ref[...].astype(F32)).astype(o_ref.dtype)

    return pl.pallas_call(
        body, name=name,
        grid_spec=pltpu.PrefetchScalarGridSpec(
            num_scalar_prefetch=1, grid=(4, nb),
            in_specs=[pl.BlockSpec((tr, cols), lambda q, i, c_ref: ((2 * q + c_ref[0]) * nb + i, 0)),
                      pl.BlockSpec((None, tr, cols), lambda q, i, c_ref: (q, i, 0))],
            out_specs=pl.BlockSpec((None, tr, cols), lambda q, i, c_ref: (q, i, 0))),
        out_shape=jax.ShapeDtypeStruct(recv.shape, recv.dtype),
        compiler_params=_cp(("parallel", "parallel")),
    )(cidx, grad, recv)


def _rs_add_chips(own, recv, slots, name):
    r, cols = recv.shape[1], recv.shape[2]
    tr = _row_tile(r)

    def body(s_ref, o_ref, r0_ref, r1_ref, r2_ref, out_ref):
        acc = o_ref[...].astype(F32)
        for ref in (r0_ref, r1_ref, r2_ref):
            acc = acc + ref[...].astype(F32)
        out_ref[...] = acc

    pick = lambda k: pl.BlockSpec((None, tr, cols), functools.partial(lambda i, s_ref, k: (s_ref[k], i, 0), k=k))
    return pl.pallas_call(
        body, name=name,
        grid_spec=pltpu.PrefetchScalarGridSpec(
            num_scalar_prefetch=1, grid=(r // tr,),
            in_specs=[pick(0), pick(1), pick(2), pick(3)],
            out_specs=pl.BlockSpec((tr, cols), lambda i, s_ref: (i, 0))),
        out_shape=jax.ShapeDtypeStruct((r, cols), F32),
        compiler_params=_cp(("parallel",)),
    )(slots, own, recv, recv, recv)


def _rs_core_copies(grad_refs, land_refs, send_sems, recv_sems):
    x, y, c = _place()
    cps = []
    for a in range(len(grad_refs)):
        r = grad_refs[a].shape[0] // N_DEV
        for q in range(4):
            start = pl.multiple_of((2 * q + 1 - c) * r, ROW_ALIGN)
            cps.append(pltpu.make_async_remote_copy(
                src_ref=grad_refs[a].at[pl.ds(start, r), :], dst_ref=land_refs[a].at[q],
                send_sem=send_sems[4 * a + q], recv_sem=recv_sems[4 * a + q],
                device_id=(x, y, 1 - c), device_id_type=MESH))
    return cps


def _reduce_scatter_chips_start(grads, recv, tag):
    cidx = lax.axis_index("c").astype(jnp.int32)[None]
    sums = [_rs_add_cores(g, rv, cidx, f"rs_add_cores_{tag}_{i}") for i, (g, rv) in enumerate(zip(grads, recv))]
    lands = [lax.empty(s.shape, s.dtype) for s in sums]
    return _split_start(sums, lands, 3 * len(sums), _rs_chip_copies, f"rs_chips_{tag}_start")


def _reduce_scatter_start(grads, tag):
    return _reduce_scatter_chips_start(grads, _rs_swap_cores(grads, f"rs_swap_cores_{tag}"), tag)


def _reduce_scatter_cores_start(grads, tag):
    lands = [lax.empty((4, g.shape[0] // N_DEV, g.shape[1]), g.dtype) for g in grads]
    return _split_start(grads, lands, 4 * len(grads), _rs_core_copies, f"rs_cores_{tag}_start")


def _reduce_scatter_cores_finish(started, after, tag):
    send_sems, recv_sems, grads, lands, _ = started
    grads, recv = _split_wait(send_sems, recv_sems, grads, lands, after, _rs_core_copies, f"rs_cores_{tag}_wait")
    return _reduce_scatter_chips_start(grads, recv, tag)


def _reduce_scatter_finish(started, after, tag):
    send_sems, recv_sems, sums, lands, _ = started
    sums, lands = _split_wait(send_sems, recv_sems, sums, lands, after, _rs_chip_copies, f"rs_chips_{tag}_wait")
    x, y = lax.axis_index("x"), lax.axis_index("y")
    slots = jnp.stack([2 * x + y, 2 * (1 - x) + y, 2 * x + 1 - y, 2 * (1 - x) + 1 - y]).astype(jnp.int32)
    return [(s, l, slots) for s, l in zip(sums, lands)]


def _ar_peers():
    x, y, c = _place()
    return [(1 - x if k & 4 else x, 1 - y if k & 2 else y, 1 - c if k & 1 else c) for k in range(1, N_DEV)]


def _ar_scatter_copies(src_refs, land_refs, send_sems, recv_sems):
    x, y, c = _place()
    cps = []
    for a in range(len(src_refs)):
        rs = src_refs[a].shape[0] // N_DEV
        for k, (px, py, pc) in enumerate(_ar_peers()):
            start = pl.multiple_of((4 * px + 2 * py + pc) * rs, 8)
            cps.append(pltpu.make_async_remote_copy(
                src_ref=src_refs[a].at[pl.ds(start, rs), :], dst_ref=land_refs[a].at[4 * x + 2 * y + c],
                send_sem=send_sems[7 * a + k], recv_sem=recv_sems[7 * a + k],
                device_id=(px, py, pc), device_id_type=MESH))
    return cps


def _ar_gather_copies(src_refs, land_refs, send_sems, recv_sems):
    x, y, c = _place()
    cps = []
    for a in range(len(land_refs)):
        rs = land_refs[a].shape[0] // N_DEV
        mine = land_refs[a].at[pl.ds(pl.multiple_of((4 * x + 2 * y + c) * rs, 8), rs), :]
        for k, peer in enumerate(_ar_peers()):
            cps.append(pltpu.make_async_remote_copy(
                src_ref=mine, dst_ref=mine, send_sem=send_sems[7 * a + k], recv_sem=recv_sems[7 * a + k],
                device_id=peer, device_id_type=MESH))
    return cps


def _allreduce_start(packs, name, after=None):
    assert all(p.shape[0] % (8 * N_DEV) == 0 for p in packs)
    x, y, c = _place()
    me = 4 * x + 2 * y + c
    lands = []
    for p in packs:
        rs = p.shape[0] // N_DEV
        own = lax.dynamic_slice(p, (me * rs, 0), (rs, p.shape[1]))
        lands.append(lax.dynamic_update_slice(lax.empty((N_DEV, rs, p.shape[1]), F32), own[None], (me, 0, 0)))
    return _split_start(packs, lands, 7 * len(packs), _ar_scatter_copies, name + "_scatter_start", after=after)


def _allreduce_middle(started, after, name):
    n = len(started[2])
    _, parts = _split_wait(started[0], started[1], started[2], started[3], after, _ar_scatter_copies, name + "_scatter_wait")

    def body(*refs):
        for p_ref, o_ref in zip(refs[:n], refs[n:]):
            acc = p_ref[0]
            for d in range(1, N_DEV):
                acc = acc + p_ref[d]
            o_ref[...] = acc

    sums = pl.pallas_call(body, name=name + "_add",
                          out_shape=[jax.ShapeDtypeStruct(p.shape[1:], F32) for p in parts])(*parts)
    x, y, c = _place()
    me = 4 * x + 2 * y + c
    lands = [lax.dynamic_update_slice(lax.empty((N_DEV * s.shape[0], s.shape[1]), F32), s, (me * s.shape[0], 0))
             for s in sums]
    return _split_start([], lands, 7 * n, _ar_gather_copies, name + "_gather_start")


def _allreduce_finish(started, after, name):
    _, lands = _split_wait(started[0], started[1], [], started[3], after, _ar_gather_copies, name + "_gather_wait")
    return lands


ADAM_TILE_BYTES = 2 * 1024 * 1024


def _adam_tiles(rows, cols, align=8):
    if rows % align:
        return rows, cols
    best = None
    for tc in {cols, cols // 2, cols // 4}:
        if tc != cols and (tc % 128 or cols % tc):
            continue
        fits = [t for t in range(align, rows + 1, align) if rows % t == 0 and t * max(tc, 128) * 4 <= ADAM_TILE_BYTES]
        if fits and (best is None or max(fits) * tc > best[0] * best[1]):
            best = (max(fits), tc)
    return best


def _adam_math(w, g, m, v):
    nm = ADAM_B1 * m + (1.0 - ADAM_B1) * g
    nv = ADAM_B2 * v + (1.0 - ADAM_B2) * jnp.square(g)
    c1 = 1.0 - ADAM_B1 ** ADAM_STEP
    c2 = 1.0 - ADAM_B2 ** ADAM_STEP
    return -ADAM_LR * ((nm / c1) / (jnp.sqrt(nv / c2) + ADAM_EPS) + ADAM_WD * w), nm, nv


def _adamw_layer(w, g, m, v, layer, carry, name):
    _, rows, cols = w.shape
    tr, tc = _adam_tiles(rows, cols)

    def body(w_ref, g_ref, m_ref, v_ref, *rest):
        go_ref, d_ref, nm_ref, nv_ref = rest[-4:]
        gv = g_ref[...]
        go_ref[...] = gv
        d_ref[...], nm_ref[...], nv_ref[...] = _adam_math(w_ref[...], gv, m_ref[...], v_ref[...])

    blk = pl.BlockSpec((None, tr, tc), lambda i, j: (layer, i, j))
    flat = pl.BlockSpec((tr, tc), lambda i, j: (i, j))
    sh = jax.ShapeDtypeStruct(w.shape, F32)
    carry = [] if carry is None else list(carry)
    return pl.pallas_call(
        body, name=name, grid=(rows // tr, cols // tc),
        in_specs=[blk, flat, blk, blk] + [ANY] * len(carry), out_specs=[blk] * 4, out_shape=[sh] * 4,
        input_output_aliases={4 + k: k for k in range(len(carry))},
        compiler_params=_cp(("parallel", "parallel")),
    )(w, g, m, v, *carry)


def _adamw_layer_sum(w, own, recv, slots, m, v, layer, carry, name):
    _, rows, cols = w.shape
    tr, tc = _adam_tiles(rows, cols, align=ROW_ALIGN)

    def body(s_ref, w_ref, o_ref, r0_ref, r1_ref, r2_ref, m_ref, v_ref, *rest):
        go_ref, d_ref, nm_ref, nv_ref = rest[-4:]
        gv = o_ref[...].astype(F32)
        for ref in (r0_ref, r1_ref, r2_ref):
            gv = gv + ref[...].astype(F32)
        go_ref[...] = gv
        d_ref[...], nm_ref[...], nv_ref[...] = _adam_math(w_ref[...], gv, m_ref[...], v_ref[...])

    blk = pl.BlockSpec((None, tr, tc), lambda i, j, s: (layer, i, j))
    pick = lambda k: pl.BlockSpec((None, tr, tc), functools.partial(lambda i, j, s, k: (s[k], i, j), k=k))
    sh = jax.ShapeDtypeStruct(w.shape, F32)
    carry = [] if carry is None else list(carry)
    return pl.pallas_call(
        body, name=name,
        grid_spec=pltpu.PrefetchScalarGridSpec(
            num_scalar_prefetch=1, grid=(rows // tr, cols // tc),
            in_specs=[blk, pick(0), pick(1), pick(2), pick(3), blk, blk] + [ANY] * len(carry),
            out_specs=[blk] * 4),
        out_shape=[sh] * 4,
        input_output_aliases={8 + k: k for k in range(len(carry))},
        compiler_params=_cp(("parallel", "parallel")),
    )(slots, w, own, recv, recv, recv, m, v, *carry)


def _adamw(w, g, m, v, name):
    shape = w.shape
    rows, cols = shape[-2:]
    lead = shape[:-2]
    nl = math.prod(lead)
    tr, tc = _adam_tiles(rows, cols)

    def body(w_ref, g_ref, m_ref, v_ref, d_ref, nm_ref, nv_ref):
        d_ref[...], nm_ref[...], nv_ref[...] = _adam_math(w_ref[...], g_ref[...], m_ref[...], v_ref[...])

    def index(b, i, j):
        return (*jnp.unravel_index(b, lead), i, j) if lead else (i, j)

    blk = pl.BlockSpec((*[None] * len(lead), tr, tc), index)
    sh = jax.ShapeDtypeStruct(shape, F32)
    return pl.pallas_call(
        body, name=name, grid=(nl, rows // tr, cols // tc), in_specs=[blk] * 4, out_specs=[blk] * 3,
        out_shape=[sh] * 3, compiler_params=_cp(("parallel", "parallel", "parallel")),
    )(w, g, m, v)


WEIGHTS = ("norm_w", "w_in", "ssm_a_re", "ssm_a_im", "ssm_log_dt", "ssm_b_re", "ssm_b_im", "ssm_c_re", "ssm_c_im",
           "ssm_d", "ssm_glu_w", "ssm_glu_b", "sg_ln_w", "sg_ln_b", "sg_w", "sg_b", "attn_sinks",
           "w_branch_a", "w_branch_b", "w_branch_c", "w_out", "final_norm_w")
BIG = ("w_in", "ssm_glu_w", "w_branch_a", "w_branch_b", "w_branch_c", "w_out")
BIG_KEY = {"w_in": ("win_t", True), "ssm_glu_w": ("glu", False), "w_branch_a": ("wba_t", True),
           "w_branch_b": ("wbb_t", True), "w_branch_c": ("wbc_t", True), "w_out": ("wout", False)}
VIEWS = {"w_in": (1, 2), "ssm_b_re": (2, 3), "ssm_b_im": (2, 3)}
MATS = ("ssm_a_re", "ssm_a_im", "ssm_c_re", "ssm_c_im", "ssm_b_re", "ssm_b_im", "sg_w")
VEC_GROUPS = (("ssm_d", "ssm_glu_b", "sg_ln_w", "sg_ln_b"), ("norm_w", "final_norm_w", "sg_b"), ("ssm_log_dt", "attn_sinks"))
PACK_ROWS = 8 * N_DEV


def _view(n, a):
    return jnp.swapaxes(a, *VIEWS[n]) if n in VIEWS else a


def _vec_moves(pack_ref, refs, to_pack):
    d, gb, lw, lb, nw, fw, sb, ld, sk = refs
    full = (slice(None), slice(None))
    moves = [((slice(2 * i, 2 * i + 2), slice(None)), r, full) for i, r in enumerate((d, gb, lw, lb))]
    moves += [((slice(8, 10), slice(None)), nw, (slice(None), slice(0, 1024))),
              ((slice(10, 12), slice(None)), nw, (slice(None), slice(1024, 2048))),
              ((slice(12, 13), slice(None)), fw, (slice(None), slice(0, 1024))),
              ((slice(13, 14), slice(None)), fw, (slice(None), slice(1024, 2048))),
              ((slice(16, 32), slice(0, 128)), sb, full),
              ((slice(32, 34), slice(0, 64)), ld, full),
              ((slice(34, 36), slice(0, 16)), sk, full)]
    for where, ref, part in moves:
        if to_pack:
            pack_ref[where] = ref[part]
        else:
            ref[part] = pack_ref[where]


def _vec_shapes(arrs):
    d, gb, lw, lb, nw, fw, sb, ld, sk = arrs
    return [d, gb, lw, lb, nw, fw.reshape(1, -1), sb.reshape(-1, sb.shape[-1]), ld, sk]


def _vec_pack(arrs, name):
    def body(*refs):
        refs[-1][...] = jnp.zeros_like(refs[-1])
        _vec_moves(refs[-1], refs[:-1], True)

    return pl.pallas_call(body, name=name, out_shape=jax.ShapeDtypeStruct((PACK_ROWS, 1024), F32))(*_vec_shapes(arrs))


def _vec_unpack(pack, like, name):
    shaped = _vec_shapes(like)

    def body(pack_ref, *refs):
        _vec_moves(pack_ref, refs, False)

    outs = pl.pallas_call(body, name=name, out_shape=[jax.ShapeDtypeStruct(a.shape, F32) for a in shaped])(pack)
    return [o.reshape(a.shape) for o, a in zip(outs, like)]


def _pack(groups, cols, name):
    assert cols == 1024
    return _vec_pack([a for arrs in groups for a in arrs], name)


def _unpack(pack, groups, name):
    return _vec_unpack(pack, [a for arrs in groups for a in arrs], name)


def kernel(x, norm_w, w_in, ssm_a_re, ssm_a_im, ssm_log_dt, ssm_b_re, ssm_b_im, ssm_c_re, ssm_c_im, ssm_d, ssm_glu_w, ssm_glu_b, sg_ln_w, sg_ln_b, sg_w, sg_b, attn_sinks, w_branch_a, w_branch_b, w_branch_c, w_out, final_norm_w, loss_target, m_norm_w, m_w_in, m_ssm_a_re, m_ssm_a_im, m_ssm_log_dt, m_ssm_b_re, m_ssm_b_im, m_ssm_c_re, m_ssm_c_im, m_ssm_d, m_ssm_glu_w, m_ssm_glu_b, m_sg_ln_w, m_sg_ln_b, m_sg_w, m_sg_b, m_attn_sinks, m_w_branch_a, m_w_branch_b, m_w_branch_c, m_w_out, m_final_norm_w, v_norm_w, v_w_in, v_ssm_a_re, v_ssm_a_im, v_ssm_log_dt, v_ssm_b_re, v_ssm_b_im, v_ssm_c_re, v_ssm_c_im, v_ssm_d, v_ssm_glu_w, v_ssm_glu_b, v_sg_ln_w, v_sg_ln_b, v_sg_w, v_sg_b, v_attn_sinks, v_w_branch_a, v_w_branch_b, v_w_branch_c, v_w_out, v_final_norm_w):
    w = dict(zip(WEIGHTS, (norm_w, w_in, ssm_a_re, ssm_a_im, ssm_log_dt, ssm_b_re, ssm_b_im, ssm_c_re, ssm_c_im, ssm_d, ssm_glu_w, ssm_glu_b, sg_ln_w, sg_ln_b, sg_w, sg_b, attn_sinks, w_branch_a, w_branch_b, w_branch_c, w_out, final_norm_w)))
    m = dict(zip(WEIGHTS, (m_norm_w, m_w_in, m_ssm_a_re, m_ssm_a_im, m_ssm_log_dt, m_ssm_b_re, m_ssm_b_im, m_ssm_c_re, m_ssm_c_im, m_ssm_d, m_ssm_glu_w, m_ssm_glu_b, m_sg_ln_w, m_sg_ln_b, m_sg_w, m_sg_b, m_attn_sinks, m_w_branch_a, m_w_branch_b, m_w_branch_c, m_w_out, m_final_norm_w)))
    v = dict(zip(WEIGHTS, (v_norm_w, v_w_in, v_ssm_a_re, v_ssm_a_im, v_ssm_log_dt, v_ssm_b_re, v_ssm_b_im, v_ssm_c_re, v_ssm_c_im, v_ssm_d, v_ssm_glu_w, v_ssm_glu_b, v_sg_ln_w, v_sg_ln_b, v_sg_w, v_sg_b, v_attn_sinks, v_w_branch_a, v_w_branch_b, v_w_branch_c, v_w_out, v_final_norm_w)))

    keys = [BIG_KEY[n][0] for n in BIG]
    wv, mv, vv = ({n: _view(n, a) for n, a in d.items()} for d in (w, m, v))
    shards = [[(wv[n][l] if n in VIEWS else w[n][l].T if BIG_KEY[n][1] else w[n][l]).astype(BF16) for n in BIG]
              for l in range(DEPTH)]
    small_p = [{n: w[n][l] for n in SMALL} for l in range(DEPTH)]
    xv, tgt = x[0], loss_target[0]
    tabs = _rope_tables(xv.shape[0])

    lands = [[[_allgather_place_rows(wv["w_in"], l, f"place_win_l{l}")], _allgather_place(shards[l][1:])]
             for l in range(DEPTH)]
    s5 = [_s5_prep(small_p[l], f"l{l}") for l in range(DEPTH)]
    vec_packs = [_pack([[d[n] for n in names] for names in VEC_GROUPS], 1024, f"pack_vec_{tag}")
                 for tag, d in (("w", wv), ("m", mv), ("v", vv))]
    near = _split_start([], lands[0][0], 3, _ag_copies_near, "ag_l0_win_near_start")
    got = {}
    x_, y_ = lax.axis_index("x"), lax.axis_index("y")
    n_tiles = D_IN // PROJ_TN
    far_first = (D_IN // 4 // PROJ_TN) * (2 * (1 - x_) + (1 - y_))
    n_far = -(-D_IN // 4 // PROJ_TN)
    tile_ids = jnp.arange(n_tiles, dtype=jnp.int32)
    is_far = (tile_ids >= far_first) & (tile_ids < far_first + n_far)
    near_tiles = jnp.sort(jnp.where(is_far, n_tiles, tile_ids))[:n_tiles - n_far]
    far_tiles = (far_first + jnp.arange(n_far)).astype(jnp.int32)

    def proj_of0(h):
        early = [h, *lands[0][1], *lands[1][0], *lands[1][1], *s5[0][1], *s5[1][1], near_tiles, far_tiles]
        early += vec_packs
        _, land = _split_wait(near[0], near[1], [], near[3], early, _ag_copies_near, "ag_l0_win_near_wait")
        far = _split_start([], land, 1, _ag_copies_far, "ag_l0_win_far_start")
        land = _ag_forward(far[3], "ag_l0_win_near_forward", which=(0, 1))
        got["ag0b"] = _allgather_start(lands[0][1], "ag_l0_rest", after=land[0])
        got["near1"] = _split_start([], lands[1][0], 3, _ag_copies_near, "ag_l1_win_near_start", after=got["ag0b"][4])
        proj = _in_proj_tiles(h, land[0], near_tiles, None, "in_proj_l0_near", after=got["near1"][4])
        _, land = _split_wait(far[0], far[1], [], land, proj, _ag_copies_far, "ag_l0_win_far_wait")
        got["win0"] = _ag_forward(land, "ag_l0_win_far_forward", which=(2,))[0]
        return _in_proj_tiles(h, got["win0"], far_tiles, proj, "in_proj_l0_far")

    def after_proj0(proj):
        got["w0"] = dict(zip(keys, [got["win0"]] + _allgather_finish(got["ag0b"], proj, "ag_l0_rest")))
        return got["w0"]

    x1, saved0 = _layer_fwd(xv, small_p[0], None, tabs, "l0", s5=s5[0], proj_of=proj_of0, after_proj=after_proj0)
    big_w0 = got["w0"]

    def proj_of1(h):
        near1 = got["near1"]
        _, land = _split_wait(near1[0], near1[1], [], near1[3], h, _ag_copies_near, "ag_l1_win_near_wait")
        far1 = _split_start([], land, 1, _ag_copies_far, "ag_l1_win_far_start")
        land = _ag_forward(far1[3], "ag_l1_win_near_forward", which=(0, 1))
        got["ag1b"] = _allgather_start(lands[1][1], "ag_l1_rest", after=land[0])
        proj = _in_proj_tiles(h, land[0], near_tiles, None, "in_proj_l1_near", after=got["ag1b"][4])
        _, land = _split_wait(far1[0], far1[1], [], land, proj, _ag_copies_far, "ag_l1_win_far_wait")
        got["win1"] = _ag_forward(land, "ag_l1_win_far_forward", which=(2,))[0]
        return _in_proj_tiles(h, got["win1"], far_tiles, proj, "in_proj_l1_far")

    def after_proj1(proj):
        got["w1"] = dict(zip(keys, [got["win1"]] + _allgather_finish(got["ag1b"], proj, "ag_l1_rest")))
        return got["w1"]

    x2, saved1 = _layer_fwd(x1, small_p[1], None, tabs, "l1", s5=s5[1], proj_of=proj_of1, after_proj=after_proj1)
    big_w1 = got["w1"]
    loss_acc, dx2, dfw = _final(x2, w["final_norm_w"][None], tgt, "final_norm_loss")
    loss = lax.psum(loss_acc[0, 0], ("x", "y", "c"))
    dfw = dfw[0]

    dx1, big_g1, small_g1 = _layer_bwd(dx2, small_p[1], big_w1, tabs, saved1, "l1")
    rs1_cores = _reduce_scatter_cores_start([big_g1[k] for k in keys], "l1")

    def after_merge0(x):
        got["rs1"] = _reduce_scatter_cores_finish(rs1_cores, x, "l1")
        return got["rs1"][4]

    def before_win0(big):
        got["rs0b"] = _reduce_scatter_start([big[k] for k in keys[1:]], "l0_rest")
        return got["rs0b"][4]

    def after_win0(big):
        got["rs0a"] = _reduce_scatter_start([big["win_t"]], "l0_win")
        return got["rs0a"][4]

    dx, big_g0, small_g0 = _layer_bwd(dx1, small_p[0], big_w0, tabs, saved0, "l0", first_after=rs1_cores[4],
                                      after_merge=after_merge0, before_win=before_win0, after_win=after_win0)
    rs1 = got["rs1"]
    small_g = [small_g0, small_g1]
    grads, delta, new_m, new_v = {}, {}, {}, {}

    def big_adam(red, layer, carry):
        outs = {}
        for i, n in enumerate(BIG):
            own, recv, slots = red[i]
            prev = None if carry is None else carry[n]
            if BIG_KEY[n][1] and n not in VIEWS:
                g = _rs_add_chips(own, recv, slots, f"rs_add_chips_{n}_l{layer}").T
                outs[n] = _adamw_layer(wv[n], g, mv[n], vv[n], layer, prev, f"adamw_{n}_l{layer}")
            else:
                outs[n] = _adamw_layer_sum(wv[n], own, recv, slots, mv[n], vv[n], layer, prev, f"adamw_{n}_l{layer}")
        return outs

    def small_grad(n):
        if n == "final_norm_w":
            return dfw
        if n in ("ssm_b_re", "ssm_b_im"):
            return jnp.stack([small_g[l][n.replace("ssm_b_", "ssm_bt_")].transpose(1, 0, 2) for l in range(DEPTH)])
        return jnp.stack([small_g[l][n] for l in range(DEPTH)])

    rows_of = lambda a: a.reshape(-1, a.shape[-1])
    g_mats = [rows_of(small_grad(n)) for n in MATS]
    g_vecs = [[small_grad(n) for n in names] for names in VEC_GROUPS]
    ar = _allreduce_start(g_mats + [_pack(g_vecs, 1024, "pack_vec_g")], "allreduce_small")
    big1 = big_adam(_reduce_scatter_finish(rs1, [dx, ar[4]], "l1"), 1, None)
    ar = _allreduce_middle(ar, [big1[n][1] for n in BIG], "allreduce_small")
    red0 = (_reduce_scatter_finish(got["rs0a"], ar[4], "l0_win")
            + _reduce_scatter_finish(got["rs0b"], ar[4], "l0_rest"))
    big0 = big_adam(red0, 0, big1)
    for n, outs in big0.items():
        grads[n], delta[n], new_m[n], new_v[n] = outs
    reduced = _allreduce_finish(ar, [big0[n][1] for n in BIG], "allreduce_small")
    for n, red in zip(MATS, reduced):
        outs = _adamw(rows_of(wv[n]), red, rows_of(mv[n]), rows_of(vv[n]), f"adamw_{n}")
        grads[n], delta[n], new_m[n], new_v[n] = (o.reshape(wv[n].shape) for o in (red, *outs))
    vec_names = [n for names in VEC_GROUPS for n in names]
    grads.update(zip(vec_names, _unpack(reduced[-1], g_vecs, "unpack_vec_g")))
    outs = _adamw(vec_packs[0], reduced[-1], vec_packs[1], vec_packs[2], "adamw_vec")
    for tag, res, o in zip("dmv", (delta, new_m, new_v), outs):
        res.update(zip(vec_names, _unpack(o, [[wv[n] for n in names] for names in VEC_GROUPS], f"unpack_vec_{tag}")))

    return (loss, dx[None], *[_view(n, d[n]) for d in (grads, delta, new_m, new_v) for n in WEIGHTS])
```

```python
import functools
import math

import jax
import jax.numpy as jnp
from jax import lax
from jax.experimental import pallas as pl
from jax.experimental.pallas import tpu as pltpu

F32 = jnp.float32
BF16 = jnp.bfloat16

D_MODEL = 2048
DEPTH = 2
EPS = 1e-6
NEG_INF = -1e30
N_DEV = 8

SSM_WIDTH = 1024
SSM_GROUP = 16
SSM_GROUPS = 64
SSM_STATE = 64
N_SLAB = 8
SLAB_CH = 128
SLAB_ST = 512
SUB = 8
N_GRP = 2
N_SEG = SUB * N_GRP

SG_HEADS = 8
CHUNK = 128
HEAD_DIM = 64
ATT_HEADS = 16
ROT_DIM = 16
ROPE_THETA = 500000.0

D_IN = 13568
OFF_UA, OFF_ZA, OFF_UB, OFF_VB, OFF_ZB, OFF_Q, OFF_KV, OFF_ZC, OFF_G = (
    0, 1024, 2048, 3072, 4096, 5120, 6144, 6400, 7424)

ADAM_LR, ADAM_B1, ADAM_B2, ADAM_EPS, ADAM_WD, ADAM_STEP = 0.001, 0.9, 0.999, 1e-08, 0.01, 10

VMEM_LIMIT = 56 * 1024 * 1024


def _cp(sem=None):
    return pltpu.CompilerParams(dimension_semantics=sem, vmem_limit_bytes=VMEM_LIMIT)


def _dot(a, b):
    return jnp.dot(a, b, preferred_element_type=F32)


def _dot_nt(a, b):
    return lax.dot_general(a, b, (((1,), (1,)), ((), ())), preferred_element_type=F32)


def _dot_tn(a, b):
    return lax.dot_general(a, b, (((0,), (0,)), ((), ())), preferred_element_type=F32)


def _mm(a, b, mode, out_dtype, tm, tn, tk, name, res=None, after=None):
    if mode == "nn":
        (m, k), (_, n) = a.shape, b.shape
    elif mode == "nt":
        (m, k), (n, _) = a.shape, b.shape
    else:
        (k, m), (_, n) = a.shape, b.shape
    tm, tn, tk = min(tm, m), min(tn, n), min(tk, k)
    assert m % tm == 0 and n % tn == 0 and k % tk == 0, (name, m, n, k, tm, tn, tk)
    nk = k // tk
    a_spec = {"nn": pl.BlockSpec((tm, tk), lambda i, j, kk: (i, kk)),
              "nt": pl.BlockSpec((tm, tk), lambda i, j, kk: (i, kk)),
              "tn": pl.BlockSpec((tk, tm), lambda i, j, kk: (kk, i))}[mode]
    b_spec = {"nn": pl.BlockSpec((tk, tn), lambda i, j, kk: (kk, j)),
              "nt": pl.BlockSpec((tn, tk), lambda i, j, kk: (j, kk)),
              "tn": pl.BlockSpec((tk, tn), lambda i, j, kk: (kk, j))}[mode]
    dot = {"nn": _dot, "nt": _dot_nt, "tn": _dot_tn}[mode]
    has_res = res is not None
    direct = out_dtype == F32 and not has_res

    def body(*refs):
        ins, outs = refs[:2 + has_res + (after is not None)], refs[2 + has_res + (after is not None):]
        a_ref, b_ref = ins[:2]
        r_ref = ins[2] if has_res else None
        o_ref = outs[0]
        acc = o_ref if direct else outs[1]
        kk = pl.program_id(2)

        @pl.when(kk == 0)
        def _():
            acc[...] = jnp.zeros_like(acc)

        acc[...] += dot(a_ref[...].astype(BF16), b_ref[...].astype(BF16))

        if not direct:
            @pl.when(kk == nk - 1)
            def _():
                r = acc[...]
                if has_res:
                    r = r + r_ref[...]
                o_ref[...] = r.astype(out_dtype)

    in_specs = [a_spec, b_spec]
    args = [a, b]
    if has_res:
        in_specs.append(pl.BlockSpec((tm, tn), lambda i, j, kk: (i, j)))
        args.append(res)
    if after is not None:
        in_specs.append(pl.BlockSpec(memory_space=pl.ANY))
        args.append(after)
    return pl.pallas_call(
        body, name=name,
        grid=(m // tm, n // tn, nk),
        in_specs=in_specs,
        out_specs=pl.BlockSpec((tm, tn), lambda i, j, kk: (i, j)),
        out_shape=jax.ShapeDtypeStruct((m, n), out_dtype),
        scratch_shapes=[] if direct else [pltpu.VMEM((tm, tn), F32)],
        compiler_params=_cp(("parallel", "parallel", "arbitrary")),
    )(*args)


PROJ_TN = 256


def _in_proj_tiles(h, win_t, tiles, carry, name, after=None):
    L, K = h.shape
    extra = [a for a in (carry, after) if a is not None]

    def body(t_ref, h_ref, w_ref, *rest):
        rest[len(extra)][...] = _dot_nt(h_ref[...], w_ref[...]).astype(BF16)

    return pl.pallas_call(
        body, name=name,
        grid_spec=pltpu.PrefetchScalarGridSpec(
            num_scalar_prefetch=1, grid=(tiles.shape[0],),
            in_specs=[pl.BlockSpec((L, K), lambda j, t: (0, 0)), pl.BlockSpec((PROJ_TN, K), lambda j, t: (t[j], 0))]
            + [pl.BlockSpec(memory_space=pl.ANY)] * len(extra),
            out_specs=pl.BlockSpec((L, PROJ_TN), lambda j, t: (0, t[j]))),
        out_shape=jax.ShapeDtypeStruct((L, win_t.shape[0]), BF16),
        input_output_aliases={} if carry is None else {3: 0},
        compiler_params=_cp(("arbitrary",)),
    )(tiles, h, win_t, *extra)


def _rms(x, w):
    return x * lax.rsqrt(jnp.mean(x * x, axis=-1, keepdims=True) + EPS) * w


def _rms_fwd(x, w, name):
    L, D = x.shape
    tm = min(L, 256)

    def body(x_ref, w_ref, h_ref):
        h_ref[...] = _rms(x_ref[...], w_ref[...]).astype(BF16)

    return pl.pallas_call(
        body, name=name, grid=(L // tm,),
        in_specs=[pl.BlockSpec((tm, D), lambda i: (i, 0)), pl.BlockSpec((1, D), lambda i: (0, 0))],
        out_specs=pl.BlockSpec((tm, D), lambda i: (i, 0)),
        out_shape=jax.ShapeDtypeStruct((L, D), BF16),
        compiler_params=_cp(("parallel",)),
    )(x, w)


def _rms_bwd(x, w, dh, dres, name):
    L, D = x.shape
    tm = min(L, 256)

    def body(x_ref, w_ref, dh_ref, dres_ref, dx_ref, dw_ref):
        _, vjp = jax.vjp(_rms, x_ref[...], w_ref[...])
        dx, dw = vjp(dh_ref[...])
        dx_ref[...] = dx + dres_ref[...]

        @pl.when(pl.program_id(0) == 0)
        def _():
            dw_ref[...] = jnp.zeros_like(dw_ref)

        dw_ref[...] += dw

    row = pl.BlockSpec((tm, D), lambda i: (i, 0))
    vec = pl.BlockSpec((1, D), lambda i: (0, 0))
    return pl.pallas_call(
        body, name=name, grid=(L // tm,),
        in_specs=[row, vec, row, row],
        out_specs=[row, vec],
        out_shape=[jax.ShapeDtypeStruct((L, D), F32), jax.ShapeDtypeStruct((1, D), F32)],
        compiler_params=_cp(("arbitrary",)),
    )(x, w, dh, dres)


def _final(x, fw, tgt, name):
    L, D = x.shape
    tm = min(L, 256)

    def loss_fn(xv, wv, tv):
        err = _rms(xv, wv) - tv
        return jnp.sum(err * err) * (0.5 / D)

    def body(x_ref, w_ref, t_ref, loss_ref, dx_ref, dw_ref):
        tv = t_ref[...]
        val, vjp = jax.vjp(lambda a, b: loss_fn(a, b, tv), x_ref[...], w_ref[...])
        dx, dw = vjp(jnp.ones((), F32))
        dx_ref[...] = dx

        @pl.when(pl.program_id(0) == 0)
        def _():
            dw_ref[...] = jnp.zeros_like(dw_ref)
            loss_ref[...] = jnp.zeros_like(loss_ref)

        dw_ref[...] += dw
        loss_ref[...] += jnp.full(loss_ref.shape, val, F32)

    row = pl.BlockSpec((tm, D), lambda i: (i, 0))
    vec = pl.BlockSpec((1, D), lambda i: (0, 0))
    return pl.pallas_call(
        body, name=name, grid=(L // tm,),
        in_specs=[row, vec, row],
        out_specs=[pl.BlockSpec((8, 128), lambda i: (0, 0)), row, vec],
        out_shape=[jax.ShapeDtypeStruct((8, 128), F32), jax.ShapeDtypeStruct((L, D), F32),
                   jax.ShapeDtypeStruct((1, D), F32)],
        compiler_params=_cp(("arbitrary",)),
    )(x, fw, tgt)


def _s5_param_fn(a_re, a_im, log_dt, bt_re, bt_im):
    dt = jnp.exp(log_dt)
    zr, zi = a_re * dt, a_im * dt
    er = jnp.exp(zr)
    lr, li = er * jnp.cos(zi), er * jnp.sin(zi)
    nr, ni = lr - 1.0, li
    den = a_re * a_re + a_im * a_im
    cr = (nr * a_re + ni * a_im) / den
    ci = (ni * a_re - nr * a_im) / den
    bbr = cr[None] * bt_re - ci[None] * bt_im
    bbi = cr[None] * bt_im + ci[None] * bt_re
    return lr, li, bbr, bbi


def _s5_params_fwd(a_re, a_im, log_dt, bt_re, bt_im, name):
    def body(ar, ai, ld, br, bi, lr, li, bbr, bbi):
        o = _s5_param_fn(ar[...], ai[...], ld[...], br[...], bi[...])
        lr[...], li[...], bbr[...], bbi[...] = o

    gp = jax.ShapeDtypeStruct(a_re.shape, F32)
    cgp = jax.ShapeDtypeStruct(bt_re.shape, F32)
    return pl.pallas_call(body, name=name, out_shape=[gp, gp, cgp, cgp])(a_re, a_im, log_dt, bt_re, bt_im)


def _s5_params_bwd(a_re, a_im, log_dt, bt_re, bt_im, dlr, dli, dbbr, dbbi, name):
    def body(ar, ai, ld, br, bi, g0, g1, g2, g3, o0, o1, o2, o3, o4):
        _, vjp = jax.vjp(_s5_param_fn, ar[...], ai[...], ld[...], br[...], bi[...])
        o0[...], o1[...], o2[...], o3[...], o4[...] = vjp((g0[...], g1[...], g2[...], g3[...]))

    gp = jax.ShapeDtypeStruct(a_re.shape, F32)
    cgp = jax.ShapeDtypeStruct(bt_re.shape, F32)
    return pl.pallas_call(body, name=name,
                          out_shape=[gp, gp, jax.ShapeDtypeStruct(log_dt.shape, F32), cgp, cgp])(
        a_re, a_im, log_dt, bt_re, bt_im, dlr, dli, dbbr, dbbi)


def _cmul(ar, ai, br, bi):
    return ar * br - ai * bi, ar * bi + ai * br


def _cpow(lr, li, n):
    rr, ri = None, None
    br, bi = lr, li
    while n:
        if n & 1:
            rr, ri = (br, bi) if rr is None else _cmul(rr, ri, br, bi)
        n >>= 1
        if n:
            br, bi = _cmul(br, bi, br, bi)
    return rr, ri


def _shift_rows(x, up):
    row = lax.broadcasted_iota(jnp.int32, x.shape, 0)
    if up:
        return jnp.where(row == SUB - 1, 0.0, pltpu.roll(x, SUB - 1, 0))
    return jnp.where(row == 0, 0.0, pltpu.roll(x, 1, 0))


NT = SLAB_ST // 128


def _lam_tiles(lr_ref, li_ref):
    return [(lr_ref[:, j * 128:(j + 1) * 128], li_ref[:, j * 128:(j + 1) * 128]) for j in range(NT)]


def _row_on_sublanes(ref, j, t):
    return ref[j, pl.ds(t, SUB, stride=0), :]


def _pow_table(pw_re, pw_im, lam_t, seg):
    assert seg % 8 == 0 and (seg // 8) & (seg // 8 - 1) == 0
    for j in range(NT):
        lr, li = lam_t[j][0][0:1], lam_t[j][1][0:1]
        r, i_ = lr, li
        for row in range(8):
            pw_re[j, row:row + 1, :] = r
            pw_im[j, row:row + 1, :] = i_
            if row < 7:
                r, i_ = _cmul(r, i_, lr, li)
        n = 8
        while n < seg:
            qr, qi = _cpow(lr, li, n)
            nr, ni = _cmul(pw_re[j, 0:n, :], pw_im[j, 0:n, :], qr, qi)
            pw_re[j, n:2 * n, :] = nr
            pw_im[j, n:2 * n, :] = ni
            n *= 2


def _seg_scan(s_re, s_im, lam_t, pw_re, pw_im, seg, reverse, prev=None):
    sgn = -1.0 if reverse else 1.0
    lt = [(lr, sgn * li) for lr, li in lam_t]
    tiles = [(g, j) for g in range(N_GRP) for j in range(NT)]
    zeros = jnp.zeros((SUB, 128), F32)

    def rows(g, i):
        return pl.ds(pl.multiple_of((g * seg + i) * SUB, SUB), SUB)

    def step1(t, carry):
        i = seg - 1 - t if reverse else t
        out = []
        for n, (g, j) in enumerate(tiles):
            nr, ni = _cmul(lt[j][0], lt[j][1], carry[2 * n], carry[2 * n + 1])
            nr = nr + s_re[j, rows(g, i), :]
            ni = ni + s_im[j, rows(g, i), :]
            s_re[j, rows(g, i), :] = nr
            s_im[j, rows(g, i), :] = ni
            out += [nr, ni]
        return tuple(out)

    zero = tuple(zeros for _ in range(2 * len(tiles)))
    ends = lax.fori_loop(0, seg, step1, zero, unroll=2)

    carries = [None] * (2 * len(tiles))
    row = lax.broadcasted_iota(jnp.int32, (SUB, 128), 0)
    dist = (SUB - 1 - row) if reverse else row
    edge = 0 if reverse else SUB - 1
    for j in range(NT):
        pr, pi = _cpow(lt[j][0], lt[j][1], seg)
        qr, qi = jnp.ones((SUB, 128), F32), zeros
        for s in range(1, SUB):
            tr, ti = _cmul(qr, qi, pr, pi)
            qr, qi = jnp.where(dist >= s, tr, qr), jnp.where(dist >= s, ti, qi)
        boundary = None
        for g in (reversed(range(N_GRP)) if reverse else range(N_GRP)):
            n = g * NT + j
            cr, ci = zeros, zeros
            for _ in range(SUB - 1):
                tr, ti = _cmul(pr, pi, cr, ci)
                cr = _shift_rows(tr + ends[2 * n], reverse)
                ci = _shift_rows(ti + ends[2 * n + 1], reverse)
            if boundary is not None:
                tr, ti = _cmul(qr, qi, boundary[0], boundary[1])
                cr, ci = cr + tr, ci + ti
            carries[2 * n], carries[2 * n + 1] = cr, ci
            fr, fi = _cmul(pr, pi, cr, ci)
            boundary = (jnp.broadcast_to((fr + ends[2 * n])[edge:edge + 1], (SUB, 128)),
                        jnp.broadcast_to((fi + ends[2 * n + 1])[edge:edge + 1], (SUB, 128)))

    def fix(t, i, acc, before):
        out = []
        pws = [(_row_on_sublanes(pw_re, j, t), sgn * _row_on_sublanes(pw_im, j, t)) for j in range(NT)]
        for n, (g, j) in enumerate(tiles):
            ar, ai = _cmul(pws[j][0], pws[j][1], carries[2 * n], carries[2 * n + 1])
            ar = ar + s_re[j, rows(g, i), :]
            ai = ai + s_im[j, rows(g, i), :]
            s_re[j, rows(g, i), :] = ar
            s_im[j, rows(g, i), :] = ai
            if before is not None:
                qr, qi = before(n)
                out += [acc[2 * n] + ar * qr + ai * qi, acc[2 * n + 1] + ai * qr - ar * qi]
        return tuple(out)

    if prev is None:
        lax.fori_loop(0, seg, lambda t, c: fix(t, seg - 1 - t if reverse else t, c, None), (), unroll=2)
        return carries
    assert reverse
    p_re, p_im, p_carries = prev

    def earlier(t):
        return lambda n: (p_re[tiles[n][1], rows(tiles[n][0], seg - 2 - t), :],
                          p_im[tiles[n][1], rows(tiles[n][0], seg - 2 - t), :])

    acc = lax.fori_loop(0, seg - 1, lambda t, c: fix(t, seg - 1 - t, c, earlier(t)), zero)
    acc = fix(seg - 1, 0, acc, lambda n: (p_carries[2 * n], p_carries[2 * n + 1]))
    return carries, [sum(acc[2 * (g * NT + j) + part] for g in range(N_GRP)) for j in range(NT) for part in range(2)]


S5_RB = 512


def _seg_slice(k, seg):
    g, r = divmod(k, SUB)
    return pl.ds(g * seg * SUB + r, seg, stride=SUB)


def _to_step_major(src_ref, dst_ref, seg):
    for k in range(N_SEG):
        dst_ref[_seg_slice(k, seg), :] = src_ref[pl.ds(k * seg, seg), :].astype(F32)


def _from_step_major(src_ref, dst_ref, seg):
    for k in range(N_SEG):
        dst_ref[pl.ds(k * seg, seg), :] = src_ref[_seg_slice(k, seg), :].astype(dst_ref.dtype)


def _blocks(L):
    rb = min(S5_RB, L)
    return [pl.ds(b * rb, rb) for b in range(L // rb)]


def _lanes_of(ref, rows):
    return jnp.concatenate([ref[j, rows, :] for j in range(NT)], axis=-1)


def _lanes_to(ref, rows, val):
    for j in range(NT):
        ref[j, rows, :] = val[:, j * 128:(j + 1) * 128]


def _s5_specs(L):
    col = lambda off: pl.BlockSpec((L, SLAB_CH), lambda j: (0, off + j))
    mat_b = pl.BlockSpec((None, SLAB_CH, SLAB_ST), lambda j: (j, 0, 0))
    mat_c = pl.BlockSpec((None, SLAB_ST, SLAB_CH), lambda j: (j, 0, 0))
    vec_s = pl.BlockSpec((None, SUB, SLAB_ST), lambda j: (j, 0, 0))
    vec_c = pl.BlockSpec((None, 1, SLAB_CH), lambda j: (j, 0, 0))
    return col, mat_b, mat_c, vec_s, vec_c


def _s5_states(u_ref, u_sm, bre_ref, bim_ref, lam_t, pw_re, pw_im, s_re, s_im, seg):
    _pow_table(pw_re, pw_im, lam_t, seg)
    _to_step_major(u_ref, u_sm, seg)
    for rows in _blocks(u_sm.shape[0]):
        ub = u_sm[rows, :].astype(BF16)
        _lanes_to(s_re, rows, _dot(ub, bre_ref[...]))
        _lanes_to(s_im, rows, _dot(ub, bim_ref[...]))
    return _seg_scan(s_re, s_im, lam_t, pw_re, pw_im, seg, reverse=False)


def _s5_fwd(proj, bre, bim, cre_t, cim_t, lam_re, lam_im, dvec, name):
    L = proj.shape[0]
    seg = L // N_SEG
    col, mat_b, mat_c, vec_s, vec_c = _s5_specs(L)

    def body(u_ref, bre_ref, bim_ref, cre_ref, cim_ref, lr_ref, li_ref, d_ref, y_ref, s_re, s_im, pw_re, pw_im, u_sm, y_sm):
        _s5_states(u_ref, u_sm, bre_ref, bim_ref, _lam_tiles(lr_ref, li_ref), pw_re, pw_im, s_re, s_im, seg)
        for rows in _blocks(L):
            y = (_dot(_lanes_of(s_re, rows).astype(BF16), cre_ref[...])
                 - _dot(_lanes_of(s_im, rows).astype(BF16), cim_ref[...]))
            y_sm[rows, :] = jax.nn.gelu(y + d_ref[...] * u_sm[rows, :])
        _from_step_major(y_sm, y_ref, seg)

    lane_tile = pltpu.VMEM((L, SLAB_CH), F32)
    return pl.pallas_call(
        body, name=name, grid=(N_SLAB,),
        in_specs=[col(OFF_UA // SLAB_CH), mat_b, mat_b, mat_c, mat_c, vec_s, vec_s, vec_c],
        out_specs=pl.BlockSpec((L, SLAB_CH), lambda j: (0, j)),
        out_shape=jax.ShapeDtypeStruct((L, SSM_WIDTH), BF16),
        scratch_shapes=[pltpu.VMEM((NT, L, 128), F32)] * 2 + [pltpu.VMEM((NT, seg, 128), F32)] * 2 + [lane_tile] * 2,
        compiler_params=_cp(("parallel",)),
    )(proj, bre, bim, cre_t, cim_t, lam_re, lam_im, dvec)


def _s5_bwd(proj, dy, bre, bim, cre_t, cim_t, lam_re, lam_im, dvec, name):
    L = proj.shape[0]
    seg = L // N_SEG
    col, mat_b, mat_c, vec_s, vec_c = _s5_specs(L)
    dlam_spec = pl.BlockSpec((None, 1, SLAB_ST), lambda j: (j, 0, 0))

    def body(u_ref, dy_ref, bre_ref, bim_ref, cre_ref, cim_ref, lr_ref, li_ref, d_ref,
             du_ref, dbre_ref, dbim_ref, dcre_ref, dcim_ref, dlr_ref, dli_ref, dd_ref,
             s_re, s_im, a_re, a_im, pw_re, pw_im, u_sm, dyp, io_sm):
        lam_t = _lam_tiles(lr_ref, li_ref)
        carry_s = _s5_states(u_ref, u_sm, bre_ref, bim_ref, lam_t, pw_re, pw_im, s_re, s_im, seg)
        _to_step_major(dy_ref, io_sm, seg)
        dcre = jnp.zeros((SLAB_ST, SLAB_CH), F32)
        dcim = jnp.zeros((SLAB_ST, SLAB_CH), F32)
        dd = jnp.zeros((1, SLAB_CH), F32)
        for rows in _blocks(L):
            sre = _lanes_of(s_re, rows).astype(BF16)
            sim = _lanes_of(s_im, rows).astype(BF16)
            uk = u_sm[rows, :]
            ypre = _dot(sre, cre_ref[...]) - _dot(sim, cim_ref[...]) + d_ref[...] * uk
            _, vjp = jax.vjp(jax.nn.gelu, ypre)
            (dyk,) = vjp(io_sm[rows, :])
            dyp[rows, :] = dyk
            dd = dd + jnp.sum(dyk * uk, axis=0, keepdims=True)
            dyb = dyk.astype(BF16)
            dcre = dcre + _dot_tn(sre, dyb)
            dcim = dcim - _dot_tn(sim, dyb)
            _lanes_to(a_re, rows, _dot_nt(dyb, cre_ref[...]))
            _lanes_to(a_im, rows, -_dot_nt(dyb, cim_ref[...]))
        dcre_ref[...] = dcre
        dcim_ref[...] = dcim
        dd_ref[...] = dd

        _, acc = _seg_scan(a_re, a_im, lam_t, pw_re, pw_im, seg, reverse=True, prev=(s_re, s_im, carry_s))
        dlr_ref[...] = jnp.concatenate([jnp.sum(acc[2 * j], axis=0, keepdims=True) for j in range(NT)], axis=-1)
        dli_ref[...] = jnp.concatenate([jnp.sum(acc[2 * j + 1], axis=0, keepdims=True) for j in range(NT)], axis=-1)

        dbre = jnp.zeros((SLAB_CH, SLAB_ST), F32)
        dbim = jnp.zeros((SLAB_CH, SLAB_ST), F32)
        for rows in _blocks(L):
            are = _lanes_of(a_re, rows).astype(BF16)
            aim = _lanes_of(a_im, rows).astype(BF16)
            ub = u_sm[rows, :].astype(BF16)
            io_sm[rows, :] = _dot_nt(are, bre_ref[...]) + _dot_nt(aim, bim_ref[...]) + dyp[rows, :] * d_ref[...]
            dbre = dbre + _dot_tn(ub, are)
            dbim = dbim + _dot_tn(ub, aim)
        _from_step_major(io_sm, du_ref, seg)
        dbre_ref[...] = dbre
        dbim_ref[...] = dbim

    scan_buf = pltpu.VMEM((NT, L, 128), F32)
    pow_buf = pltpu.VMEM((NT, seg, 128), F32)
    lane_tile = pltpu.VMEM((L, SLAB_CH), F32)
    return pl.pallas_call(
        body, name=name, grid=(N_SLAB,),
        in_specs=[col(OFF_UA // SLAB_CH), pl.BlockSpec((L, SLAB_CH), lambda j: (0, j)),
                  mat_b, mat_b, mat_c, mat_c, vec_s, vec_s, vec_c],
        out_specs=[pl.BlockSpec((L, SLAB_CH), lambda j: (0, j)), mat_b, mat_b, mat_c, mat_c, dlam_spec, dlam_spec, vec_c],
        out_shape=[jax.ShapeDtypeStruct((L, SSM_WIDTH), BF16),
                   jax.ShapeDtypeStruct((N_SLAB, SLAB_CH, SLAB_ST), F32),
                   jax.ShapeDtypeStruct((N_SLAB, SLAB_CH, SLAB_ST), F32),
                   jax.ShapeDtypeStruct((N_SLAB, SLAB_ST, SLAB_CH), F32),
                   jax.ShapeDtypeStruct((N_SLAB, SLAB_ST, SLAB_CH), F32),
                   jax.ShapeDtypeStruct((N_SLAB, 1, SLAB_ST), F32),
                   jax.ShapeDtypeStruct((N_SLAB, 1, SLAB_ST), F32),
                   jax.ShapeDtypeStruct((N_SLAB, 1, SLAB_CH), F32)],
        scratch_shapes=[scan_buf, scan_buf, scan_buf, scan_buf, pow_buf, pow_buf, lane_tile, lane_tile, lane_tile],
        compiler_params=_cp(("parallel",)),
    )(proj, dy, bre, bim, cre_t, cim_t, lam_re, lam_im, dvec)


def _glu_point(y0, pre, za, b):
    return y0 * jax.nn.sigmoid(pre + b) * jax.nn.silu(za)


def _glu_specs(L, tm):
    row = pl.BlockSpec((tm, SSM_WIDTH), lambda i: (i, 0))
    za = pl.BlockSpec((tm, SSM_WIDTH), lambda i: (i, OFF_ZA // SSM_WIDTH))
    wmat = pl.BlockSpec((SSM_WIDTH, SSM_WIDTH), lambda i: (0, 0))
    vec = pl.BlockSpec((1, SSM_WIDTH), lambda i: (0, 0))
    return row, za, wmat, vec


def _glu_fwd(ya0, proj, w, b, name):
    L = ya0.shape[0]
    tm = min(L, 512)
    row, za, wmat, vec = _glu_specs(L, tm)

    def body(y_ref, z_ref, w_ref, b_ref, o_ref):
        y0 = y_ref[...]
        pre = _dot(y0, w_ref[...])
        o_ref[...] = _glu_point(y0.astype(F32), pre, z_ref[...].astype(F32), b_ref[...]).astype(BF16)

    return pl.pallas_call(
        body, name=name, grid=(L // tm,), in_specs=[row, za, wmat, vec], out_specs=row,
        out_shape=jax.ShapeDtypeStruct((L, SSM_WIDTH), BF16), compiler_params=_cp(("parallel",)),
    )(ya0, proj, w, b)


def _glu_bwd(ya0, proj, w, b, dya, name):
    L = ya0.shape[0]
    tm = min(L, 512)
    row, za, wmat, vec = _glu_specs(L, tm)

    def body(y_ref, z_ref, w_ref, b_ref, g_ref, dy0_ref, dza_ref, dw_ref, db_ref):
        y0 = y_ref[...]
        pre = _dot(y0, w_ref[...])
        _, vjp = jax.vjp(_glu_point, y0.astype(F32), pre, z_ref[...].astype(F32), b_ref[...])
        dy0, dpre, dza, db = vjp(g_ref[...].astype(F32))
        dpb = dpre.astype(BF16)
        dy0_ref[...] = (dy0 + _dot_nt(dpb, w_ref[...])).astype(BF16)
        dza_ref[...] = dza.astype(BF16)

        @pl.when(pl.program_id(0) == 0)
        def _():
            dw_ref[...] = jnp.zeros_like(dw_ref)
            db_ref[...] = jnp.zeros_like(db_ref)

        dw_ref[...] += _dot_tn(y0, dpb)
        db_ref[...] += db

    return pl.pallas_call(
        body, name=name, grid=(L // tm,), in_specs=[row, za, wmat, vec, row],
        out_specs=[row, row, wmat, vec],
        out_shape=[jax.ShapeDtypeStruct((L, SSM_WIDTH), BF16), jax.ShapeDtypeStruct((L, SSM_WIDTH), BF16),
                   jax.ShapeDtypeStruct((SSM_WIDTH, SSM_WIDTH), F32), jax.ShapeDtypeStruct((1, SSM_WIDTH), F32)],
        compiler_params=_cp(("arbitrary",)),
    )(ya0, proj, w, b, dya)


def _sg_norm(vb, ln_w, ln_b):
    v0 = jax.nn.gelu(vb)
    mu = jnp.mean(v0, axis=-1, keepdims=True)
    var = jnp.mean(jnp.square(v0 - mu), axis=-1, keepdims=True)
    return (v0 - mu) * lax.rsqrt(var + EPS) * ln_w + ln_b


def _sg_gate(ub, mixed, zb):
    return jax.nn.gelu(ub) * mixed * jax.nn.silu(zb)


def _sg_specs():
    W = SSM_WIDTH
    blk = lambda off: pl.BlockSpec((CHUNK, W), lambda n: (n, off // W))
    out = pl.BlockSpec((CHUNK, W), lambda n: (n, 0))
    vec = pl.BlockSpec((1, W), lambda n: (0, 0))
    wsp = pl.BlockSpec((SG_HEADS, CHUNK, CHUNK), lambda n: (0, 0, 0))
    bsp = pl.BlockSpec((SG_HEADS, CHUNK, 1), lambda n: (0, 0, 0))
    return blk, out, vec, wsp, bsp


def _sg_masked(w_ref):
    t = lax.broadcasted_iota(jnp.int32, (CHUNK, CHUNK), 0)
    s = lax.broadcasted_iota(jnp.int32, (CHUNK, CHUNK), 1)
    causal = s <= t
    return causal, [jnp.where(causal, w_ref[h], 0.0).astype(BF16) for h in range(SG_HEADS)]


def _sg_mix(wm, vnb, bias_ref):
    return jnp.concatenate(
        [_dot(wm[h], vnb[:, h * CHUNK:(h + 1) * CHUNK]) + bias_ref[h] for h in range(SG_HEADS)], axis=-1)


def _sg_fwd(proj, ln_w, ln_b, w, bias, name):
    L = proj.shape[0]
    blk, out, vec, wsp, bsp = _sg_specs()

    def body(ub_ref, vb_ref, zb_ref, lw_ref, lb_ref, w_ref, bias_ref, o_ref):
        _, wm = _sg_masked(w_ref)
        vnb = _sg_norm(vb_ref[...].astype(F32), lw_ref[...], lb_ref[...]).astype(BF16)
        mixed = _sg_mix(wm, vnb, bias_ref)
        o_ref[...] = _sg_gate(ub_ref[...].astype(F32), mixed, zb_ref[...].astype(F32)).astype(BF16)

    return pl.pallas_call(
        body, name=name, grid=(L // CHUNK,),
        in_specs=[blk(OFF_UB), blk(OFF_VB), blk(OFF_ZB), vec, vec, wsp, bsp], out_specs=out,
        out_shape=jax.ShapeDtypeStruct((L, SSM_WIDTH), BF16), compiler_params=_cp(("parallel",)),
    )(proj, proj, proj, ln_w, ln_b, w, bias)


def _sg_bwd(proj, ln_w, ln_b, w, bias, dyb, name):
    L = proj.shape[0]
    blk, out, vec, wsp, bsp = _sg_specs()

    def body(ub_ref, vb_ref, zb_ref, lw_ref, lb_ref, w_ref, bias_ref, g_ref,
             dub_ref, dvb_ref, dzb_ref, dlw_ref, dlb_ref, dw_ref, dbias_ref):
        causal, wm = _sg_masked(w_ref)
        vb = vb_ref[...].astype(F32)
        vn, vjp_norm = jax.vjp(_sg_norm, vb, lw_ref[...], lb_ref[...])
        vnb = vn.astype(BF16)
        mixed = _sg_mix(wm, vnb, bias_ref)
        _, vjp_gate = jax.vjp(_sg_gate, ub_ref[...].astype(F32), mixed, zb_ref[...].astype(F32))
        dub, dmixed, dzb = vjp_gate(g_ref[...].astype(F32))
        dub_ref[...] = dub.astype(BF16)
        dzb_ref[...] = dzb.astype(BF16)

        @pl.when(pl.program_id(0) == 0)
        def _():
            dlw_ref[...] = jnp.zeros_like(dlw_ref)
            dlb_ref[...] = jnp.zeros_like(dlb_ref)
            dw_ref[...] = jnp.zeros_like(dw_ref)
            dbias_ref[...] = jnp.zeros_like(dbias_ref)

        dvn = []
        for h in range(SG_HEADS):
            dm = dmixed[:, h * CHUNK:(h + 1) * CHUNK]
            dmb = dm.astype(BF16)
            dbias_ref[h] += jnp.sum(dm, axis=-1, keepdims=True)
            dw_ref[h] += jnp.where(causal, _dot_nt(dmb, vnb[:, h * CHUNK:(h + 1) * CHUNK]), 0.0)
            dvn.append(_dot_tn(wm[h], dmb))
        dvb, dlw, dlb = vjp_norm(jnp.concatenate(dvn, axis=-1))
        dvb_ref[...] = dvb.astype(BF16)
        dlw_ref[...] += dlw
        dlb_ref[...] += dlb

    act = jax.ShapeDtypeStruct((L, SSM_WIDTH), BF16)
    return pl.pallas_call(
        body, name=name, grid=(L // CHUNK,),
        in_specs=[blk(OFF_UB), blk(OFF_VB), blk(OFF_ZB), vec, vec, wsp, bsp, out],
        out_specs=[out, out, out, vec, vec, wsp, bsp],
        out_shape=[act, act, act, jax.ShapeDtypeStruct((1, SSM_WIDTH), F32), jax.ShapeDtypeStruct((1, SSM_WIDTH), F32),
                   jax.ShapeDtypeStruct((SG_HEADS, CHUNK, CHUNK), F32), jax.ShapeDtypeStruct((SG_HEADS, CHUNK, 1), F32)],
        compiler_params=_cp(("arbitrary",)),
    )(proj, proj, proj, ln_w, ln_b, w, bias, dyb)


def _rope_tables(L):
    half = ROT_DIM // 2
    inv_freq = ROPE_THETA ** (-jnp.arange(0, ROT_DIM, 2, dtype=F32) / ROT_DIM)
    ang = jnp.arange(L, dtype=F32)[:, None] * inv_freq[None, :]
    cos, sin = jnp.cos(ang), jnp.sin(ang)
    ones = jnp.ones((L, HEAD_DIM - ROT_DIM), F32)
    cos_h = jnp.concatenate([cos, cos, ones], axis=-1)
    sin_h = jnp.concatenate([-sin, sin, 0.0 * ones], axis=-1)
    src = jnp.arange(HEAD_DIM)[:, None]
    dst = jnp.arange(HEAD_DIM)[None, :]
    p_h = (((dst < half) & (src == dst + half)) | ((dst >= half) & (dst < ROT_DIM) & (src == dst - half))).astype(F32)
    p2 = jnp.kron(jnp.eye(2, dtype=F32), p_h).astype(BF16)
    return jnp.tile(cos_h, (1, 2)), jnp.tile(sin_h, (1, 2)), p2


def _rope(t, cos, sin, p2):
    n = t.shape[1] // 128
    tb = t.astype(BF16)
    sw = jnp.concatenate([_dot(tb[:, i * 128:(i + 1) * 128], p2) for i in range(n)], axis=-1) if n > 1 else _dot(tb, p2)
    return t * jnp.tile(cos, (1, n)) + sw * jnp.tile(sin, (1, n))


def _rope_t(g, cos, sin, p2):
    n = g.shape[1] // 128
    gs = (g * jnp.tile(sin, (1, n))).astype(BF16)
    sw = jnp.concatenate([_dot_nt(gs[:, i * 128:(i + 1) * 128], p2) for i in range(n)], axis=-1) if n > 1 else _dot_nt(gs, p2)
    return g * jnp.tile(cos, (1, n)) + sw


def _lane_lo(shape):
    return (lax.broadcasted_iota(jnp.int32, shape, len(shape) - 1) % 128) < HEAD_DIM


def _dup_halves(x):
    xr = pltpu.roll(x, HEAD_DIM, 1)
    lo = _lane_lo(x.shape)
    return jnp.where(lo, x, xr), jnp.where(lo, xr, x)


def _fold_halves(d0, d1):
    f0 = d0 + pltpu.roll(d0, HEAD_DIM, 1)
    f1 = d1 + pltpu.roll(d1, HEAD_DIM, 1)
    return jnp.where(_lane_lo(d0.shape), f0, f1)


def _attn_mask():
    qi = lax.broadcasted_iota(jnp.int32, (CHUNK, 2 * CHUNK), 0)
    kj = lax.broadcasted_iota(jnp.int32, (CHUNK, 2 * CHUNK), 1)
    return qi, kj


def _attn_specs():
    qsp = pl.BlockSpec((CHUNK, 1024), lambda n: (n, OFF_Q // 1024))
    kv_cur = pl.BlockSpec((CHUNK, 256), lambda n: (n, OFF_KV // 256))
    kv_prev = pl.BlockSpec((CHUNK, 256), lambda n: (jnp.maximum(n - 1, 0), OFF_KV // 256))
    zsp = [pl.BlockSpec((CHUNK, 256), functools.partial(lambda n, q: (n, OFF_ZC // 256 + q), q=q)) for q in range(4)]
    tab_cur = pl.BlockSpec((CHUNK, 128), lambda n: (n, 0))
    tab_prev = pl.BlockSpec((CHUNK, 128), lambda n: (jnp.maximum(n - 1, 0), 0))
    p2sp = pl.BlockSpec((128, 128), lambda n: (0, 0))
    sink = pl.BlockSpec(memory_space=pltpu.SMEM)
    wide = pl.BlockSpec((CHUNK, 1024), lambda n: (n, 0))
    return qsp, kv_cur, kv_prev, zsp, tab_cur, tab_prev, p2sp, sink, wide


def _attn_prep(n, q_ref, kvc_ref, kvp_ref, cosc_ref, sinc_ref, cosp_ref, sinp_ref, p2_ref):
    p2 = p2_ref[...]
    qr = _rope(q_ref[...].astype(F32), cosc_ref[...], sinc_ref[...], p2).astype(BF16)
    kc = _rope(kvc_ref[:, 0:128].astype(F32), cosc_ref[...], sinc_ref[...], p2)
    kp = _rope(kvp_ref[:, 0:128].astype(F32), cosp_ref[...], sinp_ref[...], p2)
    k_all = jnp.concatenate([kp, kc], axis=0).astype(BF16)
    v_all = jnp.concatenate([kvp_ref[:, 128:256], kvc_ref[:, 128:256]], axis=0)
    qi, kj = _attn_mask()
    allowed = ((kj < CHUNK) & (kj > qi) & (n > 0)) | ((kj >= CHUNK) & (kj - CHUNK <= qi))
    return qr, _dup_halves(k_all), _dup_halves(v_all), allowed, _lane_lo((CHUNK, 128))


def _attn_head(qr, kd, sink_ref, h, allowed, lo):
    m, half, g = h // 2, h % 2, h // 8
    qp = qr[:, m * 128:(m + 1) * 128]
    qm = jnp.where(lo if half == 0 else ~lo, qp, jnp.zeros_like(qp))
    s = jnp.where(allowed, _dot_nt(qm, kd[g]) * (HEAD_DIM ** -0.5), NEG_INF)
    snk = sink_ref[h]
    mx = jnp.maximum(jnp.max(s, axis=-1, keepdims=True), snk)
    e = jnp.exp(s - mx)
    es = jnp.exp(snk - mx)
    inv = 1.0 / (jnp.sum(e, axis=-1, keepdims=True) + es)
    return qm, e * inv, es * inv


def _silu_gate(o, z):
    return o * jax.nn.silu(z)


def _pair_lanes(refs, m):
    return refs[m // 2][:, (m % 2) * 128:(m % 2 + 1) * 128]


def _attn_fwd(proj, sinks, tabs, name):
    L = proj.shape[0]
    cos2, sin2, p2 = tabs
    qsp, kv_cur, kv_prev, zsp, tab_cur, tab_prev, p2sp, sink, wide = _attn_specs()

    def body(q_ref, kvc_ref, kvp_ref, z0, z1, z2, z3, cosc, sinc, cosp, sinp, p2_ref, sink_ref, y_ref, o_ref):
        n = pl.program_id(0)
        qr, kd, vd, allowed, lo = _attn_prep(n, q_ref, kvc_ref, kvp_ref, cosc, sinc, cosp, sinp, p2_ref)
        probs = [_attn_head(qr, kd, sink_ref, h, allowed, lo)[1].astype(BF16) for h in range(ATT_HEADS)]
        for m in range(ATT_HEADS // 2):
            g = m // 4
            o0 = _dot(probs[2 * m], vd[g])
            o1 = _dot(probs[2 * m + 1], vd[g])
            o = jnp.where(lo, o0, o1).astype(BF16)
            o_ref[:, m * 128:(m + 1) * 128] = o
            z = _pair_lanes((z0, z1, z2, z3), m).astype(F32)
            y_ref[:, m * 128:(m + 1) * 128] = _silu_gate(o.astype(F32), z).astype(BF16)

    act = jax.ShapeDtypeStruct((L, 1024), BF16)
    return pl.pallas_call(
        body, name=name, grid=(L // CHUNK,),
        in_specs=[qsp, kv_cur, kv_prev, *zsp, tab_cur, tab_cur, tab_prev, tab_prev, p2sp, sink],
        out_specs=[wide, wide], out_shape=[act, act], compiler_params=_cp(("parallel",)),
    )(proj, proj, proj, proj, proj, proj, proj, cos2, sin2, cos2, sin2, p2, sinks)


def _attn_bwd(proj, sinks, tabs, o_att, dyc, name):
    L = proj.shape[0]
    cos2, sin2, p2 = tabs
    qsp, kv_cur, kv_prev, zsp, tab_cur, tab_prev, p2sp, sink, wide = _attn_specs()
    kvo = pl.BlockSpec((CHUNK, 256), lambda n: (n, 0))

    def body(q_ref, kvc_ref, kvp_ref, z0, z1, z2, z3, cosc, sinc, cosp, sinp, p2_ref, sink_ref, o_ref, g_ref,
             dq_ref, dz_ref, dkvc_ref, dkvp_ref, dsink_ref):
        n = pl.program_id(0)
        qr, kd, vd, allowed, lo = _attn_prep(n, q_ref, kvc_ref, kvp_ref, cosc, sinc, cosp, sinp, p2_ref)
        p2 = p2_ref[...]

        @pl.when(n == 0)
        def _():
            dsink_ref[...] = jnp.zeros_like(dsink_ref)

        dkd = [jnp.zeros((2 * CHUNK, 128), F32), jnp.zeros((2 * CHUNK, 128), F32)]
        dvd = [jnp.zeros((2 * CHUNK, 128), F32), jnp.zeros((2 * CHUNK, 128), F32)]
        probs = [_attn_head(qr, kd, sink_ref, h, allowed, lo) for h in range(ATT_HEADS)]
        for m in range(ATT_HEADS // 2):
            g = m // 4
            lanes = slice(m * 128, (m + 1) * 128)
            z = _pair_lanes((z0, z1, z2, z3), m).astype(F32)
            _, vjp = jax.vjp(_silu_gate, o_ref[:, lanes].astype(F32), z)
            do, dz = vjp(g_ref[:, lanes].astype(F32))
            dz_ref[:, lanes] = dz.astype(BF16)
            dop = do.astype(BF16)
            dq_h = []
            for half in range(2):
                h = 2 * m + half
                qm, p, ps = probs[h]
                dom = jnp.where(lo if half == 0 else ~lo, dop, jnp.zeros_like(dop))
                dp = _dot_nt(dom, vd[g])
                rs = jnp.sum(p * dp, axis=-1, keepdims=True)
                ds = (p * (dp - rs) * (HEAD_DIM ** -0.5)).astype(BF16)
                dsink_ref[h:h + 1, :] += jnp.broadcast_to(jnp.sum(-ps * rs, axis=0, keepdims=True), (1, 128))
                dq_h.append(_dot(ds, kd[g]))
                dkd[g] = dkd[g] + _dot_tn(ds, qm)
                dvd[g] = dvd[g] + _dot_tn(p.astype(BF16), dom)
            dq_ref[:, lanes] = _rope_t(jnp.where(lo, dq_h[0], dq_h[1]), cosc[...], sinc[...], p2).astype(BF16)
        dk_rot = _fold_halves(dkd[0], dkd[1])
        dv = _fold_halves(dvd[0], dvd[1])
        dkp = _rope_t(dk_rot[0:CHUNK], cosp[...], sinp[...], p2)
        dkc = _rope_t(dk_rot[CHUNK:2 * CHUNK], cosc[...], sinc[...], p2)
        dkvp_ref[...] = jnp.concatenate([dkp, dv[0:CHUNK]], axis=-1)
        dkvc_ref[...] = jnp.concatenate([dkc, dv[CHUNK:2 * CHUNK]], axis=-1)

    act = jax.ShapeDtypeStruct((L, 1024), BF16)
    kvs = jax.ShapeDtypeStruct((L, 256), F32)
    return pl.pallas_call(
        body, name=name, grid=(L // CHUNK,),
        in_specs=[qsp, kv_cur, kv_prev, *zsp, tab_cur, tab_cur, tab_prev, tab_prev, p2sp, sink, wide, wide],
        out_specs=[wide, wide, kvo, kvo, pl.BlockSpec((ATT_HEADS, 128), lambda n: (0, 0))],
        out_shape=[act, act, kvs, kvs, jax.ShapeDtypeStruct((ATT_HEADS, 128), F32)],
        compiler_params=_cp(("arbitrary",)),
    )(proj, proj, proj, proj, proj, proj, proj, cos2, sin2, cos2, sin2, p2, sinks, o_att, dyc)


MERGE_TN = 256


def _merge_point(ta, tb, tc, ga, gb, gc):
    return jax.nn.sigmoid(ga) * ta + jax.nn.sigmoid(gb) * tb + jax.nn.sigmoid(gc) * tc


def _merge_specs(tm):
    nj = D_MODEL // MERGE_TN
    t = pl.BlockSpec((tm, MERGE_TN), lambda i, j: (i, j))
    gates = [pl.BlockSpec((tm, MERGE_TN), functools.partial(lambda i, j, b: (i, OFF_G // MERGE_TN + b * nj + j), b=b))
             for b in range(3)]
    return t, gates, nj


def _merge_fwd(ta, tb, tc, proj, name):
    L = ta.shape[0]
    tm = min(L, 1024)
    t, gates, nj = _merge_specs(tm)

    def body(ta_ref, tb_ref, tc_ref, ga_ref, gb_ref, gc_ref, o_ref):
        f = lambda r: r[...].astype(F32)
        o_ref[...] = _merge_point(f(ta_ref), f(tb_ref), f(tc_ref), f(ga_ref), f(gb_ref), f(gc_ref)).astype(BF16)

    return pl.pallas_call(
        body, name=name, grid=(L // tm, nj), in_specs=[t, t, t, *gates], out_specs=t,
        out_shape=jax.ShapeDtypeStruct((L, D_MODEL), BF16), compiler_params=_cp(("parallel", "parallel")),
    )(ta, tb, tc, proj, proj, proj)


def _merge_bwd(ta, tb, tc, proj, dm, name):
    L = ta.shape[0]
    tm = min(L, 1024)
    t, gates, nj = _merge_specs(tm)

    def body(ta_ref, tb_ref, tc_ref, ga_ref, gb_ref, gc_ref, dm_ref, dta_ref, dtb_ref, dtc_ref, dga_ref, dgb_ref, dgc_ref):
        f = lambda r: r[...].astype(F32)
        _, vjp = jax.vjp(_merge_point, f(ta_ref), f(tb_ref), f(tc_ref), f(ga_ref), f(gb_ref), f(gc_ref))
        outs = vjp(f(dm_ref))
        for r, v in zip((dta_ref, dtb_ref, dtc_ref, dga_ref, dgb_ref, dgc_ref), outs):
            r[...] = v.astype(BF16)

    act = jax.ShapeDtypeStruct((L, D_MODEL), BF16)
    return pl.pallas_call(
        body, name=name, grid=(L // tm, nj), in_specs=[t, t, t, *gates, t],
        out_specs=[t] * 6, out_shape=[act] * 6,
        compiler_params=_cp(("parallel", "parallel")),
    )(ta, tb, tc, proj, proj, proj, dm)


def _concat_cols(parts, name):
    L = parts[0].shape[0]
    tm = min(L, 256)
    widths = [p.shape[1] for p in parts]

    def body(*refs):
        off = 0
        for ref, wd in zip(refs[:-1], widths):
            refs[-1][:, off:off + wd] = ref[...].astype(BF16)
            off += wd

    return pl.pallas_call(
        body, name=name, grid=(L // tm,),
        in_specs=[pl.BlockSpec((tm, wd), lambda i: (i, 0)) for wd in widths],
        out_specs=pl.BlockSpec((tm, sum(widths)), lambda i: (i, 0)),
        out_shape=jax.ShapeDtypeStruct((L, sum(widths)), BF16),
        compiler_params=_cp(("parallel",)),
    )(*parts)


GRAD_DT = BF16
SMALL = ("norm_w", "ssm_a_re", "ssm_a_im", "ssm_log_dt", "ssm_b_re", "ssm_b_im", "ssm_c_re", "ssm_c_im", "ssm_d",
         "ssm_glu_b", "sg_ln_w", "sg_ln_b", "sg_w", "sg_b", "attn_sinks")
G8 = SSM_GROUPS // N_SLAB


def _diag_mask(rows_per_group, cols_per_group):
    r = jnp.arange(G8 * rows_per_group)[:, None] // rows_per_group
    c = jnp.arange(G8 * cols_per_group)[None, :] // cols_per_group
    return r == c


def _slab_b(bb_t):
    x = bb_t.transpose(1, 0, 2).reshape(N_SLAB, SLAB_CH, SSM_STATE)
    return jnp.where(_diag_mask(SSM_GROUP, SSM_STATE), jnp.tile(x, (1, 1, G8)), 0)


def _unslab_b(d):
    x = jnp.where(_diag_mask(SSM_GROUP, SSM_STATE), d, 0).reshape(N_SLAB, SLAB_CH, G8, SSM_STATE).sum(axis=2)
    return x.reshape(SSM_GROUPS, SSM_GROUP, SSM_STATE).transpose(1, 0, 2)


def _slab_c(c):
    x = c.transpose(0, 2, 1).reshape(N_SLAB, SLAB_ST, SSM_GROUP)
    return jnp.where(_diag_mask(SSM_STATE, SSM_GROUP), jnp.tile(x, (1, 1, G8)), 0)


def _unslab_c(d):
    x = jnp.where(_diag_mask(SSM_STATE, SSM_GROUP), d, 0).reshape(N_SLAB, SLAB_ST, G8, SSM_GROUP).sum(axis=2)
    return x.reshape(SSM_GROUPS, SSM_STATE, SSM_GROUP).transpose(0, 2, 1)


def _s5_prep(p, tag):
    bt_re = p["ssm_b_re"].transpose(2, 0, 1)
    bt_im = p["ssm_b_im"].transpose(2, 0, 1)
    raw = (p["ssm_a_re"], p["ssm_a_im"], p["ssm_log_dt"][:, None], bt_re, bt_im)
    lr, li, bbr, bbi = _s5_params_fwd(*raw, name=f"s5_params_{tag}")
    ops = (_slab_b(bbr).astype(BF16), _slab_b(bbi).astype(BF16),
           _slab_c(p["ssm_c_re"]).astype(BF16), _slab_c(p["ssm_c_im"]).astype(BF16),
           jnp.broadcast_to(lr.reshape(N_SLAB, 1, SLAB_ST), (N_SLAB, SUB, SLAB_ST)),
           jnp.broadcast_to(li.reshape(N_SLAB, 1, SLAB_ST), (N_SLAB, SUB, SLAB_ST)),
           p["ssm_d"].reshape(N_SLAB, 1, SLAB_CH))
    return raw, ops


def _layer_fwd(x, p, w, tabs, tag, s5=None, proj_of=None, after_proj=None):
    L = x.shape[0]
    h = _rms_fwd(x, p["norm_w"][None], f"rms_fwd_{tag}")
    if proj_of is not None:
        proj = proj_of(h)
    else:
        proj = _mm(h, w["win_t"], "nt", BF16, L, PROJ_TN, D_MODEL, f"in_proj_{tag}")
    if after_proj is not None:
        w = after_proj(proj)
    s5_raw, s5_ops = s5 if s5 is not None else _s5_prep(p, tag)
    ya0 = _s5_fwd(proj, *s5_ops, name=f"s5_fwd_{tag}")
    ya = _glu_fwd(ya0, proj, w["glu"], p["ssm_glu_b"][None], f"glu_fwd_{tag}")
    yb = _sg_fwd(proj, p["sg_ln_w"][None], p["sg_ln_b"][None], p["sg_w"], p["sg_b"][:, :, None], f"sg_fwd_{tag}")
    yc, o_att = _attn_fwd(proj, p["attn_sinks"], tabs, f"attn_fwd_{tag}")
    ta = _mm(ya, w["wba_t"], "nt", BF16, 1024, 1024, 1024, f"branch_a_{tag}")
    tb = _mm(yb, w["wbb_t"], "nt", BF16, 1024, 1024, 1024, f"branch_b_{tag}")
    tc = _mm(yc, w["wbc_t"], "nt", BF16, 1024, 1024, 1024, f"branch_c_{tag}")
    merged = _merge_fwd(ta, tb, tc, proj, f"merge_fwd_{tag}")
    x_new = _mm(merged, w["wout"], "nn", F32, 1024, 512, D_MODEL, f"out_proj_{tag}", res=x)
    saved = dict(x=x, h=h, proj=proj, s5_raw=s5_raw, s5_ops=s5_ops, ya0=ya0, ya=ya, yb=yb, yc=yc, o_att=o_att,
                 ta=ta, tb=tb, tc=tc, merged=merged)
    return x_new, saved


def _layer_bwd(dx_out, p, w, tabs, s, tag, first_after=None, after_merge=None, before_win=None, after_win=None):
    L = dx_out.shape[0]
    proj = s["proj"]
    big, small = {}, {}
    dmerged = _mm(dx_out, w["wout"], "nt", BF16, 1024, 512, D_MODEL, f"d_merged_{tag}", after=first_after)
    big["wout"] = _mm(s["merged"], dx_out, "tn", GRAD_DT, 512, 1024, L, f"d_wout_{tag}")
    dta, dtb, dtc, dga, dgb, dgc = _merge_bwd(s["ta"], s["tb"], s["tc"], proj, dmerged, f"merge_bwd_{tag}")
    tok = after_merge(dga) if after_merge is not None else None
    dy = {}
    for br, dt in (("a", dta), ("b", dtb), ("c", dtc)):
        dy[br] = _mm(dt, w[f"wb{br}_t"], "nn", BF16, 1024, 1024, D_MODEL, f"d_y{br}_{tag}", after=tok)
        big[f"wb{br}_t"] = _mm(dt, s[f"y{br}"], "tn", GRAD_DT, 512, 1024, L, f"d_wb{br}_{tag}")

    dq, dzc, dkvc, dkvp, dsink = _attn_bwd(proj, p["attn_sinks"], tabs, s["o_att"], dy["c"], f"attn_bwd_{tag}")
    dkv = dkvc + jnp.concatenate([dkvp[CHUNK:], jnp.zeros((CHUNK, 256), F32)], axis=0)
    small["attn_sinks"] = dsink[:, 0]

    dub, dvb, dzb, dlw, dlb, dsgw, dsgb = _sg_bwd(
        proj, p["sg_ln_w"][None], p["sg_ln_b"][None], p["sg_w"], p["sg_b"][:, :, None], dy["b"], f"sg_bwd_{tag}")
    small.update(sg_ln_w=dlw[0], sg_ln_b=dlb[0], sg_w=dsgw, sg_b=dsgb[:, :, 0])

    dya0, dza, dglu, dglub = _glu_bwd(s["ya0"], proj, w["glu"], p["ssm_glu_b"][None], dy["a"], f"glu_bwd_{tag}")
    big["glu"] = dglu.astype(GRAD_DT)
    small["ssm_glu_b"] = dglub[0]

    dua, dbre, dbim, dcre, dcim, dlr, dli, dd = _s5_bwd(proj, dya0, *s["s5_ops"], name=f"s5_bwd_{tag}")
    da_re, da_im, dlog_dt, dbt_re, dbt_im = _s5_params_bwd(
        *s["s5_raw"], dlr.reshape(SSM_GROUPS, SSM_STATE), dli.reshape(SSM_GROUPS, SSM_STATE),
        _unslab_b(dbre), _unslab_b(dbim), name=f"s5_params_bwd_{tag}")
    small.update(ssm_a_re=da_re, ssm_a_im=da_im, ssm_log_dt=dlog_dt[:, 0],
                 ssm_bt_re=dbt_re, ssm_bt_im=dbt_im,
                 ssm_c_re=_unslab_c(dcre), ssm_c_im=_unslab_c(dcim), ssm_d=dd.reshape(SSM_WIDTH))

    dproj = _concat_cols([dua, dza, dub, dvb, dzb, dq, dkv, dzc, dga, dgb, dgc], f"d_proj_{tag}")
    tok = before_win(big) if before_win is not None else None
    big["win_t"] = _mm(dproj, s["h"], "tn", GRAD_DT, 256, D_MODEL, L, f"d_win_{tag}", after=tok)
    tok = after_win(big) if after_win is not None else None
    dh = _mm(dproj, w["win_t"], "nn", F32, L, D_MODEL, 256, f"d_h_{tag}", after=tok)
    dx_in, dnw = _rms_bwd(s["x"], p["norm_w"][None], dh, dx_out, f"rms_bwd_{tag}")
    small["norm_w"] = dnw[0]
    return dx_in, big, small


def _local_step(x, tgt, small_p, final_w, big_w):
    L = x.shape[0]
    tabs = _rope_tables(L)
    saved = []
    for l in range(DEPTH):
        x, s = _layer_fwd(x, small_p[l], big_w[l], tabs, f"l{l}")
        saved.append(s)
    loss_acc, dx, dfw = _final(x, final_w[None], tgt, "final_norm_loss")
    big_g, small_g = [None] * DEPTH, [None] * DEPTH
    for l in reversed(range(DEPTH)):
        dx, big_g[l], small_g[l] = _layer_bwd(dx, small_p[l], big_w[l], tabs, saved[l], f"l{l}")
    return loss_acc[0, 0], dx, dfw[0], big_g, small_g


MESH = pl.DeviceIdType.MESH
ANY = pl.BlockSpec(memory_space=pl.ANY)
ROW_ALIGN = 16


def _place():
    return lax.axis_index("x"), lax.axis_index("y"), lax.axis_index("c")


HBM = pl.BlockSpec(memory_space=pltpu.HBM)
SEM = pl.BlockSpec(memory_space=pltpu.SEMAPHORE)
EFFECT = pltpu.SideEffectType.DATAFLOW_SIDE_EFFECTING


def _split_start(srcs, lands, n_copies, copies, name, after=None):
    n, m, k = len(srcs), len(lands), n_copies
    extra = [] if after is None else [after]

    def body(*refs):
        src_refs, land_refs = refs[:n], refs[n:n + m]
        sems = refs[n + m + len(extra):]
        send_sems, recv_sems, token = sems[:k], sems[k:2 * k], refs[-1]
        for cp in copies(src_refs, land_refs, send_sems, recv_sems):
            cp.start()
        token[...] = jnp.zeros_like(token)

    ops = list(srcs) + list(lands)
    outs = pl.pallas_call(
        body, name=name,
        out_shape=(*[pltpu.SemaphoreType.DMA(())] * (2 * k),
                   *[pltpu.HBM(a.shape, a.dtype) for a in ops], jax.ShapeDtypeStruct((8, 128), F32)),
        in_specs=[HBM] * (n + m) + [ANY] * len(extra),
        out_specs=(*[SEM] * (2 * k), *[HBM] * (n + m), pl.BlockSpec(memory_space=pltpu.VMEM)),
        input_output_aliases={i: 2 * k + i for i in range(n + m)},
        compiler_params=pltpu.CompilerParams(has_side_effects=EFFECT),
    )(*[pltpu.with_memory_space_constraint(a, pltpu.HBM) for a in ops], *extra)
    return (list(outs[:k]), list(outs[k:2 * k]), list(outs[2 * k:2 * k + n]), list(outs[2 * k + n:2 * k + n + m]),
            outs[-1])


def _split_wait(send_sems, recv_sems, srcs, lands, after, copies, name):
    n, m, k = len(srcs), len(lands), len(send_sems)
    after = list(after) if isinstance(after, (list, tuple)) else [after]

    def body(*refs):
        src_refs, land_refs = refs[:n], refs[n:n + m]
        for cp in copies(src_refs, land_refs, refs[n + m:n + m + k], refs[n + m + k:n + m + 2 * k]):
            cp.wait_send()
            cp.wait_recv()

    ops = list(srcs) + list(lands)
    outs = pl.pallas_call(
        body, name=name,
        out_shape=tuple(pltpu.HBM(a.shape, a.dtype) for a in ops),
        in_specs=[HBM] * (n + m) + [SEM] * (2 * k) + [ANY] * len(after),
        out_specs=tuple([HBM] * (n + m)),
        input_output_aliases={i: i for i in range(n + m)},
        compiler_params=pltpu.CompilerParams(has_side_effects=EFFECT),
    )(*ops, *send_sems, *recv_sems, *after)
    return list(outs[:n]), list(outs[n:])


def _ag_rows(land_ref, px, py, pc):
    r = land_ref.shape[0] // N_DEV
    start = pl.multiple_of((4 * px + 2 * py + pc) * r, ROW_ALIGN)
    return land_ref.at[pl.ds(start, r), :]


def _ag_copies_to(which):
    def copies(src_refs, land_refs, send_sems, recv_sems):
        x, y, c = _place()
        peers = [(x, y, 1 - c), (1 - x, y, c), (x, 1 - y, c), (1 - x, 1 - y, c)]
        return [pltpu.make_async_remote_copy(
            src_ref=_ag_rows(land_refs[a], x, y, c), dst_ref=_ag_rows(land_refs[a], x, y, c),
            send_sem=send_sems[len(which) * a + k], recv_sem=recv_sems[len(which) * a + k],
            device_id=peers[p], device_id_type=MESH)
            for a in range(len(land_refs)) for k, p in enumerate(which)]
    return copies


_ag_copies = _ag_copies_to((0, 1, 2, 3))
_ag_copies_near = _ag_copies_to((0, 1, 2))
_ag_copies_far = _ag_copies_to((3,))


def _ag_forward(lands, name, which=(0, 1, 2)):
    n = len(lands)

    def body(*refs):
        land_refs = refs[n:2 * n]
        send_sems, recv_sems = refs[2 * n:]
        x, y, c = _place()
        chips = [(1 - x, y), (x, 1 - y), (1 - x, 1 - y)]

        def copy(a, k, pc):
            px, py = chips[which[k]]
            return pltpu.make_async_remote_copy(
                src_ref=_ag_rows(land_refs[a], px, py, pc), dst_ref=_ag_rows(land_refs[a], px, py, pc),
                send_sem=send_sems.at[a, k], recv_sem=recv_sems.at[a, k], device_id=(x, y, 1 - c), device_id_type=MESH)

        passed = [copy(a, k, c) for a in range(n) for k in range(len(which))]
        for cp in passed:
            cp.start()
        for a in range(n):
            for k in range(len(which)):
                copy(a, k, 1 - c).wait_recv()
        for cp in passed:
            cp.wait_send()

    sems = pltpu.SemaphoreType.DMA((n, len(which)))
    return pl.pallas_call(
        body, name=name,
        in_specs=[ANY] * n, out_specs=[ANY] * n,
        out_shape=[jax.ShapeDtypeStruct(l.shape, l.dtype) for l in lands],
        input_output_aliases={i: i for i in range(n)},
        scratch_shapes=[sems, sems],
    )(*lands)


def _allgather_place(shards):
    x, y, c = _place()
    return [lax.dynamic_update_slice(lax.empty((N_DEV * s.shape[0], s.shape[1]), s.dtype), s,
                                     ((4 * x + 2 * y + c) * s.shape[0], 0)) for s in shards]


def _allgather_place_rows(w, layer, name):
    _, r, cols = w.shape
    tr = _row_tile(r)
    nb = r // tr
    x, y, c = _place()
    me = (4 * x + 2 * y + c).astype(jnp.int32)[None]

    def body(me_ref, s_ref, o_ref):
        o_ref[...] = s_ref[...].astype(BF16)

    return pl.pallas_call(
        body, name=name,
        grid_spec=pltpu.PrefetchScalarGridSpec(
            num_scalar_prefetch=1, grid=(nb,),
            in_specs=[pl.BlockSpec((None, tr, cols), lambda i, me_ref: (layer, i, 0))],
            out_specs=pl.BlockSpec((tr, cols), lambda i, me_ref: (me_ref[0] * nb + i, 0))),
        out_shape=jax.ShapeDtypeStruct((N_DEV * r, cols), BF16),
        compiler_params=_cp(("parallel",)),
    )(me, w)


def _allgather_start(lands, name, after=None):
    return _split_start([], lands, 4 * len(lands), _ag_copies, name + "_start", after=after)


def _allgather_finish(started, after, name):
    send_sems, recv_sems, _, lands, _ = started
    _, lands = _split_wait(send_sems, recv_sems, [], lands, after, _ag_copies, name + "_wait")
    return list(_ag_forward(lands, name + "_forward"))


def _rs_swap_cores(grads, name):
    n = len(grads)

    def body(*refs):
        ins, outs = refs[:n], refs[n:2 * n]
        send_sems, recv_sems = refs[2 * n:]
        x, y, c = _place()
        cps = []
        for a in range(n):
            r = ins[a].shape[0] // N_DEV
            for q in range(4):
                start = pl.multiple_of((2 * q + 1 - c) * r, ROW_ALIGN)
                cps.append(pltpu.make_async_remote_copy(
                    src_ref=ins[a].at[pl.ds(start, r), :], dst_ref=outs[a].at[q],
                    send_sem=send_sems.at[a, q], recv_sem=recv_sems.at[a, q],
                    device_id=(x, y, 1 - c), device_id_type=MESH))
        for cp in cps:
            cp.start()
        for cp in cps:
            cp.wait()

    return pl.pallas_call(
        body, name=name, in_specs=[ANY] * n, out_specs=[ANY] * n,
        out_shape=[jax.ShapeDtypeStruct((4, g.shape[0] // N_DEV, g.shape[1]), g.dtype) for g in grads],
        scratch_shapes=[pltpu.SemaphoreType.DMA((n, 4)), pltpu.SemaphoreType.DMA((n, 4))],
    )(*grads)


def _rs_chip_copies(sum_refs, land_refs, send_sems, recv_sems):
    x, y, c = _place()
    chips = [(1 - x, y), (x, 1 - y), (1 - x, 1 - y)]
    return [pltpu.make_async_remote_copy(
        src_ref=sum_refs[a].at[2 * px + py], dst_ref=land_refs[a].at[2 * x + y],
        send_sem=send_sems[3 * a + j], recv_sem=recv_sems[3 * a + j], device_id=(px, py, c), device_id_type=MESH)
        for a in range(len(sum_refs)) for j, (px, py) in enumerate(chips)]


def _row_tile(r):
    return max(t for t in range(ROW_ALIGN, min(r, 1024) + 1, ROW_ALIGN) if r % t == 0)


def _rs_add_cores(grad, recv, cidx, name):
    r, cols = recv.shape[1], recv.shape[2]
    tr = _row_tile(r)
    nb = r // tr

    def body(c_ref, g_ref, r_ref, o_ref):
        o_ref[...] = (g_ref[...].astype(F32) + r_ref[...].astype(F32)).astype(o_ref.dtype)

    return pl.pallas_call(
        body, name=name,
        grid_spec=pltpu.PrefetchScalarGridSpec(
            num_scalar_prefetch=1, grid=(4, nb),
            in_specs=[pl.BlockSpec((tr, cols), lambda q, i, c_ref: ((2 * q + c_ref[0]) * nb + i, 0)),
                      pl.BlockSpec((None, tr, cols), lambda q, i, c_ref: (q, i, 0))],
            out_specs=pl.BlockSpec((None, tr, cols), lambda q, i, c_ref: (q, i, 0))),
        out_shape=jax.ShapeDtypeStruct(recv.shape, recv.dtype),
        compiler_params=_cp(("parallel", "parallel")),
    )(cidx, grad, recv)


def _rs_add_chips(own, recv, slots, name):
    r, cols = recv.shape[1], recv.shape[2]
    tr = _row_tile(r)

    def body(s_ref, o_ref, r0_ref, r1_ref, r2_ref, out_ref):
        acc = o_ref[...].astype(F32)
        for ref in (r0_ref, r1_ref, r2_ref):
            acc = acc + ref[...].astype(F32)
        out_ref[...] = acc

    pick = lambda k: pl.BlockSpec((None, tr, cols), functools.partial(lambda i, s_ref, k: (s_ref[k], i, 0), k=k))
    return pl.pallas_call(
        body, name=name,
        grid_spec=pltpu.PrefetchScalarGridSpec(
            num_scalar_prefetch=1, grid=(r // tr,),
            in_specs=[pick(0), pick(1), pick(2), pick(3)],
            out_specs=pl.BlockSpec((tr, cols), lambda i, s_ref: (i, 0))),
        out_shape=jax.ShapeDtypeStruct((r, cols), F32),
        compiler_params=_cp(("parallel",)),
    )(slots, own, recv, recv, recv)


def _rs_core_copies(grad_refs, land_refs, send_sems, recv_sems):
    x, y, c = _place()
    cps = []
    for a in range(len(grad_refs)):
        r = grad_refs[a].shape[0] // N_DEV
        for q in range(4):
            start = pl.multiple_of((2 * q + 1 - c) * r, ROW_ALIGN)
            cps.append(pltpu.make_async_remote_copy(
                src_ref=grad_refs[a].at[pl.ds(start, r), :], dst_ref=land_refs[a].at[q],
                send_sem=send_sems[4 * a + q], recv_sem=recv_sems[4 * a + q],
                device_id=(x, y, 1 - c), device_id_type=MESH))
    return cps


def _reduce_scatter_chips_start(grads, recv, tag):
    cidx = lax.axis_index("c").astype(jnp.int32)[None]
    sums = [_rs_add_cores(g, rv, cidx, f"rs_add_cores_{tag}_{i}") for i, (g, rv) in enumerate(zip(grads, recv))]
    lands = [lax.empty(s.shape, s.dtype) for s in sums]
    return _split_start(sums, lands, 3 * len(sums), _rs_chip_copies, f"rs_chips_{tag}_start")


def _reduce_scatter_start(grads, tag):
    return _reduce_scatter_chips_start(grads, _rs_swap_cores(grads, f"rs_swap_cores_{tag}"), tag)


def _reduce_scatter_cores_start(grads, tag):
    lands = [lax.empty((4, g.shape[0] // N_DEV, g.shape[1]), g.dtype) for g in grads]
    return _split_start(grads, lands, 4 * len(grads), _rs_core_copies, f"rs_cores_{tag}_start")


def _reduce_scatter_cores_finish(started, after, tag):
    send_sems, recv_sems, grads, lands, _ = started
    grads, recv = _split_wait(send_sems, recv_sems, grads, lands, after, _rs_core_copies, f"rs_cores_{tag}_wait")
    return _reduce_scatter_chips_start(grads, recv, tag)


def _reduce_scatter_finish(started, after, tag):
    send_sems, recv_sems, sums, lands, _ = started
    sums, lands = _split_wait(send_sems, recv_sems, sums, lands, after, _rs_chip_copies, f"rs_chips_{tag}_wait")
    x, y = lax.axis_index("x"), lax.axis_index("y")
    slots = jnp.stack([2 * x + y, 2 * (1 - x) + y, 2 * x + 1 - y, 2 * (1 - x) + 1 - y]).astype(jnp.int32)
    return [(s, l, slots) for s, l in zip(sums, lands)]


def _ar_peers():
    x, y, c = _place()
    return [(1 - x if k & 4 else x, 1 - y if k & 2 else y, 1 - c if k & 1 else c) for k in range(1, N_DEV)]


def _ar_scatter_copies(src_refs, land_refs, send_sems, recv_sems):
    x, y, c = _place()
    cps = []
    for a in range(len(src_refs)):
        rs = src_refs[a].shape[0] // N_DEV
        for k, (px, py, pc) in enumerate(_ar_peers()):
            start = pl.multiple_of((4 * px + 2 * py + pc) * rs, 8)
            cps.append(pltpu.make_async_remote_copy(
                src_ref=src_refs[a].at[pl.ds(start, rs), :], dst_ref=land_refs[a].at[4 * x + 2 * y + c],
                send_sem=send_sems[7 * a + k], recv_sem=recv_sems[7 * a + k],
                device_id=(px, py, pc), device_id_type=MESH))
    return cps


def _ar_gather_copies(src_refs, land_refs, send_sems, recv_sems):
    x, y, c = _place()
    cps = []
    for a in range(len(land_refs)):
        rs = land_refs[a].shape[0] // N_DEV
        mine = land_refs[a].at[pl.ds(pl.multiple_of((4 * x + 2 * y + c) * rs, 8), rs), :]
        for k, peer in enumerate(_ar_peers()):
            cps.append(pltpu.make_async_remote_copy(
                src_ref=mine, dst_ref=mine, send_sem=send_sems[7 * a + k], recv_sem=recv_sems[7 * a + k],
                device_id=peer, device_id_type=MESH))
    return cps


def _allreduce_start(packs, name, after=None):
    assert all(p.shape[0] % (8 * N_DEV) == 0 for p in packs)
    x, y, c = _place()
    me = 4 * x + 2 * y + c
    lands = []
    for p in packs:
        rs = p.shape[0] // N_DEV
        own = lax.dynamic_slice(p, (me * rs, 0), (rs, p.shape[1]))
        lands.append(lax.dynamic_update_slice(lax.empty((N_DEV, rs, p.shape[1]), F32), own[None], (me, 0, 0)))
    return _split_start(packs, lands, 7 * len(packs), _ar_scatter_copies, name + "_scatter_start", after=after)


def _allreduce_middle(started, after, name):
    n = len(started[2])
    _, parts = _split_wait(started[0], started[1], started[2], started[3], after, _ar_scatter_copies, name + "_scatter_wait")

    def body(*refs):
        for p_ref, o_ref in zip(refs[:n], refs[n:]):
            acc = p_ref[0]
            for d in range(1, N_DEV):
                acc = acc + p_ref[d]
            o_ref[...] = acc

    sums = pl.pallas_call(body, name=name + "_add",
                          out_shape=[jax.ShapeDtypeStruct(p.shape[1:], F32) for p in parts])(*parts)
    x, y, c = _place()
    me = 4 * x + 2 * y + c
    lands = [lax.dynamic_update_slice(lax.empty((N_DEV * s.shape[0], s.shape[1]), F32), s, (me * s.shape[0], 0))
             for s in sums]
    return _split_start([], lands, 7 * n, _ar_gather_copies, name + "_gather_start")


def _allreduce_finish(started, after, name):
    _, lands = _split_wait(started[0], started[1], [], started[3], after, _ar_gather_copies, name + "_gather_wait")
    return lands


ADAM_TILE_BYTES = 2 * 1024 * 1024


def _adam_tiles(rows, cols, align=8):
    if rows % align:
        return rows, cols
    best = None
    for tc in {cols, cols // 2, cols // 4}:
        if tc != cols and (tc % 128 or cols % tc):
            continue
        fits = [t for t in range(align, rows + 1, align) if rows % t == 0 and t * max(tc, 128) * 4 <= ADAM_TILE_BYTES]
        if fits and (best is None or max(fits) * tc > best[0] * best[1]):
            best = (max(fits), tc)
    return best


def _adam_math(w, g, m, v):
    nm = ADAM_B1 * m + (1.0 - ADAM_B1) * g
    nv = ADAM_B2 * v + (1.0 - ADAM_B2) * jnp.square(g)
    c1 = 1.0 - ADAM_B1 ** ADAM_STEP
    c2 = 1.0 - ADAM_B2 ** ADAM_STEP
    return -ADAM_LR * ((nm / c1) / (jnp.sqrt(nv / c2) + ADAM_EPS) + ADAM_WD * w), nm, nv


def _adamw_layer(w, g, m, v, layer, carry, name):
    _, rows, cols = w.shape
    tr, tc = _adam_tiles(rows, cols)

    def body(w_ref, g_ref, m_ref, v_ref, *rest):
        go_ref, d_ref, nm_ref, nv_ref = rest[-4:]
        gv = g_ref[...]
        go_ref[...] = gv
        d_ref[...], nm_ref[...], nv_ref[...] = _adam_math(w_ref[...], gv, m_ref[...], v_ref[...])

    blk = pl.BlockSpec((None, tr, tc), lambda i, j: (layer, i, j))
    flat = pl.BlockSpec((tr, tc), lambda i, j: (i, j))
    sh = jax.ShapeDtypeStruct(w.shape, F32)
    carry = [] if carry is None else list(carry)
    return pl.pallas_call(
        body, name=name, grid=(rows // tr, cols // tc),
        in_specs=[blk, flat, blk, blk] + [ANY] * len(carry), out_specs=[blk] * 4, out_shape=[sh] * 4,
        input_output_aliases={4 + k: k for k in range(len(carry))},
        compiler_params=_cp(("parallel", "parallel")),
    )(w, g, m, v, *carry)


def _adamw_layer_sum(w, own, recv, slots, m, v, layer, carry, name):
    _, rows, cols = w.shape
    tr, tc = _adam_tiles(rows, cols, align=ROW_ALIGN)

    def body(s_ref, w_ref, o_ref, r0_ref, r1_ref, r2_ref, m_ref, v_ref, *rest):
        go_ref, d_ref, nm_ref, nv_ref = rest[-4:]
        gv = o_ref[...].astype(F32)
        for ref in (r0_ref, r1_ref, r2_ref):
            gv = gv + ref[...].astype(F32)
        go_ref[...] = gv
        d_ref[...], nm_ref[...], nv_ref[...] = _adam_math(w_ref[...], gv, m_ref[...], v_ref[...])

    blk = pl.BlockSpec((None, tr, tc), lambda i, j, s: (layer, i, j))
    pick = lambda k: pl.BlockSpec((None, tr, tc), functools.partial(lambda i, j, s, k: (s[k], i, j), k=k))
    sh = jax.ShapeDtypeStruct(w.shape, F32)
    carry = [] if carry is None else list(carry)
    return pl.pallas_call(
        body, name=name,
        grid_spec=pltpu.PrefetchScalarGridSpec(
            num_scalar_prefetch=1, grid=(rows // tr, cols // tc),
            in_specs=[blk, pick(0), pick(1), pick(2), pick(3), blk, blk] + [ANY] * len(carry),
            out_specs=[blk] * 4),
        out_shape=[sh] * 4,
        input_output_aliases={8 + k: k for k in range(len(carry))},
        compiler_params=_cp(("parallel", "parallel")),
    )(slots, w, own, recv, recv, recv, m, v, *carry)


def _adamw(w, g, m, v, name):
    shape = w.shape
    rows, cols = shape[-2:]
    lead = shape[:-2]
    nl = math.prod(lead)
    tr, tc = _adam_tiles(rows, cols)

    def body(w_ref, g_ref, m_ref, v_ref, d_ref, nm_ref, nv_ref):
        d_ref[...], nm_ref[...], nv_ref[...] = _adam_math(w_ref[...], g_ref[...], m_ref[...], v_ref[...])

    def index(b, i, j):
        return (*jnp.unravel_index(b, lead), i, j) if lead else (i, j)

    blk = pl.BlockSpec((*[None] * len(lead), tr, tc), index)
    sh = jax.ShapeDtypeStruct(shape, F32)
    return pl.pallas_call(
        body, name=name, grid=(nl, rows // tr, cols // tc), in_specs=[blk] * 4, out_specs=[blk] * 3,
        out_shape=[sh] * 3, compiler_params=_cp(("parallel", "parallel", "parallel")),
    )(w, g, m, v)


WEIGHTS = ("norm_w", "w_in", "ssm_a_re", "ssm_a_im", "ssm_log_dt", "ssm_b_re", "ssm_b_im", "ssm_c_re", "ssm_c_im",
           "ssm_d", "ssm_glu_w", "ssm_glu_b", "sg_ln_w", "sg_ln_b", "sg_w", "sg_b", "attn_sinks",
           "w_branch_a", "w_branch_b", "w_branch_c", "w_out", "final_norm_w")
BIG = ("w_in", "ssm_glu_w", "w_branch_a", "w_branch_b", "w_branch_c", "w_out")
BIG_KEY = {"w_in": ("win_t", True), "ssm_glu_w": ("glu", False), "w_branch_a": ("wba_t", True),
           "w_branch_b": ("wbb_t", True), "w_branch_c": ("wbc_t", True), "w_out": ("wout", False)}
VIEWS = {"w_in": (1, 2), "ssm_b_re": (2, 3), "ssm_b_im": (2, 3)}
MATS = ("ssm_a_re", "ssm_a_im", "ssm_c_re", "ssm_c_im", "ssm_b_re", "ssm_b_im", "sg_w")
VEC_GROUPS = (("ssm_d", "ssm_glu_b", "sg_ln_w", "sg_ln_b"), ("norm_w", "final_norm_w", "sg_b"), ("ssm_log_dt", "attn_sinks"))
PACK_ROWS = 8 * N_DEV


def _view(n, a):
    return jnp.swapaxes(a, *VIEWS[n]) if n in VIEWS else a


def _vec_moves(pack_ref, refs, to_pack):
    d, gb, lw, lb, nw, fw, sb, ld, sk = refs
    full = (slice(None), slice(None))
    moves = [((slice(2 * i, 2 * i + 2), slice(None)), r, full) for i, r in enumerate((d, gb, lw, lb))]
    moves += [((slice(8, 10), slice(None)), nw, (slice(None), slice(0, 1024))),
              ((slice(10, 12), slice(None)), nw, (slice(None), slice(1024, 2048))),
              ((slice(12, 13), slice(None)), fw, (slice(None), slice(0, 1024))),
              ((slice(13, 14), slice(None)), fw, (slice(None), slice(1024, 2048))),
              ((slice(16, 32), slice(0, 128)), sb, full),
              ((slice(32, 34), slice(0, 64)), ld, full),
              ((slice(34, 36), slice(0, 16)), sk, full)]
    for where, ref, part in moves:
        if to_pack:
            pack_ref[where] = ref[part]
        else:
            ref[part] = pack_ref[where]


def _vec_shapes(arrs):
    d, gb, lw, lb, nw, fw, sb, ld, sk = arrs
    return [d, gb, lw, lb, nw, fw.reshape(1, -1), sb.reshape(-1, sb.shape[-1]), ld, sk]


def _vec_pack(arrs, name):
    def body(*refs):
        refs[-1][...] = jnp.zeros_like(refs[-1])
        _vec_moves(refs[-1], refs[:-1], True)

    return pl.pallas_call(body, name=name, out_shape=jax.ShapeDtypeStruct((PACK_ROWS, 1024), F32))(*_vec_shapes(arrs))


def _vec_unpack(pack, like, name):
    shaped = _vec_shapes(like)

    def body(pack_ref, *refs):
        _vec_moves(pack_ref, refs, False)

    outs = pl.pallas_call(body, name=name, out_shape=[jax.ShapeDtypeStruct(a.shape, F32) for a in shaped])(pack)
    return [o.reshape(a.shape) for o, a in zip(outs, like)]


def _pack(groups, cols, name):
    assert cols == 1024
    return _vec_pack([a for arrs in groups for a in arrs], name)


def _unpack(pack, groups, name):
    return _vec_unpack(pack, [a for arrs in groups for a in arrs], name)


def kernel(x, norm_w, w_in, ssm_a_re, ssm_a_im, ssm_log_dt, ssm_b_re, ssm_b_im, ssm_c_re, ssm_c_im, ssm_d, ssm_glu_w, ssm_glu_b, sg_ln_w, sg_ln_b, sg_w, sg_b, attn_sinks, w_branch_a, w_branch_b, w_branch_c, w_out, final_norm_w, loss_target, m_norm_w, m_w_in, m_ssm_a_re, m_ssm_a_im, m_ssm_log_dt, m_ssm_b_re, m_ssm_b_im, m_ssm_c_re, m_ssm_c_im, m_ssm_d, m_ssm_glu_w, m_ssm_glu_b, m_sg_ln_w, m_sg_ln_b, m_sg_w, m_sg_b, m_attn_sinks, m_w_branch_a, m_w_branch_b, m_w_branch_c, m_w_out, m_final_norm_w, v_norm_w, v_w_in, v_ssm_a_re, v_ssm_a_im, v_ssm_log_dt, v_ssm_b_re, v_ssm_b_im, v_ssm_c_re, v_ssm_c_im, v_ssm_d, v_ssm_glu_w, v_ssm_glu_b, v_sg_ln_w, v_sg_ln_b, v_sg_w, v_sg_b, v_attn_sinks, v_w_branch_a, v_w_branch_b, v_w_branch_c, v_w_out, v_final_norm_w):
    w = dict(zip(WEIGHTS, (norm_w, w_in, ssm_a_re, ssm_a_im, ssm_log_dt, ssm_b_re, ssm_b_im, ssm_c_re, ssm_c_im, ssm_d, ssm_glu_w, ssm_glu_b, sg_ln_w, sg_ln_b, sg_w, sg_b, attn_sinks, w_branch_a, w_branch_b, w_branch_c, w_out, final_norm_w)))
    m = dict(zip(WEIGHTS, (m_norm_w, m_w_in, m_ssm_a_re, m_ssm_a_im, m_ssm_log_dt, m_ssm_b_re, m_ssm_b_im, m_ssm_c_re, m_ssm_c_im, m_ssm_d, m_ssm_glu_w, m_ssm_glu_b, m_sg_ln_w, m_sg_ln_b, m_sg_w, m_sg_b, m_attn_sinks, m_w_branch_a, m_w_branch_b, m_w_branch_c, m_w_out, m_final_norm_w)))
    v = dict(zip(WEIGHTS, (v_norm_w, v_w_in, v_ssm_a_re, v_ssm_a_im, v_ssm_log_dt, v_ssm_b_re, v_ssm_b_im, v_ssm_c_re, v_ssm_c_im, v_ssm_d, v_ssm_glu_w, v_ssm_glu_b, v_sg_ln_w, v_sg_ln_b, v_sg_w, v_sg_b, v_attn_sinks, v_w_branch_a, v_w_branch_b, v_w_branch_c, v_w_out, v_final_norm_w)))

    keys = [BIG_KEY[n][0] for n in BIG]
    wv, mv, vv = ({n: _view(n, a) for n, a in d.items()} for d in (w, m, v))
    shards = [[(wv[n][l] if n in VIEWS else w[n][l].T if BIG_KEY[n][1] else w[n][l]).astype(BF16) for n in BIG]
              for l in range(DEPTH)]
    small_p = [{n: w[n][l] for n in SMALL} for l in range(DEPTH)]
    xv, tgt = x[0], loss_target[0]
    tabs = _rope_tables(xv.shape[0])

    lands = [[[_allgather_place_rows(wv["w_in"], l, f"place_win_l{l}")], _allgather_place(shards[l][1:])]
             for l in range(DEPTH)]
    s5 = [_s5_prep(small_p[l], f"l{l}") for l in range(DEPTH)]
    vec_packs = [_pack([[d[n] for n in names] for names in VEC_GROUPS], 1024, f"pack_vec_{tag}")
                 for tag, d in (("w", wv), ("m", mv), ("v", vv))]
    near = _split_start([], lands[0][0], 3, _ag_copies_near, "ag_l0_win_near_start")
    got = {}
    x_, y_ = lax.axis_index("x"), lax.axis_index("y")
    n_tiles = D_IN // PROJ_TN
    far_first = (D_IN // 4 // PROJ_TN) * (2 * (1 - x_) + (1 - y_))
    n_far = -(-D_IN // 4 // PROJ_TN)
    tile_ids = jnp.arange(n_tiles, dtype=jnp.int32)
    is_far = (tile_ids >= far_first) & (tile_ids < far_first + n_far)
    near_tiles = jnp.sort(jnp.where(is_far, n_tiles, tile_ids))[:n_tiles - n_far]
    far_tiles = (far_first + jnp.arange(n_far)).astype(jnp.int32)

    def proj_of0(h):
        early = [h, *lands[0][1], *lands[1][0], *lands[1][1], *s5[0][1], *s5[1][1], near_tiles, far_tiles]
        early += vec_packs
        _, land = _split_wait(near[0], near[1], [], near[3], early, _ag_copies_near, "ag_l0_win_near_wait")
        far = _split_start([], land, 1, _ag_copies_far, "ag_l0_win_far_start")
        land = _ag_forward(far[3], "ag_l0_win_near_forward", which=(0, 1))
        got["ag0b"] = _allgather_start(lands[0][1], "ag_l0_rest", after=land[0])
        got["near1"] = _split_start([], lands[1][0], 3, _ag_copies_near, "ag_l1_win_near_start", after=got["ag0b"][4])
        proj = _in_proj_tiles(h, land[0], near_tiles, None, "in_proj_l0_near", after=got["near1"][4])
        _, land = _split_wait(far[0], far[1], [], land, proj, _ag_copies_far, "ag_l0_win_far_wait")
        got["win0"] = _ag_forward(land, "ag_l0_win_far_forward", which=(2,))[0]
        return _in_proj_tiles(h, got["win0"], far_tiles, proj, "in_proj_l0_far")

    def after_proj0(proj):
        got["w0"] = dict(zip(keys, [got["win0"]] + _allgather_finish(got["ag0b"], proj, "ag_l0_rest")))
        return got["w0"]

    x1, saved0 = _layer_fwd(xv, small_p[0], None, tabs, "l0", s5=s5[0], proj_of=proj_of0, after_proj=after_proj0)
    big_w0 = got["w0"]

    def proj_of1(h):
        near1 = got["near1"]
        _, land = _split_wait(near1[0], near1[1], [], near1[3], h, _ag_copies_near, "ag_l1_win_near_wait")
        far1 = _split_start([], land, 1, _ag_copies_far, "ag_l1_win_far_start")
        land = _ag_forward(far1[3], "ag_l1_win_near_forward", which=(0, 1))
        got["ag1b"] = _allgather_start(lands[1][1], "ag_l1_rest", after=land[0])
        proj = _in_proj_tiles(h, land[0], near_tiles, None, "in_proj_l1_near", after=got["ag1b"][4])
        _, land = _split_wait(far1[0], far1[1], [], land, proj, _ag_copies_far, "ag_l1_win_far_wait")
        got["win1"] = _ag_forward(land, "ag_l1_win_far_forward", which=(2,))[0]
        return _in_proj_tiles(h, got["win1"], far_tiles, proj, "in_proj_l1_far")

    def after_proj1(proj):
        got["w1"] = dict(zip(keys, [got["win1"]] + _allgather_finish(got["ag1b"], proj, "ag_l1_rest")))
        return got["w1"]

    x2, saved1 = _layer_fwd(x1, small_p[1], None, tabs, "l1", s5=s5[1], proj_of=proj_of1, after_proj=after_proj1)
    big_w1 = got["w1"]
    loss_acc, dx2, dfw = _final(x2, w["final_norm_w"][None], tgt, "final_norm_loss")
    loss = lax.psum(loss_acc[0, 0], ("x", "y", "c"))
    dfw = dfw[0]

    dx1, big_g1, small_g1 = _layer_bwd(dx2, small_p[1], big_w1, tabs, saved1, "l1")
    rs1_cores = _reduce_scatter_cores_start([big_g1[k] for k in keys], "l1")

    def after_merge0(x):
        got["rs1"] = _reduce_scatter_cores_finish(rs1_cores, x, "l1")
        return got["rs1"][4]

    def before_win0(big):
        got["rs0b"] = _reduce_scatter_start([big[k] for k in keys[1:]], "l0_rest")
        return got["rs0b"][4]

    def after_win0(big):
        got["rs0a"] = _reduce_scatter_start([big["win_t"]], "l0_win")
        return got["rs0a"][4]

    dx, big_g0, small_g0 = _layer_bwd(dx1, small_p[0], big_w0, tabs, saved0, "l0", first_after=rs1_cores[4],
                                      after_merge=after_merge0, before_win=before_win0, after_win=after_win0)
    rs1 = got["rs1"]
    small_g = [small_g0, small_g1]
    grads, delta, new_m, new_v = {}, {}, {}, {}

    def big_adam(red, layer, carry):
        outs = {}
        for i, n in enumerate(BIG):
            own, recv, slots = red[i]
            prev = None if carry is None else carry[n]
            if BIG_KEY[n][1] and n not in VIEWS:
                g = _rs_add_chips(own, recv, slots, f"rs_add_chips_{n}_l{layer}").T
                outs[n] = _adamw_layer(wv[n], g, mv[n], vv[n], layer, prev, f"adamw_{n}_l{layer}")
            else:
                outs[n] = _adamw_layer_sum(wv[n], own, recv, slots, mv[n], vv[n], layer, prev, f"adamw_{n}_l{layer}")
        return outs

    def small_grad(n):
        if n == "final_norm_w":
            return dfw
        if n in ("ssm_b_re", "ssm_b_im"):
            return jnp.stack([small_g[l][n.replace("ssm_b_", "ssm_bt_")].transpose(1, 0, 2) for l in range(DEPTH)])
        return jnp.stack([small_g[l][n] for l in range(DEPTH)])

    rows_of = lambda a: a.reshape(-1, a.shape[-1])
    g_mats = [rows_of(small_grad(n)) for n in MATS]
    g_vecs = [[small_grad(n) for n in names] for names in VEC_GROUPS]
    ar = _allreduce_start(g_mats + [_pack(g_vecs, 1024, "pack_vec_g")], "allreduce_small")
    big1 = big_adam(_reduce_scatter_finish(rs1, [dx, ar[4]], "l1"), 1, None)
    ar = _allreduce_middle(ar, [big1[n][1] for n in BIG], "allreduce_small")
    red0 = (_reduce_scatter_finish(got["rs0a"], ar[4], "l0_win")
            + _reduce_scatter_finish(got["rs0b"], ar[4], "l0_rest"))
    big0 = big_adam(red0, 0, big1)
    for n, outs in big0.items():
        grads[n], delta[n], new_m[n], new_v[n] = outs
    reduced = _allreduce_finish(ar, [big0[n][1] for n in BIG], "allreduce_small")
    for n, red in zip(MATS, reduced):
        outs = _adamw(rows_of(wv[n]), red, rows_of(mv[n]), rows_of(vv[n]), f"adamw_{n}")
        grads[n], delta[n], new_m[n], new_v[n] = (o.reshape(wv[n].shape) for o in (red, *outs))
    vec_names = [n for names in VEC_GROUPS for n in names]
    grads.update(zip(vec_names, _unpack(reduced[-1], g_vecs, "unpack_vec_g")))
    outs = _adamw(vec_packs[0], reduced[-1], vec_packs[1], vec_packs[2], "adamw_vec")
    for tag, res, o in zip("dmv", (delta, new_m, new_v), outs):
        res.update(zip(vec_names, _unpack(o, [[wv[n] for n in names] for names in VEC_GROUPS], f"unpack_vec_{tag}")))

    return (loss, dx[None], *[_view(n, d[n]) for d in (grads, delta, new_m, new_v) for n in WEIGHTS])
```
